```python
import math
import jax, jax.numpy as jnp
from jax import lax
import numpy as np

D_MODEL = 1024
BATCH = 8
SEQ = 2048
DEPTH = 2
DEC_BATCH = 32
DEC_SEQ = 16
PAST_LEN = 1024

CHUNK = 64
Q_BLOCK = 128
EPS = 1e-6
D_MIX = 1024
GLA_HEADS = 4
GLA_DV = 64
GLA_DK = 32
GLA_WIDTH = GLA_HEADS * GLA_DV
GLA_GATE_RANK = 16
GLA_GATE_TAU = 16.0
MLA_HEADS = 4
MLA_NOPE_DIM = 64
MLA_ROPE_DIM = 32
MLA_V_DIM = 128
MLA_Q_RANK = 192
MLA_KV_RANK = 128
MLA_WIDTH = MLA_HEADS * MLA_V_DIM
ROPE_BASE = 10000.0
S5_GROUPS = 16
S5_GROUP_CH = 16
S5_STATE = 64
S5_WIDTH = S5_GROUPS * S5_GROUP_CH
DT_MIN = 1e-3
DT_MAX = 1e-1
SPLIT_SIZES = (GLA_HEADS * GLA_DK, GLA_HEADS * GLA_DK, GLA_WIDTH, GLA_GATE_RANK, GLA_WIDTH,
               MLA_Q_RANK, MLA_KV_RANK, MLA_ROPE_DIM, MLA_WIDTH,
               S5_WIDTH, S5_WIDTH)
IN_COLS = sum(SPLIT_SIZES)

kernel_name = 'hybrid_gla_mla_s5_stream_step'


def rmsnorm(x, g):
    xf = x.astype(jnp.float32)
    y = xf * lax.rsqrt(jnp.mean(xf * xf, axis=-1, keepdims=True) + EPS)
    return (y * g.astype(jnp.float32)).astype(x.dtype)


def apply_rope(x, pos):
    half = x.shape[-1] // 2
    inv = ROPE_BASE ** (-jnp.arange(half, dtype=jnp.float32) / half)
    ang = pos.astype(jnp.float32)[:, None] * inv[None, :]
    ang = ang.reshape(ang.shape[:1] + (1,) * (x.ndim - 3) + ang.shape[1:])
    cos, sin = jnp.cos(ang), jnp.sin(ang)
    xf = x.astype(jnp.float32)
    x1, x2 = xf[..., :half], xf[..., half:]
    return jnp.concatenate([x1 * cos - x2 * sin, x1 * sin + x2 * cos], axis=-1).astype(x.dtype)


def gla_recurrence(q, k, v, log_a, s0):
    f32 = jnp.float32
    bsz, t = q.shape[:2]
    L = min(CHUNK, t)
    n = t // L

    def to_chunks(a):
        return jnp.moveaxis(a.astype(f32).reshape((bsz, n, L) + a.shape[2:]), 1, 0)

    causal = jnp.tril(jnp.ones((L, L), dtype=bool))[None, :, :, None, None]

    def step(S, inp):
        qc, kc, vc, gc = inp
        b = jnp.cumsum(gc, axis=1)
        diff = b[:, :, None] - b[:, None, :]
        decay = jnp.exp(jnp.where(causal, diff, -jnp.inf))
        scores = jnp.sum(qc[:, :, None] * kc[:, None] * decay, axis=-1)
        o = (jnp.einsum('bijh,bjhv->bihv', scores, vc)
             + jnp.einsum('bihk,bhkv->bihv', qc * jnp.exp(b), S))
        b_last = b[:, -1]
        S = (jnp.exp(b_last)[..., None] * S
             + jnp.einsum('bjhk,bjhv->bhkv', kc * jnp.exp(b_last[:, None] - b), vc))
        return S, o

    S, o = lax.scan(step, s0.astype(f32), (to_chunks(q), to_chunks(k), to_chunks(v), to_chunks(log_a)))
    o = jnp.moveaxis(o, 0, 1).reshape((bsz, t) + o.shape[3:])
    return o, S


def chunk_causal_attention(q_nope, q_pe, q_pos, k_nope, k_pe, v, k_pos):
    bsz, t, nh, _ = q_nope.shape
    qb = min(Q_BLOCK, t)
    nb = t // qb
    scale = (MLA_NOPE_DIM + MLA_ROPE_DIM) ** -0.5
    k_chunk = k_pos // CHUNK

    def blocks(a):
        return jnp.moveaxis(a.reshape((bsz, nb, qb) + a.shape[2:]), 1, 0)

    def one_block(args):
        qn, qp, qpos = args
        s = (jnp.einsum('bqhd,bshd->bhqs', qn, k_nope)
             + jnp.einsum('bqhr,bsr->bhqs', qp, k_pe)).astype(jnp.float32) * scale
        mask = k_chunk[None, :] <= (qpos // CHUNK)[:, None]
        s = jnp.where(mask[None, None], s, -jnp.inf)
        p = jax.nn.softmax(s, axis=-1).astype(v.dtype)
        return jnp.einsum('bhqs,bshv->bqhv', p, v)

    o = lax.map(one_block, (blocks(q_nope), blocks(q_pe), q_pos.reshape(nb, qb)))
    return jnp.moveaxis(o, 0, 1).reshape(bsz, t, nh, v.shape[-1])


def s5_scan(u, lam_re, lam_im, b_re, b_im, c_re, c_im, d, log_dt, x0_re, x0_im):
    f32 = jnp.float32
    bsz, t, _ = u.shape
    uf = u.astype(f32).reshape(bsz, t, S5_GROUPS, S5_GROUP_CH)
    lam = lax.complex(lam_re.astype(f32), lam_im.astype(f32))
    dt = jnp.exp(log_dt.astype(f32))[:, None]
    lam_bar = jnp.exp(lam * dt)
    b_bar = ((lam_bar - 1.0) / lam)[:, :, None] * lax.complex(b_re.astype(f32), b_im.astype(f32))
    bu = jnp.einsum('btgh,gph->btgp', uf.astype(jnp.complex64), b_bar)
    x0 = lax.complex(x0_re.astype(f32), x0_im.astype(f32))
    bu = bu.at[:, 0].add(lam_bar * x0)
    a = jnp.broadcast_to(lam_bar, bu.shape)

    def combine(e1, e2):
        a1, b1 = e1
        a2, b2 = e2
        return a1 * a2, a2 * b1 + b2

    _, xs = lax.associative_scan(combine, (a, bu), axis=1)
    c = lax.complex(c_re.astype(f32), c_im.astype(f32))
    y = jnp.real(jnp.einsum('btgp,ghp->btgh', xs, c)) + d.astype(f32).reshape(S5_GROUPS, S5_GROUP_CH) * uf
    x_last = xs[:, -1]
    return y.reshape(bsz, t, S5_WIDTH), jnp.real(x_last), jnp.imag(x_last)


def hybrid_layer(h, gla_s0, ckv_past, kpe_past, s5_x0_re, s5_x0_im,
                 w_in, gla_w_gate, gla_b_gate, gla_norm_gain,
                 mla_q_norm_gain, mla_w_uq, mla_kv_norm_gain, mla_w_ukv,
                 s5_lambda_re, s5_lambda_im, s5_b_re, s5_b_im, s5_c_re, s5_c_im,
                 s5_d, s5_log_dt, s5_w_glu, s5_b_glu, w_out):
    f32 = jnp.float32
    bsz, t, _ = h.shape
    past = ckv_past.shape[1]
    q_pos = past + jnp.arange(t)
    k_pos = jnp.arange(past + t)
    points = np.cumsum(SPLIT_SIZES)[:-1].tolist()
    proj = h @ w_in
    g_q, g_k, g_v, g_lr, g_z, m_cq, m_ckv, m_kr, m_z, s_u, s_z = jnp.split(proj, points, axis=-1)

    q = g_q.reshape(bsz, t, GLA_HEADS, GLA_DK) * (GLA_DK ** -0.5)
    k = g_k.reshape(bsz, t, GLA_HEADS, GLA_DK)
    v = g_v.reshape(bsz, t, GLA_HEADS, GLA_DV)
    gate_logit = (g_lr @ gla_w_gate + gla_b_gate).astype(f32)
    log_a = (jax.nn.log_sigmoid(gate_logit) / GLA_GATE_TAU).reshape(bsz, t, GLA_HEADS, GLA_DK)
    o_gla, gla_s = gla_recurrence(q, k, v, log_a, gla_s0)
    o_gla = rmsnorm(o_gla, gla_norm_gain).reshape(bsz, t, GLA_WIDTH).astype(h.dtype) * jax.nn.silu(g_z)

    c_q = rmsnorm(m_cq, mla_q_norm_gain)
    qh = (c_q @ mla_w_uq).reshape(bsz, t, MLA_HEADS, MLA_NOPE_DIM + MLA_ROPE_DIM)
    q_nope = qh[..., :MLA_NOPE_DIM]
    q_pe = apply_rope(qh[..., MLA_NOPE_DIM:], q_pos)
    ckv_new = rmsnorm(m_ckv, mla_kv_norm_gain)
    kpe_new = apply_rope(m_kr, q_pos)
    ckv_all = jnp.concatenate([ckv_past.astype(h.dtype), ckv_new], axis=1)
    kpe_all = jnp.concatenate([kpe_past.astype(h.dtype), kpe_new], axis=1)
    kv = (ckv_all @ mla_w_ukv).reshape(bsz, past + t, MLA_HEADS, MLA_NOPE_DIM + MLA_V_DIM)
    k_nope, v_mla = kv[..., :MLA_NOPE_DIM], kv[..., MLA_NOPE_DIM:]
    o_mla = chunk_causal_attention(q_nope, q_pe, q_pos, k_nope, kpe_all, v_mla, k_pos)
    o_mla = o_mla.reshape(bsz, t, MLA_WIDTH).astype(h.dtype) * jax.nn.silu(m_z)

    y5, s_re, s_im = s5_scan(s_u, s5_lambda_re, s5_lambda_im, s5_b_re, s5_b_im, s5_c_re, s5_c_im,
                             s5_d, s5_log_dt, s5_x0_re, s5_x0_im)
    g5 = jax.nn.gelu(y5)
    y5 = g5 * jax.nn.sigmoid(g5 @ s5_w_glu.astype(f32) + s5_b_glu.astype(f32))
    o_s5 = y5.astype(h.dtype) * jax.nn.silu(s_z)

    out = jnp.concatenate([o_gla, o_mla, o_s5], axis=-1) @ w_out
    return out, gla_s, ckv_new, kpe_new, s_re, s_im


def run_trunk(x, gla_state, ckv_cache, kpe_cache, s5_re, s5_im, ln_gain, final_gain, layer_weights):
    gla_o, ckv_o, kpe_o, re_o, im_o = [], [], [], [], []
    for l in range(DEPTH):
        lw = [w[l] for w in layer_weights]
        mix, g_s, c_n, k_n, r_s, i_s = hybrid_layer(rmsnorm(x, ln_gain[l]), gla_state[l], ckv_cache[l],
                                                    kpe_cache[l], s5_re[l], s5_im[l], *lw)
        x = x + mix.astype(x.dtype)
        gla_o.append(g_s)
        ckv_o.append(c_n)
        kpe_o.append(k_n)
        re_o.append(r_s)
        im_o.append(i_s)
    y = rmsnorm(x, final_gain)
    return y, jnp.stack(gla_o), jnp.stack(ckv_o), jnp.stack(kpe_o), jnp.stack(re_o), jnp.stack(im_o)


def setup_inputs(seed: int = 0) -> dict:
    key = jax.random.key(seed)
    ks = jax.random.split(key, 28)
    f32 = jnp.float32
    nrm = lambda k, shape, s=1.0: jax.random.normal(k, shape, f32) * s
    return {
        'x_prompt': nrm(ks[0], (BATCH, SEQ, D_MODEL)),
        'x_sample': nrm(ks[1], (DEC_BATCH, DEC_SEQ, D_MODEL)),
        'state_gla': nrm(ks[2], (DEPTH, DEC_BATCH, GLA_HEADS, GLA_DK, GLA_DV), 0.3),
        'cache_mla_ckv': nrm(ks[3], (DEPTH, DEC_BATCH, PAST_LEN, MLA_KV_RANK)),
        'cache_mla_kpe': nrm(ks[4], (DEPTH, DEC_BATCH, PAST_LEN, MLA_ROPE_DIM)),
        'state_s5_re': nrm(ks[5], (DEPTH, DEC_BATCH, S5_GROUPS, S5_STATE), 0.1),
        'state_s5_im': nrm(ks[6], (DEPTH, DEC_BATCH, S5_GROUPS, S5_STATE), 0.1),
        'ln_gain': 1.0 + nrm(ks[7], (DEPTH, D_MODEL), 0.02),
        'w_in': nrm(ks[8], (DEPTH, D_MODEL, IN_COLS), D_MODEL ** -0.5),
        'gla_w_gate': nrm(ks[9], (DEPTH, GLA_GATE_RANK, GLA_HEADS * GLA_DK), GLA_GATE_RANK ** -0.5),
        'gla_b_gate': nrm(ks[10], (DEPTH, GLA_HEADS * GLA_DK), 0.1),
        'gla_norm_gain': 1.0 + nrm(ks[11], (DEPTH, GLA_DV), 0.02),
        'mla_q_norm_gain': 1.0 + nrm(ks[12], (DEPTH, MLA_Q_RANK), 0.02),
        'mla_w_uq': nrm(ks[13], (DEPTH, MLA_Q_RANK, MLA_HEADS * (MLA_NOPE_DIM + MLA_ROPE_DIM)), MLA_Q_RANK ** -0.5),
        'mla_kv_norm_gain': 1.0 + nrm(ks[14], (DEPTH, MLA_KV_RANK), 0.02),
        'mla_w_ukv': nrm(ks[15], (DEPTH, MLA_KV_RANK, MLA_HEADS * (MLA_NOPE_DIM + MLA_V_DIM)), MLA_KV_RANK ** -0.5),
        's5_lambda_re': -0.5 + nrm(ks[16], (DEPTH, S5_GROUPS, S5_STATE), 0.01),
        's5_lambda_im': jnp.pi * jnp.arange(S5_STATE, dtype=f32)[None, None, :] + nrm(ks[17], (DEPTH, S5_GROUPS, S5_STATE), 0.01),
        's5_b_re': nrm(ks[18], (DEPTH, S5_GROUPS, S5_STATE, S5_GROUP_CH), (2 * S5_GROUP_CH) ** -0.5),
        's5_b_im': nrm(ks[19], (DEPTH, S5_GROUPS, S5_STATE, S5_GROUP_CH), (2 * S5_GROUP_CH) ** -0.5),
        's5_c_re': nrm(ks[20], (DEPTH, S5_GROUPS, S5_GROUP_CH, S5_STATE), (2 * S5_STATE) ** -0.5),
        's5_c_im': nrm(ks[21], (DEPTH, S5_GROUPS, S5_GROUP_CH, S5_STATE), (2 * S5_STATE) ** -0.5),
        's5_d': nrm(ks[22], (DEPTH, S5_WIDTH)),
        's5_log_dt': jax.random.uniform(ks[23], (DEPTH, S5_GROUPS), f32, math.log(DT_MIN), math.log(DT_MAX)),
        's5_w_glu': nrm(ks[24], (DEPTH, S5_WIDTH, S5_WIDTH), S5_WIDTH ** -0.5),
        's5_b_glu': nrm(ks[25], (DEPTH, S5_WIDTH), 0.02),
        'w_out': nrm(ks[26], (DEPTH, D_MIX, D_MODEL), D_MIX ** -0.5),
        'final_gain': 1.0 + nrm(ks[27], (D_MODEL,), 0.02),
    }


def reference(x_prompt, x_sample, state_gla, cache_mla_ckv, cache_mla_kpe, state_s5_re, state_s5_im,
              ln_gain, w_in, gla_w_gate, gla_b_gate, gla_norm_gain,
              mla_q_norm_gain, mla_w_uq, mla_kv_norm_gain, mla_w_ukv,
              s5_lambda_re, s5_lambda_im, s5_b_re, s5_b_im, s5_c_re, s5_c_im,
              s5_d, s5_log_dt, s5_w_glu, s5_b_glu, w_out, final_gain):
    f32 = jnp.float32
    layer_weights = (w_in, gla_w_gate, gla_b_gate, gla_norm_gain,
                     mla_q_norm_gain, mla_w_uq, mla_kv_norm_gain, mla_w_ukv,
                     s5_lambda_re, s5_lambda_im, s5_b_re, s5_b_im, s5_c_re, s5_c_im,
                     s5_d, s5_log_dt, s5_w_glu, s5_b_glu, w_out)
    bp = x_prompt.shape[0]
    gla0 = jnp.zeros((DEPTH, bp, GLA_HEADS, GLA_DK, GLA_DV), f32)
    ckv0 = jnp.zeros((DEPTH, bp, 0, MLA_KV_RANK), x_prompt.dtype)
    kpe0 = jnp.zeros((DEPTH, bp, 0, MLA_ROPE_DIM), x_prompt.dtype)
    s50 = jnp.zeros((DEPTH, bp, S5_GROUPS, S5_STATE), f32)
    y_prompt, gla_p, ckv_p, kpe_p, s5re_p, s5im_p = run_trunk(
        x_prompt, gla0, ckv0, kpe0, s50, s50, ln_gain, final_gain, layer_weights)
    y_sample, gla_s, ckv_s, kpe_s, s5re_s, s5im_s = run_trunk(
        x_sample, state_gla, cache_mla_ckv, cache_mla_kpe, state_s5_re, state_s5_im,
        ln_gain, final_gain, layer_weights)
    return (y_prompt, y_sample, gla_p, ckv_p, kpe_p, s5re_p, s5im_p, gla_s, ckv_s, kpe_s, s5re_s, s5im_s)
```

```python
import functools

import numpy as np
import jax
import jax.numpy as jnp
from jax import lax
from jax.experimental import pallas as pl
from jax.experimental.pallas import tpu as pltpu

F32 = jnp.float32
BF16 = jnp.bfloat16

D_MODEL = 1024
CHUNK = 64
EPS = 1e-6
GLA_HEADS = 4
GLA_DV = 64
GLA_DK = 32
GLA_WIDTH = GLA_HEADS * GLA_DV
GLA_QK = GLA_HEADS * GLA_DK
GLA_GATE_RANK = 16
GLA_GATE_TAU = 16.0
GLA_SUB = 16
MLA_HEADS = 4
MLA_NOPE_DIM = 64
MLA_ROPE_DIM = 32
MLA_V_DIM = 128
MLA_Q_RANK = 192
MLA_KV_RANK = 128
MLA_WIDTH = MLA_HEADS * MLA_V_DIM
MLA_QK_PAD = 128
ROPE_BASE = 10000.0
S5_GROUPS = 16
S5_GROUP_CH = 16
S5_STATE = 64
S5_WIDTH = S5_GROUPS * S5_GROUP_CH
S5_NSTATE = S5_GROUPS * S5_STATE
S5_BATCH_TILE = 8

PG_COLS = 896
PM_COLS = 1024
PS_COLS = 512

VMEM_LIMIT_BYTES = 48 * 1024 * 1024


def _cparams(*sem):
    return pltpu.CompilerParams(dimension_semantics=sem, vmem_limit_bytes=VMEM_LIMIT_BYTES)


def _sigmoid(x):
    return 1.0 / (1.0 + jnp.exp(-x))


def _dot(a, b):
    return jnp.dot(a, b, preferred_element_type=F32)


def _dot_t(a, b):
    return lax.dot_general(a, b, (((0,), (0,)), ((), ())), preferred_element_type=F32)


def _inproj_body(x_ref, g_ref, w_ref, og_ref, om_ref, os_ref):
    x = x_ref[...]
    ms = jnp.mean(x * x, axis=-1, keepdims=True)
    h = (x * lax.rsqrt(ms + EPS) * g_ref[...]).astype(BF16)
    og_ref[...] = _dot(h, w_ref[:, 0:PG_COLS]).astype(BF16)
    om_ref[...] = _dot(h, w_ref[:, PG_COLS:PG_COLS + PM_COLS]).astype(BF16)
    os_ref[...] = _dot(h, w_ref[:, PG_COLS + PM_COLS:]).astype(BF16)


def _inproj(x2, gain, w, tm):
    n = x2.shape[0]
    cols = PG_COLS + PM_COLS + PS_COLS
    return pl.pallas_call(
        _inproj_body,
        grid=(n // tm,),
        in_specs=[pl.BlockSpec((tm, D_MODEL), lambda i: (i, 0)),
                  pl.BlockSpec((1, D_MODEL), lambda i: (0, 0)),
                  pl.BlockSpec((D_MODEL, cols), lambda i: (0, 0))],
        out_specs=[pl.BlockSpec((tm, PG_COLS), lambda i: (i, 0)),
                   pl.BlockSpec((tm, PM_COLS), lambda i: (i, 0)),
                   pl.BlockSpec((tm, PS_COLS), lambda i: (i, 0))],
        out_shape=[jax.ShapeDtypeStruct((n, PG_COLS), BF16),
                   jax.ShapeDtypeStruct((n, PM_COLS), BF16),
                   jax.ShapeDtypeStruct((n, PS_COLS), BF16)],
        compiler_params=_cparams("parallel"),
        name="inproj",
    )(x2, gain, w)


def _gla_body(p_ref, wg_ref, bg_ref, gain_ref, tri_ref, ones_k_ref, ones_v_ref, ones_r_ref, bd_ref, s0_ref,
              o_ref, sout_ref, q_scr, k_scr, v_scr, b_scr, s_scr, *, chunk, nch):
    it = pl.program_id(1)
    sub = min(GLA_SUB, chunk)
    nsub = chunk // sub

    @pl.when(it == 0)
    def _():
        s_scr[...] = s0_ref[0]

    q_scr[...] = p_ref[:, 0:128].astype(F32) * (GLA_DK ** -0.5)
    k_scr[...] = p_ref[:, 128:256].astype(F32)
    v_scr[...] = p_ref[:, 256:512].astype(F32)
    logit = _dot(p_ref[:, 768:896], wg_ref[...]) + bg_ref[...]
    log_a = (jnp.minimum(logit, 0.0) - jnp.log(1.0 + jnp.exp(-jnp.abs(logit)))) * (1.0 / GLA_GATE_TAU)
    la_hi = log_a.astype(BF16)
    la_lo = (log_a - la_hi.astype(F32)).astype(BF16)
    b_scr[...] = _dot(tri_ref[...], la_hi) + _dot(tri_ref[...], la_lo)

    row_in_sub = lax.broadcasted_iota(jnp.int32, (sub, GLA_QK), 0)
    row16 = lax.broadcasted_iota(jnp.int32, (16, GLA_QK), 0)

    def one_chunk(c, carry):
        r0 = pl.multiple_of(c * chunk, chunk)
        qc = q_scr[pl.ds(r0, chunk), :]
        kc = k_scr[pl.ds(r0, chunk), :]
        vc = v_scr[pl.ds(r0, chunk), :]
        bc = b_scr[pl.ds(r0, chunk), :]
        s_prev = s_scr[...]
        o_inter = _dot((qc * jnp.exp(bc)).astype(BF16), s_prev.astype(BF16))
        rows = []
        for si in range(nsub):
            q_i = qc[si * sub:(si + 1) * sub]
            b_i = bc[si * sub:(si + 1) * sub]
            acc = o_inter[si * sub:(si + 1) * sub]
            for sj in range(si + 1):
                es = []
                for j in range(sub):
                    row = r0 + sj * sub + j
                    b_j = b_scr[pl.ds(row, 1), :]
                    k_j = k_scr[pl.ds(row, 1), :]
                    d = b_i - b_j
                    if si == sj:
                        valid = row_in_sub >= j
                        e = jnp.where(valid, q_i * k_j * jnp.exp(jnp.where(valid, d, 0.0)), 0.0)
                    else:
                        e = q_i * k_j * jnp.exp(d)
                    es.append(e.astype(BF16))
                e_all = jnp.concatenate(es, axis=0)
                p_all = _dot(e_all, ones_k_ref[...])
                for j in range(sub):
                    v_j = v_scr[pl.ds(r0 + sj * sub + j, 1), :]
                    acc = acc + p_all[j * sub:(j + 1) * sub] * v_j
            rows.append(acc)
        o = rows[0] if nsub == 1 else jnp.concatenate(rows, axis=0)
        ms = _dot((o * o).astype(BF16), ones_v_ref[...]) * (1.0 / GLA_DV)
        o_n = o * lax.rsqrt(ms + EPS) * gain_ref[...]
        z = p_ref[pl.ds(r0, chunk), 512:768].astype(F32)
        o_ref[pl.ds(r0, chunk), :] = (o_n * (z * _sigmoid(z))).astype(BF16)
        b_end = b_scr[pl.ds(r0 + chunk - 1, 1), :]
        k_end = (kc * jnp.exp(b_end - bc)).astype(BF16)
        a_new = _dot_t(k_end, vc.astype(BF16))
        d16 = jnp.where(row16 == 0, jnp.exp(b_end), 0.0)
        d_hi = d16.astype(BF16)
        d_lo = (d16 - d_hi.astype(F32)).astype(BF16)
        d_col = _dot_t(d_hi, ones_r_ref[...]) + _dot_t(d_lo, ones_r_ref[...])
        s_scr[...] = s_prev * d_col + a_new * bd_ref[...]
        return carry

    lax.fori_loop(0, nch, one_chunk, 0)

    @pl.when(it == pl.num_programs(1) - 1)
    def _():
        sout_ref[0] = s_scr[...]


def _gla_consts(tt, chunk):
    r = np.arange(tt)
    tri = ((r[None, :] <= r[:, None]) & (r[None, :] // chunk == r[:, None] // chunk)).astype(np.float32)
    hk = np.arange(GLA_QK) // GLA_DK
    hv = np.arange(GLA_WIDTH) // GLA_DV
    same_kv = (hk[:, None] == hv[None, :]).astype(np.float32)
    same_vv = (hv[:, None] == hv[None, :]).astype(np.float32)
    return (jnp.asarray(tri, BF16), jnp.asarray(same_kv, BF16), jnp.asarray(same_vv, BF16),
            jnp.ones((16, GLA_WIDTH), BF16), jnp.asarray(same_kv, F32))


def _gla(pg, wg, bg, gain, s0_bd, bsz, t):
    chunk = min(CHUNK, t)
    tt = min(512, t)
    nch = tt // chunk
    nt = t // tt
    tri, ones_k, ones_v, ones_r, bd = _gla_consts(tt, chunk)
    const = lambda shape: pl.BlockSpec(shape, lambda b, i: (0,) * len(shape))
    return pl.pallas_call(
        functools.partial(_gla_body, chunk=chunk, nch=nch),
        grid=(bsz, nt),
        in_specs=[pl.BlockSpec((tt, PG_COLS), lambda b, i: (b * nt + i, 0)),
                  const((GLA_QK, GLA_QK)), const((1, GLA_QK)), const((1, GLA_WIDTH)),
                  const((tt, tt)), const((GLA_QK, GLA_WIDTH)), const((GLA_WIDTH, GLA_WIDTH)),
                  const((16, GLA_WIDTH)), const((GLA_QK, GLA_WIDTH)),
                  pl.BlockSpec((1, GLA_QK, GLA_WIDTH), lambda b, i: (b, 0, 0))],
        out_specs=[pl.BlockSpec((tt, GLA_WIDTH), lambda b, i: (b * nt + i, 0)),
                   pl.BlockSpec((1, GLA_QK, GLA_WIDTH), lambda b, i: (b, 0, 0))],
        out_shape=[jax.ShapeDtypeStruct((bsz * t, GLA_WIDTH), BF16),
                   jax.ShapeDtypeStruct((bsz, GLA_QK, GLA_WIDTH), F32)],
        scratch_shapes=[pltpu.VMEM((tt, GLA_QK), F32), pltpu.VMEM((tt, GLA_QK), F32),
                        pltpu.VMEM((tt, GLA_WIDTH), F32), pltpu.VMEM((tt, GLA_QK), F32),
                        pltpu.VMEM((GLA_QK, GLA_WIDTH), F32)],
        compiler_params=_cparams("parallel", "arbitrary"),
        name="gla",
    )(pg, wg, bg, gain, tri, ones_k, ones_v, ones_r, bd, s0_bd)


def _rope128(x, cos_t, sin_t):
    lane = lax.broadcasted_iota(jnp.int32, x.shape, 1)
    first_half = (lane >= MLA_NOPE_DIM) & (lane < MLA_NOPE_DIM + MLA_ROPE_DIM // 2)
    rot = jnp.where(first_half, -pltpu.roll(x, 128 - MLA_ROPE_DIM // 2, 1), pltpu.roll(x, MLA_ROPE_DIM // 2, 1))
    return x * cos_t + rot * sin_t


def _mla_prepq_body(p_ref, cos_ref, sin_ref, gq_ref, wq_ref, gkv_ref, q_ref, ckv_ref, kpe_ref):
    cos_t = cos_ref[...]
    sin_t = sin_ref[...]
    cq = p_ref[:, 0:256].astype(F32)
    ms = jnp.sum(cq * cq, axis=-1, keepdims=True) * (1.0 / MLA_Q_RANK)
    cqn = (cq * lax.rsqrt(ms + EPS) * gq_ref[...]).astype(BF16)
    qh = _dot(cqn, wq_ref[...])
    scale = (MLA_NOPE_DIM + MLA_ROPE_DIM) ** -0.5
    for h in range(MLA_HEADS):
        x = qh[:, h * MLA_QK_PAD:(h + 1) * MLA_QK_PAD]
        q_ref[:, h * MLA_QK_PAD:(h + 1) * MLA_QK_PAD] = (_rope128(x, cos_t, sin_t) * scale).astype(BF16)
    ckv = p_ref[:, 256:384].astype(F32)
    ms = jnp.mean(ckv * ckv, axis=-1, keepdims=True)
    ckv_ref[...] = ckv * lax.rsqrt(ms + EPS) * gkv_ref[...]
    kr = p_ref[:, 384:512].astype(F32)
    kpe_ref[...] = _rope128(kr, cos_t, sin_t)


def _mla_prepq(pm, cos_t, sin_t, gq, wq, gkv, t, tm):
    n = pm.shape[0]
    ntab = t // tm
    const = lambda shape: pl.BlockSpec(shape, lambda i: (0,) * len(shape))
    return pl.pallas_call(
        _mla_prepq_body,
        grid=(n // tm,),
        in_specs=[pl.BlockSpec((tm, 512), lambda i: (i, 0)),
                  pl.BlockSpec((tm, 128), lambda i: (i % ntab, 0)),
                  pl.BlockSpec((tm, 128), lambda i: (i % ntab, 0)),
                  const((1, 256)), const((256, MLA_HEADS * MLA_QK_PAD)), const((1, MLA_KV_RANK))],
        out_specs=[pl.BlockSpec((tm, 512), lambda i: (i, 0)),
                   pl.BlockSpec((tm, 128), lambda i: (i, 0)),
                   pl.BlockSpec((tm, 128), lambda i: (i, 0))],
        out_shape=[jax.ShapeDtypeStruct((n, MLA_HEADS * MLA_QK_PAD), BF16),
                   jax.ShapeDtypeStruct((n, MLA_KV_RANK), F32),
                   jax.ShapeDtypeStruct((n, 128), F32)],
        compiler_params=_cparams("parallel"),
        name="mla_prepq",
    )(pm, cos_t, sin_t, gq, wq, gkv)


def _mla_prepkv_body(ckv_ref, kpe_ref, wkv_ref, k_ref, v_ref):
    kv = _dot(ckv_ref[...].astype(BF16), wkv_ref[...])
    kpe = kpe_ref[...]
    for h in range(MLA_HEADS):
        k_ref[:, h * MLA_QK_PAD:(h + 1) * MLA_QK_PAD] = (kv[:, h * MLA_QK_PAD:(h + 1) * MLA_QK_PAD] + kpe).astype(BF16)
    v_ref[...] = kv[:, MLA_HEADS * MLA_QK_PAD:].astype(BF16)


def _mla_prepkv(ckv, kpe128, wkv, tm):
    n = ckv.shape[0]
    return pl.pallas_call(
        _mla_prepkv_body,
        grid=(n // tm,),
        in_specs=[pl.BlockSpec((tm, MLA_KV_RANK), lambda i: (i, 0)),
                  pl.BlockSpec((tm, 128), lambda i: (i, 0)),
                  pl.BlockSpec((MLA_KV_RANK, 1024), lambda i: (0, 0))],
        out_specs=[pl.BlockSpec((tm, 512), lambda i: (i, 0)),
                   pl.BlockSpec((tm, 512), lambda i: (i, 0))],
        out_shape=[jax.ShapeDtypeStruct((n, MLA_HEADS * MLA_QK_PAD), BF16),
                   jax.ShapeDtypeStruct((n, MLA_WIDTH), BF16)],
        compiler_params=_cparams("parallel"),
        name="mla_prepkv",
    )(ckv, kpe128, wkv)


def _attn_body(q_ref, k_ref, v_ref, z_ref, o_ref, m_scr, l_scr, acc_scr, *, past, tq, tk):
    iq = pl.program_id(1)
    ik = pl.program_id(2)

    @pl.when(ik == 0)
    def _():
        m_scr[...] = jnp.full(m_scr.shape, -jnp.inf, F32)
        l_scr[...] = jnp.zeros(l_scr.shape, F32)
        acc_scr[...] = jnp.zeros(acc_scr.shape, F32)

    q_last_chunk = (past + (iq + 1) * tq - 1) // CHUNK

    @pl.when((ik * tk) // CHUNK <= q_last_chunk)
    def _():
        q_chunk = (past + iq * tq + lax.broadcasted_iota(jnp.int32, (tq, tk), 0)) // CHUNK
        k_chunk = (ik * tk + lax.broadcasted_iota(jnp.int32, (tq, tk), 1)) // CHUNK
        mask = k_chunk <= q_chunk
        for h in range(MLA_HEADS):
            sl = slice(h * MLA_QK_PAD, (h + 1) * MLA_QK_PAD)
            s = lax.dot_general(q_ref[:, sl], k_ref[:, sl], (((1,), (1,)), ((), ())),
                                preferred_element_type=F32)
            s = jnp.where(mask, s, -jnp.inf)
            m_prev = m_scr[h]
            m_new = jnp.maximum(m_prev, jnp.max(s, axis=-1, keepdims=True))
            alpha = jnp.exp(m_prev - m_new)
            p = jnp.exp(s - m_new)
            l_scr[h] = alpha * l_scr[h] + jnp.sum(p, axis=-1, keepdims=True)
            acc_scr[h] = alpha * acc_scr[h] + _dot(p.astype(BF16), v_ref[:, h * MLA_V_DIM:(h + 1) * MLA_V_DIM])
            m_scr[h] = m_new

    @pl.when(ik == pl.num_programs(2) - 1)
    def _():
        for h in range(MLA_HEADS):
            sl = slice(h * MLA_V_DIM, (h + 1) * MLA_V_DIM)
            z = z_ref[:, sl].astype(F32)
            o_ref[:, sl] = (acc_scr[h] / l_scr[h] * (z * _sigmoid(z))).astype(BF16)


def _attn(q, k, v, pm, bsz, t, s_len, past, tq, tk):
    nq = t // tq
    nk = s_len // tk

    def kv_map(b, iq, ik):
        last = jnp.minimum(((past + (iq + 1) * tq - 1) // CHUNK * CHUNK + CHUNK - 1) // tk, nk - 1)
        return (b * nk + jnp.minimum(ik, last), 0)

    return pl.pallas_call(
        functools.partial(_attn_body, past=past, tq=tq, tk=tk),
        grid=(bsz, nq, nk),
        in_specs=[pl.BlockSpec((tq, 512), lambda b, iq, ik: (b * nq + iq, 0)),
                  pl.BlockSpec((tk, 512), kv_map),
                  pl.BlockSpec((tk, 512), kv_map),
                  pl.BlockSpec((tq, 512), lambda b, iq, ik: (b * nq + iq, 1))],
        out_specs=pl.BlockSpec((tq, MLA_WIDTH), lambda b, iq, ik: (b * nq + iq, 0)),
        out_shape=jax.ShapeDtypeStruct((bsz * t, MLA_WIDTH), BF16),
        scratch_shapes=[pltpu.VMEM((MLA_HEADS, tq, 1), F32), pltpu.VMEM((MLA_HEADS, tq, 1), F32),
                        pltpu.VMEM((MLA_HEADS, tq, MLA_V_DIM), F32)],
        compiler_params=_cparams("parallel", "parallel", "arbitrary"),
        name="mla_attn",
    )(q, k, v, pm)


def _s5_body(p_ref, x0r_ref, x0i_ref, lre_ref, lim_ref, wb_ref, wc_ref, d_ref, wglu_ref, bglu_ref,
             o_ref, xr_out, xi_out, uz_bt, uz_tb, bu, o_tb, xr_s, xi_s, *, lc):
    it = pl.program_id(1)
    nb = S5_BATCH_TILE

    @pl.when(it == 0)
    def _():
        xr_s[...] = x0r_ref[...]
        xi_s[...] = x0i_ref[...]

    for b in range(nb):
        for c in range(PS_COLS // 128):
            uz_bt[c, b * lc:(b + 1) * lc, :] = p_ref[b, :, c * 128:(c + 1) * 128].astype(F32)
    for t in range(lc):
        for c in range(PS_COLS // 128):
            uz_tb[t * nb:(t + 1) * nb, c * 128:(c + 1) * 128] = uz_bt[c, pl.ds(t, nb, stride=lc), :]
    u = uz_tb[:, 0:S5_WIDTH]
    bu[...] = _dot(u.astype(BF16), wb_ref[...])

    lre = jnp.broadcast_to(lre_ref[...], (nb, S5_NSTATE))
    lim = jnp.broadcast_to(lim_ref[...], (nb, S5_NSTATE))

    def step(t, carry):
        xr, xi = carry
        r0 = pl.multiple_of(t * nb, nb)
        nr = lre * xr - lim * xi + bu[pl.ds(r0, nb), 0:S5_NSTATE]
        ni = lre * xi + lim * xr + bu[pl.ds(r0, nb), S5_NSTATE:2 * S5_NSTATE]
        bu[pl.ds(r0, nb), 0:S5_NSTATE] = nr
        bu[pl.ds(r0, nb), S5_NSTATE:2 * S5_NSTATE] = ni
        return nr, ni

    xr, xi = lax.fori_loop(0, lc, step, (xr_s[...], xi_s[...]))
    xr_s[...] = xr
    xi_s[...] = xi

    y = _dot(bu[...].astype(BF16), wc_ref[...]) + d_ref[...] * u
    g5 = 0.5 * y * (1.0 + jnp.tanh(0.7978845608028654 * (y + 0.044715 * (y * y * y))))
    gate = _sigmoid(_dot(g5.astype(BF16), wglu_ref[...]) + bglu_ref[...])
    z = uz_tb[:, S5_WIDTH:2 * S5_WIDTH]
    o = g5 * gate * (z * _sigmoid(z))
    for c in range(S5_WIDTH // 128):
        o_tb[c] = o[:, c * 128:(c + 1) * 128]
    for b in range(nb):
        for c in range(S5_WIDTH // 128):
            o_ref[b, :, c * 128:(c + 1) * 128] = o_tb[c, pl.ds(b, lc, stride=nb), :].astype(BF16)

    @pl.when(it == pl.num_programs(1) - 1)
    def _():
        xr_out[...] = xr
        xi_out[...] = xi


def _s5(ps3, x0r, x0i, lre, lim, wb, wc, d, wglu, bglu, bsz, t):
    nb = S5_BATCH_TILE
    lc = min(128, t)
    nt = t // lc
    const = lambda shape: pl.BlockSpec(shape, lambda g, i: (0,) * len(shape))
    return pl.pallas_call(
        functools.partial(_s5_body, lc=lc),
        grid=(bsz // nb, nt),
        in_specs=[pl.BlockSpec((nb, lc, PS_COLS), lambda g, i: (g, i, 0)),
                  pl.BlockSpec((nb, S5_NSTATE), lambda g, i: (g, 0)),
                  pl.BlockSpec((nb, S5_NSTATE), lambda g, i: (g, 0)),
                  const((1, S5_NSTATE)), const((1, S5_NSTATE)),
                  const((S5_WIDTH, 2 * S5_NSTATE)), const((2 * S5_NSTATE, S5_WIDTH)),
                  const((1, S5_WIDTH)), const((S5_WIDTH, S5_WIDTH)), const((1, S5_WIDTH))],
        out_specs=[pl.BlockSpec((nb, lc, S5_WIDTH), lambda g, i: (g, i, 0)),
                   pl.BlockSpec((nb, S5_NSTATE), lambda g, i: (g, 0)),
                   pl.BlockSpec((nb, S5_NSTATE), lambda g, i: (g, 0))],
        out_shape=[jax.ShapeDtypeStruct((bsz, t, S5_WIDTH), BF16),
                   jax.ShapeDtypeStruct((bsz, S5_NSTATE), F32),
                   jax.ShapeDtypeStruct((bsz, S5_NSTATE), F32)],
        scratch_shapes=[pltpu.VMEM((PS_COLS // 128, nb * lc, 128), F32), pltpu.VMEM((lc * nb, PS_COLS), F32),
                        pltpu.VMEM((lc * nb, 2 * S5_NSTATE), F32), pltpu.VMEM((S5_WIDTH // 128, lc * nb, 128), F32),
                        pltpu.VMEM((nb, S5_NSTATE), F32), pltpu.VMEM((nb, S5_NSTATE), F32)],
        compiler_params=_cparams("parallel", "arbitrary"),
        name="s5",
    )(ps3, x0r, x0i, lre, lim, wb, wc, d, wglu, bglu)


def _outproj_body(x_ref, og_ref, om_ref, os_ref, w_ref, g_ref, o_ref, *, final):
    acc = _dot(og_ref[...], w_ref[0:GLA_WIDTH, :])
    acc += _dot(om_ref[...], w_ref[GLA_WIDTH:GLA_WIDTH + MLA_WIDTH, :])
    acc += _dot(os_ref[...], w_ref[GLA_WIDTH + MLA_WIDTH:, :])
    xn = x_ref[...] + acc
    if final:
        ms = jnp.mean(xn * xn, axis=-1, keepdims=True)
        xn = xn * lax.rsqrt(ms + EPS) * g_ref[...]
    o_ref[...] = xn


def _outproj(x2, og, om, os_, w, gain, tm, final):
    n = x2.shape[0]
    row = lambda c: pl.BlockSpec((tm, c), lambda i: (i, 0))
    return pl.pallas_call(
        functools.partial(_outproj_body, final=final),
        grid=(n // tm,),
        in_specs=[row(D_MODEL), row(GLA_WIDTH), row(MLA_WIDTH), row(S5_WIDTH),
                  pl.BlockSpec((D_MODEL, D_MODEL), lambda i: (0, 0)),
                  pl.BlockSpec((1, D_MODEL), lambda i: (0, 0))],
        out_specs=row(D_MODEL),
        out_shape=jax.ShapeDtypeStruct((n, D_MODEL), F32),
        compiler_params=_cparams("parallel"),
        name="outproj_final" if final else "outproj",
    )(x2, og, om, os_, w, gain)


def _layer_params(l, ln_gain, w_in, gla_w_gate, gla_b_gate, gla_norm_gain, mla_q_norm_gain, mla_w_uq,
                  mla_kv_norm_gain, mla_w_ukv, s5_lambda_re, s5_lambda_im, s5_b_re, s5_b_im, s5_c_re, s5_c_im,
                  s5_d, s5_log_dt, s5_w_glu, s5_b_glu, w_out):
    w = w_in[l]
    zc = lambda n: jnp.zeros((D_MODEL, n), F32)
    off = np.cumsum((0, 128, 128, 256, 16, 256, 192, 128, 32, 512, 256, 256))
    seg = lambda i: w[:, off[i]:off[i + 1]]
    g_q, g_k, g_v, g_lr, g_z, m_cq, m_ckv, m_kr, m_z, s_u, s_z = [seg(i) for i in range(11)]
    w_r = jnp.concatenate([g_q, g_k, g_v, g_z, g_lr, zc(112),
                           m_cq, zc(64), m_ckv, zc(64), m_kr, zc(32), m_z,
                           s_u, s_z], axis=1).astype(BF16)
    wg = jnp.zeros((GLA_QK, GLA_QK), F32).at[:GLA_GATE_RANK].set(gla_w_gate[l]).astype(BF16)
    wq = mla_w_uq[l].reshape(MLA_Q_RANK, MLA_HEADS, MLA_NOPE_DIM + MLA_ROPE_DIM)
    wq = jnp.pad(wq, ((0, 256 - MLA_Q_RANK), (0, 0), (0, MLA_QK_PAD - MLA_NOPE_DIM - MLA_ROPE_DIM)))
    wq = wq.reshape(256, MLA_HEADS * MLA_QK_PAD).astype(BF16)
    gq = jnp.pad(mla_q_norm_gain[l], (0, 256 - MLA_Q_RANK)).reshape(1, 256)
    wkv = mla_w_ukv[l].reshape(MLA_KV_RANK, MLA_HEADS, MLA_NOPE_DIM + MLA_V_DIM)
    wk = jnp.pad(wkv[:, :, :MLA_NOPE_DIM], ((0, 0), (0, 0), (0, MLA_QK_PAD - MLA_NOPE_DIM)))
    wkv_r = jnp.concatenate([wk.reshape(MLA_KV_RANK, MLA_HEADS * MLA_QK_PAD),
                             wkv[:, :, MLA_NOPE_DIM:].reshape(MLA_KV_RANK, MLA_WIDTH)], axis=1).astype(BF16)
    lam = lax.complex(s5_lambda_re[l], s5_lambda_im[l])
    dt = jnp.exp(s5_log_dt[l])[:, None]
    lam_bar = jnp.exp(lam * dt)
    b_bar = ((lam_bar - 1.0) / lam)[:, :, None] * lax.complex(s5_b_re[l], s5_b_im[l])
    eye_g = jnp.eye(S5_GROUPS, dtype=F32)
    bd_in = lambda m: jnp.einsum('gph,gk->ghkp', m, eye_g).reshape(S5_WIDTH, S5_NSTATE)
    wb = jnp.concatenate([bd_in(jnp.real(b_bar)), bd_in(jnp.imag(b_bar))], axis=1).astype(BF16)
    bd_out = lambda m: jnp.einsum('ghp,gk->gpkh', m, eye_g).reshape(S5_NSTATE, S5_WIDTH)
    wc = jnp.concatenate([bd_out(s5_c_re[l]), -bd_out(s5_c_im[l])], axis=0).astype(BF16)
    return dict(
        ln=ln_gain[l].reshape(1, D_MODEL), w_in=w_r, wg=wg, bg=gla_b_gate[l].reshape(1, GLA_QK),
        gla_gain=jnp.tile(gla_norm_gain[l], GLA_HEADS).reshape(1, GLA_WIDTH),
        gq=gq, wq=wq, gkv=mla_kv_norm_gain[l].reshape(1, MLA_KV_RANK), wkv=wkv_r,
        lre=jnp.real(lam_bar).reshape(1, S5_NSTATE), lim=jnp.imag(lam_bar).reshape(1, S5_NSTATE),
        wb=wb, wc=wc, d=s5_d[l].reshape(1, S5_WIDTH), wglu=s5_w_glu[l].astype(BF16),
        bglu=s5_b_glu[l].reshape(1, S5_WIDTH), w_out=w_out[l].astype(BF16))


def _rope_tables(past, t):
    half = MLA_ROPE_DIM // 2
    inv = ROPE_BASE ** (-jnp.arange(half, dtype=F32) / half)
    ang = (past + jnp.arange(t)).astype(F32)[:, None] * inv[None, :]
    cos, sin = jnp.cos(ang), jnp.sin(ang)
    cos_t = jnp.concatenate([jnp.ones((t, MLA_NOPE_DIM), F32), cos, cos,
                             jnp.zeros((t, MLA_QK_PAD - MLA_NOPE_DIM - MLA_ROPE_DIM), F32)], axis=1)
    sin_t = jnp.concatenate([jnp.zeros((t, MLA_NOPE_DIM), F32), sin, sin,
                             jnp.zeros((t, MLA_QK_PAD - MLA_NOPE_DIM - MLA_ROPE_DIM), F32)], axis=1)
    return cos_t, sin_t


def _gla_state_to_bd(s):
    eye_h = jnp.eye(GLA_HEADS, dtype=s.dtype)
    return jnp.einsum('bhkv,hg->bhkgv', s, eye_h).reshape(s.shape[0], GLA_QK, GLA_WIDTH)


def _gla_state_from_bd(s_bd):
    s5 = s_bd.reshape(s_bd.shape[0], GLA_HEADS, GLA_DK, GLA_HEADS, GLA_DV)
    return jnp.stack([s5[:, h, :, h, :] for h in range(GLA_HEADS)], axis=1)


def _trunk(x, gla_state, ckv_cache, kpe_cache, s5_re, s5_im, params, final_gain):
    bsz, t, _ = x.shape
    n = bsz * t
    depth = len(params)
    past = 0 if ckv_cache is None else ckv_cache.shape[2]
    s_len = past + t
    tm = min(512, n)
    tq = min(512, t)
    tk = 512 if past == 0 else s_len
    cos_t, sin_t = _rope_tables(past, t)
    x2 = x.reshape(n, D_MODEL)
    gla_o, ckv_o, kpe_o, re_o, im_o = [], [], [], [], []
    for l, p in enumerate(params):
        pg, pm, ps = _inproj(x2, p['ln'], p['w_in'], tm)
        s0 = (jnp.zeros((bsz, GLA_QK, GLA_WIDTH), F32) if gla_state is None else _gla_state_to_bd(gla_state[l]))
        o_gla, s_bd = _gla(pg, p['wg'], p['bg'], p['gla_gain'], s0, bsz, t)
        gla_o.append(_gla_state_from_bd(s_bd))
        q, ckv_new, kpe128 = _mla_prepq(pm, cos_t, sin_t, p['gq'], p['wq'], p['gkv'], t, min(tm, t))
        ckv_o.append(ckv_new.reshape(bsz, t, MLA_KV_RANK))
        kpe_o.append(kpe128[:, MLA_NOPE_DIM:MLA_NOPE_DIM + MLA_ROPE_DIM].reshape(bsz, t, MLA_ROPE_DIM))
        if past == 0:
            ckv_all, kpe_all = ckv_new, kpe128
        else:
            kpe_past = jnp.pad(kpe_cache[l], ((0, 0), (0, 0), (MLA_NOPE_DIM, MLA_QK_PAD - MLA_NOPE_DIM - MLA_ROPE_DIM)))
            ckv_all = jnp.concatenate([ckv_cache[l], ckv_new.reshape(bsz, t, -1)], axis=1).reshape(bsz * s_len, -1)
            kpe_all = jnp.concatenate([kpe_past, kpe128.reshape(bsz, t, -1)], axis=1).reshape(bsz * s_len, -1)
        k_cat, v_all = _mla_prepkv(ckv_all, kpe_all, p['wkv'], 512 if past == 0 else s_len)
        o_mla = _attn(q, k_cat, v_all, pm, bsz, t, s_len, past, tq, tk)
        x0r = jnp.zeros((bsz, S5_NSTATE), F32) if s5_re is None else s5_re[l].reshape(bsz, S5_NSTATE)
        x0i = jnp.zeros((bsz, S5_NSTATE), F32) if s5_im is None else s5_im[l].reshape(bsz, S5_NSTATE)
        o_s5, xr, xi = _s5(ps.reshape(bsz, t, PS_COLS), x0r, x0i, p['lre'], p['lim'], p['wb'], p['wc'],
                           p['d'], p['wglu'], p['bglu'], bsz, t)
        re_o.append(xr.reshape(bsz, S5_GROUPS, S5_STATE))
        im_o.append(xi.reshape(bsz, S5_GROUPS, S5_STATE))
        x2 = _outproj(x2, o_gla, o_mla, o_s5.reshape(n, S5_WIDTH), p['w_out'], final_gain.reshape(1, D_MODEL),
                      tm, final=(l == depth - 1))
    return (x2.reshape(bsz, t, D_MODEL), jnp.stack(gla_o), jnp.stack(ckv_o), jnp.stack(kpe_o),
            jnp.stack(re_o), jnp.stack(im_o))


def kernel(x_prompt, x_sample, state_gla, cache_mla_ckv, cache_mla_kpe, state_s5_re, state_s5_im, ln_gain, w_in, gla_w_gate, gla_b_gate, gla_norm_gain, mla_q_norm_gain, mla_w_uq, mla_kv_norm_gain, mla_w_ukv, s5_lambda_re, s5_lambda_im, s5_b_re, s5_b_im, s5_c_re, s5_c_im, s5_d, s5_log_dt, s5_w_glu, s5_b_glu, w_out, final_gain):
    depth = w_in.shape[0]
    params = [_layer_params(l, ln_gain, w_in, gla_w_gate, gla_b_gate, gla_norm_gain, mla_q_norm_gain, mla_w_uq,
                            mla_kv_norm_gain, mla_w_ukv, s5_lambda_re, s5_lambda_im, s5_b_re, s5_b_im,
                            s5_c_re, s5_c_im, s5_d, s5_log_dt, s5_w_glu, s5_b_glu, w_out) for l in range(depth)]
    y_p, gla_p, ckv_p, kpe_p, re_p, im_p = _trunk(x_prompt, None, None, None, None, None, params, final_gain)
    y_s, gla_s, ckv_s, kpe_s, re_s, im_s = _trunk(x_sample, state_gla, cache_mla_ckv, cache_mla_kpe,
                                                  state_s5_re, state_s5_im, params, final_gain)
    return (y_p, y_s, gla_p, ckv_p, kpe_p, re_p, im_p, gla_s, ckv_s, kpe_s, re_s, im_s)
```

```python
import functools
import math

import numpy as np
import jax
import jax.numpy as jnp
from jax import lax
from jax.experimental import pallas as pl
from jax.experimental.pallas import tpu as pltpu

F32 = jnp.float32
BF16 = jnp.bfloat16

LANES = 128
D_MODEL = 1024
CHUNK = 64
EPS = 1e-6
GLA_HEADS = 4
GLA_DV = 64
GLA_DK = 32
GLA_WIDTH = GLA_HEADS * GLA_DV
GLA_QK = GLA_HEADS * GLA_DK
GLA_GATE_RANK = 16
GLA_GATE_TAU = 16.0
GLA_SUB = 16
MLA_HEADS = 4
MLA_NOPE_DIM = 64
MLA_ROPE_DIM = 32
MLA_V_DIM = 128
MLA_Q_RANK = 192
MLA_KV_RANK = 128
MLA_WIDTH = MLA_HEADS * MLA_V_DIM
MLA_QK_PAD = 128
MLA_QK_WIDTH = MLA_HEADS * MLA_QK_PAD
ROPE_BASE = 10000.0
S5_GROUPS = 16
S5_GROUP_CH = 16
S5_STATE = 64
S5_WIDTH = S5_GROUPS * S5_GROUP_CH
S5_NSTATE = S5_GROUPS * S5_STATE
S5_BATCH_TILE = 8

PG_COLS = 896
PM_COLS = 1024
PS_COLS = 512
IN_COLS_R = PG_COLS + PM_COLS + PS_COLS

VMEM_LIMIT_BYTES = 48 * 1024 * 1024


def _tiles(bsz, t, past):
    n = bsz * t
    s_len = past + t
    return dict(
        row=min(512, n),
        prep=min(512, t),
        gla_rows=min(512, t), gla_chunk=min(CHUNK, t),
        attn_q=min(256, t), attn_k=256 if past == 0 else s_len,
        s5_rows=min(128, t))


def _cparams(*sem):
    return pltpu.CompilerParams(dimension_semantics=sem, vmem_limit_bytes=VMEM_LIMIT_BYTES)


def _sigmoid(x):
    return 1.0 / (1.0 + jnp.exp(-x))


def _dot(a, b):
    return jnp.dot(a, b, preferred_element_type=F32)


def _dot_t(a, b):
    return lax.dot_general(a, b, (((0,), (0,)), ((), ())), preferred_element_type=F32)


def _const_spec(shape, ngrid):
    zeros = (0,) * len(shape)
    return pl.BlockSpec(shape, lambda *_: zeros)


def _layer_spec(shape, layer):
    zeros = (0,) * len(shape)
    return pl.BlockSpec((None,) + tuple(shape), lambda *_: (layer,) + zeros)


def _inproj_body(x_ref, g_ref, w_ref, og_ref, om_ref, os_ref):
    x = x_ref[...]
    ms = jnp.mean(x * x, axis=-1, keepdims=True)
    h = (x * lax.rsqrt(ms + EPS) * g_ref[...]).astype(BF16)
    og_ref[...] = _dot(h, w_ref[:, 0:PG_COLS]).astype(BF16)
    om_ref[...] = _dot(h, w_ref[:, PG_COLS:PG_COLS + PM_COLS]).astype(BF16)
    os_ref[...] = _dot(h, w_ref[:, PG_COLS + PM_COLS:]).astype(BF16)


def _inproj(x2, gain, w, layer, tm):
    n = x2.shape[0]
    return pl.pallas_call(
        _inproj_body,
        grid=(n // tm,),
        in_specs=[pl.BlockSpec((tm, D_MODEL), lambda i: (i, 0)),
                  _layer_spec((1, D_MODEL), layer),
                  _layer_spec((D_MODEL, IN_COLS_R), layer)],
        out_specs=[pl.BlockSpec((tm, PG_COLS), lambda i: (i, 0)),
                   pl.BlockSpec((tm, PM_COLS), lambda i: (i, 0)),
                   pl.BlockSpec((tm, PS_COLS), lambda i: (i, 0))],
        out_shape=[jax.ShapeDtypeStruct((n, PG_COLS), BF16),
                   jax.ShapeDtypeStruct((n, PM_COLS), BF16),
                   jax.ShapeDtypeStruct((n, PS_COLS), BF16)],
        compiler_params=_cparams("parallel"),
        name="inproj",
    )(x2, gain, w)


def _gla_body(p_ref, wg_ref, bg_ref, gain_ref, tri_ref, ones_k_ref, ones_v_ref, ones_r_ref, bd_ref, s0_ref,
              o_ref, sout_ref, q_scr, k_scr, v_scr, b_scr, s_scr, *, chunk, nch):
    it = pl.program_id(1)
    sub = min(GLA_SUB, chunk)
    nsub = chunk // sub

    @pl.when(it == 0)
    def _():
        s_scr[...] = s0_ref[0]

    q_scr[...] = p_ref[:, 0:128].astype(F32) * (GLA_DK ** -0.5)
    k_scr[...] = p_ref[:, 128:256].astype(F32)
    v_scr[...] = p_ref[:, 256:512].astype(F32)
    logit = _dot(p_ref[:, 768:896], wg_ref[...]) + bg_ref[...]
    log_a = (jnp.minimum(logit, 0.0) - jnp.log(1.0 + jnp.exp(-jnp.abs(logit)))) * (1.0 / GLA_GATE_TAU)
    la_hi = log_a.astype(BF16)
    la_lo = (log_a - la_hi.astype(F32)).astype(BF16)
    b_scr[...] = _dot(tri_ref[...], la_hi) + _dot(tri_ref[...], la_lo)

    row = lax.broadcasted_iota(jnp.int32, (chunk, GLA_QK), 0)
    row_in_sub = lax.broadcasted_iota(jnp.int32, (sub, GLA_QK), 0)
    row16 = lax.broadcasted_iota(jnp.int32, (16, GLA_QK), 0)

    def one_chunk(c, carry):
        r0 = pl.multiple_of(c * chunk, chunk)
        qc = q_scr[pl.ds(r0, chunk), :]
        kc = k_scr[pl.ds(r0, chunk), :]
        vc = v_scr[pl.ds(r0, chunk), :]
        bc = b_scr[pl.ds(r0, chunk), :]
        s_prev = s_scr[...]
        xs = [qc * jnp.exp(bc)]
        ks = []
        for sj in range(nsub - 1):
            e_j = b_scr[pl.ds(r0 + (sj + 1) * sub - 1, 1), :]
            later = row >= (sj + 1) * sub
            xs.append(jnp.where(later, qc * jnp.exp(jnp.where(later, bc - e_j, 0.0)), 0.0))
            own = (row >= sj * sub) & (row < (sj + 1) * sub)
            ks.append(jnp.where(own, kc * jnp.exp(jnp.where(own, e_j - bc, 0.0)), 0.0))
        b_end = b_scr[pl.ds(r0 + chunk - 1, 1), :]
        ks.append(kc * jnp.exp(b_end - bc))
        k_all = jnp.concatenate(ks, axis=1).astype(BF16)
        a_all = _dot_t(k_all, vc.astype(BF16)) * bd_ref[...]
        w = jnp.concatenate([s_prev, a_all[:(nsub - 1) * GLA_QK]], axis=0).astype(BF16) if nsub > 1 \
            else s_prev.astype(BF16)
        o_off = _dot(jnp.concatenate(xs, axis=1).astype(BF16), w)
        rows = []
        for si in range(nsub):
            q_i = qc[si * sub:(si + 1) * sub]
            b_i = bc[si * sub:(si + 1) * sub]
            es = []
            for j in range(sub):
                r = r0 + si * sub + j
                b_j = b_scr[pl.ds(r, 1), :]
                k_j = k_scr[pl.ds(r, 1), :]
                valid = row_in_sub >= j
                es.append((q_i * k_j * jnp.exp(jnp.where(valid, b_i - b_j, -jnp.inf))).astype(BF16))
            e_all = jnp.concatenate(es, axis=0)
            p_all = _dot(e_all, ones_k_ref[...])
            acc = o_off[si * sub:(si + 1) * sub]
            for j in range(sub):
                v_j = v_scr[pl.ds(r0 + si * sub + j, 1), :]
                acc = acc + p_all[j * sub:(j + 1) * sub] * v_j
            rows.append(acc)
        o = rows[0] if nsub == 1 else jnp.concatenate(rows, axis=0)
        ms = _dot((o * o).astype(BF16), ones_v_ref[...]) * (1.0 / GLA_DV)
        o_n = o * lax.rsqrt(ms + EPS) * gain_ref[...]
        z = p_ref[pl.ds(r0, chunk), 512:768].astype(F32)
        o_ref[pl.ds(r0, chunk), :] = (o_n * (z * _sigmoid(z))).astype(BF16)
        d16 = jnp.where(row16 == 0, jnp.exp(b_end), 0.0)
        d_hi = d16.astype(BF16)
        d_lo = (d16 - d_hi.astype(F32)).astype(BF16)
        d_col = _dot_t(d_hi, ones_r_ref[...]) + _dot_t(d_lo, ones_r_ref[...])
        s_scr[...] = s_prev * d_col + a_all[(nsub - 1) * GLA_QK:]
        return carry

    lax.fori_loop(0, nch, one_chunk, 0, unroll=2 if nch % 2 == 0 else 1)

    @pl.when(it == pl.num_programs(1) - 1)
    def _():
        sout_ref[0] = s_scr[...]


def _gla_consts(tt, chunk):
    nsub = chunk // min(GLA_SUB, chunk)
    r = np.arange(tt)
    tri = ((r[None, :] <= r[:, None]) & (r[None, :] // chunk == r[:, None] // chunk)).astype(np.float32)
    hk = np.arange(GLA_QK) // GLA_DK
    hv = np.arange(GLA_WIDTH) // GLA_DV
    same_kv = (hk[:, None] == hv[None, :]).astype(np.float32)
    same_vv = (hv[:, None] == hv[None, :]).astype(np.float32)
    return (jnp.asarray(tri, BF16), jnp.asarray(same_kv, BF16), jnp.asarray(same_vv, BF16),
            jnp.ones((16, GLA_WIDTH), BF16), jnp.asarray(np.tile(same_kv, (nsub, 1)), F32))


def _gla(pg, wg, bg, gain, layer, s0_bd, bsz, t, tl):
    chunk, tt = tl['gla_chunk'], tl['gla_rows']
    nch = tt // chunk
    nt = t // tt
    nsub = chunk // min(GLA_SUB, chunk)
    tri, ones_k, ones_v, ones_r, bd = _gla_consts(tt, chunk)
    const = lambda shape: _const_spec(shape, 2)
    return pl.pallas_call(
        functools.partial(_gla_body, chunk=chunk, nch=nch),
        grid=(bsz, nt),
        in_specs=[pl.BlockSpec((tt, PG_COLS), lambda b, i: (b * nt + i, 0)),
                  _layer_spec((GLA_QK, GLA_QK), layer), _layer_spec((1, GLA_QK), layer),
                  _layer_spec((1, GLA_WIDTH), layer),
                  const((tt, tt)), const((GLA_QK, GLA_WIDTH)), const((GLA_WIDTH, GLA_WIDTH)),
                  const((16, GLA_WIDTH)), const((nsub * GLA_QK, GLA_WIDTH)),
                  pl.BlockSpec((1, GLA_QK, GLA_WIDTH), lambda b, i: (b, 0, 0))],
        out_specs=[pl.BlockSpec((tt, GLA_WIDTH), lambda b, i: (b * nt + i, 0)),
                   pl.BlockSpec((1, GLA_QK, GLA_WIDTH), lambda b, i: (b, 0, 0))],
        out_shape=[jax.ShapeDtypeStruct((bsz * t, GLA_WIDTH), BF16),
                   jax.ShapeDtypeStruct((bsz, GLA_QK, GLA_WIDTH), F32)],
        scratch_shapes=[pltpu.VMEM((tt, GLA_QK), F32), pltpu.VMEM((tt, GLA_QK), F32),
                        pltpu.VMEM((tt, GLA_WIDTH), F32), pltpu.VMEM((tt, GLA_QK), F32),
                        pltpu.VMEM((GLA_QK, GLA_WIDTH), F32)],
        compiler_params=_cparams("parallel", "arbitrary"),
        name="gla",
    )(pg, wg, bg, gain, tri, ones_k, ones_v, ones_r, bd, s0_bd)


def _rope128(x, cos_t, sin_t):
    lane = lax.broadcasted_iota(jnp.int32, x.shape, 1)
    first_half = (lane >= MLA_NOPE_DIM) & (lane < MLA_NOPE_DIM + MLA_ROPE_DIM // 2)
    rot = jnp.where(first_half, -pltpu.roll(x, LANES - MLA_ROPE_DIM // 2, 1), pltpu.roll(x, MLA_ROPE_DIM // 2, 1))
    return x * cos_t + rot * sin_t


def _mla_prepq_body(p_ref, cos_ref, sin_ref, gq_ref, wq_ref, gkv_ref, q_ref, ckv_ref, kpe_ref):
    cos_t = cos_ref[...]
    sin_t = sin_ref[...]
    cq = p_ref[:, 0:256].astype(F32)
    ms = jnp.sum(cq * cq, axis=-1, keepdims=True) * (1.0 / MLA_Q_RANK)
    cqn = (cq * lax.rsqrt(ms + EPS) * gq_ref[...]).astype(BF16)
    qh = _dot(cqn, wq_ref[...])
    scale = (MLA_NOPE_DIM + MLA_ROPE_DIM) ** -0.5 * math.log2(math.e)
    for h in range(MLA_HEADS):
        x = qh[:, h * MLA_QK_PAD:(h + 1) * MLA_QK_PAD]
        q_ref[:, h * MLA_QK_PAD:(h + 1) * MLA_QK_PAD] = (_rope128(x, cos_t, sin_t) * scale).astype(BF16)
    ckv = p_ref[:, 256:384].astype(F32)
    ms = jnp.mean(ckv * ckv, axis=-1, keepdims=True)
    ckv_ref[...] = ckv * lax.rsqrt(ms + EPS) * gkv_ref[...]
    kr = p_ref[:, 384:512].astype(F32)
    kpe_ref[...] = _rope128(kr, cos_t, sin_t)


def _mla_prepq(pm, cos_t, sin_t, gq, wq, gkv, layer, t, tm):
    n = pm.shape[0]
    ntab = t // tm
    return pl.pallas_call(
        _mla_prepq_body,
        grid=(n // tm,),
        in_specs=[pl.BlockSpec((tm, 512), lambda i: (i, 0)),
                  pl.BlockSpec((tm, LANES), lambda i: (i % ntab, 0)),
                  pl.BlockSpec((tm, LANES), lambda i: (i % ntab, 0)),
                  _layer_spec((1, 256), layer), _layer_spec((256, MLA_QK_WIDTH), layer),
                  _layer_spec((1, MLA_KV_RANK), layer)],
        out_specs=[pl.BlockSpec((tm, MLA_QK_WIDTH), lambda i: (i, 0)),
                   pl.BlockSpec((tm, MLA_KV_RANK), lambda i: (i, 0)),
                   pl.BlockSpec((tm, LANES), lambda i: (i, 0))],
        out_shape=[jax.ShapeDtypeStruct((n, MLA_QK_WIDTH), BF16),
                   jax.ShapeDtypeStruct((n, MLA_KV_RANK), F32),
                   jax.ShapeDtypeStruct((n, LANES), F32)],
        compiler_params=_cparams("parallel"),
        name="mla_prepq",
    )(pm, cos_t, sin_t, gq, wq, gkv)


def _mla_kv_rows(ckv, kpe128, wkv_ref, k_ref, v_ref, rows):
    kv = _dot(ckv.astype(BF16), wkv_ref[...])
    for h in range(MLA_HEADS):
        sl = slice(h * MLA_QK_PAD, (h + 1) * MLA_QK_PAD)
        k_ref[rows, sl] = (kv[:, sl] + kpe128).astype(BF16)
    v_ref[rows, :] = kv[:, MLA_QK_WIDTH:].astype(BF16)


def _mla_prepkv_body(ckv_ref, kpe_ref, wkv_ref, k_ref, v_ref):
    _mla_kv_rows(ckv_ref[...], kpe_ref[...], wkv_ref, k_ref, v_ref, slice(None))


def _mla_prepkv_past_body(ckv_ref, kpe_ref, ckv_past_ref, kpe_past_ref, place_ref, wkv_ref, k_ref, v_ref, *, past):
    kpe_past = _dot(kpe_past_ref[...].astype(BF16), place_ref[...])
    _mla_kv_rows(ckv_past_ref[...], kpe_past, wkv_ref, k_ref, v_ref, slice(0, past))
    _mla_kv_rows(ckv_ref[...], kpe_ref[...], wkv_ref, k_ref, v_ref, slice(past, None))


def _mla_prepkv(ckv, kpe128, wkv, layer, bsz, t, tm, ckv_past=None, kpe_past=None):
    out_shape = lambda rows: [jax.ShapeDtypeStruct((rows, MLA_QK_WIDTH), BF16),
                              jax.ShapeDtypeStruct((rows, MLA_WIDTH), BF16)]
    if ckv_past is None:
        n = bsz * t
        return pl.pallas_call(
            _mla_prepkv_body,
            grid=(n // tm,),
            in_specs=[pl.BlockSpec((tm, MLA_KV_RANK), lambda i: (i, 0)),
                      pl.BlockSpec((tm, LANES), lambda i: (i, 0)),
                      _layer_spec((MLA_KV_RANK, MLA_QK_WIDTH + MLA_WIDTH), layer)],
            out_specs=[pl.BlockSpec((tm, MLA_QK_WIDTH), lambda i: (i, 0)),
                       pl.BlockSpec((tm, MLA_WIDTH), lambda i: (i, 0))],
            out_shape=out_shape(n),
            compiler_params=_cparams("parallel"),
            name="mla_prepkv",
        )(ckv, kpe128, wkv)
    past = ckv_past.shape[2]
    s_len = past + t
    place = np.zeros((MLA_ROPE_DIM, LANES), np.float32)
    place[np.arange(MLA_ROPE_DIM), MLA_NOPE_DIM + np.arange(MLA_ROPE_DIM)] = 1.0
    return pl.pallas_call(
        functools.partial(_mla_prepkv_past_body, past=past),
        grid=(bsz,),
        in_specs=[pl.BlockSpec((t, MLA_KV_RANK), lambda b: (b, 0)),
                  pl.BlockSpec((t, LANES), lambda b: (b, 0)),
                  pl.BlockSpec((None, None, past, MLA_KV_RANK), lambda b: (layer, b, 0, 0)),
                  pl.BlockSpec((None, None, past, MLA_ROPE_DIM), lambda b: (layer, b, 0, 0)),
                  _const_spec((MLA_ROPE_DIM, LANES), 1),
                  _layer_spec((MLA_KV_RANK, MLA_QK_WIDTH + MLA_WIDTH), layer)],
        out_specs=[pl.BlockSpec((s_len, MLA_QK_WIDTH), lambda b: (b, 0)),
                   pl.BlockSpec((s_len, MLA_WIDTH), lambda b: (b, 0))],
        out_shape=out_shape(bsz * s_len),
        compiler_params=_cparams("parallel"),
        name="mla_prepkv_past",
    )(ckv, kpe128, ckv_past, kpe_past, jnp.asarray(place, BF16), wkv)


def _attn_body(q_ref, k_ref, v_ref, z_ref, o_ref, m_scr, acc_scr, *, past, tq, tk, s_len):
    iq = pl.program_id(1)
    q_first = past + iq * tq
    full_keys = jnp.minimum((q_first // CHUNK + 1) * CHUNK, s_len)
    vis_keys = jnp.minimum(((q_first + tq - 1) // CHUNK + 1) * CHUNK, s_len)
    n_full = full_keys // tk
    n_vis = (vis_keys + tk - 1) // tk
    ones_v = jnp.ones((tk, MLA_V_DIM), BF16)
    ntile, rem = tk // LANES, tk % LANES

    m_scr[...] = jnp.full(m_scr.shape, -jnp.inf, F32)
    acc_scr[...] = jnp.zeros(acc_scr.shape, F32)

    def block(kb, carry, masked):
        k0 = pl.multiple_of(kb * tk, tk)
        if masked:
            q_chunk = (q_first + lax.broadcasted_iota(jnp.int32, (tq, tk), 0)) // CHUNK
            k_chunk = (k0 + lax.broadcasted_iota(jnp.int32, (tq, tk), 1)) // CHUNK
            visible = k_chunk <= q_chunk
        for h in range(MLA_HEADS):
            sl = slice(h * MLA_QK_PAD, (h + 1) * MLA_QK_PAD)
            vsl = slice(h * MLA_V_DIM, (h + 1) * MLA_V_DIM)
            s = lax.dot_general(q_ref[:, sl], k_ref[pl.ds(k0, tk), sl], (((1,), (1,)), ((), ())),
                                preferred_element_type=F32)
            if masked:
                s = jnp.where(visible, s, -jnp.inf)
            m_prev = m_scr[h]
            m_new = jnp.maximum(m_prev, jnp.max(s, axis=-1, keepdims=True))
            alpha = jnp.exp2(m_prev - m_new)
            ps = [jnp.exp2(s[:, c * LANES:(c + 1) * LANES] - m_new) for c in range(ntile)]
            if rem:
                ps.append(jnp.exp2(s[:, ntile * LANES:] - m_new[:, :rem]))
            p = jnp.concatenate(ps, axis=1).astype(BF16)
            v_ext = jnp.concatenate([v_ref[pl.ds(k0, tk), vsl], ones_v], axis=1)
            acc_scr[h] = jnp.concatenate([alpha, alpha], axis=1) * acc_scr[h] + _dot(p, v_ext)
            m_scr[h] = m_new
        return carry

    lax.fori_loop(0, n_full, functools.partial(block, masked=False), 0)
    lax.fori_loop(n_full, n_vis, functools.partial(block, masked=True), 0)
    for h in range(MLA_HEADS):
        vsl = slice(h * MLA_V_DIM, (h + 1) * MLA_V_DIM)
        z = z_ref[:, vsl].astype(F32)
        acc = acc_scr[h]
        o_ref[:, vsl] = (acc[:, :MLA_V_DIM] / acc[:, MLA_V_DIM:] * (z * _sigmoid(z))).astype(BF16)


def _attn(q, k, v, pm, bsz, t, s_len, past, tq, tk):
    nq = t // tq
    return pl.pallas_call(
        functools.partial(_attn_body, past=past, tq=tq, tk=tk, s_len=s_len),
        grid=(bsz, nq),
        in_specs=[pl.BlockSpec((tq, MLA_QK_WIDTH), lambda b, iq: (b * nq + iq, 0)),
                  pl.BlockSpec((s_len, MLA_QK_WIDTH), lambda b, iq: (b, 0)),
                  pl.BlockSpec((s_len, MLA_WIDTH), lambda b, iq: (b, 0)),
                  pl.BlockSpec((tq, MLA_WIDTH), lambda b, iq: (b * nq + iq, 1))],
        out_specs=pl.BlockSpec((tq, MLA_WIDTH), lambda b, iq: (b * nq + iq, 0)),
        out_shape=jax.ShapeDtypeStruct((bsz * t, MLA_WIDTH), BF16),
        scratch_shapes=[pltpu.VMEM((MLA_HEADS, tq, LANES), F32), pltpu.VMEM((MLA_HEADS, tq, 2 * MLA_V_DIM), F32)],
        compiler_params=_cparams("parallel", "arbitrary"),
        name="mla_attn",
    )(q, k, v, pm)


def _s5_body(p_ref, x0r_ref, x0i_ref, lre_ref, lim_ref, wb_ref, wc_ref, d_ref, wglu_ref, bglu_ref,
             o_ref, xr_out, xi_out, uz_bt, uz_tb, bu, o_tb, xr_s, xi_s, *, lc, pitch):
    it = pl.program_id(1)
    nb = S5_BATCH_TILE

    @pl.when(it == 0)
    def _():
        xr_s[...] = x0r_ref[...]
        xi_s[...] = x0i_ref[...]

    for b in range(nb):
        for c in range(PS_COLS // LANES):
            uz_bt[c, b * pitch:b * pitch + lc, :] = p_ref[b, :, c * LANES:(c + 1) * LANES].astype(F32)
    for t in range(lc):
        for c in range(PS_COLS // LANES):
            uz_tb[t * nb:(t + 1) * nb, c * LANES:(c + 1) * LANES] = uz_bt[c, pl.ds(t, nb, stride=pitch), :]
    u = uz_tb[:, 0:S5_WIDTH]
    bu[...] = _dot(u.astype(BF16), wb_ref[...])

    lre = jnp.broadcast_to(lre_ref[...], (nb, S5_NSTATE))
    lim = jnp.broadcast_to(lim_ref[...], (nb, S5_NSTATE))

    def step(t, carry):
        xr, xi = carry
        r0 = pl.multiple_of(t * nb, nb)
        nr = lre * xr - lim * xi + bu[pl.ds(r0, nb), 0:S5_NSTATE]
        ni = lre * xi + lim * xr + bu[pl.ds(r0, nb), S5_NSTATE:2 * S5_NSTATE]
        bu[pl.ds(r0, nb), 0:S5_NSTATE] = nr
        bu[pl.ds(r0, nb), S5_NSTATE:2 * S5_NSTATE] = ni
        return nr, ni

    xr, xi = lax.fori_loop(0, lc, step, (xr_s[...], xi_s[...]))
    xr_s[...] = xr
    xi_s[...] = xi

    y = _dot(bu[...].astype(BF16), wc_ref[...]) + d_ref[...] * u
    g5 = 0.5 * y * (1.0 + jnp.tanh(0.7978845608028654 * (y + 0.044715 * (y * y * y))))
    gate = _sigmoid(_dot(g5.astype(BF16), wglu_ref[...]) + bglu_ref[...])
    z = uz_tb[:, S5_WIDTH:2 * S5_WIDTH]
    o = g5 * gate * (z * _sigmoid(z))
    for c in range(S5_WIDTH // LANES):
        o_tb[c] = o[:, c * LANES:(c + 1) * LANES]
    for b in range(nb):
        for c in range(S5_WIDTH // LANES):
            o_ref[b, :, c * LANES:(c + 1) * LANES] = o_tb[c, pl.ds(b, lc, stride=nb), :].astype(BF16)

    @pl.when(it == pl.num_programs(1) - 1)
    def _():
        xr_out[...] = xr
        xi_out[...] = xi


def _s5(ps3, x0r, x0i, lre, lim, wb, wc, d, wglu, bglu, layer, bsz, t, lc):
    nb = S5_BATCH_TILE
    nt = t // lc
    pitch = lc + 8
    return pl.pallas_call(
        functools.partial(_s5_body, lc=lc, pitch=pitch),
        grid=(bsz // nb, nt),
        in_specs=[pl.BlockSpec((nb, lc, PS_COLS), lambda g, i: (g, i, 0)),
                  pl.BlockSpec((nb, S5_NSTATE), lambda g, i: (g, 0)),
                  pl.BlockSpec((nb, S5_NSTATE), lambda g, i: (g, 0)),
                  _layer_spec((1, S5_NSTATE), layer), _layer_spec((1, S5_NSTATE), layer),
                  _layer_spec((S5_WIDTH, 2 * S5_NSTATE), layer), _layer_spec((2 * S5_NSTATE, S5_WIDTH), layer),
                  _layer_spec((1, S5_WIDTH), layer), _layer_spec((S5_WIDTH, S5_WIDTH), layer),
                  _layer_spec((1, S5_WIDTH), layer)],
        out_specs=[pl.BlockSpec((nb, lc, S5_WIDTH), lambda g, i: (g, i, 0)),
                   pl.BlockSpec((nb, S5_NSTATE), lambda g, i: (g, 0)),
                   pl.BlockSpec((nb, S5_NSTATE), lambda g, i: (g, 0))],
        out_shape=[jax.ShapeDtypeStruct((bsz, t, S5_WIDTH), BF16),
                   jax.ShapeDtypeStruct((bsz, S5_NSTATE), F32),
                   jax.ShapeDtypeStruct((bsz, S5_NSTATE), F32)],
        scratch_shapes=[pltpu.VMEM((PS_COLS // LANES, nb * pitch, LANES), F32), pltpu.VMEM((lc * nb, PS_COLS), F32),
                        pltpu.VMEM((lc * nb, 2 * S5_NSTATE), F32), pltpu.VMEM((S5_WIDTH // LANES, lc * nb, LANES), F32),
                        pltpu.VMEM((nb, S5_NSTATE), F32), pltpu.VMEM((nb, S5_NSTATE), F32)],
        compiler_params=_cparams("parallel", "arbitrary"),
        name="s5",
    )(ps3, x0r, x0i, lre, lim, wb, wc, d, wglu, bglu)


def _outproj_body(x_ref, og_ref, om_ref, os_ref, w_ref, g_ref, o_ref, *, final):
    acc = _dot(og_ref[...], w_ref[0:GLA_WIDTH, :])
    acc += _dot(om_ref[...], w_ref[GLA_WIDTH:GLA_WIDTH + MLA_WIDTH, :])
    acc += _dot(os_ref[...], w_ref[GLA_WIDTH + MLA_WIDTH:, :])
    xn = x_ref[...] + acc
    if final:
        ms = jnp.mean(xn * xn, axis=-1, keepdims=True)
        xn = xn * lax.rsqrt(ms + EPS) * g_ref[...]
    o_ref[...] = xn


def _outproj(x2, og, om, os_, w, gain, layer, tm, final):
    n = x2.shape[0]
    row = lambda c: pl.BlockSpec((tm, c), lambda i: (i, 0))
    return pl.pallas_call(
        functools.partial(_outproj_body, final=final),
        grid=(n // tm,),
        in_specs=[row(D_MODEL), row(GLA_WIDTH), row(MLA_WIDTH), row(S5_WIDTH),
                  _layer_spec((D_MODEL, D_MODEL), layer),
                  _const_spec((1, D_MODEL), 1)],
        out_specs=row(D_MODEL),
        out_shape=jax.ShapeDtypeStruct((n, D_MODEL), F32),
        compiler_params=_cparams("parallel"),
        name="outproj_final" if final else "outproj",
    )(x2, og, om, os_, w, gain)


def _prepare_params(ln_gain, w_in, gla_w_gate, gla_b_gate, gla_norm_gain, mla_q_norm_gain, mla_w_uq,
                    mla_kv_norm_gain, mla_w_ukv, s5_lambda_re, s5_lambda_im, s5_b_re, s5_b_im, s5_c_re, s5_c_im,
                    s5_d, s5_log_dt, s5_w_glu, s5_b_glu, w_out):
    depth = w_in.shape[0]
    zc = lambda n: jnp.zeros((depth, D_MODEL, n), F32)
    off = np.cumsum((0, 128, 128, 256, 16, 256, 192, 128, 32, 512, 256, 256))
    g_q, g_k, g_v, g_lr, g_z, m_cq, m_ckv, m_kr, m_z, s_u, s_z = [w_in[:, :, off[i]:off[i + 1]] for i in range(11)]
    w_r = jnp.concatenate([g_q, g_k, g_v, g_z, g_lr, zc(112),
                           m_cq, zc(64), m_ckv, zc(64), m_kr, zc(32), m_z,
                           s_u, s_z], axis=2).astype(BF16)
    wg = jnp.pad(gla_w_gate, ((0, 0), (0, GLA_QK - GLA_GATE_RANK), (0, 0))).astype(BF16)
    wq = mla_w_uq.reshape(depth, MLA_Q_RANK, MLA_HEADS, MLA_NOPE_DIM + MLA_ROPE_DIM)
    wq = jnp.pad(wq, ((0, 0), (0, 256 - MLA_Q_RANK), (0, 0), (0, MLA_QK_PAD - MLA_NOPE_DIM - MLA_ROPE_DIM)))
    wq = wq.reshape(depth, 256, MLA_QK_WIDTH).astype(BF16)
    gq = jnp.pad(mla_q_norm_gain, ((0, 0), (0, 256 - MLA_Q_RANK))).reshape(depth, 1, 256)
    wkv = mla_w_ukv.reshape(depth, MLA_KV_RANK, MLA_HEADS, MLA_NOPE_DIM + MLA_V_DIM)
    wk = jnp.pad(wkv[..., :MLA_NOPE_DIM], ((0, 0), (0, 0), (0, 0), (0, MLA_QK_PAD - MLA_NOPE_DIM)))
    wkv_r = jnp.concatenate([wk.reshape(depth, MLA_KV_RANK, MLA_QK_WIDTH),
                             wkv[..., MLA_NOPE_DIM:].reshape(depth, MLA_KV_RANK, MLA_WIDTH)], axis=2).astype(BF16)
    dt = jnp.exp(s5_log_dt)[:, :, None]
    mag = jnp.exp(s5_lambda_re * dt)
    lbr, lbi = mag * jnp.cos(s5_lambda_im * dt), mag * jnp.sin(s5_lambda_im * dt)
    den = s5_lambda_re * s5_lambda_re + s5_lambda_im * s5_lambda_im
    qr = ((lbr - 1.0) * s5_lambda_re + lbi * s5_lambda_im) / den
    qi = (lbi * s5_lambda_re - (lbr - 1.0) * s5_lambda_im) / den
    bbr = qr[..., None] * s5_b_re - qi[..., None] * s5_b_im
    bbi = qr[..., None] * s5_b_im + qi[..., None] * s5_b_re
    eye_g = jnp.eye(S5_GROUPS, dtype=F32)
    bd_in = lambda m: (jnp.swapaxes(m, 2, 3)[:, :, :, None, :] * eye_g[None, :, None, :, None]
                       ).reshape(depth, S5_WIDTH, S5_NSTATE)
    bd_out = lambda m: (jnp.swapaxes(m, 2, 3)[:, :, :, None, :] * eye_g[None, :, None, :, None]
                        ).reshape(depth, S5_NSTATE, S5_WIDTH)
    wb = jnp.concatenate([bd_in(bbr), bd_in(bbi)], axis=2).astype(BF16)
    wc = jnp.concatenate([bd_out(s5_c_re), -bd_out(s5_c_im)], axis=1).astype(BF16)
    return dict(
        ln=ln_gain.reshape(depth, 1, D_MODEL), w_in=w_r, wg=wg, bg=gla_b_gate.reshape(depth, 1, GLA_QK),
        gla_gain=jnp.tile(gla_norm_gain, (1, GLA_HEADS)).reshape(depth, 1, GLA_WIDTH),
        gq=gq, wq=wq, gkv=mla_kv_norm_gain.reshape(depth, 1, MLA_KV_RANK), wkv=wkv_r,
        lre=lbr.reshape(depth, 1, S5_NSTATE), lim=lbi.reshape(depth, 1, S5_NSTATE),
        wb=wb, wc=wc, d=s5_d.reshape(depth, 1, S5_WIDTH), wglu=s5_w_glu.astype(BF16),
        bglu=s5_b_glu.reshape(depth, 1, S5_WIDTH), w_out=w_out.astype(BF16))


def _rope_tables(past, t):
    half = MLA_ROPE_DIM // 2
    inv = ROPE_BASE ** (-jnp.arange(half, dtype=F32) / half)
    ang = (past + jnp.arange(t)).astype(F32)[:, None] * inv[None, :]
    cos, sin = jnp.cos(ang), jnp.sin(ang)
    pad = MLA_QK_PAD - MLA_NOPE_DIM - MLA_ROPE_DIM
    cos_t = jnp.concatenate([jnp.ones((t, MLA_NOPE_DIM), F32), cos, cos, jnp.zeros((t, pad), F32)], axis=1)
    sin_t = jnp.concatenate([jnp.zeros((t, MLA_NOPE_DIM), F32), sin, sin, jnp.zeros((t, pad), F32)], axis=1)
    return cos_t, sin_t


def _gla_state_to_bd(s):
    eye_h = jnp.eye(GLA_HEADS, dtype=s.dtype)
    return (s[:, :, :, None, :] * eye_h[None, :, None, :, None]).reshape(s.shape[0], GLA_QK, GLA_WIDTH)


def _gla_state_from_bd(s_bd):
    s5 = s_bd.reshape(s_bd.shape[0], GLA_HEADS, GLA_DK, GLA_HEADS, GLA_DV)
    return jnp.stack([s5[:, h, :, h, :] for h in range(GLA_HEADS)], axis=1)


def _trunk(x, gla_state, ckv_cache, kpe_cache, s5_re, s5_im, p, final_gain):
    bsz, t, _ = x.shape
    n = bsz * t
    depth = p['w_in'].shape[0]
    past = 0 if ckv_cache is None else ckv_cache.shape[2]
    s_len = past + t
    tl = _tiles(bsz, t, past)
    cos_t, sin_t = _rope_tables(past, t)
    x2 = x.reshape(n, D_MODEL)
    gain_f = final_gain.reshape(1, D_MODEL)
    gla_o, ckv_o, kpe_o, re_o, im_o = [], [], [], [], []
    for l in range(depth):
        pg, pm, ps = _inproj(x2, p['ln'], p['w_in'], l, tl['row'])
        s0 = (jnp.zeros((bsz, GLA_QK, GLA_WIDTH), F32) if gla_state is None else _gla_state_to_bd(gla_state[l]))
        o_gla, s_bd = _gla(pg, p['wg'], p['bg'], p['gla_gain'], l, s0, bsz, t, tl)
        gla_o.append(_gla_state_from_bd(s_bd))
        q, ckv_new, kpe128 = _mla_prepq(pm, cos_t, sin_t, p['gq'], p['wq'], p['gkv'], l, t, tl['prep'])
        ckv_o.append(ckv_new.reshape(bsz, t, MLA_KV_RANK))
        kpe_o.append(kpe128[:, MLA_NOPE_DIM:MLA_NOPE_DIM + MLA_ROPE_DIM].reshape(bsz, t, MLA_ROPE_DIM))
        if past == 0:
            k_cat, v_all = _mla_prepkv(ckv_new, kpe128, p['wkv'], l, bsz, t, tl['prep'])
        else:
            k_cat, v_all = _mla_prepkv(ckv_new, kpe128, p['wkv'], l, bsz, t, tl['prep'], ckv_cache, kpe_cache)
        o_mla = _attn(q, k_cat, v_all, pm, bsz, t, s_len, past, tl['attn_q'], tl['attn_k'])
        x0r = jnp.zeros((bsz, S5_NSTATE), F32) if s5_re is None else s5_re[l].reshape(bsz, S5_NSTATE)
        x0i = jnp.zeros((bsz, S5_NSTATE), F32) if s5_im is None else s5_im[l].reshape(bsz, S5_NSTATE)
        o_s5, xr, xi = _s5(ps.reshape(bsz, t, PS_COLS), x0r, x0i, p['lre'], p['lim'], p['wb'], p['wc'],
                           p['d'], p['wglu'], p['bglu'], l, bsz, t, tl['s5_rows'])
        re_o.append(xr.reshape(bsz, S5_GROUPS, S5_STATE))
        im_o.append(xi.reshape(bsz, S5_GROUPS, S5_STATE))
        x2 = _outproj(x2, o_gla, o_mla, o_s5.reshape(n, S5_WIDTH), p['w_out'], gain_f, l,
                      tl['row'], final=(l == depth - 1))
    return (x2.reshape(bsz, t, D_MODEL), jnp.stack(gla_o), jnp.stack(ckv_o), jnp.stack(kpe_o),
            jnp.stack(re_o), jnp.stack(im_o))


def kernel(x_prompt, x_sample, state_gla, cache_mla_ckv, cache_mla_kpe, state_s5_re, state_s5_im, ln_gain, w_in, gla_w_gate, gla_b_gate, gla_norm_gain, mla_q_norm_gain, mla_w_uq, mla_kv_norm_gain, mla_w_ukv, s5_lambda_re, s5_lambda_im, s5_b_re, s5_b_im, s5_c_re, s5_c_im, s5_d, s5_log_dt, s5_w_glu, s5_b_glu, w_out, final_gain):
    p = _prepare_params(ln_gain, w_in, gla_w_gate, gla_b_gate, gla_norm_gain, mla_q_norm_gain, mla_w_uq,
                        mla_kv_norm_gain, mla_w_ukv, s5_lambda_re, s5_lambda_im, s5_b_re, s5_b_im,
                        s5_c_re, s5_c_im, s5_d, s5_log_dt, s5_w_glu, s5_b_glu, w_out)
    y_p, gla_p, ckv_p, kpe_p, re_p, im_p = _trunk(x_prompt, None, None, None, None, None, p, final_gain)
    y_s, gla_s, ckv_s, kpe_s, re_s, im_s = _trunk(x_sample, state_gla, cache_mla_ckv, cache_mla_kpe,
                                                  state_s5_re, state_s5_im, p, final_gain)
    return (y_p, y_s, gla_p, ckv_p, kpe_p, re_p, im_p, gla_s, ckv_s, kpe_s, re_s, im_s)
```

```python
import functools
import math

import numpy as np
import jax
import jax.numpy as jnp
from jax import lax
from jax.experimental import pallas as pl
from jax.experimental.pallas import tpu as pltpu

F32 = jnp.float32
BF16 = jnp.bfloat16

LANES = 128
D_MODEL = 1024
CHUNK = 64
EPS = 1e-6
GLA_HEADS = 4
GLA_DV = 64
GLA_DK = 32
GLA_WIDTH = GLA_HEADS * GLA_DV
GLA_QK = GLA_HEADS * GLA_DK
GLA_GATE_RANK = 16
GLA_GATE_TAU = 16.0
GLA_SUB = 16
GLA_PLAIN_MAX_DECAY = 60.0
MLA_HEADS = 4
MLA_NOPE_DIM = 64
MLA_ROPE_DIM = 32
MLA_V_DIM = 128
MLA_Q_RANK = 192
MLA_KV_RANK = 128
MLA_WIDTH = MLA_HEADS * MLA_V_DIM
MLA_QK_PAD = 128
MLA_QK_WIDTH = MLA_HEADS * MLA_QK_PAD
ROPE_BASE = 10000.0
S5_GROUPS = 16
S5_GROUP_CH = 16
S5_STATE = 64
S5_WIDTH = S5_GROUPS * S5_GROUP_CH
S5_NSTATE = S5_GROUPS * S5_STATE
S5_BATCH_TILE = 8

PG_COLS = 896
PM_COLS = 1024
PS_COLS = 512
IN_COLS_R = PG_COLS + PM_COLS + PS_COLS

VMEM_LIMIT_BYTES = 48 * 1024 * 1024


def _tiles(bsz, t, past):
    n = bsz * t
    s_len = past + t
    return dict(
        row=min(512, n),
        prep=min(512, t),
        gla_rows=min(512, t), gla_chunk=min(CHUNK, t),
        attn_q=min(256, t), attn_k=256 if past == 0 else s_len,
        s5_rows=min(128, t))


def _cparams(*sem):
    return pltpu.CompilerParams(dimension_semantics=sem, vmem_limit_bytes=VMEM_LIMIT_BYTES)


def _sigmoid(x):
    return 1.0 / (1.0 + jnp.exp(-x))


def _dot(a, b):
    return jnp.dot(a, b, preferred_element_type=F32)


def _dot_t(a, b):
    return lax.dot_general(a, b, (((0,), (0,)), ((), ())), preferred_element_type=F32)


def _const_spec(shape, ngrid):
    zeros = (0,) * len(shape)
    return pl.BlockSpec(shape, lambda *_: zeros)


def _layer_spec(shape, layer):
    zeros = (0,) * len(shape)
    return pl.BlockSpec((None,) + tuple(shape), lambda *_: (layer,) + zeros)


def _inproj_body(x_ref, g_ref, w_ref, og_ref, om_ref, os_ref):
    x = x_ref[...]
    ms = jnp.mean(x * x, axis=-1, keepdims=True)
    h = (x * lax.rsqrt(ms + EPS) * g_ref[...]).astype(BF16)
    og_ref[...] = _dot(h, w_ref[:, 0:PG_COLS]).astype(BF16)
    om_ref[...] = _dot(h, w_ref[:, PG_COLS:PG_COLS + PM_COLS]).astype(BF16)
    os_ref[...] = _dot(h, w_ref[:, PG_COLS + PM_COLS:]).astype(BF16)


def _inproj(x2, gain, w, layer, tm):
    n = x2.shape[0]
    return pl.pallas_call(
        _inproj_body,
        grid=(n // tm,),
        in_specs=[pl.BlockSpec((tm, D_MODEL), lambda i: (i, 0)),
                  _layer_spec((1, D_MODEL), layer),
                  _layer_spec((D_MODEL, IN_COLS_R), layer)],
        out_specs=[pl.BlockSpec((tm, PG_COLS), lambda i: (i, 0)),
                   pl.BlockSpec((tm, PM_COLS), lambda i: (i, 0)),
                   pl.BlockSpec((tm, PS_COLS), lambda i: (i, 0))],
        out_shape=[jax.ShapeDtypeStruct((n, PG_COLS), BF16),
                   jax.ShapeDtypeStruct((n, PM_COLS), BF16),
                   jax.ShapeDtypeStruct((n, PS_COLS), BF16)],
        compiler_params=_cparams("parallel"),
        name="inproj",
    )(x2, gain, w)


def _gla_body(p_ref, wg_ref, bg_ref, gain_ref, tri_ref, ones_k_ref, ones_v_ref, bd_ref, s0_ref,
              o_ref, sout_ref, q_scr, k_scr, v_scr, b_scr, s_scr, qe_scr, qs_scr, ke_scr, vb_scr, o_scr,
              *, chunk, nch):
    it = pl.program_id(1)
    sub = min(GLA_SUB, chunk)
    nsub = chunk // sub

    @pl.when(it == 0)
    def _():
        s_scr[...] = s0_ref[0]

    q_scr[...] = p_ref[:, 0:128].astype(F32) * (GLA_DK ** -0.5)
    k_scr[...] = p_ref[:, 128:256].astype(F32)
    v_scr[...] = p_ref[:, 256:512].astype(F32)
    logit = _dot(p_ref[:, 768:896], wg_ref[...]) + bg_ref[...]
    log_a = (jnp.minimum(logit, 0.0) - jnp.log(1.0 + jnp.exp(-jnp.abs(logit)))) * (1.0 / GLA_GATE_TAU)
    la_hi = log_a.astype(BF16)
    la_lo = (log_a - la_hi.astype(F32)).astype(BF16)
    for c in range(nch):
        rs = slice(c * chunk, (c + 1) * chunk)
        b_scr[rs, :] = _dot(tri_ref[...], la_hi[rs]) + _dot(tri_ref[...], la_lo[rs])

    row = lax.broadcasted_iota(jnp.int32, (chunk, GLA_QK), 0)
    row_in_sub = lax.broadcasted_iota(jnp.int32, (sub, GLA_QK), 0)

    def decay_columns(b_end):
        col = jnp.transpose(jnp.broadcast_to(jnp.exp(b_end), (GLA_QK, GLA_QK)))
        return jnp.concatenate([col, col], axis=1)

    def load_chunk(c):
        r0 = pl.multiple_of(c * chunk, chunk)
        return (r0, q_scr[pl.ds(r0, chunk), :], k_scr[pl.ds(r0, chunk), :], v_scr[pl.ds(r0, chunk), :],
                b_scr[pl.ds(r0, chunk), :], b_scr[pl.ds(r0 + chunk - 1, 1), :], s_scr[...])

    def finish_chunk(r0, o, s_prev, a_state, b_end):
        ms = _dot((o * o).astype(BF16), ones_v_ref[...]) * (1.0 / GLA_DV)
        o_n = o * lax.rsqrt(ms + EPS) * gain_ref[...]
        z = p_ref[pl.ds(r0, chunk), 512:768].astype(F32)
        o_ref[pl.ds(r0, chunk), :] = (o_n * (z * _sigmoid(z))).astype(BF16)
        s_scr[...] = s_prev * decay_columns(b_end) + a_state

    def robust_chunk(c, carry):
        r0, qc, kc, vc, bc, b_end, s_prev = load_chunk(c)
        xs = [qc * jnp.exp(bc)]
        ks = []
        for sj in range(nsub - 1):
            e_j = b_scr[pl.ds(r0 + (sj + 1) * sub - 1, 1), :]
            later = row >= (sj + 1) * sub
            xs.append(jnp.where(later, qc * jnp.exp(jnp.where(later, bc - e_j, 0.0)), 0.0))
            own = (row >= sj * sub) & (row < (sj + 1) * sub)
            ks.append(jnp.where(own, kc * jnp.exp(jnp.where(own, e_j - bc, 0.0)), 0.0))
        ks.append(kc * jnp.exp(b_end - bc))
        k_all = jnp.concatenate(ks, axis=1).astype(BF16)
        a_all = _dot_t(k_all, vc.astype(BF16)) * bd_ref[...]
        w = jnp.concatenate([s_prev, a_all[:(nsub - 1) * GLA_QK]], axis=0).astype(BF16) if nsub > 1 \
            else s_prev.astype(BF16)
        o_off = _dot(jnp.concatenate(xs, axis=1).astype(BF16), w)
        rows = []
        for si in range(nsub):
            q_i = qc[si * sub:(si + 1) * sub]
            b_i = bc[si * sub:(si + 1) * sub]
            es = []
            for j in range(sub):
                r = r0 + si * sub + j
                b_j = b_scr[pl.ds(r, 1), :]
                k_j = k_scr[pl.ds(r, 1), :]
                valid = row_in_sub >= j
                es.append(q_i * k_j * jnp.exp(jnp.where(valid, b_i - b_j, -jnp.inf)))
            e_all = jnp.concatenate(es, axis=0)
            e_hi = e_all.astype(BF16)
            e_lo = (e_all - e_hi.astype(F32)).astype(BF16)
            p_all = _dot(e_hi, ones_k_ref[...]) + _dot(e_lo, ones_k_ref[...])
            acc = o_off[si * sub:(si + 1) * sub]
            for j in range(sub):
                v_j = v_scr[pl.ds(r0 + si * sub + j, 1), :]
                acc = acc + p_all[j * sub:(j + 1) * sub] * v_j
            rows.append(acc)
        o = rows[0] if nsub == 1 else jnp.concatenate(rows, axis=0)
        finish_chunk(r0, o, s_prev, a_all[(nsub - 1) * GLA_QK:], b_end)
        return carry

    lane_head_v = lax.broadcasted_iota(jnp.int32, (chunk, GLA_WIDTH), 1) // GLA_DV
    causal = (lax.broadcasted_iota(jnp.int32, (GLA_HEADS * chunk, chunk), 0) % chunk
              >= lax.broadcasted_iota(jnp.int32, (GLA_HEADS * chunk, chunk), 1))

    def plain_block():
        tt = nch * chunk
        b_all = b_scr[...]
        q_all = q_scr[...]
        k_all = k_scr[...]
        qe = q_all * jnp.exp(b_all)
        lane_head = lax.broadcasted_iota(jnp.int32, (tt, GLA_QK), 1) // GLA_DK
        qe_scr[...] = qe.astype(BF16)
        for h in range(GLA_HEADS):
            qs_scr[h] = jnp.where(lane_head == h, qe, 0.0).astype(BF16)
        ke_scr[...] = (k_all * jnp.exp(-b_all)).astype(BF16)
        vb_scr[...] = p_ref[:, 256:512]
        s_cur = s_scr[...]
        for c in range(nch):
            rs = slice(c * chunk, (c + 1) * chunk)
            qs = jnp.concatenate([qs_scr[h, rs, :] for h in range(GLA_HEADS)], axis=0)
            s = lax.dot_general(qs, ke_scr[rs, :], (((1,), (1,)), ((), ())), preferred_element_type=F32)
            s = jnp.where(causal, s, 0.0).astype(BF16)
            r = _dot(s, vb_scr[rs, :])
            o = _dot(qe_scr[rs, :], s_cur.astype(BF16))
            for h in range(GLA_HEADS):
                o = o + jnp.where(lane_head_v == h, r[h * chunk:(h + 1) * chunk], 0.0)
            o_scr[rs, :] = o
            b_end = b_scr[(c + 1) * chunk - 1:(c + 1) * chunk, :]
            k_end = (k_scr[rs, :] * jnp.exp(b_end - b_scr[rs, :])).astype(BF16)
            a_state = _dot_t(k_end, vb_scr[rs, :]) * bd_ref[0:GLA_QK, :]
            s_cur = s_cur * decay_columns(b_end) + a_state
        s_scr[...] = s_cur
        o = o_scr[...]
        ms = _dot((o * o).astype(BF16), ones_v_ref[...]) * (1.0 / GLA_DV)
        o_n = o * lax.rsqrt(ms + EPS) * gain_ref[...]
        z = p_ref[:, 512:768].astype(F32)
        o_ref[...] = (o_n * (z * _sigmoid(z))).astype(BF16)

    in_range = jnp.max(-b_scr[...]) < GLA_PLAIN_MAX_DECAY

    @pl.when(in_range)
    def _():
        plain_block()

    @pl.when(jnp.logical_not(in_range))
    def _():
        lax.fori_loop(0, nch, robust_chunk, 0)

    @pl.when(it == pl.num_programs(1) - 1)
    def _():
        sout_ref[0] = s_scr[...]


def _gla_consts(chunk):
    nsub = chunk // min(GLA_SUB, chunk)
    r = np.arange(chunk)
    tri = (r[None, :] <= r[:, None]).astype(np.float32)
    hk = np.arange(GLA_QK) // GLA_DK
    hv = np.arange(GLA_WIDTH) // GLA_DV
    same_kv = (hk[:, None] == hv[None, :]).astype(np.float32)
    same_vv = (hv[:, None] == hv[None, :]).astype(np.float32)
    return (jnp.asarray(tri, BF16), jnp.asarray(same_kv, BF16), jnp.asarray(same_vv, BF16),
            jnp.asarray(np.tile(same_kv, (nsub, 1)), F32))


def _gla(pg, wg, bg, gain, layer, s0_bd, bsz, t, tl):
    chunk, tt = tl['gla_chunk'], tl['gla_rows']
    nch = tt // chunk
    nt = t // tt
    nsub = chunk // min(GLA_SUB, chunk)
    tri, ones_k, ones_v, bd = _gla_consts(chunk)
    const = lambda shape: _const_spec(shape, 2)
    return pl.pallas_call(
        functools.partial(_gla_body, chunk=chunk, nch=nch),
        grid=(bsz, nt),
        in_specs=[pl.BlockSpec((tt, PG_COLS), lambda b, i: (b * nt + i, 0)),
                  _layer_spec((GLA_QK, GLA_QK), layer), _layer_spec((1, GLA_QK), layer),
                  _layer_spec((1, GLA_WIDTH), layer),
                  const((chunk, chunk)), const((GLA_QK, GLA_WIDTH)), const((GLA_WIDTH, GLA_WIDTH)),
                  const((nsub * GLA_QK, GLA_WIDTH)),
                  pl.BlockSpec((1, GLA_QK, GLA_WIDTH), lambda b, i: (b, 0, 0))],
        out_specs=[pl.BlockSpec((tt, GLA_WIDTH), lambda b, i: (b * nt + i, 0)),
                   pl.BlockSpec((1, GLA_QK, GLA_WIDTH), lambda b, i: (b, 0, 0))],
        out_shape=[jax.ShapeDtypeStruct((bsz * t, GLA_WIDTH), BF16),
                   jax.ShapeDtypeStruct((bsz, GLA_QK, GLA_WIDTH), F32)],
        scratch_shapes=[pltpu.VMEM((tt, GLA_QK), F32), pltpu.VMEM((tt, GLA_QK), F32),
                        pltpu.VMEM((tt, GLA_WIDTH), F32), pltpu.VMEM((tt, GLA_QK), F32),
                        pltpu.VMEM((GLA_QK, GLA_WIDTH), F32),
                        pltpu.VMEM((tt, GLA_QK), BF16), pltpu.VMEM((GLA_HEADS, tt, GLA_QK), BF16),
                        pltpu.VMEM((tt, GLA_QK), BF16),
                        pltpu.VMEM((tt, GLA_WIDTH), BF16), pltpu.VMEM((tt, GLA_WIDTH), F32)],
        compiler_params=_cparams("parallel", "arbitrary"),
        name="gla",
    )(pg, wg, bg, gain, tri, ones_k, ones_v, bd, s0_bd)


def _rope128(x, cos_t, sin_t):
    lane = lax.broadcasted_iota(jnp.int32, x.shape, 1)
    first_half = (lane >= MLA_NOPE_DIM) & (lane < MLA_NOPE_DIM + MLA_ROPE_DIM // 2)
    rot = jnp.where(first_half, -pltpu.roll(x, LANES - MLA_ROPE_DIM // 2, 1), pltpu.roll(x, MLA_ROPE_DIM // 2, 1))
    return x * cos_t + rot * sin_t


def _mla_prepq_body(p_ref, cos_ref, sin_ref, gq_ref, wq_ref, gkv_ref, q_ref, ckv_ref, kpe_ref):
    cos_t = cos_ref[...]
    sin_t = sin_ref[...]
    cq = p_ref[:, 0:256].astype(F32)
    ms = jnp.sum(cq * cq, axis=-1, keepdims=True) * (1.0 / MLA_Q_RANK)
    cqn = (cq * lax.rsqrt(ms + EPS) * gq_ref[...]).astype(BF16)
    qh = _dot(cqn, wq_ref[...])
    scale = (MLA_NOPE_DIM + MLA_ROPE_DIM) ** -0.5 * math.log2(math.e)
    for h in range(MLA_HEADS):
        x = qh[:, h * MLA_QK_PAD:(h + 1) * MLA_QK_PAD]
        q_ref[:, h * MLA_QK_PAD:(h + 1) * MLA_QK_PAD] = (_rope128(x, cos_t, sin_t) * scale).astype(BF16)
    ckv = p_ref[:, 256:384].astype(F32)
    ms = jnp.mean(ckv * ckv, axis=-1, keepdims=True)
    ckv_ref[...] = ckv * lax.rsqrt(ms + EPS) * gkv_ref[...]
    kr = p_ref[:, 384:512].astype(F32)
    kpe_ref[...] = _rope128(kr, cos_t, sin_t)


def _mla_prepq(pm, cos_t, sin_t, gq, wq, gkv, layer, t, tm):
    n = pm.shape[0]
    ntab = t // tm
    return pl.pallas_call(
        _mla_prepq_body,
        grid=(n // tm,),
        in_specs=[pl.BlockSpec((tm, 512), lambda i: (i, 0)),
                  pl.BlockSpec((tm, LANES), lambda i: (i % ntab, 0)),
                  pl.BlockSpec((tm, LANES), lambda i: (i % ntab, 0)),
                  _layer_spec((1, 256), layer), _layer_spec((256, MLA_QK_WIDTH), layer),
                  _layer_spec((1, MLA_KV_RANK), layer)],
        out_specs=[pl.BlockSpec((tm, MLA_QK_WIDTH), lambda i: (i, 0)),
                   pl.BlockSpec((tm, MLA_KV_RANK), lambda i: (i, 0)),
                   pl.BlockSpec((tm, LANES), lambda i: (i, 0))],
        out_shape=[jax.ShapeDtypeStruct((n, MLA_QK_WIDTH), BF16),
                   jax.ShapeDtypeStruct((n, MLA_KV_RANK), F32),
                   jax.ShapeDtypeStruct((n, LANES), F32)],
        compiler_params=_cparams("parallel"),
        name="mla_prepq",
    )(pm, cos_t, sin_t, gq, wq, gkv)


def _mla_kv_rows(ckv, kpe128, wkv_ref, k_ref, v_ref, rows):
    kv = _dot(ckv.astype(BF16), wkv_ref[...])
    for h in range(MLA_HEADS):
        sl = slice(h * MLA_QK_PAD, (h + 1) * MLA_QK_PAD)
        k_ref[rows, sl] = (kv[:, sl] + kpe128).astype(BF16)
    v_ref[rows, :] = kv[:, MLA_QK_WIDTH:].astype(BF16)


def _mla_prepkv_body(ckv_ref, kpe_ref, wkv_ref, k_ref, v_ref):
    _mla_kv_rows(ckv_ref[...], kpe_ref[...], wkv_ref, k_ref, v_ref, slice(None))


def _mla_prepkv_past_body(ckv_ref, kpe_ref, ckv_past_ref, kpe_past_ref, place_ref, wkv_ref, k_ref, v_ref, *, past):
    kpe_past = _dot(kpe_past_ref[...].astype(BF16), place_ref[...])
    _mla_kv_rows(ckv_past_ref[...], kpe_past, wkv_ref, k_ref, v_ref, slice(0, past))
    _mla_kv_rows(ckv_ref[...], kpe_ref[...], wkv_ref, k_ref, v_ref, slice(past, None))


def _mla_prepkv(ckv, kpe128, wkv, layer, bsz, t, tm, ckv_past=None, kpe_past=None):
    out_shape = lambda rows: [jax.ShapeDtypeStruct((rows, MLA_QK_WIDTH), BF16),
                              jax.ShapeDtypeStruct((rows, MLA_WIDTH), BF16)]
    if ckv_past is None:
        n = bsz * t
        return pl.pallas_call(
            _mla_prepkv_body,
            grid=(n // tm,),
            in_specs=[pl.BlockSpec((tm, MLA_KV_RANK), lambda i: (i, 0)),
                      pl.BlockSpec((tm, LANES), lambda i: (i, 0)),
                      _layer_spec((MLA_KV_RANK, MLA_QK_WIDTH + MLA_WIDTH), layer)],
            out_specs=[pl.BlockSpec((tm, MLA_QK_WIDTH), lambda i: (i, 0)),
                       pl.BlockSpec((tm, MLA_WIDTH), lambda i: (i, 0))],
            out_shape=out_shape(n),
            compiler_params=_cparams("parallel"),
            name="mla_prepkv",
        )(ckv, kpe128, wkv)
    past = ckv_past.shape[2]
    s_len = past + t
    place = np.zeros((MLA_ROPE_DIM, LANES), np.float32)
    place[np.arange(MLA_ROPE_DIM), MLA_NOPE_DIM + np.arange(MLA_ROPE_DIM)] = 1.0
    return pl.pallas_call(
        functools.partial(_mla_prepkv_past_body, past=past),
        grid=(bsz,),
        in_specs=[pl.BlockSpec((t, MLA_KV_RANK), lambda b: (b, 0)),
                  pl.BlockSpec((t, LANES), lambda b: (b, 0)),
                  pl.BlockSpec((None, None, past, MLA_KV_RANK), lambda b: (layer, b, 0, 0)),
                  pl.BlockSpec((None, None, past, MLA_ROPE_DIM), lambda b: (layer, b, 0, 0)),
                  _const_spec((MLA_ROPE_DIM, LANES), 1),
                  _layer_spec((MLA_KV_RANK, MLA_QK_WIDTH + MLA_WIDTH), layer)],
        out_specs=[pl.BlockSpec((s_len, MLA_QK_WIDTH), lambda b: (b, 0)),
                   pl.BlockSpec((s_len, MLA_WIDTH), lambda b: (b, 0))],
        out_shape=out_shape(bsz * s_len),
        compiler_params=_cparams("parallel"),
        name="mla_prepkv_past",
    )(ckv, kpe128, ckv_past, kpe_past, jnp.asarray(place, BF16), wkv)


def _attn_body(q_ref, k_ref, v_ref, z_ref, o_ref, m_scr, acc_scr, *, past, tq, tk, s_len):
    iq = pl.program_id(1)
    q_first = past + iq * tq
    full_keys = jnp.minimum((q_first // CHUNK + 1) * CHUNK, s_len)
    vis_keys = jnp.minimum(((q_first + tq - 1) // CHUNK + 1) * CHUNK, s_len)
    n_full = full_keys // tk
    n_vis = (vis_keys + tk - 1) // tk
    ones_v = jnp.ones((tk, MLA_V_DIM), BF16)
    ntile, rem = tk // LANES, tk % LANES

    m_scr[...] = jnp.full(m_scr.shape, -jnp.inf, F32)
    acc_scr[...] = jnp.zeros(acc_scr.shape, F32)

    def block(kb, carry, masked):
        k0 = pl.multiple_of(kb * tk, tk)
        if masked:
            q_chunk = (q_first + lax.broadcasted_iota(jnp.int32, (tq, tk), 0)) // CHUNK
            k_chunk = (k0 + lax.broadcasted_iota(jnp.int32, (tq, tk), 1)) // CHUNK
            visible = k_chunk <= q_chunk
        for h in range(MLA_HEADS):
            sl = slice(h * MLA_QK_PAD, (h + 1) * MLA_QK_PAD)
            vsl = slice(h * MLA_V_DIM, (h + 1) * MLA_V_DIM)
            s = lax.dot_general(q_ref[:, sl], k_ref[pl.ds(k0, tk), sl], (((1,), (1,)), ((), ())),
                                preferred_element_type=F32)
            if masked:
                s = jnp.where(visible, s, -jnp.inf)
            m_prev = m_scr[h]
            m_new = jnp.maximum(m_prev, jnp.max(s, axis=-1, keepdims=True))
            alpha = jnp.exp2(m_prev - m_new)
            ps = [jnp.exp2(s[:, c * LANES:(c + 1) * LANES] - m_new) for c in range(ntile)]
            if rem:
                ps.append(jnp.exp2(s[:, ntile * LANES:] - m_new[:, :rem]))
            p = jnp.concatenate(ps, axis=1).astype(BF16)
            v_ext = jnp.concatenate([v_ref[pl.ds(k0, tk), vsl], ones_v], axis=1)
            acc_scr[h] = jnp.concatenate([alpha, alpha], axis=1) * acc_scr[h] + _dot(p, v_ext)
            m_scr[h] = m_new
        return carry

    lax.fori_loop(0, n_full, functools.partial(block, masked=False), 0)
    lax.fori_loop(n_full, n_vis, functools.partial(block, masked=True), 0)
    for h in range(MLA_HEADS):
        vsl = slice(h * MLA_V_DIM, (h + 1) * MLA_V_DIM)
        z = z_ref[:, vsl].astype(F32)
        acc = acc_scr[h]
        o_ref[:, vsl] = (acc[:, :MLA_V_DIM] / acc[:, MLA_V_DIM:] * (z * _sigmoid(z))).astype(BF16)


def _attn(q, k, v, pm, bsz, t, s_len, past, tq, tk):
    nq = t // tq
    return pl.pallas_call(
        functools.partial(_attn_body, past=past, tq=tq, tk=tk, s_len=s_len),
        grid=(bsz, nq),
        in_specs=[pl.BlockSpec((tq, MLA_QK_WIDTH), lambda b, iq: (b * nq + iq, 0)),
                  pl.BlockSpec((s_len, MLA_QK_WIDTH), lambda b, iq: (b, 0)),
                  pl.BlockSpec((s_len, MLA_WIDTH), lambda b, iq: (b, 0)),
                  pl.BlockSpec((tq, MLA_WIDTH), lambda b, iq: (b * nq + iq, 1))],
        out_specs=pl.BlockSpec((tq, MLA_WIDTH), lambda b, iq: (b * nq + iq, 0)),
        out_shape=jax.ShapeDtypeStruct((bsz * t, MLA_WIDTH), BF16),
        scratch_shapes=[pltpu.VMEM((MLA_HEADS, tq, LANES), F32), pltpu.VMEM((MLA_HEADS, tq, 2 * MLA_V_DIM), F32)],
        compiler_params=_cparams("parallel", "arbitrary"),
        name="mla_attn",
    )(q, k, v, pm)


def _s5_body(p_ref, x0r_ref, x0i_ref, lre_ref, lim_ref, wb_ref, wc_ref, d_ref, wglu_ref, bglu_ref,
             o_ref, xr_out, xi_out, uz_bt, uz_tb, bu, o_tb, xr_s, xi_s, *, lc, pitch):
    it = pl.program_id(1)
    nb = S5_BATCH_TILE

    @pl.when(it == 0)
    def _():
        xr_s[...] = x0r_ref[...]
        xi_s[...] = x0i_ref[...]

    for b in range(nb):
        for c in range(PS_COLS // LANES):
            uz_bt[c, b * pitch:b * pitch + lc, :] = p_ref[b, :, c * LANES:(c + 1) * LANES].astype(F32)
    for t in range(lc):
        for c in range(PS_COLS // LANES):
            uz_tb[t * nb:(t + 1) * nb, c * LANES:(c + 1) * LANES] = uz_bt[c, pl.ds(t, nb, stride=pitch), :]
    u = uz_tb[:, 0:S5_WIDTH]
    bu[...] = _dot(u.astype(BF16), wb_ref[...])

    lre = jnp.broadcast_to(lre_ref[...], (nb, S5_NSTATE))
    lim = jnp.broadcast_to(lim_ref[...], (nb, S5_NSTATE))

    def step(t, carry):
        xr, xi = carry
        r0 = pl.multiple_of(t * nb, nb)
        nr = lre * xr - lim * xi + bu[pl.ds(r0, nb), 0:S5_NSTATE]
        ni = lre * xi + lim * xr + bu[pl.ds(r0, nb), S5_NSTATE:2 * S5_NSTATE]
        bu[pl.ds(r0, nb), 0:S5_NSTATE] = nr
        bu[pl.ds(r0, nb), S5_NSTATE:2 * S5_NSTATE] = ni
        return nr, ni

    xr, xi = lax.fori_loop(0, lc, step, (xr_s[...], xi_s[...]))
    xr_s[...] = xr
    xi_s[...] = xi

    y = _dot(bu[...].astype(BF16), wc_ref[...]) + d_ref[...] * u
    g5 = 0.5 * y * (1.0 + jnp.tanh(0.7978845608028654 * (y + 0.044715 * (y * y * y))))
    gate = _sigmoid(_dot(g5.astype(BF16), wglu_ref[...]) + bglu_ref[...])
    z = uz_tb[:, S5_WIDTH:2 * S5_WIDTH]
    o = g5 * gate * (z * _sigmoid(z))
    for c in range(S5_WIDTH // LANES):
        o_tb[c] = o[:, c * LANES:(c + 1) * LANES]
    for b in range(nb):
        for c in range(S5_WIDTH // LANES):
            o_ref[b, :, c * LANES:(c + 1) * LANES] = o_tb[c, pl.ds(b, lc, stride=nb), :].astype(BF16)

    @pl.when(it == pl.num_programs(1) - 1)
    def _():
        xr_out[...] = xr
        xi_out[...] = xi


def _s5(ps3, x0r, x0i, lre, lim, wb, wc, d, wglu, bglu, layer, bsz, t, lc):
    nb = S5_BATCH_TILE
    nt = t // lc
    pitch = lc + 8
    return pl.pallas_call(
        functools.partial(_s5_body, lc=lc, pitch=pitch),
        grid=(bsz // nb, nt),
        in_specs=[pl.BlockSpec((nb, lc, PS_COLS), lambda g, i: (g, i, 0)),
                  pl.BlockSpec((nb, S5_NSTATE), lambda g, i: (g, 0)),
                  pl.BlockSpec((nb, S5_NSTATE), lambda g, i: (g, 0)),
                  _layer_spec((1, S5_NSTATE), layer), _layer_spec((1, S5_NSTATE), layer),
                  _layer_spec((S5_WIDTH, 2 * S5_NSTATE), layer), _layer_spec((2 * S5_NSTATE, S5_WIDTH), layer),
                  _layer_spec((1, S5_WIDTH), layer), _layer_spec((S5_WIDTH, S5_WIDTH), layer),
                  _layer_spec((1, S5_WIDTH), layer)],
        out_specs=[pl.BlockSpec((nb, lc, S5_WIDTH), lambda g, i: (g, i, 0)),
                   pl.BlockSpec((nb, S5_NSTATE), lambda g, i: (g, 0)),
                   pl.BlockSpec((nb, S5_NSTATE), lambda g, i: (g, 0))],
        out_shape=[jax.ShapeDtypeStruct((bsz, t, S5_WIDTH), BF16),
                   jax.ShapeDtypeStruct((bsz, S5_NSTATE), F32),
                   jax.ShapeDtypeStruct((bsz, S5_NSTATE), F32)],
        scratch_shapes=[pltpu.VMEM((PS_COLS // LANES, nb * pitch, LANES), F32), pltpu.VMEM((lc * nb, PS_COLS), F32),
                        pltpu.VMEM((lc * nb, 2 * S5_NSTATE), F32), pltpu.VMEM((S5_WIDTH // LANES, lc * nb, LANES), F32),
                        pltpu.VMEM((nb, S5_NSTATE), F32), pltpu.VMEM((nb, S5_NSTATE), F32)],
        compiler_params=_cparams("parallel", "arbitrary"),
        name="s5",
    )(ps3, x0r, x0i, lre, lim, wb, wc, d, wglu, bglu)


def _outproj_body(x_ref, og_ref, om_ref, os_ref, w_ref, g_ref, o_ref, *, final):
    acc = _dot(og_ref[...], w_ref[0:GLA_WIDTH, :])
    acc += _dot(om_ref[...], w_ref[GLA_WIDTH:GLA_WIDTH + MLA_WIDTH, :])
    acc += _dot(os_ref[...], w_ref[GLA_WIDTH + MLA_WIDTH:, :])
    xn = x_ref[...] + acc
    if final:
        ms = jnp.mean(xn * xn, axis=-1, keepdims=True)
        xn = xn * lax.rsqrt(ms + EPS) * g_ref[...]
    o_ref[...] = xn


def _outproj(x2, og, om, os_, w, gain, layer, tm, final):
    n = x2.shape[0]
    row = lambda c: pl.BlockSpec((tm, c), lambda i: (i, 0))
    return pl.pallas_call(
        functools.partial(_outproj_body, final=final),
        grid=(n // tm,),
        in_specs=[row(D_MODEL), row(GLA_WIDTH), row(MLA_WIDTH), row(S5_WIDTH),
                  _layer_spec((D_MODEL, D_MODEL), layer),
                  _const_spec((1, D_MODEL), 1)],
        out_specs=row(D_MODEL),
        out_shape=jax.ShapeDtypeStruct((n, D_MODEL), F32),
        compiler_params=_cparams("parallel"),
        name="outproj_final" if final else "outproj",
    )(x2, og, om, os_, w, gain)


def _prepare_params(ln_gain, w_in, gla_w_gate, gla_b_gate, gla_norm_gain, mla_q_norm_gain, mla_w_uq,
                    mla_kv_norm_gain, mla_w_ukv, s5_lambda_re, s5_lambda_im, s5_b_re, s5_b_im, s5_c_re, s5_c_im,
                    s5_d, s5_log_dt, s5_w_glu, s5_b_glu, w_out):
    depth = w_in.shape[0]
    zc = lambda n: jnp.zeros((depth, D_MODEL, n), F32)
    off = np.cumsum((0, 128, 128, 256, 16, 256, 192, 128, 32, 512, 256, 256))
    g_q, g_k, g_v, g_lr, g_z, m_cq, m_ckv, m_kr, m_z, s_u, s_z = [w_in[:, :, off[i]:off[i + 1]] for i in range(11)]
    w_r = jnp.concatenate([g_q, g_k, g_v, g_z, g_lr, zc(112),
                           m_cq, zc(64), m_ckv, zc(64), m_kr, zc(32), m_z,
                           s_u, s_z], axis=2).astype(BF16)
    wg = jnp.pad(gla_w_gate, ((0, 0), (0, GLA_QK - GLA_GATE_RANK), (0, 0))).astype(BF16)
    wq = mla_w_uq.reshape(depth, MLA_Q_RANK, MLA_HEADS, MLA_NOPE_DIM + MLA_ROPE_DIM)
    wq = jnp.pad(wq, ((0, 0), (0, 256 - MLA_Q_RANK), (0, 0), (0, MLA_QK_PAD - MLA_NOPE_DIM - MLA_ROPE_DIM)))
    wq = wq.reshape(depth, 256, MLA_QK_WIDTH).astype(BF16)
    gq = jnp.pad(mla_q_norm_gain, ((0, 0), (0, 256 - MLA_Q_RANK))).reshape(depth, 1, 256)
    wkv = mla_w_ukv.reshape(depth, MLA_KV_RANK, MLA_HEADS, MLA_NOPE_DIM + MLA_V_DIM)
    wk = jnp.pad(wkv[..., :MLA_NOPE_DIM], ((0, 0), (0, 0), (0, 0), (0, MLA_QK_PAD - MLA_NOPE_DIM)))
    wkv_r = jnp.concatenate([wk.reshape(depth, MLA_KV_RANK, MLA_QK_WIDTH),
                             wkv[..., MLA_NOPE_DIM:].reshape(depth, MLA_KV_RANK, MLA_WIDTH)], axis=2).astype(BF16)
    dt = jnp.exp(s5_log_dt)[:, :, None]
    mag = jnp.exp(s5_lambda_re * dt)
    lbr, lbi = mag * jnp.cos(s5_lambda_im * dt), mag * jnp.sin(s5_lambda_im * dt)
    den = s5_lambda_re * s5_lambda_re + s5_lambda_im * s5_lambda_im
    qr = ((lbr - 1.0) * s5_lambda_re + lbi * s5_lambda_im) / den
    qi = (lbi * s5_lambda_re - (lbr - 1.0) * s5_lambda_im) / den
    bbr = qr[..., None] * s5_b_re - qi[..., None] * s5_b_im
    bbi = qr[..., None] * s5_b_im + qi[..., None] * s5_b_re
    eye_g = jnp.eye(S5_GROUPS, dtype=F32)
    bd_in = lambda m: (jnp.swapaxes(m, 2, 3)[:, :, :, None, :] * eye_g[None, :, None, :, None]
                       ).reshape(depth, S5_WIDTH, S5_NSTATE)
    bd_out = lambda m: (jnp.swapaxes(m, 2, 3)[:, :, :, None, :] * eye_g[None, :, None, :, None]
                        ).reshape(depth, S5_NSTATE, S5_WIDTH)
    wb = jnp.concatenate([bd_in(bbr), bd_in(bbi)], axis=2).astype(BF16)
    wc = jnp.concatenate([bd_out(s5_c_re), -bd_out(s5_c_im)], axis=1).astype(BF16)
    return dict(
        ln=ln_gain.reshape(depth, 1, D_MODEL), w_in=w_r, wg=wg, bg=gla_b_gate.reshape(depth, 1, GLA_QK),
        gla_gain=jnp.tile(gla_norm_gain, (1, GLA_HEADS)).reshape(depth, 1, GLA_WIDTH),
        gq=gq, wq=wq, gkv=mla_kv_norm_gain.reshape(depth, 1, MLA_KV_RANK), wkv=wkv_r,
        lre=lbr.reshape(depth, 1, S5_NSTATE), lim=lbi.reshape(depth, 1, S5_NSTATE),
        wb=wb, wc=wc, d=s5_d.reshape(depth, 1, S5_WIDTH), wglu=s5_w_glu.astype(BF16),
        bglu=s5_b_glu.reshape(depth, 1, S5_WIDTH), w_out=w_out.astype(BF16))


def _rope_tables(past, t):
    half = MLA_ROPE_DIM // 2
    inv = ROPE_BASE ** (-jnp.arange(half, dtype=F32) / half)
    ang = (past + jnp.arange(t)).astype(F32)[:, None] * inv[None, :]
    cos, sin = jnp.cos(ang), jnp.sin(ang)
    pad = MLA_QK_PAD - MLA_NOPE_DIM - MLA_ROPE_DIM
    cos_t = jnp.concatenate([jnp.ones((t, MLA_NOPE_DIM), F32), cos, cos, jnp.zeros((t, pad), F32)], axis=1)
    sin_t = jnp.concatenate([jnp.zeros((t, MLA_NOPE_DIM), F32), sin, sin, jnp.zeros((t, pad), F32)], axis=1)
    return cos_t, sin_t


def _gla_state_to_bd(s):
    eye_h = jnp.eye(GLA_HEADS, dtype=s.dtype)
    return (s[:, :, :, None, :] * eye_h[None, :, None, :, None]).reshape(s.shape[0], GLA_QK, GLA_WIDTH)


def _gla_state_from_bd(s_bd):
    s5 = s_bd.reshape(s_bd.shape[0], GLA_HEADS, GLA_DK, GLA_HEADS, GLA_DV)
    return jnp.stack([s5[:, h, :, h, :] for h in range(GLA_HEADS)], axis=1)


def _trunk(x, gla_state, ckv_cache, kpe_cache, s5_re, s5_im, p, final_gain):
    bsz, t, _ = x.shape
    n = bsz * t
    depth = p['w_in'].shape[0]
    past = 0 if ckv_cache is None else ckv_cache.shape[2]
    s_len = past + t
    tl = _tiles(bsz, t, past)
    cos_t, sin_t = _rope_tables(past, t)
    x2 = x.reshape(n, D_MODEL)
    gain_f = final_gain.reshape(1, D_MODEL)
    gla_o, ckv_o, kpe_o, re_o, im_o = [], [], [], [], []
    for l in range(depth):
        pg, pm, ps = _inproj(x2, p['ln'], p['w_in'], l, tl['row'])
        s0 = (jnp.zeros((bsz, GLA_QK, GLA_WIDTH), F32) if gla_state is None else _gla_state_to_bd(gla_state[l]))
        o_gla, s_bd = _gla(pg, p['wg'], p['bg'], p['gla_gain'], l, s0, bsz, t, tl)
        gla_o.append(_gla_state_from_bd(s_bd))
        q, ckv_new, kpe128 = _mla_prepq(pm, cos_t, sin_t, p['gq'], p['wq'], p['gkv'], l, t, tl['prep'])
        ckv_o.append(ckv_new.reshape(bsz, t, MLA_KV_RANK))
        kpe_o.append(kpe128[:, MLA_NOPE_DIM:MLA_NOPE_DIM + MLA_ROPE_DIM].reshape(bsz, t, MLA_ROPE_DIM))
        if past == 0:
            k_cat, v_all = _mla_prepkv(ckv_new, kpe128, p['wkv'], l, bsz, t, tl['prep'])
        else:
            k_cat, v_all = _mla_prepkv(ckv_new, kpe128, p['wkv'], l, bsz, t, tl['prep'], ckv_cache, kpe_cache)
        o_mla = _attn(q, k_cat, v_all, pm, bsz, t, s_len, past, tl['attn_q'], tl['attn_k'])
        x0r = jnp.zeros((bsz, S5_NSTATE), F32) if s5_re is None else s5_re[l].reshape(bsz, S5_NSTATE)
        x0i = jnp.zeros((bsz, S5_NSTATE), F32) if s5_im is None else s5_im[l].reshape(bsz, S5_NSTATE)
        o_s5, xr, xi = _s5(ps.reshape(bsz, t, PS_COLS), x0r, x0i, p['lre'], p['lim'], p['wb'], p['wc'],
                           p['d'], p['wglu'], p['bglu'], l, bsz, t, tl['s5_rows'])
        re_o.append(xr.reshape(bsz, S5_GROUPS, S5_STATE))
        im_o.append(xi.reshape(bsz, S5_GROUPS, S5_STATE))
        x2 = _outproj(x2, o_gla, o_mla, o_s5.reshape(n, S5_WIDTH), p['w_out'], gain_f, l,
                      tl['row'], final=(l == depth - 1))
    return (x2.reshape(bsz, t, D_MODEL), jnp.stack(gla_o), jnp.stack(ckv_o), jnp.stack(kpe_o),
            jnp.stack(re_o), jnp.stack(im_o))


def kernel(x_prompt, x_sample, state_gla, cache_mla_ckv, cache_mla_kpe, state_s5_re, state_s5_im, ln_gain, w_in, gla_w_gate, gla_b_gate, gla_norm_gain, mla_q_norm_gain, mla_w_uq, mla_kv_norm_gain, mla_w_ukv, s5_lambda_re, s5_lambda_im, s5_b_re, s5_b_im, s5_c_re, s5_c_im, s5_d, s5_log_dt, s5_w_glu, s5_b_glu, w_out, final_gain):
    p = _prepare_params(ln_gain, w_in, gla_w_gate, gla_b_gate, gla_norm_gain, mla_q_norm_gain, mla_w_uq,
                        mla_kv_norm_gain, mla_w_ukv, s5_lambda_re, s5_lambda_im, s5_b_re, s5_b_im,
                        s5_c_re, s5_c_im, s5_d, s5_log_dt, s5_w_glu, s5_b_glu, w_out)
    y_p, gla_p, ckv_p, kpe_p, re_p, im_p = _trunk(x_prompt, None, None, None, None, None, p, final_gain)
    y_s, gla_s, ckv_s, kpe_s, re_s, im_s = _trunk(x_sample, state_gla, cache_mla_ckv, cache_mla_kpe,
                                                  state_s5_re, state_s5_im, p, final_gain)
    return (y_p, y_s, gla_p, ckv_p, kpe_p, re_p, im_p, gla_s, ckv_s, kpe_s, re_s, im_s)
```

```python
import functools
import math

import numpy as np
import jax
import jax.numpy as jnp
from jax import lax
from jax.experimental import pallas as pl
from jax.experimental.pallas import tpu as pltpu

F32 = jnp.float32
BF16 = jnp.bfloat16

LANES = 128
D_MODEL = 1024
CHUNK = 64
EPS = 1e-6
GLA_HEADS = 4
GLA_DV = 64
GLA_DK = 32
GLA_WIDTH = GLA_HEADS * GLA_DV
GLA_QK = GLA_HEADS * GLA_DK
GLA_GATE_RANK = 16
GLA_GATE_TAU = 16.0
GLA_SUB = 16
GLA_PLAIN_MAX_DECAY = 60.0
MLA_HEADS = 4
MLA_NOPE_DIM = 64
MLA_ROPE_DIM = 32
MLA_V_DIM = 128
MLA_Q_RANK = 192
MLA_KV_RANK = 128
MLA_WIDTH = MLA_HEADS * MLA_V_DIM
MLA_QK_PAD = 128
MLA_QK_WIDTH = MLA_HEADS * MLA_QK_PAD
ROPE_BASE = 10000.0
S5_GROUPS = 16
S5_GROUP_CH = 16
S5_STATE = 64
S5_WIDTH = S5_GROUPS * S5_GROUP_CH
S5_NSTATE = S5_GROUPS * S5_STATE
S5_BATCH_TILE = 8

PG_COLS = 896
PM_COLS = 1024
PS_COLS = 512
IN_COLS_R = PG_COLS + PM_COLS + PS_COLS

VMEM_LIMIT_BYTES = 48 * 1024 * 1024


def _tiles(bsz, t, past):
    n = bsz * t
    s_len = past + t
    return dict(
        row=min(512, n),
        prep=min(512, t),
        gla_rows=min(512, t), gla_chunk=min(CHUNK, t),
        attn_q=min(512, t), attn_k=min(512, s_len),
        s5_rows=min(128, t))


def _cparams(*sem):
    return pltpu.CompilerParams(dimension_semantics=sem, vmem_limit_bytes=VMEM_LIMIT_BYTES)


def _sigmoid(x):
    return 1.0 / (1.0 + jnp.exp(-x))


def _dot(a, b):
    return jnp.dot(a, b, preferred_element_type=F32)


def _dot_t(a, b):
    return lax.dot_general(a, b, (((0,), (0,)), ((), ())), preferred_element_type=F32)


def _const_spec(shape, ngrid):
    zeros = (0,) * len(shape)
    return pl.BlockSpec(shape, lambda *_: zeros)


def _layer_spec(shape, layer):
    zeros = (0,) * len(shape)
    return pl.BlockSpec((None,) + tuple(shape), lambda *_: (layer,) + zeros)


def _inproj_body(x_ref, g_ref, w_ref, og_ref, om_ref, os_ref):
    x = x_ref[...]
    ms = jnp.mean(x * x, axis=-1, keepdims=True)
    h = (x * lax.rsqrt(ms + EPS) * g_ref[...]).astype(BF16)
    og_ref[...] = _dot(h, w_ref[:, 0:PG_COLS]).astype(BF16)
    om_ref[...] = _dot(h, w_ref[:, PG_COLS:PG_COLS + PM_COLS]).astype(BF16)
    os_ref[...] = _dot(h, w_ref[:, PG_COLS + PM_COLS:]).astype(BF16)


def _inproj(x2, gain, w, layer, tm):
    n = x2.shape[0]
    return pl.pallas_call(
        _inproj_body,
        grid=(n // tm,),
        in_specs=[pl.BlockSpec((tm, D_MODEL), lambda i: (i, 0)),
                  _layer_spec((1, D_MODEL), layer),
                  _layer_spec((D_MODEL, IN_COLS_R), layer)],
        out_specs=[pl.BlockSpec((tm, PG_COLS), lambda i: (i, 0)),
                   pl.BlockSpec((tm, PM_COLS), lambda i: (i, 0)),
                   pl.BlockSpec((tm, PS_COLS), lambda i: (i, 0))],
        out_shape=[jax.ShapeDtypeStruct((n, PG_COLS), BF16),
                   jax.ShapeDtypeStruct((n, PM_COLS), BF16),
                   jax.ShapeDtypeStruct((n, PS_COLS), BF16)],
        compiler_params=_cparams("parallel"),
        name="inproj",
    )(x2, gain, w)


def _gla_body(p_ref, wg_ref, bg_ref, gain_ref, tri_ref, ones_k_ref, ones_v_ref, bd_ref, s0_ref,
              o_ref, sout_ref, q_scr, k_scr, v_scr, b_scr, s_scr, qe_scr, qs_scr, ke_scr, vb_scr, o_scr,
              *, chunk, nch):
    it = pl.program_id(1)
    sub = min(GLA_SUB, chunk)
    nsub = chunk // sub

    @pl.when(it == 0)
    def _():
        s_scr[...] = jnp.zeros(s_scr.shape, F32)
        for h in range(GLA_HEADS):
            s_scr[h * GLA_DK:(h + 1) * GLA_DK, h * GLA_DV:(h + 1) * GLA_DV] = s0_ref[0, h]

    q_scr[...] = p_ref[:, 0:128].astype(F32) * (GLA_DK ** -0.5)
    k_scr[...] = p_ref[:, 128:256].astype(F32)
    v_scr[...] = p_ref[:, 256:512].astype(F32)
    logit = _dot(p_ref[:, 768:896], wg_ref[...]) + bg_ref[...]
    log_a = (jnp.minimum(logit, 0.0) - jnp.log(1.0 + jnp.exp(-jnp.abs(logit)))) * (1.0 / GLA_GATE_TAU)
    la_hi = log_a.astype(BF16)
    la_lo = (log_a - la_hi.astype(F32)).astype(BF16)
    for c in range(nch):
        rs = slice(c * chunk, (c + 1) * chunk)
        b_scr[rs, :] = _dot(tri_ref[...], la_hi[rs]) + _dot(tri_ref[...], la_lo[rs])

    row = lax.broadcasted_iota(jnp.int32, (chunk, GLA_QK), 0)
    row_in_sub = lax.broadcasted_iota(jnp.int32, (sub, GLA_QK), 0)

    def decay_columns(b_end):
        col = jnp.transpose(jnp.broadcast_to(jnp.exp(b_end), (GLA_QK, GLA_QK)))
        return jnp.concatenate([col, col], axis=1)

    def load_chunk(c):
        r0 = pl.multiple_of(c * chunk, chunk)
        return (r0, q_scr[pl.ds(r0, chunk), :], k_scr[pl.ds(r0, chunk), :], v_scr[pl.ds(r0, chunk), :],
                b_scr[pl.ds(r0, chunk), :], b_scr[pl.ds(r0 + chunk - 1, 1), :], s_scr[...])

    def finish_chunk(r0, o, s_prev, a_state, b_end):
        ms = _dot((o * o).astype(BF16), ones_v_ref[...]) * (1.0 / GLA_DV)
        o_n = o * lax.rsqrt(ms + EPS) * gain_ref[...]
        z = p_ref[pl.ds(r0, chunk), 512:768].astype(F32)
        o_ref[pl.ds(r0, chunk), :] = (o_n * (z * _sigmoid(z))).astype(BF16)
        s_scr[...] = s_prev * decay_columns(b_end) + a_state

    def robust_chunk(c, carry):
        r0, qc, kc, vc, bc, b_end, s_prev = load_chunk(c)
        xs = [qc * jnp.exp(bc)]
        ks = []
        for sj in range(nsub - 1):
            e_j = b_scr[pl.ds(r0 + (sj + 1) * sub - 1, 1), :]
            later = row >= (sj + 1) * sub
            xs.append(jnp.where(later, qc * jnp.exp(jnp.where(later, bc - e_j, 0.0)), 0.0))
            own = (row >= sj * sub) & (row < (sj + 1) * sub)
            ks.append(jnp.where(own, kc * jnp.exp(jnp.where(own, e_j - bc, 0.0)), 0.0))
        ks.append(kc * jnp.exp(b_end - bc))
        k_all = jnp.concatenate(ks, axis=1).astype(BF16)
        a_all = _dot_t(k_all, vc.astype(BF16)) * bd_ref[...]
        w = jnp.concatenate([s_prev, a_all[:(nsub - 1) * GLA_QK]], axis=0).astype(BF16) if nsub > 1 \
            else s_prev.astype(BF16)
        o_off = _dot(jnp.concatenate(xs, axis=1).astype(BF16), w)
        rows = []
        for si in range(nsub):
            q_i = qc[si * sub:(si + 1) * sub]
            b_i = bc[si * sub:(si + 1) * sub]
            es = []
            for j in range(sub):
                r = r0 + si * sub + j
                b_j = b_scr[pl.ds(r, 1), :]
                k_j = k_scr[pl.ds(r, 1), :]
                valid = row_in_sub >= j
                es.append(q_i * k_j * jnp.exp(jnp.where(valid, b_i - b_j, -jnp.inf)))
            e_all = jnp.concatenate(es, axis=0)
            e_hi = e_all.astype(BF16)
            e_lo = (e_all - e_hi.astype(F32)).astype(BF16)
            p_all = _dot(e_hi, ones_k_ref[...]) + _dot(e_lo, ones_k_ref[...])
            acc = o_off[si * sub:(si + 1) * sub]
            for j in range(sub):
                v_j = v_scr[pl.ds(r0 + si * sub + j, 1), :]
                acc = acc + p_all[j * sub:(j + 1) * sub] * v_j
            rows.append(acc)
        o = rows[0] if nsub == 1 else jnp.concatenate(rows, axis=0)
        finish_chunk(r0, o, s_prev, a_all[(nsub - 1) * GLA_QK:], b_end)
        return carry

    lane_head_v = lax.broadcasted_iota(jnp.int32, (chunk, GLA_WIDTH), 1) // GLA_DV
    causal = (lax.broadcasted_iota(jnp.int32, (GLA_HEADS * chunk, chunk), 0) % chunk
              >= lax.broadcasted_iota(jnp.int32, (GLA_HEADS * chunk, chunk), 1))

    def plain_block():
        tt = nch * chunk
        b_all = b_scr[...]
        q_all = q_scr[...]
        k_all = k_scr[...]
        qe = q_all * jnp.exp(b_all)
        lane_head = lax.broadcasted_iota(jnp.int32, (tt, GLA_QK), 1) // GLA_DK
        qe_scr[...] = qe.astype(BF16)
        for h in range(GLA_HEADS):
            qs_scr[h] = jnp.where(lane_head == h, qe, 0.0).astype(BF16)
        ke_scr[...] = (k_all * jnp.exp(-b_all)).astype(BF16)
        vb_scr[...] = p_ref[:, 256:512]
        s_cur = s_scr[...]
        for c in range(nch):
            rs = slice(c * chunk, (c + 1) * chunk)
            qs = jnp.concatenate([qs_scr[h, rs, :] for h in range(GLA_HEADS)], axis=0)
            s = lax.dot_general(qs, ke_scr[rs, :], (((1,), (1,)), ((), ())), preferred_element_type=F32)
            s = jnp.where(causal, s, 0.0).astype(BF16)
            r = _dot(s, vb_scr[rs, :])
            o = _dot(qe_scr[rs, :], s_cur.astype(BF16))
            for h in range(GLA_HEADS):
                o = o + jnp.where(lane_head_v == h, r[h * chunk:(h + 1) * chunk], 0.0)
            o_scr[rs, :] = o
            b_end = b_scr[(c + 1) * chunk - 1:(c + 1) * chunk, :]
            k_end = (k_scr[rs, :] * jnp.exp(b_end - b_scr[rs, :])).astype(BF16)
            a_state = _dot_t(k_end, vb_scr[rs, :]) * bd_ref[0:GLA_QK, :]
            s_cur = s_cur * decay_columns(b_end) + a_state
        s_scr[...] = s_cur
        o = o_scr[...]
        ms = _dot((o * o).astype(BF16), ones_v_ref[...]) * (1.0 / GLA_DV)
        o_n = o * lax.rsqrt(ms + EPS) * gain_ref[...]
        z = p_ref[:, 512:768].astype(F32)
        o_ref[...] = (o_n * (z * _sigmoid(z))).astype(BF16)

    in_range = jnp.max(-b_scr[...]) < GLA_PLAIN_MAX_DECAY

    @pl.when(in_range)
    def _():
        plain_block()

    @pl.when(jnp.logical_not(in_range))
    def _():
        lax.fori_loop(0, nch, robust_chunk, 0)

    @pl.when(it == pl.num_programs(1) - 1)
    def _():
        for h in range(GLA_HEADS):
            sout_ref[0, h] = s_scr[h * GLA_DK:(h + 1) * GLA_DK, h * GLA_DV:(h + 1) * GLA_DV]


def _gla_consts(chunk):
    nsub = chunk // min(GLA_SUB, chunk)
    r = np.arange(chunk)
    tri = (r[None, :] <= r[:, None]).astype(np.float32)
    hk = np.arange(GLA_QK) // GLA_DK
    hv = np.arange(GLA_WIDTH) // GLA_DV
    same_kv = (hk[:, None] == hv[None, :]).astype(np.float32)
    same_vv = (hv[:, None] == hv[None, :]).astype(np.float32)
    return (jnp.asarray(tri, BF16), jnp.asarray(same_kv, BF16), jnp.asarray(same_vv, BF16),
            jnp.asarray(np.tile(same_kv, (nsub, 1)), F32))


def _gla(pg, wg, bg, gain, layer, s0_bd, bsz, t, tl):
    chunk, tt = tl['gla_chunk'], tl['gla_rows']
    nch = tt // chunk
    nt = t // tt
    nsub = chunk // min(GLA_SUB, chunk)
    tri, ones_k, ones_v, bd = _gla_consts(chunk)
    const = lambda shape: _const_spec(shape, 2)
    return pl.pallas_call(
        functools.partial(_gla_body, chunk=chunk, nch=nch),
        grid=(bsz, nt),
        in_specs=[pl.BlockSpec((tt, PG_COLS), lambda b, i: (b * nt + i, 0)),
                  _layer_spec((GLA_QK, GLA_QK), layer), _layer_spec((1, GLA_QK), layer),
                  _layer_spec((1, GLA_WIDTH), layer),
                  const((chunk, chunk)), const((GLA_QK, GLA_WIDTH)), const((GLA_WIDTH, GLA_WIDTH)),
                  const((nsub * GLA_QK, GLA_WIDTH)),
                  pl.BlockSpec((1, GLA_HEADS, GLA_DK, GLA_DV), lambda b, i: (b, 0, 0, 0))],
        out_specs=[pl.BlockSpec((tt, GLA_WIDTH), lambda b, i: (b * nt + i, 0)),
                   pl.BlockSpec((1, GLA_HEADS, GLA_DK, GLA_DV), lambda b, i: (b, 0, 0, 0))],
        out_shape=[jax.ShapeDtypeStruct((bsz * t, GLA_WIDTH), BF16),
                   jax.ShapeDtypeStruct((bsz, GLA_HEADS, GLA_DK, GLA_DV), F32)],
        scratch_shapes=[pltpu.VMEM((tt, GLA_QK), F32), pltpu.VMEM((tt, GLA_QK), F32),
                        pltpu.VMEM((tt, GLA_WIDTH), F32), pltpu.VMEM((tt, GLA_QK), F32),
                        pltpu.VMEM((GLA_QK, GLA_WIDTH), F32),
                        pltpu.VMEM((tt, GLA_QK), BF16), pltpu.VMEM((GLA_HEADS, tt, GLA_QK), BF16),
                        pltpu.VMEM((tt, GLA_QK), BF16),
                        pltpu.VMEM((tt, GLA_WIDTH), BF16), pltpu.VMEM((tt, GLA_WIDTH), F32)],
        compiler_params=_cparams("parallel", "arbitrary"),
        name="gla",
    )(pg, wg, bg, gain, tri, ones_k, ones_v, bd, s0_bd)


def _rope128(x, cos_t, sin_t):
    lane = lax.broadcasted_iota(jnp.int32, x.shape, 1)
    first_half = (lane >= MLA_NOPE_DIM) & (lane < MLA_NOPE_DIM + MLA_ROPE_DIM // 2)
    rot = jnp.where(first_half, -pltpu.roll(x, LANES - MLA_ROPE_DIM // 2, 1), pltpu.roll(x, MLA_ROPE_DIM // 2, 1))
    return x * cos_t + rot * sin_t


def _mla_prepq_body(p_ref, cos_ref, sin_ref, gq_ref, wq_ref, gkv_ref, q_ref, ckv_ref, kpe_ref):
    cos_t = cos_ref[...]
    sin_t = sin_ref[...]
    cq = p_ref[:, 0:256].astype(F32)
    ms = jnp.sum(cq * cq, axis=-1, keepdims=True) * (1.0 / MLA_Q_RANK)
    cqn = (cq * lax.rsqrt(ms + EPS) * gq_ref[...]).astype(BF16)
    qh = _dot(cqn, wq_ref[...])
    scale = (MLA_NOPE_DIM + MLA_ROPE_DIM) ** -0.5 * math.log2(math.e)
    for h in range(MLA_HEADS):
        x = qh[:, h * MLA_QK_PAD:(h + 1) * MLA_QK_PAD]
        q_ref[:, h * MLA_QK_PAD:(h + 1) * MLA_QK_PAD] = (_rope128(x, cos_t, sin_t) * scale).astype(BF16)
    ckv = p_ref[:, 256:384].astype(F32)
    ms = jnp.mean(ckv * ckv, axis=-1, keepdims=True)
    ckv_ref[...] = ckv * lax.rsqrt(ms + EPS) * gkv_ref[...]
    kr = p_ref[:, 384:512].astype(F32)
    kpe_ref[...] = _rope128(kr, cos_t, sin_t)


def _mla_prepq(pm, cos_t, sin_t, gq, wq, gkv, layer, t, tm):
    n = pm.shape[0]
    ntab = t // tm
    return pl.pallas_call(
        _mla_prepq_body,
        grid=(n // tm,),
        in_specs=[pl.BlockSpec((tm, 512), lambda i: (i, 0)),
                  pl.BlockSpec((tm, LANES), lambda i: (i % ntab, 0)),
                  pl.BlockSpec((tm, LANES), lambda i: (i % ntab, 0)),
                  _layer_spec((1, 256), layer), _layer_spec((256, MLA_QK_WIDTH), layer),
                  _layer_spec((1, MLA_KV_RANK), layer)],
        out_specs=[pl.BlockSpec((tm, MLA_QK_WIDTH), lambda i: (i, 0)),
                   pl.BlockSpec((tm, MLA_KV_RANK), lambda i: (i, 0)),
                   pl.BlockSpec((tm, LANES), lambda i: (i, 0))],
        out_shape=[jax.ShapeDtypeStruct((n, MLA_QK_WIDTH), BF16),
                   jax.ShapeDtypeStruct((n, MLA_KV_RANK), F32),
                   jax.ShapeDtypeStruct((n, LANES), F32)],
        compiler_params=_cparams("parallel"),
        name="mla_prepq",
    )(pm, cos_t, sin_t, gq, wq, gkv)


def _mla_kv_rows(ckv, kpe128, wkv_ref, k_ref, v_ref, rows):
    kv = _dot(ckv.astype(BF16), wkv_ref[...])
    for h in range(MLA_HEADS):
        sl = slice(h * MLA_QK_PAD, (h + 1) * MLA_QK_PAD)
        k_ref[rows, sl] = (kv[:, sl] + kpe128).astype(BF16)
    v_ref[rows, :] = kv[:, MLA_QK_WIDTH:].astype(BF16)


def _mla_prepkv_body(ckv_ref, kpe_ref, wkv_ref, k_ref, v_ref):
    _mla_kv_rows(ckv_ref[...], kpe_ref[...], wkv_ref, k_ref, v_ref, slice(None))


def _mla_prepkv(ckv, kpe128, wkv, layer, bsz, t, tm):
    n = bsz * t
    return pl.pallas_call(
        _mla_prepkv_body,
        grid=(n // tm,),
        in_specs=[pl.BlockSpec((tm, MLA_KV_RANK), lambda i: (i, 0)),
                  pl.BlockSpec((tm, LANES), lambda i: (i, 0)),
                  _layer_spec((MLA_KV_RANK, MLA_QK_WIDTH + MLA_WIDTH), layer)],
        out_specs=[pl.BlockSpec((tm, MLA_QK_WIDTH), lambda i: (i, 0)),
                   pl.BlockSpec((tm, MLA_WIDTH), lambda i: (i, 0))],
        out_shape=[jax.ShapeDtypeStruct((n, MLA_QK_WIDTH), BF16), jax.ShapeDtypeStruct((n, MLA_WIDTH), BF16)],
        compiler_params=_cparams("parallel"),
        name="mla_prepkv",
    )(ckv, kpe128, wkv)


def _attn_cached_body(q_ref, z_ref, ckv_new_ref, kpe_new_ref, ckv_past_ref, kpet_past_ref, wka_ref, wv_ref,
                      o_ref, kpet_scr, *, past, t):
    last = (((1,), (1,)), ((), ()))
    heads = [q_ref[:, h * MLA_QK_PAD:(h + 1) * MLA_QK_PAD] for h in range(MLA_HEADS)]
    q_rows = jnp.concatenate(heads, axis=0)
    q_lat = jnp.concatenate([_dot(heads[h], wka_ref[h]) for h in range(MLA_HEADS)], axis=0).astype(BF16)
    c_past = ckv_past_ref[...].astype(BF16)
    c_new = ckv_new_ref[...].astype(BF16)
    kpet_scr[...] = jnp.zeros(kpet_scr.shape, BF16)
    kpet_scr[MLA_NOPE_DIM:MLA_NOPE_DIM + MLA_ROPE_DIM, :] = kpet_past_ref[...].astype(BF16)
    s_past = (lax.dot_general(q_lat, c_past, last, preferred_element_type=F32)
              + _dot(q_rows, kpet_scr[...]))
    s_new = (lax.dot_general(q_lat, c_new, last, preferred_element_type=F32)
             + lax.dot_general(q_rows, kpe_new_ref[...].astype(BF16), last, preferred_element_type=F32))
    if past // CHUNK != (past + t - 1) // CHUNK:
        q_chunk = (past + lax.broadcasted_iota(jnp.int32, s_past.shape, 0) % t) // CHUNK
        s_past = jnp.where(lax.broadcasted_iota(jnp.int32, s_past.shape, 1) // CHUNK <= q_chunk, s_past, -jnp.inf)
        q_chunk = (past + lax.broadcasted_iota(jnp.int32, s_new.shape, 0) % t) // CHUNK
        s_new = jnp.where((past + lax.broadcasted_iota(jnp.int32, s_new.shape, 1)) // CHUNK <= q_chunk,
                          s_new, -jnp.inf)
    m = jnp.maximum(jnp.max(s_past, axis=-1, keepdims=True), jnp.max(s_new, axis=-1, keepdims=True))
    p_past = jnp.exp2(s_past - m)
    p_new = jnp.exp2(s_new - m)
    l = jnp.sum(p_past, axis=-1, keepdims=True) + jnp.sum(p_new, axis=-1, keepdims=True)
    o_lat = ((_dot(p_past.astype(BF16), c_past) + _dot(p_new.astype(BF16), c_new)) / l).astype(BF16)
    for h in range(MLA_HEADS):
        vsl = slice(h * MLA_V_DIM, (h + 1) * MLA_V_DIM)
        z = z_ref[:, vsl].astype(F32)
        o_ref[:, vsl] = (_dot(o_lat[h * t:(h + 1) * t], wv_ref[h]) * (z * _sigmoid(z))).astype(BF16)


def _attn_cached(q, pm, ckv_new, kpe128, ckv_past, kpet_past, wka, wv, layer, bsz, t):
    past = ckv_past.shape[2]
    return pl.pallas_call(
        functools.partial(_attn_cached_body, past=past, t=t),
        grid=(bsz,),
        in_specs=[pl.BlockSpec((t, MLA_QK_WIDTH), lambda b: (b, 0)),
                  pl.BlockSpec((t, MLA_WIDTH), lambda b: (b, 1)),
                  pl.BlockSpec((t, MLA_KV_RANK), lambda b: (b, 0)),
                  pl.BlockSpec((t, LANES), lambda b: (b, 0)),
                  pl.BlockSpec((None, None, past, MLA_KV_RANK), lambda b: (layer, b, 0, 0)),
                  pl.BlockSpec((None, None, MLA_ROPE_DIM, past), lambda b: (layer, b, 0, 0)),
                  _layer_spec((MLA_HEADS, MLA_QK_PAD, MLA_KV_RANK), layer),
                  _layer_spec((MLA_HEADS, MLA_KV_RANK, MLA_V_DIM), layer)],
        out_specs=pl.BlockSpec((t, MLA_WIDTH), lambda b: (b, 0)),
        out_shape=jax.ShapeDtypeStruct((bsz * t, MLA_WIDTH), BF16),
        scratch_shapes=[pltpu.VMEM((MLA_QK_PAD, past), BF16)],
        compiler_params=_cparams("parallel"),
        name="mla_attn_cached",
    )(q, pm, ckv_new, kpe128, ckv_past, kpet_past, wka, wv)


def _attn_body(q_ref, k_ref, v_ref, z_ref, o_ref, m_scr, acc_scr, *, past, tq, tk, s_len):
    iq = pl.program_id(1)
    q_first = past + iq * tq
    full_keys = jnp.minimum((q_first // CHUNK + 1) * CHUNK, s_len)
    vis_keys = jnp.minimum(((q_first + tq - 1) // CHUNK + 1) * CHUNK, s_len)
    n_full = full_keys // tk
    n_vis = (vis_keys + tk - 1) // tk
    ones_v = jnp.ones((tk, MLA_V_DIM), BF16)
    ntile, rem = tk // LANES, tk % LANES

    m_scr[...] = jnp.full(m_scr.shape, -jnp.inf, F32)
    acc_scr[...] = jnp.zeros(acc_scr.shape, F32)

    def block(kb, carry, masked):
        k0 = pl.multiple_of(kb * tk, tk)
        if masked:
            q_chunk = (q_first + lax.broadcasted_iota(jnp.int32, (tq, tk), 0)) // CHUNK
            k_chunk = (k0 + lax.broadcasted_iota(jnp.int32, (tq, tk), 1)) // CHUNK
            visible = k_chunk <= q_chunk
        for h in range(MLA_HEADS):
            sl = slice(h * MLA_QK_PAD, (h + 1) * MLA_QK_PAD)
            vsl = slice(h * MLA_V_DIM, (h + 1) * MLA_V_DIM)
            s = lax.dot_general(q_ref[:, sl], k_ref[pl.ds(k0, tk), sl], (((1,), (1,)), ((), ())),
                                preferred_element_type=F32)
            if masked:
                s = jnp.where(visible, s, -jnp.inf)
            m_prev = m_scr[h]
            m_new = jnp.maximum(m_prev, jnp.max(s, axis=-1, keepdims=True))
            alpha = jnp.exp2(m_prev - m_new)
            ps = [jnp.exp2(s[:, c * LANES:(c + 1) * LANES] - m_new) for c in range(ntile)]
            if rem:
                ps.append(jnp.exp2(s[:, ntile * LANES:] - m_new[:, :rem]))
            p = jnp.concatenate(ps, axis=1).astype(BF16)
            v_ext = jnp.concatenate([v_ref[pl.ds(k0, tk), vsl], ones_v], axis=1)
            acc_scr[h] = jnp.concatenate([alpha, alpha], axis=1) * acc_scr[h] + _dot(p, v_ext)
            m_scr[h] = m_new
        return carry

    lax.fori_loop(0, n_full, functools.partial(block, masked=False), 0)
    lax.fori_loop(n_full, n_vis, functools.partial(block, masked=True), 0)
    for h in range(MLA_HEADS):
        vsl = slice(h * MLA_V_DIM, (h + 1) * MLA_V_DIM)
        z = z_ref[:, vsl].astype(F32)
        acc = acc_scr[h]
        o_ref[:, vsl] = (acc[:, :MLA_V_DIM] / acc[:, MLA_V_DIM:] * (z * _sigmoid(z))).astype(BF16)


def _attn(q, k, v, pm, bsz, t, s_len, past, tq, tk):
    nq = t // tq
    return pl.pallas_call(
        functools.partial(_attn_body, past=past, tq=tq, tk=tk, s_len=s_len),
        grid=(bsz, nq),
        in_specs=[pl.BlockSpec((tq, MLA_QK_WIDTH), lambda b, iq: (b * nq + iq, 0)),
                  pl.BlockSpec((s_len, MLA_QK_WIDTH), lambda b, iq: (b, 0)),
                  pl.BlockSpec((s_len, MLA_WIDTH), lambda b, iq: (b, 0)),
                  pl.BlockSpec((tq, MLA_WIDTH), lambda b, iq: (b * nq + iq, 1))],
        out_specs=pl.BlockSpec((tq, MLA_WIDTH), lambda b, iq: (b * nq + iq, 0)),
        out_shape=jax.ShapeDtypeStruct((bsz * t, MLA_WIDTH), BF16),
        scratch_shapes=[pltpu.VMEM((MLA_HEADS, tq, LANES), F32), pltpu.VMEM((MLA_HEADS, tq, 2 * MLA_V_DIM), F32)],
        compiler_params=_cparams("parallel", "arbitrary"),
        name="mla_attn",
    )(q, k, v, pm)


def _s5_body(p_ref, x0r_ref, x0i_ref, lre_ref, lim_ref, wb_ref, wc_ref, d_ref, wglu_ref, bglu_ref,
             o_ref, xr_out, xi_out, uz_bt, uz_tb, bu, o_tb, xr_s, xi_s, *, lc, pitch):
    it = pl.program_id(1)
    nb = S5_BATCH_TILE

    @pl.when(it == 0)
    def _():
        xr_s[...] = x0r_ref[...]
        xi_s[...] = x0i_ref[...]

    for b in range(nb):
        for c in range(PS_COLS // LANES):
            uz_bt[c, b * pitch:b * pitch + lc, :] = p_ref[b, :, c * LANES:(c + 1) * LANES].astype(F32)
    for t in range(lc):
        for c in range(PS_COLS // LANES):
            uz_tb[t * nb:(t + 1) * nb, c * LANES:(c + 1) * LANES] = uz_bt[c, pl.ds(t, nb, stride=pitch), :]
    u = uz_tb[:, 0:S5_WIDTH]
    bu[...] = _dot(u.astype(BF16), wb_ref[...])

    lre = jnp.broadcast_to(lre_ref[...], (nb, S5_NSTATE))
    lim = jnp.broadcast_to(lim_ref[...], (nb, S5_NSTATE))

    def step(t, carry):
        xr, xi = carry
        r0 = pl.multiple_of(t * nb, nb)
        nr = lre * xr - lim * xi + bu[pl.ds(r0, nb), 0:S5_NSTATE]
        ni = lre * xi + lim * xr + bu[pl.ds(r0, nb), S5_NSTATE:2 * S5_NSTATE]
        bu[pl.ds(r0, nb), 0:S5_NSTATE] = nr
        bu[pl.ds(r0, nb), S5_NSTATE:2 * S5_NSTATE] = ni
        return nr, ni

    xr, xi = lax.fori_loop(0, lc, step, (xr_s[...], xi_s[...]))
    xr_s[...] = xr
    xi_s[...] = xi

    y = _dot(bu[...].astype(BF16), wc_ref[...]) + d_ref[...] * u
    g5 = 0.5 * y * (1.0 + jnp.tanh(0.7978845608028654 * (y + 0.044715 * (y * y * y))))
    gate = _sigmoid(_dot(g5.astype(BF16), wglu_ref[...]) + bglu_ref[...])
    z = uz_tb[:, S5_WIDTH:2 * S5_WIDTH]
    o = g5 * gate * (z * _sigmoid(z))
    for c in range(S5_WIDTH // LANES):
        o_tb[c] = o[:, c * LANES:(c + 1) * LANES]
    for b in range(nb):
        for c in range(S5_WIDTH // LANES):
            o_ref[b, :, c * LANES:(c + 1) * LANES] = o_tb[c, pl.ds(b, lc, stride=nb), :].astype(BF16)

    @pl.when(it == pl.num_programs(1) - 1)
    def _():
        xr_out[...] = xr
        xi_out[...] = xi


def _s5(ps3, x0r, x0i, lre, lim, wb, wc, d, wglu, bglu, layer, bsz, t, lc):
    nb = S5_BATCH_TILE
    nt = t // lc
    pitch = lc + 8
    return pl.pallas_call(
        functools.partial(_s5_body, lc=lc, pitch=pitch),
        grid=(bsz // nb, nt),
        in_specs=[pl.BlockSpec((nb, lc, PS_COLS), lambda g, i: (g, i, 0)),
                  pl.BlockSpec((nb, S5_NSTATE), lambda g, i: (g, 0)),
                  pl.BlockSpec((nb, S5_NSTATE), lambda g, i: (g, 0)),
                  _layer_spec((1, S5_NSTATE), layer), _layer_spec((1, S5_NSTATE), layer),
                  _layer_spec((S5_WIDTH, 2 * S5_NSTATE), layer), _layer_spec((2 * S5_NSTATE, S5_WIDTH), layer),
                  _layer_spec((1, S5_WIDTH), layer), _layer_spec((S5_WIDTH, S5_WIDTH), layer),
                  _layer_spec((1, S5_WIDTH), layer)],
        out_specs=[pl.BlockSpec((nb, lc, S5_WIDTH), lambda g, i: (g, i, 0)),
                   pl.BlockSpec((nb, S5_NSTATE), lambda g, i: (g, 0)),
                   pl.BlockSpec((nb, S5_NSTATE), lambda g, i: (g, 0))],
        out_shape=[jax.ShapeDtypeStruct((bsz, t, S5_WIDTH), BF16),
                   jax.ShapeDtypeStruct((bsz, S5_NSTATE), F32),
                   jax.ShapeDtypeStruct((bsz, S5_NSTATE), F32)],
        scratch_shapes=[pltpu.VMEM((PS_COLS // LANES, nb * pitch, LANES), F32), pltpu.VMEM((lc * nb, PS_COLS), F32),
                        pltpu.VMEM((lc * nb, 2 * S5_NSTATE), F32), pltpu.VMEM((S5_WIDTH // LANES, lc * nb, LANES), F32),
                        pltpu.VMEM((nb, S5_NSTATE), F32), pltpu.VMEM((nb, S5_NSTATE), F32)],
        compiler_params=_cparams("parallel", "arbitrary"),
        name="s5",
    )(ps3, x0r, x0i, lre, lim, wb, wc, d, wglu, bglu)


def _outproj_body(x_ref, og_ref, om_ref, os_ref, w_ref, g_ref, o_ref, *, final):
    acc = _dot(og_ref[...], w_ref[0:GLA_WIDTH, :])
    acc += _dot(om_ref[...], w_ref[GLA_WIDTH:GLA_WIDTH + MLA_WIDTH, :])
    acc += _dot(os_ref[...], w_ref[GLA_WIDTH + MLA_WIDTH:, :])
    xn = x_ref[...] + acc
    if final:
        ms = jnp.mean(xn * xn, axis=-1, keepdims=True)
        xn = xn * lax.rsqrt(ms + EPS) * g_ref[...]
    o_ref[...] = xn


def _outproj(x2, og, om, os_, w, gain, layer, tm, final):
    n = x2.shape[0]
    row = lambda c: pl.BlockSpec((tm, c), lambda i: (i, 0))
    return pl.pallas_call(
        functools.partial(_outproj_body, final=final),
        grid=(n // tm,),
        in_specs=[row(D_MODEL), row(GLA_WIDTH), row(MLA_WIDTH), row(S5_WIDTH),
                  _layer_spec((D_MODEL, D_MODEL), layer),
                  _const_spec((1, D_MODEL), 1)],
        out_specs=row(D_MODEL),
        out_shape=jax.ShapeDtypeStruct((n, D_MODEL), F32),
        compiler_params=_cparams("parallel"),
        name="outproj_final" if final else "outproj",
    )(x2, og, om, os_, w, gain)


def _prepare_params(ln_gain, w_in, gla_w_gate, gla_b_gate, gla_norm_gain, mla_q_norm_gain, mla_w_uq,
                    mla_kv_norm_gain, mla_w_ukv, s5_lambda_re, s5_lambda_im, s5_b_re, s5_b_im, s5_c_re, s5_c_im,
                    s5_d, s5_log_dt, s5_w_glu, s5_b_glu, w_out):
    depth = w_in.shape[0]
    zc = lambda n: jnp.zeros((depth, D_MODEL, n), F32)
    off = np.cumsum((0, 128, 128, 256, 16, 256, 192, 128, 32, 512, 256, 256))
    g_q, g_k, g_v, g_lr, g_z, m_cq, m_ckv, m_kr, m_z, s_u, s_z = [w_in[:, :, off[i]:off[i + 1]] for i in range(11)]
    w_r = jnp.concatenate([g_q, g_k, g_v, g_z, g_lr, zc(112),
                           m_cq, zc(64), m_ckv, zc(64), m_kr, zc(32), m_z,
                           s_u, s_z], axis=2).astype(BF16)
    wg = jnp.pad(gla_w_gate, ((0, 0), (0, GLA_QK - GLA_GATE_RANK), (0, 0))).astype(BF16)
    wq = mla_w_uq.reshape(depth, MLA_Q_RANK, MLA_HEADS, MLA_NOPE_DIM + MLA_ROPE_DIM)
    wq = jnp.pad(wq, ((0, 0), (0, 256 - MLA_Q_RANK), (0, 0), (0, MLA_QK_PAD - MLA_NOPE_DIM - MLA_ROPE_DIM)))
    wq = wq.reshape(depth, 256, MLA_QK_WIDTH).astype(BF16)
    gq = jnp.pad(mla_q_norm_gain, ((0, 0), (0, 256 - MLA_Q_RANK))).reshape(depth, 1, 256)
    wkv = mla_w_ukv.reshape(depth, MLA_KV_RANK, MLA_HEADS, MLA_NOPE_DIM + MLA_V_DIM)
    wk = jnp.pad(wkv[..., :MLA_NOPE_DIM], ((0, 0), (0, 0), (0, 0), (0, MLA_QK_PAD - MLA_NOPE_DIM)))
    wkv_r = jnp.concatenate([wk.reshape(depth, MLA_KV_RANK, MLA_QK_WIDTH),
                             wkv[..., MLA_NOPE_DIM:].reshape(depth, MLA_KV_RANK, MLA_WIDTH)], axis=2).astype(BF16)
    wka = jnp.pad(jnp.transpose(wkv[..., :MLA_NOPE_DIM], (0, 2, 3, 1)),
                  ((0, 0), (0, 0), (0, MLA_QK_PAD - MLA_NOPE_DIM), (0, 0))).astype(BF16)
    wv = jnp.transpose(wkv[..., MLA_NOPE_DIM:], (0, 2, 1, 3)).astype(BF16)
    dt = jnp.exp(s5_log_dt)[:, :, None]
    mag = jnp.exp(s5_lambda_re * dt)
    lbr, lbi = mag * jnp.cos(s5_lambda_im * dt), mag * jnp.sin(s5_lambda_im * dt)
    den = s5_lambda_re * s5_lambda_re + s5_lambda_im * s5_lambda_im
    qr = ((lbr - 1.0) * s5_lambda_re + lbi * s5_lambda_im) / den
    qi = (lbi * s5_lambda_re - (lbr - 1.0) * s5_lambda_im) / den
    bbr = qr[..., None] * s5_b_re - qi[..., None] * s5_b_im
    bbi = qr[..., None] * s5_b_im + qi[..., None] * s5_b_re
    eye_g = jnp.eye(S5_GROUPS, dtype=F32)
    bd_in = lambda m: (jnp.swapaxes(m, 2, 3)[:, :, :, None, :] * eye_g[None, :, None, :, None]
                       ).reshape(depth, S5_WIDTH, S5_NSTATE)
    bd_out = lambda m: (jnp.swapaxes(m, 2, 3)[:, :, :, None, :] * eye_g[None, :, None, :, None]
                        ).reshape(depth, S5_NSTATE, S5_WIDTH)
    wb = jnp.concatenate([bd_in(bbr), bd_in(bbi)], axis=2).astype(BF16)
    wc = jnp.concatenate([bd_out(s5_c_re), -bd_out(s5_c_im)], axis=1).astype(BF16)
    return dict(
        ln=ln_gain.reshape(depth, 1, D_MODEL), w_in=w_r, wg=wg, bg=gla_b_gate.reshape(depth, 1, GLA_QK),
        gla_gain=jnp.tile(gla_norm_gain, (1, GLA_HEADS)).reshape(depth, 1, GLA_WIDTH),
        gq=gq, wq=wq, gkv=mla_kv_norm_gain.reshape(depth, 1, MLA_KV_RANK), wkv=wkv_r, wka=wka, wv=wv,
        lre=lbr.reshape(depth, 1, S5_NSTATE), lim=lbi.reshape(depth, 1, S5_NSTATE),
        wb=wb, wc=wc, d=s5_d.reshape(depth, 1, S5_WIDTH), wglu=s5_w_glu.astype(BF16),
        bglu=s5_b_glu.reshape(depth, 1, S5_WIDTH), w_out=w_out.astype(BF16))


def _rope_tables(past, t):
    half = MLA_ROPE_DIM // 2
    inv = ROPE_BASE ** (-jnp.arange(half, dtype=F32) / half)
    ang = (past + jnp.arange(t)).astype(F32)[:, None] * inv[None, :]
    cos, sin = jnp.cos(ang), jnp.sin(ang)
    pad = MLA_QK_PAD - MLA_NOPE_DIM - MLA_ROPE_DIM
    cos_t = jnp.concatenate([jnp.ones((t, MLA_NOPE_DIM), F32), cos, cos, jnp.zeros((t, pad), F32)], axis=1)
    sin_t = jnp.concatenate([jnp.zeros((t, MLA_NOPE_DIM), F32), sin, sin, jnp.zeros((t, pad), F32)], axis=1)
    return cos_t, sin_t


def _trunk(x, gla_state, ckv_cache, kpe_cache, s5_re, s5_im, p, final_gain):
    bsz, t, _ = x.shape
    n = bsz * t
    depth = p['w_in'].shape[0]
    past = 0 if ckv_cache is None else ckv_cache.shape[2]
    s_len = past + t
    tl = _tiles(bsz, t, past)
    cos_t, sin_t = _rope_tables(past, t)
    kpet_cache = None if kpe_cache is None else jnp.swapaxes(kpe_cache, 2, 3)
    x2 = x.reshape(n, D_MODEL)
    gain_f = final_gain.reshape(1, D_MODEL)
    gla_o, ckv_o, kpe_o, re_o, im_o = [], [], [], [], []
    for l in range(depth):
        pg, pm, ps = _inproj(x2, p['ln'], p['w_in'], l, tl['row'])
        s0 = jnp.zeros((bsz, GLA_HEADS, GLA_DK, GLA_DV), F32) if gla_state is None else gla_state[l]
        o_gla, s_new = _gla(pg, p['wg'], p['bg'], p['gla_gain'], l, s0, bsz, t, tl)
        gla_o.append(s_new)
        q, ckv_new, kpe128 = _mla_prepq(pm, cos_t, sin_t, p['gq'], p['wq'], p['gkv'], l, t, tl['prep'])
        ckv_o.append(ckv_new.reshape(bsz, t, MLA_KV_RANK))
        kpe_o.append(kpe128[:, MLA_NOPE_DIM:MLA_NOPE_DIM + MLA_ROPE_DIM].reshape(bsz, t, MLA_ROPE_DIM))
        if past == 0:
            k_cat, v_all = _mla_prepkv(ckv_new, kpe128, p['wkv'], l, bsz, t, tl['prep'])
            o_mla = _attn(q, k_cat, v_all, pm, bsz, t, s_len, past, tl['attn_q'], tl['attn_k'])
        else:
            o_mla = _attn_cached(q, pm, ckv_new, kpe128, ckv_cache, kpet_cache, p['wka'], p['wv'], l, bsz, t)
        x0r = jnp.zeros((bsz, S5_NSTATE), F32) if s5_re is None else s5_re[l].reshape(bsz, S5_NSTATE)
        x0i = jnp.zeros((bsz, S5_NSTATE), F32) if s5_im is None else s5_im[l].reshape(bsz, S5_NSTATE)
        o_s5, xr, xi = _s5(ps.reshape(bsz, t, PS_COLS), x0r, x0i, p['lre'], p['lim'], p['wb'], p['wc'],
                           p['d'], p['wglu'], p['bglu'], l, bsz, t, tl['s5_rows'])
        re_o.append(xr.reshape(bsz, S5_GROUPS, S5_STATE))
        im_o.append(xi.reshape(bsz, S5_GROUPS, S5_STATE))
        x2 = _outproj(x2, o_gla, o_mla, o_s5.reshape(n, S5_WIDTH), p['w_out'], gain_f, l,
                      tl['row'], final=(l == depth - 1))
    return (x2.reshape(bsz, t, D_MODEL), jnp.stack(gla_o), jnp.stack(ckv_o), jnp.stack(kpe_o),
            jnp.stack(re_o), jnp.stack(im_o))


def kernel(x_prompt, x_sample, state_gla, cache_mla_ckv, cache_mla_kpe, state_s5_re, state_s5_im, ln_gain, w_in, gla_w_gate, gla_b_gate, gla_norm_gain, mla_q_norm_gain, mla_w_uq, mla_kv_norm_gain, mla_w_ukv, s5_lambda_re, s5_lambda_im, s5_b_re, s5_b_im, s5_c_re, s5_c_im, s5_d, s5_log_dt, s5_w_glu, s5_b_glu, w_out, final_gain):
    p = _prepare_params(ln_gain, w_in, gla_w_gate, gla_b_gate, gla_norm_gain, mla_q_norm_gain, mla_w_uq,
                        mla_kv_norm_gain, mla_w_ukv, s5_lambda_re, s5_lambda_im, s5_b_re, s5_b_im,
                        s5_c_re, s5_c_im, s5_d, s5_log_dt, s5_w_glu, s5_b_glu, w_out)
    y_p, gla_p, ckv_p, kpe_p, re_p, im_p = _trunk(x_prompt, None, None, None, None, None, p, final_gain)
    y_s, gla_s, ckv_s, kpe_s, re_s, im_s = _trunk(x_sample, state_gla, cache_mla_ckv, cache_mla_kpe,
                                                  state_s5_re, state_s5_im, p, final_gain)
    return (y_p, y_s, gla_p, ckv_p, kpe_p, re_p, im_p, gla_s, ckv_s, kpe_s, re_s, im_s)
```

```python
import functools
import math

import numpy as np
import jax
import jax.numpy as jnp
from jax import lax
from jax.experimental import pallas as pl
from jax.experimental.pallas import tpu as pltpu

F32 = jnp.float32
BF16 = jnp.bfloat16

LANES = 128
D_MODEL = 1024
CHUNK = 64
EPS = 1e-6
GLA_HEADS = 4
GLA_DV = 64
GLA_DK = 32
GLA_WIDTH = GLA_HEADS * GLA_DV
GLA_QK = GLA_HEADS * GLA_DK
GLA_GATE_RANK = 16
GLA_GATE_TAU = 16.0
GLA_SUB = 16
GLA_PLAIN_MAX_DECAY = 60.0
MLA_HEADS = 4
MLA_NOPE_DIM = 64
MLA_ROPE_DIM = 32
MLA_V_DIM = 128
MLA_Q_RANK = 192
MLA_KV_RANK = 128
MLA_WIDTH = MLA_HEADS * MLA_V_DIM
MLA_QK_PAD = 128
MLA_QK_WIDTH = MLA_HEADS * MLA_QK_PAD
ROPE_BASE = 10000.0
S5_GROUPS = 16
S5_GROUP_CH = 16
S5_STATE = 64
S5_WIDTH = S5_GROUPS * S5_GROUP_CH
S5_NSTATE = S5_GROUPS * S5_STATE
S5_BATCH_TILE = 8

PG_COLS = 896
PM_COLS = 1024
PS_COLS = 512
_IN_SEGS = (('g_q', GLA_QK), ('g_k', GLA_QK), ('g_v', GLA_WIDTH), ('g_lr', GLA_GATE_RANK), ('g_z', GLA_WIDTH),
            ('m_cq', MLA_Q_RANK), ('m_ckv', MLA_KV_RANK), ('m_kr', MLA_ROPE_DIM), ('m_z', MLA_WIDTH),
            ('s_u', S5_WIDTH), ('s_z', S5_WIDTH), ('end', 0))
IN_OFF = dict(zip([n for n, _ in _IN_SEGS], np.cumsum([0] + [w for _, w in _IN_SEGS[:-1]]).tolist()))

VMEM_LIMIT_BYTES = 48 * 1024 * 1024


def _tiles(bsz, t, past):
    n = bsz * t
    s_len = past + t
    return dict(
        row=min(512, n),
        prep=min(512, t),
        gla_rows=min(512, t), gla_chunk=min(CHUNK, t),
        attn_q=min(512, t), attn_k=min(512, s_len),
        s5_rows=min(128, t))


def _cparams(*sem):
    return pltpu.CompilerParams(dimension_semantics=sem, vmem_limit_bytes=VMEM_LIMIT_BYTES)


def _sigmoid(x):
    return 0.5 * (1.0 + jnp.tanh(0.5 * x))


def _dot(a, b):
    return jnp.dot(a, b, preferred_element_type=F32)


def _dot_t(a, b):
    return lax.dot_general(a, b, (((0,), (0,)), ((), ())), preferred_element_type=F32)


def _const_spec(shape, ngrid):
    zeros = (0,) * len(shape)
    return pl.BlockSpec(shape, lambda *_: zeros)


def _layer_spec(shape, layer):
    zeros = (0,) * len(shape)
    return pl.BlockSpec((None,) + tuple(shape), lambda *_: (layer,) + zeros)


def _inproj_body(x_ref, g_ref, wt_ref, og_ref, om_ref, os_ref, w_scr):
    @pl.when(pl.program_id(0) == 0)
    def _():
        w_scr[...] = wt_ref[...].astype(BF16)

    x = x_ref[...]
    ms = jnp.mean(x * x, axis=-1, keepdims=True)
    h = (x * lax.rsqrt(ms + EPS) * g_ref[...]).astype(BF16)

    def seg(a, b):
        return lax.dot_general(h, w_scr[a:b, :], (((1,), (1,)), ((), ())), preferred_element_type=F32)

    lane = lax.broadcasted_iota(jnp.int32, (x.shape[0], LANES), 1)
    lane2 = lax.broadcasted_iota(jnp.int32, (x.shape[0], 2 * LANES), 1)
    c = IN_OFF
    og_ref[:, 0:512] = seg(c['g_q'], c['g_lr']).astype(BF16)
    og_ref[:, 512:768] = seg(c['g_z'], c['m_cq']).astype(BF16)
    og_ref[:, 768:896] = jnp.where(lane < GLA_GATE_RANK, seg(c['g_lr'], c['g_lr'] + LANES), 0.0).astype(BF16)
    om_ref[:, 0:256] = jnp.where(lane2 < MLA_Q_RANK, seg(c['m_cq'], c['m_cq'] + 2 * LANES), 0.0).astype(BF16)
    om_ref[:, 256:384] = seg(c['m_ckv'], c['m_kr']).astype(BF16)
    kr = seg(c['m_kr'] - MLA_NOPE_DIM, c['m_kr'] - MLA_NOPE_DIM + LANES)
    om_ref[:, 384:512] = jnp.where((lane >= MLA_NOPE_DIM) & (lane < MLA_NOPE_DIM + MLA_ROPE_DIM), kr, 0.0).astype(BF16)
    om_ref[:, 512:1024] = seg(c['m_z'], c['s_u']).astype(BF16)
    os_ref[...] = seg(c['s_u'], c['end']).astype(BF16)


def _inproj(x2, gain, wt, layer, tm):
    n = x2.shape[0]
    return pl.pallas_call(
        _inproj_body,
        grid=(n // tm,),
        in_specs=[pl.BlockSpec((tm, D_MODEL), lambda i: (i, 0)),
                  _layer_spec((1, D_MODEL), layer),
                  _layer_spec((IN_OFF['end'], D_MODEL), layer)],
        scratch_shapes=[pltpu.VMEM((IN_OFF['end'], D_MODEL), BF16)],
        out_specs=[pl.BlockSpec((tm, PG_COLS), lambda i: (i, 0)),
                   pl.BlockSpec((tm, PM_COLS), lambda i: (i, 0)),
                   pl.BlockSpec((tm, PS_COLS), lambda i: (i, 0))],
        out_shape=[jax.ShapeDtypeStruct((n, PG_COLS), BF16),
                   jax.ShapeDtypeStruct((n, PM_COLS), BF16),
                   jax.ShapeDtypeStruct((n, PS_COLS), BF16)],
        compiler_params=_cparams("arbitrary"),
        name="inproj",
    )(x2, gain, wt)


def _gla_body(p_ref, wg_ref, bg_ref, gain_ref, tri_ref, ones_k_ref, ones_v_ref, bd_ref, s0_ref,
              o_ref, sout_ref, q_scr, k_scr, v_scr, b_scr, s_scr, qe_scr, qs_scr, ke_scr, vb_scr, o_scr,
              *, chunk, nch):
    it = pl.program_id(1)
    sub = min(GLA_SUB, chunk)
    nsub = chunk // sub

    @pl.when(it == 0)
    def _():
        s_scr[...] = jnp.zeros(s_scr.shape, F32)
        for h in range(GLA_HEADS):
            s_scr[h * GLA_DK:(h + 1) * GLA_DK, h * GLA_DV:(h + 1) * GLA_DV] = s0_ref[0, h]

    q_scr[...] = p_ref[:, 0:128].astype(F32) * (GLA_DK ** -0.5)
    k_scr[...] = p_ref[:, 128:256].astype(F32)
    v_scr[...] = p_ref[:, 256:512].astype(F32)
    logit = _dot(p_ref[:, 768:896], wg_ref[...]) + bg_ref[...]
    log_a = (jnp.minimum(logit, 0.0) - jnp.log(1.0 + jnp.exp(-jnp.abs(logit)))) * (1.0 / GLA_GATE_TAU)
    la_hi = log_a.astype(BF16)
    la_lo = (log_a - la_hi.astype(F32)).astype(BF16)
    for c in range(nch):
        rs = slice(c * chunk, (c + 1) * chunk)
        b_scr[rs, :] = _dot(tri_ref[...], la_hi[rs]) + _dot(tri_ref[...], la_lo[rs])

    row = lax.broadcasted_iota(jnp.int32, (chunk, GLA_QK), 0)
    row_in_sub = lax.broadcasted_iota(jnp.int32, (sub, GLA_QK), 0)

    def decay_columns(b_end):
        col = jnp.transpose(jnp.broadcast_to(jnp.exp(b_end), (GLA_QK, GLA_QK)))
        return jnp.concatenate([col, col], axis=1)

    def load_chunk(c):
        r0 = pl.multiple_of(c * chunk, chunk)
        return (r0, q_scr[pl.ds(r0, chunk), :], k_scr[pl.ds(r0, chunk), :], v_scr[pl.ds(r0, chunk), :],
                b_scr[pl.ds(r0, chunk), :], b_scr[pl.ds(r0 + chunk - 1, 1), :], s_scr[...])

    def finish_chunk(r0, o, s_prev, a_state, b_end):
        ms = _dot((o * o).astype(BF16), ones_v_ref[...]) * (1.0 / GLA_DV)
        o_n = o * lax.rsqrt(ms + EPS) * gain_ref[...]
        z = p_ref[pl.ds(r0, chunk), 512:768].astype(F32)
        o_ref[pl.ds(r0, chunk), :] = (o_n * (z * _sigmoid(z))).astype(BF16)
        s_scr[...] = s_prev * decay_columns(b_end) + a_state

    def robust_chunk(c, carry):
        r0, qc, kc, vc, bc, b_end, s_prev = load_chunk(c)
        xs = [qc * jnp.exp(bc)]
        ks = []
        for sj in range(nsub - 1):
            e_j = b_scr[pl.ds(r0 + (sj + 1) * sub - 1, 1), :]
            later = row >= (sj + 1) * sub
            xs.append(jnp.where(later, qc * jnp.exp(jnp.where(later, bc - e_j, 0.0)), 0.0))
            own = (row >= sj * sub) & (row < (sj + 1) * sub)
            ks.append(jnp.where(own, kc * jnp.exp(jnp.where(own, e_j - bc, 0.0)), 0.0))
        ks.append(kc * jnp.exp(b_end - bc))
        k_all = jnp.concatenate(ks, axis=1).astype(BF16)
        a_all = _dot_t(k_all, vc.astype(BF16)) * bd_ref[...]
        w = jnp.concatenate([s_prev, a_all[:(nsub - 1) * GLA_QK]], axis=0).astype(BF16) if nsub > 1 \
            else s_prev.astype(BF16)
        o_off = _dot(jnp.concatenate(xs, axis=1).astype(BF16), w)
        rows = []
        for si in range(nsub):
            q_i = qc[si * sub:(si + 1) * sub]
            b_i = bc[si * sub:(si + 1) * sub]
            es = []
            for j in range(sub):
                r = r0 + si * sub + j
                b_j = b_scr[pl.ds(r, 1), :]
                k_j = k_scr[pl.ds(r, 1), :]
                valid = row_in_sub >= j
                es.append(q_i * k_j * jnp.exp(jnp.where(valid, b_i - b_j, -jnp.inf)))
            e_all = jnp.concatenate(es, axis=0)
            e_hi = e_all.astype(BF16)
            e_lo = (e_all - e_hi.astype(F32)).astype(BF16)
            p_all = _dot(e_hi, ones_k_ref[...]) + _dot(e_lo, ones_k_ref[...])
            acc = o_off[si * sub:(si + 1) * sub]
            for j in range(sub):
                v_j = v_scr[pl.ds(r0 + si * sub + j, 1), :]
                acc = acc + p_all[j * sub:(j + 1) * sub] * v_j
            rows.append(acc)
        o = rows[0] if nsub == 1 else jnp.concatenate(rows, axis=0)
        finish_chunk(r0, o, s_prev, a_all[(nsub - 1) * GLA_QK:], b_end)
        return carry

    lane_head_v = lax.broadcasted_iota(jnp.int32, (chunk, GLA_WIDTH), 1) // GLA_DV
    causal = (lax.broadcasted_iota(jnp.int32, (GLA_HEADS * chunk, chunk), 0) % chunk
              >= lax.broadcasted_iota(jnp.int32, (GLA_HEADS * chunk, chunk), 1))

    def plain_block():
        tt = nch * chunk
        b_all = b_scr[...]
        q_all = q_scr[...]
        k_all = k_scr[...]
        qe = q_all * jnp.exp(b_all)
        lane_head = lax.broadcasted_iota(jnp.int32, (tt, GLA_QK), 1) // GLA_DK
        qe_scr[...] = qe.astype(BF16)
        for h in range(GLA_HEADS):
            qs_scr[h] = jnp.where(lane_head == h, qe, 0.0).astype(BF16)
        ke_scr[...] = (k_all * jnp.exp(-b_all)).astype(BF16)
        vb_scr[...] = p_ref[:, 256:512]
        s_cur = s_scr[...]
        for c in range(nch):
            rs = slice(c * chunk, (c + 1) * chunk)
            qs = jnp.concatenate([qs_scr[h, rs, :] for h in range(GLA_HEADS)], axis=0)
            s = lax.dot_general(qs, ke_scr[rs, :], (((1,), (1,)), ((), ())), preferred_element_type=F32)
            s = jnp.where(causal, s, 0.0).astype(BF16)
            r = _dot(s, vb_scr[rs, :])
            o = _dot(qe_scr[rs, :], s_cur.astype(BF16))
            for h in range(GLA_HEADS):
                o = o + jnp.where(lane_head_v == h, r[h * chunk:(h + 1) * chunk], 0.0)
            o_scr[rs, :] = o
            b_end = b_scr[(c + 1) * chunk - 1:(c + 1) * chunk, :]
            k_end = (k_scr[rs, :] * jnp.exp(b_end - b_scr[rs, :])).astype(BF16)
            a_state = _dot_t(k_end, vb_scr[rs, :]) * bd_ref[0:GLA_QK, :]
            s_cur = s_cur * decay_columns(b_end) + a_state
        s_scr[...] = s_cur
        o = o_scr[...]
        ms = _dot((o * o).astype(BF16), ones_v_ref[...]) * (1.0 / GLA_DV)
        o_n = o * lax.rsqrt(ms + EPS) * gain_ref[...]
        z = p_ref[:, 512:768].astype(F32)
        o_ref[...] = (o_n * (z * _sigmoid(z))).astype(BF16)

    in_range = jnp.max(-b_scr[...]) < GLA_PLAIN_MAX_DECAY

    @pl.when(in_range)
    def _():
        plain_block()

    @pl.when(jnp.logical_not(in_range))
    def _():
        lax.fori_loop(0, nch, robust_chunk, 0)

    @pl.when(it == pl.num_programs(1) - 1)
    def _():
        for h in range(GLA_HEADS):
            sout_ref[0, h] = s_scr[h * GLA_DK:(h + 1) * GLA_DK, h * GLA_DV:(h + 1) * GLA_DV]


def _gla_consts(chunk):
    nsub = chunk // min(GLA_SUB, chunk)
    r = np.arange(chunk)
    tri = (r[None, :] <= r[:, None]).astype(np.float32)
    hk = np.arange(GLA_QK) // GLA_DK
    hv = np.arange(GLA_WIDTH) // GLA_DV
    same_kv = (hk[:, None] == hv[None, :]).astype(np.float32)
    same_vv = (hv[:, None] == hv[None, :]).astype(np.float32)
    return (jnp.asarray(tri, BF16), jnp.asarray(same_kv, BF16), jnp.asarray(same_vv, BF16),
            jnp.asarray(np.tile(same_kv, (nsub, 1)), F32))


def _gla(pg, wg, bg, gain, layer, s0_bd, bsz, t, tl):
    chunk, tt = tl['gla_chunk'], tl['gla_rows']
    nch = tt // chunk
    nt = t // tt
    nsub = chunk // min(GLA_SUB, chunk)
    tri, ones_k, ones_v, bd = _gla_consts(chunk)
    const = lambda shape: _const_spec(shape, 2)
    return pl.pallas_call(
        functools.partial(_gla_body, chunk=chunk, nch=nch),
        grid=(bsz, nt),
        in_specs=[pl.BlockSpec((tt, PG_COLS), lambda b, i: (b * nt + i, 0)),
                  _layer_spec((GLA_QK, GLA_QK), layer), _layer_spec((1, GLA_QK), layer),
                  _layer_spec((1, GLA_WIDTH), layer),
                  const((chunk, chunk)), const((GLA_QK, GLA_WIDTH)), const((GLA_WIDTH, GLA_WIDTH)),
                  const((nsub * GLA_QK, GLA_WIDTH)),
                  pl.BlockSpec((1, GLA_HEADS, GLA_DK, GLA_DV), lambda b, i: (b, 0, 0, 0))],
        out_specs=[pl.BlockSpec((tt, GLA_WIDTH), lambda b, i: (b * nt + i, 0)),
                   pl.BlockSpec((1, GLA_HEADS, GLA_DK, GLA_DV), lambda b, i: (b, 0, 0, 0))],
        out_shape=[jax.ShapeDtypeStruct((bsz * t, GLA_WIDTH), BF16),
                   jax.ShapeDtypeStruct((bsz, GLA_HEADS, GLA_DK, GLA_DV), F32)],
        scratch_shapes=[pltpu.VMEM((tt, GLA_QK), F32), pltpu.VMEM((tt, GLA_QK), F32),
                        pltpu.VMEM((tt, GLA_WIDTH), F32), pltpu.VMEM((tt, GLA_QK), F32),
                        pltpu.VMEM((GLA_QK, GLA_WIDTH), F32),
                        pltpu.VMEM((tt, GLA_QK), BF16), pltpu.VMEM((GLA_HEADS, tt, GLA_QK), BF16),
                        pltpu.VMEM((tt, GLA_QK), BF16),
                        pltpu.VMEM((tt, GLA_WIDTH), BF16), pltpu.VMEM((tt, GLA_WIDTH), F32)],
        compiler_params=_cparams("parallel", "arbitrary"),
        name="gla",
    )(pg, wg, bg, gain, tri, ones_k, ones_v, bd, s0_bd)


def _rope128(x, cos_t, sin_t):
    lane = lax.broadcasted_iota(jnp.int32, x.shape, 1)
    first_half = (lane >= MLA_NOPE_DIM) & (lane < MLA_NOPE_DIM + MLA_ROPE_DIM // 2)
    rot = jnp.where(first_half, -pltpu.roll(x, LANES - MLA_ROPE_DIM // 2, 1), pltpu.roll(x, MLA_ROPE_DIM // 2, 1))
    return x * cos_t + rot * sin_t


def _mla_prepq_body(p_ref, cos_ref, sin_ref, gq_ref, wq_ref, gkv_ref, q_ref, ckv_ref, kpe_ref):
    cos_t = cos_ref[...]
    sin_t = sin_ref[...]
    cq = p_ref[:, 0:256].astype(F32)
    ms = jnp.sum(cq * cq, axis=-1, keepdims=True) * (1.0 / MLA_Q_RANK)
    cqn = (cq * lax.rsqrt(ms + EPS) * gq_ref[...]).astype(BF16)
    qh = _dot(cqn, wq_ref[...])
    scale = (MLA_NOPE_DIM + MLA_ROPE_DIM) ** -0.5 * math.log2(math.e)
    for h in range(MLA_HEADS):
        x = qh[:, h * MLA_QK_PAD:(h + 1) * MLA_QK_PAD]
        q_ref[:, h * MLA_QK_PAD:(h + 1) * MLA_QK_PAD] = (_rope128(x, cos_t, sin_t) * scale).astype(BF16)
    ckv = p_ref[:, 256:384].astype(F32)
    ms = jnp.mean(ckv * ckv, axis=-1, keepdims=True)
    ckv_ref[...] = ckv * lax.rsqrt(ms + EPS) * gkv_ref[...]
    kr = p_ref[:, 384:512].astype(F32)
    kpe_ref[...] = _rope128(kr, cos_t, sin_t)


def _mla_prepq(pm, cos_t, sin_t, gq, wq, gkv, layer, t, tm):
    n = pm.shape[0]
    ntab = t // tm
    return pl.pallas_call(
        _mla_prepq_body,
        grid=(n // tm,),
        in_specs=[pl.BlockSpec((tm, 512), lambda i: (i, 0)),
                  pl.BlockSpec((tm, LANES), lambda i: (i % ntab, 0)),
                  pl.BlockSpec((tm, LANES), lambda i: (i % ntab, 0)),
                  _layer_spec((1, 256), layer), _layer_spec((256, MLA_QK_WIDTH), layer),
                  _layer_spec((1, MLA_KV_RANK), layer)],
        out_specs=[pl.BlockSpec((tm, MLA_QK_WIDTH), lambda i: (i, 0)),
                   pl.BlockSpec((tm, MLA_KV_RANK), lambda i: (i, 0)),
                   pl.BlockSpec((tm, LANES), lambda i: (i, 0))],
        out_shape=[jax.ShapeDtypeStruct((n, MLA_QK_WIDTH), BF16),
                   jax.ShapeDtypeStruct((n, MLA_KV_RANK), F32),
                   jax.ShapeDtypeStruct((n, LANES), F32)],
        compiler_params=_cparams("parallel"),
        name="mla_prepq",
    )(pm, cos_t, sin_t, gq, wq, gkv)


def _mla_kv_rows(ckv, kpe128, wkv_ref, k_ref, v_ref, rows):
    kv = _dot(ckv.astype(BF16), wkv_ref[...])
    for h in range(MLA_HEADS):
        sl = slice(h * MLA_QK_PAD, (h + 1) * MLA_QK_PAD)
        k_ref[rows, sl] = (kv[:, sl] + kpe128).astype(BF16)
    v_ref[rows, :] = kv[:, MLA_QK_WIDTH:].astype(BF16)


def _mla_prepkv_body(ckv_ref, kpe_ref, wkv_ref, k_ref, v_ref):
    _mla_kv_rows(ckv_ref[...], kpe_ref[...], wkv_ref, k_ref, v_ref, slice(None))


def _mla_prepkv(ckv, kpe128, wkv, layer, bsz, t, tm):
    n = bsz * t
    return pl.pallas_call(
        _mla_prepkv_body,
        grid=(n // tm,),
        in_specs=[pl.BlockSpec((tm, MLA_KV_RANK), lambda i: (i, 0)),
                  pl.BlockSpec((tm, LANES), lambda i: (i, 0)),
                  _layer_spec((MLA_KV_RANK, MLA_QK_WIDTH + MLA_WIDTH), layer)],
        out_specs=[pl.BlockSpec((tm, MLA_QK_WIDTH), lambda i: (i, 0)),
                   pl.BlockSpec((tm, MLA_WIDTH), lambda i: (i, 0))],
        out_shape=[jax.ShapeDtypeStruct((n, MLA_QK_WIDTH), BF16), jax.ShapeDtypeStruct((n, MLA_WIDTH), BF16)],
        compiler_params=_cparams("parallel"),
        name="mla_prepkv",
    )(ckv, kpe128, wkv)


def _attn_cached_body(q_ref, z_ref, ckv_new_ref, kpe_new_ref, ckv_past_ref, kpet_past_ref, wka_ref, wv_ref,
                      o_ref, kpet_scr, *, past, t):
    last = (((1,), (1,)), ((), ()))
    heads = [q_ref[:, h * MLA_QK_PAD:(h + 1) * MLA_QK_PAD] for h in range(MLA_HEADS)]
    q_rows = jnp.concatenate(heads, axis=0)
    q_lat = jnp.concatenate([_dot(heads[h], wka_ref[h]) for h in range(MLA_HEADS)], axis=0).astype(BF16)
    c_past = ckv_past_ref[...].astype(BF16)
    c_new = ckv_new_ref[...].astype(BF16)
    kpet_scr[...] = jnp.zeros(kpet_scr.shape, BF16)
    kpet_scr[MLA_NOPE_DIM:MLA_NOPE_DIM + MLA_ROPE_DIM, :] = kpet_past_ref[...].astype(BF16)
    s_past = (lax.dot_general(q_lat, c_past, last, preferred_element_type=F32)
              + _dot(q_rows, kpet_scr[...]))
    s_new = (lax.dot_general(q_lat, c_new, last, preferred_element_type=F32)
             + lax.dot_general(q_rows, kpe_new_ref[...].astype(BF16), last, preferred_element_type=F32))
    if past // CHUNK != (past + t - 1) // CHUNK:
        q_chunk = (past + lax.broadcasted_iota(jnp.int32, s_past.shape, 0) % t) // CHUNK
        s_past = jnp.where(lax.broadcasted_iota(jnp.int32, s_past.shape, 1) // CHUNK <= q_chunk, s_past, -jnp.inf)
        q_chunk = (past + lax.broadcasted_iota(jnp.int32, s_new.shape, 0) % t) // CHUNK
        s_new = jnp.where((past + lax.broadcasted_iota(jnp.int32, s_new.shape, 1)) // CHUNK <= q_chunk,
                          s_new, -jnp.inf)
    m = jnp.maximum(jnp.max(s_past, axis=-1, keepdims=True), jnp.max(s_new, axis=-1, keepdims=True))
    p_past = jnp.exp2(s_past - m)
    p_new = jnp.exp2(s_new - m)
    l = jnp.sum(p_past, axis=-1, keepdims=True) + jnp.sum(p_new, axis=-1, keepdims=True)
    o_lat = ((_dot(p_past.astype(BF16), c_past) + _dot(p_new.astype(BF16), c_new)) / l).astype(BF16)
    for h in range(MLA_HEADS):
        vsl = slice(h * MLA_V_DIM, (h + 1) * MLA_V_DIM)
        z = z_ref[:, vsl].astype(F32)
        o_ref[:, vsl] = (_dot(o_lat[h * t:(h + 1) * t], wv_ref[h]) * (z * _sigmoid(z))).astype(BF16)


def _attn_cached(q, pm, ckv_new, kpe128, ckv_past, kpet_past, wka, wv, layer, bsz, t):
    past = ckv_past.shape[2]
    return pl.pallas_call(
        functools.partial(_attn_cached_body, past=past, t=t),
        grid=(bsz,),
        in_specs=[pl.BlockSpec((t, MLA_QK_WIDTH), lambda b: (b, 0)),
                  pl.BlockSpec((t, MLA_WIDTH), lambda b: (b, 1)),
                  pl.BlockSpec((t, MLA_KV_RANK), lambda b: (b, 0)),
                  pl.BlockSpec((t, LANES), lambda b: (b, 0)),
                  pl.BlockSpec((None, None, past, MLA_KV_RANK), lambda b: (layer, b, 0, 0)),
                  pl.BlockSpec((None, None, MLA_ROPE_DIM, past), lambda b: (layer, b, 0, 0)),
                  _layer_spec((MLA_HEADS, MLA_QK_PAD, MLA_KV_RANK), layer),
                  _layer_spec((MLA_HEADS, MLA_KV_RANK, MLA_V_DIM), layer)],
        out_specs=pl.BlockSpec((t, MLA_WIDTH), lambda b: (b, 0)),
        out_shape=jax.ShapeDtypeStruct((bsz * t, MLA_WIDTH), BF16),
        scratch_shapes=[pltpu.VMEM((MLA_QK_PAD, past), BF16)],
        compiler_params=_cparams("parallel"),
        name="mla_attn_cached",
    )(q, pm, ckv_new, kpe128, ckv_past, kpet_past, wka, wv)


def _attn_body(q_ref, k_ref, v_ref, z_ref, o_ref, m_scr, acc_scr, *, past, tq, tk, s_len):
    iq = pl.program_id(1)
    q_first = past + iq * tq
    full_keys = jnp.minimum((q_first // CHUNK + 1) * CHUNK, s_len)
    vis_keys = jnp.minimum(((q_first + tq - 1) // CHUNK + 1) * CHUNK, s_len)
    n_full = full_keys // tk
    n_vis = (vis_keys + tk - 1) // tk
    ones_v = jnp.ones((tk, MLA_V_DIM), BF16)
    ntile, rem = tk // LANES, tk % LANES

    m_scr[...] = jnp.full(m_scr.shape, -jnp.inf, F32)
    acc_scr[...] = jnp.zeros(acc_scr.shape, F32)

    def block(kb, carry, masked):
        k0 = pl.multiple_of(kb * tk, tk)
        if masked:
            q_chunk = (q_first + lax.broadcasted_iota(jnp.int32, (tq, tk), 0)) // CHUNK
            k_chunk = (k0 + lax.broadcasted_iota(jnp.int32, (tq, tk), 1)) // CHUNK
            visible = k_chunk <= q_chunk
        for h in range(MLA_HEADS):
            sl = slice(h * MLA_QK_PAD, (h + 1) * MLA_QK_PAD)
            vsl = slice(h * MLA_V_DIM, (h + 1) * MLA_V_DIM)
            s = lax.dot_general(q_ref[:, sl], k_ref[pl.ds(k0, tk), sl], (((1,), (1,)), ((), ())),
                                preferred_element_type=F32)
            if masked:
                s = jnp.where(visible, s, -jnp.inf)
            m_prev = m_scr[h]
            m_new = jnp.maximum(m_prev, jnp.max(s, axis=-1, keepdims=True))
            alpha = jnp.exp2(m_prev - m_new)
            ps = [jnp.exp2(s[:, c * LANES:(c + 1) * LANES] - m_new) for c in range(ntile)]
            if rem:
                ps.append(jnp.exp2(s[:, ntile * LANES:] - m_new[:, :rem]))
            p = jnp.concatenate(ps, axis=1).astype(BF16)
            v_ext = jnp.concatenate([v_ref[pl.ds(k0, tk), vsl], ones_v], axis=1)
            acc_scr[h] = jnp.concatenate([alpha, alpha], axis=1) * acc_scr[h] + _dot(p, v_ext)
            m_scr[h] = m_new
        return carry

    lax.fori_loop(0, n_full, functools.partial(block, masked=False), 0)
    lax.fori_loop(n_full, n_vis, functools.partial(block, masked=True), 0)
    for h in range(MLA_HEADS):
        vsl = slice(h * MLA_V_DIM, (h + 1) * MLA_V_DIM)
        z = z_ref[:, vsl].astype(F32)
        acc = acc_scr[h]
        o_ref[:, vsl] = (acc[:, :MLA_V_DIM] / acc[:, MLA_V_DIM:] * (z * _sigmoid(z))).astype(BF16)


def _attn(q, k, v, pm, bsz, t, s_len, past, tq, tk):
    nq = t // tq
    return pl.pallas_call(
        functools.partial(_attn_body, past=past, tq=tq, tk=tk, s_len=s_len),
        grid=(bsz, nq),
        in_specs=[pl.BlockSpec((tq, MLA_QK_WIDTH), lambda b, iq: (b * nq + iq, 0)),
                  pl.BlockSpec((s_len, MLA_QK_WIDTH), lambda b, iq: (b, 0)),
                  pl.BlockSpec((s_len, MLA_WIDTH), lambda b, iq: (b, 0)),
                  pl.BlockSpec((tq, MLA_WIDTH), lambda b, iq: (b * nq + iq, 1))],
        out_specs=pl.BlockSpec((tq, MLA_WIDTH), lambda b, iq: (b * nq + iq, 0)),
        out_shape=jax.ShapeDtypeStruct((bsz * t, MLA_WIDTH), BF16),
        scratch_shapes=[pltpu.VMEM((MLA_HEADS, tq, LANES), F32), pltpu.VMEM((MLA_HEADS, tq, 2 * MLA_V_DIM), F32)],
        compiler_params=_cparams("parallel", "arbitrary"),
        name="mla_attn",
    )(q, k, v, pm)


def _s5_weights_body(bre_ref, bim_ref, cre_ref, cim_ref, spread_ref, mask_ref, wb_ref, wc_ref):
    def expand(m_ref):
        return _dot(m_ref[...].astype(BF16), spread_ref[...]) * mask_ref[...]

    wb_ref[:, 0:S5_NSTATE] = expand(bre_ref).astype(BF16)
    wb_ref[:, S5_NSTATE:] = expand(bim_ref).astype(BF16)
    wc_ref[0:S5_NSTATE, :] = jnp.transpose(expand(cre_ref)).astype(BF16)
    wc_ref[S5_NSTATE:, :] = jnp.transpose(-expand(cim_ref)).astype(BF16)


def _s5_weights(bbr, bbi, c_re, c_im):
    depth = bbr.shape[0]
    g_row = np.arange(S5_WIDTH) // S5_GROUP_CH
    g_col = np.arange(S5_NSTATE) // S5_STATE
    mask = (g_row[:, None] == g_col[None, :]).astype(np.float32)
    spread = (np.arange(S5_STATE)[:, None] == (np.arange(S5_NSTATE) % S5_STATE)[None, :]).astype(np.float32)
    small = pl.BlockSpec((None, S5_WIDTH, S5_STATE), lambda l: (l, 0, 0))
    return pl.pallas_call(
        _s5_weights_body,
        grid=(depth,),
        in_specs=[small, small, small, small,
                  _const_spec((S5_STATE, S5_NSTATE), 1), _const_spec((S5_WIDTH, S5_NSTATE), 1)],
        out_specs=[pl.BlockSpec((None, S5_WIDTH, 2 * S5_NSTATE), lambda l: (l, 0, 0)),
                   pl.BlockSpec((None, 2 * S5_NSTATE, S5_WIDTH), lambda l: (l, 0, 0))],
        out_shape=[jax.ShapeDtypeStruct((depth, S5_WIDTH, 2 * S5_NSTATE), BF16),
                   jax.ShapeDtypeStruct((depth, 2 * S5_NSTATE, S5_WIDTH), BF16)],
        compiler_params=_cparams("parallel"),
        name="s5_weights",
    )(bbr, bbi, c_re, c_im, jnp.asarray(spread, BF16), jnp.asarray(mask, F32))


def _s5_body(p_ref, x0r_ref, x0i_ref, lre_ref, lim_ref, wb_ref, wc_ref, d_ref, wglu_ref, bglu_ref,
             o_ref, xr_out, xi_out, uz_bt, uz_tb, bu, o_tb, xr_s, xi_s, *, lc, pitch):
    it = pl.program_id(1)
    nb = S5_BATCH_TILE

    @pl.when(it == 0)
    def _():
        xr_s[...] = x0r_ref[...]
        xi_s[...] = x0i_ref[...]

    for b in range(nb):
        for c in range(PS_COLS // LANES):
            uz_bt[c, b * pitch:b * pitch + lc, :] = p_ref[b, :, c * LANES:(c + 1) * LANES].astype(F32)
    for t in range(lc):
        for c in range(PS_COLS // LANES):
            uz_tb[t * nb:(t + 1) * nb, c * LANES:(c + 1) * LANES] = uz_bt[c, pl.ds(t, nb, stride=pitch), :]
    u = uz_tb[:, 0:S5_WIDTH]
    bu[...] = _dot(u.astype(BF16), wb_ref[...])

    lre = jnp.broadcast_to(lre_ref[...], (nb, S5_NSTATE))
    lim = jnp.broadcast_to(lim_ref[...], (nb, S5_NSTATE))

    def step(t, carry):
        xr, xi = carry
        r0 = pl.multiple_of(t * nb, nb)
        nr = lre * xr - lim * xi + bu[pl.ds(r0, nb), 0:S5_NSTATE]
        ni = lre * xi + lim * xr + bu[pl.ds(r0, nb), S5_NSTATE:2 * S5_NSTATE]
        bu[pl.ds(r0, nb), 0:S5_NSTATE] = nr
        bu[pl.ds(r0, nb), S5_NSTATE:2 * S5_NSTATE] = ni
        return nr, ni

    xr, xi = lax.fori_loop(0, lc, step, (xr_s[...], xi_s[...]))
    xr_s[...] = xr
    xi_s[...] = xi

    y = _dot(bu[...].astype(BF16), wc_ref[...]) + d_ref[...] * u
    g5 = 0.5 * y * (1.0 + jnp.tanh(0.7978845608028654 * (y + 0.044715 * (y * y * y))))
    gate = _sigmoid(_dot(g5.astype(BF16), wglu_ref[...]) + bglu_ref[...])
    z = uz_tb[:, S5_WIDTH:2 * S5_WIDTH]
    o = g5 * gate * (z * _sigmoid(z))
    for c in range(S5_WIDTH // LANES):
        o_tb[c] = o[:, c * LANES:(c + 1) * LANES]
    for b in range(nb):
        for c in range(S5_WIDTH // LANES):
            o_ref[b, :, c * LANES:(c + 1) * LANES] = o_tb[c, pl.ds(b, lc, stride=nb), :].astype(BF16)

    @pl.when(it == pl.num_programs(1) - 1)
    def _():
        xr_out[...] = xr
        xi_out[...] = xi


def _s5(ps3, x0r, x0i, lre, lim, wb, wc, d, wglu, bglu, layer, bsz, t, lc):
    nb = S5_BATCH_TILE
    nt = t // lc
    pitch = lc + 8
    return pl.pallas_call(
        functools.partial(_s5_body, lc=lc, pitch=pitch),
        grid=(bsz // nb, nt),
        in_specs=[pl.BlockSpec((nb, lc, PS_COLS), lambda g, i: (g, i, 0)),
                  pl.BlockSpec((nb, S5_NSTATE), lambda g, i: (g, 0)),
                  pl.BlockSpec((nb, S5_NSTATE), lambda g, i: (g, 0)),
                  _layer_spec((1, S5_NSTATE), layer), _layer_spec((1, S5_NSTATE), layer),
                  _layer_spec((S5_WIDTH, 2 * S5_NSTATE), layer), _layer_spec((2 * S5_NSTATE, S5_WIDTH), layer),
                  _layer_spec((1, S5_WIDTH), layer), _layer_spec((S5_WIDTH, S5_WIDTH), layer),
                  _layer_spec((1, S5_WIDTH), layer)],
        out_specs=[pl.BlockSpec((nb, lc, S5_WIDTH), lambda g, i: (g, i, 0)),
                   pl.BlockSpec((nb, S5_NSTATE), lambda g, i: (g, 0)),
                   pl.BlockSpec((nb, S5_NSTATE), lambda g, i: (g, 0))],
        out_shape=[jax.ShapeDtypeStruct((bsz, t, S5_WIDTH), BF16),
                   jax.ShapeDtypeStruct((bsz, S5_NSTATE), F32),
                   jax.ShapeDtypeStruct((bsz, S5_NSTATE), F32)],
        scratch_shapes=[pltpu.VMEM((PS_COLS // LANES, nb * pitch, LANES), F32), pltpu.VMEM((lc * nb, PS_COLS), F32),
                        pltpu.VMEM((lc * nb, 2 * S5_NSTATE), F32), pltpu.VMEM((S5_WIDTH // LANES, lc * nb, LANES), F32),
                        pltpu.VMEM((nb, S5_NSTATE), F32), pltpu.VMEM((nb, S5_NSTATE), F32)],
        compiler_params=_cparams("parallel", "arbitrary"),
        name="s5",
    )(ps3, x0r, x0i, lre, lim, wb, wc, d, wglu, bglu)


def _outproj_body(x_ref, og_ref, om_ref, os_ref, w_ref, g_ref, o_ref, *, final):
    acc = _dot(og_ref[...], w_ref[0:GLA_WIDTH, :])
    acc += _dot(om_ref[...], w_ref[GLA_WIDTH:GLA_WIDTH + MLA_WIDTH, :])
    acc += _dot(os_ref[...], w_ref[GLA_WIDTH + MLA_WIDTH:, :])
    xn = x_ref[...] + acc
    if final:
        ms = jnp.mean(xn * xn, axis=-1, keepdims=True)
        xn = xn * lax.rsqrt(ms + EPS) * g_ref[...]
    o_ref[...] = xn


def _outproj(x2, og, om, os_, w, gain, layer, tm, final):
    n = x2.shape[0]
    row = lambda c: pl.BlockSpec((tm, c), lambda i: (i, 0))
    return pl.pallas_call(
        functools.partial(_outproj_body, final=final),
        grid=(n // tm,),
        in_specs=[row(D_MODEL), row(GLA_WIDTH), row(MLA_WIDTH), row(S5_WIDTH),
                  _layer_spec((D_MODEL, D_MODEL), layer),
                  _const_spec((1, D_MODEL), 1)],
        out_specs=row(D_MODEL),
        out_shape=jax.ShapeDtypeStruct((n, D_MODEL), F32),
        compiler_params=_cparams("parallel"),
        name="outproj_final" if final else "outproj",
    )(x2, og, om, os_, w, gain)


def _prepare_params(ln_gain, w_in, gla_w_gate, gla_b_gate, gla_norm_gain, mla_q_norm_gain, mla_w_uq,
                    mla_kv_norm_gain, mla_w_ukv, s5_lambda_re, s5_lambda_im, s5_b_re, s5_b_im, s5_c_re, s5_c_im,
                    s5_d, s5_log_dt, s5_w_glu, s5_b_glu, w_out):
    depth = w_in.shape[0]
    w_t = jnp.swapaxes(w_in, 1, 2)
    wg = jnp.pad(gla_w_gate, ((0, 0), (0, GLA_QK - GLA_GATE_RANK), (0, 0))).astype(BF16)
    wq = mla_w_uq.reshape(depth, MLA_Q_RANK, MLA_HEADS, MLA_NOPE_DIM + MLA_ROPE_DIM)
    wq = jnp.pad(wq, ((0, 0), (0, 256 - MLA_Q_RANK), (0, 0), (0, MLA_QK_PAD - MLA_NOPE_DIM - MLA_ROPE_DIM)))
    wq = wq.reshape(depth, 256, MLA_QK_WIDTH).astype(BF16)
    gq = jnp.pad(mla_q_norm_gain, ((0, 0), (0, 256 - MLA_Q_RANK))).reshape(depth, 1, 256)
    wkv = mla_w_ukv.reshape(depth, MLA_KV_RANK, MLA_HEADS, MLA_NOPE_DIM + MLA_V_DIM)
    wk = jnp.pad(wkv[..., :MLA_NOPE_DIM], ((0, 0), (0, 0), (0, 0), (0, MLA_QK_PAD - MLA_NOPE_DIM)))
    wkv_r = jnp.concatenate([wk.reshape(depth, MLA_KV_RANK, MLA_QK_WIDTH),
                             wkv[..., MLA_NOPE_DIM:].reshape(depth, MLA_KV_RANK, MLA_WIDTH)], axis=2).astype(BF16)
    wka = jnp.pad(jnp.transpose(wkv[..., :MLA_NOPE_DIM], (0, 2, 3, 1)),
                  ((0, 0), (0, 0), (0, MLA_QK_PAD - MLA_NOPE_DIM), (0, 0))).astype(BF16)
    wv = jnp.transpose(wkv[..., MLA_NOPE_DIM:], (0, 2, 1, 3)).astype(BF16)
    dt = jnp.exp(s5_log_dt)[:, :, None]
    mag = jnp.exp(s5_lambda_re * dt)
    lbr, lbi = mag * jnp.cos(s5_lambda_im * dt), mag * jnp.sin(s5_lambda_im * dt)
    den = s5_lambda_re * s5_lambda_re + s5_lambda_im * s5_lambda_im
    qr = ((lbr - 1.0) * s5_lambda_re + lbi * s5_lambda_im) / den
    qi = (lbi * s5_lambda_re - (lbr - 1.0) * s5_lambda_im) / den
    b_re_t, b_im_t = jnp.swapaxes(s5_b_re, 2, 3), jnp.swapaxes(s5_b_im, 2, 3)
    bbr = qr[:, :, None, :] * b_re_t - qi[:, :, None, :] * b_im_t
    bbi = qr[:, :, None, :] * b_im_t + qi[:, :, None, :] * b_re_t
    rows = lambda m: m.reshape(depth, S5_WIDTH, S5_STATE)
    wb, wc = _s5_weights(rows(bbr), rows(bbi), rows(s5_c_re), rows(s5_c_im))
    return dict(
        ln=ln_gain.reshape(depth, 1, D_MODEL), w_in=w_t, wg=wg, bg=gla_b_gate.reshape(depth, 1, GLA_QK),
        gla_gain=jnp.tile(gla_norm_gain, (1, GLA_HEADS)).reshape(depth, 1, GLA_WIDTH),
        gq=gq, wq=wq, gkv=mla_kv_norm_gain.reshape(depth, 1, MLA_KV_RANK), wkv=wkv_r, wka=wka, wv=wv,
        lre=lbr.reshape(depth, 1, S5_NSTATE), lim=lbi.reshape(depth, 1, S5_NSTATE),
        wb=wb, wc=wc, d=s5_d.reshape(depth, 1, S5_WIDTH), wglu=s5_w_glu.astype(BF16),
        bglu=s5_b_glu.reshape(depth, 1, S5_WIDTH), w_out=w_out.astype(BF16))


def _rope_tables(past, t):
    half = MLA_ROPE_DIM // 2
    inv = ROPE_BASE ** (-np.arange(half, dtype=np.float64) / half)
    ang = (past + np.arange(t, dtype=np.float64))[:, None] * inv[None, :]
    cos, sin = np.cos(ang), np.sin(ang)
    pad = MLA_QK_PAD - MLA_NOPE_DIM - MLA_ROPE_DIM
    cos_t = np.concatenate([np.ones((t, MLA_NOPE_DIM)), cos, cos, np.zeros((t, pad))], axis=1)
    sin_t = np.concatenate([np.zeros((t, MLA_NOPE_DIM)), sin, sin, np.zeros((t, pad))], axis=1)
    return jnp.asarray(cos_t, F32), jnp.asarray(sin_t, F32)


def _trunk(x, gla_state, ckv_cache, kpe_cache, s5_re, s5_im, p, final_gain):
    bsz, t, _ = x.shape
    n = bsz * t
    depth = p['w_in'].shape[0]
    past = 0 if ckv_cache is None else ckv_cache.shape[2]
    s_len = past + t
    tl = _tiles(bsz, t, past)
    cos_t, sin_t = _rope_tables(past, t)
    kpet_cache = None if kpe_cache is None else jnp.swapaxes(kpe_cache, 2, 3)
    x2 = x.reshape(n, D_MODEL)
    gain_f = final_gain.reshape(1, D_MODEL)
    gla_o, ckv_o, kpe_o, re_o, im_o = [], [], [], [], []
    for l in range(depth):
        pg, pm, ps = _inproj(x2, p['ln'], p['w_in'], l, tl['row'])
        s0 = jnp.zeros((bsz, GLA_HEADS, GLA_DK, GLA_DV), F32) if gla_state is None else gla_state[l]
        o_gla, s_new = _gla(pg, p['wg'], p['bg'], p['gla_gain'], l, s0, bsz, t, tl)
        gla_o.append(s_new)
        q, ckv_new, kpe128 = _mla_prepq(pm, cos_t, sin_t, p['gq'], p['wq'], p['gkv'], l, t, tl['prep'])
        ckv_o.append(ckv_new.reshape(bsz, t, MLA_KV_RANK))
        kpe_o.append(kpe128[:, MLA_NOPE_DIM:MLA_NOPE_DIM + MLA_ROPE_DIM].reshape(bsz, t, MLA_ROPE_DIM))
        if past == 0:
            k_cat, v_all = _mla_prepkv(ckv_new, kpe128, p['wkv'], l, bsz, t, tl['prep'])
            o_mla = _attn(q, k_cat, v_all, pm, bsz, t, s_len, past, tl['attn_q'], tl['attn_k'])
        else:
            o_mla = _attn_cached(q, pm, ckv_new, kpe128, ckv_cache, kpet_cache, p['wka'], p['wv'], l, bsz, t)
        x0r = jnp.zeros((bsz, S5_NSTATE), F32) if s5_re is None else s5_re[l].reshape(bsz, S5_NSTATE)
        x0i = jnp.zeros((bsz, S5_NSTATE), F32) if s5_im is None else s5_im[l].reshape(bsz, S5_NSTATE)
        o_s5, xr, xi = _s5(ps.reshape(bsz, t, PS_COLS), x0r, x0i, p['lre'], p['lim'], p['wb'], p['wc'],
                           p['d'], p['wglu'], p['bglu'], l, bsz, t, tl['s5_rows'])
        re_o.append(xr.reshape(bsz, S5_GROUPS, S5_STATE))
        im_o.append(xi.reshape(bsz, S5_GROUPS, S5_STATE))
        x2 = _outproj(x2, o_gla, o_mla, o_s5.reshape(n, S5_WIDTH), p['w_out'], gain_f, l,
                      tl['row'], final=(l == depth - 1))
    return (x2.reshape(bsz, t, D_MODEL), jnp.stack(gla_o), jnp.stack(ckv_o), jnp.stack(kpe_o),
            jnp.stack(re_o), jnp.stack(im_o))


def kernel(x_prompt, x_sample, state_gla, cache_mla_ckv, cache_mla_kpe, state_s5_re, state_s5_im, ln_gain, w_in, gla_w_gate, gla_b_gate, gla_norm_gain, mla_q_norm_gain, mla_w_uq, mla_kv_norm_gain, mla_w_ukv, s5_lambda_re, s5_lambda_im, s5_b_re, s5_b_im, s5_c_re, s5_c_im, s5_d, s5_log_dt, s5_w_glu, s5_b_glu, w_out, final_gain):
    p = _prepare_params(ln_gain, w_in, gla_w_gate, gla_b_gate, gla_norm_gain, mla_q_norm_gain, mla_w_uq,
                        mla_kv_norm_gain, mla_w_ukv, s5_lambda_re, s5_lambda_im, s5_b_re, s5_b_im,
                        s5_c_re, s5_c_im, s5_d, s5_log_dt, s5_w_glu, s5_b_glu, w_out)
    y_p, gla_p, ckv_p, kpe_p, re_p, im_p = _trunk(x_prompt, None, None, None, None, None, p, final_gain)
    y_s, gla_s, ckv_s, kpe_s, re_s, im_s = _trunk(x_sample, state_gla, cache_mla_ckv, cache_mla_kpe,
                                                  state_s5_re, state_s5_im, p, final_gain)
    return (y_p, y_s, gla_p, ckv_p, kpe_p, re_p, im_p, gla_s, ckv_s, kpe_s, re_s, im_s)
```

```python
import functools
import math

import numpy as np
import jax
import jax.numpy as jnp
from jax import lax
from jax.experimental import pallas as pl
from jax.experimental.pallas import tpu as pltpu

F32 = jnp.float32
BF16 = jnp.bfloat16

LANES = 128
D_MODEL = 1024
CHUNK = 64
EPS = 1e-6
GLA_HEADS = 4
GLA_DV = 64
GLA_DK = 32
GLA_WIDTH = GLA_HEADS * GLA_DV
GLA_QK = GLA_HEADS * GLA_DK
GLA_GATE_RANK = 16
GLA_GATE_TAU = 16.0
GLA_SUB = 16
GLA_PLAIN_MAX_DECAY = 60.0
MLA_HEADS = 4
MLA_NOPE_DIM = 64
MLA_ROPE_DIM = 32
MLA_V_DIM = 128
MLA_Q_RANK = 192
MLA_KV_RANK = 128
MLA_WIDTH = MLA_HEADS * MLA_V_DIM
MLA_QK_PAD = 128
MLA_QK_WIDTH = MLA_HEADS * MLA_QK_PAD
ROPE_BASE = 10000.0
S5_GROUPS = 16
S5_GROUP_CH = 16
S5_STATE = 64
S5_WIDTH = S5_GROUPS * S5_GROUP_CH
S5_NSTATE = S5_GROUPS * S5_STATE
S5_BATCH_TILE = 8

PG_COLS = 896
PM_COLS = 1024
PS_COLS = 512
_IN_SEGS = (('g_q', GLA_QK), ('g_k', GLA_QK), ('g_v', GLA_WIDTH), ('g_lr', GLA_GATE_RANK), ('g_z', GLA_WIDTH),
            ('m_cq', MLA_Q_RANK), ('m_ckv', MLA_KV_RANK), ('m_kr', MLA_ROPE_DIM), ('m_z', MLA_WIDTH),
            ('s_u', S5_WIDTH), ('s_z', S5_WIDTH), ('end', 0))
IN_OFF = dict(zip([n for n, _ in _IN_SEGS], np.cumsum([0] + [w for _, w in _IN_SEGS[:-1]]).tolist()))

VMEM_LIMIT_BYTES = 48 * 1024 * 1024


def _tiles(bsz, t, past):
    n = bsz * t
    s_len = past + t
    return dict(
        row=min(512, n),
        out_row=min(1024, n),
        prep=min(512, t),
        gla_rows=min(512, t), gla_chunk=min(CHUNK, t),
        attn_q=min(512, t), attn_k=min(512, s_len),
        s5_rows=min(64, t))


def _cparams(*sem):
    return pltpu.CompilerParams(dimension_semantics=sem, vmem_limit_bytes=VMEM_LIMIT_BYTES)


def _sigmoid(x):
    return 0.5 * (1.0 + jnp.tanh(0.5 * x))


def _dot(a, b):
    return jnp.dot(a, b, preferred_element_type=F32)


def _dot_t(a, b):
    return lax.dot_general(a, b, (((0,), (0,)), ((), ())), preferred_element_type=F32)


def _const_spec(shape, ngrid):
    zeros = (0,) * len(shape)
    return pl.BlockSpec(shape, lambda *_: zeros)


def _layer_spec(shape, layer):
    zeros = (0,) * len(shape)
    return pl.BlockSpec((None,) + tuple(shape), lambda *_: (layer,) + zeros)


def _inproj_body(x_ref, g_ref, wt_ref, og_ref, om_ref, os_ref, w_scr):
    @pl.when(pl.program_id(0) == 0)
    def _():
        w_scr[...] = wt_ref[...].astype(BF16)

    x = x_ref[...]
    ms = jnp.mean(x * x, axis=-1, keepdims=True)
    h = (x * lax.rsqrt(ms + EPS) * g_ref[...]).astype(BF16)

    def seg(a, b):
        return lax.dot_general(h, w_scr[a:b, :], (((1,), (1,)), ((), ())), preferred_element_type=F32)

    lane = lax.broadcasted_iota(jnp.int32, (x.shape[0], LANES), 1)
    lane2 = lax.broadcasted_iota(jnp.int32, (x.shape[0], 2 * LANES), 1)
    c = IN_OFF
    og_ref[:, 0:512] = seg(c['g_q'], c['g_lr']).astype(BF16)
    og_ref[:, 512:768] = seg(c['g_z'], c['m_cq']).astype(BF16)
    og_ref[:, 768:896] = jnp.where(lane < GLA_GATE_RANK, seg(c['g_lr'], c['g_lr'] + LANES), 0.0).astype(BF16)
    om_ref[:, 0:256] = jnp.where(lane2 < MLA_Q_RANK, seg(c['m_cq'], c['m_cq'] + 2 * LANES), 0.0).astype(BF16)
    om_ref[:, 256:384] = seg(c['m_ckv'], c['m_kr']).astype(BF16)
    kr = seg(c['m_kr'] - MLA_NOPE_DIM, c['m_kr'] - MLA_NOPE_DIM + LANES)
    om_ref[:, 384:512] = jnp.where((lane >= MLA_NOPE_DIM) & (lane < MLA_NOPE_DIM + MLA_ROPE_DIM), kr, 0.0).astype(BF16)
    om_ref[:, 512:1024] = seg(c['m_z'], c['s_u']).astype(BF16)
    os_ref[...] = seg(c['s_u'], c['end']).astype(BF16)


def _inproj(x2, gain, wt, layer, tm):
    n = x2.shape[0]
    return pl.pallas_call(
        _inproj_body,
        grid=(n // tm,),
        in_specs=[pl.BlockSpec((tm, D_MODEL), lambda i: (i, 0)),
                  _layer_spec((1, D_MODEL), layer),
                  _layer_spec((IN_OFF['end'], D_MODEL), layer)],
        scratch_shapes=[pltpu.VMEM((IN_OFF['end'], D_MODEL), BF16)],
        out_specs=[pl.BlockSpec((tm, PG_COLS), lambda i: (i, 0)),
                   pl.BlockSpec((tm, PM_COLS), lambda i: (i, 0)),
                   pl.BlockSpec((tm, PS_COLS), lambda i: (i, 0))],
        out_shape=[jax.ShapeDtypeStruct((n, PG_COLS), BF16),
                   jax.ShapeDtypeStruct((n, PM_COLS), BF16),
                   jax.ShapeDtypeStruct((n, PS_COLS), BF16)],
        compiler_params=_cparams("arbitrary"),
        name="inproj",
    )(x2, gain, wt)


def _gla_body(p_ref, wg_ref, bg_ref, gain_ref, tri_ref, ones_k_ref, ones_v_ref, bd_ref, s0_ref,
              o_ref, sout_ref, q_scr, k_scr, v_scr, b_scr, s_scr, qe_scr, qs_scr, ke_scr, vb_scr, o_scr,
              *, chunk, nch):
    it = pl.program_id(1)
    sub = min(GLA_SUB, chunk)
    nsub = chunk // sub

    @pl.when(it == 0)
    def _():
        s_scr[...] = jnp.zeros(s_scr.shape, F32)
        for h in range(GLA_HEADS):
            s_scr[h * GLA_DK:(h + 1) * GLA_DK, h * GLA_DV:(h + 1) * GLA_DV] = s0_ref[0, h]

    q_scr[...] = p_ref[:, 0:128].astype(F32) * (GLA_DK ** -0.5)
    k_scr[...] = p_ref[:, 128:256].astype(F32)
    v_scr[...] = p_ref[:, 256:512].astype(F32)
    logit = _dot(p_ref[:, 768:896], wg_ref[...]) + bg_ref[...]
    log_a = (jnp.minimum(logit, 0.0) - jnp.log(1.0 + jnp.exp(-jnp.abs(logit)))) * (1.0 / GLA_GATE_TAU)
    la_hi = log_a.astype(BF16)
    la_lo = (log_a - la_hi.astype(F32)).astype(BF16)
    for c in range(nch):
        rs = slice(c * chunk, (c + 1) * chunk)
        b_scr[rs, :] = _dot(tri_ref[...], la_hi[rs]) + _dot(tri_ref[...], la_lo[rs])

    row = lax.broadcasted_iota(jnp.int32, (chunk, GLA_QK), 0)
    row_in_sub = lax.broadcasted_iota(jnp.int32, (sub, GLA_QK), 0)

    def decay_columns(b_end):
        col = jnp.transpose(jnp.broadcast_to(jnp.exp(b_end), (GLA_QK, GLA_QK)))
        return jnp.concatenate([col, col], axis=1)

    def load_chunk(c):
        r0 = pl.multiple_of(c * chunk, chunk)
        return (r0, q_scr[pl.ds(r0, chunk), :], k_scr[pl.ds(r0, chunk), :], v_scr[pl.ds(r0, chunk), :],
                b_scr[pl.ds(r0, chunk), :], b_scr[pl.ds(r0 + chunk - 1, 1), :], s_scr[...])

    def finish_chunk(r0, o, s_prev, a_state, b_end):
        ms = _dot((o * o).astype(BF16), ones_v_ref[...]) * (1.0 / GLA_DV)
        o_n = o * lax.rsqrt(ms + EPS) * gain_ref[...]
        z = p_ref[pl.ds(r0, chunk), 512:768].astype(F32)
        o_ref[pl.ds(r0, chunk), :] = (o_n * (z * _sigmoid(z))).astype(BF16)
        s_scr[...] = s_prev * decay_columns(b_end) + a_state

    def robust_chunk(c, carry):
        r0, qc, kc, vc, bc, b_end, s_prev = load_chunk(c)
        xs = [qc * jnp.exp(bc)]
        ks = []
        for sj in range(nsub - 1):
            e_j = b_scr[pl.ds(r0 + (sj + 1) * sub - 1, 1), :]
            later = row >= (sj + 1) * sub
            xs.append(jnp.where(later, qc * jnp.exp(jnp.where(later, bc - e_j, 0.0)), 0.0))
            own = (row >= sj * sub) & (row < (sj + 1) * sub)
            ks.append(jnp.where(own, kc * jnp.exp(jnp.where(own, e_j - bc, 0.0)), 0.0))
        ks.append(kc * jnp.exp(b_end - bc))
        k_all = jnp.concatenate(ks, axis=1).astype(BF16)
        a_all = _dot_t(k_all, vc.astype(BF16)) * bd_ref[...]
        w = jnp.concatenate([s_prev, a_all[:(nsub - 1) * GLA_QK]], axis=0).astype(BF16) if nsub > 1 \
            else s_prev.astype(BF16)
        o_off = _dot(jnp.concatenate(xs, axis=1).astype(BF16), w)
        rows = []
        for si in range(nsub):
            q_i = qc[si * sub:(si + 1) * sub]
            b_i = bc[si * sub:(si + 1) * sub]
            es = []
            for j in range(sub):
                r = r0 + si * sub + j
                b_j = b_scr[pl.ds(r, 1), :]
                k_j = k_scr[pl.ds(r, 1), :]
                valid = row_in_sub >= j
                es.append(q_i * k_j * jnp.exp(jnp.where(valid, b_i - b_j, -jnp.inf)))
            e_all = jnp.concatenate(es, axis=0)
            e_hi = e_all.astype(BF16)
            e_lo = (e_all - e_hi.astype(F32)).astype(BF16)
            p_all = _dot(e_hi, ones_k_ref[...]) + _dot(e_lo, ones_k_ref[...])
            acc = o_off[si * sub:(si + 1) * sub]
            for j in range(sub):
                v_j = v_scr[pl.ds(r0 + si * sub + j, 1), :]
                acc = acc + p_all[j * sub:(j + 1) * sub] * v_j
            rows.append(acc)
        o = rows[0] if nsub == 1 else jnp.concatenate(rows, axis=0)
        finish_chunk(r0, o, s_prev, a_all[(nsub - 1) * GLA_QK:], b_end)
        return carry

    lane_head_v = lax.broadcasted_iota(jnp.int32, (chunk, GLA_WIDTH), 1) // GLA_DV
    causal = (lax.broadcasted_iota(jnp.int32, (GLA_HEADS * chunk, chunk), 0) % chunk
              >= lax.broadcasted_iota(jnp.int32, (GLA_HEADS * chunk, chunk), 1))

    def plain_block():
        tt = nch * chunk
        b_all = b_scr[...]
        q_all = q_scr[...]
        k_all = k_scr[...]
        qe = q_all * jnp.exp(b_all)
        lane_head = lax.broadcasted_iota(jnp.int32, (tt, GLA_QK), 1) // GLA_DK
        qe_scr[...] = qe.astype(BF16)
        for h in range(GLA_HEADS):
            qs_scr[h] = jnp.where(lane_head == h, qe, 0.0).astype(BF16)
        ke_scr[...] = (k_all * jnp.exp(-b_all)).astype(BF16)
        vb_scr[...] = p_ref[:, 256:512]
        s_cur = s_scr[...]
        for c in range(nch):
            rs = slice(c * chunk, (c + 1) * chunk)
            qs = jnp.concatenate([qs_scr[h, rs, :] for h in range(GLA_HEADS)], axis=0)
            s = lax.dot_general(qs, ke_scr[rs, :], (((1,), (1,)), ((), ())), preferred_element_type=F32)
            s = jnp.where(causal, s, 0.0).astype(BF16)
            r = _dot(s, vb_scr[rs, :])
            o = _dot(qe_scr[rs, :], s_cur.astype(BF16))
            for h in range(GLA_HEADS):
                o = o + jnp.where(lane_head_v == h, r[h * chunk:(h + 1) * chunk], 0.0)
            o_scr[rs, :] = o
            b_end = b_scr[(c + 1) * chunk - 1:(c + 1) * chunk, :]
            k_end = (k_scr[rs, :] * jnp.exp(b_end - b_scr[rs, :])).astype(BF16)
            a_state = _dot_t(k_end, vb_scr[rs, :]) * bd_ref[0:GLA_QK, :]
            s_cur = s_cur * decay_columns(b_end) + a_state
        s_scr[...] = s_cur
        o = o_scr[...]
        ms = _dot((o * o).astype(BF16), ones_v_ref[...]) * (1.0 / GLA_DV)
        o_n = o * lax.rsqrt(ms + EPS) * gain_ref[...]
        z = p_ref[:, 512:768].astype(F32)
        o_ref[...] = (o_n * (z * _sigmoid(z))).astype(BF16)

    in_range = jnp.max(-b_scr[...]) < GLA_PLAIN_MAX_DECAY

    @pl.when(in_range)
    def _():
        plain_block()

    @pl.when(jnp.logical_not(in_range))
    def _():
        lax.fori_loop(0, nch, robust_chunk, 0)

    @pl.when(it == pl.num_programs(1) - 1)
    def _():
        for h in range(GLA_HEADS):
            sout_ref[0, h] = s_scr[h * GLA_DK:(h + 1) * GLA_DK, h * GLA_DV:(h + 1) * GLA_DV]


def _gla_consts(chunk):
    nsub = chunk // min(GLA_SUB, chunk)
    r = np.arange(chunk)
    tri = (r[None, :] <= r[:, None]).astype(np.float32)
    hk = np.arange(GLA_QK) // GLA_DK
    hv = np.arange(GLA_WIDTH) // GLA_DV
    same_kv = (hk[:, None] == hv[None, :]).astype(np.float32)
    same_vv = (hv[:, None] == hv[None, :]).astype(np.float32)
    return (jnp.asarray(tri, BF16), jnp.asarray(same_kv, BF16), jnp.asarray(same_vv, BF16),
            jnp.asarray(np.tile(same_kv, (nsub, 1)), F32))


def _gla(pg, wg, bg, gain, layer, s0_bd, bsz, t, tl):
    chunk, tt = tl['gla_chunk'], tl['gla_rows']
    nch = tt // chunk
    nt = t // tt
    nsub = chunk // min(GLA_SUB, chunk)
    tri, ones_k, ones_v, bd = _gla_consts(chunk)
    const = lambda shape: _const_spec(shape, 2)
    return pl.pallas_call(
        functools.partial(_gla_body, chunk=chunk, nch=nch),
        grid=(bsz, nt),
        in_specs=[pl.BlockSpec((tt, PG_COLS), lambda b, i: (b * nt + i, 0)),
                  _layer_spec((GLA_QK, GLA_QK), layer), _layer_spec((1, GLA_QK), layer),
                  _layer_spec((1, GLA_WIDTH), layer),
                  const((chunk, chunk)), const((GLA_QK, GLA_WIDTH)), const((GLA_WIDTH, GLA_WIDTH)),
                  const((nsub * GLA_QK, GLA_WIDTH)),
                  pl.BlockSpec((1, GLA_HEADS, GLA_DK, GLA_DV), lambda b, i: (b, 0, 0, 0))],
        out_specs=[pl.BlockSpec((tt, GLA_WIDTH), lambda b, i: (b * nt + i, 0)),
                   pl.BlockSpec((1, GLA_HEADS, GLA_DK, GLA_DV), lambda b, i: (b, 0, 0, 0))],
        out_shape=[jax.ShapeDtypeStruct((bsz * t, GLA_WIDTH), BF16),
                   jax.ShapeDtypeStruct((bsz, GLA_HEADS, GLA_DK, GLA_DV), F32)],
        scratch_shapes=[pltpu.VMEM((tt, GLA_QK), F32), pltpu.VMEM((tt, GLA_QK), F32),
                        pltpu.VMEM((tt, GLA_WIDTH), F32), pltpu.VMEM((tt, GLA_QK), F32),
                        pltpu.VMEM((GLA_QK, GLA_WIDTH), F32),
                        pltpu.VMEM((tt, GLA_QK), BF16), pltpu.VMEM((GLA_HEADS, tt, GLA_QK), BF16),
                        pltpu.VMEM((tt, GLA_QK), BF16),
                        pltpu.VMEM((tt, GLA_WIDTH), BF16), pltpu.VMEM((tt, GLA_WIDTH), F32)],
        compiler_params=_cparams("parallel", "arbitrary"),
        name="gla",
    )(pg, wg, bg, gain, tri, ones_k, ones_v, bd, s0_bd)


def _rope128(x, cos_t, sin_t):
    lane = lax.broadcasted_iota(jnp.int32, x.shape, 1)
    first_half = (lane >= MLA_NOPE_DIM) & (lane < MLA_NOPE_DIM + MLA_ROPE_DIM // 2)
    rot = jnp.where(first_half, -pltpu.roll(x, LANES - MLA_ROPE_DIM // 2, 1), pltpu.roll(x, MLA_ROPE_DIM // 2, 1))
    return x * cos_t + rot * sin_t


def _mla_prepq_body(p_ref, cos_ref, sin_ref, gq_ref, wq_ref, gkv_ref, q_ref, ckv_ref, kpe_ref):
    cos_t = cos_ref[...]
    sin_t = sin_ref[...]
    cq = p_ref[:, 0:256].astype(F32)
    ms = jnp.sum(cq * cq, axis=-1, keepdims=True) * (1.0 / MLA_Q_RANK)
    cqn = (cq * lax.rsqrt(ms + EPS) * gq_ref[...]).astype(BF16)
    qh = _dot(cqn, wq_ref[...])
    scale = (MLA_NOPE_DIM + MLA_ROPE_DIM) ** -0.5 * math.log2(math.e)
    for h in range(MLA_HEADS):
        x = qh[:, h * MLA_QK_PAD:(h + 1) * MLA_QK_PAD]
        q_ref[:, h * MLA_QK_PAD:(h + 1) * MLA_QK_PAD] = (_rope128(x, cos_t, sin_t) * scale).astype(BF16)
    ckv = p_ref[:, 256:384].astype(F32)
    ms = jnp.mean(ckv * ckv, axis=-1, keepdims=True)
    ckv_ref[...] = ckv * lax.rsqrt(ms + EPS) * gkv_ref[...]
    kr = p_ref[:, 384:512].astype(F32)
    kpe_ref[...] = _rope128(kr, cos_t, sin_t)


def _mla_prepq(pm, cos_t, sin_t, gq, wq, gkv, layer, t, tm):
    n = pm.shape[0]
    ntab = t // tm
    return pl.pallas_call(
        _mla_prepq_body,
        grid=(n // tm,),
        in_specs=[pl.BlockSpec((tm, 512), lambda i: (i, 0)),
                  pl.BlockSpec((tm, LANES), lambda i: (i % ntab, 0)),
                  pl.BlockSpec((tm, LANES), lambda i: (i % ntab, 0)),
                  _layer_spec((1, 256), layer), _layer_spec((256, MLA_QK_WIDTH), layer),
                  _layer_spec((1, MLA_KV_RANK), layer)],
        out_specs=[pl.BlockSpec((tm, MLA_QK_WIDTH), lambda i: (i, 0)),
                   pl.BlockSpec((tm, MLA_KV_RANK), lambda i: (i, 0)),
                   pl.BlockSpec((tm, LANES), lambda i: (i, 0))],
        out_shape=[jax.ShapeDtypeStruct((n, MLA_QK_WIDTH), BF16),
                   jax.ShapeDtypeStruct((n, MLA_KV_RANK), F32),
                   jax.ShapeDtypeStruct((n, LANES), F32)],
        compiler_params=_cparams("parallel"),
        name="mla_prepq",
    )(pm, cos_t, sin_t, gq, wq, gkv)


def _mla_kv_rows(ckv, kpe128, wkv_ref, k_ref, v_ref, rows):
    kv = _dot(ckv.astype(BF16), wkv_ref[...])
    for h in range(MLA_HEADS):
        sl = slice(h * MLA_QK_PAD, (h + 1) * MLA_QK_PAD)
        k_ref[rows, sl] = (kv[:, sl] + kpe128).astype(BF16)
    v_ref[rows, :] = kv[:, MLA_QK_WIDTH:].astype(BF16)


def _mla_prepkv_body(ckv_ref, kpe_ref, wkv_ref, k_ref, v_ref):
    _mla_kv_rows(ckv_ref[...], kpe_ref[...], wkv_ref, k_ref, v_ref, slice(None))


def _mla_prepkv(ckv, kpe128, wkv, layer, bsz, t, tm):
    n = bsz * t
    return pl.pallas_call(
        _mla_prepkv_body,
        grid=(n // tm,),
        in_specs=[pl.BlockSpec((tm, MLA_KV_RANK), lambda i: (i, 0)),
                  pl.BlockSpec((tm, LANES), lambda i: (i, 0)),
                  _layer_spec((MLA_KV_RANK, MLA_QK_WIDTH + MLA_WIDTH), layer)],
        out_specs=[pl.BlockSpec((tm, MLA_QK_WIDTH), lambda i: (i, 0)),
                   pl.BlockSpec((tm, MLA_WIDTH), lambda i: (i, 0))],
        out_shape=[jax.ShapeDtypeStruct((n, MLA_QK_WIDTH), BF16), jax.ShapeDtypeStruct((n, MLA_WIDTH), BF16)],
        compiler_params=_cparams("parallel"),
        name="mla_prepkv",
    )(ckv, kpe128, wkv)


def _attn_cached_body(q_ref, z_ref, ckv_new_ref, kpe_new_ref, ckv_past_ref, kpet_past_ref, wka_ref, wv_ref,
                      o_ref, kpet_scr, *, past, t):
    last = (((1,), (1,)), ((), ()))
    heads = [q_ref[:, h * MLA_QK_PAD:(h + 1) * MLA_QK_PAD] for h in range(MLA_HEADS)]
    q_rows = jnp.concatenate(heads, axis=0)
    q_lat = jnp.concatenate([_dot(heads[h], wka_ref[h]) for h in range(MLA_HEADS)], axis=0).astype(BF16)
    c_past = ckv_past_ref[...].astype(BF16)
    c_new = ckv_new_ref[...].astype(BF16)
    kpet_scr[...] = jnp.zeros(kpet_scr.shape, BF16)
    kpet_scr[MLA_NOPE_DIM:MLA_NOPE_DIM + MLA_ROPE_DIM, :] = kpet_past_ref[...].astype(BF16)
    s_past = (lax.dot_general(q_lat, c_past, last, preferred_element_type=F32)
              + _dot(q_rows, kpet_scr[...]))
    s_new = (lax.dot_general(q_lat, c_new, last, preferred_element_type=F32)
             + lax.dot_general(q_rows, kpe_new_ref[...].astype(BF16), last, preferred_element_type=F32))
    if past // CHUNK != (past + t - 1) // CHUNK:
        q_chunk = (past + lax.broadcasted_iota(jnp.int32, s_past.shape, 0) % t) // CHUNK
        s_past = jnp.where(lax.broadcasted_iota(jnp.int32, s_past.shape, 1) // CHUNK <= q_chunk, s_past, -jnp.inf)
        q_chunk = (past + lax.broadcasted_iota(jnp.int32, s_new.shape, 0) % t) // CHUNK
        s_new = jnp.where((past + lax.broadcasted_iota(jnp.int32, s_new.shape, 1)) // CHUNK <= q_chunk,
                          s_new, -jnp.inf)
    m = jnp.maximum(jnp.max(s_past, axis=-1, keepdims=True), jnp.max(s_new, axis=-1, keepdims=True))
    p_past = jnp.exp2(s_past - m)
    p_new = jnp.exp2(s_new - m)
    l = jnp.sum(p_past, axis=-1, keepdims=True) + jnp.sum(p_new, axis=-1, keepdims=True)
    o_lat = ((_dot(p_past.astype(BF16), c_past) + _dot(p_new.astype(BF16), c_new)) / l).astype(BF16)
    for h in range(MLA_HEADS):
        vsl = slice(h * MLA_V_DIM, (h + 1) * MLA_V_DIM)
        z = z_ref[:, vsl].astype(F32)
        o_ref[:, vsl] = (_dot(o_lat[h * t:(h + 1) * t], wv_ref[h]) * (z * _sigmoid(z))).astype(BF16)


def _attn_cached(q, pm, ckv_new, kpe128, ckv_past, kpet_past, wka, wv, layer, bsz, t):
    past = ckv_past.shape[2]
    return pl.pallas_call(
        functools.partial(_attn_cached_body, past=past, t=t),
        grid=(bsz,),
        in_specs=[pl.BlockSpec((t, MLA_QK_WIDTH), lambda b: (b, 0)),
                  pl.BlockSpec((t, MLA_WIDTH), lambda b: (b, 1)),
                  pl.BlockSpec((t, MLA_KV_RANK), lambda b: (b, 0)),
                  pl.BlockSpec((t, LANES), lambda b: (b, 0)),
                  pl.BlockSpec((None, None, past, MLA_KV_RANK), lambda b: (layer, b, 0, 0)),
                  pl.BlockSpec((None, None, MLA_ROPE_DIM, past), lambda b: (layer, b, 0, 0)),
                  _layer_spec((MLA_HEADS, MLA_QK_PAD, MLA_KV_RANK), layer),
                  _layer_spec((MLA_HEADS, MLA_KV_RANK, MLA_V_DIM), layer)],
        out_specs=pl.BlockSpec((t, MLA_WIDTH), lambda b: (b, 0)),
        out_shape=jax.ShapeDtypeStruct((bsz * t, MLA_WIDTH), BF16),
        scratch_shapes=[pltpu.VMEM((MLA_QK_PAD, past), BF16)],
        compiler_params=_cparams("parallel"),
        name="mla_attn_cached",
    )(q, pm, ckv_new, kpe128, ckv_past, kpet_past, wka, wv)


def _attn_body(q_ref, k_ref, v_ref, z_ref, o_ref, m_scr, acc_scr, *, past, tq, tk, s_len):
    iq = pl.program_id(1)
    q_first = past + iq * tq
    full_keys = jnp.minimum((q_first // CHUNK + 1) * CHUNK, s_len)
    vis_keys = jnp.minimum(((q_first + tq - 1) // CHUNK + 1) * CHUNK, s_len)
    n_full = full_keys // tk
    n_vis = (vis_keys + tk - 1) // tk
    ones_v = jnp.ones((tk, MLA_V_DIM), BF16)
    ntile, rem = tk // LANES, tk % LANES

    m_scr[...] = jnp.full(m_scr.shape, -jnp.inf, F32)
    acc_scr[...] = jnp.zeros(acc_scr.shape, F32)

    def block(kb, carry, masked):
        k0 = pl.multiple_of(kb * tk, tk)
        if masked:
            q_chunk = (q_first + lax.broadcasted_iota(jnp.int32, (tq, tk), 0)) // CHUNK
            k_chunk = (k0 + lax.broadcasted_iota(jnp.int32, (tq, tk), 1)) // CHUNK
            visible = k_chunk <= q_chunk
        for h in range(MLA_HEADS):
            sl = slice(h * MLA_QK_PAD, (h + 1) * MLA_QK_PAD)
            vsl = slice(h * MLA_V_DIM, (h + 1) * MLA_V_DIM)
            s = lax.dot_general(q_ref[:, sl], k_ref[pl.ds(k0, tk), sl], (((1,), (1,)), ((), ())),
                                preferred_element_type=F32)
            if masked:
                s = jnp.where(visible, s, -jnp.inf)
            m_prev = m_scr[h]
            m_new = jnp.maximum(m_prev, jnp.max(s, axis=-1, keepdims=True))
            alpha = jnp.exp2(m_prev - m_new)
            ps = [jnp.exp2(s[:, c * LANES:(c + 1) * LANES] - m_new) for c in range(ntile)]
            if rem:
                ps.append(jnp.exp2(s[:, ntile * LANES:] - m_new[:, :rem]))
            p = jnp.concatenate(ps, axis=1).astype(BF16)
            v_ext = jnp.concatenate([v_ref[pl.ds(k0, tk), vsl], ones_v], axis=1)
            acc_scr[h] = jnp.concatenate([alpha, alpha], axis=1) * acc_scr[h] + _dot(p, v_ext)
            m_scr[h] = m_new
        return carry

    lax.fori_loop(0, n_full, functools.partial(block, masked=False), 0)
    lax.fori_loop(n_full, n_vis, functools.partial(block, masked=True), 0)
    for h in range(MLA_HEADS):
        vsl = slice(h * MLA_V_DIM, (h + 1) * MLA_V_DIM)
        z = z_ref[:, vsl].astype(F32)
        acc = acc_scr[h]
        o_ref[:, vsl] = (acc[:, :MLA_V_DIM] / acc[:, MLA_V_DIM:] * (z * _sigmoid(z))).astype(BF16)


def _attn(q, k, v, pm, bsz, t, s_len, past, tq, tk):
    nq = t // tq
    return pl.pallas_call(
        functools.partial(_attn_body, past=past, tq=tq, tk=tk, s_len=s_len),
        grid=(bsz, nq),
        in_specs=[pl.BlockSpec((tq, MLA_QK_WIDTH), lambda b, iq: (b * nq + iq, 0)),
                  pl.BlockSpec((s_len, MLA_QK_WIDTH), lambda b, iq: (b, 0)),
                  pl.BlockSpec((s_len, MLA_WIDTH), lambda b, iq: (b, 0)),
                  pl.BlockSpec((tq, MLA_WIDTH), lambda b, iq: (b * nq + iq, 1))],
        out_specs=pl.BlockSpec((tq, MLA_WIDTH), lambda b, iq: (b * nq + iq, 0)),
        out_shape=jax.ShapeDtypeStruct((bsz * t, MLA_WIDTH), BF16),
        scratch_shapes=[pltpu.VMEM((MLA_HEADS, tq, LANES), F32), pltpu.VMEM((MLA_HEADS, tq, 2 * MLA_V_DIM), F32)],
        compiler_params=_cparams("parallel", "arbitrary"),
        name="mla_attn",
    )(q, k, v, pm)


def _s5_weights_body(bre_ref, bim_ref, cre_ref, cim_ref, spread_ref, mask_ref, wb_ref, wc_ref):
    def expand(m_ref):
        return _dot(m_ref[...].astype(BF16), spread_ref[...]) * mask_ref[...]

    wb_ref[:, 0:S5_NSTATE] = expand(bre_ref).astype(BF16)
    wb_ref[:, S5_NSTATE:] = expand(bim_ref).astype(BF16)
    wc_ref[0:S5_NSTATE, :] = jnp.transpose(expand(cre_ref)).astype(BF16)
    wc_ref[S5_NSTATE:, :] = jnp.transpose(-expand(cim_ref)).astype(BF16)


def _s5_weights(bbr, bbi, c_re, c_im):
    depth = bbr.shape[0]
    g_row = np.arange(S5_WIDTH) // S5_GROUP_CH
    g_col = np.arange(S5_NSTATE) // S5_STATE
    mask = (g_row[:, None] == g_col[None, :]).astype(np.float32)
    spread = (np.arange(S5_STATE)[:, None] == (np.arange(S5_NSTATE) % S5_STATE)[None, :]).astype(np.float32)
    small = pl.BlockSpec((None, S5_WIDTH, S5_STATE), lambda l: (l, 0, 0))
    return pl.pallas_call(
        _s5_weights_body,
        grid=(depth,),
        in_specs=[small, small, small, small,
                  _const_spec((S5_STATE, S5_NSTATE), 1), _const_spec((S5_WIDTH, S5_NSTATE), 1)],
        out_specs=[pl.BlockSpec((None, S5_WIDTH, 2 * S5_NSTATE), lambda l: (l, 0, 0)),
                   pl.BlockSpec((None, 2 * S5_NSTATE, S5_WIDTH), lambda l: (l, 0, 0))],
        out_shape=[jax.ShapeDtypeStruct((depth, S5_WIDTH, 2 * S5_NSTATE), BF16),
                   jax.ShapeDtypeStruct((depth, 2 * S5_NSTATE, S5_WIDTH), BF16)],
        compiler_params=_cparams("parallel"),
        name="s5_weights",
    )(bbr, bbi, c_re, c_im, jnp.asarray(spread, BF16), jnp.asarray(mask, F32))


def _s5_body(p0_ref, pn_ref, x0r_ref, x0i_ref, lre_ref, lim_ref, wb_ref, wc_ref, d_ref, wglu_ref, bglu_ref,
             o_ref, xr_out, xi_out, uz_bt, uz_a, uz_b, uz_c, bu_a, bu_b, bu_c, o_tb, xr_s, xi_s, *, lc, pitch):
    it = pl.program_id(1)
    nb = S5_BATCH_TILE
    ring = ((uz_a, bu_a), (uz_b, bu_b), (uz_c, bu_c))

    def stage_in(blk_ref, uz_tb, bu):
        for b in range(nb):
            for c in range(PS_COLS // LANES):
                uz_bt[c, b * pitch:b * pitch + lc, :] = blk_ref[b, :, c * LANES:(c + 1) * LANES].astype(F32)
        for t in range(lc):
            for c in range(PS_COLS // LANES):
                uz_tb[t * nb:(t + 1) * nb, c * LANES:(c + 1) * LANES] = uz_bt[c, pl.ds(t, nb, stride=pitch), :]
        bu[...] = _dot(uz_tb[:, 0:S5_WIDTH].astype(BF16), wb_ref[...])

    def stage_scan(bu):
        lre = jnp.broadcast_to(lre_ref[...], (nb, S5_NSTATE))
        lim = jnp.broadcast_to(lim_ref[...], (nb, S5_NSTATE))
        xr, xi = xr_s[...], xi_s[...]
        for t in range(lc):
            rs = slice(t * nb, (t + 1) * nb)
            xr, xi = (lre * xr - lim * xi + bu[rs, 0:S5_NSTATE],
                      lre * xi + lim * xr + bu[rs, S5_NSTATE:2 * S5_NSTATE])
            bu[rs, 0:S5_NSTATE] = xr
            bu[rs, S5_NSTATE:2 * S5_NSTATE] = xi
        xr_s[...] = xr
        xi_s[...] = xi

    def stage_out(uz_tb, xs):
        y = _dot(xs[...].astype(BF16), wc_ref[...]) + d_ref[...] * uz_tb[:, 0:S5_WIDTH]
        g5 = 0.5 * y * (1.0 + jnp.tanh(0.7978845608028654 * (y + 0.044715 * (y * y * y))))
        gate = _sigmoid(_dot(g5.astype(BF16), wglu_ref[...]) + bglu_ref[...])
        z = uz_tb[:, S5_WIDTH:2 * S5_WIDTH]
        o = g5 * gate * (z * _sigmoid(z))
        for c in range(S5_WIDTH // LANES):
            o_tb[c] = o[:, c * LANES:(c + 1) * LANES]
        for b in range(nb):
            for c in range(S5_WIDTH // LANES):
                o_ref[b, :, c * LANES:(c + 1) * LANES] = o_tb[c, pl.ds(b, lc, stride=nb), :].astype(BF16)

    @pl.when(it == 0)
    def _():
        xr_s[...] = x0r_ref[...]
        xi_s[...] = x0i_ref[...]
        stage_in(p0_ref, uz_a, bu_a)
        uz_c[...] = jnp.zeros(uz_c.shape, F32)
        bu_c[...] = jnp.zeros(bu_c.shape, F32)

    for r in range(3):
        @pl.when(it % 3 == r)
        def _(r=r):
            stage_in(pn_ref, *ring[(r + 1) % 3])
            stage_scan(ring[r][1])
            stage_out(*ring[(r + 2) % 3])

    @pl.when(it == pl.num_programs(1) - 2)
    def _():
        xr_out[...] = xr_s[...]
        xi_out[...] = xi_s[...]


def _s5(ps3, x0r, x0i, lre, lim, wb, wc, d, wglu, bglu, layer, bsz, t, lc):
    nb = S5_BATCH_TILE
    nt = t // lc
    pitch = lc + 8
    rows = lc * nb
    return pl.pallas_call(
        functools.partial(_s5_body, lc=lc, pitch=pitch),
        grid=(bsz // nb, nt + 1),
        in_specs=[pl.BlockSpec((nb, lc, PS_COLS), lambda g, i: (g, 0, 0)),
                  pl.BlockSpec((nb, lc, PS_COLS), lambda g, i: (g, jnp.minimum(i + 1, nt - 1), 0)),
                  pl.BlockSpec((nb, S5_NSTATE), lambda g, i: (g, 0)),
                  pl.BlockSpec((nb, S5_NSTATE), lambda g, i: (g, 0)),
                  _layer_spec((1, S5_NSTATE), layer), _layer_spec((1, S5_NSTATE), layer),
                  _layer_spec((S5_WIDTH, 2 * S5_NSTATE), layer), _layer_spec((2 * S5_NSTATE, S5_WIDTH), layer),
                  _layer_spec((1, S5_WIDTH), layer), _layer_spec((S5_WIDTH, S5_WIDTH), layer),
                  _layer_spec((1, S5_WIDTH), layer)],
        out_specs=[pl.BlockSpec((nb, lc, S5_WIDTH), lambda g, i: (g, jnp.maximum(i - 1, 0), 0)),
                   pl.BlockSpec((nb, S5_NSTATE), lambda g, i: (g, 0)),
                   pl.BlockSpec((nb, S5_NSTATE), lambda g, i: (g, 0))],
        out_shape=[jax.ShapeDtypeStruct((bsz, t, S5_WIDTH), BF16),
                   jax.ShapeDtypeStruct((bsz, S5_NSTATE), F32),
                   jax.ShapeDtypeStruct((bsz, S5_NSTATE), F32)],
        scratch_shapes=[pltpu.VMEM((PS_COLS // LANES, nb * pitch, LANES), F32)]
                       + [pltpu.VMEM((rows, PS_COLS), F32)] * 3
                       + [pltpu.VMEM((rows, 2 * S5_NSTATE), F32)] * 3
                       + [pltpu.VMEM((S5_WIDTH // LANES, rows, LANES), F32),
                          pltpu.VMEM((nb, S5_NSTATE), F32), pltpu.VMEM((nb, S5_NSTATE), F32)],
        compiler_params=_cparams("parallel", "arbitrary"),
        name="s5",
    )(ps3, ps3, x0r, x0i, lre, lim, wb, wc, d, wglu, bglu)


def _outproj_body(x_ref, og_ref, om_ref, os_ref, w_ref, g_ref, o_ref, *, final):
    acc = _dot(og_ref[...], w_ref[0:GLA_WIDTH, :])
    acc += _dot(om_ref[...], w_ref[GLA_WIDTH:GLA_WIDTH + MLA_WIDTH, :])
    acc += _dot(os_ref[...], w_ref[GLA_WIDTH + MLA_WIDTH:, :])
    xn = x_ref[...] + acc
    if final:
        ms = jnp.mean(xn * xn, axis=-1, keepdims=True)
        xn = xn * lax.rsqrt(ms + EPS) * g_ref[...]
    o_ref[...] = xn


def _outproj(x2, og, om, os_, w, gain, layer, tm, final):
    n = x2.shape[0]
    row = lambda c: pl.BlockSpec((tm, c), lambda i: (i, 0))
    return pl.pallas_call(
        functools.partial(_outproj_body, final=final),
        grid=(n // tm,),
        in_specs=[row(D_MODEL), row(GLA_WIDTH), row(MLA_WIDTH), row(S5_WIDTH),
                  _layer_spec((D_MODEL, D_MODEL), layer),
                  _const_spec((1, D_MODEL), 1)],
        out_specs=row(D_MODEL),
        out_shape=jax.ShapeDtypeStruct((n, D_MODEL), F32),
        compiler_params=_cparams("parallel"),
        name="outproj_final" if final else "outproj",
    )(x2, og, om, os_, w, gain)


def _prepare_params(ln_gain, w_in, gla_w_gate, gla_b_gate, gla_norm_gain, mla_q_norm_gain, mla_w_uq,
                    mla_kv_norm_gain, mla_w_ukv, s5_lambda_re, s5_lambda_im, s5_b_re, s5_b_im, s5_c_re, s5_c_im,
                    s5_d, s5_log_dt, s5_w_glu, s5_b_glu, w_out):
    depth = w_in.shape[0]
    w_t = jnp.swapaxes(w_in, 1, 2)
    wg = jnp.pad(gla_w_gate, ((0, 0), (0, GLA_QK - GLA_GATE_RANK), (0, 0))).astype(BF16)
    wq = mla_w_uq.reshape(depth, MLA_Q_RANK, MLA_HEADS, MLA_NOPE_DIM + MLA_ROPE_DIM)
    wq = jnp.pad(wq, ((0, 0), (0, 256 - MLA_Q_RANK), (0, 0), (0, MLA_QK_PAD - MLA_NOPE_DIM - MLA_ROPE_DIM)))
    wq = wq.reshape(depth, 256, MLA_QK_WIDTH).astype(BF16)
    gq = jnp.pad(mla_q_norm_gain, ((0, 0), (0, 256 - MLA_Q_RANK))).reshape(depth, 1, 256)
    wkv = mla_w_ukv.reshape(depth, MLA_KV_RANK, MLA_HEADS, MLA_NOPE_DIM + MLA_V_DIM)
    wk = jnp.pad(wkv[..., :MLA_NOPE_DIM], ((0, 0), (0, 0), (0, 0), (0, MLA_QK_PAD - MLA_NOPE_DIM)))
    wkv_r = jnp.concatenate([wk.reshape(depth, MLA_KV_RANK, MLA_QK_WIDTH),
                             wkv[..., MLA_NOPE_DIM:].reshape(depth, MLA_KV_RANK, MLA_WIDTH)], axis=2).astype(BF16)
    wka = jnp.pad(jnp.transpose(wkv[..., :MLA_NOPE_DIM], (0, 2, 3, 1)),
                  ((0, 0), (0, 0), (0, MLA_QK_PAD - MLA_NOPE_DIM), (0, 0))).astype(BF16)
    wv = jnp.transpose(wkv[..., MLA_NOPE_DIM:], (0, 2, 1, 3)).astype(BF16)
    dt = jnp.exp(s5_log_dt)[:, :, None]
    mag = jnp.exp(s5_lambda_re * dt)
    lbr, lbi = mag * jnp.cos(s5_lambda_im * dt), mag * jnp.sin(s5_lambda_im * dt)
    den = s5_lambda_re * s5_lambda_re + s5_lambda_im * s5_lambda_im
    qr = ((lbr - 1.0) * s5_lambda_re + lbi * s5_lambda_im) / den
    qi = (lbi * s5_lambda_re - (lbr - 1.0) * s5_lambda_im) / den
    b_re_t, b_im_t = jnp.swapaxes(s5_b_re, 2, 3), jnp.swapaxes(s5_b_im, 2, 3)
    bbr = qr[:, :, None, :] * b_re_t - qi[:, :, None, :] * b_im_t
    bbi = qr[:, :, None, :] * b_im_t + qi[:, :, None, :] * b_re_t
    rows = lambda m: m.reshape(depth, S5_WIDTH, S5_STATE)
    wb, wc = _s5_weights(rows(bbr), rows(bbi), rows(s5_c_re), rows(s5_c_im))
    return dict(
        ln=ln_gain.reshape(depth, 1, D_MODEL), w_in=w_t, wg=wg, bg=gla_b_gate.reshape(depth, 1, GLA_QK),
        gla_gain=jnp.tile(gla_norm_gain, (1, GLA_HEADS)).reshape(depth, 1, GLA_WIDTH),
        gq=gq, wq=wq, gkv=mla_kv_norm_gain.reshape(depth, 1, MLA_KV_RANK), wkv=wkv_r, wka=wka, wv=wv,
        lre=lbr.reshape(depth, 1, S5_NSTATE), lim=lbi.reshape(depth, 1, S5_NSTATE),
        wb=wb, wc=wc, d=s5_d.reshape(depth, 1, S5_WIDTH), wglu=s5_w_glu.astype(BF16),
        bglu=s5_b_glu.reshape(depth, 1, S5_WIDTH), w_out=w_out.astype(BF16))


def _rope_tables(past, t):
    half = MLA_ROPE_DIM // 2
    inv = ROPE_BASE ** (-np.arange(half, dtype=np.float64) / half)
    ang = (past + np.arange(t, dtype=np.float64))[:, None] * inv[None, :]
    cos, sin = np.cos(ang), np.sin(ang)
    pad = MLA_QK_PAD - MLA_NOPE_DIM - MLA_ROPE_DIM
    cos_t = np.concatenate([np.ones((t, MLA_NOPE_DIM)), cos, cos, np.zeros((t, pad))], axis=1)
    sin_t = np.concatenate([np.zeros((t, MLA_NOPE_DIM)), sin, sin, np.zeros((t, pad))], axis=1)
    return jnp.asarray(cos_t, F32), jnp.asarray(sin_t, F32)


def _trunk(x, gla_state, ckv_cache, kpe_cache, s5_re, s5_im, p, final_gain):
    bsz, t, _ = x.shape
    n = bsz * t
    depth = p['w_in'].shape[0]
    past = 0 if ckv_cache is None else ckv_cache.shape[2]
    s_len = past + t
    tl = _tiles(bsz, t, past)
    cos_t, sin_t = _rope_tables(past, t)
    kpet_cache = None if kpe_cache is None else jnp.swapaxes(kpe_cache, 2, 3)
    x2 = x.reshape(n, D_MODEL)
    gain_f = final_gain.reshape(1, D_MODEL)
    gla_o, ckv_o, kpe_o, re_o, im_o = [], [], [], [], []
    for l in range(depth):
        pg, pm, ps = _inproj(x2, p['ln'], p['w_in'], l, tl['row'])
        s0 = jnp.zeros((bsz, GLA_HEADS, GLA_DK, GLA_DV), F32) if gla_state is None else gla_state[l]
        o_gla, s_new = _gla(pg, p['wg'], p['bg'], p['gla_gain'], l, s0, bsz, t, tl)
        gla_o.append(s_new)
        q, ckv_new, kpe128 = _mla_prepq(pm, cos_t, sin_t, p['gq'], p['wq'], p['gkv'], l, t, tl['prep'])
        ckv_o.append(ckv_new.reshape(bsz, t, MLA_KV_RANK))
        kpe_o.append(kpe128[:, MLA_NOPE_DIM:MLA_NOPE_DIM + MLA_ROPE_DIM].reshape(bsz, t, MLA_ROPE_DIM))
        if past == 0:
            k_cat, v_all = _mla_prepkv(ckv_new, kpe128, p['wkv'], l, bsz, t, tl['prep'])
            o_mla = _attn(q, k_cat, v_all, pm, bsz, t, s_len, past, tl['attn_q'], tl['attn_k'])
        else:
            o_mla = _attn_cached(q, pm, ckv_new, kpe128, ckv_cache, kpet_cache, p['wka'], p['wv'], l, bsz, t)
        x0r = jnp.zeros((bsz, S5_NSTATE), F32) if s5_re is None else s5_re[l].reshape(bsz, S5_NSTATE)
        x0i = jnp.zeros((bsz, S5_NSTATE), F32) if s5_im is None else s5_im[l].reshape(bsz, S5_NSTATE)
        o_s5, xr, xi = _s5(ps.reshape(bsz, t, PS_COLS), x0r, x0i, p['lre'], p['lim'], p['wb'], p['wc'],
                           p['d'], p['wglu'], p['bglu'], l, bsz, t, tl['s5_rows'])
        re_o.append(xr.reshape(bsz, S5_GROUPS, S5_STATE))
        im_o.append(xi.reshape(bsz, S5_GROUPS, S5_STATE))
        x2 = _outproj(x2, o_gla, o_mla, o_s5.reshape(n, S5_WIDTH), p['w_out'], gain_f, l,
                      tl['out_row'], final=(l == depth - 1))
    return (x2.reshape(bsz, t, D_MODEL), jnp.stack(gla_o), jnp.stack(ckv_o), jnp.stack(kpe_o),
            jnp.stack(re_o), jnp.stack(im_o))


def kernel(x_prompt, x_sample, state_gla, cache_mla_ckv, cache_mla_kpe, state_s5_re, state_s5_im, ln_gain, w_in, gla_w_gate, gla_b_gate, gla_norm_gain, mla_q_norm_gain, mla_w_uq, mla_kv_norm_gain, mla_w_ukv, s5_lambda_re, s5_lambda_im, s5_b_re, s5_b_im, s5_c_re, s5_c_im, s5_d, s5_log_dt, s5_w_glu, s5_b_glu, w_out, final_gain):
    p = _prepare_params(ln_gain, w_in, gla_w_gate, gla_b_gate, gla_norm_gain, mla_q_norm_gain, mla_w_uq,
                        mla_kv_norm_gain, mla_w_ukv, s5_lambda_re, s5_lambda_im, s5_b_re, s5_b_im,
                        s5_c_re, s5_c_im, s5_d, s5_log_dt, s5_w_glu, s5_b_glu, w_out)
    y_p, gla_p, ckv_p, kpe_p, re_p, im_p = _trunk(x_prompt, None, None, None, None, None, p, final_gain)
    y_s, gla_s, ckv_s, kpe_s, re_s, im_s = _trunk(x_sample, state_gla, cache_mla_ckv, cache_mla_kpe,
                                                  state_s5_re, state_s5_im, p, final_gain)
    return (y_p, y_s, gla_p, ckv_p, kpe_p, re_p, im_p, gla_s, ckv_s, kpe_s, re_s, im_s)
```

```python
import functools
import math

import numpy as np
import jax
import jax.numpy as jnp
from jax import lax
from jax.experimental import pallas as pl
from jax.experimental.pallas import tpu as pltpu

F32 = jnp.float32
BF16 = jnp.bfloat16

LANES = 128
D_MODEL = 1024
CHUNK = 64
EPS = 1e-6
GLA_HEADS = 4
GLA_DV = 64
GLA_DK = 32
GLA_WIDTH = GLA_HEADS * GLA_DV
GLA_QK = GLA_HEADS * GLA_DK
GLA_GATE_RANK = 16
GLA_GATE_TAU = 16.0
GLA_SUB = 16
GLA_PLAIN_MAX_DECAY = 60.0
MLA_HEADS = 4
MLA_NOPE_DIM = 64
MLA_ROPE_DIM = 32
MLA_V_DIM = 128
MLA_Q_RANK = 192
MLA_KV_RANK = 128
MLA_WIDTH = MLA_HEADS * MLA_V_DIM
MLA_QK_PAD = 128
MLA_QK_WIDTH = MLA_HEADS * MLA_QK_PAD
ROPE_BASE = 10000.0
S5_GROUPS = 16
S5_GROUP_CH = 16
S5_STATE = 64
S5_WIDTH = S5_GROUPS * S5_GROUP_CH
S5_NSTATE = S5_GROUPS * S5_STATE
S5_BATCH_TILE = 8

PG_COLS = 896
PM_COLS = 1024
PS_COLS = 512
_IN_SEGS = (('g_q', GLA_QK), ('g_k', GLA_QK), ('g_v', GLA_WIDTH), ('g_lr', GLA_GATE_RANK), ('g_z', GLA_WIDTH),
            ('m_cq', MLA_Q_RANK), ('m_ckv', MLA_KV_RANK), ('m_kr', MLA_ROPE_DIM), ('m_z', MLA_WIDTH),
            ('s_u', S5_WIDTH), ('s_z', S5_WIDTH), ('end', 0))
IN_OFF = dict(zip([n for n, _ in _IN_SEGS], np.cumsum([0] + [w for _, w in _IN_SEGS[:-1]]).tolist()))

VMEM_LIMIT_BYTES = 48 * 1024 * 1024


def _tiles(bsz, t, past):
    n = bsz * t
    s_len = past + t
    return dict(
        row=min(512, n),
        out_row=min(1024, n),
        prep=min(512, t),
        gla_rows=min(512, t), gla_chunk=min(CHUNK, t),
        attn_q=min(512, t), attn_k=min(512, s_len),
        s5_rows=min(64, t))


def _cparams(*sem):
    return pltpu.CompilerParams(dimension_semantics=sem, vmem_limit_bytes=VMEM_LIMIT_BYTES)


def _sigmoid(x):
    return 0.5 * (1.0 + jnp.tanh(0.5 * x))


def _dot(a, b):
    return jnp.dot(a, b, preferred_element_type=F32)


def _dot_t(a, b):
    return lax.dot_general(a, b, (((0,), (0,)), ((), ())), preferred_element_type=F32)


def _const_spec(shape, ngrid):
    zeros = (0,) * len(shape)
    return pl.BlockSpec(shape, lambda *_: zeros)


def _layer_spec(shape, layer, single_buffer=False):
    zeros = (0,) * len(shape)
    mode = dict(pipeline_mode=pl.Buffered(1)) if single_buffer else {}
    return pl.BlockSpec((None,) + tuple(shape), lambda *_: (layer,) + zeros, **mode)


def _inproj_rows(x, g_ref, w_scr, og_ref, om_ref, os_ref):
    ms = jnp.mean(x * x, axis=-1, keepdims=True)
    h = (x * lax.rsqrt(ms + EPS) * g_ref[...]).astype(BF16)

    def seg(a, b):
        return lax.dot_general(h, w_scr[a:b, :], (((1,), (1,)), ((), ())), preferred_element_type=F32)

    lane = lax.broadcasted_iota(jnp.int32, (x.shape[0], LANES), 1)
    lane2 = lax.broadcasted_iota(jnp.int32, (x.shape[0], 2 * LANES), 1)
    c = IN_OFF
    og_ref[:, 0:512] = seg(c['g_q'], c['g_lr']).astype(BF16)
    og_ref[:, 512:768] = seg(c['g_z'], c['m_cq']).astype(BF16)
    og_ref[:, 768:896] = jnp.where(lane < GLA_GATE_RANK, seg(c['g_lr'], c['g_lr'] + LANES), 0.0).astype(BF16)
    om_ref[:, 0:256] = jnp.where(lane2 < MLA_Q_RANK, seg(c['m_cq'], c['m_cq'] + 2 * LANES), 0.0).astype(BF16)
    om_ref[:, 256:384] = seg(c['m_ckv'], c['m_kr']).astype(BF16)
    kr = seg(c['m_kr'] - MLA_NOPE_DIM, c['m_kr'] - MLA_NOPE_DIM + LANES)
    om_ref[:, 384:512] = jnp.where((lane >= MLA_NOPE_DIM) & (lane < MLA_NOPE_DIM + MLA_ROPE_DIM), kr, 0.0).astype(BF16)
    om_ref[:, 512:1024] = seg(c['m_z'], c['s_u']).astype(BF16)
    os_ref[...] = seg(c['s_u'], c['end']).astype(BF16)


def _inproj_body(x_ref, g_ref, wt_ref, og_ref, om_ref, os_ref, w_scr):
    @pl.when(pl.program_id(0) == 0)
    def _():
        w_scr[...] = wt_ref[...].astype(BF16)

    _inproj_rows(x_ref[...], g_ref, w_scr, og_ref, om_ref, os_ref)


def _inproj(x2, gain, wt, layer, tm):
    n = x2.shape[0]
    return pl.pallas_call(
        _inproj_body,
        grid=(n // tm,),
        in_specs=[pl.BlockSpec((tm, D_MODEL), lambda i: (i, 0)),
                  _layer_spec((1, D_MODEL), layer),
                  _layer_spec((IN_OFF['end'], D_MODEL), layer, single_buffer=True)],
        scratch_shapes=[pltpu.VMEM((IN_OFF['end'], D_MODEL), BF16)],
        out_specs=[pl.BlockSpec((tm, PG_COLS), lambda i: (i, 0)),
                   pl.BlockSpec((tm, PM_COLS), lambda i: (i, 0)),
                   pl.BlockSpec((tm, PS_COLS), lambda i: (i, 0))],
        out_shape=[jax.ShapeDtypeStruct((n, PG_COLS), BF16),
                   jax.ShapeDtypeStruct((n, PM_COLS), BF16),
                   jax.ShapeDtypeStruct((n, PS_COLS), BF16)],
        compiler_params=_cparams("arbitrary"),
        name="inproj",
    )(x2, gain, wt)


def _gla_body(p_ref, wg_ref, bg_ref, gain_ref, tri_ref, ones_k_ref, ones_v_ref, bd_ref, s0_ref,
              o_ref, sout_ref, q_scr, k_scr, v_scr, b_scr, s_scr, qe_scr, qs_scr, ke_scr, vb_scr, o_scr,
              *, chunk, nch):
    it = pl.program_id(1)
    sub = min(GLA_SUB, chunk)
    nsub = chunk // sub

    @pl.when(it == 0)
    def _():
        s_scr[...] = jnp.zeros(s_scr.shape, F32)
        for h in range(GLA_HEADS):
            s_scr[h * GLA_DK:(h + 1) * GLA_DK, h * GLA_DV:(h + 1) * GLA_DV] = s0_ref[0, h]

    q_scr[...] = p_ref[:, 0:128].astype(F32) * (GLA_DK ** -0.5)
    k_scr[...] = p_ref[:, 128:256].astype(F32)
    v_scr[...] = p_ref[:, 256:512].astype(F32)
    logit = _dot(p_ref[:, 768:896], wg_ref[...]) + bg_ref[...]
    log_a = (jnp.minimum(logit, 0.0) - jnp.log(1.0 + jnp.exp(-jnp.abs(logit)))) * (1.0 / GLA_GATE_TAU)
    la_hi = log_a.astype(BF16)
    la_lo = (log_a - la_hi.astype(F32)).astype(BF16)
    for c in range(nch):
        rs = slice(c * chunk, (c + 1) * chunk)
        b_scr[rs, :] = _dot(tri_ref[...], la_hi[rs]) + _dot(tri_ref[...], la_lo[rs])

    row = lax.broadcasted_iota(jnp.int32, (chunk, GLA_QK), 0)
    row_in_sub = lax.broadcasted_iota(jnp.int32, (sub, GLA_QK), 0)

    def decay_columns(b_end):
        col = jnp.transpose(jnp.broadcast_to(jnp.exp(b_end), (GLA_QK, GLA_QK)))
        return jnp.concatenate([col, col], axis=1)

    def load_chunk(c):
        r0 = pl.multiple_of(c * chunk, chunk)
        return (r0, q_scr[pl.ds(r0, chunk), :], k_scr[pl.ds(r0, chunk), :], v_scr[pl.ds(r0, chunk), :],
                b_scr[pl.ds(r0, chunk), :], b_scr[pl.ds(r0 + chunk - 1, 1), :], s_scr[...])

    def finish_chunk(r0, o, s_prev, a_state, b_end):
        ms = _dot((o * o).astype(BF16), ones_v_ref[...]) * (1.0 / GLA_DV)
        o_n = o * lax.rsqrt(ms + EPS) * gain_ref[...]
        z = p_ref[pl.ds(r0, chunk), 512:768].astype(F32)
        o_ref[pl.ds(r0, chunk), :] = (o_n * (z * _sigmoid(z))).astype(BF16)
        s_scr[...] = s_prev * decay_columns(b_end) + a_state

    def robust_chunk(c, carry):
        r0, qc, kc, vc, bc, b_end, s_prev = load_chunk(c)
        xs = [qc * jnp.exp(bc)]
        ks = []
        for sj in range(nsub - 1):
            e_j = b_scr[pl.ds(r0 + (sj + 1) * sub - 1, 1), :]
            later = row >= (sj + 1) * sub
            xs.append(jnp.where(later, qc * jnp.exp(jnp.where(later, bc - e_j, 0.0)), 0.0))
            own = (row >= sj * sub) & (row < (sj + 1) * sub)
            ks.append(jnp.where(own, kc * jnp.exp(jnp.where(own, e_j - bc, 0.0)), 0.0))
        ks.append(kc * jnp.exp(b_end - bc))
        k_all = jnp.concatenate(ks, axis=1).astype(BF16)
        a_all = _dot_t(k_all, vc.astype(BF16)) * bd_ref[...]
        w = jnp.concatenate([s_prev, a_all[:(nsub - 1) * GLA_QK]], axis=0).astype(BF16) if nsub > 1 \
            else s_prev.astype(BF16)
        o_off = _dot(jnp.concatenate(xs, axis=1).astype(BF16), w)
        rows = []
        for si in range(nsub):
            q_i = qc[si * sub:(si + 1) * sub]
            b_i = bc[si * sub:(si + 1) * sub]
            es = []
            for j in range(sub):
                r = r0 + si * sub + j
                b_j = b_scr[pl.ds(r, 1), :]
                k_j = k_scr[pl.ds(r, 1), :]
                valid = row_in_sub >= j
                es.append(q_i * k_j * jnp.exp(jnp.where(valid, b_i - b_j, -jnp.inf)))
            e_all = jnp.concatenate(es, axis=0)
            e_hi = e_all.astype(BF16)
            e_lo = (e_all - e_hi.astype(F32)).astype(BF16)
            p_all = _dot(e_hi, ones_k_ref[...]) + _dot(e_lo, ones_k_ref[...])
            acc = o_off[si * sub:(si + 1) * sub]
            for j in range(sub):
                v_j = v_scr[pl.ds(r0 + si * sub + j, 1), :]
                acc = acc + p_all[j * sub:(j + 1) * sub] * v_j
            rows.append(acc)
        o = rows[0] if nsub == 1 else jnp.concatenate(rows, axis=0)
        finish_chunk(r0, o, s_prev, a_all[(nsub - 1) * GLA_QK:], b_end)
        return carry

    lane_head_v = lax.broadcasted_iota(jnp.int32, (chunk, GLA_WIDTH), 1) // GLA_DV
    causal = (lax.broadcasted_iota(jnp.int32, (GLA_HEADS * chunk, chunk), 0) % chunk
              >= lax.broadcasted_iota(jnp.int32, (GLA_HEADS * chunk, chunk), 1))

    def plain_block():
        tt = nch * chunk
        b_all = b_scr[...]
        q_all = q_scr[...]
        k_all = k_scr[...]
        qe = q_all * jnp.exp(b_all)
        lane_head = lax.broadcasted_iota(jnp.int32, (tt, GLA_QK), 1) // GLA_DK
        qe_scr[...] = qe.astype(BF16)
        for h in range(GLA_HEADS):
            qs_scr[h] = jnp.where(lane_head == h, qe, 0.0).astype(BF16)
        ke_scr[...] = (k_all * jnp.exp(-b_all)).astype(BF16)
        vb_scr[...] = p_ref[:, 256:512]
        s_cur = s_scr[...]
        for c in range(nch):
            rs = slice(c * chunk, (c + 1) * chunk)
            qs = jnp.concatenate([qs_scr[h, rs, :] for h in range(GLA_HEADS)], axis=0)
            s = lax.dot_general(qs, ke_scr[rs, :], (((1,), (1,)), ((), ())), preferred_element_type=F32)
            s = jnp.where(causal, s, 0.0).astype(BF16)
            r = _dot(s, vb_scr[rs, :])
            o = _dot(qe_scr[rs, :], s_cur.astype(BF16))
            for h in range(GLA_HEADS):
                o = o + jnp.where(lane_head_v == h, r[h * chunk:(h + 1) * chunk], 0.0)
            o_scr[rs, :] = o
            b_end = b_scr[(c + 1) * chunk - 1:(c + 1) * chunk, :]
            k_end = (k_scr[rs, :] * jnp.exp(b_end - b_scr[rs, :])).astype(BF16)
            a_state = _dot_t(k_end, vb_scr[rs, :]) * bd_ref[0:GLA_QK, :]
            s_cur = s_cur * decay_columns(b_end) + a_state
        s_scr[...] = s_cur
        o = o_scr[...]
        ms = _dot((o * o).astype(BF16), ones_v_ref[...]) * (1.0 / GLA_DV)
        o_n = o * lax.rsqrt(ms + EPS) * gain_ref[...]
        z = p_ref[:, 512:768].astype(F32)
        o_ref[...] = (o_n * (z * _sigmoid(z))).astype(BF16)

    in_range = jnp.max(-b_scr[...]) < GLA_PLAIN_MAX_DECAY

    @pl.when(in_range)
    def _():
        plain_block()

    @pl.when(jnp.logical_not(in_range))
    def _():
        lax.fori_loop(0, nch, robust_chunk, 0)

    @pl.when(it == pl.num_programs(1) - 1)
    def _():
        for h in range(GLA_HEADS):
            sout_ref[0, h] = s_scr[h * GLA_DK:(h + 1) * GLA_DK, h * GLA_DV:(h + 1) * GLA_DV]


def _gla_consts(chunk):
    nsub = chunk // min(GLA_SUB, chunk)
    r = np.arange(chunk)
    tri = (r[None, :] <= r[:, None]).astype(np.float32)
    hk = np.arange(GLA_QK) // GLA_DK
    hv = np.arange(GLA_WIDTH) // GLA_DV
    same_kv = (hk[:, None] == hv[None, :]).astype(np.float32)
    same_vv = (hv[:, None] == hv[None, :]).astype(np.float32)
    return (jnp.asarray(tri, BF16), jnp.asarray(same_kv, BF16), jnp.asarray(same_vv, BF16),
            jnp.asarray(np.tile(same_kv, (nsub, 1)), F32))


def _gla(pg, wg, bg, gain, layer, s0_bd, bsz, t, tl):
    chunk, tt = tl['gla_chunk'], tl['gla_rows']
    nch = tt // chunk
    nt = t // tt
    nsub = chunk // min(GLA_SUB, chunk)
    tri, ones_k, ones_v, bd = _gla_consts(chunk)
    const = lambda shape: _const_spec(shape, 2)
    return pl.pallas_call(
        functools.partial(_gla_body, chunk=chunk, nch=nch),
        grid=(bsz, nt),
        in_specs=[pl.BlockSpec((tt, PG_COLS), lambda b, i: (b * nt + i, 0)),
                  _layer_spec((GLA_QK, GLA_QK), layer), _layer_spec((1, GLA_QK), layer),
                  _layer_spec((1, GLA_WIDTH), layer),
                  const((chunk, chunk)), const((GLA_QK, GLA_WIDTH)), const((GLA_WIDTH, GLA_WIDTH)),
                  const((nsub * GLA_QK, GLA_WIDTH)),
                  pl.BlockSpec((1, GLA_HEADS, GLA_DK, GLA_DV), lambda b, i: (b, 0, 0, 0))],
        out_specs=[pl.BlockSpec((tt, GLA_WIDTH), lambda b, i: (b * nt + i, 0)),
                   pl.BlockSpec((1, GLA_HEADS, GLA_DK, GLA_DV), lambda b, i: (b, 0, 0, 0))],
        out_shape=[jax.ShapeDtypeStruct((bsz * t, GLA_WIDTH), BF16),
                   jax.ShapeDtypeStruct((bsz, GLA_HEADS, GLA_DK, GLA_DV), F32)],
        scratch_shapes=[pltpu.VMEM((tt, GLA_QK), F32), pltpu.VMEM((tt, GLA_QK), F32),
                        pltpu.VMEM((tt, GLA_WIDTH), F32), pltpu.VMEM((tt, GLA_QK), F32),
                        pltpu.VMEM((GLA_QK, GLA_WIDTH), F32),
                        pltpu.VMEM((tt, GLA_QK), BF16), pltpu.VMEM((GLA_HEADS, tt, GLA_QK), BF16),
                        pltpu.VMEM((tt, GLA_QK), BF16),
                        pltpu.VMEM((tt, GLA_WIDTH), BF16), pltpu.VMEM((tt, GLA_WIDTH), F32)],
        compiler_params=_cparams("parallel", "arbitrary"),
        name="gla",
    )(pg, wg, bg, gain, tri, ones_k, ones_v, bd, s0_bd)


def _rope128(x, cos_t, sin_t):
    lane = lax.broadcasted_iota(jnp.int32, x.shape, 1)
    first_half = (lane >= MLA_NOPE_DIM) & (lane < MLA_NOPE_DIM + MLA_ROPE_DIM // 2)
    rot = jnp.where(first_half, -pltpu.roll(x, LANES - MLA_ROPE_DIM // 2, 1), pltpu.roll(x, MLA_ROPE_DIM // 2, 1))
    return x * cos_t + rot * sin_t


def _mla_prepq_body(p_ref, cos_ref, sin_ref, gq_ref, wq_ref, gkv_ref, q_ref, ckv_ref, kpe_ref):
    cos_t = cos_ref[...]
    sin_t = sin_ref[...]
    cq = p_ref[:, 0:256].astype(F32)
    ms = jnp.sum(cq * cq, axis=-1, keepdims=True) * (1.0 / MLA_Q_RANK)
    cqn = (cq * lax.rsqrt(ms + EPS) * gq_ref[...]).astype(BF16)
    qh = _dot(cqn, wq_ref[...])
    scale = (MLA_NOPE_DIM + MLA_ROPE_DIM) ** -0.5 * math.log2(math.e)
    for h in range(MLA_HEADS):
        x = qh[:, h * MLA_QK_PAD:(h + 1) * MLA_QK_PAD]
        q_ref[:, h * MLA_QK_PAD:(h + 1) * MLA_QK_PAD] = (_rope128(x, cos_t, sin_t) * scale).astype(BF16)
    ckv = p_ref[:, 256:384].astype(F32)
    ms = jnp.mean(ckv * ckv, axis=-1, keepdims=True)
    ckv_ref[...] = ckv * lax.rsqrt(ms + EPS) * gkv_ref[...]
    kr = p_ref[:, 384:512].astype(F32)
    kpe_ref[...] = _rope128(kr, cos_t, sin_t)


def _mla_prepq(pm, cos_t, sin_t, gq, wq, gkv, layer, t, tm):
    n = pm.shape[0]
    ntab = t // tm
    return pl.pallas_call(
        _mla_prepq_body,
        grid=(n // tm,),
        in_specs=[pl.BlockSpec((tm, 512), lambda i: (i, 0)),
                  pl.BlockSpec((tm, LANES), lambda i: (i % ntab, 0)),
                  pl.BlockSpec((tm, LANES), lambda i: (i % ntab, 0)),
                  _layer_spec((1, 256), layer), _layer_spec((256, MLA_QK_WIDTH), layer),
                  _layer_spec((1, MLA_KV_RANK), layer)],
        out_specs=[pl.BlockSpec((tm, MLA_QK_WIDTH), lambda i: (i, 0)),
                   pl.BlockSpec((tm, MLA_KV_RANK), lambda i: (i, 0)),
                   pl.BlockSpec((tm, LANES), lambda i: (i, 0))],
        out_shape=[jax.ShapeDtypeStruct((n, MLA_QK_WIDTH), BF16),
                   jax.ShapeDtypeStruct((n, MLA_KV_RANK), F32),
                   jax.ShapeDtypeStruct((n, LANES), F32)],
        compiler_params=_cparams("parallel"),
        name="mla_prepq",
    )(pm, cos_t, sin_t, gq, wq, gkv)


def _mla_kv_rows(ckv, kpe128, wkv_ref, k_ref, v_ref, rows):
    kv = _dot(ckv.astype(BF16), wkv_ref[...])
    for h in range(MLA_HEADS):
        sl = slice(h * MLA_QK_PAD, (h + 1) * MLA_QK_PAD)
        k_ref[rows, sl] = (kv[:, sl] + kpe128).astype(BF16)
    v_ref[rows, :] = kv[:, MLA_QK_WIDTH:].astype(BF16)


def _mla_prepkv_body(ckv_ref, kpe_ref, wkv_ref, k_ref, v_ref):
    _mla_kv_rows(ckv_ref[...], kpe_ref[...], wkv_ref, k_ref, v_ref, slice(None))


def _mla_prepkv(ckv, kpe128, wkv, layer, bsz, t, tm):
    n = bsz * t
    return pl.pallas_call(
        _mla_prepkv_body,
        grid=(n // tm,),
        in_specs=[pl.BlockSpec((tm, MLA_KV_RANK), lambda i: (i, 0)),
                  pl.BlockSpec((tm, LANES), lambda i: (i, 0)),
                  _layer_spec((MLA_KV_RANK, MLA_QK_WIDTH + MLA_WIDTH), layer)],
        out_specs=[pl.BlockSpec((tm, MLA_QK_WIDTH), lambda i: (i, 0)),
                   pl.BlockSpec((tm, MLA_WIDTH), lambda i: (i, 0))],
        out_shape=[jax.ShapeDtypeStruct((n, MLA_QK_WIDTH), BF16), jax.ShapeDtypeStruct((n, MLA_WIDTH), BF16)],
        compiler_params=_cparams("parallel"),
        name="mla_prepkv",
    )(ckv, kpe128, wkv)


def _attn_cached_body(q_ref, z_ref, ckv_new_ref, kpe_new_ref, ckv_past_ref, kpet_past_ref, wka_ref, wv_ref,
                      o_ref, kpet_scr, *, past, t):
    last = (((1,), (1,)), ((), ()))
    heads = [q_ref[:, h * MLA_QK_PAD:(h + 1) * MLA_QK_PAD] for h in range(MLA_HEADS)]
    q_rows = jnp.concatenate(heads, axis=0)
    q_lat = jnp.concatenate([_dot(heads[h], wka_ref[h]) for h in range(MLA_HEADS)], axis=0).astype(BF16)
    c_past = ckv_past_ref[...].astype(BF16)
    c_new = ckv_new_ref[...].astype(BF16)
    kpet_scr[...] = jnp.zeros(kpet_scr.shape, BF16)
    kpet_scr[MLA_NOPE_DIM:MLA_NOPE_DIM + MLA_ROPE_DIM, :] = kpet_past_ref[...].astype(BF16)
    s_past = (lax.dot_general(q_lat, c_past, last, preferred_element_type=F32)
              + _dot(q_rows, kpet_scr[...]))
    s_new = (lax.dot_general(q_lat, c_new, last, preferred_element_type=F32)
             + lax.dot_general(q_rows, kpe_new_ref[...].astype(BF16), last, preferred_element_type=F32))
    if past // CHUNK != (past + t - 1) // CHUNK:
        q_chunk = (past + lax.broadcasted_iota(jnp.int32, s_past.shape, 0) % t) // CHUNK
        s_past = jnp.where(lax.broadcasted_iota(jnp.int32, s_past.shape, 1) // CHUNK <= q_chunk, s_past, -jnp.inf)
        q_chunk = (past + lax.broadcasted_iota(jnp.int32, s_new.shape, 0) % t) // CHUNK
        s_new = jnp.where((past + lax.broadcasted_iota(jnp.int32, s_new.shape, 1)) // CHUNK <= q_chunk,
                          s_new, -jnp.inf)
    m = jnp.maximum(jnp.max(s_past, axis=-1, keepdims=True), jnp.max(s_new, axis=-1, keepdims=True))
    p_past = jnp.exp2(s_past - m)
    p_new = jnp.exp2(s_new - m)
    l = jnp.sum(p_past, axis=-1, keepdims=True) + jnp.sum(p_new, axis=-1, keepdims=True)
    o_lat = ((_dot(p_past.astype(BF16), c_past) + _dot(p_new.astype(BF16), c_new)) / l).astype(BF16)
    for h in range(MLA_HEADS):
        vsl = slice(h * MLA_V_DIM, (h + 1) * MLA_V_DIM)
        z = z_ref[:, vsl].astype(F32)
        o_ref[:, vsl] = (_dot(o_lat[h * t:(h + 1) * t], wv_ref[h]) * (z * _sigmoid(z))).astype(BF16)


def _attn_cached(q, pm, ckv_new, kpe128, ckv_past, kpet_past, wka, wv, layer, bsz, t):
    past = ckv_past.shape[2]
    return pl.pallas_call(
        functools.partial(_attn_cached_body, past=past, t=t),
        grid=(bsz,),
        in_specs=[pl.BlockSpec((t, MLA_QK_WIDTH), lambda b: (b, 0)),
                  pl.BlockSpec((t, MLA_WIDTH), lambda b: (b, 1)),
                  pl.BlockSpec((t, MLA_KV_RANK), lambda b: (b, 0)),
                  pl.BlockSpec((t, LANES), lambda b: (b, 0)),
                  pl.BlockSpec((None, None, past, MLA_KV_RANK), lambda b: (layer, b, 0, 0)),
                  pl.BlockSpec((None, None, MLA_ROPE_DIM, past), lambda b: (layer, b, 0, 0)),
                  _layer_spec((MLA_HEADS, MLA_QK_PAD, MLA_KV_RANK), layer),
                  _layer_spec((MLA_HEADS, MLA_KV_RANK, MLA_V_DIM), layer)],
        out_specs=pl.BlockSpec((t, MLA_WIDTH), lambda b: (b, 0)),
        out_shape=jax.ShapeDtypeStruct((bsz * t, MLA_WIDTH), BF16),
        scratch_shapes=[pltpu.VMEM((MLA_QK_PAD, past), BF16)],
        compiler_params=_cparams("parallel"),
        name="mla_attn_cached",
    )(q, pm, ckv_new, kpe128, ckv_past, kpet_past, wka, wv)


def _attn_body(q_ref, k_ref, v_ref, z_ref, o_ref, m_scr, acc_scr, *, past, tq, tk, s_len):
    iq = pl.program_id(1)
    q_first = past + iq * tq
    full_keys = jnp.minimum((q_first // CHUNK + 1) * CHUNK, s_len)
    vis_keys = jnp.minimum(((q_first + tq - 1) // CHUNK + 1) * CHUNK, s_len)
    n_full = full_keys // tk
    n_vis = (vis_keys + tk - 1) // tk
    m_scr[...] = jnp.full(m_scr.shape, -jnp.inf, F32)
    acc_scr[...] = jnp.zeros(acc_scr.shape, F32)
    split_diagonal = past % tk == 0 and tq == tk and (tq // 2) % CHUNK == 0

    def block(kb, carry, masked):
        k0 = pl.multiple_of(kb * tk, tk)
        if masked and split_diagonal:
            parts = [(r * (tq // 2), tq // 2, (r + 1) * (tk // 2)) for r in range(2)]
        else:
            parts = [(0, tq, tk)]
        for r0, nr, kext in parts:
            rows = slice(r0, r0 + nr)
            if masked:
                q_pos = (r0 if split_diagonal else q_first + r0) + lax.broadcasted_iota(jnp.int32, (nr, kext), 0)
                k_pos = (0 if split_diagonal else k0) + lax.broadcasted_iota(jnp.int32, (nr, kext), 1)
                visible = k_pos // CHUNK <= q_pos // CHUNK
            ntile, rem = kext // LANES, kext % LANES
            ones_v = jnp.ones((kext, MLA_V_DIM), BF16)
            for h in range(MLA_HEADS):
                sl = slice(h * MLA_QK_PAD, (h + 1) * MLA_QK_PAD)
                vsl = slice(h * MLA_V_DIM, (h + 1) * MLA_V_DIM)
                s = lax.dot_general(q_ref[rows, sl], k_ref[pl.ds(k0, kext), sl], (((1,), (1,)), ((), ())),
                                    preferred_element_type=F32)
                if masked:
                    s = jnp.where(visible, s, -jnp.inf)
                m_prev = m_scr[h, rows]
                m_new = jnp.maximum(m_prev, jnp.max(s, axis=-1, keepdims=True))
                alpha = jnp.exp2(m_prev - m_new)
                ps = [jnp.exp2(s[:, c * LANES:(c + 1) * LANES] - m_new) for c in range(ntile)]
                if rem:
                    ps.append(jnp.exp2(s[:, ntile * LANES:] - m_new[:, :rem]))
                p = jnp.concatenate(ps, axis=1).astype(BF16)
                v_ext = jnp.concatenate([v_ref[pl.ds(k0, kext), vsl], ones_v], axis=1)
                acc_scr[h, rows] = jnp.concatenate([alpha, alpha], axis=1) * acc_scr[h, rows] + _dot(p, v_ext)
                m_scr[h, rows] = m_new
        return carry

    lax.fori_loop(0, n_full, functools.partial(block, masked=False), 0)
    lax.fori_loop(n_full, n_vis, functools.partial(block, masked=True), 0)
    for h in range(MLA_HEADS):
        vsl = slice(h * MLA_V_DIM, (h + 1) * MLA_V_DIM)
        z = z_ref[:, vsl].astype(F32)
        acc = acc_scr[h]
        o_ref[:, vsl] = (acc[:, :MLA_V_DIM] / acc[:, MLA_V_DIM:] * (z * _sigmoid(z))).astype(BF16)


def _attn(q, k, v, pm, bsz, t, s_len, past, tq, tk):
    nq = t // tq
    return pl.pallas_call(
        functools.partial(_attn_body, past=past, tq=tq, tk=tk, s_len=s_len),
        grid=(bsz, nq),
        in_specs=[pl.BlockSpec((tq, MLA_QK_WIDTH), lambda b, iq: (b * nq + iq, 0)),
                  pl.BlockSpec((s_len, MLA_QK_WIDTH), lambda b, iq: (b, 0)),
                  pl.BlockSpec((s_len, MLA_WIDTH), lambda b, iq: (b, 0)),
                  pl.BlockSpec((tq, MLA_WIDTH), lambda b, iq: (b * nq + iq, 1))],
        out_specs=pl.BlockSpec((tq, MLA_WIDTH), lambda b, iq: (b * nq + iq, 0)),
        out_shape=jax.ShapeDtypeStruct((bsz * t, MLA_WIDTH), BF16),
        scratch_shapes=[pltpu.VMEM((MLA_HEADS, tq, LANES), F32), pltpu.VMEM((MLA_HEADS, tq, 2 * MLA_V_DIM), F32)],
        compiler_params=_cparams("parallel", "arbitrary"),
        name="mla_attn",
    )(q, k, v, pm)


def _s5_weights_body(bre_ref, bim_ref, cre_ref, cim_ref, spread_ref, mask_ref, wb_ref, wc_ref):
    def expand(m_ref):
        return _dot(m_ref[...].astype(BF16), spread_ref[...]) * mask_ref[...]

    wb_ref[:, 0:S5_NSTATE] = expand(bre_ref).astype(BF16)
    wb_ref[:, S5_NSTATE:] = expand(bim_ref).astype(BF16)
    wc_ref[0:S5_NSTATE, :] = jnp.transpose(expand(cre_ref)).astype(BF16)
    wc_ref[S5_NSTATE:, :] = jnp.transpose(-expand(cim_ref)).astype(BF16)


def _s5_weights(bbr, bbi, c_re, c_im):
    depth = bbr.shape[0]
    g_row = np.arange(S5_WIDTH) // S5_GROUP_CH
    g_col = np.arange(S5_NSTATE) // S5_STATE
    mask = (g_row[:, None] == g_col[None, :]).astype(np.float32)
    spread = (np.arange(S5_STATE)[:, None] == (np.arange(S5_NSTATE) % S5_STATE)[None, :]).astype(np.float32)
    small = pl.BlockSpec((None, S5_WIDTH, S5_STATE), lambda l: (l, 0, 0))
    return pl.pallas_call(
        _s5_weights_body,
        grid=(depth,),
        in_specs=[small, small, small, small,
                  _const_spec((S5_STATE, S5_NSTATE), 1), _const_spec((S5_WIDTH, S5_NSTATE), 1)],
        out_specs=[pl.BlockSpec((None, S5_WIDTH, 2 * S5_NSTATE), lambda l: (l, 0, 0)),
                   pl.BlockSpec((None, 2 * S5_NSTATE, S5_WIDTH), lambda l: (l, 0, 0))],
        out_shape=[jax.ShapeDtypeStruct((depth, S5_WIDTH, 2 * S5_NSTATE), BF16),
                   jax.ShapeDtypeStruct((depth, 2 * S5_NSTATE, S5_WIDTH), BF16)],
        compiler_params=_cparams("parallel"),
        name="s5_weights",
    )(bbr, bbi, c_re, c_im, jnp.asarray(spread, BF16), jnp.asarray(mask, F32))


def _s5_body(p0_ref, pn_ref, x0r_ref, x0i_ref, lre_ref, lim_ref, wb_ref, wc_ref, d_ref, wglu_ref, bglu_ref,
             o_ref, xr_out, xi_out, uz_bt, uz_a, uz_b, uz_c, bu_a, bu_b, bu_c, o_tb, xr_s, xi_s, *, lc, pitch):
    it = pl.program_id(1)
    nb = S5_BATCH_TILE
    ring = ((uz_a, bu_a), (uz_b, bu_b), (uz_c, bu_c))

    def stage_in(blk_ref, uz_tb, bu):
        for b in range(nb):
            for c in range(PS_COLS // LANES):
                uz_bt[c, b * pitch:b * pitch + lc, :] = blk_ref[b, :, c * LANES:(c + 1) * LANES].astype(F32)
        for t in range(lc):
            for c in range(PS_COLS // LANES):
                uz_tb[t * nb:(t + 1) * nb, c * LANES:(c + 1) * LANES] = uz_bt[c, pl.ds(t, nb, stride=pitch), :]
        bu[...] = _dot(uz_tb[:, 0:S5_WIDTH].astype(BF16), wb_ref[...])

    def stage_scan(bu):
        lre = jnp.broadcast_to(lre_ref[...], (nb, S5_NSTATE))
        lim = jnp.broadcast_to(lim_ref[...], (nb, S5_NSTATE))
        xr, xi = xr_s[...], xi_s[...]
        for t in range(lc):
            rs = slice(t * nb, (t + 1) * nb)
            xr, xi = (lre * xr - lim * xi + bu[rs, 0:S5_NSTATE],
                      lre * xi + lim * xr + bu[rs, S5_NSTATE:2 * S5_NSTATE])
            bu[rs, 0:S5_NSTATE] = xr
            bu[rs, S5_NSTATE:2 * S5_NSTATE] = xi
        xr_s[...] = xr
        xi_s[...] = xi

    def stage_out(uz_tb, xs):
        y = _dot(xs[...].astype(BF16), wc_ref[...]) + d_ref[...] * uz_tb[:, 0:S5_WIDTH]
        g5 = 0.5 * y * (1.0 + jnp.tanh(0.7978845608028654 * (y + 0.044715 * (y * y * y))))
        gate = _sigmoid(_dot(g5.astype(BF16), wglu_ref[...]) + bglu_ref[...])
        z = uz_tb[:, S5_WIDTH:2 * S5_WIDTH]
        o = g5 * gate * (z * _sigmoid(z))
        for c in range(S5_WIDTH // LANES):
            o_tb[c] = o[:, c * LANES:(c + 1) * LANES]
        for b in range(nb):
            for c in range(S5_WIDTH // LANES):
                o_ref[b, :, c * LANES:(c + 1) * LANES] = o_tb[c, pl.ds(b, lc, stride=nb), :].astype(BF16)

    @pl.when(it == 0)
    def _():
        xr_s[...] = x0r_ref[...]
        xi_s[...] = x0i_ref[...]
        stage_in(p0_ref, uz_a, bu_a)
        uz_c[...] = jnp.zeros(uz_c.shape, F32)
        bu_c[...] = jnp.zeros(bu_c.shape, F32)

    for r in range(3):
        @pl.when(it % 3 == r)
        def _(r=r):
            stage_in(pn_ref, *ring[(r + 1) % 3])
            stage_scan(ring[r][1])
            stage_out(*ring[(r + 2) % 3])

    @pl.when(it == pl.num_programs(1) - 2)
    def _():
        xr_out[...] = xr_s[...]
        xi_out[...] = xi_s[...]


def _s5(ps3, x0r, x0i, lre, lim, wb, wc, d, wglu, bglu, layer, bsz, t, lc):
    nb = S5_BATCH_TILE
    nt = t // lc
    pitch = lc + 8
    rows = lc * nb
    return pl.pallas_call(
        functools.partial(_s5_body, lc=lc, pitch=pitch),
        grid=(bsz // nb, nt + 1),
        in_specs=[pl.BlockSpec((nb, lc, PS_COLS), lambda g, i: (g, 0, 0)),
                  pl.BlockSpec((nb, lc, PS_COLS), lambda g, i: (g, jnp.minimum(i + 1, nt - 1), 0)),
                  pl.BlockSpec((nb, S5_NSTATE), lambda g, i: (g, 0)),
                  pl.BlockSpec((nb, S5_NSTATE), lambda g, i: (g, 0)),
                  _layer_spec((1, S5_NSTATE), layer), _layer_spec((1, S5_NSTATE), layer),
                  _layer_spec((S5_WIDTH, 2 * S5_NSTATE), layer), _layer_spec((2 * S5_NSTATE, S5_WIDTH), layer),
                  _layer_spec((1, S5_WIDTH), layer), _layer_spec((S5_WIDTH, S5_WIDTH), layer),
                  _layer_spec((1, S5_WIDTH), layer)],
        out_specs=[pl.BlockSpec((nb, lc, S5_WIDTH), lambda g, i: (g, jnp.maximum(i - 1, 0), 0)),
                   pl.BlockSpec((nb, S5_NSTATE), lambda g, i: (g, 0)),
                   pl.BlockSpec((nb, S5_NSTATE), lambda g, i: (g, 0))],
        out_shape=[jax.ShapeDtypeStruct((bsz, t, S5_WIDTH), BF16),
                   jax.ShapeDtypeStruct((bsz, S5_NSTATE), F32),
                   jax.ShapeDtypeStruct((bsz, S5_NSTATE), F32)],
        scratch_shapes=[pltpu.VMEM((PS_COLS // LANES, nb * pitch, LANES), F32)]
                       + [pltpu.VMEM((rows, PS_COLS), F32)] * 3
                       + [pltpu.VMEM((rows, 2 * S5_NSTATE), F32)] * 3
                       + [pltpu.VMEM((S5_WIDTH // LANES, rows, LANES), F32),
                          pltpu.VMEM((nb, S5_NSTATE), F32), pltpu.VMEM((nb, S5_NSTATE), F32)],
        compiler_params=_cparams("parallel", "arbitrary"),
        name="s5",
    )(ps3, ps3, x0r, x0i, lre, lim, wb, wc, d, wglu, bglu)


def _outproj_body(x_ref, og_ref, om_ref, os_ref, w_ref, g_ref, o_ref, *, final):
    acc = _dot(og_ref[...], w_ref[0:GLA_WIDTH, :])
    acc += _dot(om_ref[...], w_ref[GLA_WIDTH:GLA_WIDTH + MLA_WIDTH, :])
    acc += _dot(os_ref[...], w_ref[GLA_WIDTH + MLA_WIDTH:, :])
    xn = x_ref[...] + acc
    if final:
        ms = jnp.mean(xn * xn, axis=-1, keepdims=True)
        xn = xn * lax.rsqrt(ms + EPS) * g_ref[...]
    o_ref[...] = xn


def _out_in_body(x_ref, og_ref, om_ref, os_ref, wo_ref, g_ref, wt_ref, xo_ref, pg_ref, pm_ref, ps_ref, w_scr):
    @pl.when(pl.program_id(0) == 0)
    def _():
        w_scr[...] = wt_ref[...].astype(BF16)

    acc = _dot(og_ref[...], wo_ref[0:GLA_WIDTH, :])
    acc += _dot(om_ref[...], wo_ref[GLA_WIDTH:GLA_WIDTH + MLA_WIDTH, :])
    acc += _dot(os_ref[...], wo_ref[GLA_WIDTH + MLA_WIDTH:, :])
    xn = x_ref[...] + acc
    xo_ref[...] = xn
    _inproj_rows(xn, g_ref, w_scr, pg_ref, pm_ref, ps_ref)


def _out_in(x2, og, om, os_, w_out, ln, wt, layer, tm):
    n = x2.shape[0]
    row = lambda c: pl.BlockSpec((tm, c), lambda i: (i, 0))
    return pl.pallas_call(
        _out_in_body,
        grid=(n // tm,),
        in_specs=[row(D_MODEL), row(GLA_WIDTH), row(MLA_WIDTH), row(S5_WIDTH),
                  _layer_spec((D_MODEL, D_MODEL), layer, single_buffer=True),
                  _layer_spec((1, D_MODEL), layer + 1),
                  _layer_spec((IN_OFF['end'], D_MODEL), layer + 1, single_buffer=True)],
        out_specs=[row(D_MODEL), row(PG_COLS), row(PM_COLS), row(PS_COLS)],
        out_shape=[jax.ShapeDtypeStruct((n, D_MODEL), F32), jax.ShapeDtypeStruct((n, PG_COLS), BF16),
                   jax.ShapeDtypeStruct((n, PM_COLS), BF16), jax.ShapeDtypeStruct((n, PS_COLS), BF16)],
        scratch_shapes=[pltpu.VMEM((IN_OFF['end'], D_MODEL), BF16)],
        compiler_params=_cparams("arbitrary"),
        name="outproj_inproj",
    )(x2, og, om, os_, w_out, ln, wt)


def _outproj(x2, og, om, os_, w, gain, layer, tm, final):
    n = x2.shape[0]
    row = lambda c: pl.BlockSpec((tm, c), lambda i: (i, 0))
    return pl.pallas_call(
        functools.partial(_outproj_body, final=final),
        grid=(n // tm,),
        in_specs=[row(D_MODEL), row(GLA_WIDTH), row(MLA_WIDTH), row(S5_WIDTH),
                  _layer_spec((D_MODEL, D_MODEL), layer),
                  _const_spec((1, D_MODEL), 1)],
        out_specs=row(D_MODEL),
        out_shape=jax.ShapeDtypeStruct((n, D_MODEL), F32),
        compiler_params=_cparams("parallel"),
        name="outproj_final" if final else "outproj",
    )(x2, og, om, os_, w, gain)


def _prepare_params(ln_gain, w_in, gla_w_gate, gla_b_gate, gla_norm_gain, mla_q_norm_gain, mla_w_uq,
                    mla_kv_norm_gain, mla_w_ukv, s5_lambda_re, s5_lambda_im, s5_b_re, s5_b_im, s5_c_re, s5_c_im,
                    s5_d, s5_log_dt, s5_w_glu, s5_b_glu, w_out):
    depth = w_in.shape[0]
    w_t = jnp.swapaxes(w_in, 1, 2)
    wg = jnp.pad(gla_w_gate, ((0, 0), (0, GLA_QK - GLA_GATE_RANK), (0, 0))).astype(BF16)
    wq = mla_w_uq.reshape(depth, MLA_Q_RANK, MLA_HEADS, MLA_NOPE_DIM + MLA_ROPE_DIM)
    wq = jnp.pad(wq, ((0, 0), (0, 256 - MLA_Q_RANK), (0, 0), (0, MLA_QK_PAD - MLA_NOPE_DIM - MLA_ROPE_DIM)))
    wq = wq.reshape(depth, 256, MLA_QK_WIDTH).astype(BF16)
    gq = jnp.pad(mla_q_norm_gain, ((0, 0), (0, 256 - MLA_Q_RANK))).reshape(depth, 1, 256)
    wkv = mla_w_ukv.reshape(depth, MLA_KV_RANK, MLA_HEADS, MLA_NOPE_DIM + MLA_V_DIM)
    wk = jnp.pad(wkv[..., :MLA_NOPE_DIM], ((0, 0), (0, 0), (0, 0), (0, MLA_QK_PAD - MLA_NOPE_DIM)))
    wkv_r = jnp.concatenate([wk.reshape(depth, MLA_KV_RANK, MLA_QK_WIDTH),
                             wkv[..., MLA_NOPE_DIM:].reshape(depth, MLA_KV_RANK, MLA_WIDTH)], axis=2).astype(BF16)
    wka = jnp.pad(jnp.transpose(wkv[..., :MLA_NOPE_DIM], (0, 2, 3, 1)),
                  ((0, 0), (0, 0), (0, MLA_QK_PAD - MLA_NOPE_DIM), (0, 0))).astype(BF16)
    wv = jnp.transpose(wkv[..., MLA_NOPE_DIM:], (0, 2, 1, 3)).astype(BF16)
    dt = jnp.exp(s5_log_dt)[:, :, None]
    mag = jnp.exp(s5_lambda_re * dt)
    lbr, lbi = mag * jnp.cos(s5_lambda_im * dt), mag * jnp.sin(s5_lambda_im * dt)
    den = s5_lambda_re * s5_lambda_re + s5_lambda_im * s5_lambda_im
    qr = ((lbr - 1.0) * s5_lambda_re + lbi * s5_lambda_im) / den
    qi = (lbi * s5_lambda_re - (lbr - 1.0) * s5_lambda_im) / den
    b_re_t, b_im_t = jnp.swapaxes(s5_b_re, 2, 3), jnp.swapaxes(s5_b_im, 2, 3)
    bbr = qr[:, :, None, :] * b_re_t - qi[:, :, None, :] * b_im_t
    bbi = qr[:, :, None, :] * b_im_t + qi[:, :, None, :] * b_re_t
    rows = lambda m: m.reshape(depth, S5_WIDTH, S5_STATE)
    wb, wc = _s5_weights(rows(bbr), rows(bbi), rows(s5_c_re), rows(s5_c_im))
    return dict(
        ln=ln_gain.reshape(depth, 1, D_MODEL), w_in=w_t, wg=wg, bg=gla_b_gate.reshape(depth, 1, GLA_QK),
        gla_gain=jnp.tile(gla_norm_gain, (1, GLA_HEADS)).reshape(depth, 1, GLA_WIDTH),
        gq=gq, wq=wq, gkv=mla_kv_norm_gain.reshape(depth, 1, MLA_KV_RANK), wkv=wkv_r, wka=wka, wv=wv,
        lre=lbr.reshape(depth, 1, S5_NSTATE), lim=lbi.reshape(depth, 1, S5_NSTATE),
        wb=wb, wc=wc, d=s5_d.reshape(depth, 1, S5_WIDTH), wglu=s5_w_glu.astype(BF16),
        bglu=s5_b_glu.reshape(depth, 1, S5_WIDTH), w_out=w_out.astype(BF16))


def _rope_tables(past, t):
    half = MLA_ROPE_DIM // 2
    inv = ROPE_BASE ** (-np.arange(half, dtype=np.float64) / half)
    ang = (past + np.arange(t, dtype=np.float64))[:, None] * inv[None, :]
    cos, sin = np.cos(ang), np.sin(ang)
    pad = MLA_QK_PAD - MLA_NOPE_DIM - MLA_ROPE_DIM
    cos_t = np.concatenate([np.ones((t, MLA_NOPE_DIM)), cos, cos, np.zeros((t, pad))], axis=1)
    sin_t = np.concatenate([np.zeros((t, MLA_NOPE_DIM)), sin, sin, np.zeros((t, pad))], axis=1)
    return jnp.asarray(cos_t, F32), jnp.asarray(sin_t, F32)


def _trunk(x, gla_state, ckv_cache, kpe_cache, s5_re, s5_im, p, final_gain):
    bsz, t, _ = x.shape
    n = bsz * t
    depth = p['w_in'].shape[0]
    past = 0 if ckv_cache is None else ckv_cache.shape[2]
    s_len = past + t
    tl = _tiles(bsz, t, past)
    cos_t, sin_t = _rope_tables(past, t)
    kpet_cache = None if kpe_cache is None else jnp.swapaxes(kpe_cache, 2, 3)
    x2 = x.reshape(n, D_MODEL)
    gain_f = final_gain.reshape(1, D_MODEL)
    gla_o, ckv_o, kpe_o, re_o, im_o = [], [], [], [], []
    pg, pm, ps = _inproj(x2, p['ln'], p['w_in'], 0, tl['row'])
    for l in range(depth):
        s0 = jnp.zeros((bsz, GLA_HEADS, GLA_DK, GLA_DV), F32) if gla_state is None else gla_state[l]
        o_gla, s_new = _gla(pg, p['wg'], p['bg'], p['gla_gain'], l, s0, bsz, t, tl)
        gla_o.append(s_new)
        q, ckv_new, kpe128 = _mla_prepq(pm, cos_t, sin_t, p['gq'], p['wq'], p['gkv'], l, t, tl['prep'])
        ckv_o.append(ckv_new.reshape(bsz, t, MLA_KV_RANK))
        kpe_o.append(kpe128[:, MLA_NOPE_DIM:MLA_NOPE_DIM + MLA_ROPE_DIM].reshape(bsz, t, MLA_ROPE_DIM))
        if past == 0:
            k_cat, v_all = _mla_prepkv(ckv_new, kpe128, p['wkv'], l, bsz, t, tl['prep'])
            o_mla = _attn(q, k_cat, v_all, pm, bsz, t, s_len, past, tl['attn_q'], tl['attn_k'])
        else:
            o_mla = _attn_cached(q, pm, ckv_new, kpe128, ckv_cache, kpet_cache, p['wka'], p['wv'], l, bsz, t)
        x0r = jnp.zeros((bsz, S5_NSTATE), F32) if s5_re is None else s5_re[l].reshape(bsz, S5_NSTATE)
        x0i = jnp.zeros((bsz, S5_NSTATE), F32) if s5_im is None else s5_im[l].reshape(bsz, S5_NSTATE)
        o_s5, xr, xi = _s5(ps.reshape(bsz, t, PS_COLS), x0r, x0i, p['lre'], p['lim'], p['wb'], p['wc'],
                           p['d'], p['wglu'], p['bglu'], l, bsz, t, tl['s5_rows'])
        re_o.append(xr.reshape(bsz, S5_GROUPS, S5_STATE))
        im_o.append(xi.reshape(bsz, S5_GROUPS, S5_STATE))
        o_s5 = o_s5.reshape(n, S5_WIDTH)
        if l < depth - 1:
            x2, pg, pm, ps = _out_in(x2, o_gla, o_mla, o_s5, p['w_out'], p['ln'], p['w_in'], l, tl['row'])
        else:
            x2 = _outproj(x2, o_gla, o_mla, o_s5, p['w_out'], gain_f, l, tl['out_row'], final=True)
    return (x2.reshape(bsz, t, D_MODEL), jnp.stack(gla_o), jnp.stack(ckv_o), jnp.stack(kpe_o),
            jnp.stack(re_o), jnp.stack(im_o))


def kernel(x_prompt, x_sample, state_gla, cache_mla_ckv, cache_mla_kpe, state_s5_re, state_s5_im, ln_gain, w_in, gla_w_gate, gla_b_gate, gla_norm_gain, mla_q_norm_gain, mla_w_uq, mla_kv_norm_gain, mla_w_ukv, s5_lambda_re, s5_lambda_im, s5_b_re, s5_b_im, s5_c_re, s5_c_im, s5_d, s5_log_dt, s5_w_glu, s5_b_glu, w_out, final_gain):
    p = _prepare_params(ln_gain, w_in, gla_w_gate, gla_b_gate, gla_norm_gain, mla_q_norm_gain, mla_w_uq,
                        mla_kv_norm_gain, mla_w_ukv, s5_lambda_re, s5_lambda_im, s5_b_re, s5_b_im,
                        s5_c_re, s5_c_im, s5_d, s5_log_dt, s5_w_glu, s5_b_glu, w_out)
    y_p, gla_p, ckv_p, kpe_p, re_p, im_p = _trunk(x_prompt, None, None, None, None, None, p, final_gain)
    y_s, gla_s, ckv_s, kpe_s, re_s, im_s = _trunk(x_sample, state_gla, cache_mla_ckv, cache_mla_kpe,
                                                  state_s5_re, state_s5_im, p, final_gain)
    return (y_p, y_s, gla_p, ckv_p, kpe_p, re_p, im_p, gla_s, ckv_s, kpe_s, re_s, im_s)
```

```python
import functools
import math

import numpy as np
import jax
import jax.numpy as jnp
from jax import lax
from jax.experimental import pallas as pl
from jax.experimental.pallas import tpu as pltpu

F32 = jnp.float32
BF16 = jnp.bfloat16

LANES = 128
D_MODEL = 1024
CHUNK = 64
EPS = 1e-6
GLA_HEADS = 4
GLA_DV = 64
GLA_DK = 32
GLA_WIDTH = GLA_HEADS * GLA_DV
GLA_QK = GLA_HEADS * GLA_DK
GLA_GATE_RANK = 16
GLA_GATE_TAU = 16.0
GLA_SUB = 16
GLA_PLAIN_MAX_DECAY = 60.0
MLA_HEADS = 4
MLA_NOPE_DIM = 64
MLA_ROPE_DIM = 32
MLA_V_DIM = 128
MLA_Q_RANK = 192
MLA_KV_RANK = 128
MLA_WIDTH = MLA_HEADS * MLA_V_DIM
MLA_QK_PAD = 128
MLA_QK_WIDTH = MLA_HEADS * MLA_QK_PAD
ROPE_BASE = 10000.0
S5_GROUPS = 16
S5_GROUP_CH = 16
S5_STATE = 64
S5_WIDTH = S5_GROUPS * S5_GROUP_CH
S5_NSTATE = S5_GROUPS * S5_STATE
S5_BATCH_TILE = 8

PG_COLS = 896
PM_COLS = 1024
PS_COLS = 512
_IN_SEGS = (('g_q', GLA_QK), ('g_k', GLA_QK), ('g_v', GLA_WIDTH), ('g_lr', GLA_GATE_RANK), ('g_z', GLA_WIDTH),
            ('m_cq', MLA_Q_RANK), ('m_ckv', MLA_KV_RANK), ('m_kr', MLA_ROPE_DIM), ('m_z', MLA_WIDTH),
            ('s_u', S5_WIDTH), ('s_z', S5_WIDTH), ('end', 0))
IN_OFF = dict(zip([n for n, _ in _IN_SEGS], np.cumsum([0] + [w for _, w in _IN_SEGS[:-1]]).tolist()))

VMEM_LIMIT_BYTES = 48 * 1024 * 1024


def _tiles(bsz, t, past):
    n = bsz * t
    s_len = past + t
    return dict(
        row=min(512, n),
        out_row=min(1024, n),
        prep=min(512, t),
        gla_rows=min(512, t), gla_chunk=min(CHUNK, t),
        gla_seqs=1 if t > 512 else math.gcd(bsz, max(1, 128 // t)),
        attn_q=min(512, t), attn_k=min(512, s_len),
        s5_rows=min(64, t))


def _cparams(*sem):
    return pltpu.CompilerParams(dimension_semantics=sem, vmem_limit_bytes=VMEM_LIMIT_BYTES)


def _sigmoid(x):
    return 0.5 * (1.0 + jnp.tanh(0.5 * x))


def _dot(a, b):
    return jnp.dot(a, b, preferred_element_type=F32)


def _dot_t(a, b):
    return lax.dot_general(a, b, (((0,), (0,)), ((), ())), preferred_element_type=F32)


def _const_spec(shape, ngrid):
    zeros = (0,) * len(shape)
    return pl.BlockSpec(shape, lambda *_: zeros)


def _layer_spec(shape, layer, single_buffer=False):
    zeros = (0,) * len(shape)
    mode = dict(pipeline_mode=pl.Buffered(1)) if single_buffer else {}
    return pl.BlockSpec((None,) + tuple(shape), lambda *_: (layer,) + zeros, **mode)


def _inproj_rows(x, g_ref, w_scr, og_ref, om_ref, os_ref):
    ms = jnp.mean(x * x, axis=-1, keepdims=True)
    h = (x * lax.rsqrt(ms + EPS) * g_ref[...]).astype(BF16)

    def seg(a, b):
        return lax.dot_general(h, w_scr[a:b, :], (((1,), (1,)), ((), ())), preferred_element_type=F32)

    lane = lax.broadcasted_iota(jnp.int32, (x.shape[0], LANES), 1)
    lane2 = lax.broadcasted_iota(jnp.int32, (x.shape[0], 2 * LANES), 1)
    c = IN_OFF
    og_ref[:, 0:512] = seg(c['g_q'], c['g_lr']).astype(BF16)
    og_ref[:, 512:768] = seg(c['g_z'], c['m_cq']).astype(BF16)
    og_ref[:, 768:896] = jnp.where(lane < GLA_GATE_RANK, seg(c['g_lr'], c['g_lr'] + LANES), 0.0).astype(BF16)
    om_ref[:, 0:256] = jnp.where(lane2 < MLA_Q_RANK, seg(c['m_cq'], c['m_cq'] + 2 * LANES), 0.0).astype(BF16)
    om_ref[:, 256:384] = seg(c['m_ckv'], c['m_kr']).astype(BF16)
    kr = seg(c['m_kr'] - MLA_NOPE_DIM, c['m_kr'] - MLA_NOPE_DIM + LANES)
    om_ref[:, 384:512] = jnp.where((lane >= MLA_NOPE_DIM) & (lane < MLA_NOPE_DIM + MLA_ROPE_DIM), kr, 0.0).astype(BF16)
    om_ref[:, 512:1024] = seg(c['m_z'], c['s_u']).astype(BF16)
    os_ref[...] = seg(c['s_u'], c['end']).astype(BF16)


def _inproj_body(x_ref, g_ref, wt_ref, og_ref, om_ref, os_ref, w_scr):
    @pl.when(pl.program_id(0) == 0)
    def _():
        w_scr[...] = wt_ref[...].astype(BF16)

    _inproj_rows(x_ref[...], g_ref, w_scr, og_ref, om_ref, os_ref)


def _inproj(x2, gain, wt, layer, tm):
    n = x2.shape[0]
    return pl.pallas_call(
        _inproj_body,
        grid=(n // tm,),
        in_specs=[pl.BlockSpec((tm, D_MODEL), lambda i: (i, 0)),
                  _layer_spec((1, D_MODEL), layer),
                  _layer_spec((IN_OFF['end'], D_MODEL), layer, single_buffer=True)],
        scratch_shapes=[pltpu.VMEM((IN_OFF['end'], D_MODEL), BF16)],
        out_specs=[pl.BlockSpec((tm, PG_COLS), lambda i: (i, 0)),
                   pl.BlockSpec((tm, PM_COLS), lambda i: (i, 0)),
                   pl.BlockSpec((tm, PS_COLS), lambda i: (i, 0))],
        out_shape=[jax.ShapeDtypeStruct((n, PG_COLS), BF16),
                   jax.ShapeDtypeStruct((n, PM_COLS), BF16),
                   jax.ShapeDtypeStruct((n, PS_COLS), BF16)],
        compiler_params=_cparams("arbitrary"),
        name="inproj",
    )(x2, gain, wt)


def _gla_body(p_ref, wg_ref, bg_ref, gain_ref, tri_ref, ones_k_ref, ones_v_ref, bd_ref, s0_ref,
              o_ref, sout_ref, q_scr, k_scr, v_scr, b_scr, s_scr, qe_scr, qs_scr, ke_scr, vb_scr, o_scr,
              *, chunk, nch, nseq):
    it = pl.program_id(1)
    sub = min(GLA_SUB, chunk)
    nsub = chunk // sub

    @pl.when(it == 0)
    def _():
        s_scr[...] = jnp.zeros(s_scr.shape, F32)
        for sq in range(nseq):
            for h in range(GLA_HEADS):
                s_scr[sq, h * GLA_DK:(h + 1) * GLA_DK, h * GLA_DV:(h + 1) * GLA_DV] = s0_ref[sq, h]

    q_scr[...] = p_ref[:, 0:128].astype(F32) * (GLA_DK ** -0.5)
    k_scr[...] = p_ref[:, 128:256].astype(F32)
    v_scr[...] = p_ref[:, 256:512].astype(F32)
    logit = _dot(p_ref[:, 768:896], wg_ref[...]) + bg_ref[...]
    log_a = (jnp.minimum(logit, 0.0) - jnp.log(1.0 + jnp.exp(-jnp.abs(logit)))) * (1.0 / GLA_GATE_TAU)
    la_hi = log_a.astype(BF16)
    la_lo = (log_a - la_hi.astype(F32)).astype(BF16)
    for c in range(nseq * nch):
        rs = slice(c * chunk, (c + 1) * chunk)
        b_scr[rs, :] = _dot(tri_ref[...], la_hi[rs]) + _dot(tri_ref[...], la_lo[rs])

    row = lax.broadcasted_iota(jnp.int32, (chunk, GLA_QK), 0)
    row_in_sub = lax.broadcasted_iota(jnp.int32, (sub, GLA_QK), 0)

    def decay_columns(b_end):
        col = jnp.transpose(jnp.broadcast_to(jnp.exp(b_end), (GLA_QK, GLA_QK)))
        return jnp.concatenate([col, col], axis=1)

    def load_chunk(sq, c):
        r0 = pl.multiple_of((sq * nch + c) * chunk, chunk)
        return (r0, q_scr[pl.ds(r0, chunk), :], k_scr[pl.ds(r0, chunk), :], v_scr[pl.ds(r0, chunk), :],
                b_scr[pl.ds(r0, chunk), :], b_scr[pl.ds(r0 + chunk - 1, 1), :], s_scr[sq])

    def finish_chunk(sq, r0, o, s_prev, a_state, b_end):
        ms = _dot((o * o).astype(BF16), ones_v_ref[...]) * (1.0 / GLA_DV)
        o_n = o * lax.rsqrt(ms + EPS) * gain_ref[...]
        z = p_ref[pl.ds(r0, chunk), 512:768].astype(F32)
        o_ref[pl.ds(r0, chunk), :] = (o_n * (z * _sigmoid(z))).astype(BF16)
        s_scr[sq] = s_prev * decay_columns(b_end) + a_state

    def robust_chunk(c, carry, sq):
        r0, qc, kc, vc, bc, b_end, s_prev = load_chunk(sq, c)
        xs = [qc * jnp.exp(bc)]
        ks = []
        for sj in range(nsub - 1):
            e_j = b_scr[pl.ds(r0 + (sj + 1) * sub - 1, 1), :]
            later = row >= (sj + 1) * sub
            xs.append(jnp.where(later, qc * jnp.exp(jnp.where(later, bc - e_j, 0.0)), 0.0))
            own = (row >= sj * sub) & (row < (sj + 1) * sub)
            ks.append(jnp.where(own, kc * jnp.exp(jnp.where(own, e_j - bc, 0.0)), 0.0))
        ks.append(kc * jnp.exp(b_end - bc))
        k_all = jnp.concatenate(ks, axis=1).astype(BF16)
        a_all = _dot_t(k_all, vc.astype(BF16)) * bd_ref[...]
        w = jnp.concatenate([s_prev, a_all[:(nsub - 1) * GLA_QK]], axis=0).astype(BF16) if nsub > 1 \
            else s_prev.astype(BF16)
        o_off = _dot(jnp.concatenate(xs, axis=1).astype(BF16), w)
        rows = []
        for si in range(nsub):
            q_i = qc[si * sub:(si + 1) * sub]
            b_i = bc[si * sub:(si + 1) * sub]
            es = []
            for j in range(sub):
                r = r0 + si * sub + j
                b_j = b_scr[pl.ds(r, 1), :]
                k_j = k_scr[pl.ds(r, 1), :]
                valid = row_in_sub >= j
                es.append(q_i * k_j * jnp.exp(jnp.where(valid, b_i - b_j, -jnp.inf)))
            e_all = jnp.concatenate(es, axis=0)
            e_hi = e_all.astype(BF16)
            e_lo = (e_all - e_hi.astype(F32)).astype(BF16)
            p_all = _dot(e_hi, ones_k_ref[...]) + _dot(e_lo, ones_k_ref[...])
            acc = o_off[si * sub:(si + 1) * sub]
            for j in range(sub):
                v_j = v_scr[pl.ds(r0 + si * sub + j, 1), :]
                acc = acc + p_all[j * sub:(j + 1) * sub] * v_j
            rows.append(acc)
        o = rows[0] if nsub == 1 else jnp.concatenate(rows, axis=0)
        finish_chunk(sq, r0, o, s_prev, a_all[(nsub - 1) * GLA_QK:], b_end)
        return carry

    lane_head_v = lax.broadcasted_iota(jnp.int32, (chunk, GLA_WIDTH), 1) // GLA_DV
    causal = (lax.broadcasted_iota(jnp.int32, (GLA_HEADS * chunk, chunk), 0) % chunk
              >= lax.broadcasted_iota(jnp.int32, (GLA_HEADS * chunk, chunk), 1))

    def plain_block():
        tt = nseq * nch * chunk
        b_all = b_scr[...]
        q_all = q_scr[...]
        k_all = k_scr[...]
        qe = q_all * jnp.exp(b_all)
        lane_head = lax.broadcasted_iota(jnp.int32, (tt, GLA_QK), 1) // GLA_DK
        qe_scr[...] = qe.astype(BF16)
        for h in range(GLA_HEADS):
            qs_scr[h] = jnp.where(lane_head == h, qe, 0.0).astype(BF16)
        ke_scr[...] = (k_all * jnp.exp(-b_all)).astype(BF16)
        vb_scr[...] = p_ref[:, 256:512]
        for sq, c in [(sq, c) for sq in range(nseq) for c in range(nch)]:
            r0 = (sq * nch + c) * chunk
            rs = slice(r0, r0 + chunk)
            s_cur = s_scr[sq] if c == 0 else s_cur
            qs = jnp.concatenate([qs_scr[h, rs, :] for h in range(GLA_HEADS)], axis=0)
            s = lax.dot_general(qs, ke_scr[rs, :], (((1,), (1,)), ((), ())), preferred_element_type=F32)
            s = jnp.where(causal, s, 0.0).astype(BF16)
            r = _dot(s, vb_scr[rs, :])
            o = _dot(qe_scr[rs, :], s_cur.astype(BF16))
            for h in range(GLA_HEADS):
                o = o + jnp.where(lane_head_v == h, r[h * chunk:(h + 1) * chunk], 0.0)
            o_scr[rs, :] = o
            b_end = b_scr[r0 + chunk - 1:r0 + chunk, :]
            k_end = (k_scr[rs, :] * jnp.exp(b_end - b_scr[rs, :])).astype(BF16)
            a_state = _dot_t(k_end, vb_scr[rs, :]) * bd_ref[0:GLA_QK, :]
            s_cur = s_cur * decay_columns(b_end) + a_state
            if c == nch - 1:
                s_scr[sq] = s_cur
        o = o_scr[...]
        ms = _dot((o * o).astype(BF16), ones_v_ref[...]) * (1.0 / GLA_DV)
        o_n = o * lax.rsqrt(ms + EPS) * gain_ref[...]
        z = p_ref[:, 512:768].astype(F32)
        o_ref[...] = (o_n * (z * _sigmoid(z))).astype(BF16)

    in_range = jnp.max(-b_scr[...]) < GLA_PLAIN_MAX_DECAY

    @pl.when(in_range)
    def _():
        plain_block()

    @pl.when(jnp.logical_not(in_range))
    def _():
        for sq in range(nseq):
            lax.fori_loop(0, nch, functools.partial(robust_chunk, sq=sq), 0)

    @pl.when(it == pl.num_programs(1) - 1)
    def _():
        for sq in range(nseq):
            for h in range(GLA_HEADS):
                sout_ref[sq, h] = s_scr[sq, h * GLA_DK:(h + 1) * GLA_DK, h * GLA_DV:(h + 1) * GLA_DV]


def _gla_consts(chunk):
    nsub = chunk // min(GLA_SUB, chunk)
    r = np.arange(chunk)
    tri = (r[None, :] <= r[:, None]).astype(np.float32)
    hk = np.arange(GLA_QK) // GLA_DK
    hv = np.arange(GLA_WIDTH) // GLA_DV
    same_kv = (hk[:, None] == hv[None, :]).astype(np.float32)
    same_vv = (hv[:, None] == hv[None, :]).astype(np.float32)
    return (jnp.asarray(tri, BF16), jnp.asarray(same_kv, BF16), jnp.asarray(same_vv, BF16),
            jnp.asarray(np.tile(same_kv, (nsub, 1)), F32))


def _gla(pg, wg, bg, gain, layer, s0_bd, bsz, t, tl):
    chunk, tt, nseq = tl['gla_chunk'], tl['gla_rows'], tl['gla_seqs']
    nch = tt // chunk
    nt = t // tt
    rows = nseq * tt
    nsub = chunk // min(GLA_SUB, chunk)
    tri, ones_k, ones_v, bd = _gla_consts(chunk)
    const = lambda shape: _const_spec(shape, 2)
    return pl.pallas_call(
        functools.partial(_gla_body, chunk=chunk, nch=nch, nseq=nseq),
        grid=(bsz // nseq, nt),
        in_specs=[pl.BlockSpec((rows, PG_COLS), lambda b, i: (b * nt + i, 0)),
                  _layer_spec((GLA_QK, GLA_QK), layer), _layer_spec((1, GLA_QK), layer),
                  _layer_spec((1, GLA_WIDTH), layer),
                  const((chunk, chunk)), const((GLA_QK, GLA_WIDTH)), const((GLA_WIDTH, GLA_WIDTH)),
                  const((nsub * GLA_QK, GLA_WIDTH)),
                  pl.BlockSpec((nseq, GLA_HEADS, GLA_DK, GLA_DV), lambda b, i: (b, 0, 0, 0))],
        out_specs=[pl.BlockSpec((rows, GLA_WIDTH), lambda b, i: (b * nt + i, 0)),
                   pl.BlockSpec((nseq, GLA_HEADS, GLA_DK, GLA_DV), lambda b, i: (b, 0, 0, 0))],
        out_shape=[jax.ShapeDtypeStruct((bsz * t, GLA_WIDTH), BF16),
                   jax.ShapeDtypeStruct((bsz, GLA_HEADS, GLA_DK, GLA_DV), F32)],
        scratch_shapes=[pltpu.VMEM((rows, GLA_QK), F32), pltpu.VMEM((rows, GLA_QK), F32),
                        pltpu.VMEM((rows, GLA_WIDTH), F32), pltpu.VMEM((rows, GLA_QK), F32),
                        pltpu.VMEM((nseq, GLA_QK, GLA_WIDTH), F32),
                        pltpu.VMEM((rows, GLA_QK), BF16), pltpu.VMEM((GLA_HEADS, rows, GLA_QK), BF16),
                        pltpu.VMEM((rows, GLA_QK), BF16),
                        pltpu.VMEM((rows, GLA_WIDTH), BF16), pltpu.VMEM((rows, GLA_WIDTH), F32)],
        compiler_params=_cparams("parallel", "arbitrary"),
        name="gla",
    )(pg, wg, bg, gain, tri, ones_k, ones_v, bd, s0_bd)


def _rope128(x, cos_t, sin_t):
    lane = lax.broadcasted_iota(jnp.int32, x.shape, 1)
    first_half = (lane >= MLA_NOPE_DIM) & (lane < MLA_NOPE_DIM + MLA_ROPE_DIM // 2)
    rot = jnp.where(first_half, -pltpu.roll(x, LANES - MLA_ROPE_DIM // 2, 1), pltpu.roll(x, MLA_ROPE_DIM // 2, 1))
    return x * cos_t + rot * sin_t


def _mla_prep_body(p_ref, cos_ref, sin_ref, gq_ref, wq_ref, gkv_ref, *rest, with_kv):
    if with_kv:
        wkv_ref, q_ref, ckv_ref, kpet_ref, k_ref, v_ref = rest
    else:
        q_ref, ckv_ref, kpe_ref = rest
    cos_t = cos_ref[...]
    sin_t = sin_ref[...]
    cq = p_ref[:, 0:256].astype(F32)
    ms = jnp.sum(cq * cq, axis=-1, keepdims=True) * (1.0 / MLA_Q_RANK)
    cqn = (cq * lax.rsqrt(ms + EPS) * gq_ref[...]).astype(BF16)
    qh = _dot(cqn, wq_ref[...])
    scale = (MLA_NOPE_DIM + MLA_ROPE_DIM) ** -0.5 * math.log2(math.e)
    for h in range(MLA_HEADS):
        x = qh[:, h * MLA_QK_PAD:(h + 1) * MLA_QK_PAD]
        q_ref[:, h * MLA_QK_PAD:(h + 1) * MLA_QK_PAD] = (_rope128(x, cos_t, sin_t) * scale).astype(BF16)
    ckv = p_ref[:, 256:384].astype(F32)
    ms = jnp.mean(ckv * ckv, axis=-1, keepdims=True)
    ckv_n = ckv * lax.rsqrt(ms + EPS) * gkv_ref[...]
    ckv_ref[...] = ckv_n
    kpe128 = _rope128(p_ref[:, 384:512].astype(F32), cos_t, sin_t)
    if with_kv:
        kpet_ref[...] = jnp.transpose(kpe128)[MLA_NOPE_DIM:MLA_NOPE_DIM + MLA_ROPE_DIM, :]
        kv = _dot(ckv_n.astype(BF16), wkv_ref[...])
        for h in range(MLA_HEADS):
            sl = slice(h * MLA_QK_PAD, (h + 1) * MLA_QK_PAD)
            k_ref[:, sl] = (kv[:, sl] + kpe128).astype(BF16)
        v_ref[...] = kv[:, MLA_QK_WIDTH:].astype(BF16)
    else:
        kpe_ref[...] = kpe128


def _mla_prep(pm, cos_t, sin_t, gq, wq, gkv, wkv, layer, bsz, t, tm, with_kv):
    n = bsz * t
    ntab = t // tm
    row = lambda c: pl.BlockSpec((tm, c), lambda i: (i, 0))
    in_specs = [row(512),
                pl.BlockSpec((tm, LANES), lambda i: (i % ntab, 0)),
                pl.BlockSpec((tm, LANES), lambda i: (i % ntab, 0)),
                _layer_spec((1, 256), layer), _layer_spec((256, MLA_QK_WIDTH), layer),
                _layer_spec((1, MLA_KV_RANK), layer)]
    out_specs = [row(MLA_QK_WIDTH), row(MLA_KV_RANK)]
    out_shape = [jax.ShapeDtypeStruct((n, MLA_QK_WIDTH), BF16), jax.ShapeDtypeStruct((n, MLA_KV_RANK), F32)]
    args = [pm, cos_t, sin_t, gq, wq, gkv]
    if with_kv:
        in_specs.append(_layer_spec((MLA_KV_RANK, MLA_QK_WIDTH + MLA_WIDTH), layer))
        args.append(wkv)
        out_specs += [pl.BlockSpec((None, MLA_ROPE_DIM, tm), lambda i: (i // ntab, 0, i % ntab)),
                      row(MLA_QK_WIDTH), row(MLA_WIDTH)]
        out_shape += [jax.ShapeDtypeStruct((bsz, MLA_ROPE_DIM, t), F32),
                      jax.ShapeDtypeStruct((n, MLA_QK_WIDTH), BF16), jax.ShapeDtypeStruct((n, MLA_WIDTH), BF16)]
    else:
        out_specs.append(row(LANES))
        out_shape.append(jax.ShapeDtypeStruct((n, LANES), F32))
    return pl.pallas_call(
        functools.partial(_mla_prep_body, with_kv=with_kv),
        grid=(n // tm,),
        in_specs=in_specs, out_specs=out_specs, out_shape=out_shape,
        compiler_params=_cparams("parallel"),
        name="mla_prep_kv" if with_kv else "mla_prep",
    )(*args)


def _attn_cached_body(q_ref, z_ref, ckv_new_ref, kpe_new_ref, ckv_past_ref, kpet_past_ref, wka_ref, wv_ref,
                      o_ref, kpet_scr, *, past, t):
    last = (((1,), (1,)), ((), ()))
    heads = [q_ref[:, h * MLA_QK_PAD:(h + 1) * MLA_QK_PAD] for h in range(MLA_HEADS)]
    q_rows = jnp.concatenate(heads, axis=0)
    q_lat = jnp.concatenate([_dot(heads[h], wka_ref[h]) for h in range(MLA_HEADS)], axis=0).astype(BF16)
    c_past = ckv_past_ref[...].astype(BF16)
    c_new = ckv_new_ref[...].astype(BF16)
    kpet_scr[...] = jnp.zeros(kpet_scr.shape, BF16)
    kpet_scr[MLA_NOPE_DIM:MLA_NOPE_DIM + MLA_ROPE_DIM, :] = kpet_past_ref[...].astype(BF16)
    s_past = (lax.dot_general(q_lat, c_past, last, preferred_element_type=F32)
              + _dot(q_rows, kpet_scr[...]))
    s_new = (lax.dot_general(q_lat, c_new, last, preferred_element_type=F32)
             + lax.dot_general(q_rows, kpe_new_ref[...].astype(BF16), last, preferred_element_type=F32))
    if past // CHUNK != (past + t - 1) // CHUNK:
        q_chunk = (past + lax.broadcasted_iota(jnp.int32, s_past.shape, 0) % t) // CHUNK
        s_past = jnp.where(lax.broadcasted_iota(jnp.int32, s_past.shape, 1) // CHUNK <= q_chunk, s_past, -jnp.inf)
        q_chunk = (past + lax.broadcasted_iota(jnp.int32, s_new.shape, 0) % t) // CHUNK
        s_new = jnp.where((past + lax.broadcasted_iota(jnp.int32, s_new.shape, 1)) // CHUNK <= q_chunk,
                          s_new, -jnp.inf)
    m = jnp.maximum(jnp.max(s_past, axis=-1, keepdims=True), jnp.max(s_new, axis=-1, keepdims=True))
    p_past = jnp.exp2(s_past - m)
    p_new = jnp.exp2(s_new - m)
    l = jnp.sum(p_past, axis=-1, keepdims=True) + jnp.sum(p_new, axis=-1, keepdims=True)
    o_lat = ((_dot(p_past.astype(BF16), c_past) + _dot(p_new.astype(BF16), c_new)) / l).astype(BF16)
    for h in range(MLA_HEADS):
        vsl = slice(h * MLA_V_DIM, (h + 1) * MLA_V_DIM)
        z = z_ref[:, vsl].astype(F32)
        o_ref[:, vsl] = (_dot(o_lat[h * t:(h + 1) * t], wv_ref[h]) * (z * _sigmoid(z))).astype(BF16)


def _attn_cached(q, pm, ckv_new, kpe128, ckv_past, kpet_past, wka, wv, layer, bsz, t):
    past = ckv_past.shape[2]
    return pl.pallas_call(
        functools.partial(_attn_cached_body, past=past, t=t),
        grid=(bsz,),
        in_specs=[pl.BlockSpec((t, MLA_QK_WIDTH), lambda b: (b, 0)),
                  pl.BlockSpec((t, MLA_WIDTH), lambda b: (b, 1)),
                  pl.BlockSpec((t, MLA_KV_RANK), lambda b: (b, 0)),
                  pl.BlockSpec((t, LANES), lambda b: (b, 0)),
                  pl.BlockSpec((None, None, past, MLA_KV_RANK), lambda b: (layer, b, 0, 0)),
                  pl.BlockSpec((None, None, MLA_ROPE_DIM, past), lambda b: (layer, b, 0, 0)),
                  _layer_spec((MLA_HEADS, MLA_QK_PAD, MLA_KV_RANK), layer),
                  _layer_spec((MLA_HEADS, MLA_KV_RANK, MLA_V_DIM), layer)],
        out_specs=pl.BlockSpec((t, MLA_WIDTH), lambda b: (b, 0)),
        out_shape=jax.ShapeDtypeStruct((bsz * t, MLA_WIDTH), BF16),
        scratch_shapes=[pltpu.VMEM((MLA_QK_PAD, past), BF16)],
        compiler_params=_cparams("parallel"),
        name="mla_attn_cached",
    )(q, pm, ckv_new, kpe128, ckv_past, kpet_past, wka, wv)


def _attn_body(q_ref, k_ref, v_ref, z_ref, o_ref, m_scr, acc_scr, *, past, tq, tk, s_len):
    iq = pl.program_id(1)
    q_first = past + iq * tq
    full_keys = jnp.minimum((q_first // CHUNK + 1) * CHUNK, s_len)
    vis_keys = jnp.minimum(((q_first + tq - 1) // CHUNK + 1) * CHUNK, s_len)
    n_full = full_keys // tk
    n_vis = (vis_keys + tk - 1) // tk
    m_scr[...] = jnp.full(m_scr.shape, -jnp.inf, F32)
    acc_scr[...] = jnp.zeros(acc_scr.shape, F32)
    split_diagonal = past % tk == 0 and tq == tk and (tq // 2) % CHUNK == 0

    def block(kb, carry, masked):
        k0 = pl.multiple_of(kb * tk, tk)
        if masked and split_diagonal:
            parts = [(r * (tq // 2), tq // 2, (r + 1) * (tk // 2)) for r in range(2)]
        else:
            parts = [(0, tq, tk)]
        for r0, nr, kext in parts:
            rows = slice(r0, r0 + nr)
            if masked:
                q_pos = (r0 if split_diagonal else q_first + r0) + lax.broadcasted_iota(jnp.int32, (nr, kext), 0)
                k_pos = (0 if split_diagonal else k0) + lax.broadcasted_iota(jnp.int32, (nr, kext), 1)
                visible = k_pos // CHUNK <= q_pos // CHUNK
            ntile, rem = kext // LANES, kext % LANES
            ones_v = jnp.ones((kext, MLA_V_DIM), BF16)
            for h in range(MLA_HEADS):
                sl = slice(h * MLA_QK_PAD, (h + 1) * MLA_QK_PAD)
                vsl = slice(h * MLA_V_DIM, (h + 1) * MLA_V_DIM)
                s = lax.dot_general(q_ref[rows, sl], k_ref[pl.ds(k0, kext), sl], (((1,), (1,)), ((), ())),
                                    preferred_element_type=F32)
                if masked:
                    s = jnp.where(visible, s, -jnp.inf)
                m_prev = m_scr[h, rows]
                m_new = jnp.maximum(m_prev, jnp.max(s, axis=-1, keepdims=True))
                alpha = jnp.exp2(m_prev - m_new)
                ps = [jnp.exp2(s[:, c * LANES:(c + 1) * LANES] - m_new) for c in range(ntile)]
                if rem:
                    ps.append(jnp.exp2(s[:, ntile * LANES:] - m_new[:, :rem]))
                p = jnp.concatenate(ps, axis=1).astype(BF16)
                v_ext = jnp.concatenate([v_ref[pl.ds(k0, kext), vsl], ones_v], axis=1)
                acc_scr[h, rows] = jnp.concatenate([alpha, alpha], axis=1) * acc_scr[h, rows] + _dot(p, v_ext)
                m_scr[h, rows] = m_new
        return carry

    lax.fori_loop(0, n_full, functools.partial(block, masked=False), 0)
    lax.fori_loop(n_full, n_vis, functools.partial(block, masked=True), 0)
    for h in range(MLA_HEADS):
        vsl = slice(h * MLA_V_DIM, (h + 1) * MLA_V_DIM)
        z = z_ref[:, vsl].astype(F32)
        acc = acc_scr[h]
        o_ref[:, vsl] = (acc[:, :MLA_V_DIM] / acc[:, MLA_V_DIM:] * (z * _sigmoid(z))).astype(BF16)


def _attn(q, k, v, pm, bsz, t, s_len, past, tq, tk):
    nq = t // tq
    return pl.pallas_call(
        functools.partial(_attn_body, past=past, tq=tq, tk=tk, s_len=s_len),
        grid=(bsz, nq),
        in_specs=[pl.BlockSpec((tq, MLA_QK_WIDTH), lambda b, iq: (b * nq + iq, 0)),
                  pl.BlockSpec((s_len, MLA_QK_WIDTH), lambda b, iq: (b, 0)),
                  pl.BlockSpec((s_len, MLA_WIDTH), lambda b, iq: (b, 0)),
                  pl.BlockSpec((tq, MLA_WIDTH), lambda b, iq: (b * nq + iq, 1))],
        out_specs=pl.BlockSpec((tq, MLA_WIDTH), lambda b, iq: (b * nq + iq, 0)),
        out_shape=jax.ShapeDtypeStruct((bsz * t, MLA_WIDTH), BF16),
        scratch_shapes=[pltpu.VMEM((MLA_HEADS, tq, LANES), F32), pltpu.VMEM((MLA_HEADS, tq, 2 * MLA_V_DIM), F32)],
        compiler_params=_cparams("parallel", "arbitrary"),
        name="mla_attn",
    )(q, k, v, pm)


def _s5_weights_body(bre_ref, bim_ref, cre_ref, cim_ref, spread_ref, mask_ref, wb_ref, wc_ref):
    def expand(m_ref):
        return _dot(m_ref[...].astype(BF16), spread_ref[...]) * mask_ref[...]

    wb_ref[:, 0:S5_NSTATE] = expand(bre_ref).astype(BF16)
    wb_ref[:, S5_NSTATE:] = expand(bim_ref).astype(BF16)
    wc_ref[0:S5_NSTATE, :] = jnp.transpose(expand(cre_ref)).astype(BF16)
    wc_ref[S5_NSTATE:, :] = jnp.transpose(-expand(cim_ref)).astype(BF16)


def _s5_weights(bbr, bbi, c_re, c_im):
    depth = bbr.shape[0]
    g_row = np.arange(S5_WIDTH) // S5_GROUP_CH
    g_col = np.arange(S5_NSTATE) // S5_STATE
    mask = (g_row[:, None] == g_col[None, :]).astype(np.float32)
    spread = (np.arange(S5_STATE)[:, None] == (np.arange(S5_NSTATE) % S5_STATE)[None, :]).astype(np.float32)
    small = pl.BlockSpec((None, S5_WIDTH, S5_STATE), lambda l: (l, 0, 0))
    return pl.pallas_call(
        _s5_weights_body,
        grid=(depth,),
        in_specs=[small, small, small, small,
                  _const_spec((S5_STATE, S5_NSTATE), 1), _const_spec((S5_WIDTH, S5_NSTATE), 1)],
        out_specs=[pl.BlockSpec((None, S5_WIDTH, 2 * S5_NSTATE), lambda l: (l, 0, 0)),
                   pl.BlockSpec((None, 2 * S5_NSTATE, S5_WIDTH), lambda l: (l, 0, 0))],
        out_shape=[jax.ShapeDtypeStruct((depth, S5_WIDTH, 2 * S5_NSTATE), BF16),
                   jax.ShapeDtypeStruct((depth, 2 * S5_NSTATE, S5_WIDTH), BF16)],
        compiler_params=_cparams("parallel"),
        name="s5_weights",
    )(bbr, bbi, c_re, c_im, jnp.asarray(spread, BF16), jnp.asarray(mask, F32))


def _s5_body(p0_ref, pn_ref, x0r_ref, x0i_ref, lre_ref, lim_ref, wb_ref, wc_ref, d_ref, wglu_ref, bglu_ref,
             o_ref, xr_out, xi_out, uz_bt, uz_a, uz_b, uz_c, bu_a, bu_b, bu_c, o_tb, xr_s, xi_s, *, lc, pitch):
    it = pl.program_id(1)
    nb = S5_BATCH_TILE
    ring = ((uz_a, bu_a), (uz_b, bu_b), (uz_c, bu_c))

    def stage_in(blk_ref, uz_tb, bu):
        for b in range(nb):
            for c in range(PS_COLS // LANES):
                uz_bt[c, b * pitch:b * pitch + lc, :] = blk_ref[b, :, c * LANES:(c + 1) * LANES].astype(F32)
        for t in range(lc):
            for c in range(PS_COLS // LANES):
                uz_tb[t * nb:(t + 1) * nb, c * LANES:(c + 1) * LANES] = uz_bt[c, pl.ds(t, nb, stride=pitch), :]
        bu[...] = _dot(uz_tb[:, 0:S5_WIDTH].astype(BF16), wb_ref[...])

    def stage_scan(bu):
        lre = jnp.broadcast_to(lre_ref[...], (nb, S5_NSTATE))
        lim = jnp.broadcast_to(lim_ref[...], (nb, S5_NSTATE))
        xr, xi = xr_s[...], xi_s[...]
        for t in range(lc):
            rs = slice(t * nb, (t + 1) * nb)
            xr, xi = (lre * xr - lim * xi + bu[rs, 0:S5_NSTATE],
                      lre * xi + lim * xr + bu[rs, S5_NSTATE:2 * S5_NSTATE])
            bu[rs, 0:S5_NSTATE] = xr
            bu[rs, S5_NSTATE:2 * S5_NSTATE] = xi
        xr_s[...] = xr
        xi_s[...] = xi

    def stage_out(uz_tb, xs):
        y = _dot(xs[...].astype(BF16), wc_ref[...]) + d_ref[...] * uz_tb[:, 0:S5_WIDTH]
        g5 = 0.5 * y * (1.0 + jnp.tanh(0.7978845608028654 * (y + 0.044715 * (y * y * y))))
        gate = _sigmoid(_dot(g5.astype(BF16), wglu_ref[...]) + bglu_ref[...])
        z = uz_tb[:, S5_WIDTH:2 * S5_WIDTH]
        o = g5 * gate * (z * _sigmoid(z))
        for c in range(S5_WIDTH // LANES):
            o_tb[c] = o[:, c * LANES:(c + 1) * LANES]
        for b in range(nb):
            for c in range(S5_WIDTH // LANES):
                o_ref[b, :, c * LANES:(c + 1) * LANES] = o_tb[c, pl.ds(b, lc, stride=nb), :].astype(BF16)

    @pl.when(it == 0)
    def _():
        xr_s[...] = x0r_ref[...]
        xi_s[...] = x0i_ref[...]
        stage_in(p0_ref, uz_a, bu_a)
        uz_c[...] = jnp.zeros(uz_c.shape, F32)
        bu_c[...] = jnp.zeros(bu_c.shape, F32)

    for r in range(3):
        @pl.when(it % 3 == r)
        def _(r=r):
            stage_in(pn_ref, *ring[(r + 1) % 3])
            stage_scan(ring[r][1])
            stage_out(*ring[(r + 2) % 3])

    @pl.when(it == pl.num_programs(1) - 2)
    def _():
        xr_out[...] = xr_s[...]
        xi_out[...] = xi_s[...]


def _s5(ps3, x0r, x0i, lre, lim, wb, wc, d, wglu, bglu, layer, bsz, t, lc):
    nb = S5_BATCH_TILE
    nt = t // lc
    pitch = lc + 8
    rows = lc * nb
    return pl.pallas_call(
        functools.partial(_s5_body, lc=lc, pitch=pitch),
        grid=(bsz // nb, nt + 1),
        in_specs=[pl.BlockSpec((nb, lc, PS_COLS), lambda g, i: (g, 0, 0)),
                  pl.BlockSpec((nb, lc, PS_COLS), lambda g, i: (g, jnp.minimum(i + 1, nt - 1), 0)),
                  pl.BlockSpec((nb, S5_NSTATE), lambda g, i: (g, 0)),
                  pl.BlockSpec((nb, S5_NSTATE), lambda g, i: (g, 0)),
                  _layer_spec((1, S5_NSTATE), layer), _layer_spec((1, S5_NSTATE), layer),
                  _layer_spec((S5_WIDTH, 2 * S5_NSTATE), layer), _layer_spec((2 * S5_NSTATE, S5_WIDTH), layer),
                  _layer_spec((1, S5_WIDTH), layer), _layer_spec((S5_WIDTH, S5_WIDTH), layer),
                  _layer_spec((1, S5_WIDTH), layer)],
        out_specs=[pl.BlockSpec((nb, lc, S5_WIDTH), lambda g, i: (g, jnp.maximum(i - 1, 0), 0)),
                   pl.BlockSpec((nb, S5_NSTATE), lambda g, i: (g, 0)),
                   pl.BlockSpec((nb, S5_NSTATE), lambda g, i: (g, 0))],
        out_shape=[jax.ShapeDtypeStruct((bsz, t, S5_WIDTH), BF16),
                   jax.ShapeDtypeStruct((bsz, S5_NSTATE), F32),
                   jax.ShapeDtypeStruct((bsz, S5_NSTATE), F32)],
        scratch_shapes=[pltpu.VMEM((PS_COLS // LANES, nb * pitch, LANES), F32)]
                       + [pltpu.VMEM((rows, PS_COLS), F32)] * 3
                       + [pltpu.VMEM((rows, 2 * S5_NSTATE), F32)] * 3
                       + [pltpu.VMEM((S5_WIDTH // LANES, rows, LANES), F32),
                          pltpu.VMEM((nb, S5_NSTATE), F32), pltpu.VMEM((nb, S5_NSTATE), F32)],
        compiler_params=_cparams("parallel", "arbitrary"),
        name="s5",
    )(ps3, ps3, x0r, x0i, lre, lim, wb, wc, d, wglu, bglu)


def _outproj_body(x_ref, og_ref, om_ref, os_ref, w_ref, g_ref, o_ref, *, final):
    acc = _dot(og_ref[...], w_ref[0:GLA_WIDTH, :])
    acc += _dot(om_ref[...], w_ref[GLA_WIDTH:GLA_WIDTH + MLA_WIDTH, :])
    acc += _dot(os_ref[...], w_ref[GLA_WIDTH + MLA_WIDTH:, :])
    xn = x_ref[...] + acc
    if final:
        ms = jnp.mean(xn * xn, axis=-1, keepdims=True)
        xn = xn * lax.rsqrt(ms + EPS) * g_ref[...]
    o_ref[...] = xn


def _out_in_body(x_ref, og_ref, om_ref, os_ref, wo_ref, g_ref, wt_ref, xo_ref, pg_ref, pm_ref, ps_ref, w_scr):
    @pl.when(pl.program_id(0) == 0)
    def _():
        w_scr[...] = wt_ref[...].astype(BF16)

    acc = _dot(og_ref[...], wo_ref[0:GLA_WIDTH, :])
    acc += _dot(om_ref[...], wo_ref[GLA_WIDTH:GLA_WIDTH + MLA_WIDTH, :])
    acc += _dot(os_ref[...], wo_ref[GLA_WIDTH + MLA_WIDTH:, :])
    xn = x_ref[...] + acc
    xo_ref[...] = xn
    _inproj_rows(xn, g_ref, w_scr, pg_ref, pm_ref, ps_ref)


def _out_in(x2, og, om, os_, w_out, ln, wt, layer, tm):
    n = x2.shape[0]
    row = lambda c: pl.BlockSpec((tm, c), lambda i: (i, 0))
    return pl.pallas_call(
        _out_in_body,
        grid=(n // tm,),
        in_specs=[row(D_MODEL), row(GLA_WIDTH), row(MLA_WIDTH), row(S5_WIDTH),
                  _layer_spec((D_MODEL, D_MODEL), layer, single_buffer=True),
                  _layer_spec((1, D_MODEL), layer + 1),
                  _layer_spec((IN_OFF['end'], D_MODEL), layer + 1, single_buffer=True)],
        out_specs=[row(D_MODEL), row(PG_COLS), row(PM_COLS), row(PS_COLS)],
        out_shape=[jax.ShapeDtypeStruct((n, D_MODEL), F32), jax.ShapeDtypeStruct((n, PG_COLS), BF16),
                   jax.ShapeDtypeStruct((n, PM_COLS), BF16), jax.ShapeDtypeStruct((n, PS_COLS), BF16)],
        scratch_shapes=[pltpu.VMEM((IN_OFF['end'], D_MODEL), BF16)],
        compiler_params=_cparams("arbitrary"),
        name="outproj_inproj",
    )(x2, og, om, os_, w_out, ln, wt)


def _outproj(x2, og, om, os_, w, gain, layer, tm, final):
    n = x2.shape[0]
    row = lambda c: pl.BlockSpec((tm, c), lambda i: (i, 0))
    return pl.pallas_call(
        functools.partial(_outproj_body, final=final),
        grid=(n // tm,),
        in_specs=[row(D_MODEL), row(GLA_WIDTH), row(MLA_WIDTH), row(S5_WIDTH),
                  _layer_spec((D_MODEL, D_MODEL), layer),
                  _const_spec((1, D_MODEL), 1)],
        out_specs=row(D_MODEL),
        out_shape=jax.ShapeDtypeStruct((n, D_MODEL), F32),
        compiler_params=_cparams("parallel"),
        name="outproj_final" if final else "outproj",
    )(x2, og, om, os_, w, gain)


def _prepare_params(ln_gain, w_in, gla_w_gate, gla_b_gate, gla_norm_gain, mla_q_norm_gain, mla_w_uq,
                    mla_kv_norm_gain, mla_w_ukv, s5_lambda_re, s5_lambda_im, s5_b_re, s5_b_im, s5_c_re, s5_c_im,
                    s5_d, s5_log_dt, s5_w_glu, s5_b_glu, w_out):
    depth = w_in.shape[0]
    w_t = jnp.swapaxes(w_in, 1, 2)
    wg = jnp.pad(gla_w_gate, ((0, 0), (0, GLA_QK - GLA_GATE_RANK), (0, 0))).astype(BF16)
    wq = mla_w_uq.reshape(depth, MLA_Q_RANK, MLA_HEADS, MLA_NOPE_DIM + MLA_ROPE_DIM)
    wq = jnp.pad(wq, ((0, 0), (0, 256 - MLA_Q_RANK), (0, 0), (0, MLA_QK_PAD - MLA_NOPE_DIM - MLA_ROPE_DIM)))
    wq = wq.reshape(depth, 256, MLA_QK_WIDTH).astype(BF16)
    gq = jnp.pad(mla_q_norm_gain, ((0, 0), (0, 256 - MLA_Q_RANK))).reshape(depth, 1, 256)
    wkv = mla_w_ukv.reshape(depth, MLA_KV_RANK, MLA_HEADS, MLA_NOPE_DIM + MLA_V_DIM)
    wk = jnp.pad(wkv[..., :MLA_NOPE_DIM], ((0, 0), (0, 0), (0, 0), (0, MLA_QK_PAD - MLA_NOPE_DIM)))
    wkv_r = jnp.concatenate([wk.reshape(depth, MLA_KV_RANK, MLA_QK_WIDTH),
                             wkv[..., MLA_NOPE_DIM:].reshape(depth, MLA_KV_RANK, MLA_WIDTH)], axis=2).astype(BF16)
    wka = jnp.pad(jnp.transpose(wkv[..., :MLA_NOPE_DIM], (0, 2, 3, 1)),
                  ((0, 0), (0, 0), (0, MLA_QK_PAD - MLA_NOPE_DIM), (0, 0))).astype(BF16)
    wv = jnp.transpose(wkv[..., MLA_NOPE_DIM:], (0, 2, 1, 3)).astype(BF16)
    dt = jnp.exp(s5_log_dt)[:, :, None]
    mag = jnp.exp(s5_lambda_re * dt)
    lbr, lbi = mag * jnp.cos(s5_lambda_im * dt), mag * jnp.sin(s5_lambda_im * dt)
    den = s5_lambda_re * s5_lambda_re + s5_lambda_im * s5_lambda_im
    qr = ((lbr - 1.0) * s5_lambda_re + lbi * s5_lambda_im) / den
    qi = (lbi * s5_lambda_re - (lbr - 1.0) * s5_lambda_im) / den
    b_re_t, b_im_t = jnp.swapaxes(s5_b_re, 2, 3), jnp.swapaxes(s5_b_im, 2, 3)
    bbr = qr[:, :, None, :] * b_re_t - qi[:, :, None, :] * b_im_t
    bbi = qr[:, :, None, :] * b_im_t + qi[:, :, None, :] * b_re_t
    rows = lambda m: m.reshape(depth, S5_WIDTH, S5_STATE)
    wb, wc = _s5_weights(rows(bbr), rows(bbi), rows(s5_c_re), rows(s5_c_im))
    return dict(
        ln=ln_gain.reshape(depth, 1, D_MODEL), w_in=w_t, wg=wg, bg=gla_b_gate.reshape(depth, 1, GLA_QK),
        gla_gain=jnp.tile(gla_norm_gain, (1, GLA_HEADS)).reshape(depth, 1, GLA_WIDTH),
        gq=gq, wq=wq, gkv=mla_kv_norm_gain.reshape(depth, 1, MLA_KV_RANK), wkv=wkv_r, wka=wka, wv=wv,
        lre=lbr.reshape(depth, 1, S5_NSTATE), lim=lbi.reshape(depth, 1, S5_NSTATE),
        wb=wb, wc=wc, d=s5_d.reshape(depth, 1, S5_WIDTH), wglu=s5_w_glu.astype(BF16),
        bglu=s5_b_glu.reshape(depth, 1, S5_WIDTH), w_out=w_out.astype(BF16))


def _rope_tables(past, t):
    half = MLA_ROPE_DIM // 2
    inv = ROPE_BASE ** (-np.arange(half, dtype=np.float64) / half)
    ang = (past + np.arange(t, dtype=np.float64))[:, None] * inv[None, :]
    cos, sin = np.cos(ang), np.sin(ang)
    pad = MLA_QK_PAD - MLA_NOPE_DIM - MLA_ROPE_DIM
    cos_t = np.concatenate([np.ones((t, MLA_NOPE_DIM)), cos, cos, np.zeros((t, pad))], axis=1)
    sin_t = np.concatenate([np.zeros((t, MLA_NOPE_DIM)), sin, sin, np.zeros((t, pad))], axis=1)
    return jnp.asarray(cos_t, F32), jnp.asarray(sin_t, F32)


def _trunk(x, gla_state, ckv_cache, kpe_cache, s5_re, s5_im, p, final_gain):
    bsz, t, _ = x.shape
    n = bsz * t
    depth = p['w_in'].shape[0]
    past = 0 if ckv_cache is None else ckv_cache.shape[2]
    s_len = past + t
    tl = _tiles(bsz, t, past)
    cos_t, sin_t = _rope_tables(past, t)
    kpet_cache = None if kpe_cache is None else jnp.swapaxes(kpe_cache, 2, 3)
    x2 = x.reshape(n, D_MODEL)
    gain_f = final_gain.reshape(1, D_MODEL)
    gla_o, ckv_o, kpe_o, re_o, im_o = [], [], [], [], []
    pg, pm, ps = _inproj(x2, p['ln'], p['w_in'], 0, tl['row'])
    for l in range(depth):
        s0 = jnp.zeros((bsz, GLA_HEADS, GLA_DK, GLA_DV), F32) if gla_state is None else gla_state[l]
        o_gla, s_new = _gla(pg, p['wg'], p['bg'], p['gla_gain'], l, s0, bsz, t, tl)
        gla_o.append(s_new)
        prep = functools.partial(_mla_prep, pm, cos_t, sin_t, p['gq'], p['wq'], p['gkv'], p['wkv'], l, bsz, t,
                                 tl['prep'])
        if past == 0:
            q, ckv_new, kpet, k_cat, v_all = prep(with_kv=True)
            kpe_o.append(jnp.swapaxes(kpet, 1, 2))
            o_mla = _attn(q, k_cat, v_all, pm, bsz, t, s_len, past, tl['attn_q'], tl['attn_k'])
        else:
            q, ckv_new, kpe128 = prep(with_kv=False)
            kpe_o.append(kpe128[:, MLA_NOPE_DIM:MLA_NOPE_DIM + MLA_ROPE_DIM].reshape(bsz, t, MLA_ROPE_DIM))
            o_mla = _attn_cached(q, pm, ckv_new, kpe128, ckv_cache, kpet_cache, p['wka'], p['wv'], l, bsz, t)
        ckv_o.append(ckv_new.reshape(bsz, t, MLA_KV_RANK))
        x0r = jnp.zeros((bsz, S5_NSTATE), F32) if s5_re is None else s5_re[l].reshape(bsz, S5_NSTATE)
        x0i = jnp.zeros((bsz, S5_NSTATE), F32) if s5_im is None else s5_im[l].reshape(bsz, S5_NSTATE)
        o_s5, xr, xi = _s5(ps.reshape(bsz, t, PS_COLS), x0r, x0i, p['lre'], p['lim'], p['wb'], p['wc'],
                           p['d'], p['wglu'], p['bglu'], l, bsz, t, tl['s5_rows'])
        re_o.append(xr.reshape(bsz, S5_GROUPS, S5_STATE))
        im_o.append(xi.reshape(bsz, S5_GROUPS, S5_STATE))
        o_s5 = o_s5.reshape(n, S5_WIDTH)
        if l < depth - 1:
            x2, pg, pm, ps = _out_in(x2, o_gla, o_mla, o_s5, p['w_out'], p['ln'], p['w_in'], l, tl['row'])
        else:
            x2 = _outproj(x2, o_gla, o_mla, o_s5, p['w_out'], gain_f, l, tl['out_row'], final=True)
    return (x2.reshape(bsz, t, D_MODEL), jnp.stack(gla_o), jnp.stack(ckv_o), jnp.stack(kpe_o),
            jnp.stack(re_o), jnp.stack(im_o))


def kernel(x_prompt, x_sample, state_gla, cache_mla_ckv, cache_mla_kpe, state_s5_re, state_s5_im, ln_gain, w_in, gla_w_gate, gla_b_gate, gla_norm_gain, mla_q_norm_gain, mla_w_uq, mla_kv_norm_gain, mla_w_ukv, s5_lambda_re, s5_lambda_im, s5_b_re, s5_b_im, s5_c_re, s5_c_im, s5_d, s5_log_dt, s5_w_glu, s5_b_glu, w_out, final_gain):
    p = _prepare_params(ln_gain, w_in, gla_w_gate, gla_b_gate, gla_norm_gain, mla_q_norm_gain, mla_w_uq,
                        mla_kv_norm_gain, mla_w_ukv, s5_lambda_re, s5_lambda_im, s5_b_re, s5_b_im,
                        s5_c_re, s5_c_im, s5_d, s5_log_dt, s5_w_glu, s5_b_glu, w_out)
    y_p, gla_p, ckv_p, kpe_p, re_p, im_p = _trunk(x_prompt, None, None, None, None, None, p, final_gain)
    y_s, gla_s, ckv_s, kpe_s, re_s, im_s = _trunk(x_sample, state_gla, cache_mla_ckv, cache_mla_kpe,
                                                  state_s5_re, state_s5_im, p, final_gain)
    return (y_p, y_s, gla_p, ckv_p, kpe_p, re_p, im_p, gla_s, ckv_s, kpe_s, re_s, im_s)
```

```python
import functools
import math

import numpy as np
import jax
import jax.numpy as jnp
from jax import lax
from jax.experimental import pallas as pl
from jax.experimental.pallas import tpu as pltpu

F32 = jnp.float32
BF16 = jnp.bfloat16

LANES = 128
D_MODEL = 1024
CHUNK = 64
EPS = 1e-6
GLA_HEADS = 4
GLA_DV = 64
GLA_DK = 32
GLA_WIDTH = GLA_HEADS * GLA_DV
GLA_QK = GLA_HEADS * GLA_DK
GLA_GATE_RANK = 16
GLA_GATE_TAU = 16.0
GLA_SUB = 16
GLA_PLAIN_MAX_DECAY = 60.0
MLA_HEADS = 4
MLA_NOPE_DIM = 64
MLA_ROPE_DIM = 32
MLA_V_DIM = 128
MLA_Q_RANK = 192
MLA_KV_RANK = 128
MLA_WIDTH = MLA_HEADS * MLA_V_DIM
MLA_QK_PAD = 128
MLA_QK_WIDTH = MLA_HEADS * MLA_QK_PAD
ROPE_BASE = 10000.0
S5_GROUPS = 16
S5_GROUP_CH = 16
S5_STATE = 64
S5_WIDTH = S5_GROUPS * S5_GROUP_CH
S5_NSTATE = S5_GROUPS * S5_STATE
S5_BATCH_TILE = 8

PG_COLS = 896
PM_COLS = 1024
PS_COLS = 512
_IN_SEGS = (('g_q', GLA_QK), ('g_k', GLA_QK), ('g_v', GLA_WIDTH), ('g_lr', GLA_GATE_RANK), ('g_z', GLA_WIDTH),
            ('m_cq', MLA_Q_RANK), ('m_ckv', MLA_KV_RANK), ('m_kr', MLA_ROPE_DIM), ('m_z', MLA_WIDTH),
            ('s_u', S5_WIDTH), ('s_z', S5_WIDTH), ('end', 0))
IN_OFF = dict(zip([n for n, _ in _IN_SEGS], np.cumsum([0] + [w for _, w in _IN_SEGS[:-1]]).tolist()))

VMEM_LIMIT_BYTES = 48 * 1024 * 1024


def _tiles(bsz, t, past):
    n = bsz * t
    s_len = past + t
    return dict(
        row=min(512, n),
        out_row=min(1024, n),
        prep=min(512, n) if t < 512 else 512,
        cached_seqs=math.gcd(bsz, max(1, 64 // t)),
        gla_rows=min(512, t), gla_chunk=min(CHUNK, t),
        gla_seqs=math.gcd(bsz, max(2, 128 // t)),
        attn_q=min(512, t), attn_k=min(512, s_len),
        s5_rows=min(64, t))


def _cparams(*sem):
    return pltpu.CompilerParams(dimension_semantics=sem, vmem_limit_bytes=VMEM_LIMIT_BYTES)


def _sigmoid(x):
    return 0.5 * (1.0 + jnp.tanh(0.5 * x))


def _dot(a, b):
    return jnp.dot(a, b, preferred_element_type=F32)


def _dot_t(a, b):
    return lax.dot_general(a, b, (((0,), (0,)), ((), ())), preferred_element_type=F32)


def _const_spec(shape, ngrid):
    zeros = (0,) * len(shape)
    return pl.BlockSpec(shape, lambda *_: zeros)


def _layer_spec(shape, layer, single_buffer=False):
    zeros = (0,) * len(shape)
    mode = dict(pipeline_mode=pl.Buffered(1)) if single_buffer else {}
    return pl.BlockSpec((None,) + tuple(shape), lambda *_: (layer,) + zeros, **mode)


def _inproj_rows(x, g_ref, w_scr, og_ref, om_ref, os_ref):
    ms = jnp.mean(x * x, axis=-1, keepdims=True)
    h = (x * lax.rsqrt(ms + EPS) * g_ref[...]).astype(BF16)

    def seg(a, b):
        return lax.dot_general(h, w_scr[a:b, :], (((1,), (1,)), ((), ())), preferred_element_type=F32)

    lane = lax.broadcasted_iota(jnp.int32, (x.shape[0], LANES), 1)
    lane2 = lax.broadcasted_iota(jnp.int32, (x.shape[0], 2 * LANES), 1)
    c = IN_OFF
    og_ref[:, 0:512] = seg(c['g_q'], c['g_lr']).astype(BF16)
    og_ref[:, 512:768] = seg(c['g_z'], c['m_cq']).astype(BF16)
    og_ref[:, 768:896] = jnp.where(lane < GLA_GATE_RANK, seg(c['g_lr'], c['g_lr'] + LANES), 0.0).astype(BF16)
    om_ref[:, 0:256] = jnp.where(lane2 < MLA_Q_RANK, seg(c['m_cq'], c['m_cq'] + 2 * LANES), 0.0).astype(BF16)
    om_ref[:, 256:384] = seg(c['m_ckv'], c['m_kr']).astype(BF16)
    kr = seg(c['m_kr'] - MLA_NOPE_DIM, c['m_kr'] - MLA_NOPE_DIM + LANES)
    om_ref[:, 384:512] = jnp.where((lane >= MLA_NOPE_DIM) & (lane < MLA_NOPE_DIM + MLA_ROPE_DIM), kr, 0.0).astype(BF16)
    om_ref[:, 512:1024] = seg(c['m_z'], c['s_u']).astype(BF16)
    os_ref[...] = seg(c['s_u'], c['end']).astype(BF16)


def _inproj_body(x_ref, g_ref, wt_ref, og_ref, om_ref, os_ref, w_scr):
    @pl.when(pl.program_id(0) == 0)
    def _():
        w_scr[...] = wt_ref[...].astype(BF16)

    _inproj_rows(x_ref[...], g_ref, w_scr, og_ref, om_ref, os_ref)


def _inproj(x2, gain, wt, layer, tm):
    n = x2.shape[0]
    return pl.pallas_call(
        _inproj_body,
        grid=(n // tm,),
        in_specs=[pl.BlockSpec((tm, D_MODEL), lambda i: (i, 0)),
                  _layer_spec((1, D_MODEL), layer),
                  _layer_spec((IN_OFF['end'], D_MODEL), layer, single_buffer=True)],
        scratch_shapes=[pltpu.VMEM((IN_OFF['end'], D_MODEL), BF16)],
        out_specs=[pl.BlockSpec((tm, PG_COLS), lambda i: (i, 0)),
                   pl.BlockSpec((tm, PM_COLS), lambda i: (i, 0)),
                   pl.BlockSpec((tm, PS_COLS), lambda i: (i, 0))],
        out_shape=[jax.ShapeDtypeStruct((n, PG_COLS), BF16),
                   jax.ShapeDtypeStruct((n, PM_COLS), BF16),
                   jax.ShapeDtypeStruct((n, PS_COLS), BF16)],
        compiler_params=_cparams("arbitrary"),
        name="inproj",
    )(x2, gain, wt)


def _gla_body(p_ref, wg_ref, bg_ref, gain_ref, ones_k_ref, ones_v_ref, bd_ref, s0_ref,
              o_ref, sout_ref, q_scr, k_scr, v_scr, b_scr, s_scr, qe_scr, qs_scr, ke_scr, vb_scr, o_scr,
              *, chunk, nch, nseq):
    it = pl.program_id(1)
    sub = min(GLA_SUB, chunk)
    nsub = chunk // sub

    @pl.when(it == 0)
    def _():
        s_scr[...] = jnp.zeros(s_scr.shape, F32)
        for sq in range(nseq):
            for h in range(GLA_HEADS):
                s_scr[sq, h * GLA_DK:(h + 1) * GLA_DK, h * GLA_DV:(h + 1) * GLA_DV] = s0_ref[sq, h]

    def cols(a, b):
        parts = [p_ref[sq, :, a:b] for sq in range(nseq)]
        return parts[0] if nseq == 1 else jnp.concatenate(parts, axis=0)

    q_scr[...] = cols(0, 128).astype(F32) * (GLA_DK ** -0.5)
    k_scr[...] = cols(128, 256).astype(F32)
    v_scr[...] = cols(256, 512).astype(F32)
    logit = _dot(cols(768, 896), wg_ref[...]) + bg_ref[...]
    log_a = (jnp.minimum(logit, 0.0) - jnp.log(1.0 + jnp.exp(-jnp.abs(logit)))) * (1.0 / GLA_GATE_TAU)
    row_in_chunk = lax.broadcasted_iota(jnp.int32, log_a.shape, 0) % chunk
    b_all = log_a
    shift = 1
    while shift < chunk:
        b_all = b_all + jnp.where(row_in_chunk >= shift, pltpu.roll(b_all, shift, 0), 0.0)
        shift *= 2
    b_scr[...] = b_all

    row = lax.broadcasted_iota(jnp.int32, (chunk, GLA_QK), 0)
    row_in_sub = lax.broadcasted_iota(jnp.int32, (sub, GLA_QK), 0)

    def decay_columns(b_end):
        col = jnp.transpose(jnp.broadcast_to(jnp.exp(b_end), (GLA_QK, GLA_QK)))
        return jnp.concatenate([col, col], axis=1)

    def load_chunk(sq, c):
        r0 = pl.multiple_of((sq * nch + c) * chunk, chunk)
        return (r0, q_scr[pl.ds(r0, chunk), :], k_scr[pl.ds(r0, chunk), :], v_scr[pl.ds(r0, chunk), :],
                b_scr[pl.ds(r0, chunk), :], b_scr[pl.ds(r0 + chunk - 1, 1), :], s_scr[sq])

    def finish_chunk(sq, r0, o, s_prev, a_state, b_end):
        ms = _dot((o * o).astype(BF16), ones_v_ref[...]) * (1.0 / GLA_DV)
        o_n = o * lax.rsqrt(ms + EPS) * gain_ref[...]
        t0 = r0 - sq * nch * chunk
        z = p_ref[sq, pl.ds(t0, chunk), 512:768].astype(F32)
        o_ref[sq, pl.ds(t0, chunk), :] = (o_n * (z * _sigmoid(z))).astype(BF16)
        s_scr[sq] = s_prev * decay_columns(b_end) + a_state

    def robust_chunk(c, carry, sq):
        r0, qc, kc, vc, bc, b_end, s_prev = load_chunk(sq, c)
        xs = [qc * jnp.exp(bc)]
        ks = []
        for sj in range(nsub - 1):
            e_j = b_scr[pl.ds(r0 + (sj + 1) * sub - 1, 1), :]
            later = row >= (sj + 1) * sub
            xs.append(jnp.where(later, qc * jnp.exp(jnp.where(later, bc - e_j, 0.0)), 0.0))
            own = (row >= sj * sub) & (row < (sj + 1) * sub)
            ks.append(jnp.where(own, kc * jnp.exp(jnp.where(own, e_j - bc, 0.0)), 0.0))
        ks.append(kc * jnp.exp(b_end - bc))
        k_all = jnp.concatenate(ks, axis=1).astype(BF16)
        a_all = _dot_t(k_all, vc.astype(BF16)) * bd_ref[...]
        w = jnp.concatenate([s_prev, a_all[:(nsub - 1) * GLA_QK]], axis=0).astype(BF16) if nsub > 1 \
            else s_prev.astype(BF16)
        o_off = _dot(jnp.concatenate(xs, axis=1).astype(BF16), w)
        rows = []
        for si in range(nsub):
            q_i = qc[si * sub:(si + 1) * sub]
            b_i = bc[si * sub:(si + 1) * sub]
            es = []
            for j in range(sub):
                r = r0 + si * sub + j
                b_j = b_scr[pl.ds(r, 1), :]
                k_j = k_scr[pl.ds(r, 1), :]
                valid = row_in_sub >= j
                es.append(q_i * k_j * jnp.exp(jnp.where(valid, b_i - b_j, -jnp.inf)))
            e_all = jnp.concatenate(es, axis=0)
            e_hi = e_all.astype(BF16)
            e_lo = (e_all - e_hi.astype(F32)).astype(BF16)
            p_all = _dot(e_hi, ones_k_ref[...]) + _dot(e_lo, ones_k_ref[...])
            acc = o_off[si * sub:(si + 1) * sub]
            for j in range(sub):
                v_j = v_scr[pl.ds(r0 + si * sub + j, 1), :]
                acc = acc + p_all[j * sub:(j + 1) * sub] * v_j
            rows.append(acc)
        o = rows[0] if nsub == 1 else jnp.concatenate(rows, axis=0)
        finish_chunk(sq, r0, o, s_prev, a_all[(nsub - 1) * GLA_QK:], b_end)
        return carry

    lane_head_v = lax.broadcasted_iota(jnp.int32, (chunk, GLA_WIDTH), 1) // GLA_DV
    causal = (lax.broadcasted_iota(jnp.int32, (GLA_HEADS * chunk, chunk), 0) % chunk
              >= lax.broadcasted_iota(jnp.int32, (GLA_HEADS * chunk, chunk), 1))

    def plain_block():
        tt = nseq * nch * chunk
        b_all = b_scr[...]
        q_all = q_scr[...]
        k_all = k_scr[...]
        qe = q_all * jnp.exp(b_all)
        lane_head = lax.broadcasted_iota(jnp.int32, (tt, GLA_QK), 1) // GLA_DK
        qe_scr[...] = qe.astype(BF16)
        for h in range(GLA_HEADS):
            qs_scr[h] = jnp.where(lane_head == h, qe, 0.0).astype(BF16)
        ke_scr[...] = (k_all * jnp.exp(-b_all)).astype(BF16)
        vb_scr[...] = cols(256, 512)
        states = [s_scr[sq] for sq in range(nseq)]
        for c, sq in [(c, sq) for c in range(nch) for sq in range(nseq)]:
            r0 = (sq * nch + c) * chunk
            rs = slice(r0, r0 + chunk)
            s_cur = states[sq]
            qs = jnp.concatenate([qs_scr[h, rs, :] for h in range(GLA_HEADS)], axis=0)
            s = lax.dot_general(qs, ke_scr[rs, :], (((1,), (1,)), ((), ())), preferred_element_type=F32)
            s = jnp.where(causal, s, 0.0).astype(BF16)
            r = _dot(s, vb_scr[rs, :])
            o = _dot(qe_scr[rs, :], s_cur.astype(BF16))
            for h in range(GLA_HEADS):
                o = o + jnp.where(lane_head_v == h, r[h * chunk:(h + 1) * chunk], 0.0)
            o_scr[rs, :] = o
            b_end = b_scr[r0 + chunk - 1:r0 + chunk, :]
            k_end = (k_scr[rs, :] * jnp.exp(b_end - b_scr[rs, :])).astype(BF16)
            a_state = _dot_t(k_end, vb_scr[rs, :]) * bd_ref[0:GLA_QK, :]
            states[sq] = s_cur * decay_columns(b_end) + a_state
        for sq in range(nseq):
            s_scr[sq] = states[sq]
        o = o_scr[...]
        ms = _dot((o * o).astype(BF16), ones_v_ref[...]) * (1.0 / GLA_DV)
        o_n = o * lax.rsqrt(ms + EPS) * gain_ref[...]
        z = cols(512, 768).astype(F32)
        o_all = (o_n * (z * _sigmoid(z))).astype(BF16)
        for sq in range(nseq):
            o_ref[sq] = o_all[sq * nch * chunk:(sq + 1) * nch * chunk]

    in_range = jnp.max(-b_scr[...]) < GLA_PLAIN_MAX_DECAY

    @pl.when(in_range)
    def _():
        plain_block()

    @pl.when(jnp.logical_not(in_range))
    def _():
        for sq in range(nseq):
            lax.fori_loop(0, nch, functools.partial(robust_chunk, sq=sq), 0)

    @pl.when(it == pl.num_programs(1) - 1)
    def _():
        for sq in range(nseq):
            for h in range(GLA_HEADS):
                sout_ref[sq, h] = s_scr[sq, h * GLA_DK:(h + 1) * GLA_DK, h * GLA_DV:(h + 1) * GLA_DV]


def _gla_consts(chunk):
    nsub = chunk // min(GLA_SUB, chunk)
    hk = np.arange(GLA_QK) // GLA_DK
    hv = np.arange(GLA_WIDTH) // GLA_DV
    same_kv = (hk[:, None] == hv[None, :]).astype(np.float32)
    same_vv = (hv[:, None] == hv[None, :]).astype(np.float32)
    return (jnp.asarray(same_kv, BF16), jnp.asarray(same_vv, BF16),
            jnp.asarray(np.tile(same_kv, (nsub, 1)), F32))


def _gla(pg, wg, bg, gain, layer, s0_bd, bsz, t, tl):
    chunk, tt, nseq = tl['gla_chunk'], tl['gla_rows'], tl['gla_seqs']
    nch = tt // chunk
    nt = t // tt
    rows = nseq * tt
    nsub = chunk // min(GLA_SUB, chunk)
    ones_k, ones_v, bd = _gla_consts(chunk)
    const = lambda shape: _const_spec(shape, 2)
    return pl.pallas_call(
        functools.partial(_gla_body, chunk=chunk, nch=nch, nseq=nseq),
        grid=(bsz // nseq, nt),
        in_specs=[pl.BlockSpec((nseq, tt, PG_COLS), lambda b, i: (b, i, 0)),
                  _layer_spec((GLA_QK, GLA_QK), layer), _layer_spec((1, GLA_QK), layer),
                  _layer_spec((1, GLA_WIDTH), layer),
                  const((GLA_QK, GLA_WIDTH)), const((GLA_WIDTH, GLA_WIDTH)),
                  const((nsub * GLA_QK, GLA_WIDTH)),
                  pl.BlockSpec((nseq, GLA_HEADS, GLA_DK, GLA_DV), lambda b, i: (b, 0, 0, 0))],
        out_specs=[pl.BlockSpec((nseq, tt, GLA_WIDTH), lambda b, i: (b, i, 0)),
                   pl.BlockSpec((nseq, GLA_HEADS, GLA_DK, GLA_DV), lambda b, i: (b, 0, 0, 0))],
        out_shape=[jax.ShapeDtypeStruct((bsz, t, GLA_WIDTH), BF16),
                   jax.ShapeDtypeStruct((bsz, GLA_HEADS, GLA_DK, GLA_DV), F32)],
        scratch_shapes=[pltpu.VMEM((rows, GLA_QK), F32), pltpu.VMEM((rows, GLA_QK), F32),
                        pltpu.VMEM((rows, GLA_WIDTH), F32), pltpu.VMEM((rows, GLA_QK), F32),
                        pltpu.VMEM((nseq, GLA_QK, GLA_WIDTH), F32),
                        pltpu.VMEM((rows, GLA_QK), BF16), pltpu.VMEM((GLA_HEADS, rows, GLA_QK), BF16),
                        pltpu.VMEM((rows, GLA_QK), BF16),
                        pltpu.VMEM((rows, GLA_WIDTH), BF16), pltpu.VMEM((rows, GLA_WIDTH), F32)],
        compiler_params=_cparams("parallel", "arbitrary"),
        name="gla",
    )(pg, wg, bg, gain, ones_k, ones_v, bd, s0_bd)


def _rope128(x, cos_t, sin_t):
    lane = lax.broadcasted_iota(jnp.int32, x.shape, 1)
    first_half = (lane >= MLA_NOPE_DIM) & (lane < MLA_NOPE_DIM + MLA_ROPE_DIM // 2)
    rot = jnp.where(first_half, -pltpu.roll(x, LANES - MLA_ROPE_DIM // 2, 1), pltpu.roll(x, MLA_ROPE_DIM // 2, 1))
    return x * cos_t + rot * sin_t


def _mla_prep_body(p_ref, cos_ref, sin_ref, gq_ref, wq_ref, gkv_ref, *rest, with_kv):
    if with_kv:
        wkv_ref, q_ref, ckv_ref, kpet_ref, k_ref, v_ref = rest
    else:
        q_ref, ckv_ref, kpe_ref = rest
    cos_t = cos_ref[...]
    sin_t = sin_ref[...]
    cq = p_ref[:, 0:256].astype(F32)
    ms = jnp.sum(cq * cq, axis=-1, keepdims=True) * (1.0 / MLA_Q_RANK)
    cqn = (cq * lax.rsqrt(ms + EPS) * gq_ref[...]).astype(BF16)
    qh = _dot(cqn, wq_ref[...])
    scale = (MLA_NOPE_DIM + MLA_ROPE_DIM) ** -0.5 * math.log2(math.e)
    for h in range(MLA_HEADS):
        x = qh[:, h * MLA_QK_PAD:(h + 1) * MLA_QK_PAD]
        q_ref[:, h * MLA_QK_PAD:(h + 1) * MLA_QK_PAD] = (_rope128(x, cos_t, sin_t) * scale).astype(BF16)
    ckv = p_ref[:, 256:384].astype(F32)
    ms = jnp.mean(ckv * ckv, axis=-1, keepdims=True)
    ckv_n = ckv * lax.rsqrt(ms + EPS) * gkv_ref[...]
    ckv_ref[...] = ckv_n
    kpe128 = _rope128(p_ref[:, 384:512].astype(F32), cos_t, sin_t)
    if with_kv:
        kpet_ref[...] = jnp.transpose(kpe128)[MLA_NOPE_DIM:MLA_NOPE_DIM + MLA_ROPE_DIM, :]
        kv = _dot(ckv_n.astype(BF16), wkv_ref[...])
        for h in range(MLA_HEADS):
            sl = slice(h * MLA_QK_PAD, (h + 1) * MLA_QK_PAD)
            k_ref[:, sl] = (kv[:, sl] + kpe128).astype(BF16)
        v_ref[...] = kv[:, MLA_QK_WIDTH:].astype(BF16)
    else:
        kpe_ref[...] = kpe128


def _mla_prep(pm, cos_t, sin_t, gq, wq, gkv, wkv, layer, bsz, t, tm, with_kv):
    n = bsz * t
    ntab = max(1, t // tm)
    row = lambda c: pl.BlockSpec((tm, c), lambda i: (i, 0))
    in_specs = [row(512),
                pl.BlockSpec((tm, LANES), lambda i: (i % ntab, 0)),
                pl.BlockSpec((tm, LANES), lambda i: (i % ntab, 0)),
                _layer_spec((1, 256), layer), _layer_spec((256, MLA_QK_WIDTH), layer),
                _layer_spec((1, MLA_KV_RANK), layer)]
    out_specs = [row(MLA_QK_WIDTH), row(MLA_KV_RANK)]
    out_shape = [jax.ShapeDtypeStruct((n, MLA_QK_WIDTH), BF16), jax.ShapeDtypeStruct((n, MLA_KV_RANK), F32)]
    args = [pm, cos_t, sin_t, gq, wq, gkv]
    if with_kv:
        in_specs.append(_layer_spec((MLA_KV_RANK, MLA_QK_WIDTH + MLA_WIDTH), layer))
        args.append(wkv)
        out_specs += [pl.BlockSpec((None, MLA_ROPE_DIM, tm), lambda i: (i // ntab, 0, i % ntab)),
                      row(MLA_QK_WIDTH), row(MLA_WIDTH)]
        out_shape += [jax.ShapeDtypeStruct((bsz, MLA_ROPE_DIM, t), F32),
                      jax.ShapeDtypeStruct((n, MLA_QK_WIDTH), BF16), jax.ShapeDtypeStruct((n, MLA_WIDTH), BF16)]
    else:
        out_specs.append(row(LANES))
        out_shape.append(jax.ShapeDtypeStruct((n, LANES), F32))
    return pl.pallas_call(
        functools.partial(_mla_prep_body, with_kv=with_kv),
        grid=(n // tm,),
        in_specs=in_specs, out_specs=out_specs, out_shape=out_shape,
        compiler_params=_cparams("parallel"),
        name="mla_prep_kv" if with_kv else "mla_prep",
    )(*args)


def _attn_cached_body(q_ref, z_ref, ckv_new_ref, kpe_new_ref, ckv_past_ref, kpet_past_ref, wka_ref, wv_ref,
                      o_ref, kpet_scr, *, past, t, nseq):
    last = (((1,), (1,)), ((), ()))
    hsl = [slice(h * MLA_QK_PAD, (h + 1) * MLA_QK_PAD) for h in range(MLA_HEADS)]
    q_lat_h = [_dot(q_ref[:, hsl[h]], wka_ref[h]).astype(BF16) for h in range(MLA_HEADS)]
    kpet_scr[...] = jnp.zeros(kpet_scr.shape, BF16)
    o_lat = []
    for sq in range(nseq):
        rs = slice(sq * t, (sq + 1) * t)
        q_rows = jnp.concatenate([q_ref[rs, hsl[h]] for h in range(MLA_HEADS)], axis=0)
        q_lat = jnp.concatenate([q_lat_h[h][rs] for h in range(MLA_HEADS)], axis=0)
        c_past = ckv_past_ref[sq].astype(BF16)
        c_new = ckv_new_ref[rs, :].astype(BF16)
        kpet_scr[sq, MLA_NOPE_DIM:MLA_NOPE_DIM + MLA_ROPE_DIM, :] = kpet_past_ref[sq].astype(BF16)
        s_past = (lax.dot_general(q_lat, c_past, last, preferred_element_type=F32)
                  + _dot(q_rows, kpet_scr[sq]))
        s_new = (lax.dot_general(q_lat, c_new, last, preferred_element_type=F32)
                 + lax.dot_general(q_rows, kpe_new_ref[rs, :].astype(BF16), last, preferred_element_type=F32))
        if past // CHUNK != (past + t - 1) // CHUNK:
            q_chunk = (past + lax.broadcasted_iota(jnp.int32, s_past.shape, 0) % t) // CHUNK
            s_past = jnp.where(lax.broadcasted_iota(jnp.int32, s_past.shape, 1) // CHUNK <= q_chunk, s_past, -jnp.inf)
            q_chunk = (past + lax.broadcasted_iota(jnp.int32, s_new.shape, 0) % t) // CHUNK
            s_new = jnp.where((past + lax.broadcasted_iota(jnp.int32, s_new.shape, 1)) // CHUNK <= q_chunk,
                              s_new, -jnp.inf)
        m = jnp.maximum(jnp.max(s_past, axis=-1, keepdims=True), jnp.max(s_new, axis=-1, keepdims=True))
        p_past = jnp.exp2(s_past - m)
        p_new = jnp.exp2(s_new - m)
        l = jnp.sum(p_past, axis=-1, keepdims=True) + jnp.sum(p_new, axis=-1, keepdims=True)
        o_lat.append(((_dot(p_past.astype(BF16), c_past) + _dot(p_new.astype(BF16), c_new)) / l).astype(BF16))
    for h in range(MLA_HEADS):
        vsl = slice(h * MLA_V_DIM, (h + 1) * MLA_V_DIM)
        o_h = jnp.concatenate([o_lat[sq][h * t:(h + 1) * t] for sq in range(nseq)], axis=0)
        z = z_ref[:, vsl].astype(F32)
        o_ref[:, vsl] = (_dot(o_h, wv_ref[h]) * (z * _sigmoid(z))).astype(BF16)


def _attn_cached(q, pm, ckv_new, kpe128, ckv_past, kpet_past, wka, wv, layer, bsz, t, nseq):
    past = ckv_past.shape[2]
    rows = nseq * t
    return pl.pallas_call(
        functools.partial(_attn_cached_body, past=past, t=t, nseq=nseq),
        grid=(bsz // nseq,),
        in_specs=[pl.BlockSpec((rows, MLA_QK_WIDTH), lambda b: (b, 0)),
                  pl.BlockSpec((rows, MLA_WIDTH), lambda b: (b, 1)),
                  pl.BlockSpec((rows, MLA_KV_RANK), lambda b: (b, 0)),
                  pl.BlockSpec((rows, LANES), lambda b: (b, 0)),
                  pl.BlockSpec((None, nseq, past, MLA_KV_RANK), lambda b: (layer, b, 0, 0)),
                  pl.BlockSpec((None, nseq, MLA_ROPE_DIM, past), lambda b: (layer, b, 0, 0)),
                  _layer_spec((MLA_HEADS, MLA_QK_PAD, MLA_KV_RANK), layer),
                  _layer_spec((MLA_HEADS, MLA_KV_RANK, MLA_V_DIM), layer)],
        out_specs=pl.BlockSpec((rows, MLA_WIDTH), lambda b: (b, 0)),
        out_shape=jax.ShapeDtypeStruct((bsz * t, MLA_WIDTH), BF16),
        scratch_shapes=[pltpu.VMEM((nseq, MLA_QK_PAD, past), BF16)],
        compiler_params=_cparams("parallel"),
        name="mla_attn_cached",
    )(q, pm, ckv_new, kpe128, ckv_past, kpet_past, wka, wv)


def _attn_body(q_ref, k_ref, v_ref, z_ref, o_ref, m_scr, acc_scr, *, past, tq, tk, s_len):
    iq = pl.program_id(1)
    q_first = past + iq * tq
    full_keys = jnp.minimum((q_first // CHUNK + 1) * CHUNK, s_len)
    vis_keys = jnp.minimum(((q_first + tq - 1) // CHUNK + 1) * CHUNK, s_len)
    n_full = full_keys // tk
    n_vis = (vis_keys + tk - 1) // tk
    m_scr[...] = jnp.full(m_scr.shape, -jnp.inf, F32)
    acc_scr[...] = jnp.zeros(acc_scr.shape, F32)
    split_diagonal = past % tk == 0 and tq == tk and (tq // 2) % CHUNK == 0

    def block(kb, carry, masked):
        k0 = pl.multiple_of(kb * tk, tk)
        if masked and split_diagonal:
            parts = [(r * (tq // 2), tq // 2, (r + 1) * (tk // 2)) for r in range(2)]
        else:
            parts = [(0, tq, tk)]
        for r0, nr, kext in parts:
            rows = slice(r0, r0 + nr)
            if masked:
                q_pos = (r0 if split_diagonal else q_first + r0) + lax.broadcasted_iota(jnp.int32, (nr, kext), 0)
                k_pos = (0 if split_diagonal else k0) + lax.broadcasted_iota(jnp.int32, (nr, kext), 1)
                visible = k_pos // CHUNK <= q_pos // CHUNK
            ntile, rem = kext // LANES, kext % LANES
            ones_v = jnp.ones((kext, MLA_V_DIM), BF16)
            for h in range(MLA_HEADS):
                sl = slice(h * MLA_QK_PAD, (h + 1) * MLA_QK_PAD)
                vsl = slice(h * MLA_V_DIM, (h + 1) * MLA_V_DIM)
                s = lax.dot_general(q_ref[rows, sl], k_ref[pl.ds(k0, kext), sl], (((1,), (1,)), ((), ())),
                                    preferred_element_type=F32)
                if masked:
                    s = jnp.where(visible, s, -jnp.inf)
                m_prev = m_scr[h, rows]
                m_new = jnp.maximum(m_prev, jnp.max(s, axis=-1, keepdims=True))
                alpha = jnp.exp2(m_prev - m_new)
                ps = [jnp.exp2(s[:, c * LANES:(c + 1) * LANES] - m_new) for c in range(ntile)]
                if rem:
                    ps.append(jnp.exp2(s[:, ntile * LANES:] - m_new[:, :rem]))
                p = jnp.concatenate(ps, axis=1).astype(BF16)
                v_ext = jnp.concatenate([v_ref[pl.ds(k0, kext), vsl], ones_v], axis=1)
                acc_scr[h, rows] = jnp.concatenate([alpha, alpha], axis=1) * acc_scr[h, rows] + _dot(p, v_ext)
                m_scr[h, rows] = m_new
        return carry

    lax.fori_loop(0, n_full, functools.partial(block, masked=False), 0)
    lax.fori_loop(n_full, n_vis, functools.partial(block, masked=True), 0)
    for h in range(MLA_HEADS):
        vsl = slice(h * MLA_V_DIM, (h + 1) * MLA_V_DIM)
        z = z_ref[:, vsl].astype(F32)
        acc = acc_scr[h]
        o_ref[:, vsl] = (acc[:, :MLA_V_DIM] / acc[:, MLA_V_DIM:] * (z * _sigmoid(z))).astype(BF16)


def _attn(q, k, v, pm, bsz, t, s_len, past, tq, tk):
    nq = t // tq
    return pl.pallas_call(
        functools.partial(_attn_body, past=past, tq=tq, tk=tk, s_len=s_len),
        grid=(bsz, nq),
        in_specs=[pl.BlockSpec((tq, MLA_QK_WIDTH), lambda b, iq: (b * nq + iq, 0)),
                  pl.BlockSpec((s_len, MLA_QK_WIDTH), lambda b, iq: (b, 0)),
                  pl.BlockSpec((s_len, MLA_WIDTH), lambda b, iq: (b, 0)),
                  pl.BlockSpec((tq, MLA_WIDTH), lambda b, iq: (b * nq + iq, 1))],
        out_specs=pl.BlockSpec((tq, MLA_WIDTH), lambda b, iq: (b * nq + iq, 0)),
        out_shape=jax.ShapeDtypeStruct((bsz * t, MLA_WIDTH), BF16),
        scratch_shapes=[pltpu.VMEM((MLA_HEADS, tq, LANES), F32), pltpu.VMEM((MLA_HEADS, tq, 2 * MLA_V_DIM), F32)],
        compiler_params=_cparams("parallel", "arbitrary"),
        name="mla_attn",
    )(q, k, v, pm)


def _s5_weights_body(bre_ref, bim_ref, cre_ref, cim_ref, spread_ref, mask_ref, wb_ref, wc_ref):
    def expand(m_ref):
        return _dot(m_ref[...].astype(BF16), spread_ref[...]) * mask_ref[...]

    wb_ref[:, 0:S5_NSTATE] = expand(bre_ref).astype(BF16)
    wb_ref[:, S5_NSTATE:] = expand(bim_ref).astype(BF16)
    wc_ref[0:S5_NSTATE, :] = jnp.transpose(expand(cre_ref)).astype(BF16)
    wc_ref[S5_NSTATE:, :] = jnp.transpose(-expand(cim_ref)).astype(BF16)


def _s5_weights(bbr, bbi, c_re, c_im):
    depth = bbr.shape[0]
    g_row = np.arange(S5_WIDTH) // S5_GROUP_CH
    g_col = np.arange(S5_NSTATE) // S5_STATE
    mask = (g_row[:, None] == g_col[None, :]).astype(np.float32)
    spread = (np.arange(S5_STATE)[:, None] == (np.arange(S5_NSTATE) % S5_STATE)[None, :]).astype(np.float32)
    small = pl.BlockSpec((None, S5_WIDTH, S5_STATE), lambda l: (l, 0, 0))
    return pl.pallas_call(
        _s5_weights_body,
        grid=(depth,),
        in_specs=[small, small, small, small,
                  _const_spec((S5_STATE, S5_NSTATE), 1), _const_spec((S5_WIDTH, S5_NSTATE), 1)],
        out_specs=[pl.BlockSpec((None, S5_WIDTH, 2 * S5_NSTATE), lambda l: (l, 0, 0)),
                   pl.BlockSpec((None, 2 * S5_NSTATE, S5_WIDTH), lambda l: (l, 0, 0))],
        out_shape=[jax.ShapeDtypeStruct((depth, S5_WIDTH, 2 * S5_NSTATE), BF16),
                   jax.ShapeDtypeStruct((depth, 2 * S5_NSTATE, S5_WIDTH), BF16)],
        compiler_params=_cparams("parallel"),
        name="s5_weights",
    )(bbr, bbi, c_re, c_im, jnp.asarray(spread, BF16), jnp.asarray(mask, F32))


def _s5_body(p0_ref, pn_ref, x0r_ref, x0i_ref, lre_ref, lim_ref, wb_ref, wc_ref, d_ref, wglu_ref, bglu_ref,
             o_ref, xr_out, xi_out, uz_bt, uz_a, uz_b, uz_c, bu_a, bu_b, bu_c, o_tb, xr_s, xi_s, *, lc, pitch):
    it = pl.program_id(1)
    nb = S5_BATCH_TILE
    ring = ((uz_a, bu_a), (uz_b, bu_b), (uz_c, bu_c))

    def stage_in(blk_ref, uz_tb, bu):
        for b in range(nb):
            for c in range(PS_COLS // LANES):
                uz_bt[c, b * pitch:b * pitch + lc, :] = blk_ref[b, :, c * LANES:(c + 1) * LANES].astype(F32)
        for t in range(lc):
            for c in range(PS_COLS // LANES):
                uz_tb[t * nb:(t + 1) * nb, c * LANES:(c + 1) * LANES] = uz_bt[c, pl.ds(t, nb, stride=pitch), :]
        bu[...] = _dot(uz_tb[:, 0:S5_WIDTH].astype(BF16), wb_ref[...])

    def stage_scan(bu):
        lre = jnp.broadcast_to(lre_ref[...], (nb, S5_NSTATE))
        lim = jnp.broadcast_to(lim_ref[...], (nb, S5_NSTATE))
        xr, xi = xr_s[...], xi_s[...]
        for t in range(lc):
            rs = slice(t * nb, (t + 1) * nb)
            xr, xi = (lre * xr - lim * xi + bu[rs, 0:S5_NSTATE],
                      lre * xi + lim * xr + bu[rs, S5_NSTATE:2 * S5_NSTATE])
            bu[rs, 0:S5_NSTATE] = xr
            bu[rs, S5_NSTATE:2 * S5_NSTATE] = xi
        xr_s[...] = xr
        xi_s[...] = xi

    def stage_out(uz_tb, xs):
        y = _dot(xs[...].astype(BF16), wc_ref[...]) + d_ref[...] * uz_tb[:, 0:S5_WIDTH]
        g5 = 0.5 * y * (1.0 + jnp.tanh(0.7978845608028654 * (y + 0.044715 * (y * y * y))))
        gate = _sigmoid(_dot(g5.astype(BF16), wglu_ref[...]) + bglu_ref[...])
        z = uz_tb[:, S5_WIDTH:2 * S5_WIDTH]
        o = g5 * gate * (z * _sigmoid(z))
        for c in range(S5_WIDTH // LANES):
            o_tb[c] = o[:, c * LANES:(c + 1) * LANES]
        for b in range(nb):
            for c in range(S5_WIDTH // LANES):
                o_ref[b, :, c * LANES:(c + 1) * LANES] = o_tb[c, pl.ds(b, lc, stride=nb), :].astype(BF16)

    @pl.when(it == 0)
    def _():
        xr_s[...] = x0r_ref[...]
        xi_s[...] = x0i_ref[...]
        stage_in(p0_ref, uz_a, bu_a)
        uz_c[...] = jnp.zeros(uz_c.shape, F32)
        bu_c[...] = jnp.zeros(bu_c.shape, F32)

    for r in range(3):
        @pl.when(it % 3 == r)
        def _(r=r):
            stage_in(pn_ref, *ring[(r + 1) % 3])
            stage_scan(ring[r][1])
            stage_out(*ring[(r + 2) % 3])

    @pl.when(it == pl.num_programs(1) - 2)
    def _():
        xr_out[...] = xr_s[...]
        xi_out[...] = xi_s[...]


def _s5(ps3, x0r, x0i, lre, lim, wb, wc, d, wglu, bglu, layer, bsz, t, lc):
    nb = S5_BATCH_TILE
    nt = t // lc
    pitch = lc + 8
    rows = lc * nb
    return pl.pallas_call(
        functools.partial(_s5_body, lc=lc, pitch=pitch),
        grid=(bsz // nb, nt + 1),
        in_specs=[pl.BlockSpec((nb, lc, PS_COLS), lambda g, i: (g, 0, 0)),
                  pl.BlockSpec((nb, lc, PS_COLS), lambda g, i: (g, jnp.minimum(i + 1, nt - 1), 0)),
                  pl.BlockSpec((nb, S5_NSTATE), lambda g, i: (g, 0)),
                  pl.BlockSpec((nb, S5_NSTATE), lambda g, i: (g, 0)),
                  _layer_spec((1, S5_NSTATE), layer), _layer_spec((1, S5_NSTATE), layer),
                  _layer_spec((S5_WIDTH, 2 * S5_NSTATE), layer), _layer_spec((2 * S5_NSTATE, S5_WIDTH), layer),
                  _layer_spec((1, S5_WIDTH), layer), _layer_spec((S5_WIDTH, S5_WIDTH), layer),
                  _layer_spec((1, S5_WIDTH), layer)],
        out_specs=[pl.BlockSpec((nb, lc, S5_WIDTH), lambda g, i: (g, jnp.maximum(i - 1, 0), 0)),
                   pl.BlockSpec((nb, S5_NSTATE), lambda g, i: (g, 0)),
                   pl.BlockSpec((nb, S5_NSTATE), lambda g, i: (g, 0))],
        out_shape=[jax.ShapeDtypeStruct((bsz, t, S5_WIDTH), BF16),
                   jax.ShapeDtypeStruct((bsz, S5_NSTATE), F32),
                   jax.ShapeDtypeStruct((bsz, S5_NSTATE), F32)],
        scratch_shapes=[pltpu.VMEM((PS_COLS // LANES, nb * pitch, LANES), F32)]
                       + [pltpu.VMEM((rows, PS_COLS), F32)] * 3
                       + [pltpu.VMEM((rows, 2 * S5_NSTATE), F32)] * 3
                       + [pltpu.VMEM((S5_WIDTH // LANES, rows, LANES), F32),
                          pltpu.VMEM((nb, S5_NSTATE), F32), pltpu.VMEM((nb, S5_NSTATE), F32)],
        compiler_params=_cparams("parallel", "arbitrary"),
        name="s5",
    )(ps3, ps3, x0r, x0i, lre, lim, wb, wc, d, wglu, bglu)


def _outproj_body(x_ref, og_ref, om_ref, os_ref, w_ref, g_ref, o_ref, *, final):
    acc = _dot(og_ref[...], w_ref[0:GLA_WIDTH, :])
    acc += _dot(om_ref[...], w_ref[GLA_WIDTH:GLA_WIDTH + MLA_WIDTH, :])
    acc += _dot(os_ref[...], w_ref[GLA_WIDTH + MLA_WIDTH:, :])
    xn = x_ref[...] + acc
    if final:
        ms = jnp.mean(xn * xn, axis=-1, keepdims=True)
        xn = xn * lax.rsqrt(ms + EPS) * g_ref[...]
    o_ref[...] = xn


def _out_in_body(x_ref, og_ref, om_ref, os_ref, wo_ref, g_ref, wt_ref, xo_ref, pg_ref, pm_ref, ps_ref, w_scr):
    @pl.when(pl.program_id(0) == 0)
    def _():
        w_scr[...] = wt_ref[...].astype(BF16)

    acc = _dot(og_ref[...], wo_ref[0:GLA_WIDTH, :])
    acc += _dot(om_ref[...], wo_ref[GLA_WIDTH:GLA_WIDTH + MLA_WIDTH, :])
    acc += _dot(os_ref[...], wo_ref[GLA_WIDTH + MLA_WIDTH:, :])
    xn = x_ref[...] + acc
    xo_ref[...] = xn
    _inproj_rows(xn, g_ref, w_scr, pg_ref, pm_ref, ps_ref)


def _out_in(x2, og, om, os_, w_out, ln, wt, layer, tm):
    n = x2.shape[0]
    row = lambda c: pl.BlockSpec((tm, c), lambda i: (i, 0))
    return pl.pallas_call(
        _out_in_body,
        grid=(n // tm,),
        in_specs=[row(D_MODEL), row(GLA_WIDTH), row(MLA_WIDTH), row(S5_WIDTH),
                  _layer_spec((D_MODEL, D_MODEL), layer, single_buffer=True),
                  _layer_spec((1, D_MODEL), layer + 1),
                  _layer_spec((IN_OFF['end'], D_MODEL), layer + 1, single_buffer=True)],
        out_specs=[row(D_MODEL), row(PG_COLS), row(PM_COLS), row(PS_COLS)],
        out_shape=[jax.ShapeDtypeStruct((n, D_MODEL), F32), jax.ShapeDtypeStruct((n, PG_COLS), BF16),
                   jax.ShapeDtypeStruct((n, PM_COLS), BF16), jax.ShapeDtypeStruct((n, PS_COLS), BF16)],
        scratch_shapes=[pltpu.VMEM((IN_OFF['end'], D_MODEL), BF16)],
        compiler_params=_cparams("arbitrary"),
        name="outproj_inproj",
    )(x2, og, om, os_, w_out, ln, wt)


def _outproj(x2, og, om, os_, w, gain, layer, tm, final):
    n = x2.shape[0]
    row = lambda c: pl.BlockSpec((tm, c), lambda i: (i, 0))
    return pl.pallas_call(
        functools.partial(_outproj_body, final=final),
        grid=(n // tm,),
        in_specs=[row(D_MODEL), row(GLA_WIDTH), row(MLA_WIDTH), row(S5_WIDTH),
                  _layer_spec((D_MODEL, D_MODEL), layer),
                  _const_spec((1, D_MODEL), 1)],
        out_specs=row(D_MODEL),
        out_shape=jax.ShapeDtypeStruct((n, D_MODEL), F32),
        compiler_params=_cparams("parallel"),
        name="outproj_final" if final else "outproj",
    )(x2, og, om, os_, w, gain)


def _prepare_params(ln_gain, w_in, gla_w_gate, gla_b_gate, gla_norm_gain, mla_q_norm_gain, mla_w_uq,
                    mla_kv_norm_gain, mla_w_ukv, s5_lambda_re, s5_lambda_im, s5_b_re, s5_b_im, s5_c_re, s5_c_im,
                    s5_d, s5_log_dt, s5_w_glu, s5_b_glu, w_out):
    depth = w_in.shape[0]
    w_t = jnp.swapaxes(w_in, 1, 2)
    wg = jnp.pad(gla_w_gate, ((0, 0), (0, GLA_QK - GLA_GATE_RANK), (0, 0))).astype(BF16)
    wq = mla_w_uq.reshape(depth, MLA_Q_RANK, MLA_HEADS, MLA_NOPE_DIM + MLA_ROPE_DIM)
    wq = jnp.pad(wq, ((0, 0), (0, 256 - MLA_Q_RANK), (0, 0), (0, MLA_QK_PAD - MLA_NOPE_DIM - MLA_ROPE_DIM)))
    wq = wq.reshape(depth, 256, MLA_QK_WIDTH).astype(BF16)
    gq = jnp.pad(mla_q_norm_gain, ((0, 0), (0, 256 - MLA_Q_RANK))).reshape(depth, 1, 256)
    wkv = mla_w_ukv.reshape(depth, MLA_KV_RANK, MLA_HEADS, MLA_NOPE_DIM + MLA_V_DIM)
    wk = jnp.pad(wkv[..., :MLA_NOPE_DIM], ((0, 0), (0, 0), (0, 0), (0, MLA_QK_PAD - MLA_NOPE_DIM)))
    wkv_r = jnp.concatenate([wk.reshape(depth, MLA_KV_RANK, MLA_QK_WIDTH),
                             wkv[..., MLA_NOPE_DIM:].reshape(depth, MLA_KV_RANK, MLA_WIDTH)], axis=2).astype(BF16)
    wka = jnp.pad(jnp.transpose(wkv[..., :MLA_NOPE_DIM], (0, 2, 3, 1)),
                  ((0, 0), (0, 0), (0, MLA_QK_PAD - MLA_NOPE_DIM), (0, 0))).astype(BF16)
    wv = jnp.transpose(wkv[..., MLA_NOPE_DIM:], (0, 2, 1, 3)).astype(BF16)
    dt = jnp.exp(s5_log_dt)[:, :, None]
    mag = jnp.exp(s5_lambda_re * dt)
    lbr, lbi = mag * jnp.cos(s5_lambda_im * dt), mag * jnp.sin(s5_lambda_im * dt)
    den = s5_lambda_re * s5_lambda_re + s5_lambda_im * s5_lambda_im
    qr = ((lbr - 1.0) * s5_lambda_re + lbi * s5_lambda_im) / den
    qi = (lbi * s5_lambda_re - (lbr - 1.0) * s5_lambda_im) / den
    b_re_t, b_im_t = jnp.swapaxes(s5_b_re, 2, 3), jnp.swapaxes(s5_b_im, 2, 3)
    bbr = qr[:, :, None, :] * b_re_t - qi[:, :, None, :] * b_im_t
    bbi = qr[:, :, None, :] * b_im_t + qi[:, :, None, :] * b_re_t
    rows = lambda m: m.reshape(depth, S5_WIDTH, S5_STATE)
    wb, wc = _s5_weights(rows(bbr), rows(bbi), rows(s5_c_re), rows(s5_c_im))
    return dict(
        ln=ln_gain.reshape(depth, 1, D_MODEL), w_in=w_t, wg=wg, bg=gla_b_gate.reshape(depth, 1, GLA_QK),
        gla_gain=jnp.tile(gla_norm_gain, (1, GLA_HEADS)).reshape(depth, 1, GLA_WIDTH),
        gq=gq, wq=wq, gkv=mla_kv_norm_gain.reshape(depth, 1, MLA_KV_RANK), wkv=wkv_r, wka=wka, wv=wv,
        lre=lbr.reshape(depth, 1, S5_NSTATE), lim=lbi.reshape(depth, 1, S5_NSTATE),
        wb=wb, wc=wc, d=s5_d.reshape(depth, 1, S5_WIDTH), wglu=s5_w_glu.astype(BF16),
        bglu=s5_b_glu.reshape(depth, 1, S5_WIDTH), w_out=w_out.astype(BF16))


def _rope_tables(past, t, reps):
    half = MLA_ROPE_DIM // 2
    inv = ROPE_BASE ** (-np.arange(half, dtype=np.float64) / half)
    ang = (past + np.arange(t, dtype=np.float64))[:, None] * inv[None, :]
    cos, sin = np.cos(ang), np.sin(ang)
    pad = MLA_QK_PAD - MLA_NOPE_DIM - MLA_ROPE_DIM
    cos_t = np.concatenate([np.ones((t, MLA_NOPE_DIM)), cos, cos, np.zeros((t, pad))], axis=1)
    sin_t = np.concatenate([np.zeros((t, MLA_NOPE_DIM)), sin, sin, np.zeros((t, pad))], axis=1)
    return jnp.asarray(np.tile(cos_t, (reps, 1)), F32), jnp.asarray(np.tile(sin_t, (reps, 1)), F32)


def _trunk(x, gla_state, ckv_cache, kpe_cache, s5_re, s5_im, p, final_gain):
    bsz, t, _ = x.shape
    n = bsz * t
    depth = p['w_in'].shape[0]
    past = 0 if ckv_cache is None else ckv_cache.shape[2]
    s_len = past + t
    tl = _tiles(bsz, t, past)
    cos_t, sin_t = _rope_tables(past, t, max(1, tl['prep'] // t))
    kpet_cache = None if kpe_cache is None else jnp.swapaxes(kpe_cache, 2, 3)
    x2 = x.reshape(n, D_MODEL)
    gain_f = final_gain.reshape(1, D_MODEL)
    gla_o, ckv_o, kpe_o, re_o, im_o = [], [], [], [], []
    pg, pm, ps = _inproj(x2, p['ln'], p['w_in'], 0, tl['row'])
    for l in range(depth):
        s0 = jnp.zeros((bsz, GLA_HEADS, GLA_DK, GLA_DV), F32) if gla_state is None else gla_state[l]
        o_gla, s_new = _gla(pg.reshape(bsz, t, PG_COLS), p['wg'], p['bg'], p['gla_gain'], l, s0, bsz, t, tl)
        o_gla = o_gla.reshape(n, GLA_WIDTH)
        gla_o.append(s_new)
        prep = functools.partial(_mla_prep, pm, cos_t, sin_t, p['gq'], p['wq'], p['gkv'], p['wkv'], l, bsz, t,
                                 tl['prep'])
        if past == 0:
            q, ckv_new, kpet, k_cat, v_all = prep(with_kv=True)
            kpe_o.append(jnp.swapaxes(kpet, 1, 2))
            o_mla = _attn(q, k_cat, v_all, pm, bsz, t, s_len, past, tl['attn_q'], tl['attn_k'])
        else:
            q, ckv_new, kpe128 = prep(with_kv=False)
            kpe_o.append(kpe128[:, MLA_NOPE_DIM:MLA_NOPE_DIM + MLA_ROPE_DIM].reshape(bsz, t, MLA_ROPE_DIM))
            o_mla = _attn_cached(q, pm, ckv_new, kpe128, ckv_cache, kpet_cache, p['wka'], p['wv'], l, bsz, t,
                                 tl['cached_seqs'])
        ckv_o.append(ckv_new.reshape(bsz, t, MLA_KV_RANK))
        x0r = jnp.zeros((bsz, S5_NSTATE), F32) if s5_re is None else s5_re[l].reshape(bsz, S5_NSTATE)
        x0i = jnp.zeros((bsz, S5_NSTATE), F32) if s5_im is None else s5_im[l].reshape(bsz, S5_NSTATE)
        o_s5, xr, xi = _s5(ps.reshape(bsz, t, PS_COLS), x0r, x0i, p['lre'], p['lim'], p['wb'], p['wc'],
                           p['d'], p['wglu'], p['bglu'], l, bsz, t, tl['s5_rows'])
        re_o.append(xr.reshape(bsz, S5_GROUPS, S5_STATE))
        im_o.append(xi.reshape(bsz, S5_GROUPS, S5_STATE))
        o_s5 = o_s5.reshape(n, S5_WIDTH)
        if l < depth - 1:
            x2, pg, pm, ps = _out_in(x2, o_gla, o_mla, o_s5, p['w_out'], p['ln'], p['w_in'], l, tl['row'])
        else:
            x2 = _outproj(x2, o_gla, o_mla, o_s5, p['w_out'], gain_f, l, tl['out_row'], final=True)
    return (x2.reshape(bsz, t, D_MODEL), jnp.stack(gla_o), jnp.stack(ckv_o), jnp.stack(kpe_o),
            jnp.stack(re_o), jnp.stack(im_o))


def kernel(x_prompt, x_sample, state_gla, cache_mla_ckv, cache_mla_kpe, state_s5_re, state_s5_im, ln_gain, w_in, gla_w_gate, gla_b_gate, gla_norm_gain, mla_q_norm_gain, mla_w_uq, mla_kv_norm_gain, mla_w_ukv, s5_lambda_re, s5_lambda_im, s5_b_re, s5_b_im, s5_c_re, s5_c_im, s5_d, s5_log_dt, s5_w_glu, s5_b_glu, w_out, final_gain):
    p = _prepare_params(ln_gain, w_in, gla_w_gate, gla_b_gate, gla_norm_gain, mla_q_norm_gain, mla_w_uq,
                        mla_kv_norm_gain, mla_w_ukv, s5_lambda_re, s5_lambda_im, s5_b_re, s5_b_im,
                        s5_c_re, s5_c_im, s5_d, s5_log_dt, s5_w_glu, s5_b_glu, w_out)
    y_p, gla_p, ckv_p, kpe_p, re_p, im_p = _trunk(x_prompt, None, None, None, None, None, p, final_gain)
    y_s, gla_s, ckv_s, kpe_s, re_s, im_s = _trunk(x_sample, state_gla, cache_mla_ckv, cache_mla_kpe,
                                                  state_s5_re, state_s5_im, p, final_gain)
    return (y_p, y_s, gla_p, ckv_p, kpe_p, re_p, im_p, gla_s, ckv_s, kpe_s, re_s, im_s)
```

```python
import functools
import math

import numpy as np
import jax
import jax.numpy as jnp
from jax import lax
from jax.experimental import pallas as pl
from jax.experimental.pallas import tpu as pltpu

F32 = jnp.float32
BF16 = jnp.bfloat16

LANES = 128
D_MODEL = 1024
CHUNK = 64
EPS = 1e-6
GLA_HEADS = 4
GLA_DV = 64
GLA_DK = 32
GLA_WIDTH = GLA_HEADS * GLA_DV
GLA_QK = GLA_HEADS * GLA_DK
GLA_GATE_RANK = 16
GLA_GATE_TAU = 16.0
GLA_SUB = 16
GLA_PLAIN_MAX_DECAY = 60.0
MLA_HEADS = 4
MLA_NOPE_DIM = 64
MLA_ROPE_DIM = 32
MLA_V_DIM = 128
MLA_Q_RANK = 192
MLA_KV_RANK = 128
MLA_WIDTH = MLA_HEADS * MLA_V_DIM
MLA_QK_PAD = 128
MLA_QK_WIDTH = MLA_HEADS * MLA_QK_PAD
ROPE_BASE = 10000.0
S5_GROUPS = 16
S5_GROUP_CH = 16
S5_STATE = 64
S5_WIDTH = S5_GROUPS * S5_GROUP_CH
S5_NSTATE = S5_GROUPS * S5_STATE
S5_BATCH_TILE = 8

PG_COLS = 896
PM_COLS = 1024
PS_COLS = 512
_IN_SEGS = (('g_q', GLA_QK), ('g_k', GLA_QK), ('g_v', GLA_WIDTH), ('g_lr', GLA_GATE_RANK), ('g_z', GLA_WIDTH),
            ('m_cq', MLA_Q_RANK), ('m_ckv', MLA_KV_RANK), ('m_kr', MLA_ROPE_DIM), ('m_z', MLA_WIDTH),
            ('s_u', S5_WIDTH), ('s_z', S5_WIDTH), ('end', 0))
IN_OFF = dict(zip([n for n, _ in _IN_SEGS], np.cumsum([0] + [w for _, w in _IN_SEGS[:-1]]).tolist()))

VMEM_LIMIT_BYTES = 48 * 1024 * 1024


def _tiles(bsz, t, past):
    n = bsz * t
    s_len = past + t
    return dict(
        row=min(512, n),
        out_row=min(1024, n),
        prep=min(512, n) if t < 512 else 512,
        cached_seqs=math.gcd(bsz, max(1, 64 // t)),
        gla_rows=min(512, t), gla_chunk=min(CHUNK, t),
        gla_seqs=math.gcd(bsz, max(2, 128 // t)),
        attn_q=min(512, t), attn_k=min(512, s_len),
        s5_rows=min(64, t))


def _cparams(*sem):
    return pltpu.CompilerParams(dimension_semantics=sem, vmem_limit_bytes=VMEM_LIMIT_BYTES)


def _sigmoid(x):
    return 0.5 * (1.0 + jnp.tanh(0.5 * x))


def _dot(a, b):
    return jnp.dot(a, b, preferred_element_type=F32)


def _dot_t(a, b):
    return lax.dot_general(a, b, (((0,), (0,)), ((), ())), preferred_element_type=F32)


def _const_spec(shape, ngrid):
    zeros = (0,) * len(shape)
    return pl.BlockSpec(shape, lambda *_: zeros)


def _layer_spec(shape, layer, single_buffer=False):
    zeros = (0,) * len(shape)
    mode = dict(pipeline_mode=pl.Buffered(1)) if single_buffer else {}
    return pl.BlockSpec((None,) + tuple(shape), lambda *_: (layer,) + zeros, **mode)


def _inproj_rows(x, g_ref, w_scr, og_ref, om_ref, os_ref):
    ms = jnp.mean(x * x, axis=-1, keepdims=True)
    h = (x * lax.rsqrt(ms + EPS) * g_ref[...]).astype(BF16)

    def seg(a, b):
        return lax.dot_general(h, w_scr[a:b, :], (((1,), (1,)), ((), ())), preferred_element_type=F32)

    lane = lax.broadcasted_iota(jnp.int32, (x.shape[0], LANES), 1)
    lane2 = lax.broadcasted_iota(jnp.int32, (x.shape[0], 2 * LANES), 1)
    c = IN_OFF
    og_ref[:, 0:512] = seg(c['g_q'], c['g_lr']).astype(BF16)
    og_ref[:, 512:768] = seg(c['g_z'], c['m_cq']).astype(BF16)
    og_ref[:, 768:896] = jnp.where(lane < GLA_GATE_RANK, seg(c['g_lr'], c['g_lr'] + LANES), 0.0).astype(BF16)
    om_ref[:, 0:256] = jnp.where(lane2 < MLA_Q_RANK, seg(c['m_cq'], c['m_cq'] + 2 * LANES), 0.0).astype(BF16)
    ckv_kr = seg(c['m_ckv'], c['m_ckv'] + 2 * LANES)
    om_ref[:, 256:384] = ckv_kr[:, 0:LANES].astype(BF16)
    kr = pltpu.roll(ckv_kr[:, LANES:], MLA_NOPE_DIM, 1)
    om_ref[:, 384:512] = jnp.where((lane >= MLA_NOPE_DIM) & (lane < MLA_NOPE_DIM + MLA_ROPE_DIM), kr, 0.0).astype(BF16)
    om_ref[:, 512:1024] = seg(c['m_z'], c['s_u']).astype(BF16)
    os_ref[...] = seg(c['s_u'], c['end']).astype(BF16)


def _inproj_body(x_ref, g_ref, wt_ref, og_ref, om_ref, os_ref, w_scr):
    @pl.when(pl.program_id(0) == 0)
    def _():
        w_scr[...] = wt_ref[...].astype(BF16)

    _inproj_rows(x_ref[...], g_ref, w_scr, og_ref, om_ref, os_ref)


def _inproj(x2, gain, wt, layer, tm):
    n = x2.shape[0]
    return pl.pallas_call(
        _inproj_body,
        grid=(n // tm,),
        in_specs=[pl.BlockSpec((tm, D_MODEL), lambda i: (i, 0)),
                  _layer_spec((1, D_MODEL), layer),
                  _layer_spec((IN_OFF['end'], D_MODEL), layer, single_buffer=True)],
        scratch_shapes=[pltpu.VMEM((IN_OFF['end'], D_MODEL), BF16)],
        out_specs=[pl.BlockSpec((tm, PG_COLS), lambda i: (i, 0)),
                   pl.BlockSpec((tm, PM_COLS), lambda i: (i, 0)),
                   pl.BlockSpec((tm, PS_COLS), lambda i: (i, 0))],
        out_shape=[jax.ShapeDtypeStruct((n, PG_COLS), BF16),
                   jax.ShapeDtypeStruct((n, PM_COLS), BF16),
                   jax.ShapeDtypeStruct((n, PS_COLS), BF16)],
        compiler_params=_cparams("arbitrary"),
        name="inproj",
    )(x2, gain, wt)


def _gla_body(p_ref, wg_ref, bg_ref, gain_ref, ones_k_ref, ones_v_ref, bd_ref, s0_ref,
              o_ref, sout_ref, q_scr, k_scr, v_scr, b_scr, s_scr, qe_scr, qs_scr, ke_scr, vb_scr, o_scr,
              *, chunk, nch, nseq):
    it = pl.program_id(1)
    sub = min(GLA_SUB, chunk)
    nsub = chunk // sub

    @pl.when(it == 0)
    def _():
        s_scr[...] = jnp.zeros(s_scr.shape, F32)
        for sq in range(nseq):
            for h in range(GLA_HEADS):
                s_scr[sq, h * GLA_DK:(h + 1) * GLA_DK, h * GLA_DV:(h + 1) * GLA_DV] = s0_ref[sq, h]

    def cols(a, b):
        parts = [p_ref[sq, :, a:b] for sq in range(nseq)]
        return parts[0] if nseq == 1 else jnp.concatenate(parts, axis=0)

    q_scr[...] = cols(0, 128).astype(F32) * (GLA_DK ** -0.5)
    k_scr[...] = cols(128, 256).astype(F32)
    v_scr[...] = cols(256, 512).astype(F32)
    logit = _dot(cols(768, 896), wg_ref[...]) + bg_ref[...]
    log_a = (jnp.minimum(logit, 0.0) - jnp.log(1.0 + jnp.exp(-jnp.abs(logit)))) * (1.0 / GLA_GATE_TAU)
    row_in_chunk = lax.broadcasted_iota(jnp.int32, log_a.shape, 0) % chunk
    b_all = log_a
    shift = 1
    while shift < chunk:
        b_all = b_all + jnp.where(row_in_chunk >= shift, pltpu.roll(b_all, shift, 0), 0.0)
        shift *= 2
    b_scr[...] = b_all

    row = lax.broadcasted_iota(jnp.int32, (chunk, GLA_QK), 0)
    row_in_sub = lax.broadcasted_iota(jnp.int32, (sub, GLA_QK), 0)

    def decay_columns(b_end):
        col = jnp.transpose(jnp.broadcast_to(jnp.exp(b_end), (GLA_QK, GLA_QK)))
        return jnp.concatenate([col, col], axis=1)

    def load_chunk(sq, c):
        r0 = pl.multiple_of((sq * nch + c) * chunk, chunk)
        return (r0, q_scr[pl.ds(r0, chunk), :], k_scr[pl.ds(r0, chunk), :], v_scr[pl.ds(r0, chunk), :],
                b_scr[pl.ds(r0, chunk), :], b_scr[pl.ds(r0 + chunk - 1, 1), :], s_scr[sq])

    def finish_chunk(sq, r0, o, s_prev, a_state, b_end):
        ms = _dot((o * o).astype(BF16), ones_v_ref[...]) * (1.0 / GLA_DV)
        o_n = o * lax.rsqrt(ms + EPS) * gain_ref[...]
        t0 = r0 - sq * nch * chunk
        z = p_ref[sq, pl.ds(t0, chunk), 512:768].astype(F32)
        o_ref[sq, pl.ds(t0, chunk), :] = (o_n * (z * _sigmoid(z))).astype(BF16)
        s_scr[sq] = s_prev * decay_columns(b_end) + a_state

    def robust_chunk(c, carry, sq):
        r0, qc, kc, vc, bc, b_end, s_prev = load_chunk(sq, c)
        xs = [qc * jnp.exp(bc)]
        ks = []
        for sj in range(nsub - 1):
            e_j = b_scr[pl.ds(r0 + (sj + 1) * sub - 1, 1), :]
            later = row >= (sj + 1) * sub
            xs.append(jnp.where(later, qc * jnp.exp(jnp.where(later, bc - e_j, 0.0)), 0.0))
            own = (row >= sj * sub) & (row < (sj + 1) * sub)
            ks.append(jnp.where(own, kc * jnp.exp(jnp.where(own, e_j - bc, 0.0)), 0.0))
        ks.append(kc * jnp.exp(b_end - bc))
        k_all = jnp.concatenate(ks, axis=1).astype(BF16)
        a_all = _dot_t(k_all, vc.astype(BF16)) * bd_ref[...]
        w = jnp.concatenate([s_prev, a_all[:(nsub - 1) * GLA_QK]], axis=0).astype(BF16) if nsub > 1 \
            else s_prev.astype(BF16)
        o_off = _dot(jnp.concatenate(xs, axis=1).astype(BF16), w)
        rows = []
        for si in range(nsub):
            q_i = qc[si * sub:(si + 1) * sub]
            b_i = bc[si * sub:(si + 1) * sub]
            es = []
            for j in range(sub):
                r = r0 + si * sub + j
                b_j = b_scr[pl.ds(r, 1), :]
                k_j = k_scr[pl.ds(r, 1), :]
                valid = row_in_sub >= j
                es.append(q_i * k_j * jnp.exp(jnp.where(valid, b_i - b_j, -jnp.inf)))
            e_all = jnp.concatenate(es, axis=0)
            e_hi = e_all.astype(BF16)
            e_lo = (e_all - e_hi.astype(F32)).astype(BF16)
            p_all = _dot(e_hi, ones_k_ref[...]) + _dot(e_lo, ones_k_ref[...])
            acc = o_off[si * sub:(si + 1) * sub]
            for j in range(sub):
                v_j = v_scr[pl.ds(r0 + si * sub + j, 1), :]
                acc = acc + p_all[j * sub:(j + 1) * sub] * v_j
            rows.append(acc)
        o = rows[0] if nsub == 1 else jnp.concatenate(rows, axis=0)
        finish_chunk(sq, r0, o, s_prev, a_all[(nsub - 1) * GLA_QK:], b_end)
        return carry

    lane_head_v = lax.broadcasted_iota(jnp.int32, (chunk, GLA_WIDTH), 1) // GLA_DV
    causal = (lax.broadcasted_iota(jnp.int32, (GLA_HEADS * chunk, chunk), 0) % chunk
              >= lax.broadcasted_iota(jnp.int32, (GLA_HEADS * chunk, chunk), 1))

    def plain_block():
        tt = nseq * nch * chunk
        b_all = b_scr[...]
        q_all = q_scr[...]
        k_all = k_scr[...]
        qe = q_all * jnp.exp(b_all)
        lane_head = lax.broadcasted_iota(jnp.int32, (tt, GLA_QK), 1) // GLA_DK
        qe_scr[...] = qe.astype(BF16)
        for h in range(GLA_HEADS):
            qs_scr[h] = jnp.where(lane_head == h, qe, 0.0).astype(BF16)
        ke_scr[...] = (k_all * jnp.exp(-b_all)).astype(BF16)
        vb_scr[...] = cols(256, 512)
        states = [s_scr[sq] for sq in range(nseq)]
        for c, sq in [(c, sq) for c in range(nch) for sq in range(nseq)]:
            r0 = (sq * nch + c) * chunk
            rs = slice(r0, r0 + chunk)
            s_cur = states[sq]
            qs = jnp.concatenate([qs_scr[h, rs, :] for h in range(GLA_HEADS)], axis=0)
            s = lax.dot_general(qs, ke_scr[rs, :], (((1,), (1,)), ((), ())), preferred_element_type=F32)
            s = jnp.where(causal, s, 0.0).astype(BF16)
            r = _dot(s, vb_scr[rs, :])
            o = _dot(qe_scr[rs, :], s_cur.astype(BF16))
            for h in range(GLA_HEADS):
                o = o + jnp.where(lane_head_v == h, r[h * chunk:(h + 1) * chunk], 0.0)
            o_scr[rs, :] = o
            b_end = b_scr[r0 + chunk - 1:r0 + chunk, :]
            k_end = (k_scr[rs, :] * jnp.exp(b_end - b_scr[rs, :])).astype(BF16)
            a_state = _dot_t(k_end, vb_scr[rs, :]) * bd_ref[0:GLA_QK, :]
            states[sq] = s_cur * decay_columns(b_end) + a_state
        for sq in range(nseq):
            s_scr[sq] = states[sq]
        o = o_scr[...]
        ms = _dot((o * o).astype(BF16), ones_v_ref[...]) * (1.0 / GLA_DV)
        o_n = o * lax.rsqrt(ms + EPS) * gain_ref[...]
        z = cols(512, 768).astype(F32)
        o_all = (o_n * (z * _sigmoid(z))).astype(BF16)
        for sq in range(nseq):
            o_ref[sq] = o_all[sq * nch * chunk:(sq + 1) * nch * chunk]

    in_range = jnp.max(-b_scr[...]) < GLA_PLAIN_MAX_DECAY

    @pl.when(in_range)
    def _():
        plain_block()

    @pl.when(jnp.logical_not(in_range))
    def _():
        for sq in range(nseq):
            lax.fori_loop(0, nch, functools.partial(robust_chunk, sq=sq), 0)

    @pl.when(it == pl.num_programs(1) - 1)
    def _():
        for sq in range(nseq):
            for h in range(GLA_HEADS):
                sout_ref[sq, h] = s_scr[sq, h * GLA_DK:(h + 1) * GLA_DK, h * GLA_DV:(h + 1) * GLA_DV]


def _gla_consts(chunk):
    nsub = chunk // min(GLA_SUB, chunk)
    hk = np.arange(GLA_QK) // GLA_DK
    hv = np.arange(GLA_WIDTH) // GLA_DV
    same_kv = (hk[:, None] == hv[None, :]).astype(np.float32)
    same_vv = (hv[:, None] == hv[None, :]).astype(np.float32)
    return (jnp.asarray(same_kv, BF16), jnp.asarray(same_vv, BF16),
            jnp.asarray(np.tile(same_kv, (nsub, 1)), F32))


def _gla(pg, wg, bg, gain, layer, s0_bd, bsz, t, tl):
    chunk, tt, nseq = tl['gla_chunk'], tl['gla_rows'], tl['gla_seqs']
    nch = tt // chunk
    nt = t // tt
    rows = nseq * tt
    nsub = chunk // min(GLA_SUB, chunk)
    ones_k, ones_v, bd = _gla_consts(chunk)
    const = lambda shape: _const_spec(shape, 2)
    return pl.pallas_call(
        functools.partial(_gla_body, chunk=chunk, nch=nch, nseq=nseq),
        grid=(bsz // nseq, nt),
        in_specs=[pl.BlockSpec((nseq, tt, PG_COLS), lambda b, i: (b, i, 0)),
                  _layer_spec((GLA_QK, GLA_QK), layer), _layer_spec((1, GLA_QK), layer),
                  _layer_spec((1, GLA_WIDTH), layer),
                  const((GLA_QK, GLA_WIDTH)), const((GLA_WIDTH, GLA_WIDTH)),
                  const((nsub * GLA_QK, GLA_WIDTH)),
                  pl.BlockSpec((nseq, GLA_HEADS, GLA_DK, GLA_DV), lambda b, i: (b, 0, 0, 0))],
        out_specs=[pl.BlockSpec((nseq, tt, GLA_WIDTH), lambda b, i: (b, i, 0)),
                   pl.BlockSpec((nseq, GLA_HEADS, GLA_DK, GLA_DV), lambda b, i: (b, 0, 0, 0))],
        out_shape=[jax.ShapeDtypeStruct((bsz, t, GLA_WIDTH), BF16),
                   jax.ShapeDtypeStruct((bsz, GLA_HEADS, GLA_DK, GLA_DV), F32)],
        scratch_shapes=[pltpu.VMEM((rows, GLA_QK), F32), pltpu.VMEM((rows, GLA_QK), F32),
                        pltpu.VMEM((rows, GLA_WIDTH), F32), pltpu.VMEM((rows, GLA_QK), F32),
                        pltpu.VMEM((nseq, GLA_QK, GLA_WIDTH), F32),
                        pltpu.VMEM((rows, GLA_QK), BF16), pltpu.VMEM((GLA_HEADS, rows, GLA_QK), BF16),
                        pltpu.VMEM((rows, GLA_QK), BF16),
                        pltpu.VMEM((rows, GLA_WIDTH), BF16), pltpu.VMEM((rows, GLA_WIDTH), F32)],
        compiler_params=_cparams("parallel", "arbitrary"),
        name="gla",
    )(pg, wg, bg, gain, ones_k, ones_v, bd, s0_bd)


def _rope128(x, cos_t, sin_t):
    lane = lax.broadcasted_iota(jnp.int32, x.shape, 1)
    first_half = (lane >= MLA_NOPE_DIM) & (lane < MLA_NOPE_DIM + MLA_ROPE_DIM // 2)
    rot = jnp.where(first_half, -pltpu.roll(x, LANES - MLA_ROPE_DIM // 2, 1), pltpu.roll(x, MLA_ROPE_DIM // 2, 1))
    return x * cos_t + rot * sin_t


def _mla_prep_body(p_ref, cos_ref, sin_ref, gq_ref, wq_ref, gkv_ref, *rest, with_kv):
    if with_kv:
        wkv_ref, q_ref, ckv_ref, kpet_ref, k_ref, v_ref = rest
    else:
        q_ref, ckv_ref, kpe_ref = rest
    cos_t = cos_ref[...]
    sin_t = sin_ref[...]
    cq = p_ref[:, 0:256].astype(F32)
    ms = jnp.sum(cq * cq, axis=-1, keepdims=True) * (1.0 / MLA_Q_RANK)
    cqn = (cq * lax.rsqrt(ms + EPS) * gq_ref[...]).astype(BF16)
    qh = _dot(cqn, wq_ref[...])
    scale = (MLA_NOPE_DIM + MLA_ROPE_DIM) ** -0.5 * math.log2(math.e)
    for h in range(MLA_HEADS):
        x = qh[:, h * MLA_QK_PAD:(h + 1) * MLA_QK_PAD]
        q_ref[:, h * MLA_QK_PAD:(h + 1) * MLA_QK_PAD] = (_rope128(x, cos_t, sin_t) * scale).astype(BF16)
    ckv = p_ref[:, 256:384].astype(F32)
    ms = jnp.mean(ckv * ckv, axis=-1, keepdims=True)
    ckv_n = ckv * lax.rsqrt(ms + EPS) * gkv_ref[...]
    ckv_ref[...] = ckv_n
    kpe128 = _rope128(p_ref[:, 384:512].astype(F32), cos_t, sin_t)
    if with_kv:
        kpet_ref[...] = jnp.transpose(kpe128)[MLA_NOPE_DIM:MLA_NOPE_DIM + MLA_ROPE_DIM, :]
        kv = _dot(ckv_n.astype(BF16), wkv_ref[...])
        for h in range(MLA_HEADS):
            sl = slice(h * MLA_QK_PAD, (h + 1) * MLA_QK_PAD)
            k_ref[:, sl] = (kv[:, sl] + kpe128).astype(BF16)
        v_ref[...] = kv[:, MLA_QK_WIDTH:].astype(BF16)
    else:
        kpe_ref[...] = kpe128


def _mla_prep(pm, cos_t, sin_t, gq, wq, gkv, wkv, layer, bsz, t, tm, with_kv):
    n = bsz * t
    ntab = max(1, t // tm)
    row = lambda c: pl.BlockSpec((tm, c), lambda i: (i, 0))
    in_specs = [row(512),
                pl.BlockSpec((tm, LANES), lambda i: (i % ntab, 0)),
                pl.BlockSpec((tm, LANES), lambda i: (i % ntab, 0)),
                _layer_spec((1, 256), layer), _layer_spec((256, MLA_QK_WIDTH), layer),
                _layer_spec((1, MLA_KV_RANK), layer)]
    out_specs = [row(MLA_QK_WIDTH), row(MLA_KV_RANK)]
    out_shape = [jax.ShapeDtypeStruct((n, MLA_QK_WIDTH), BF16), jax.ShapeDtypeStruct((n, MLA_KV_RANK), F32)]
    args = [pm, cos_t, sin_t, gq, wq, gkv]
    if with_kv:
        in_specs.append(_layer_spec((MLA_KV_RANK, MLA_QK_WIDTH + MLA_WIDTH), layer))
        args.append(wkv)
        out_specs += [pl.BlockSpec((None, MLA_ROPE_DIM, tm), lambda i: (i // ntab, 0, i % ntab)),
                      row(MLA_QK_WIDTH), row(MLA_WIDTH)]
        out_shape += [jax.ShapeDtypeStruct((bsz, MLA_ROPE_DIM, t), F32),
                      jax.ShapeDtypeStruct((n, MLA_QK_WIDTH), BF16), jax.ShapeDtypeStruct((n, MLA_WIDTH), BF16)]
    else:
        out_specs.append(row(LANES))
        out_shape.append(jax.ShapeDtypeStruct((n, LANES), F32))
    return pl.pallas_call(
        functools.partial(_mla_prep_body, with_kv=with_kv),
        grid=(n // tm,),
        in_specs=in_specs, out_specs=out_specs, out_shape=out_shape,
        compiler_params=_cparams("parallel"),
        name="mla_prep_kv" if with_kv else "mla_prep",
    )(*args)


def _attn_cached_body(q_ref, z_ref, ckv_new_ref, kpe_new_ref, ckv_past_ref, kpet_past_ref, wka_ref, wv_ref,
                      o_ref, kpet_scr, *, past, t, nseq):
    last = (((1,), (1,)), ((), ()))
    hsl = [slice(h * MLA_QK_PAD, (h + 1) * MLA_QK_PAD) for h in range(MLA_HEADS)]
    q_lat_h = [_dot(q_ref[:, hsl[h]], wka_ref[h]).astype(BF16) for h in range(MLA_HEADS)]
    kpet_scr[...] = jnp.zeros(kpet_scr.shape, BF16)
    o_lat = []
    for sq in range(nseq):
        rs = slice(sq * t, (sq + 1) * t)
        q_rows = jnp.concatenate([q_ref[rs, hsl[h]] for h in range(MLA_HEADS)], axis=0)
        q_lat = jnp.concatenate([q_lat_h[h][rs] for h in range(MLA_HEADS)], axis=0)
        c_past = ckv_past_ref[sq].astype(BF16)
        c_new = ckv_new_ref[rs, :].astype(BF16)
        kpet_scr[sq, MLA_NOPE_DIM:MLA_NOPE_DIM + MLA_ROPE_DIM, :] = kpet_past_ref[sq].astype(BF16)
        s_past = (lax.dot_general(q_lat, c_past, last, preferred_element_type=F32)
                  + _dot(q_rows, kpet_scr[sq]))
        s_new = (lax.dot_general(q_lat, c_new, last, preferred_element_type=F32)
                 + lax.dot_general(q_rows, kpe_new_ref[rs, :].astype(BF16), last, preferred_element_type=F32))
        if past // CHUNK != (past + t - 1) // CHUNK:
            q_chunk = (past + lax.broadcasted_iota(jnp.int32, s_past.shape, 0) % t) // CHUNK
            s_past = jnp.where(lax.broadcasted_iota(jnp.int32, s_past.shape, 1) // CHUNK <= q_chunk, s_past, -jnp.inf)
            q_chunk = (past + lax.broadcasted_iota(jnp.int32, s_new.shape, 0) % t) // CHUNK
            s_new = jnp.where((past + lax.broadcasted_iota(jnp.int32, s_new.shape, 1)) // CHUNK <= q_chunk,
                              s_new, -jnp.inf)
        m = jnp.maximum(jnp.max(s_past, axis=-1, keepdims=True), jnp.max(s_new, axis=-1, keepdims=True))
        p_past = jnp.exp2(s_past - m)
        p_new = jnp.exp2(s_new - m)
        l = jnp.sum(p_past, axis=-1, keepdims=True) + jnp.sum(p_new, axis=-1, keepdims=True)
        o_lat.append(((_dot(p_past.astype(BF16), c_past) + _dot(p_new.astype(BF16), c_new)) / l).astype(BF16))
    for h in range(MLA_HEADS):
        vsl = slice(h * MLA_V_DIM, (h + 1) * MLA_V_DIM)
        o_h = jnp.concatenate([o_lat[sq][h * t:(h + 1) * t] for sq in range(nseq)], axis=0)
        z = z_ref[:, vsl].astype(F32)
        o_ref[:, vsl] = (_dot(o_h, wv_ref[h]) * (z * _sigmoid(z))).astype(BF16)


def _attn_cached(q, pm, ckv_new, kpe128, ckv_past, kpet_past, wka, wv, layer, bsz, t, nseq):
    past = ckv_past.shape[2]
    rows = nseq * t
    return pl.pallas_call(
        functools.partial(_attn_cached_body, past=past, t=t, nseq=nseq),
        grid=(bsz // nseq,),
        in_specs=[pl.BlockSpec((rows, MLA_QK_WIDTH), lambda b: (b, 0)),
                  pl.BlockSpec((rows, MLA_WIDTH), lambda b: (b, 1)),
                  pl.BlockSpec((rows, MLA_KV_RANK), lambda b: (b, 0)),
                  pl.BlockSpec((rows, LANES), lambda b: (b, 0)),
                  pl.BlockSpec((None, nseq, past, MLA_KV_RANK), lambda b: (layer, b, 0, 0)),
                  pl.BlockSpec((None, nseq, MLA_ROPE_DIM, past), lambda b: (layer, b, 0, 0)),
                  _layer_spec((MLA_HEADS, MLA_QK_PAD, MLA_KV_RANK), layer),
                  _layer_spec((MLA_HEADS, MLA_KV_RANK, MLA_V_DIM), layer)],
        out_specs=pl.BlockSpec((rows, MLA_WIDTH), lambda b: (b, 0)),
        out_shape=jax.ShapeDtypeStruct((bsz * t, MLA_WIDTH), BF16),
        scratch_shapes=[pltpu.VMEM((nseq, MLA_QK_PAD, past), BF16)],
        compiler_params=_cparams("parallel"),
        name="mla_attn_cached",
    )(q, pm, ckv_new, kpe128, ckv_past, kpet_past, wka, wv)


def _attn_body(q_ref, k_ref, v_ref, z_ref, o_ref, m_scr, acc_scr, *, past, tq, tk, s_len):
    iq = pl.program_id(1)
    q_first = past + iq * tq
    full_keys = jnp.minimum((q_first // CHUNK + 1) * CHUNK, s_len)
    vis_keys = jnp.minimum(((q_first + tq - 1) // CHUNK + 1) * CHUNK, s_len)
    n_full = full_keys // tk
    n_vis = (vis_keys + tk - 1) // tk
    m_scr[...] = jnp.full(m_scr.shape, -jnp.inf, F32)
    acc_scr[...] = jnp.zeros(acc_scr.shape, F32)
    split_diagonal = past % tk == 0 and tq == tk and (tq // 2) % CHUNK == 0

    def block(kb, carry, masked):
        k0 = pl.multiple_of(kb * tk, tk)
        if masked and split_diagonal:
            parts = [(r * (tq // 2), tq // 2, (r + 1) * (tk // 2)) for r in range(2)]
        else:
            parts = [(0, tq, tk)]
        for r0, nr, kext in parts:
            rows = slice(r0, r0 + nr)
            if masked:
                q_pos = (r0 if split_diagonal else q_first + r0) + lax.broadcasted_iota(jnp.int32, (nr, kext), 0)
                k_pos = (0 if split_diagonal else k0) + lax.broadcasted_iota(jnp.int32, (nr, kext), 1)
                visible = k_pos // CHUNK <= q_pos // CHUNK
            ntile, rem = kext // LANES, kext % LANES
            ones_v = jnp.ones((kext, MLA_V_DIM), BF16)
            for h in range(MLA_HEADS):
                sl = slice(h * MLA_QK_PAD, (h + 1) * MLA_QK_PAD)
                vsl = slice(h * MLA_V_DIM, (h + 1) * MLA_V_DIM)
                s = lax.dot_general(q_ref[rows, sl], k_ref[pl.ds(k0, kext), sl], (((1,), (1,)), ((), ())),
                                    preferred_element_type=F32)
                if masked:
                    s = jnp.where(visible, s, -jnp.inf)
                m_prev = m_scr[h, rows]
                m_new = jnp.maximum(m_prev, jnp.max(s, axis=-1, keepdims=True))
                alpha = jnp.exp2(m_prev - m_new)
                ps = [jnp.exp2(s[:, c * LANES:(c + 1) * LANES] - m_new) for c in range(ntile)]
                if rem:
                    ps.append(jnp.exp2(s[:, ntile * LANES:] - m_new[:, :rem]))
                p = jnp.concatenate(ps, axis=1).astype(BF16)
                v_ext = jnp.concatenate([v_ref[pl.ds(k0, kext), vsl], ones_v], axis=1)
                acc_scr[h, rows] = jnp.concatenate([alpha, alpha], axis=1) * acc_scr[h, rows] + _dot(p, v_ext)
                m_scr[h, rows] = m_new
        return carry

    lax.fori_loop(0, n_full, functools.partial(block, masked=False), 0)
    lax.fori_loop(n_full, n_vis, functools.partial(block, masked=True), 0)
    for h in range(MLA_HEADS):
        vsl = slice(h * MLA_V_DIM, (h + 1) * MLA_V_DIM)
        z = z_ref[:, vsl].astype(F32)
        acc = acc_scr[h]
        o_ref[:, vsl] = (acc[:, :MLA_V_DIM] / acc[:, MLA_V_DIM:] * (z * _sigmoid(z))).astype(BF16)


def _attn(q, k, v, pm, bsz, t, s_len, past, tq, tk):
    nq = t // tq
    return pl.pallas_call(
        functools.partial(_attn_body, past=past, tq=tq, tk=tk, s_len=s_len),
        grid=(bsz, nq),
        in_specs=[pl.BlockSpec((tq, MLA_QK_WIDTH), lambda b, iq: (b * nq + iq, 0)),
                  pl.BlockSpec((s_len, MLA_QK_WIDTH), lambda b, iq: (b, 0)),
                  pl.BlockSpec((s_len, MLA_WIDTH), lambda b, iq: (b, 0)),
                  pl.BlockSpec((tq, MLA_WIDTH), lambda b, iq: (b * nq + iq, 1))],
        out_specs=pl.BlockSpec((tq, MLA_WIDTH), lambda b, iq: (b * nq + iq, 0)),
        out_shape=jax.ShapeDtypeStruct((bsz * t, MLA_WIDTH), BF16),
        scratch_shapes=[pltpu.VMEM((MLA_HEADS, tq, LANES), F32), pltpu.VMEM((MLA_HEADS, tq, 2 * MLA_V_DIM), F32)],
        compiler_params=_cparams("parallel", "arbitrary"),
        name="mla_attn",
    )(q, k, v, pm)


def _s5_weights_body(bre_ref, bim_ref, cre_ref, cim_ref, spread_ref, mask_ref, wb_ref, wc_ref):
    def expand(m_ref):
        return _dot(m_ref[...].astype(BF16), spread_ref[...]) * mask_ref[...]

    wb_ref[:, 0:S5_NSTATE] = expand(bre_ref).astype(BF16)
    wb_ref[:, S5_NSTATE:] = expand(bim_ref).astype(BF16)
    wc_ref[0:S5_NSTATE, :] = jnp.transpose(expand(cre_ref)).astype(BF16)
    wc_ref[S5_NSTATE:, :] = jnp.transpose(-expand(cim_ref)).astype(BF16)


def _s5_weights(bbr, bbi, c_re, c_im):
    depth = bbr.shape[0]
    g_row = np.arange(S5_WIDTH) // S5_GROUP_CH
    g_col = np.arange(S5_NSTATE) // S5_STATE
    mask = (g_row[:, None] == g_col[None, :]).astype(np.float32)
    spread = (np.arange(S5_STATE)[:, None] == (np.arange(S5_NSTATE) % S5_STATE)[None, :]).astype(np.float32)
    small = pl.BlockSpec((None, S5_WIDTH, S5_STATE), lambda l: (l, 0, 0))
    return pl.pallas_call(
        _s5_weights_body,
        grid=(depth,),
        in_specs=[small, small, small, small,
                  _const_spec((S5_STATE, S5_NSTATE), 1), _const_spec((S5_WIDTH, S5_NSTATE), 1)],
        out_specs=[pl.BlockSpec((None, S5_WIDTH, 2 * S5_NSTATE), lambda l: (l, 0, 0)),
                   pl.BlockSpec((None, 2 * S5_NSTATE, S5_WIDTH), lambda l: (l, 0, 0))],
        out_shape=[jax.ShapeDtypeStruct((depth, S5_WIDTH, 2 * S5_NSTATE), BF16),
                   jax.ShapeDtypeStruct((depth, 2 * S5_NSTATE, S5_WIDTH), BF16)],
        compiler_params=_cparams("parallel"),
        name="s5_weights",
    )(bbr, bbi, c_re, c_im, jnp.asarray(spread, BF16), jnp.asarray(mask, F32))


def _s5_body(p0_ref, pn_ref, x0r_ref, x0i_ref, lre_ref, lim_ref, wb_ref, wc_ref, d_ref, wglu_ref, bglu_ref,
             o_ref, xr_out, xi_out, uz_bt, uz_a, uz_b, uz_c, bu_a, bu_b, bu_c, o_tb, xr_s, xi_s, *, lc, pitch):
    it = pl.program_id(1)
    nb = S5_BATCH_TILE
    ring = ((uz_a, bu_a), (uz_b, bu_b), (uz_c, bu_c))

    def stage_in(blk_ref, uz_tb, bu):
        for b in range(nb):
            for c in range(PS_COLS // LANES):
                uz_bt[c, b * pitch:b * pitch + lc, :] = blk_ref[b, :, c * LANES:(c + 1) * LANES].astype(F32)
        for t in range(lc):
            for c in range(PS_COLS // LANES):
                uz_tb[t * nb:(t + 1) * nb, c * LANES:(c + 1) * LANES] = uz_bt[c, pl.ds(t, nb, stride=pitch), :]
        bu[...] = _dot(uz_tb[:, 0:S5_WIDTH].astype(BF16), wb_ref[...])

    def stage_scan(bu):
        lre = jnp.broadcast_to(lre_ref[...], (nb, S5_NSTATE))
        lim = jnp.broadcast_to(lim_ref[...], (nb, S5_NSTATE))
        xr, xi = xr_s[...], xi_s[...]
        for t in range(lc):
            rs = slice(t * nb, (t + 1) * nb)
            xr, xi = (lre * xr - lim * xi + bu[rs, 0:S5_NSTATE],
                      lre * xi + lim * xr + bu[rs, S5_NSTATE:2 * S5_NSTATE])
            bu[rs, 0:S5_NSTATE] = xr
            bu[rs, S5_NSTATE:2 * S5_NSTATE] = xi
        xr_s[...] = xr
        xi_s[...] = xi

    def stage_out(uz_tb, xs):
        y = _dot(xs[...].astype(BF16), wc_ref[...]) + d_ref[...] * uz_tb[:, 0:S5_WIDTH]
        g5 = 0.5 * y * (1.0 + jnp.tanh(0.7978845608028654 * (y + 0.044715 * (y * y * y))))
        gate = _sigmoid(_dot(g5.astype(BF16), wglu_ref[...]) + bglu_ref[...])
        z = uz_tb[:, S5_WIDTH:2 * S5_WIDTH]
        o = g5 * gate * (z * _sigmoid(z))
        for c in range(S5_WIDTH // LANES):
            o_tb[c] = o[:, c * LANES:(c + 1) * LANES]
        for b in range(nb):
            for c in range(S5_WIDTH // LANES):
                o_ref[b, :, c * LANES:(c + 1) * LANES] = o_tb[c, pl.ds(b, lc, stride=nb), :].astype(BF16)

    @pl.when(it == 0)
    def _():
        xr_s[...] = x0r_ref[...]
        xi_s[...] = x0i_ref[...]
        stage_in(p0_ref, uz_a, bu_a)
        uz_c[...] = jnp.zeros(uz_c.shape, F32)
        bu_c[...] = jnp.zeros(bu_c.shape, F32)

    for r in range(3):
        @pl.when(it % 3 == r)
        def _(r=r):
            stage_in(pn_ref, *ring[(r + 1) % 3])
            stage_scan(ring[r][1])
            stage_out(*ring[(r + 2) % 3])

    @pl.when(it == pl.num_programs(1) - 2)
    def _():
        xr_out[...] = xr_s[...]
        xi_out[...] = xi_s[...]


def _s5(ps3, x0r, x0i, lre, lim, wb, wc, d, wglu, bglu, layer, bsz, t, lc):
    nb = S5_BATCH_TILE
    nt = t // lc
    pitch = lc + 8
    rows = lc * nb
    return pl.pallas_call(
        functools.partial(_s5_body, lc=lc, pitch=pitch),
        grid=(bsz // nb, nt + 1),
        in_specs=[pl.BlockSpec((nb, lc, PS_COLS), lambda g, i: (g, 0, 0)),
                  pl.BlockSpec((nb, lc, PS_COLS), lambda g, i: (g, jnp.minimum(i + 1, nt - 1), 0)),
                  pl.BlockSpec((nb, S5_NSTATE), lambda g, i: (g, 0)),
                  pl.BlockSpec((nb, S5_NSTATE), lambda g, i: (g, 0)),
                  _layer_spec((1, S5_NSTATE), layer), _layer_spec((1, S5_NSTATE), layer),
                  _layer_spec((S5_WIDTH, 2 * S5_NSTATE), layer), _layer_spec((2 * S5_NSTATE, S5_WIDTH), layer),
                  _layer_spec((1, S5_WIDTH), layer), _layer_spec((S5_WIDTH, S5_WIDTH), layer),
                  _layer_spec((1, S5_WIDTH), layer)],
        out_specs=[pl.BlockSpec((nb, lc, S5_WIDTH), lambda g, i: (g, jnp.maximum(i - 1, 0), 0)),
                   pl.BlockSpec((nb, S5_NSTATE), lambda g, i: (g, 0)),
                   pl.BlockSpec((nb, S5_NSTATE), lambda g, i: (g, 0))],
        out_shape=[jax.ShapeDtypeStruct((bsz, t, S5_WIDTH), BF16),
                   jax.ShapeDtypeStruct((bsz, S5_NSTATE), F32),
                   jax.ShapeDtypeStruct((bsz, S5_NSTATE), F32)],
        scratch_shapes=[pltpu.VMEM((PS_COLS // LANES, nb * pitch, LANES), F32)]
                       + [pltpu.VMEM((rows, PS_COLS), F32)] * 3
                       + [pltpu.VMEM((rows, 2 * S5_NSTATE), F32)] * 3
                       + [pltpu.VMEM((S5_WIDTH // LANES, rows, LANES), F32),
                          pltpu.VMEM((nb, S5_NSTATE), F32), pltpu.VMEM((nb, S5_NSTATE), F32)],
        compiler_params=_cparams("parallel", "arbitrary"),
        name="s5",
    )(ps3, ps3, x0r, x0i, lre, lim, wb, wc, d, wglu, bglu)


def _outproj_rows(x, og_ref, om_ref, os_ref, wo_scr):
    acc = _dot(og_ref[...], wo_scr[0:GLA_WIDTH, :])
    acc += _dot(om_ref[...], wo_scr[GLA_WIDTH:GLA_WIDTH + MLA_WIDTH, :])
    acc += _dot(os_ref[...], wo_scr[GLA_WIDTH + MLA_WIDTH:, :])
    return x + acc


def _outproj_body(x_ref, og_ref, om_ref, os_ref, w_ref, g_ref, o_ref, wo_scr, *, final):
    @pl.when(pl.program_id(0) == 0)
    def _():
        wo_scr[...] = w_ref[...].astype(BF16)

    xn = _outproj_rows(x_ref[...], og_ref, om_ref, os_ref, wo_scr)
    if final:
        ms = jnp.mean(xn * xn, axis=-1, keepdims=True)
        xn = xn * lax.rsqrt(ms + EPS) * g_ref[...]
    o_ref[...] = xn


def _out_in_body(x_ref, og_ref, om_ref, os_ref, wo_ref, g_ref, wt_ref, xo_ref, pg_ref, pm_ref, ps_ref,
                 wo_scr, w_scr):
    @pl.when(pl.program_id(0) == 0)
    def _():
        wo_scr[...] = wo_ref[...].astype(BF16)
        w_scr[...] = wt_ref[...].astype(BF16)

    xn = _outproj_rows(x_ref[...], og_ref, om_ref, os_ref, wo_scr)
    xo_ref[...] = xn
    _inproj_rows(xn, g_ref, w_scr, pg_ref, pm_ref, ps_ref)


def _out_in(x2, og, om, os_, w_out, ln, wt, layer, tm):
    n = x2.shape[0]
    row = lambda c: pl.BlockSpec((tm, c), lambda i: (i, 0))
    return pl.pallas_call(
        _out_in_body,
        grid=(n // tm,),
        in_specs=[row(D_MODEL), row(GLA_WIDTH), row(MLA_WIDTH), row(S5_WIDTH),
                  _layer_spec((D_MODEL, D_MODEL), layer, single_buffer=True),
                  _layer_spec((1, D_MODEL), layer + 1),
                  _layer_spec((IN_OFF['end'], D_MODEL), layer + 1, single_buffer=True)],
        out_specs=[row(D_MODEL), row(PG_COLS), row(PM_COLS), row(PS_COLS)],
        out_shape=[jax.ShapeDtypeStruct((n, D_MODEL), F32), jax.ShapeDtypeStruct((n, PG_COLS), BF16),
                   jax.ShapeDtypeStruct((n, PM_COLS), BF16), jax.ShapeDtypeStruct((n, PS_COLS), BF16)],
        scratch_shapes=[pltpu.VMEM((D_MODEL, D_MODEL), BF16), pltpu.VMEM((IN_OFF['end'], D_MODEL), BF16)],
        compiler_params=_cparams("arbitrary"),
        name="outproj_inproj",
    )(x2, og, om, os_, w_out, ln, wt)


def _outproj(x2, og, om, os_, w, gain, layer, tm, final):
    n = x2.shape[0]
    row = lambda c: pl.BlockSpec((tm, c), lambda i: (i, 0))
    return pl.pallas_call(
        functools.partial(_outproj_body, final=final),
        grid=(n // tm,),
        in_specs=[row(D_MODEL), row(GLA_WIDTH), row(MLA_WIDTH), row(S5_WIDTH),
                  _layer_spec((D_MODEL, D_MODEL), layer, single_buffer=True),
                  _const_spec((1, D_MODEL), 1)],
        out_specs=row(D_MODEL),
        out_shape=jax.ShapeDtypeStruct((n, D_MODEL), F32),
        scratch_shapes=[pltpu.VMEM((D_MODEL, D_MODEL), BF16)],
        compiler_params=_cparams("arbitrary"),
        name="outproj_final" if final else "outproj",
    )(x2, og, om, os_, w, gain)


def _prepare_params(ln_gain, w_in, gla_w_gate, gla_b_gate, gla_norm_gain, mla_q_norm_gain, mla_w_uq,
                    mla_kv_norm_gain, mla_w_ukv, s5_lambda_re, s5_lambda_im, s5_b_re, s5_b_im, s5_c_re, s5_c_im,
                    s5_d, s5_log_dt, s5_w_glu, s5_b_glu, w_out):
    depth = w_in.shape[0]
    w_t = jnp.swapaxes(w_in, 1, 2)
    wg = jnp.pad(gla_w_gate, ((0, 0), (0, GLA_QK - GLA_GATE_RANK), (0, 0))).astype(BF16)
    wq = mla_w_uq.reshape(depth, MLA_Q_RANK, MLA_HEADS, MLA_NOPE_DIM + MLA_ROPE_DIM)
    wq = jnp.pad(wq, ((0, 0), (0, 256 - MLA_Q_RANK), (0, 0), (0, MLA_QK_PAD - MLA_NOPE_DIM - MLA_ROPE_DIM)))
    wq = wq.reshape(depth, 256, MLA_QK_WIDTH).astype(BF16)
    gq = jnp.pad(mla_q_norm_gain, ((0, 0), (0, 256 - MLA_Q_RANK))).reshape(depth, 1, 256)
    wkv = mla_w_ukv.reshape(depth, MLA_KV_RANK, MLA_HEADS, MLA_NOPE_DIM + MLA_V_DIM)
    wk = jnp.pad(wkv[..., :MLA_NOPE_DIM], ((0, 0), (0, 0), (0, 0), (0, MLA_QK_PAD - MLA_NOPE_DIM)))
    wkv_r = jnp.concatenate([wk.reshape(depth, MLA_KV_RANK, MLA_QK_WIDTH),
                             wkv[..., MLA_NOPE_DIM:].reshape(depth, MLA_KV_RANK, MLA_WIDTH)], axis=2).astype(BF16)
    wka = jnp.pad(jnp.transpose(wkv[..., :MLA_NOPE_DIM], (0, 2, 3, 1)),
                  ((0, 0), (0, 0), (0, MLA_QK_PAD - MLA_NOPE_DIM), (0, 0))).astype(BF16)
    wv = jnp.transpose(wkv[..., MLA_NOPE_DIM:], (0, 2, 1, 3)).astype(BF16)
    dt = jnp.exp(s5_log_dt)[:, :, None]
    mag = jnp.exp(s5_lambda_re * dt)
    lbr, lbi = mag * jnp.cos(s5_lambda_im * dt), mag * jnp.sin(s5_lambda_im * dt)
    den = s5_lambda_re * s5_lambda_re + s5_lambda_im * s5_lambda_im
    qr = ((lbr - 1.0) * s5_lambda_re + lbi * s5_lambda_im) / den
    qi = (lbi * s5_lambda_re - (lbr - 1.0) * s5_lambda_im) / den
    b_re_t, b_im_t = jnp.swapaxes(s5_b_re, 2, 3), jnp.swapaxes(s5_b_im, 2, 3)
    bbr = qr[:, :, None, :] * b_re_t - qi[:, :, None, :] * b_im_t
    bbi = qr[:, :, None, :] * b_im_t + qi[:, :, None, :] * b_re_t
    rows = lambda m: m.reshape(depth, S5_WIDTH, S5_STATE)
    wb, wc = _s5_weights(rows(bbr), rows(bbi), rows(s5_c_re), rows(s5_c_im))
    return dict(
        ln=ln_gain.reshape(depth, 1, D_MODEL), w_in=w_t, wg=wg, bg=gla_b_gate.reshape(depth, 1, GLA_QK),
        gla_gain=jnp.tile(gla_norm_gain, (1, GLA_HEADS)).reshape(depth, 1, GLA_WIDTH),
        gq=gq, wq=wq, gkv=mla_kv_norm_gain.reshape(depth, 1, MLA_KV_RANK), wkv=wkv_r, wka=wka, wv=wv,
        lre=lbr.reshape(depth, 1, S5_NSTATE), lim=lbi.reshape(depth, 1, S5_NSTATE),
        wb=wb, wc=wc, d=s5_d.reshape(depth, 1, S5_WIDTH), wglu=s5_w_glu.astype(BF16),
        bglu=s5_b_glu.reshape(depth, 1, S5_WIDTH), w_out=w_out)


def _rope_tables(past, t, reps):
    half = MLA_ROPE_DIM // 2
    inv = ROPE_BASE ** (-np.arange(half, dtype=np.float64) / half)
    ang = (past + np.arange(t, dtype=np.float64))[:, None] * inv[None, :]
    cos, sin = np.cos(ang), np.sin(ang)
    pad = MLA_QK_PAD - MLA_NOPE_DIM - MLA_ROPE_DIM
    cos_t = np.concatenate([np.ones((t, MLA_NOPE_DIM)), cos, cos, np.zeros((t, pad))], axis=1)
    sin_t = np.concatenate([np.zeros((t, MLA_NOPE_DIM)), sin, sin, np.zeros((t, pad))], axis=1)
    return jnp.asarray(np.tile(cos_t, (reps, 1)), F32), jnp.asarray(np.tile(sin_t, (reps, 1)), F32)


def _trunk(x, gla_state, ckv_cache, kpe_cache, s5_re, s5_im, p, final_gain):
    bsz, t, _ = x.shape
    n = bsz * t
    depth = p['w_in'].shape[0]
    past = 0 if ckv_cache is None else ckv_cache.shape[2]
    s_len = past + t
    tl = _tiles(bsz, t, past)
    cos_t, sin_t = _rope_tables(past, t, max(1, tl['prep'] // t))
    kpet_cache = None if kpe_cache is None else jnp.swapaxes(kpe_cache, 2, 3)
    x2 = x.reshape(n, D_MODEL)
    gain_f = final_gain.reshape(1, D_MODEL)
    gla_o, ckv_o, kpe_o, re_o, im_o = [], [], [], [], []
    pg, pm, ps = _inproj(x2, p['ln'], p['w_in'], 0, tl['row'])
    for l in range(depth):
        s0 = jnp.zeros((bsz, GLA_HEADS, GLA_DK, GLA_DV), F32) if gla_state is None else gla_state[l]
        o_gla, s_new = _gla(pg.reshape(bsz, t, PG_COLS), p['wg'], p['bg'], p['gla_gain'], l, s0, bsz, t, tl)
        o_gla = o_gla.reshape(n, GLA_WIDTH)
        gla_o.append(s_new)
        prep = functools.partial(_mla_prep, pm, cos_t, sin_t, p['gq'], p['wq'], p['gkv'], p['wkv'], l, bsz, t,
                                 tl['prep'])
        if past == 0:
            q, ckv_new, kpet, k_cat, v_all = prep(with_kv=True)
            kpe_o.append(jnp.swapaxes(kpet, 1, 2))
            o_mla = _attn(q, k_cat, v_all, pm, bsz, t, s_len, past, tl['attn_q'], tl['attn_k'])
        else:
            q, ckv_new, kpe128 = prep(with_kv=False)
            kpe_o.append(kpe128[:, MLA_NOPE_DIM:MLA_NOPE_DIM + MLA_ROPE_DIM].reshape(bsz, t, MLA_ROPE_DIM))
            o_mla = _attn_cached(q, pm, ckv_new, kpe128, ckv_cache, kpet_cache, p['wka'], p['wv'], l, bsz, t,
                                 tl['cached_seqs'])
        ckv_o.append(ckv_new.reshape(bsz, t, MLA_KV_RANK))
        x0r = jnp.zeros((bsz, S5_NSTATE), F32) if s5_re is None else s5_re[l].reshape(bsz, S5_NSTATE)
        x0i = jnp.zeros((bsz, S5_NSTATE), F32) if s5_im is None else s5_im[l].reshape(bsz, S5_NSTATE)
        o_s5, xr, xi = _s5(ps.reshape(bsz, t, PS_COLS), x0r, x0i, p['lre'], p['lim'], p['wb'], p['wc'],
                           p['d'], p['wglu'], p['bglu'], l, bsz, t, tl['s5_rows'])
        re_o.append(xr.reshape(bsz, S5_GROUPS, S5_STATE))
        im_o.append(xi.reshape(bsz, S5_GROUPS, S5_STATE))
        o_s5 = o_s5.reshape(n, S5_WIDTH)
        if l < depth - 1:
            x2, pg, pm, ps = _out_in(x2, o_gla, o_mla, o_s5, p['w_out'], p['ln'], p['w_in'], l, tl['row'])
        else:
            x2 = _outproj(x2, o_gla, o_mla, o_s5, p['w_out'], gain_f, l, tl['out_row'], final=True)
    return (x2.reshape(bsz, t, D_MODEL), jnp.stack(gla_o), jnp.stack(ckv_o), jnp.stack(kpe_o),
            jnp.stack(re_o), jnp.stack(im_o))


def kernel(x_prompt, x_sample, state_gla, cache_mla_ckv, cache_mla_kpe, state_s5_re, state_s5_im, ln_gain, w_in, gla_w_gate, gla_b_gate, gla_norm_gain, mla_q_norm_gain, mla_w_uq, mla_kv_norm_gain, mla_w_ukv, s5_lambda_re, s5_lambda_im, s5_b_re, s5_b_im, s5_c_re, s5_c_im, s5_d, s5_log_dt, s5_w_glu, s5_b_glu, w_out, final_gain):
    p = _prepare_params(ln_gain, w_in, gla_w_gate, gla_b_gate, gla_norm_gain, mla_q_norm_gain, mla_w_uq,
                        mla_kv_norm_gain, mla_w_ukv, s5_lambda_re, s5_lambda_im, s5_b_re, s5_b_im,
                        s5_c_re, s5_c_im, s5_d, s5_log_dt, s5_w_glu, s5_b_glu, w_out)
    y_p, gla_p, ckv_p, kpe_p, re_p, im_p = _trunk(x_prompt, None, None, None, None, None, p, final_gain)
    y_s, gla_s, ckv_s, kpe_s, re_s, im_s = _trunk(x_sample, state_gla, cache_mla_ckv, cache_mla_kpe,
                                                  state_s5_re, state_s5_im, p, final_gain)
    return (y_p, y_s, gla_p, ckv_p, kpe_p, re_p, im_p, gla_s, ckv_s, kpe_s, re_s, im_s)
```

```python
import functools
import math

import numpy as np
import jax
import jax.numpy as jnp
from jax import lax
from jax.experimental import pallas as pl
from jax.experimental.pallas import tpu as pltpu

F32 = jnp.float32
BF16 = jnp.bfloat16

LANES = 128
D_MODEL = 1024
CHUNK = 64
EPS = 1e-6
GLA_HEADS = 4
GLA_DV = 64
GLA_DK = 32
GLA_WIDTH = GLA_HEADS * GLA_DV
GLA_QK = GLA_HEADS * GLA_DK
GLA_GATE_RANK = 16
GLA_GATE_TAU = 16.0
GLA_SUB = 16
GLA_PLAIN_MAX_DECAY = 60.0
MLA_HEADS = 4
MLA_NOPE_DIM = 64
MLA_ROPE_DIM = 32
MLA_V_DIM = 128
MLA_Q_RANK = 192
MLA_KV_RANK = 128
MLA_WIDTH = MLA_HEADS * MLA_V_DIM
MLA_QK_PAD = 128
MLA_QK_WIDTH = MLA_HEADS * MLA_QK_PAD
ROPE_BASE = 10000.0
S5_GROUPS = 16
S5_GROUP_CH = 16
S5_STATE = 64
S5_WIDTH = S5_GROUPS * S5_GROUP_CH
S5_NSTATE = S5_GROUPS * S5_STATE
S5_BATCH_TILE = 8
S5_SCAN_SLICES = 4

PG_COLS = 896
PM_COLS = 1024
PS_COLS = 512
_IN_SEGS = (('g_q', GLA_QK), ('g_k', GLA_QK), ('g_v', GLA_WIDTH), ('g_lr', GLA_GATE_RANK), ('g_z', GLA_WIDTH),
            ('m_cq', MLA_Q_RANK), ('m_ckv', MLA_KV_RANK), ('m_kr', MLA_ROPE_DIM), ('m_z', MLA_WIDTH),
            ('s_u', S5_WIDTH), ('s_z', S5_WIDTH), ('end', 0))
IN_OFF = dict(zip([n for n, _ in _IN_SEGS], np.cumsum([0] + [w for _, w in _IN_SEGS[:-1]]).tolist()))

VMEM_LIMIT_BYTES = 48 * 1024 * 1024


def _tiles(bsz, t, past):
    n = bsz * t
    s_len = past + t
    return dict(
        row=min(512, n),
        out_row=min(1024, n),
        prep=min(512, n) if t < 512 else 512,
        cached_seqs=math.gcd(bsz, max(1, 64 // t)),
        gla_rows=min(512, t), gla_chunk=min(CHUNK, t),
        gla_seqs=math.gcd(bsz, max(2, 128 // t)),
        attn_q=min(512, t), attn_k=min(512, s_len),
        s5_rows=min(64, t))


def _cparams(*sem):
    return pltpu.CompilerParams(dimension_semantics=sem, vmem_limit_bytes=VMEM_LIMIT_BYTES)


def _sigmoid(x):
    return 0.5 * (1.0 + jnp.tanh(0.5 * x))


def _dot(a, b):
    return jnp.dot(a, b, preferred_element_type=F32)


def _dot_t(a, b):
    return lax.dot_general(a, b, (((0,), (0,)), ((), ())), preferred_element_type=F32)


def _const_spec(shape, ngrid):
    zeros = (0,) * len(shape)
    return pl.BlockSpec(shape, lambda *_: zeros)


def _layer_spec(shape, layer, single_buffer=False):
    zeros = (0,) * len(shape)
    mode = dict(pipeline_mode=pl.Buffered(1)) if single_buffer else {}
    return pl.BlockSpec((None,) + tuple(shape), lambda *_: (layer,) + zeros, **mode)


def _inproj_rows(x, g_ref, w_scr, og_ref, om_ref, os_ref):
    ms = jnp.mean(x * x, axis=-1, keepdims=True)
    h = (x * lax.rsqrt(ms + EPS) * g_ref[...]).astype(BF16)

    def seg(a, b):
        return lax.dot_general(h, w_scr[a:b, :], (((1,), (1,)), ((), ())), preferred_element_type=F32)

    lane = lax.broadcasted_iota(jnp.int32, (x.shape[0], LANES), 1)
    lane2 = lax.broadcasted_iota(jnp.int32, (x.shape[0], 2 * LANES), 1)
    c = IN_OFF
    og_ref[:, 0:512] = seg(c['g_q'], c['g_lr']).astype(BF16)
    og_ref[:, 512:768] = seg(c['g_z'], c['m_cq']).astype(BF16)
    og_ref[:, 768:896] = jnp.where(lane < GLA_GATE_RANK, seg(c['g_lr'], c['g_lr'] + LANES), 0.0).astype(BF16)
    om_ref[:, 0:256] = jnp.where(lane2 < MLA_Q_RANK, seg(c['m_cq'], c['m_cq'] + 2 * LANES), 0.0).astype(BF16)
    ckv_kr = seg(c['m_ckv'], c['m_ckv'] + 2 * LANES)
    om_ref[:, 256:384] = ckv_kr[:, 0:LANES].astype(BF16)
    kr = pltpu.roll(ckv_kr[:, LANES:], MLA_NOPE_DIM, 1)
    om_ref[:, 384:512] = jnp.where((lane >= MLA_NOPE_DIM) & (lane < MLA_NOPE_DIM + MLA_ROPE_DIM), kr, 0.0).astype(BF16)
    om_ref[:, 512:1024] = seg(c['m_z'], c['s_u']).astype(BF16)
    os_ref[...] = seg(c['s_u'], c['end']).astype(BF16)


def _inproj_body(x_ref, g_ref, wt_ref, og_ref, om_ref, os_ref, w_scr):
    @pl.when(pl.program_id(0) == 0)
    def _():
        w_scr[...] = wt_ref[...].astype(BF16)

    _inproj_rows(x_ref[...], g_ref, w_scr, og_ref, om_ref, os_ref)


def _inproj(x2, gain, wt, layer, tm):
    n = x2.shape[0]
    return pl.pallas_call(
        _inproj_body,
        grid=(n // tm,),
        in_specs=[pl.BlockSpec((tm, D_MODEL), lambda i: (i, 0)),
                  _layer_spec((1, D_MODEL), layer),
                  _layer_spec((IN_OFF['end'], D_MODEL), layer, single_buffer=True)],
        scratch_shapes=[pltpu.VMEM((IN_OFF['end'], D_MODEL), BF16)],
        out_specs=[pl.BlockSpec((tm, PG_COLS), lambda i: (i, 0)),
                   pl.BlockSpec((tm, PM_COLS), lambda i: (i, 0)),
                   pl.BlockSpec((tm, PS_COLS), lambda i: (i, 0))],
        out_shape=[jax.ShapeDtypeStruct((n, PG_COLS), BF16),
                   jax.ShapeDtypeStruct((n, PM_COLS), BF16),
                   jax.ShapeDtypeStruct((n, PS_COLS), BF16)],
        compiler_params=_cparams("arbitrary"),
        name="inproj",
    )(x2, gain, wt)


def _gla_body(p_ref, wg_ref, bg_ref, gain_ref, ones_k_ref, ones_v_ref, bd_ref, s0_ref,
              o_ref, sout_ref, q_scr, k_scr, v_scr, b_scr, s_scr, qe_scr, qs_scr, ke_scr, vb_scr, o_scr,
              *, chunk, nch, nseq):
    it = pl.program_id(1)
    sub = min(GLA_SUB, chunk)
    nsub = chunk // sub

    @pl.when(it == 0)
    def _():
        s_scr[...] = jnp.zeros(s_scr.shape, F32)
        for sq in range(nseq):
            for h in range(GLA_HEADS):
                s_scr[sq, h * GLA_DK:(h + 1) * GLA_DK, h * GLA_DV:(h + 1) * GLA_DV] = s0_ref[sq, h]

    def cols(a, b):
        parts = [p_ref[sq, :, a:b] for sq in range(nseq)]
        return parts[0] if nseq == 1 else jnp.concatenate(parts, axis=0)

    q_scr[...] = cols(0, 128).astype(F32) * (GLA_DK ** -0.5)
    k_scr[...] = cols(128, 256).astype(F32)
    v_scr[...] = cols(256, 512).astype(F32)
    logit = _dot(cols(768, 896), wg_ref[...]) + bg_ref[...]
    log_a = (jnp.minimum(logit, 0.0) - jnp.log(1.0 + jnp.exp(-jnp.abs(logit)))) * (1.0 / GLA_GATE_TAU)
    row_in_chunk = lax.broadcasted_iota(jnp.int32, log_a.shape, 0) % chunk
    b_all = log_a
    shift = 1
    while shift < chunk:
        b_all = b_all + jnp.where(row_in_chunk >= shift, pltpu.roll(b_all, shift, 0), 0.0)
        shift *= 2
    b_scr[...] = b_all

    row = lax.broadcasted_iota(jnp.int32, (chunk, GLA_QK), 0)
    row_in_sub = lax.broadcasted_iota(jnp.int32, (sub, GLA_QK), 0)

    def decay_columns(b_end):
        col = jnp.transpose(jnp.broadcast_to(jnp.exp(b_end), (GLA_QK, GLA_QK)))
        return jnp.concatenate([col, col], axis=1)

    def load_chunk(sq, c):
        r0 = pl.multiple_of((sq * nch + c) * chunk, chunk)
        return (r0, q_scr[pl.ds(r0, chunk), :], k_scr[pl.ds(r0, chunk), :], v_scr[pl.ds(r0, chunk), :],
                b_scr[pl.ds(r0, chunk), :], b_scr[pl.ds(r0 + chunk - 1, 1), :], s_scr[sq])

    def finish_chunk(sq, r0, o, s_prev, a_state, b_end):
        ms = _dot((o * o).astype(BF16), ones_v_ref[...]) * (1.0 / GLA_DV)
        o_n = o * lax.rsqrt(ms + EPS) * gain_ref[...]
        t0 = r0 - sq * nch * chunk
        z = p_ref[sq, pl.ds(t0, chunk), 512:768].astype(F32)
        o_ref[sq, pl.ds(t0, chunk), :] = (o_n * (z * _sigmoid(z))).astype(BF16)
        s_scr[sq] = s_prev * decay_columns(b_end) + a_state

    def robust_chunk(c, carry, sq):
        r0, qc, kc, vc, bc, b_end, s_prev = load_chunk(sq, c)
        xs = [qc * jnp.exp(bc)]
        ks = []
        for sj in range(nsub - 1):
            e_j = b_scr[pl.ds(r0 + (sj + 1) * sub - 1, 1), :]
            later = row >= (sj + 1) * sub
            xs.append(jnp.where(later, qc * jnp.exp(jnp.where(later, bc - e_j, 0.0)), 0.0))
            own = (row >= sj * sub) & (row < (sj + 1) * sub)
            ks.append(jnp.where(own, kc * jnp.exp(jnp.where(own, e_j - bc, 0.0)), 0.0))
        ks.append(kc * jnp.exp(b_end - bc))
        k_all = jnp.concatenate(ks, axis=1).astype(BF16)
        a_all = _dot_t(k_all, vc.astype(BF16)) * bd_ref[...]
        w = jnp.concatenate([s_prev, a_all[:(nsub - 1) * GLA_QK]], axis=0).astype(BF16) if nsub > 1 \
            else s_prev.astype(BF16)
        o_off = _dot(jnp.concatenate(xs, axis=1).astype(BF16), w)
        rows = []
        for si in range(nsub):
            q_i = qc[si * sub:(si + 1) * sub]
            b_i = bc[si * sub:(si + 1) * sub]
            es = []
            for j in range(sub):
                r = r0 + si * sub + j
                b_j = b_scr[pl.ds(r, 1), :]
                k_j = k_scr[pl.ds(r, 1), :]
                valid = row_in_sub >= j
                es.append(q_i * k_j * jnp.exp(jnp.where(valid, b_i - b_j, -jnp.inf)))
            e_all = jnp.concatenate(es, axis=0)
            e_hi = e_all.astype(BF16)
            e_lo = (e_all - e_hi.astype(F32)).astype(BF16)
            p_all = _dot(e_hi, ones_k_ref[...]) + _dot(e_lo, ones_k_ref[...])
            acc = o_off[si * sub:(si + 1) * sub]
            for j in range(sub):
                v_j = v_scr[pl.ds(r0 + si * sub + j, 1), :]
                acc = acc + p_all[j * sub:(j + 1) * sub] * v_j
            rows.append(acc)
        o = rows[0] if nsub == 1 else jnp.concatenate(rows, axis=0)
        finish_chunk(sq, r0, o, s_prev, a_all[(nsub - 1) * GLA_QK:], b_end)
        return carry

    lane_head_v = lax.broadcasted_iota(jnp.int32, (chunk, GLA_WIDTH), 1) // GLA_DV
    causal = (lax.broadcasted_iota(jnp.int32, (GLA_HEADS * chunk, chunk), 0) % chunk
              >= lax.broadcasted_iota(jnp.int32, (GLA_HEADS * chunk, chunk), 1))

    def plain_block():
        tt = nseq * nch * chunk
        b_all = b_scr[...]
        q_all = q_scr[...]
        k_all = k_scr[...]
        qe = q_all * jnp.exp(b_all)
        lane_head = lax.broadcasted_iota(jnp.int32, (tt, GLA_QK), 1) // GLA_DK
        qe_scr[...] = qe.astype(BF16)
        for h in range(GLA_HEADS):
            qs_scr[h] = jnp.where(lane_head == h, qe, 0.0).astype(BF16)
        ke_scr[...] = (k_all * jnp.exp(-b_all)).astype(BF16)
        vb_scr[...] = cols(256, 512)
        states = [s_scr[sq] for sq in range(nseq)]
        for c, sq in [(c, sq) for c in range(nch) for sq in range(nseq)]:
            r0 = (sq * nch + c) * chunk
            rs = slice(r0, r0 + chunk)
            s_cur = states[sq]
            qs = jnp.concatenate([qs_scr[h, rs, :] for h in range(GLA_HEADS)], axis=0)
            s = lax.dot_general(qs, ke_scr[rs, :], (((1,), (1,)), ((), ())), preferred_element_type=F32)
            s = jnp.where(causal, s, 0.0).astype(BF16)
            r = _dot(s, vb_scr[rs, :])
            o = _dot(qe_scr[rs, :], s_cur.astype(BF16))
            for h in range(GLA_HEADS):
                o = o + jnp.where(lane_head_v == h, r[h * chunk:(h + 1) * chunk], 0.0)
            o_scr[rs, :] = o
            b_end = b_scr[r0 + chunk - 1:r0 + chunk, :]
            k_end = (k_scr[rs, :] * jnp.exp(b_end - b_scr[rs, :])).astype(BF16)
            a_state = _dot_t(k_end, vb_scr[rs, :]) * bd_ref[0:GLA_QK, :]
            states[sq] = s_cur * decay_columns(b_end) + a_state
        for sq in range(nseq):
            s_scr[sq] = states[sq]
        o = o_scr[...]
        ms = _dot((o * o).astype(BF16), ones_v_ref[...]) * (1.0 / GLA_DV)
        o_n = o * lax.rsqrt(ms + EPS) * gain_ref[...]
        z = cols(512, 768).astype(F32)
        o_all = (o_n * (z * _sigmoid(z))).astype(BF16)
        for sq in range(nseq):
            o_ref[sq] = o_all[sq * nch * chunk:(sq + 1) * nch * chunk]

    in_range = jnp.max(-b_scr[...]) < GLA_PLAIN_MAX_DECAY

    @pl.when(in_range)
    def _():
        plain_block()

    @pl.when(jnp.logical_not(in_range))
    def _():
        for sq in range(nseq):
            lax.fori_loop(0, nch, functools.partial(robust_chunk, sq=sq), 0)

    @pl.when(it == pl.num_programs(1) - 1)
    def _():
        for sq in range(nseq):
            for h in range(GLA_HEADS):
                sout_ref[sq, h] = s_scr[sq, h * GLA_DK:(h + 1) * GLA_DK, h * GLA_DV:(h + 1) * GLA_DV]


def _gla_consts(chunk):
    nsub = chunk // min(GLA_SUB, chunk)
    hk = np.arange(GLA_QK) // GLA_DK
    hv = np.arange(GLA_WIDTH) // GLA_DV
    same_kv = (hk[:, None] == hv[None, :]).astype(np.float32)
    same_vv = (hv[:, None] == hv[None, :]).astype(np.float32)
    return (jnp.asarray(same_kv, BF16), jnp.asarray(same_vv, BF16),
            jnp.asarray(np.tile(same_kv, (nsub, 1)), F32))


def _gla(pg, wg, bg, gain, layer, s0_bd, bsz, t, tl):
    chunk, tt, nseq = tl['gla_chunk'], tl['gla_rows'], tl['gla_seqs']
    nch = tt // chunk
    nt = t // tt
    rows = nseq * tt
    nsub = chunk // min(GLA_SUB, chunk)
    ones_k, ones_v, bd = _gla_consts(chunk)
    const = lambda shape: _const_spec(shape, 2)
    return pl.pallas_call(
        functools.partial(_gla_body, chunk=chunk, nch=nch, nseq=nseq),
        grid=(bsz // nseq, nt),
        in_specs=[pl.BlockSpec((nseq, tt, PG_COLS), lambda b, i: (b, i, 0)),
                  _layer_spec((GLA_QK, GLA_QK), layer), _layer_spec((1, GLA_QK), layer),
                  _layer_spec((1, GLA_WIDTH), layer),
                  const((GLA_QK, GLA_WIDTH)), const((GLA_WIDTH, GLA_WIDTH)),
                  const((nsub * GLA_QK, GLA_WIDTH)),
                  pl.BlockSpec((nseq, GLA_HEADS, GLA_DK, GLA_DV), lambda b, i: (b, 0, 0, 0))],
        out_specs=[pl.BlockSpec((nseq, tt, GLA_WIDTH), lambda b, i: (b, i, 0)),
                   pl.BlockSpec((nseq, GLA_HEADS, GLA_DK, GLA_DV), lambda b, i: (b, 0, 0, 0))],
        out_shape=[jax.ShapeDtypeStruct((bsz, t, GLA_WIDTH), BF16),
                   jax.ShapeDtypeStruct((bsz, GLA_HEADS, GLA_DK, GLA_DV), F32)],
        scratch_shapes=[pltpu.VMEM((rows, GLA_QK), F32), pltpu.VMEM((rows, GLA_QK), F32),
                        pltpu.VMEM((rows, GLA_WIDTH), F32), pltpu.VMEM((rows, GLA_QK), F32),
                        pltpu.VMEM((nseq, GLA_QK, GLA_WIDTH), F32),
                        pltpu.VMEM((rows, GLA_QK), BF16), pltpu.VMEM((GLA_HEADS, rows, GLA_QK), BF16),
                        pltpu.VMEM((rows, GLA_QK), BF16),
                        pltpu.VMEM((rows, GLA_WIDTH), BF16), pltpu.VMEM((rows, GLA_WIDTH), F32)],
        compiler_params=_cparams("parallel", "arbitrary"),
        name="gla",
    )(pg, wg, bg, gain, ones_k, ones_v, bd, s0_bd)


def _rope128(x, cos_t, sin_t):
    lane = lax.broadcasted_iota(jnp.int32, x.shape, 1)
    first_half = (lane >= MLA_NOPE_DIM) & (lane < MLA_NOPE_DIM + MLA_ROPE_DIM // 2)
    rot = jnp.where(first_half, -pltpu.roll(x, LANES - MLA_ROPE_DIM // 2, 1), pltpu.roll(x, MLA_ROPE_DIM // 2, 1))
    return x * cos_t + rot * sin_t


def _mla_prep_body(p_ref, cos_ref, sin_ref, gq_ref, wq_ref, gkv_ref, *rest, with_kv):
    if with_kv:
        wkv_ref, q_ref, ckv_ref, kpet_ref, k_ref, v_ref = rest
    else:
        q_ref, ckv_ref, kpe_ref = rest
    cos_t = cos_ref[...]
    sin_t = sin_ref[...]
    cq = p_ref[:, 0:256].astype(F32)
    ms = jnp.sum(cq * cq, axis=-1, keepdims=True) * (1.0 / MLA_Q_RANK)
    cqn = (cq * lax.rsqrt(ms + EPS) * gq_ref[...]).astype(BF16)
    qh = _dot(cqn, wq_ref[...])
    scale = (MLA_NOPE_DIM + MLA_ROPE_DIM) ** -0.5 * math.log2(math.e)
    for h in range(MLA_HEADS):
        x = qh[:, h * MLA_QK_PAD:(h + 1) * MLA_QK_PAD]
        q_ref[:, h * MLA_QK_PAD:(h + 1) * MLA_QK_PAD] = (_rope128(x, cos_t, sin_t) * scale).astype(BF16)
    ckv = p_ref[:, 256:384].astype(F32)
    ms = jnp.mean(ckv * ckv, axis=-1, keepdims=True)
    ckv_n = ckv * lax.rsqrt(ms + EPS) * gkv_ref[...]
    ckv_ref[...] = ckv_n
    kpe128 = _rope128(p_ref[:, 384:512].astype(F32), cos_t, sin_t)
    if with_kv:
        kpet_ref[...] = jnp.transpose(kpe128)[MLA_NOPE_DIM:MLA_NOPE_DIM + MLA_ROPE_DIM, :]
        kv = _dot(ckv_n.astype(BF16), wkv_ref[...])
        for h in range(MLA_HEADS):
            sl = slice(h * MLA_QK_PAD, (h + 1) * MLA_QK_PAD)
            k_ref[:, sl] = (kv[:, sl] + kpe128).astype(BF16)
        v_ref[...] = kv[:, MLA_QK_WIDTH:].astype(BF16)
    else:
        kpe_ref[...] = kpe128


def _mla_prep(pm, cos_t, sin_t, gq, wq, gkv, wkv, layer, bsz, t, tm, with_kv):
    n = bsz * t
    ntab = max(1, t // tm)
    row = lambda c: pl.BlockSpec((tm, c), lambda i: (i, 0))
    in_specs = [row(512),
                pl.BlockSpec((tm, LANES), lambda i: (i % ntab, 0)),
                pl.BlockSpec((tm, LANES), lambda i: (i % ntab, 0)),
                _layer_spec((1, 256), layer), _layer_spec((256, MLA_QK_WIDTH), layer),
                _layer_spec((1, MLA_KV_RANK), layer)]
    out_specs = [row(MLA_QK_WIDTH), row(MLA_KV_RANK)]
    out_shape = [jax.ShapeDtypeStruct((n, MLA_QK_WIDTH), BF16), jax.ShapeDtypeStruct((n, MLA_KV_RANK), F32)]
    args = [pm, cos_t, sin_t, gq, wq, gkv]
    if with_kv:
        in_specs.append(_layer_spec((MLA_KV_RANK, MLA_QK_WIDTH + MLA_WIDTH), layer))
        args.append(wkv)
        out_specs += [pl.BlockSpec((None, MLA_ROPE_DIM, tm), lambda i: (i // ntab, 0, i % ntab)),
                      row(MLA_QK_WIDTH), row(MLA_WIDTH)]
        out_shape += [jax.ShapeDtypeStruct((bsz, MLA_ROPE_DIM, t), F32),
                      jax.ShapeDtypeStruct((n, MLA_QK_WIDTH), BF16), jax.ShapeDtypeStruct((n, MLA_WIDTH), BF16)]
    else:
        out_specs.append(row(LANES))
        out_shape.append(jax.ShapeDtypeStruct((n, LANES), F32))
    return pl.pallas_call(
        functools.partial(_mla_prep_body, with_kv=with_kv),
        grid=(n // tm,),
        in_specs=in_specs, out_specs=out_specs, out_shape=out_shape,
        compiler_params=_cparams("parallel"),
        name="mla_prep_kv" if with_kv else "mla_prep",
    )(*args)


def _attn_cached_body(q_ref, z_ref, ckv_new_ref, kpe_new_ref, ckv_past_ref, kpet_past_ref, wka_ref, wv_ref,
                      o_ref, kpet_scr, *, past, t, nseq):
    last = (((1,), (1,)), ((), ()))
    hsl = [slice(h * MLA_QK_PAD, (h + 1) * MLA_QK_PAD) for h in range(MLA_HEADS)]
    q_lat_h = [_dot(q_ref[:, hsl[h]], wka_ref[h]).astype(BF16) for h in range(MLA_HEADS)]
    kpet_scr[...] = jnp.zeros(kpet_scr.shape, BF16)
    o_lat = []
    for sq in range(nseq):
        rs = slice(sq * t, (sq + 1) * t)
        q_rows = jnp.concatenate([q_ref[rs, hsl[h]] for h in range(MLA_HEADS)], axis=0)
        q_lat = jnp.concatenate([q_lat_h[h][rs] for h in range(MLA_HEADS)], axis=0)
        c_past = ckv_past_ref[sq].astype(BF16)
        c_new = ckv_new_ref[rs, :].astype(BF16)
        kpet_scr[sq, MLA_NOPE_DIM:MLA_NOPE_DIM + MLA_ROPE_DIM, :] = kpet_past_ref[sq].astype(BF16)
        s_past = (lax.dot_general(q_lat, c_past, last, preferred_element_type=F32)
                  + _dot(q_rows, kpet_scr[sq]))
        s_new = (lax.dot_general(q_lat, c_new, last, preferred_element_type=F32)
                 + lax.dot_general(q_rows, kpe_new_ref[rs, :].astype(BF16), last, preferred_element_type=F32))
        if past // CHUNK != (past + t - 1) // CHUNK:
            q_chunk = (past + lax.broadcasted_iota(jnp.int32, s_past.shape, 0) % t) // CHUNK
            s_past = jnp.where(lax.broadcasted_iota(jnp.int32, s_past.shape, 1) // CHUNK <= q_chunk, s_past, -jnp.inf)
            q_chunk = (past + lax.broadcasted_iota(jnp.int32, s_new.shape, 0) % t) // CHUNK
            s_new = jnp.where((past + lax.broadcasted_iota(jnp.int32, s_new.shape, 1)) // CHUNK <= q_chunk,
                              s_new, -jnp.inf)
        m = jnp.maximum(jnp.max(s_past, axis=-1, keepdims=True), jnp.max(s_new, axis=-1, keepdims=True))
        p_past = jnp.exp2(s_past - m)
        p_new = jnp.exp2(s_new - m)
        l = jnp.sum(p_past, axis=-1, keepdims=True) + jnp.sum(p_new, axis=-1, keepdims=True)
        o_lat.append(((_dot(p_past.astype(BF16), c_past) + _dot(p_new.astype(BF16), c_new)) / l).astype(BF16))
    for h in range(MLA_HEADS):
        vsl = slice(h * MLA_V_DIM, (h + 1) * MLA_V_DIM)
        o_h = jnp.concatenate([o_lat[sq][h * t:(h + 1) * t] for sq in range(nseq)], axis=0)
        z = z_ref[:, vsl].astype(F32)
        o_ref[:, vsl] = (_dot(o_h, wv_ref[h]) * (z * _sigmoid(z))).astype(BF16)


def _attn_cached(q, pm, ckv_new, kpe128, ckv_past, kpet_past, wka, wv, layer, bsz, t, nseq):
    past = ckv_past.shape[2]
    rows = nseq * t
    return pl.pallas_call(
        functools.partial(_attn_cached_body, past=past, t=t, nseq=nseq),
        grid=(bsz // nseq,),
        in_specs=[pl.BlockSpec((rows, MLA_QK_WIDTH), lambda b: (b, 0)),
                  pl.BlockSpec((rows, MLA_WIDTH), lambda b: (b, 1)),
                  pl.BlockSpec((rows, MLA_KV_RANK), lambda b: (b, 0)),
                  pl.BlockSpec((rows, LANES), lambda b: (b, 0)),
                  pl.BlockSpec((None, nseq, past, MLA_KV_RANK), lambda b: (layer, b, 0, 0)),
                  pl.BlockSpec((None, nseq, MLA_ROPE_DIM, past), lambda b: (layer, b, 0, 0)),
                  _layer_spec((MLA_HEADS, MLA_QK_PAD, MLA_KV_RANK), layer),
                  _layer_spec((MLA_HEADS, MLA_KV_RANK, MLA_V_DIM), layer)],
        out_specs=pl.BlockSpec((rows, MLA_WIDTH), lambda b: (b, 0)),
        out_shape=jax.ShapeDtypeStruct((bsz * t, MLA_WIDTH), BF16),
        scratch_shapes=[pltpu.VMEM((nseq, MLA_QK_PAD, past), BF16)],
        compiler_params=_cparams("parallel"),
        name="mla_attn_cached",
    )(q, pm, ckv_new, kpe128, ckv_past, kpet_past, wka, wv)


def _attn_body(q_ref, k_ref, v_ref, z_ref, o_ref, m_scr, acc_scr, *, past, tq, tk, s_len):
    iq = pl.program_id(1)
    q_first = past + iq * tq
    full_keys = jnp.minimum((q_first // CHUNK + 1) * CHUNK, s_len)
    vis_keys = jnp.minimum(((q_first + tq - 1) // CHUNK + 1) * CHUNK, s_len)
    n_full = full_keys // tk
    n_vis = (vis_keys + tk - 1) // tk
    m_scr[...] = jnp.full(m_scr.shape, -jnp.inf, F32)
    acc_scr[...] = jnp.zeros(acc_scr.shape, F32)
    split_diagonal = past % tk == 0 and tq == tk and (tq // 2) % CHUNK == 0

    def block(kb, carry, masked):
        k0 = pl.multiple_of(kb * tk, tk)
        if masked and split_diagonal:
            parts = [(r * (tq // 2), tq // 2, (r + 1) * (tk // 2)) for r in range(2)]
        else:
            parts = [(0, tq, tk)]
        for r0, nr, kext in parts:
            rows = slice(r0, r0 + nr)
            if masked:
                q_pos = (r0 if split_diagonal else q_first + r0) + lax.broadcasted_iota(jnp.int32, (nr, kext), 0)
                k_pos = (0 if split_diagonal else k0) + lax.broadcasted_iota(jnp.int32, (nr, kext), 1)
                visible = k_pos // CHUNK <= q_pos // CHUNK
            ntile, rem = kext // LANES, kext % LANES
            ones_v = jnp.ones((kext, MLA_V_DIM), BF16)
            for h in range(MLA_HEADS):
                sl = slice(h * MLA_QK_PAD, (h + 1) * MLA_QK_PAD)
                vsl = slice(h * MLA_V_DIM, (h + 1) * MLA_V_DIM)
                s = lax.dot_general(q_ref[rows, sl], k_ref[pl.ds(k0, kext), sl], (((1,), (1,)), ((), ())),
                                    preferred_element_type=F32)
                if masked:
                    s = jnp.where(visible, s, -jnp.inf)
                m_prev = m_scr[h, rows]
                m_new = jnp.maximum(m_prev, jnp.max(s, axis=-1, keepdims=True))
                alpha = jnp.exp2(m_prev - m_new)
                ps = [jnp.exp2(s[:, c * LANES:(c + 1) * LANES] - m_new) for c in range(ntile)]
                if rem:
                    ps.append(jnp.exp2(s[:, ntile * LANES:] - m_new[:, :rem]))
                p = jnp.concatenate(ps, axis=1).astype(BF16)
                v_ext = jnp.concatenate([v_ref[pl.ds(k0, kext), vsl], ones_v], axis=1)
                acc_scr[h, rows] = jnp.concatenate([alpha, alpha], axis=1) * acc_scr[h, rows] + _dot(p, v_ext)
                m_scr[h, rows] = m_new
        return carry

    lax.fori_loop(0, n_full, functools.partial(block, masked=False), 0)
    lax.fori_loop(n_full, n_vis, functools.partial(block, masked=True), 0)
    for h in range(MLA_HEADS):
        vsl = slice(h * MLA_V_DIM, (h + 1) * MLA_V_DIM)
        z = z_ref[:, vsl].astype(F32)
        acc = acc_scr[h]
        o_ref[:, vsl] = (acc[:, :MLA_V_DIM] / acc[:, MLA_V_DIM:] * (z * _sigmoid(z))).astype(BF16)


def _attn(q, k, v, pm, bsz, t, s_len, past, tq, tk):
    nq = t // tq
    return pl.pallas_call(
        functools.partial(_attn_body, past=past, tq=tq, tk=tk, s_len=s_len),
        grid=(bsz, nq),
        in_specs=[pl.BlockSpec((tq, MLA_QK_WIDTH), lambda b, iq: (b * nq + iq, 0)),
                  pl.BlockSpec((s_len, MLA_QK_WIDTH), lambda b, iq: (b, 0)),
                  pl.BlockSpec((s_len, MLA_WIDTH), lambda b, iq: (b, 0)),
                  pl.BlockSpec((tq, MLA_WIDTH), lambda b, iq: (b * nq + iq, 1))],
        out_specs=pl.BlockSpec((tq, MLA_WIDTH), lambda b, iq: (b * nq + iq, 0)),
        out_shape=jax.ShapeDtypeStruct((bsz * t, MLA_WIDTH), BF16),
        scratch_shapes=[pltpu.VMEM((MLA_HEADS, tq, LANES), F32), pltpu.VMEM((MLA_HEADS, tq, 2 * MLA_V_DIM), F32)],
        compiler_params=_cparams("parallel", "arbitrary"),
        name="mla_attn",
    )(q, k, v, pm)


def _s5_weights_body(bre_ref, bim_ref, cre_ref, cim_ref, spread_ref, mask_ref, wb_ref, wc_ref):
    def expand(m_ref):
        return _dot(m_ref[...].astype(BF16), spread_ref[...]) * mask_ref[...]

    wb_ref[:, 0:S5_NSTATE] = expand(bre_ref).astype(BF16)
    wb_ref[:, S5_NSTATE:] = expand(bim_ref).astype(BF16)
    wc_ref[0:S5_NSTATE, :] = jnp.transpose(expand(cre_ref)).astype(BF16)
    wc_ref[S5_NSTATE:, :] = jnp.transpose(-expand(cim_ref)).astype(BF16)


def _s5_weights(bbr, bbi, c_re, c_im):
    depth = bbr.shape[0]
    g_row = np.arange(S5_WIDTH) // S5_GROUP_CH
    g_col = np.arange(S5_NSTATE) // S5_STATE
    mask = (g_row[:, None] == g_col[None, :]).astype(np.float32)
    spread = (np.arange(S5_STATE)[:, None] == (np.arange(S5_NSTATE) % S5_STATE)[None, :]).astype(np.float32)
    small = pl.BlockSpec((None, S5_WIDTH, S5_STATE), lambda l: (l, 0, 0))
    return pl.pallas_call(
        _s5_weights_body,
        grid=(depth,),
        in_specs=[small, small, small, small,
                  _const_spec((S5_STATE, S5_NSTATE), 1), _const_spec((S5_WIDTH, S5_NSTATE), 1)],
        out_specs=[pl.BlockSpec((None, S5_WIDTH, 2 * S5_NSTATE), lambda l: (l, 0, 0)),
                   pl.BlockSpec((None, 2 * S5_NSTATE, S5_WIDTH), lambda l: (l, 0, 0))],
        out_shape=[jax.ShapeDtypeStruct((depth, S5_WIDTH, 2 * S5_NSTATE), BF16),
                   jax.ShapeDtypeStruct((depth, 2 * S5_NSTATE, S5_WIDTH), BF16)],
        compiler_params=_cparams("parallel"),
        name="s5_weights",
    )(bbr, bbi, c_re, c_im, jnp.asarray(spread, BF16), jnp.asarray(mask, F32))


def _s5_body(p0_ref, pn_ref, x0r_ref, x0i_ref, lre_ref, lim_ref, wb_ref, wc_ref, d_ref, wglu_ref, bglu_ref, never_ref,
             o_ref, xr_out, xi_out, uz_bt, uz_a, uz_b, uz_c, bu_a, bu_b, bu_c, o_tb, xr_s, xi_s, *, lc, pitch):
    it = pl.program_id(1)
    nb = S5_BATCH_TILE
    ring = ((uz_a, bu_a), (uz_b, bu_b), (uz_c, bu_c))

    def stage_in(blk_ref, uz_tb, bu):
        for b in range(nb):
            for c in range(PS_COLS // LANES):
                uz_bt[c, b * pitch:b * pitch + lc, :] = blk_ref[b, :, c * LANES:(c + 1) * LANES].astype(F32)
        for t in range(lc):
            for c in range(PS_COLS // LANES):
                uz_tb[t * nb:(t + 1) * nb, c * LANES:(c + 1) * LANES] = uz_bt[c, pl.ds(t, nb, stride=pitch), :]
        bu[...] = _dot(uz_tb[:, 0:S5_WIDTH].astype(BF16), wb_ref[...])

    def stage_scan(bu):
        w = S5_NSTATE // S5_SCAN_SLICES
        never = never_ref[...] != 0
        last = None
        for c0 in range(0, S5_NSTATE, w):
            re, im = slice(c0, c0 + w), slice(S5_NSTATE + c0, S5_NSTATE + c0 + w)
            lre = jnp.broadcast_to(lre_ref[:, re], (nb, w))
            lim = jnp.broadcast_to(lim_ref[:, re], (nb, w))
            xr, xi = xr_s[:, re], xi_s[:, re]
            if last is not None:
                xr = jnp.where(never, last, xr)
            for t in range(lc):
                rs = slice(t * nb, (t + 1) * nb)
                xr, xi = lre * xr - lim * xi + bu[rs, re], lre * xi + lim * xr + bu[rs, im]
                bu[rs, re] = xr
                bu[rs, im] = xi
            xr_s[:, re] = xr
            xi_s[:, re] = xi
            last = xr

    def stage_out(uz_tb, xs):
        y = _dot(xs[...].astype(BF16), wc_ref[...]) + d_ref[...] * uz_tb[:, 0:S5_WIDTH]
        g5 = 0.5 * y * (1.0 + jnp.tanh(0.7978845608028654 * (y + 0.044715 * (y * y * y))))
        gate = _sigmoid(_dot(g5.astype(BF16), wglu_ref[...]) + bglu_ref[...])
        z = uz_tb[:, S5_WIDTH:2 * S5_WIDTH]
        o = g5 * gate * (z * _sigmoid(z))
        for c in range(S5_WIDTH // LANES):
            o_tb[c] = o[:, c * LANES:(c + 1) * LANES]
        for b in range(nb):
            for c in range(S5_WIDTH // LANES):
                o_ref[b, :, c * LANES:(c + 1) * LANES] = o_tb[c, pl.ds(b, lc, stride=nb), :].astype(BF16)

    @pl.when(it == 0)
    def _():
        xr_s[...] = x0r_ref[...]
        xi_s[...] = x0i_ref[...]
        stage_in(p0_ref, uz_a, bu_a)
        uz_c[...] = jnp.zeros(uz_c.shape, F32)
        bu_c[...] = jnp.zeros(bu_c.shape, F32)

    for r in range(3):
        @pl.when(it % 3 == r)
        def _(r=r):
            stage_in(pn_ref, *ring[(r + 1) % 3])
            stage_scan(ring[r][1])
            stage_out(*ring[(r + 2) % 3])

    @pl.when(it == pl.num_programs(1) - 2)
    def _():
        xr_out[...] = xr_s[...]
        xi_out[...] = xi_s[...]


def _s5(ps3, x0r, x0i, lre, lim, wb, wc, d, wglu, bglu, layer, bsz, t, lc):
    nb = S5_BATCH_TILE
    nt = t // lc
    pitch = lc + 8
    rows = lc * nb
    return pl.pallas_call(
        functools.partial(_s5_body, lc=lc, pitch=pitch),
        grid=(bsz // nb, nt + 1),
        in_specs=[pl.BlockSpec((nb, lc, PS_COLS), lambda g, i: (g, 0, 0)),
                  pl.BlockSpec((nb, lc, PS_COLS), lambda g, i: (g, jnp.minimum(i + 1, nt - 1), 0)),
                  pl.BlockSpec((nb, S5_NSTATE), lambda g, i: (g, 0)),
                  pl.BlockSpec((nb, S5_NSTATE), lambda g, i: (g, 0)),
                  _layer_spec((1, S5_NSTATE), layer), _layer_spec((1, S5_NSTATE), layer),
                  _layer_spec((S5_WIDTH, 2 * S5_NSTATE), layer), _layer_spec((2 * S5_NSTATE, S5_WIDTH), layer),
                  _layer_spec((1, S5_WIDTH), layer), _layer_spec((S5_WIDTH, S5_WIDTH), layer),
                  _layer_spec((1, S5_WIDTH), layer),
                  _const_spec((1, S5_NSTATE // S5_SCAN_SLICES), 2)],
        out_specs=[pl.BlockSpec((nb, lc, S5_WIDTH), lambda g, i: (g, jnp.maximum(i - 1, 0), 0)),
                   pl.BlockSpec((nb, S5_NSTATE), lambda g, i: (g, 0)),
                   pl.BlockSpec((nb, S5_NSTATE), lambda g, i: (g, 0))],
        out_shape=[jax.ShapeDtypeStruct((bsz, t, S5_WIDTH), BF16),
                   jax.ShapeDtypeStruct((bsz, S5_NSTATE), F32),
                   jax.ShapeDtypeStruct((bsz, S5_NSTATE), F32)],
        scratch_shapes=[pltpu.VMEM((PS_COLS // LANES, nb * pitch, LANES), F32)]
                       + [pltpu.VMEM((rows, PS_COLS), F32)] * 3
                       + [pltpu.VMEM((rows, 2 * S5_NSTATE), F32)] * 3
                       + [pltpu.VMEM((S5_WIDTH // LANES, rows, LANES), F32),
                          pltpu.VMEM((nb, S5_NSTATE), F32), pltpu.VMEM((nb, S5_NSTATE), F32)],
        compiler_params=_cparams("parallel", "arbitrary"),
        name="s5",
    )(ps3, ps3, x0r, x0i, lre, lim, wb, wc, d, wglu, bglu, jnp.zeros((1, S5_NSTATE // S5_SCAN_SLICES), jnp.int32))


def _outproj_rows(x, og_ref, om_ref, os_ref, wo_scr):
    acc = _dot(og_ref[...], wo_scr[0:GLA_WIDTH, :])
    acc += _dot(om_ref[...], wo_scr[GLA_WIDTH:GLA_WIDTH + MLA_WIDTH, :])
    acc += _dot(os_ref[...], wo_scr[GLA_WIDTH + MLA_WIDTH:, :])
    return x + acc


def _outproj_body(x_ref, og_ref, om_ref, os_ref, w_ref, g_ref, o_ref, wo_scr, *, final):
    @pl.when(pl.program_id(0) == 0)
    def _():
        wo_scr[...] = w_ref[...].astype(BF16)

    xn = _outproj_rows(x_ref[...], og_ref, om_ref, os_ref, wo_scr)
    if final:
        ms = jnp.mean(xn * xn, axis=-1, keepdims=True)
        xn = xn * lax.rsqrt(ms + EPS) * g_ref[...]
    o_ref[...] = xn


def _out_in_body(x_ref, og_ref, om_ref, os_ref, wo_ref, g_ref, wt_ref, xo_ref, pg_ref, pm_ref, ps_ref,
                 wo_scr, w_scr):
    @pl.when(pl.program_id(0) == 0)
    def _():
        wo_scr[...] = wo_ref[...].astype(BF16)
        w_scr[...] = wt_ref[...].astype(BF16)

    xn = _outproj_rows(x_ref[...], og_ref, om_ref, os_ref, wo_scr)
    xo_ref[...] = xn
    _inproj_rows(xn, g_ref, w_scr, pg_ref, pm_ref, ps_ref)


def _out_in(x2, og, om, os_, w_out, ln, wt, layer, tm):
    n = x2.shape[0]
    row = lambda c: pl.BlockSpec((tm, c), lambda i: (i, 0))
    return pl.pallas_call(
        _out_in_body,
        grid=(n // tm,),
        in_specs=[row(D_MODEL), row(GLA_WIDTH), row(MLA_WIDTH), row(S5_WIDTH),
                  _layer_spec((D_MODEL, D_MODEL), layer, single_buffer=True),
                  _layer_spec((1, D_MODEL), layer + 1),
                  _layer_spec((IN_OFF['end'], D_MODEL), layer + 1, single_buffer=True)],
        out_specs=[row(D_MODEL), row(PG_COLS), row(PM_COLS), row(PS_COLS)],
        out_shape=[jax.ShapeDtypeStruct((n, D_MODEL), F32), jax.ShapeDtypeStruct((n, PG_COLS), BF16),
                   jax.ShapeDtypeStruct((n, PM_COLS), BF16), jax.ShapeDtypeStruct((n, PS_COLS), BF16)],
        scratch_shapes=[pltpu.VMEM((D_MODEL, D_MODEL), BF16), pltpu.VMEM((IN_OFF['end'], D_MODEL), BF16)],
        compiler_params=_cparams("arbitrary"),
        name="outproj_inproj",
    )(x2, og, om, os_, w_out, ln, wt)


def _outproj(x2, og, om, os_, w, gain, layer, tm, final):
    n = x2.shape[0]
    row = lambda c: pl.BlockSpec((tm, c), lambda i: (i, 0))
    return pl.pallas_call(
        functools.partial(_outproj_body, final=final),
        grid=(n // tm,),
        in_specs=[row(D_MODEL), row(GLA_WIDTH), row(MLA_WIDTH), row(S5_WIDTH),
                  _layer_spec((D_MODEL, D_MODEL), layer, single_buffer=True),
                  _const_spec((1, D_MODEL), 1)],
        out_specs=row(D_MODEL),
        out_shape=jax.ShapeDtypeStruct((n, D_MODEL), F32),
        scratch_shapes=[pltpu.VMEM((D_MODEL, D_MODEL), BF16)],
        compiler_params=_cparams("arbitrary"),
        name="outproj_final" if final else "outproj",
    )(x2, og, om, os_, w, gain)


def _prepare_params(ln_gain, w_in, gla_w_gate, gla_b_gate, gla_norm_gain, mla_q_norm_gain, mla_w_uq,
                    mla_kv_norm_gain, mla_w_ukv, s5_lambda_re, s5_lambda_im, s5_b_re, s5_b_im, s5_c_re, s5_c_im,
                    s5_d, s5_log_dt, s5_w_glu, s5_b_glu, w_out):
    depth = w_in.shape[0]
    w_t = jnp.swapaxes(w_in, 1, 2)
    wg = jnp.pad(gla_w_gate, ((0, 0), (0, GLA_QK - GLA_GATE_RANK), (0, 0))).astype(BF16)
    wq = mla_w_uq.reshape(depth, MLA_Q_RANK, MLA_HEADS, MLA_NOPE_DIM + MLA_ROPE_DIM)
    wq = jnp.pad(wq, ((0, 0), (0, 256 - MLA_Q_RANK), (0, 0), (0, MLA_QK_PAD - MLA_NOPE_DIM - MLA_ROPE_DIM)))
    wq = wq.reshape(depth, 256, MLA_QK_WIDTH).astype(BF16)
    gq = jnp.pad(mla_q_norm_gain, ((0, 0), (0, 256 - MLA_Q_RANK))).reshape(depth, 1, 256)
    wkv = mla_w_ukv.reshape(depth, MLA_KV_RANK, MLA_HEADS, MLA_NOPE_DIM + MLA_V_DIM)
    wk = jnp.pad(wkv[..., :MLA_NOPE_DIM], ((0, 0), (0, 0), (0, 0), (0, MLA_QK_PAD - MLA_NOPE_DIM)))
    wkv_r = jnp.concatenate([wk.reshape(depth, MLA_KV_RANK, MLA_QK_WIDTH),
                             wkv[..., MLA_NOPE_DIM:].reshape(depth, MLA_KV_RANK, MLA_WIDTH)], axis=2).astype(BF16)
    wka = jnp.pad(jnp.transpose(wkv[..., :MLA_NOPE_DIM], (0, 2, 3, 1)),
                  ((0, 0), (0, 0), (0, MLA_QK_PAD - MLA_NOPE_DIM), (0, 0))).astype(BF16)
    wv = jnp.transpose(wkv[..., MLA_NOPE_DIM:], (0, 2, 1, 3)).astype(BF16)
    dt = jnp.exp(s5_log_dt)[:, :, None]
    mag = jnp.exp(s5_lambda_re * dt)
    lbr, lbi = mag * jnp.cos(s5_lambda_im * dt), mag * jnp.sin(s5_lambda_im * dt)
    den = s5_lambda_re * s5_lambda_re + s5_lambda_im * s5_lambda_im
    qr = ((lbr - 1.0) * s5_lambda_re + lbi * s5_lambda_im) / den
    qi = (lbi * s5_lambda_re - (lbr - 1.0) * s5_lambda_im) / den
    b_re_t, b_im_t = jnp.swapaxes(s5_b_re, 2, 3), jnp.swapaxes(s5_b_im, 2, 3)
    bbr = qr[:, :, None, :] * b_re_t - qi[:, :, None, :] * b_im_t
    bbi = qr[:, :, None, :] * b_im_t + qi[:, :, None, :] * b_re_t
    rows = lambda m: m.reshape(depth, S5_WIDTH, S5_STATE)
    wb, wc = _s5_weights(rows(bbr), rows(bbi), rows(s5_c_re), rows(s5_c_im))
    return dict(
        ln=ln_gain.reshape(depth, 1, D_MODEL), w_in=w_t, wg=wg, bg=gla_b_gate.reshape(depth, 1, GLA_QK),
        gla_gain=jnp.tile(gla_norm_gain, (1, GLA_HEADS)).reshape(depth, 1, GLA_WIDTH),
        gq=gq, wq=wq, gkv=mla_kv_norm_gain.reshape(depth, 1, MLA_KV_RANK), wkv=wkv_r, wka=wka, wv=wv,
        lre=lbr.reshape(depth, 1, S5_NSTATE), lim=lbi.reshape(depth, 1, S5_NSTATE),
        wb=wb, wc=wc, d=s5_d.reshape(depth, 1, S5_WIDTH), wglu=s5_w_glu.astype(BF16),
        bglu=s5_b_glu.reshape(depth, 1, S5_WIDTH), w_out=w_out)


def _rope_tables(past, t, reps):
    half = MLA_ROPE_DIM // 2
    inv = ROPE_BASE ** (-np.arange(half, dtype=np.float64) / half)
    ang = (past + np.arange(t, dtype=np.float64))[:, None] * inv[None, :]
    cos, sin = np.cos(ang), np.sin(ang)
    pad = MLA_QK_PAD - MLA_NOPE_DIM - MLA_ROPE_DIM
    cos_t = np.concatenate([np.ones((t, MLA_NOPE_DIM)), cos, cos, np.zeros((t, pad))], axis=1)
    sin_t = np.concatenate([np.zeros((t, MLA_NOPE_DIM)), sin, sin, np.zeros((t, pad))], axis=1)
    return jnp.asarray(np.tile(cos_t, (reps, 1)), F32), jnp.asarray(np.tile(sin_t, (reps, 1)), F32)


def _trunk(x, gla_state, ckv_cache, kpe_cache, s5_re, s5_im, p, final_gain):
    bsz, t, _ = x.shape
    n = bsz * t
    depth = p['w_in'].shape[0]
    past = 0 if ckv_cache is None else ckv_cache.shape[2]
    s_len = past + t
    tl = _tiles(bsz, t, past)
    cos_t, sin_t = _rope_tables(past, t, max(1, tl['prep'] // t))
    kpet_cache = None if kpe_cache is None else jnp.swapaxes(kpe_cache, 2, 3)
    x2 = x.reshape(n, D_MODEL)
    gain_f = final_gain.reshape(1, D_MODEL)
    gla_o, ckv_o, kpe_o, re_o, im_o = [], [], [], [], []
    pg, pm, ps = _inproj(x2, p['ln'], p['w_in'], 0, tl['row'])
    for l in range(depth):
        s0 = jnp.zeros((bsz, GLA_HEADS, GLA_DK, GLA_DV), F32) if gla_state is None else gla_state[l]
        o_gla, s_new = _gla(pg.reshape(bsz, t, PG_COLS), p['wg'], p['bg'], p['gla_gain'], l, s0, bsz, t, tl)
        o_gla = o_gla.reshape(n, GLA_WIDTH)
        gla_o.append(s_new)
        prep = functools.partial(_mla_prep, pm, cos_t, sin_t, p['gq'], p['wq'], p['gkv'], p['wkv'], l, bsz, t,
                                 tl['prep'])
        if past == 0:
            q, ckv_new, kpet, k_cat, v_all = prep(with_kv=True)
            kpe_o.append(jnp.swapaxes(kpet, 1, 2))
            o_mla = _attn(q, k_cat, v_all, pm, bsz, t, s_len, past, tl['attn_q'], tl['attn_k'])
        else:
            q, ckv_new, kpe128 = prep(with_kv=False)
            kpe_o.append(kpe128[:, MLA_NOPE_DIM:MLA_NOPE_DIM + MLA_ROPE_DIM].reshape(bsz, t, MLA_ROPE_DIM))
            o_mla = _attn_cached(q, pm, ckv_new, kpe128, ckv_cache, kpet_cache, p['wka'], p['wv'], l, bsz, t,
                                 tl['cached_seqs'])
        ckv_o.append(ckv_new.reshape(bsz, t, MLA_KV_RANK))
        x0r = jnp.zeros((bsz, S5_NSTATE), F32) if s5_re is None else s5_re[l].reshape(bsz, S5_NSTATE)
        x0i = jnp.zeros((bsz, S5_NSTATE), F32) if s5_im is None else s5_im[l].reshape(bsz, S5_NSTATE)
        o_s5, xr, xi = _s5(ps.reshape(bsz, t, PS_COLS), x0r, x0i, p['lre'], p['lim'], p['wb'], p['wc'],
                           p['d'], p['wglu'], p['bglu'], l, bsz, t, tl['s5_rows'])
        re_o.append(xr.reshape(bsz, S5_GROUPS, S5_STATE))
        im_o.append(xi.reshape(bsz, S5_GROUPS, S5_STATE))
        o_s5 = o_s5.reshape(n, S5_WIDTH)
        if l < depth - 1:
            x2, pg, pm, ps = _out_in(x2, o_gla, o_mla, o_s5, p['w_out'], p['ln'], p['w_in'], l, tl['row'])
        else:
            x2 = _outproj(x2, o_gla, o_mla, o_s5, p['w_out'], gain_f, l, tl['out_row'], final=True)
    return (x2.reshape(bsz, t, D_MODEL), jnp.stack(gla_o), jnp.stack(ckv_o), jnp.stack(kpe_o),
            jnp.stack(re_o), jnp.stack(im_o))


def kernel(x_prompt, x_sample, state_gla, cache_mla_ckv, cache_mla_kpe, state_s5_re, state_s5_im, ln_gain, w_in, gla_w_gate, gla_b_gate, gla_norm_gain, mla_q_norm_gain, mla_w_uq, mla_kv_norm_gain, mla_w_ukv, s5_lambda_re, s5_lambda_im, s5_b_re, s5_b_im, s5_c_re, s5_c_im, s5_d, s5_log_dt, s5_w_glu, s5_b_glu, w_out, final_gain):
    p = _prepare_params(ln_gain, w_in, gla_w_gate, gla_b_gate, gla_norm_gain, mla_q_norm_gain, mla_w_uq,
                        mla_kv_norm_gain, mla_w_ukv, s5_lambda_re, s5_lambda_im, s5_b_re, s5_b_im,
                        s5_c_re, s5_c_im, s5_d, s5_log_dt, s5_w_glu, s5_b_glu, w_out)
    y_p, gla_p, ckv_p, kpe_p, re_p, im_p = _trunk(x_prompt, None, None, None, None, None, p, final_gain)
    y_s, gla_s, ckv_s, kpe_s, re_s, im_s = _trunk(x_sample, state_gla, cache_mla_ckv, cache_mla_kpe,
                                                  state_s5_re, state_s5_im, p, final_gain)
    return (y_p, y_s, gla_p, ckv_p, kpe_p, re_p, im_p, gla_s, ckv_s, kpe_s, re_s, im_s)
```

```python
import functools
import math

import numpy as np
import jax
import jax.numpy as jnp
from jax import lax
from jax.experimental import pallas as pl
from jax.experimental.pallas import tpu as pltpu

F32 = jnp.float32
BF16 = jnp.bfloat16

LANES = 128
D_MODEL = 1024
CHUNK = 64
EPS = 1e-6
GLA_HEADS = 4
GLA_DV = 64
GLA_DK = 32
GLA_WIDTH = GLA_HEADS * GLA_DV
GLA_QK = GLA_HEADS * GLA_DK
GLA_GATE_RANK = 16
GLA_GATE_TAU = 16.0
GLA_SUB = 16
GLA_PLAIN_MAX_DECAY = 60.0
MLA_HEADS = 4
MLA_NOPE_DIM = 64
MLA_ROPE_DIM = 32
MLA_V_DIM = 128
MLA_Q_RANK = 192
MLA_KV_RANK = 128
MLA_WIDTH = MLA_HEADS * MLA_V_DIM
MLA_QK_PAD = 128
MLA_QK_WIDTH = MLA_HEADS * MLA_QK_PAD
ROPE_BASE = 10000.0
S5_GROUPS = 16
S5_GROUP_CH = 16
S5_STATE = 64
S5_WIDTH = S5_GROUPS * S5_GROUP_CH
S5_NSTATE = S5_GROUPS * S5_STATE
S5_BATCH_TILE = 8
S5_SCAN_SLICES = 4

PG_COLS = 896
PM_COLS = 1024
PS_COLS = 512
_IN_SEGS = (('g_q', GLA_QK), ('g_k', GLA_QK), ('g_v', GLA_WIDTH), ('g_lr', GLA_GATE_RANK), ('g_z', GLA_WIDTH),
            ('m_cq', MLA_Q_RANK), ('m_ckv', MLA_KV_RANK), ('m_kr', MLA_ROPE_DIM), ('m_z', MLA_WIDTH),
            ('s_u', S5_WIDTH), ('s_z', S5_WIDTH), ('end', 0))
IN_OFF = dict(zip([n for n, _ in _IN_SEGS], np.cumsum([0] + [w for _, w in _IN_SEGS[:-1]]).tolist()))

VMEM_LIMIT_BYTES = 48 * 1024 * 1024


def _tiles(bsz, t, past):
    n = bsz * t
    s_len = past + t
    return dict(
        row=min(512, n),
        out_row=min(1024, n),
        prep=min(512, n) if t < 512 else 512,
        cached_seqs=math.gcd(bsz, max(1, 64 // t)),
        gla_rows=min(512, t), gla_chunk=min(CHUNK, t),
        gla_seqs=math.gcd(bsz, max(2, 128 // t)),
        attn_q=min(512, t), attn_k=min(512, s_len),
        s5_rows=min(64, t))


def _cparams(*sem):
    return pltpu.CompilerParams(dimension_semantics=sem, vmem_limit_bytes=VMEM_LIMIT_BYTES)


def _sigmoid(x):
    return 0.5 * (1.0 + jnp.tanh(0.5 * x))


def _dot(a, b):
    return jnp.dot(a, b, preferred_element_type=F32)


def _dot_t(a, b):
    return lax.dot_general(a, b, (((0,), (0,)), ((), ())), preferred_element_type=F32)


def _const_spec(shape, ngrid):
    zeros = (0,) * len(shape)
    return pl.BlockSpec(shape, lambda *_: zeros)


def _layer_spec(shape, layer, single_buffer=False):
    zeros = (0,) * len(shape)
    mode = dict(pipeline_mode=pl.Buffered(1)) if single_buffer else {}
    return pl.BlockSpec((None,) + tuple(shape), lambda *_: (layer,) + zeros, **mode)


def _inproj_rows(x, g_ref, w_scr, og_ref, om_ref, os_ref):
    ms = jnp.mean(x * x, axis=-1, keepdims=True)
    h = (x * lax.rsqrt(ms + EPS) * g_ref[...]).astype(BF16)

    def seg(a, b):
        return lax.dot_general(h, w_scr[a:b, :], (((1,), (1,)), ((), ())), preferred_element_type=F32)

    lane = lax.broadcasted_iota(jnp.int32, (x.shape[0], LANES), 1)
    lane2 = lax.broadcasted_iota(jnp.int32, (x.shape[0], 2 * LANES), 1)
    c = IN_OFF
    og_ref[:, 0:512] = seg(c['g_q'], c['g_lr']).astype(BF16)
    og_ref[:, 512:768] = seg(c['g_z'], c['m_cq']).astype(BF16)
    og_ref[:, 768:896] = jnp.where(lane < GLA_GATE_RANK, seg(c['g_lr'], c['g_lr'] + LANES), 0.0).astype(BF16)
    om_ref[:, 0:256] = jnp.where(lane2 < MLA_Q_RANK, seg(c['m_cq'], c['m_cq'] + 2 * LANES), 0.0).astype(BF16)
    ckv_kr = seg(c['m_ckv'], c['m_ckv'] + 2 * LANES)
    om_ref[:, 256:384] = ckv_kr[:, 0:LANES].astype(BF16)
    kr = pltpu.roll(ckv_kr[:, LANES:], MLA_NOPE_DIM, 1)
    om_ref[:, 384:512] = jnp.where((lane >= MLA_NOPE_DIM) & (lane < MLA_NOPE_DIM + MLA_ROPE_DIM), kr, 0.0).astype(BF16)
    om_ref[:, 512:1024] = seg(c['m_z'], c['s_u']).astype(BF16)
    os_ref[...] = seg(c['s_u'], c['end']).astype(BF16)


def _inproj_body(x_ref, g_ref, wt_ref, og_ref, om_ref, os_ref, w_scr):
    @pl.when(pl.program_id(0) == 0)
    def _():
        w_scr[...] = wt_ref[...].astype(BF16)

    _inproj_rows(x_ref[...], g_ref, w_scr, og_ref, om_ref, os_ref)


def _inproj(x2, gain, wt, layer, tm):
    n = x2.shape[0]
    return pl.pallas_call(
        _inproj_body,
        grid=(n // tm,),
        in_specs=[pl.BlockSpec((tm, D_MODEL), lambda i: (i, 0)),
                  _layer_spec((1, D_MODEL), layer),
                  _layer_spec((IN_OFF['end'], D_MODEL), layer, single_buffer=True)],
        scratch_shapes=[pltpu.VMEM((IN_OFF['end'], D_MODEL), BF16)],
        out_specs=[pl.BlockSpec((tm, PG_COLS), lambda i: (i, 0)),
                   pl.BlockSpec((tm, PM_COLS), lambda i: (i, 0)),
                   pl.BlockSpec((tm, PS_COLS), lambda i: (i, 0))],
        out_shape=[jax.ShapeDtypeStruct((n, PG_COLS), BF16),
                   jax.ShapeDtypeStruct((n, PM_COLS), BF16),
                   jax.ShapeDtypeStruct((n, PS_COLS), BF16)],
        compiler_params=_cparams("arbitrary"),
        name="inproj",
    )(x2, gain, wt)


def _gla_body(p_ref, wg_ref, bg_ref, gain_ref, ones_k_ref, ones_v_ref, bd_ref, s0_ref,
              o_ref, sout_ref, q_scr, k_scr, v_scr, b_scr, s_scr, qe_scr, qs_scr, ke_scr, vb_scr, o_scr,
              *, chunk, nch, nseq):
    it = pl.program_id(1)
    sub = min(GLA_SUB, chunk)
    nsub = chunk // sub

    @pl.when(it == 0)
    def _():
        s_scr[...] = jnp.zeros(s_scr.shape, F32)
        for sq in range(nseq):
            for h in range(GLA_HEADS):
                s_scr[sq, h * GLA_DK:(h + 1) * GLA_DK, h * GLA_DV:(h + 1) * GLA_DV] = s0_ref[sq, h]

    def cols(a, b):
        parts = [p_ref[sq, :, a:b] for sq in range(nseq)]
        return parts[0] if nseq == 1 else jnp.concatenate(parts, axis=0)

    q_scr[...] = cols(0, 128).astype(F32) * (GLA_DK ** -0.5)
    k_scr[...] = cols(128, 256).astype(F32)
    v_scr[...] = cols(256, 512).astype(F32)
    logit = _dot(cols(768, 896), wg_ref[...]) + bg_ref[...]
    log_a = (jnp.minimum(logit, 0.0) - jnp.log(1.0 + jnp.exp(-jnp.abs(logit)))) * (1.0 / GLA_GATE_TAU)
    row_in_chunk = lax.broadcasted_iota(jnp.int32, log_a.shape, 0) % chunk
    b_all = log_a
    shift = 1
    while shift < chunk:
        b_all = b_all + jnp.where(row_in_chunk >= shift, pltpu.roll(b_all, shift, 0), 0.0)
        shift *= 2
    b_scr[...] = b_all

    row = lax.broadcasted_iota(jnp.int32, (chunk, GLA_QK), 0)
    row_in_sub = lax.broadcasted_iota(jnp.int32, (sub, GLA_QK), 0)

    def decay_columns(b_end):
        col = jnp.transpose(jnp.broadcast_to(jnp.exp(b_end), (GLA_QK, GLA_QK)))
        return jnp.concatenate([col, col], axis=1)

    def load_chunk(sq, c):
        r0 = pl.multiple_of((sq * nch + c) * chunk, chunk)
        return (r0, q_scr[pl.ds(r0, chunk), :], k_scr[pl.ds(r0, chunk), :], v_scr[pl.ds(r0, chunk), :],
                b_scr[pl.ds(r0, chunk), :], b_scr[pl.ds(r0 + chunk - 1, 1), :], s_scr[sq])

    def finish_chunk(sq, r0, o, s_prev, a_state, b_end):
        ms = _dot((o * o).astype(BF16), ones_v_ref[...]) * (1.0 / GLA_DV)
        o_n = o * lax.rsqrt(ms + EPS) * gain_ref[...]
        t0 = r0 - sq * nch * chunk
        z = p_ref[sq, pl.ds(t0, chunk), 512:768].astype(F32)
        o_ref[sq, pl.ds(t0, chunk), :] = (o_n * (z * _sigmoid(z))).astype(BF16)
        s_scr[sq] = s_prev * decay_columns(b_end) + a_state

    def robust_chunk(c, carry, sq):
        r0, qc, kc, vc, bc, b_end, s_prev = load_chunk(sq, c)
        xs = [qc * jnp.exp(bc)]
        ks = []
        for sj in range(nsub - 1):
            e_j = b_scr[pl.ds(r0 + (sj + 1) * sub - 1, 1), :]
            later = row >= (sj + 1) * sub
            xs.append(jnp.where(later, qc * jnp.exp(jnp.where(later, bc - e_j, 0.0)), 0.0))
            own = (row >= sj * sub) & (row < (sj + 1) * sub)
            ks.append(jnp.where(own, kc * jnp.exp(jnp.where(own, e_j - bc, 0.0)), 0.0))
        ks.append(kc * jnp.exp(b_end - bc))
        k_all = jnp.concatenate(ks, axis=1).astype(BF16)
        a_all = _dot_t(k_all, vc.astype(BF16)) * bd_ref[...]
        w = jnp.concatenate([s_prev, a_all[:(nsub - 1) * GLA_QK]], axis=0).astype(BF16) if nsub > 1 \
            else s_prev.astype(BF16)
        o_off = _dot(jnp.concatenate(xs, axis=1).astype(BF16), w)
        rows = []
        for si in range(nsub):
            q_i = qc[si * sub:(si + 1) * sub]
            b_i = bc[si * sub:(si + 1) * sub]
            es = []
            for j in range(sub):
                r = r0 + si * sub + j
                b_j = b_scr[pl.ds(r, 1), :]
                k_j = k_scr[pl.ds(r, 1), :]
                valid = row_in_sub >= j
                es.append(q_i * k_j * jnp.exp(jnp.where(valid, b_i - b_j, -jnp.inf)))
            e_all = jnp.concatenate(es, axis=0)
            e_hi = e_all.astype(BF16)
            e_lo = (e_all - e_hi.astype(F32)).astype(BF16)
            p_all = _dot(e_hi, ones_k_ref[...]) + _dot(e_lo, ones_k_ref[...])
            acc = o_off[si * sub:(si + 1) * sub]
            for j in range(sub):
                v_j = v_scr[pl.ds(r0 + si * sub + j, 1), :]
                acc = acc + p_all[j * sub:(j + 1) * sub] * v_j
            rows.append(acc)
        o = rows[0] if nsub == 1 else jnp.concatenate(rows, axis=0)
        finish_chunk(sq, r0, o, s_prev, a_all[(nsub - 1) * GLA_QK:], b_end)
        return carry

    lane_head_v = lax.broadcasted_iota(jnp.int32, (chunk, GLA_WIDTH), 1) // GLA_DV
    causal = (lax.broadcasted_iota(jnp.int32, (GLA_HEADS * chunk, chunk), 0) % chunk
              >= lax.broadcasted_iota(jnp.int32, (GLA_HEADS * chunk, chunk), 1))

    def plain_block():
        tt = nseq * nch * chunk
        b_all = b_scr[...]
        q_all = q_scr[...]
        k_all = k_scr[...]
        qe = q_all * jnp.exp(b_all)
        lane_head = lax.broadcasted_iota(jnp.int32, (tt, GLA_QK), 1) // GLA_DK
        qe_scr[...] = qe.astype(BF16)
        for h in range(GLA_HEADS):
            qs_scr[h] = jnp.where(lane_head == h, qe, 0.0).astype(BF16)
        ke_scr[...] = (k_all * jnp.exp(-b_all)).astype(BF16)
        vb_scr[...] = cols(256, 512)
        states = [s_scr[sq] for sq in range(nseq)]
        for c, sq in [(c, sq) for c in range(nch) for sq in range(nseq)]:
            r0 = (sq * nch + c) * chunk
            rs = slice(r0, r0 + chunk)
            s_cur = states[sq]
            qs = jnp.concatenate([qs_scr[h, rs, :] for h in range(GLA_HEADS)], axis=0)
            s = lax.dot_general(qs, ke_scr[rs, :], (((1,), (1,)), ((), ())), preferred_element_type=F32)
            s = jnp.where(causal, s, 0.0).astype(BF16)
            r = _dot(s, vb_scr[rs, :])
            o = _dot(qe_scr[rs, :], s_cur.astype(BF16))
            for h in range(GLA_HEADS):
                o = o + jnp.where(lane_head_v == h, r[h * chunk:(h + 1) * chunk], 0.0)
            o_scr[rs, :] = o
            b_end = b_scr[r0 + chunk - 1:r0 + chunk, :]
            k_end = (k_scr[rs, :] * jnp.exp(b_end - b_scr[rs, :])).astype(BF16)
            a_state = _dot_t(k_end, vb_scr[rs, :]) * bd_ref[0:GLA_QK, :]
            states[sq] = s_cur * decay_columns(b_end) + a_state
        for sq in range(nseq):
            s_scr[sq] = states[sq]
        o = o_scr[...]
        ms = _dot((o * o).astype(BF16), ones_v_ref[...]) * (1.0 / GLA_DV)
        o_n = o * lax.rsqrt(ms + EPS) * gain_ref[...]
        z = cols(512, 768).astype(F32)
        o_all = (o_n * (z * _sigmoid(z))).astype(BF16)
        for sq in range(nseq):
            o_ref[sq] = o_all[sq * nch * chunk:(sq + 1) * nch * chunk]

    in_range = jnp.max(-b_scr[...]) < GLA_PLAIN_MAX_DECAY

    @pl.when(in_range)
    def _():
        plain_block()

    @pl.when(jnp.logical_not(in_range))
    def _():
        for sq in range(nseq):
            lax.fori_loop(0, nch, functools.partial(robust_chunk, sq=sq), 0)

    @pl.when(it == pl.num_programs(1) - 1)
    def _():
        for sq in range(nseq):
            for h in range(GLA_HEADS):
                sout_ref[sq, h] = s_scr[sq, h * GLA_DK:(h + 1) * GLA_DK, h * GLA_DV:(h + 1) * GLA_DV]


def _gla_consts(chunk):
    nsub = chunk // min(GLA_SUB, chunk)
    hk = np.arange(GLA_QK) // GLA_DK
    hv = np.arange(GLA_WIDTH) // GLA_DV
    same_kv = (hk[:, None] == hv[None, :]).astype(np.float32)
    same_vv = (hv[:, None] == hv[None, :]).astype(np.float32)
    return (jnp.asarray(same_kv, BF16), jnp.asarray(same_vv, BF16),
            jnp.asarray(np.tile(same_kv, (nsub, 1)), F32))


def _gla(pg, wg, bg, gain, layer, s0_bd, bsz, t, tl):
    chunk, tt, nseq = tl['gla_chunk'], tl['gla_rows'], tl['gla_seqs']
    nch = tt // chunk
    nt = t // tt
    rows = nseq * tt
    nsub = chunk // min(GLA_SUB, chunk)
    ones_k, ones_v, bd = _gla_consts(chunk)
    const = lambda shape: _const_spec(shape, 2)
    return pl.pallas_call(
        functools.partial(_gla_body, chunk=chunk, nch=nch, nseq=nseq),
        grid=(bsz // nseq, nt),
        in_specs=[pl.BlockSpec((nseq, tt, PG_COLS), lambda b, i: (b, i, 0)),
                  _layer_spec((GLA_QK, GLA_QK), layer), _layer_spec((1, GLA_QK), layer),
                  _layer_spec((1, GLA_WIDTH), layer),
                  const((GLA_QK, GLA_WIDTH)), const((GLA_WIDTH, GLA_WIDTH)),
                  const((nsub * GLA_QK, GLA_WIDTH)),
                  pl.BlockSpec((nseq, GLA_HEADS, GLA_DK, GLA_DV), lambda b, i: (b, 0, 0, 0))],
        out_specs=[pl.BlockSpec((nseq, tt, GLA_WIDTH), lambda b, i: (b, i, 0)),
                   pl.BlockSpec((nseq, GLA_HEADS, GLA_DK, GLA_DV), lambda b, i: (b, 0, 0, 0))],
        out_shape=[jax.ShapeDtypeStruct((bsz, t, GLA_WIDTH), BF16),
                   jax.ShapeDtypeStruct((bsz, GLA_HEADS, GLA_DK, GLA_DV), F32)],
        scratch_shapes=[pltpu.VMEM((rows, GLA_QK), F32), pltpu.VMEM((rows, GLA_QK), F32),
                        pltpu.VMEM((rows, GLA_WIDTH), F32), pltpu.VMEM((rows, GLA_QK), F32),
                        pltpu.VMEM((nseq, GLA_QK, GLA_WIDTH), F32),
                        pltpu.VMEM((rows, GLA_QK), BF16), pltpu.VMEM((GLA_HEADS, rows, GLA_QK), BF16),
                        pltpu.VMEM((rows, GLA_QK), BF16),
                        pltpu.VMEM((rows, GLA_WIDTH), BF16), pltpu.VMEM((rows, GLA_WIDTH), F32)],
        compiler_params=_cparams("parallel", "arbitrary"),
        name="gla",
    )(pg, wg, bg, gain, ones_k, ones_v, bd, s0_bd)


def _rope128(x, cos_t, sin_t):
    lane = lax.broadcasted_iota(jnp.int32, x.shape, 1)
    first_half = (lane >= MLA_NOPE_DIM) & (lane < MLA_NOPE_DIM + MLA_ROPE_DIM // 2)
    rot = jnp.where(first_half, -pltpu.roll(x, LANES - MLA_ROPE_DIM // 2, 1), pltpu.roll(x, MLA_ROPE_DIM // 2, 1))
    return x * cos_t + rot * sin_t


def _mla_prep_body(p_ref, cos_ref, sin_ref, gq_ref, wq_ref, gkv_ref, *rest, with_kv):
    if with_kv:
        wkv_ref, q_ref, ckv_ref, kpet_ref, k_ref, v_ref = rest
    else:
        q_ref, ckv_ref, kpe_ref = rest
    cos_t = cos_ref[...]
    sin_t = sin_ref[...]
    ones = jnp.ones((2 * LANES, LANES), BF16)
    cq = p_ref[:, 0:256].astype(F32)
    ms = _dot((cq * cq).astype(BF16), ones) * (1.0 / MLA_Q_RANK)
    inv = lax.rsqrt(ms + EPS)
    cqn = (cq * jnp.concatenate([inv, inv], axis=1) * gq_ref[...]).astype(BF16)
    qh = _dot(cqn, wq_ref[...])
    scale = (MLA_NOPE_DIM + MLA_ROPE_DIM) ** -0.5 * math.log2(math.e)
    for h in range(MLA_HEADS):
        x = qh[:, h * MLA_QK_PAD:(h + 1) * MLA_QK_PAD]
        roped = x * cos_t + pltpu.roll(x, LANES - MLA_ROPE_DIM, 1) * sin_t
        q_ref[:, h * MLA_QK_PAD:(h + 1) * MLA_QK_PAD] = (roped * scale).astype(BF16)
    ckv = p_ref[:, 256:384].astype(F32)
    ms = _dot((ckv * ckv).astype(BF16), ones[0:LANES]) * (1.0 / MLA_KV_RANK)
    ckv_n = ckv * lax.rsqrt(ms + EPS) * gkv_ref[...]
    ckv_ref[...] = ckv_n
    kpe128 = _rope128(p_ref[:, 384:512].astype(F32), cos_t, sin_t)
    if with_kv:
        kpet_ref[...] = jnp.transpose(kpe128)[MLA_NOPE_DIM:MLA_NOPE_DIM + MLA_ROPE_DIM, :]
        kv = _dot(ckv_n.astype(BF16), wkv_ref[...])
        for h in range(MLA_HEADS):
            sl = slice(h * MLA_QK_PAD, (h + 1) * MLA_QK_PAD)
            k_ref[:, sl] = (kv[:, sl] + kpe128).astype(BF16)
        v_ref[...] = kv[:, MLA_QK_WIDTH:].astype(BF16)
    else:
        kpe_ref[...] = kpe128


def _mla_prep(pm, cos_t, sin_t, gq, wq, gkv, wkv, layer, bsz, t, tm, with_kv):
    n = bsz * t
    ntab = max(1, t // tm)
    row = lambda c: pl.BlockSpec((tm, c), lambda i: (i, 0))
    in_specs = [row(512),
                pl.BlockSpec((tm, LANES), lambda i: (i % ntab, 0)),
                pl.BlockSpec((tm, LANES), lambda i: (i % ntab, 0)),
                _layer_spec((1, 256), layer), _layer_spec((256, MLA_QK_WIDTH), layer),
                _layer_spec((1, MLA_KV_RANK), layer)]
    out_specs = [row(MLA_QK_WIDTH), row(MLA_KV_RANK)]
    out_shape = [jax.ShapeDtypeStruct((n, MLA_QK_WIDTH), BF16), jax.ShapeDtypeStruct((n, MLA_KV_RANK), F32)]
    args = [pm, cos_t, sin_t, gq, wq, gkv]
    if with_kv:
        in_specs.append(_layer_spec((MLA_KV_RANK, MLA_QK_WIDTH + MLA_WIDTH), layer))
        args.append(wkv)
        out_specs += [pl.BlockSpec((None, MLA_ROPE_DIM, tm), lambda i: (i // ntab, 0, i % ntab)),
                      row(MLA_QK_WIDTH), row(MLA_WIDTH)]
        out_shape += [jax.ShapeDtypeStruct((bsz, MLA_ROPE_DIM, t), F32),
                      jax.ShapeDtypeStruct((n, MLA_QK_WIDTH), BF16), jax.ShapeDtypeStruct((n, MLA_WIDTH), BF16)]
    else:
        out_specs.append(row(LANES))
        out_shape.append(jax.ShapeDtypeStruct((n, LANES), F32))
    return pl.pallas_call(
        functools.partial(_mla_prep_body, with_kv=with_kv),
        grid=(n // tm,),
        in_specs=in_specs, out_specs=out_specs, out_shape=out_shape,
        compiler_params=_cparams("parallel"),
        name="mla_prep_kv" if with_kv else "mla_prep",
    )(*args)


def _attn_cached_body(q_ref, z_ref, ckv_new_ref, kpe_new_ref, ckv_past_ref, kpet_past_ref, wka_ref, wv_ref,
                      o_ref, kpet_scr, *, past, t, nseq):
    last = (((1,), (1,)), ((), ()))
    hsl = [slice(h * MLA_QK_PAD, (h + 1) * MLA_QK_PAD) for h in range(MLA_HEADS)]
    q_lat_h = [_dot(q_ref[:, hsl[h]], wka_ref[h]).astype(BF16) for h in range(MLA_HEADS)]
    kpet_scr[...] = jnp.zeros(kpet_scr.shape, BF16)
    o_lat = []
    for sq in range(nseq):
        rs = slice(sq * t, (sq + 1) * t)
        q_rows = jnp.concatenate([q_ref[rs, hsl[h]] for h in range(MLA_HEADS)], axis=0)
        q_lat = jnp.concatenate([q_lat_h[h][rs] for h in range(MLA_HEADS)], axis=0)
        c_past = ckv_past_ref[sq].astype(BF16)
        c_new = ckv_new_ref[rs, :].astype(BF16)
        kpet_scr[sq, MLA_NOPE_DIM:MLA_NOPE_DIM + MLA_ROPE_DIM, :] = kpet_past_ref[sq].astype(BF16)
        s_past = (lax.dot_general(q_lat, c_past, last, preferred_element_type=F32)
                  + _dot(q_rows, kpet_scr[sq]))
        s_new = (lax.dot_general(q_lat, c_new, last, preferred_element_type=F32)
                 + lax.dot_general(q_rows, kpe_new_ref[rs, :].astype(BF16), last, preferred_element_type=F32))
        if past // CHUNK != (past + t - 1) // CHUNK:
            q_chunk = (past + lax.broadcasted_iota(jnp.int32, s_past.shape, 0) % t) // CHUNK
            s_past = jnp.where(lax.broadcasted_iota(jnp.int32, s_past.shape, 1) // CHUNK <= q_chunk, s_past, -jnp.inf)
            q_chunk = (past + lax.broadcasted_iota(jnp.int32, s_new.shape, 0) % t) // CHUNK
            s_new = jnp.where((past + lax.broadcasted_iota(jnp.int32, s_new.shape, 1)) // CHUNK <= q_chunk,
                              s_new, -jnp.inf)
        m = jnp.maximum(jnp.max(s_past, axis=-1, keepdims=True), jnp.max(s_new, axis=-1, keepdims=True))
        p_past = jnp.exp2(s_past - m)
        p_new = jnp.exp2(s_new - m)
        l = jnp.sum(p_past, axis=-1, keepdims=True) + jnp.sum(p_new, axis=-1, keepdims=True)
        o_lat.append(((_dot(p_past.astype(BF16), c_past) + _dot(p_new.astype(BF16), c_new)) / l).astype(BF16))
    for h in range(MLA_HEADS):
        vsl = slice(h * MLA_V_DIM, (h + 1) * MLA_V_DIM)
        o_h = jnp.concatenate([o_lat[sq][h * t:(h + 1) * t] for sq in range(nseq)], axis=0)
        z = z_ref[:, vsl].astype(F32)
        o_ref[:, vsl] = (_dot(o_h, wv_ref[h]) * (z * _sigmoid(z))).astype(BF16)


def _attn_cached(q, pm, ckv_new, kpe128, ckv_past, kpet_past, wka, wv, layer, bsz, t, nseq):
    past = ckv_past.shape[2]
    rows = nseq * t
    return pl.pallas_call(
        functools.partial(_attn_cached_body, past=past, t=t, nseq=nseq),
        grid=(bsz // nseq,),
        in_specs=[pl.BlockSpec((rows, MLA_QK_WIDTH), lambda b: (b, 0)),
                  pl.BlockSpec((rows, MLA_WIDTH), lambda b: (b, 1)),
                  pl.BlockSpec((rows, MLA_KV_RANK), lambda b: (b, 0)),
                  pl.BlockSpec((rows, LANES), lambda b: (b, 0)),
                  pl.BlockSpec((None, nseq, past, MLA_KV_RANK), lambda b: (layer, b, 0, 0)),
                  pl.BlockSpec((None, nseq, MLA_ROPE_DIM, past), lambda b: (layer, b, 0, 0)),
                  _layer_spec((MLA_HEADS, MLA_QK_PAD, MLA_KV_RANK), layer),
                  _layer_spec((MLA_HEADS, MLA_KV_RANK, MLA_V_DIM), layer)],
        out_specs=pl.BlockSpec((rows, MLA_WIDTH), lambda b: (b, 0)),
        out_shape=jax.ShapeDtypeStruct((bsz * t, MLA_WIDTH), BF16),
        scratch_shapes=[pltpu.VMEM((nseq, MLA_QK_PAD, past), BF16)],
        compiler_params=_cparams("parallel"),
        name="mla_attn_cached",
    )(q, pm, ckv_new, kpe128, ckv_past, kpet_past, wka, wv)


def _attn_body(q_ref, k_ref, v_ref, z_ref, o_ref, m_scr, acc_scr, *, past, tq, tk, s_len):
    iq = pl.program_id(1)
    q_first = past + iq * tq
    full_keys = jnp.minimum((q_first // CHUNK + 1) * CHUNK, s_len)
    vis_keys = jnp.minimum(((q_first + tq - 1) // CHUNK + 1) * CHUNK, s_len)
    n_full = full_keys // tk
    n_vis = (vis_keys + tk - 1) // tk
    m_scr[...] = jnp.full(m_scr.shape, -jnp.inf, F32)
    acc_scr[...] = jnp.zeros(acc_scr.shape, F32)
    split_diagonal = past % tk == 0 and tq == tk and (tq // 2) % CHUNK == 0

    def block(kb, carry, masked):
        k0 = pl.multiple_of(kb * tk, tk)
        if masked and split_diagonal:
            parts = [(r * (tq // 2), tq // 2, (r + 1) * (tk // 2)) for r in range(2)]
        else:
            parts = [(0, tq, tk)]
        for r0, nr, kext in parts:
            rows = slice(r0, r0 + nr)
            if masked:
                q_pos = (r0 if split_diagonal else q_first + r0) + lax.broadcasted_iota(jnp.int32, (nr, kext), 0)
                k_pos = (0 if split_diagonal else k0) + lax.broadcasted_iota(jnp.int32, (nr, kext), 1)
                visible = k_pos // CHUNK <= q_pos // CHUNK
            ntile, rem = kext // LANES, kext % LANES
            ones_v = jnp.ones((kext, MLA_V_DIM), BF16)
            for h in range(MLA_HEADS):
                sl = slice(h * MLA_QK_PAD, (h + 1) * MLA_QK_PAD)
                vsl = slice(h * MLA_V_DIM, (h + 1) * MLA_V_DIM)
                s = lax.dot_general(q_ref[rows, sl], k_ref[pl.ds(k0, kext), sl], (((1,), (1,)), ((), ())),
                                    preferred_element_type=F32)
                if masked:
                    s = jnp.where(visible, s, -jnp.inf)
                m_prev = m_scr[h, rows]
                m_new = jnp.maximum(m_prev, jnp.max(s, axis=-1, keepdims=True))
                alpha = jnp.exp2(m_prev - m_new)
                ps = [jnp.exp2(s[:, c * LANES:(c + 1) * LANES] - m_new) for c in range(ntile)]
                if rem:
                    ps.append(jnp.exp2(s[:, ntile * LANES:] - m_new[:, :rem]))
                p = jnp.concatenate(ps, axis=1).astype(BF16)
                v_ext = jnp.concatenate([v_ref[pl.ds(k0, kext), vsl], ones_v], axis=1)
                acc_scr[h, rows] = jnp.concatenate([alpha, alpha], axis=1) * acc_scr[h, rows] + _dot(p, v_ext)
                m_scr[h, rows] = m_new
        return carry

    lax.fori_loop(0, n_full, functools.partial(block, masked=False), 0)
    lax.fori_loop(n_full, n_vis, functools.partial(block, masked=True), 0)
    for h in range(MLA_HEADS):
        vsl = slice(h * MLA_V_DIM, (h + 1) * MLA_V_DIM)
        z = z_ref[:, vsl].astype(F32)
        acc = acc_scr[h]
        o_ref[:, vsl] = (acc[:, :MLA_V_DIM] / acc[:, MLA_V_DIM:] * (z * _sigmoid(z))).astype(BF16)


def _attn(q, k, v, pm, bsz, t, s_len, past, tq, tk):
    nq = t // tq
    return pl.pallas_call(
        functools.partial(_attn_body, past=past, tq=tq, tk=tk, s_len=s_len),
        grid=(bsz, nq),
        in_specs=[pl.BlockSpec((tq, MLA_QK_WIDTH), lambda b, iq: (b * nq + iq, 0)),
                  pl.BlockSpec((s_len, MLA_QK_WIDTH), lambda b, iq: (b, 0)),
                  pl.BlockSpec((s_len, MLA_WIDTH), lambda b, iq: (b, 0)),
                  pl.BlockSpec((tq, MLA_WIDTH), lambda b, iq: (b * nq + iq, 1))],
        out_specs=pl.BlockSpec((tq, MLA_WIDTH), lambda b, iq: (b * nq + iq, 0)),
        out_shape=jax.ShapeDtypeStruct((bsz * t, MLA_WIDTH), BF16),
        scratch_shapes=[pltpu.VMEM((MLA_HEADS, tq, LANES), F32), pltpu.VMEM((MLA_HEADS, tq, 2 * MLA_V_DIM), F32)],
        compiler_params=_cparams("parallel", "arbitrary"),
        name="mla_attn",
    )(q, k, v, pm)


def _s5_weights_body(bre_ref, bim_ref, cre_ref, cim_ref, spread_ref, mask_ref, wb_ref, wc_ref):
    def expand(m_ref):
        return _dot(m_ref[...].astype(BF16), spread_ref[...]) * mask_ref[...]

    wb_ref[:, 0:S5_NSTATE] = expand(bre_ref).astype(BF16)
    wb_ref[:, S5_NSTATE:] = expand(bim_ref).astype(BF16)
    wc_ref[0:S5_NSTATE, :] = jnp.transpose(expand(cre_ref)).astype(BF16)
    wc_ref[S5_NSTATE:, :] = jnp.transpose(-expand(cim_ref)).astype(BF16)


def _s5_weights(bbr, bbi, c_re, c_im):
    depth = bbr.shape[0]
    g_row = np.arange(S5_WIDTH) // S5_GROUP_CH
    g_col = np.arange(S5_NSTATE) // S5_STATE
    mask = (g_row[:, None] == g_col[None, :]).astype(np.float32)
    spread = (np.arange(S5_STATE)[:, None] == (np.arange(S5_NSTATE) % S5_STATE)[None, :]).astype(np.float32)
    small = pl.BlockSpec((None, S5_WIDTH, S5_STATE), lambda l: (l, 0, 0))
    return pl.pallas_call(
        _s5_weights_body,
        grid=(depth,),
        in_specs=[small, small, small, small,
                  _const_spec((S5_STATE, S5_NSTATE), 1), _const_spec((S5_WIDTH, S5_NSTATE), 1)],
        out_specs=[pl.BlockSpec((None, S5_WIDTH, 2 * S5_NSTATE), lambda l: (l, 0, 0)),
                   pl.BlockSpec((None, 2 * S5_NSTATE, S5_WIDTH), lambda l: (l, 0, 0))],
        out_shape=[jax.ShapeDtypeStruct((depth, S5_WIDTH, 2 * S5_NSTATE), BF16),
                   jax.ShapeDtypeStruct((depth, 2 * S5_NSTATE, S5_WIDTH), BF16)],
        compiler_params=_cparams("parallel"),
        name="s5_weights",
    )(bbr, bbi, c_re, c_im, jnp.asarray(spread, BF16), jnp.asarray(mask, F32))


def _s5_body(p0_ref, pn_ref, x0r_ref, x0i_ref, lre_ref, lim_ref, wb_ref, wc_ref, d_ref, wglu_ref, bglu_ref, never_ref,
             o_ref, xr_out, xi_out, uz_bt, uz_a, uz_b, uz_c, bu_a, bu_b, bu_c, o_tb, xr_s, xi_s, *, lc, pitch):
    it = pl.program_id(1)
    nb = S5_BATCH_TILE
    ring = ((uz_a, bu_a), (uz_b, bu_b), (uz_c, bu_c))

    def stage_in(blk_ref, uz_tb, bu):
        for b in range(nb):
            for c in range(PS_COLS // LANES):
                uz_bt[c, b * pitch:b * pitch + lc, :] = blk_ref[b, :, c * LANES:(c + 1) * LANES].astype(F32)
        for t in range(lc):
            for c in range(PS_COLS // LANES):
                uz_tb[t * nb:(t + 1) * nb, c * LANES:(c + 1) * LANES] = uz_bt[c, pl.ds(t, nb, stride=pitch), :]
        bu[...] = _dot(uz_tb[:, 0:S5_WIDTH].astype(BF16), wb_ref[...])

    def stage_scan(bu):
        w = S5_NSTATE // S5_SCAN_SLICES
        never = never_ref[...] != 0
        last = None
        for c0 in range(0, S5_NSTATE, w):
            re, im = slice(c0, c0 + w), slice(S5_NSTATE + c0, S5_NSTATE + c0 + w)
            lre = jnp.broadcast_to(lre_ref[:, re], (nb, w))
            lim = jnp.broadcast_to(lim_ref[:, re], (nb, w))
            xr, xi = xr_s[:, re], xi_s[:, re]
            if last is not None:
                xr = jnp.where(never, last, xr)
            for t in range(lc):
                rs = slice(t * nb, (t + 1) * nb)
                xr, xi = lre * xr - lim * xi + bu[rs, re], lre * xi + lim * xr + bu[rs, im]
                bu[rs, re] = xr
                bu[rs, im] = xi
            xr_s[:, re] = xr
            xi_s[:, re] = xi
            last = xr

    def stage_out(uz_tb, xs):
        y = _dot(xs[...].astype(BF16), wc_ref[...]) + d_ref[...] * uz_tb[:, 0:S5_WIDTH]
        g5 = 0.5 * y * (1.0 + jnp.tanh(0.7978845608028654 * (y + 0.044715 * (y * y * y))))
        gate = _sigmoid(_dot(g5.astype(BF16), wglu_ref[...]) + bglu_ref[...])
        z = uz_tb[:, S5_WIDTH:2 * S5_WIDTH]
        o = g5 * gate * (z * _sigmoid(z))
        for c in range(S5_WIDTH // LANES):
            o_tb[c] = o[:, c * LANES:(c + 1) * LANES]
        for b in range(nb):
            for c in range(S5_WIDTH // LANES):
                o_ref[b, :, c * LANES:(c + 1) * LANES] = o_tb[c, pl.ds(b, lc, stride=nb), :].astype(BF16)

    @pl.when(it == 0)
    def _():
        xr_s[...] = x0r_ref[...]
        xi_s[...] = x0i_ref[...]
        stage_in(p0_ref, uz_a, bu_a)
        uz_c[...] = jnp.zeros(uz_c.shape, F32)
        bu_c[...] = jnp.zeros(bu_c.shape, F32)

    for r in range(3):
        @pl.when(it % 3 == r)
        def _(r=r):
            stage_in(pn_ref, *ring[(r + 1) % 3])
            stage_scan(ring[r][1])
            stage_out(*ring[(r + 2) % 3])

    @pl.when(it == pl.num_programs(1) - 2)
    def _():
        xr_out[...] = xr_s[...]
        xi_out[...] = xi_s[...]


def _s5(ps3, x0r, x0i, lre, lim, wb, wc, d, wglu, bglu, layer, bsz, t, lc):
    nb = S5_BATCH_TILE
    nt = t // lc
    pitch = lc + 8
    rows = lc * nb
    return pl.pallas_call(
        functools.partial(_s5_body, lc=lc, pitch=pitch),
        grid=(bsz // nb, nt + 1),
        in_specs=[pl.BlockSpec((nb, lc, PS_COLS), lambda g, i: (g, 0, 0)),
                  pl.BlockSpec((nb, lc, PS_COLS), lambda g, i: (g, jnp.minimum(i + 1, nt - 1), 0)),
                  pl.BlockSpec((nb, S5_NSTATE), lambda g, i: (g, 0)),
                  pl.BlockSpec((nb, S5_NSTATE), lambda g, i: (g, 0)),
                  _layer_spec((1, S5_NSTATE), layer), _layer_spec((1, S5_NSTATE), layer),
                  _layer_spec((S5_WIDTH, 2 * S5_NSTATE), layer), _layer_spec((2 * S5_NSTATE, S5_WIDTH), layer),
                  _layer_spec((1, S5_WIDTH), layer), _layer_spec((S5_WIDTH, S5_WIDTH), layer),
                  _layer_spec((1, S5_WIDTH), layer),
                  _const_spec((1, S5_NSTATE // S5_SCAN_SLICES), 2)],
        out_specs=[pl.BlockSpec((nb, lc, S5_WIDTH), lambda g, i: (g, jnp.maximum(i - 1, 0), 0)),
                   pl.BlockSpec((nb, S5_NSTATE), lambda g, i: (g, 0)),
                   pl.BlockSpec((nb, S5_NSTATE), lambda g, i: (g, 0))],
        out_shape=[jax.ShapeDtypeStruct((bsz, t, S5_WIDTH), BF16),
                   jax.ShapeDtypeStruct((bsz, S5_NSTATE), F32),
                   jax.ShapeDtypeStruct((bsz, S5_NSTATE), F32)],
        scratch_shapes=[pltpu.VMEM((PS_COLS // LANES, nb * pitch, LANES), F32)]
                       + [pltpu.VMEM((rows, PS_COLS), F32)] * 3
                       + [pltpu.VMEM((rows, 2 * S5_NSTATE), F32)] * 3
                       + [pltpu.VMEM((S5_WIDTH // LANES, rows, LANES), F32),
                          pltpu.VMEM((nb, S5_NSTATE), F32), pltpu.VMEM((nb, S5_NSTATE), F32)],
        compiler_params=_cparams("parallel", "arbitrary"),
        name="s5",
    )(ps3, ps3, x0r, x0i, lre, lim, wb, wc, d, wglu, bglu, jnp.zeros((1, S5_NSTATE // S5_SCAN_SLICES), jnp.int32))


def _outproj_rows(x, og_ref, om_ref, os_ref, wo_scr):
    acc = _dot(og_ref[...], wo_scr[0:GLA_WIDTH, :])
    acc += _dot(om_ref[...], wo_scr[GLA_WIDTH:GLA_WIDTH + MLA_WIDTH, :])
    acc += _dot(os_ref[...], wo_scr[GLA_WIDTH + MLA_WIDTH:, :])
    return x + acc


def _outproj_body(x_ref, og_ref, om_ref, os_ref, w_ref, g_ref, o_ref, wo_scr, *, final):
    @pl.when(pl.program_id(0) == 0)
    def _():
        wo_scr[...] = w_ref[...].astype(BF16)

    xn = _outproj_rows(x_ref[...], og_ref, om_ref, os_ref, wo_scr)
    if final:
        ms = jnp.mean(xn * xn, axis=-1, keepdims=True)
        xn = xn * lax.rsqrt(ms + EPS) * g_ref[...]
    o_ref[...] = xn


def _out_in_body(x_ref, og_ref, om_ref, os_ref, wo_ref, g_ref, wt_ref, xo_ref, pg_ref, pm_ref, ps_ref,
                 wo_scr, w_scr):
    @pl.when(pl.program_id(0) == 0)
    def _():
        wo_scr[...] = wo_ref[...].astype(BF16)
        w_scr[...] = wt_ref[...].astype(BF16)

    xn = _outproj_rows(x_ref[...], og_ref, om_ref, os_ref, wo_scr)
    xo_ref[...] = xn
    _inproj_rows(xn, g_ref, w_scr, pg_ref, pm_ref, ps_ref)


def _out_in(x2, og, om, os_, w_out, ln, wt, layer, tm):
    n = x2.shape[0]
    row = lambda c: pl.BlockSpec((tm, c), lambda i: (i, 0))
    return pl.pallas_call(
        _out_in_body,
        grid=(n // tm,),
        in_specs=[row(D_MODEL), row(GLA_WIDTH), row(MLA_WIDTH), row(S5_WIDTH),
                  _layer_spec((D_MODEL, D_MODEL), layer, single_buffer=True),
                  _layer_spec((1, D_MODEL), layer + 1),
                  _layer_spec((IN_OFF['end'], D_MODEL), layer + 1, single_buffer=True)],
        out_specs=[row(D_MODEL), row(PG_COLS), row(PM_COLS), row(PS_COLS)],
        out_shape=[jax.ShapeDtypeStruct((n, D_MODEL), F32), jax.ShapeDtypeStruct((n, PG_COLS), BF16),
                   jax.ShapeDtypeStruct((n, PM_COLS), BF16), jax.ShapeDtypeStruct((n, PS_COLS), BF16)],
        scratch_shapes=[pltpu.VMEM((D_MODEL, D_MODEL), BF16), pltpu.VMEM((IN_OFF['end'], D_MODEL), BF16)],
        compiler_params=_cparams("arbitrary"),
        name="outproj_inproj",
    )(x2, og, om, os_, w_out, ln, wt)


def _outproj(x2, og, om, os_, w, gain, layer, tm, final):
    n = x2.shape[0]
    row = lambda c: pl.BlockSpec((tm, c), lambda i: (i, 0))
    return pl.pallas_call(
        functools.partial(_outproj_body, final=final),
        grid=(n // tm,),
        in_specs=[row(D_MODEL), row(GLA_WIDTH), row(MLA_WIDTH), row(S5_WIDTH),
                  _layer_spec((D_MODEL, D_MODEL), layer, single_buffer=True),
                  _const_spec((1, D_MODEL), 1)],
        out_specs=row(D_MODEL),
        out_shape=jax.ShapeDtypeStruct((n, D_MODEL), F32),
        scratch_shapes=[pltpu.VMEM((D_MODEL, D_MODEL), BF16)],
        compiler_params=_cparams("arbitrary"),
        name="outproj_final" if final else "outproj",
    )(x2, og, om, os_, w, gain)


def _prepare_params(ln_gain, w_in, gla_w_gate, gla_b_gate, gla_norm_gain, mla_q_norm_gain, mla_w_uq,
                    mla_kv_norm_gain, mla_w_ukv, s5_lambda_re, s5_lambda_im, s5_b_re, s5_b_im, s5_c_re, s5_c_im,
                    s5_d, s5_log_dt, s5_w_glu, s5_b_glu, w_out):
    depth = w_in.shape[0]
    w_t = jnp.swapaxes(w_in, 1, 2)
    wg = jnp.pad(gla_w_gate, ((0, 0), (0, GLA_QK - GLA_GATE_RANK), (0, 0))).astype(BF16)
    wq = mla_w_uq.reshape(depth, MLA_Q_RANK, MLA_HEADS, MLA_NOPE_DIM + MLA_ROPE_DIM)
    half = MLA_ROPE_DIM // 2
    wq = jnp.concatenate([wq, -wq[..., MLA_NOPE_DIM + half:], wq[..., MLA_NOPE_DIM:MLA_NOPE_DIM + half]], axis=-1)
    wq = jnp.pad(wq, ((0, 0), (0, 256 - MLA_Q_RANK), (0, 0), (0, 0)))
    wq = wq.reshape(depth, 256, MLA_QK_WIDTH).astype(BF16)
    gq = jnp.pad(mla_q_norm_gain, ((0, 0), (0, 256 - MLA_Q_RANK))).reshape(depth, 1, 256)
    wkv = mla_w_ukv.reshape(depth, MLA_KV_RANK, MLA_HEADS, MLA_NOPE_DIM + MLA_V_DIM)
    wk = jnp.pad(wkv[..., :MLA_NOPE_DIM], ((0, 0), (0, 0), (0, 0), (0, MLA_QK_PAD - MLA_NOPE_DIM)))
    wkv_r = jnp.concatenate([wk.reshape(depth, MLA_KV_RANK, MLA_QK_WIDTH),
                             wkv[..., MLA_NOPE_DIM:].reshape(depth, MLA_KV_RANK, MLA_WIDTH)], axis=2).astype(BF16)
    wka = jnp.pad(jnp.transpose(wkv[..., :MLA_NOPE_DIM], (0, 2, 3, 1)),
                  ((0, 0), (0, 0), (0, MLA_QK_PAD - MLA_NOPE_DIM), (0, 0))).astype(BF16)
    wv = jnp.transpose(wkv[..., MLA_NOPE_DIM:], (0, 2, 1, 3)).astype(BF16)
    dt = jnp.exp(s5_log_dt)[:, :, None]
    mag = jnp.exp(s5_lambda_re * dt)
    lbr, lbi = mag * jnp.cos(s5_lambda_im * dt), mag * jnp.sin(s5_lambda_im * dt)
    den = s5_lambda_re * s5_lambda_re + s5_lambda_im * s5_lambda_im
    qr = ((lbr - 1.0) * s5_lambda_re + lbi * s5_lambda_im) / den
    qi = (lbi * s5_lambda_re - (lbr - 1.0) * s5_lambda_im) / den
    b_re_t, b_im_t = jnp.swapaxes(s5_b_re, 2, 3), jnp.swapaxes(s5_b_im, 2, 3)
    bbr = qr[:, :, None, :] * b_re_t - qi[:, :, None, :] * b_im_t
    bbi = qr[:, :, None, :] * b_im_t + qi[:, :, None, :] * b_re_t
    rows = lambda m: m.reshape(depth, S5_WIDTH, S5_STATE)
    wb, wc = _s5_weights(rows(bbr), rows(bbi), rows(s5_c_re), rows(s5_c_im))
    return dict(
        ln=ln_gain.reshape(depth, 1, D_MODEL), w_in=w_t, wg=wg, bg=gla_b_gate.reshape(depth, 1, GLA_QK),
        gla_gain=jnp.tile(gla_norm_gain, (1, GLA_HEADS)).reshape(depth, 1, GLA_WIDTH),
        gq=gq, wq=wq, gkv=mla_kv_norm_gain.reshape(depth, 1, MLA_KV_RANK), wkv=wkv_r, wka=wka, wv=wv,
        lre=lbr.reshape(depth, 1, S5_NSTATE), lim=lbi.reshape(depth, 1, S5_NSTATE),
        wb=wb, wc=wc, d=s5_d.reshape(depth, 1, S5_WIDTH), wglu=s5_w_glu.astype(BF16),
        bglu=s5_b_glu.reshape(depth, 1, S5_WIDTH), w_out=w_out)


def _rope_tables(past, t, reps):
    half = MLA_ROPE_DIM // 2
    inv = ROPE_BASE ** (-np.arange(half, dtype=np.float64) / half)
    ang = (past + np.arange(t, dtype=np.float64))[:, None] * inv[None, :]
    cos, sin = np.cos(ang), np.sin(ang)
    pad = MLA_QK_PAD - MLA_NOPE_DIM - MLA_ROPE_DIM
    cos_t = np.concatenate([np.ones((t, MLA_NOPE_DIM)), cos, cos, np.zeros((t, pad))], axis=1)
    sin_t = np.concatenate([np.zeros((t, MLA_NOPE_DIM)), sin, sin, np.zeros((t, pad))], axis=1)
    return jnp.asarray(np.tile(cos_t, (reps, 1)), F32), jnp.asarray(np.tile(sin_t, (reps, 1)), F32)


def _trunk(x, gla_state, ckv_cache, kpe_cache, s5_re, s5_im, p, final_gain):
    bsz, t, _ = x.shape
    n = bsz * t
    depth = p['w_in'].shape[0]
    past = 0 if ckv_cache is None else ckv_cache.shape[2]
    s_len = past + t
    tl = _tiles(bsz, t, past)
    cos_t, sin_t = _rope_tables(past, t, max(1, tl['prep'] // t))
    kpet_cache = None if kpe_cache is None else jnp.swapaxes(kpe_cache, 2, 3)
    x2 = x.reshape(n, D_MODEL)
    gain_f = final_gain.reshape(1, D_MODEL)
    gla_o, ckv_o, kpe_o, re_o, im_o = [], [], [], [], []
    pg, pm, ps = _inproj(x2, p['ln'], p['w_in'], 0, tl['row'])
    for l in range(depth):
        s0 = jnp.zeros((bsz, GLA_HEADS, GLA_DK, GLA_DV), F32) if gla_state is None else gla_state[l]
        o_gla, s_new = _gla(pg.reshape(bsz, t, PG_COLS), p['wg'], p['bg'], p['gla_gain'], l, s0, bsz, t, tl)
        o_gla = o_gla.reshape(n, GLA_WIDTH)
        gla_o.append(s_new)
        prep = functools.partial(_mla_prep, pm, cos_t, sin_t, p['gq'], p['wq'], p['gkv'], p['wkv'], l, bsz, t,
                                 tl['prep'])
        if past == 0:
            q, ckv_new, kpet, k_cat, v_all = prep(with_kv=True)
            kpe_o.append(jnp.swapaxes(kpet, 1, 2))
            o_mla = _attn(q, k_cat, v_all, pm, bsz, t, s_len, past, tl['attn_q'], tl['attn_k'])
        else:
            q, ckv_new, kpe128 = prep(with_kv=False)
            kpe_o.append(kpe128[:, MLA_NOPE_DIM:MLA_NOPE_DIM + MLA_ROPE_DIM].reshape(bsz, t, MLA_ROPE_DIM))
            o_mla = _attn_cached(q, pm, ckv_new, kpe128, ckv_cache, kpet_cache, p['wka'], p['wv'], l, bsz, t,
                                 tl['cached_seqs'])
        ckv_o.append(ckv_new.reshape(bsz, t, MLA_KV_RANK))
        x0r = jnp.zeros((bsz, S5_NSTATE), F32) if s5_re is None else s5_re[l].reshape(bsz, S5_NSTATE)
        x0i = jnp.zeros((bsz, S5_NSTATE), F32) if s5_im is None else s5_im[l].reshape(bsz, S5_NSTATE)
        o_s5, xr, xi = _s5(ps.reshape(bsz, t, PS_COLS), x0r, x0i, p['lre'], p['lim'], p['wb'], p['wc'],
                           p['d'], p['wglu'], p['bglu'], l, bsz, t, tl['s5_rows'])
        re_o.append(xr.reshape(bsz, S5_GROUPS, S5_STATE))
        im_o.append(xi.reshape(bsz, S5_GROUPS, S5_STATE))
        o_s5 = o_s5.reshape(n, S5_WIDTH)
        if l < depth - 1:
            x2, pg, pm, ps = _out_in(x2, o_gla, o_mla, o_s5, p['w_out'], p['ln'], p['w_in'], l, tl['row'])
        else:
            x2 = _outproj(x2, o_gla, o_mla, o_s5, p['w_out'], gain_f, l, tl['out_row'], final=True)
    return (x2.reshape(bsz, t, D_MODEL), jnp.stack(gla_o), jnp.stack(ckv_o), jnp.stack(kpe_o),
            jnp.stack(re_o), jnp.stack(im_o))


def kernel(x_prompt, x_sample, state_gla, cache_mla_ckv, cache_mla_kpe, state_s5_re, state_s5_im, ln_gain, w_in, gla_w_gate, gla_b_gate, gla_norm_gain, mla_q_norm_gain, mla_w_uq, mla_kv_norm_gain, mla_w_ukv, s5_lambda_re, s5_lambda_im, s5_b_re, s5_b_im, s5_c_re, s5_c_im, s5_d, s5_log_dt, s5_w_glu, s5_b_glu, w_out, final_gain):
    p = _prepare_params(ln_gain, w_in, gla_w_gate, gla_b_gate, gla_norm_gain, mla_q_norm_gain, mla_w_uq,
                        mla_kv_norm_gain, mla_w_ukv, s5_lambda_re, s5_lambda_im, s5_b_re, s5_b_im,
                        s5_c_re, s5_c_im, s5_d, s5_log_dt, s5_w_glu, s5_b_glu, w_out)
    y_p, gla_p, ckv_p, kpe_p, re_p, im_p = _trunk(x_prompt, None, None, None, None, None, p, final_gain)
    y_s, gla_s, ckv_s, kpe_s, re_s, im_s = _trunk(x_sample, state_gla, cache_mla_ckv, cache_mla_kpe,
                                                  state_s5_re, state_s5_im, p, final_gain)
    return (y_p, y_s, gla_p, ckv_p, kpe_p, re_p, im_p, gla_s, ckv_s, kpe_s, re_s, im_s)
```

```python
import functools
import math

import numpy as np
import jax
import jax.numpy as jnp
from jax import lax
from jax.experimental import pallas as pl
from jax.experimental.pallas import tpu as pltpu

F32 = jnp.float32
BF16 = jnp.bfloat16

LANES = 128
D_MODEL = 1024
CHUNK = 64
EPS = 1e-6
GLA_HEADS = 4
GLA_DV = 64
GLA_DK = 32
GLA_WIDTH = GLA_HEADS * GLA_DV
GLA_QK = GLA_HEADS * GLA_DK
GLA_GATE_RANK = 16
GLA_GATE_TAU = 16.0
GLA_SUB = 16
GLA_PLAIN_MAX_DECAY = 60.0
MLA_HEADS = 4
MLA_NOPE_DIM = 64
MLA_ROPE_DIM = 32
MLA_V_DIM = 128
MLA_Q_RANK = 192
MLA_KV_RANK = 128
MLA_WIDTH = MLA_HEADS * MLA_V_DIM
MLA_QK_PAD = 128
MLA_QK_WIDTH = MLA_HEADS * MLA_QK_PAD
ROPE_BASE = 10000.0
S5_GROUPS = 16
S5_GROUP_CH = 16
S5_STATE = 64
S5_WIDTH = S5_GROUPS * S5_GROUP_CH
S5_NSTATE = S5_GROUPS * S5_STATE
S5_BATCH_TILE = 8
S5_SCAN_SLICES = 4

PG_COLS = 896
PM_COLS = 1024
PS_COLS = 512
_IN_SEGS = (('g_q', GLA_QK), ('g_k', GLA_QK), ('g_v', GLA_WIDTH), ('g_lr', GLA_GATE_RANK), ('g_z', GLA_WIDTH),
            ('m_cq', MLA_Q_RANK), ('m_ckv', MLA_KV_RANK), ('m_kr', MLA_ROPE_DIM), ('m_z', MLA_WIDTH),
            ('s_u', S5_WIDTH), ('s_z', S5_WIDTH), ('end', 0))
PG_Q, PG_K, PG_V, PG_Z, PG_LR = slice(0, 128), slice(128, 256), slice(256, 512), slice(512, 768), slice(768, 896)
PG_QKV = slice(PG_Q.start, PG_V.stop)
PM_CQ, PM_CKV, PM_KR, PM_Z = slice(0, 256), slice(256, 384), slice(384, 512), slice(512, 1024)
IN_OFF = dict(zip([n for n, _ in _IN_SEGS], np.cumsum([0] + [w for _, w in _IN_SEGS[:-1]]).tolist()))

VMEM_LIMIT_BYTES = 48 * 1024 * 1024


def _tiles(bsz, t, past):
    n = bsz * t
    s_len = past + t
    return dict(
        row=min(512, n),
        out_row=min(1024, n),
        prep=min(512, n) if t < 512 else 512,
        cached_seqs=math.gcd(bsz, max(1, 64 // t)),
        gla_rows=min(512, t), gla_chunk=min(CHUNK, t),
        gla_seqs=math.gcd(bsz, max(2, 128 // t)),
        attn_q=min(512, t), attn_k=min(512, s_len),
        s5_rows=min(64, t))


def _cparams(*sem):
    return pltpu.CompilerParams(dimension_semantics=sem, vmem_limit_bytes=VMEM_LIMIT_BYTES)


def _sigmoid(x):
    return 0.5 * (1.0 + jnp.tanh(0.5 * x))


def _dot(a, b):
    return jnp.dot(a, b, preferred_element_type=F32)


def _dot_t(a, b):
    return lax.dot_general(a, b, (((0,), (0,)), ((), ())), preferred_element_type=F32)


def _const_spec(shape, ngrid):
    zeros = (0,) * len(shape)
    return pl.BlockSpec(shape, lambda *_: zeros)


def _layer_spec(shape, layer, single_buffer=False):
    zeros = (0,) * len(shape)
    mode = dict(pipeline_mode=pl.Buffered(1)) if single_buffer else {}
    return pl.BlockSpec((None,) + tuple(shape), lambda *_: (layer,) + zeros, **mode)


def _inproj_rows(x, g_ref, w_scr, og_ref, om_ref, os_ref):
    ms = jnp.mean(x * x, axis=-1, keepdims=True)
    h = (x * lax.rsqrt(ms + EPS) * g_ref[...]).astype(BF16)

    def seg(a, b):
        return lax.dot_general(h, w_scr[a:b, :], (((1,), (1,)), ((), ())), preferred_element_type=F32)

    lane = lax.broadcasted_iota(jnp.int32, (x.shape[0], LANES), 1)
    lane2 = lax.broadcasted_iota(jnp.int32, (x.shape[0], 2 * LANES), 1)
    c = IN_OFF
    og_ref[:, PG_QKV] = seg(c['g_q'], c['g_lr']).astype(BF16)
    og_ref[:, PG_Z] = seg(c['g_z'], c['m_cq']).astype(BF16)
    og_ref[:, PG_LR] = jnp.where(lane < GLA_GATE_RANK, seg(c['g_lr'], c['g_lr'] + LANES), 0.0).astype(BF16)
    om_ref[:, PM_CQ] = jnp.where(lane2 < MLA_Q_RANK, seg(c['m_cq'], c['m_cq'] + 2 * LANES), 0.0).astype(BF16)
    ckv_kr = seg(c['m_ckv'], c['m_ckv'] + 2 * LANES)
    om_ref[:, PM_CKV] = ckv_kr[:, 0:LANES].astype(BF16)
    kr = pltpu.roll(ckv_kr[:, LANES:], MLA_NOPE_DIM, 1)
    om_ref[:, PM_KR] = jnp.where((lane >= MLA_NOPE_DIM) & (lane < MLA_NOPE_DIM + MLA_ROPE_DIM), kr, 0.0).astype(BF16)
    om_ref[:, PM_Z] = seg(c['m_z'], c['s_u']).astype(BF16)
    os_ref[...] = seg(c['s_u'], c['end']).astype(BF16)


def _inproj_body(x_ref, g_ref, wt_ref, og_ref, om_ref, os_ref, w_scr):
    @pl.when(pl.program_id(0) == 0)
    def _():
        w_scr[...] = wt_ref[...].astype(BF16)

    _inproj_rows(x_ref[...], g_ref, w_scr, og_ref, om_ref, os_ref)


def _inproj(x2, gain, wt, layer, tm):
    n = x2.shape[0]
    return pl.pallas_call(
        _inproj_body,
        grid=(n // tm,),
        in_specs=[pl.BlockSpec((tm, D_MODEL), lambda i: (i, 0)),
                  _layer_spec((1, D_MODEL), layer),
                  _layer_spec((IN_OFF['end'], D_MODEL), layer, single_buffer=True)],
        scratch_shapes=[pltpu.VMEM((IN_OFF['end'], D_MODEL), BF16)],
        out_specs=[pl.BlockSpec((tm, PG_COLS), lambda i: (i, 0)),
                   pl.BlockSpec((tm, PM_COLS), lambda i: (i, 0)),
                   pl.BlockSpec((tm, PS_COLS), lambda i: (i, 0))],
        out_shape=[jax.ShapeDtypeStruct((n, PG_COLS), BF16),
                   jax.ShapeDtypeStruct((n, PM_COLS), BF16),
                   jax.ShapeDtypeStruct((n, PS_COLS), BF16)],
        compiler_params=_cparams("arbitrary"),
        name="inproj",
    )(x2, gain, wt)


def _gla_body(p_ref, wg_ref, bg_ref, gain_ref, ones_k_ref, ones_v_ref, bd_ref, s0_ref,
              o_ref, sout_ref, q_scr, k_scr, v_scr, b_scr, s_scr, qe_scr, qs_scr, ke_scr, vb_scr, o_scr,
              *, chunk, nch, nseq):
    it = pl.program_id(1)
    sub = min(GLA_SUB, chunk)
    nsub = chunk // sub

    @pl.when(it == 0)
    def _():
        s_scr[...] = jnp.zeros(s_scr.shape, F32)
        for sq in range(nseq):
            for h in range(GLA_HEADS):
                s_scr[sq, h * GLA_DK:(h + 1) * GLA_DK, h * GLA_DV:(h + 1) * GLA_DV] = s0_ref[sq, h]

    def cols(lanes):
        parts = [p_ref[sq, :, lanes] for sq in range(nseq)]
        return parts[0] if nseq == 1 else jnp.concatenate(parts, axis=0)

    q_scr[...] = cols(PG_Q).astype(F32) * (GLA_DK ** -0.5)
    k_scr[...] = cols(PG_K).astype(F32)
    v_scr[...] = cols(PG_V).astype(F32)
    logit = _dot(cols(PG_LR), wg_ref[...]) + bg_ref[...]
    log_a = (jnp.minimum(logit, 0.0) - jnp.log(1.0 + jnp.exp(-jnp.abs(logit)))) * (1.0 / GLA_GATE_TAU)
    row_in_chunk = lax.broadcasted_iota(jnp.int32, log_a.shape, 0) % chunk
    b_all = log_a
    shift = 1
    while shift < chunk:
        b_all = b_all + jnp.where(row_in_chunk >= shift, pltpu.roll(b_all, shift, 0), 0.0)
        shift *= 2
    b_scr[...] = b_all

    row = lax.broadcasted_iota(jnp.int32, (chunk, GLA_QK), 0)
    row_in_sub = lax.broadcasted_iota(jnp.int32, (sub, GLA_QK), 0)

    def decay_columns(b_end):
        col = jnp.transpose(jnp.broadcast_to(jnp.exp(b_end), (GLA_QK, GLA_QK)))
        return jnp.concatenate([col, col], axis=1)

    def load_chunk(sq, c):
        r0 = pl.multiple_of((sq * nch + c) * chunk, chunk)
        return (r0, q_scr[pl.ds(r0, chunk), :], k_scr[pl.ds(r0, chunk), :], v_scr[pl.ds(r0, chunk), :],
                b_scr[pl.ds(r0, chunk), :], b_scr[pl.ds(r0 + chunk - 1, 1), :], s_scr[sq])

    def finish_chunk(sq, r0, o, s_prev, a_state, b_end):
        ms = _dot((o * o).astype(BF16), ones_v_ref[...]) * (1.0 / GLA_DV)
        o_n = o * lax.rsqrt(ms + EPS) * gain_ref[...]
        t0 = r0 - sq * nch * chunk
        z = p_ref[sq, pl.ds(t0, chunk), PG_Z].astype(F32)
        o_ref[sq, pl.ds(t0, chunk), :] = (o_n * (z * _sigmoid(z))).astype(BF16)
        s_scr[sq] = s_prev * decay_columns(b_end) + a_state

    def robust_chunk(c, carry, sq):
        r0, qc, kc, vc, bc, b_end, s_prev = load_chunk(sq, c)
        xs = [qc * jnp.exp(bc)]
        ks = []
        for sj in range(nsub - 1):
            e_j = b_scr[pl.ds(r0 + (sj + 1) * sub - 1, 1), :]
            later = row >= (sj + 1) * sub
            xs.append(jnp.where(later, qc * jnp.exp(jnp.where(later, bc - e_j, 0.0)), 0.0))
            own = (row >= sj * sub) & (row < (sj + 1) * sub)
            ks.append(jnp.where(own, kc * jnp.exp(jnp.where(own, e_j - bc, 0.0)), 0.0))
        ks.append(kc * jnp.exp(b_end - bc))
        k_all = jnp.concatenate(ks, axis=1).astype(BF16)
        a_all = _dot_t(k_all, vc.astype(BF16)) * bd_ref[...]
        w = jnp.concatenate([s_prev, a_all[:(nsub - 1) * GLA_QK]], axis=0).astype(BF16) if nsub > 1 \
            else s_prev.astype(BF16)
        o_off = _dot(jnp.concatenate(xs, axis=1).astype(BF16), w)
        rows = []
        for si in range(nsub):
            q_i = qc[si * sub:(si + 1) * sub]
            b_i = bc[si * sub:(si + 1) * sub]
            es = []
            for j in range(sub):
                r = r0 + si * sub + j
                b_j = b_scr[pl.ds(r, 1), :]
                k_j = k_scr[pl.ds(r, 1), :]
                valid = row_in_sub >= j
                es.append(q_i * k_j * jnp.exp(jnp.where(valid, b_i - b_j, -jnp.inf)))
            e_all = jnp.concatenate(es, axis=0)
            e_hi = e_all.astype(BF16)
            e_lo = (e_all - e_hi.astype(F32)).astype(BF16)
            p_all = _dot(e_hi, ones_k_ref[...]) + _dot(e_lo, ones_k_ref[...])
            acc = o_off[si * sub:(si + 1) * sub]
            for j in range(sub):
                v_j = v_scr[pl.ds(r0 + si * sub + j, 1), :]
                acc = acc + p_all[j * sub:(j + 1) * sub] * v_j
            rows.append(acc)
        o = rows[0] if nsub == 1 else jnp.concatenate(rows, axis=0)
        finish_chunk(sq, r0, o, s_prev, a_all[(nsub - 1) * GLA_QK:], b_end)
        return carry

    lane_head_v = lax.broadcasted_iota(jnp.int32, (chunk, GLA_WIDTH), 1) // GLA_DV
    causal = (lax.broadcasted_iota(jnp.int32, (GLA_HEADS * chunk, chunk), 0) % chunk
              >= lax.broadcasted_iota(jnp.int32, (GLA_HEADS * chunk, chunk), 1))

    def plain_block():
        tt = nseq * nch * chunk
        b_all = b_scr[...]
        q_all = q_scr[...]
        k_all = k_scr[...]
        qe = q_all * jnp.exp(b_all)
        lane_head = lax.broadcasted_iota(jnp.int32, (tt, GLA_QK), 1) // GLA_DK
        qe_scr[...] = qe.astype(BF16)
        for h in range(GLA_HEADS):
            qs_scr[h] = jnp.where(lane_head == h, qe, 0.0).astype(BF16)
        ke_scr[...] = (k_all * jnp.exp(-b_all)).astype(BF16)
        vb_scr[...] = cols(PG_V)
        states = [s_scr[sq] for sq in range(nseq)]
        for c, sq in [(c, sq) for c in range(nch) for sq in range(nseq)]:
            r0 = (sq * nch + c) * chunk
            rs = slice(r0, r0 + chunk)
            s_cur = states[sq]
            qs = jnp.concatenate([qs_scr[h, rs, :] for h in range(GLA_HEADS)], axis=0)
            s = lax.dot_general(qs, ke_scr[rs, :], (((1,), (1,)), ((), ())), preferred_element_type=F32)
            s = jnp.where(causal, s, 0.0).astype(BF16)
            r = _dot(s, vb_scr[rs, :])
            o = _dot(qe_scr[rs, :], s_cur.astype(BF16))
            for h in range(GLA_HEADS):
                o = o + jnp.where(lane_head_v == h, r[h * chunk:(h + 1) * chunk], 0.0)
            o_scr[rs, :] = o
            b_end = b_scr[r0 + chunk - 1:r0 + chunk, :]
            k_end = (k_scr[rs, :] * jnp.exp(b_end - b_scr[rs, :])).astype(BF16)
            a_state = _dot_t(k_end, vb_scr[rs, :]) * bd_ref[0:GLA_QK, :]
            states[sq] = s_cur * decay_columns(b_end) + a_state
        for sq in range(nseq):
            s_scr[sq] = states[sq]
        o = o_scr[...]
        ms = _dot((o * o).astype(BF16), ones_v_ref[...]) * (1.0 / GLA_DV)
        o_n = o * lax.rsqrt(ms + EPS) * gain_ref[...]
        z = cols(PG_Z).astype(F32)
        o_all = (o_n * (z * _sigmoid(z))).astype(BF16)
        for sq in range(nseq):
            o_ref[sq] = o_all[sq * nch * chunk:(sq + 1) * nch * chunk]

    in_range = jnp.max(-b_scr[...]) < GLA_PLAIN_MAX_DECAY

    @pl.when(in_range)
    def _():
        plain_block()

    @pl.when(jnp.logical_not(in_range))
    def _():
        for sq in range(nseq):
            lax.fori_loop(0, nch, functools.partial(robust_chunk, sq=sq), 0)

    @pl.when(it == pl.num_programs(1) - 1)
    def _():
        for sq in range(nseq):
            for h in range(GLA_HEADS):
                sout_ref[sq, h] = s_scr[sq, h * GLA_DK:(h + 1) * GLA_DK, h * GLA_DV:(h + 1) * GLA_DV]


def _gla_consts(chunk):
    nsub = chunk // min(GLA_SUB, chunk)
    hk = np.arange(GLA_QK) // GLA_DK
    hv = np.arange(GLA_WIDTH) // GLA_DV
    same_kv = (hk[:, None] == hv[None, :]).astype(np.float32)
    same_vv = (hv[:, None] == hv[None, :]).astype(np.float32)
    return (jnp.asarray(same_kv, BF16), jnp.asarray(same_vv, BF16),
            jnp.asarray(np.tile(same_kv, (nsub, 1)), F32))


def _gla(pg, wg, bg, gain, layer, s0_bd, bsz, t, tl):
    chunk, tt, nseq = tl['gla_chunk'], tl['gla_rows'], tl['gla_seqs']
    nch = tt // chunk
    nt = t // tt
    rows = nseq * tt
    nsub = chunk // min(GLA_SUB, chunk)
    ones_k, ones_v, bd = _gla_consts(chunk)
    const = lambda shape: _const_spec(shape, 2)
    return pl.pallas_call(
        functools.partial(_gla_body, chunk=chunk, nch=nch, nseq=nseq),
        grid=(bsz // nseq, nt),
        in_specs=[pl.BlockSpec((nseq, tt, PG_COLS), lambda b, i: (b, i, 0)),
                  _layer_spec((GLA_QK, GLA_QK), layer), _layer_spec((1, GLA_QK), layer),
                  _layer_spec((1, GLA_WIDTH), layer),
                  const((GLA_QK, GLA_WIDTH)), const((GLA_WIDTH, GLA_WIDTH)),
                  const((nsub * GLA_QK, GLA_WIDTH)),
                  pl.BlockSpec((nseq, GLA_HEADS, GLA_DK, GLA_DV), lambda b, i: (b, 0, 0, 0))],
        out_specs=[pl.BlockSpec((nseq, tt, GLA_WIDTH), lambda b, i: (b, i, 0)),
                   pl.BlockSpec((nseq, GLA_HEADS, GLA_DK, GLA_DV), lambda b, i: (b, 0, 0, 0))],
        out_shape=[jax.ShapeDtypeStruct((bsz, t, GLA_WIDTH), BF16),
                   jax.ShapeDtypeStruct((bsz, GLA_HEADS, GLA_DK, GLA_DV), F32)],
        scratch_shapes=[pltpu.VMEM((rows, GLA_QK), F32), pltpu.VMEM((rows, GLA_QK), F32),
                        pltpu.VMEM((rows, GLA_WIDTH), F32), pltpu.VMEM((rows, GLA_QK), F32),
                        pltpu.VMEM((nseq, GLA_QK, GLA_WIDTH), F32),
                        pltpu.VMEM((rows, GLA_QK), BF16), pltpu.VMEM((GLA_HEADS, rows, GLA_QK), BF16),
                        pltpu.VMEM((rows, GLA_QK), BF16),
                        pltpu.VMEM((rows, GLA_WIDTH), BF16), pltpu.VMEM((rows, GLA_WIDTH), F32)],
        compiler_params=_cparams("parallel", "arbitrary"),
        name="gla",
    )(pg, wg, bg, gain, ones_k, ones_v, bd, s0_bd)


def _rope128(x, cos_t, sin_t):
    lane = lax.broadcasted_iota(jnp.int32, x.shape, 1)
    first_half = (lane >= MLA_NOPE_DIM) & (lane < MLA_NOPE_DIM + MLA_ROPE_DIM // 2)
    rot = jnp.where(first_half, -pltpu.roll(x, LANES - MLA_ROPE_DIM // 2, 1), pltpu.roll(x, MLA_ROPE_DIM // 2, 1))
    return x * cos_t + rot * sin_t


def _mla_prep_body(p_ref, cos_ref, sin_ref, gq_ref, wq_ref, gkv_ref, *rest, with_kv):
    if with_kv:
        wkv_ref, q_ref, ckv_ref, kpet_ref, k_ref, v_ref = rest
    else:
        q_ref, ckv_ref, kpe_ref = rest
    cos_t = cos_ref[...]
    sin_t = sin_ref[...]
    ones = jnp.ones((2 * LANES, LANES), BF16)
    cq = p_ref[:, PM_CQ].astype(F32)
    ms = _dot((cq * cq).astype(BF16), ones) * (1.0 / MLA_Q_RANK)
    inv = lax.rsqrt(ms + EPS)
    cqn = (cq * jnp.concatenate([inv, inv], axis=1) * gq_ref[...]).astype(BF16)
    qh = _dot(cqn, wq_ref[...])
    scale = (MLA_NOPE_DIM + MLA_ROPE_DIM) ** -0.5 * math.log2(math.e)
    for h in range(MLA_HEADS):
        x = qh[:, h * MLA_QK_PAD:(h + 1) * MLA_QK_PAD]
        roped = x * cos_t + pltpu.roll(x, LANES - MLA_ROPE_DIM, 1) * sin_t
        q_ref[:, h * MLA_QK_PAD:(h + 1) * MLA_QK_PAD] = (roped * scale).astype(BF16)
    ckv = p_ref[:, PM_CKV].astype(F32)
    ms = _dot((ckv * ckv).astype(BF16), ones[0:LANES]) * (1.0 / MLA_KV_RANK)
    ckv_n = ckv * lax.rsqrt(ms + EPS) * gkv_ref[...]
    ckv_ref[...] = ckv_n
    kpe128 = _rope128(p_ref[:, PM_KR].astype(F32), cos_t, sin_t)
    if with_kv:
        kpet_ref[...] = jnp.transpose(kpe128)[MLA_NOPE_DIM:MLA_NOPE_DIM + MLA_ROPE_DIM, :]
        kv = _dot(ckv_n.astype(BF16), wkv_ref[...])
        for h in range(MLA_HEADS):
            sl = slice(h * MLA_QK_PAD, (h + 1) * MLA_QK_PAD)
            k_ref[:, sl] = (kv[:, sl] + kpe128).astype(BF16)
        v_ref[...] = kv[:, MLA_QK_WIDTH:].astype(BF16)
    else:
        kpe_ref[...] = kpe128


def _mla_prep(pm, cos_t, sin_t, gq, wq, gkv, wkv, layer, bsz, t, tm, with_kv):
    n = bsz * t
    ntab = max(1, t // tm)
    row = lambda c: pl.BlockSpec((tm, c), lambda i: (i, 0))
    in_specs = [row(512),
                pl.BlockSpec((tm, LANES), lambda i: (i % ntab, 0)),
                pl.BlockSpec((tm, LANES), lambda i: (i % ntab, 0)),
                _layer_spec((1, 256), layer), _layer_spec((256, MLA_QK_WIDTH), layer),
                _layer_spec((1, MLA_KV_RANK), layer)]
    out_specs = [row(MLA_QK_WIDTH), row(MLA_KV_RANK)]
    out_shape = [jax.ShapeDtypeStruct((n, MLA_QK_WIDTH), BF16), jax.ShapeDtypeStruct((n, MLA_KV_RANK), F32)]
    args = [pm, cos_t, sin_t, gq, wq, gkv]
    if with_kv:
        in_specs.append(_layer_spec((MLA_KV_RANK, MLA_QK_WIDTH + MLA_WIDTH), layer))
        args.append(wkv)
        out_specs += [pl.BlockSpec((None, MLA_ROPE_DIM, tm), lambda i: (i // ntab, 0, i % ntab)),
                      row(MLA_QK_WIDTH), row(MLA_WIDTH)]
        out_shape += [jax.ShapeDtypeStruct((bsz, MLA_ROPE_DIM, t), F32),
                      jax.ShapeDtypeStruct((n, MLA_QK_WIDTH), BF16), jax.ShapeDtypeStruct((n, MLA_WIDTH), BF16)]
    else:
        out_specs.append(row(LANES))
        out_shape.append(jax.ShapeDtypeStruct((n, LANES), F32))
    return pl.pallas_call(
        functools.partial(_mla_prep_body, with_kv=with_kv),
        grid=(n // tm,),
        in_specs=in_specs, out_specs=out_specs, out_shape=out_shape,
        compiler_params=_cparams("parallel"),
        name="mla_prep_kv" if with_kv else "mla_prep",
    )(*args)


def _attn_cached_body(q_ref, z_ref, ckv_new_ref, kpe_new_ref, ckv_past_ref, kpet_past_ref, wka_ref, wv_ref,
                      o_ref, kpet_scr, *, past, t, nseq):
    last = (((1,), (1,)), ((), ()))
    hsl = [slice(h * MLA_QK_PAD, (h + 1) * MLA_QK_PAD) for h in range(MLA_HEADS)]
    q_lat_h = [_dot(q_ref[:, hsl[h]], wka_ref[h]).astype(BF16) for h in range(MLA_HEADS)]
    kpet_scr[...] = jnp.zeros(kpet_scr.shape, BF16)
    o_lat = []
    for sq in range(nseq):
        rs = slice(sq * t, (sq + 1) * t)
        q_rows = jnp.concatenate([q_ref[rs, hsl[h]] for h in range(MLA_HEADS)], axis=0)
        q_lat = jnp.concatenate([q_lat_h[h][rs] for h in range(MLA_HEADS)], axis=0)
        c_past = ckv_past_ref[sq].astype(BF16)
        c_new = ckv_new_ref[rs, :].astype(BF16)
        kpet_scr[sq, MLA_NOPE_DIM:MLA_NOPE_DIM + MLA_ROPE_DIM, :] = kpet_past_ref[sq].astype(BF16)
        s_past = (lax.dot_general(q_lat, c_past, last, preferred_element_type=F32)
                  + _dot(q_rows, kpet_scr[sq]))
        s_new = (lax.dot_general(q_lat, c_new, last, preferred_element_type=F32)
                 + lax.dot_general(q_rows, kpe_new_ref[rs, :].astype(BF16), last, preferred_element_type=F32))
        if past // CHUNK != (past + t - 1) // CHUNK:
            q_chunk = (past + lax.broadcasted_iota(jnp.int32, s_past.shape, 0) % t) // CHUNK
            s_past = jnp.where(lax.broadcasted_iota(jnp.int32, s_past.shape, 1) // CHUNK <= q_chunk, s_past, -jnp.inf)
            q_chunk = (past + lax.broadcasted_iota(jnp.int32, s_new.shape, 0) % t) // CHUNK
            s_new = jnp.where((past + lax.broadcasted_iota(jnp.int32, s_new.shape, 1)) // CHUNK <= q_chunk,
                              s_new, -jnp.inf)
        m = jnp.maximum(jnp.max(s_past, axis=-1, keepdims=True), jnp.max(s_new, axis=-1, keepdims=True))
        p_past = jnp.exp2(s_past - m)
        p_new = jnp.exp2(s_new - m)
        l = jnp.sum(p_past, axis=-1, keepdims=True) + jnp.sum(p_new, axis=-1, keepdims=True)
        o_lat.append(((_dot(p_past.astype(BF16), c_past) + _dot(p_new.astype(BF16), c_new)) / l).astype(BF16))
    for h in range(MLA_HEADS):
        vsl = slice(h * MLA_V_DIM, (h + 1) * MLA_V_DIM)
        o_h = jnp.concatenate([o_lat[sq][h * t:(h + 1) * t] for sq in range(nseq)], axis=0)
        z = z_ref[:, vsl].astype(F32)
        o_ref[:, vsl] = (_dot(o_h, wv_ref[h]) * (z * _sigmoid(z))).astype(BF16)


def _attn_cached(q, pm, ckv_new, kpe128, ckv_past, kpet_past, wka, wv, layer, bsz, t, nseq):
    past = ckv_past.shape[2]
    rows = nseq * t
    return pl.pallas_call(
        functools.partial(_attn_cached_body, past=past, t=t, nseq=nseq),
        grid=(bsz // nseq,),
        in_specs=[pl.BlockSpec((rows, MLA_QK_WIDTH), lambda b: (b, 0)),
                  pl.BlockSpec((rows, MLA_WIDTH), lambda b: (b, 1)),
                  pl.BlockSpec((rows, MLA_KV_RANK), lambda b: (b, 0)),
                  pl.BlockSpec((rows, LANES), lambda b: (b, 0)),
                  pl.BlockSpec((None, nseq, past, MLA_KV_RANK), lambda b: (layer, b, 0, 0)),
                  pl.BlockSpec((None, nseq, MLA_ROPE_DIM, past), lambda b: (layer, b, 0, 0)),
                  _layer_spec((MLA_HEADS, MLA_QK_PAD, MLA_KV_RANK), layer),
                  _layer_spec((MLA_HEADS, MLA_KV_RANK, MLA_V_DIM), layer)],
        out_specs=pl.BlockSpec((rows, MLA_WIDTH), lambda b: (b, 0)),
        out_shape=jax.ShapeDtypeStruct((bsz * t, MLA_WIDTH), BF16),
        scratch_shapes=[pltpu.VMEM((nseq, MLA_QK_PAD, past), BF16)],
        compiler_params=_cparams("parallel"),
        name="mla_attn_cached",
    )(q, pm, ckv_new, kpe128, ckv_past, kpet_past, wka, wv)


def _attn_body(q_ref, k_ref, v_ref, z_ref, o_ref, m_scr, acc_scr, *, past, tq, tk, s_len):
    iq = pl.program_id(1)
    q_first = past + iq * tq
    full_keys = jnp.minimum((q_first // CHUNK + 1) * CHUNK, s_len)
    vis_keys = jnp.minimum(((q_first + tq - 1) // CHUNK + 1) * CHUNK, s_len)
    n_full = full_keys // tk
    n_vis = (vis_keys + tk - 1) // tk
    m_scr[...] = jnp.full(m_scr.shape, -jnp.inf, F32)
    acc_scr[...] = jnp.zeros(acc_scr.shape, F32)
    split_diagonal = past % tk == 0 and tq == tk and (tq // 2) % CHUNK == 0

    def block(kb, carry, masked):
        k0 = pl.multiple_of(kb * tk, tk)
        if masked and split_diagonal:
            parts = [(r * (tq // 2), tq // 2, (r + 1) * (tk // 2)) for r in range(2)]
        else:
            parts = [(0, tq, tk)]
        for r0, nr, kext in parts:
            rows = slice(r0, r0 + nr)
            if masked:
                q_pos = (r0 if split_diagonal else q_first + r0) + lax.broadcasted_iota(jnp.int32, (nr, kext), 0)
                k_pos = (0 if split_diagonal else k0) + lax.broadcasted_iota(jnp.int32, (nr, kext), 1)
                visible = k_pos // CHUNK <= q_pos // CHUNK
            ntile, rem = kext // LANES, kext % LANES
            ones_v = jnp.ones((kext, MLA_V_DIM), BF16)
            for h in range(MLA_HEADS):
                sl = slice(h * MLA_QK_PAD, (h + 1) * MLA_QK_PAD)
                vsl = slice(h * MLA_V_DIM, (h + 1) * MLA_V_DIM)
                s = lax.dot_general(q_ref[rows, sl], k_ref[pl.ds(k0, kext), sl], (((1,), (1,)), ((), ())),
                                    preferred_element_type=F32)
                if masked:
                    s = jnp.where(visible, s, -jnp.inf)
                m_prev = m_scr[h, rows]
                m_new = jnp.maximum(m_prev, jnp.max(s, axis=-1, keepdims=True))
                alpha = jnp.exp2(m_prev - m_new)
                ps = [jnp.exp2(s[:, c * LANES:(c + 1) * LANES] - m_new) for c in range(ntile)]
                if rem:
                    ps.append(jnp.exp2(s[:, ntile * LANES:] - m_new[:, :rem]))
                p = jnp.concatenate(ps, axis=1).astype(BF16)
                v_ext = jnp.concatenate([v_ref[pl.ds(k0, kext), vsl], ones_v], axis=1)
                acc_scr[h, rows] = jnp.concatenate([alpha, alpha], axis=1) * acc_scr[h, rows] + _dot(p, v_ext)
                m_scr[h, rows] = m_new
        return carry

    lax.fori_loop(0, n_full, functools.partial(block, masked=False), 0)
    lax.fori_loop(n_full, n_vis, functools.partial(block, masked=True), 0)
    for h in range(MLA_HEADS):
        vsl = slice(h * MLA_V_DIM, (h + 1) * MLA_V_DIM)
        z = z_ref[:, vsl].astype(F32)
        acc = acc_scr[h]
        o_ref[:, vsl] = (acc[:, :MLA_V_DIM] / acc[:, MLA_V_DIM:] * (z * _sigmoid(z))).astype(BF16)


def _attn(q, k, v, pm, bsz, t, s_len, past, tq, tk):
    nq = t // tq
    return pl.pallas_call(
        functools.partial(_attn_body, past=past, tq=tq, tk=tk, s_len=s_len),
        grid=(bsz, nq),
        in_specs=[pl.BlockSpec((tq, MLA_QK_WIDTH), lambda b, iq: (b * nq + iq, 0)),
                  pl.BlockSpec((s_len, MLA_QK_WIDTH), lambda b, iq: (b, 0)),
                  pl.BlockSpec((s_len, MLA_WIDTH), lambda b, iq: (b, 0)),
                  pl.BlockSpec((tq, MLA_WIDTH), lambda b, iq: (b * nq + iq, 1))],
        out_specs=pl.BlockSpec((tq, MLA_WIDTH), lambda b, iq: (b * nq + iq, 0)),
        out_shape=jax.ShapeDtypeStruct((bsz * t, MLA_WIDTH), BF16),
        scratch_shapes=[pltpu.VMEM((MLA_HEADS, tq, LANES), F32), pltpu.VMEM((MLA_HEADS, tq, 2 * MLA_V_DIM), F32)],
        compiler_params=_cparams("parallel", "arbitrary"),
        name="mla_attn",
    )(q, k, v, pm)


def _s5_weights_body(bre_ref, bim_ref, cre_ref, cim_ref, spread_ref, mask_ref, wb_ref, wc_ref):
    def expand(m_ref):
        return _dot(m_ref[...].astype(BF16), spread_ref[...]) * mask_ref[...]

    wb_ref[:, 0:S5_NSTATE] = expand(bre_ref).astype(BF16)
    wb_ref[:, S5_NSTATE:] = expand(bim_ref).astype(BF16)
    wc_ref[0:S5_NSTATE, :] = jnp.transpose(expand(cre_ref)).astype(BF16)
    wc_ref[S5_NSTATE:, :] = jnp.transpose(-expand(cim_ref)).astype(BF16)


def _s5_weights(bbr, bbi, c_re, c_im):
    depth = bbr.shape[0]
    g_row = np.arange(S5_WIDTH) // S5_GROUP_CH
    g_col = np.arange(S5_NSTATE) // S5_STATE
    mask = (g_row[:, None] == g_col[None, :]).astype(np.float32)
    spread = (np.arange(S5_STATE)[:, None] == (np.arange(S5_NSTATE) % S5_STATE)[None, :]).astype(np.float32)
    small = pl.BlockSpec((None, S5_WIDTH, S5_STATE), lambda l: (l, 0, 0))
    return pl.pallas_call(
        _s5_weights_body,
        grid=(depth,),
        in_specs=[small, small, small, small,
                  _const_spec((S5_STATE, S5_NSTATE), 1), _const_spec((S5_WIDTH, S5_NSTATE), 1)],
        out_specs=[pl.BlockSpec((None, S5_WIDTH, 2 * S5_NSTATE), lambda l: (l, 0, 0)),
                   pl.BlockSpec((None, 2 * S5_NSTATE, S5_WIDTH), lambda l: (l, 0, 0))],
        out_shape=[jax.ShapeDtypeStruct((depth, S5_WIDTH, 2 * S5_NSTATE), BF16),
                   jax.ShapeDtypeStruct((depth, 2 * S5_NSTATE, S5_WIDTH), BF16)],
        compiler_params=_cparams("parallel"),
        name="s5_weights",
    )(bbr, bbi, c_re, c_im, jnp.asarray(spread, BF16), jnp.asarray(mask, F32))


def _s5_body(p0_ref, pn_ref, x0r_ref, x0i_ref, lre_ref, lim_ref, wb_ref, wc_ref, d_ref, wglu_ref, bglu_ref, never_ref,
             o_ref, xr_out, xi_out, uz_bt, uz_a, uz_b, uz_c, bu_a, bu_b, bu_c, o_tb, xr_s, xi_s, *, lc, pitch):
    it = pl.program_id(1)
    nb = S5_BATCH_TILE
    ring = ((uz_a, bu_a), (uz_b, bu_b), (uz_c, bu_c))

    def stage_in(blk_ref, uz_tb, bu):
        for b in range(nb):
            for c in range(PS_COLS // LANES):
                uz_bt[c, b * pitch:b * pitch + lc, :] = blk_ref[b, :, c * LANES:(c + 1) * LANES].astype(F32)
        for t in range(lc):
            for c in range(PS_COLS // LANES):
                uz_tb[t * nb:(t + 1) * nb, c * LANES:(c + 1) * LANES] = uz_bt[c, pl.ds(t, nb, stride=pitch), :]
        bu[...] = _dot(uz_tb[:, 0:S5_WIDTH].astype(BF16), wb_ref[...])

    def stage_scan(bu):
        w = S5_NSTATE // S5_SCAN_SLICES
        never = never_ref[...] != 0
        last = None
        for c0 in range(0, S5_NSTATE, w):
            re, im = slice(c0, c0 + w), slice(S5_NSTATE + c0, S5_NSTATE + c0 + w)
            lre = jnp.broadcast_to(lre_ref[:, re], (nb, w))
            lim = jnp.broadcast_to(lim_ref[:, re], (nb, w))
            xr, xi = xr_s[:, re], xi_s[:, re]
            if last is not None:
                xr = jnp.where(never, last, xr)
            for t in range(lc):
                rs = slice(t * nb, (t + 1) * nb)
                xr, xi = lre * xr - lim * xi + bu[rs, re], lre * xi + lim * xr + bu[rs, im]
                bu[rs, re] = xr
                bu[rs, im] = xi
            xr_s[:, re] = xr
            xi_s[:, re] = xi
            last = xr

    def stage_out(uz_tb, xs):
        y = _dot(xs[...].astype(BF16), wc_ref[...]) + d_ref[...] * uz_tb[:, 0:S5_WIDTH]
        g5 = 0.5 * y * (1.0 + jnp.tanh(0.7978845608028654 * (y + 0.044715 * (y * y * y))))
        gate = _sigmoid(_dot(g5.astype(BF16), wglu_ref[...]) + bglu_ref[...])
        z = uz_tb[:, S5_WIDTH:2 * S5_WIDTH]
        o = g5 * gate * (z * _sigmoid(z))
        for c in range(S5_WIDTH // LANES):
            o_tb[c] = o[:, c * LANES:(c + 1) * LANES]
        for b in range(nb):
            for c in range(S5_WIDTH // LANES):
                o_ref[b, :, c * LANES:(c + 1) * LANES] = o_tb[c, pl.ds(b, lc, stride=nb), :].astype(BF16)

    @pl.when(it == 0)
    def _():
        xr_s[...] = x0r_ref[...]
        xi_s[...] = x0i_ref[...]
        stage_in(p0_ref, uz_a, bu_a)
        uz_c[...] = jnp.zeros(uz_c.shape, F32)
        bu_c[...] = jnp.zeros(bu_c.shape, F32)

    for r in range(3):
        @pl.when(it % 3 == r)
        def _(r=r):
            stage_in(pn_ref, *ring[(r + 1) % 3])
            stage_scan(ring[r][1])
            stage_out(*ring[(r + 2) % 3])

    @pl.when(it == pl.num_programs(1) - 2)
    def _():
        xr_out[...] = xr_s[...]
        xi_out[...] = xi_s[...]


def _s5(ps3, x0r, x0i, lre, lim, wb, wc, d, wglu, bglu, layer, bsz, t, lc):
    nb = S5_BATCH_TILE
    nt = t // lc
    pitch = lc + 8
    rows = lc * nb
    return pl.pallas_call(
        functools.partial(_s5_body, lc=lc, pitch=pitch),
        grid=(bsz // nb, nt + 1),
        in_specs=[pl.BlockSpec((nb, lc, PS_COLS), lambda g, i: (g, 0, 0)),
                  pl.BlockSpec((nb, lc, PS_COLS), lambda g, i: (g, jnp.minimum(i + 1, nt - 1), 0)),
                  pl.BlockSpec((nb, S5_NSTATE), lambda g, i: (g, 0)),
                  pl.BlockSpec((nb, S5_NSTATE), lambda g, i: (g, 0)),
                  _layer_spec((1, S5_NSTATE), layer), _layer_spec((1, S5_NSTATE), layer),
                  _layer_spec((S5_WIDTH, 2 * S5_NSTATE), layer), _layer_spec((2 * S5_NSTATE, S5_WIDTH), layer),
                  _layer_spec((1, S5_WIDTH), layer), _layer_spec((S5_WIDTH, S5_WIDTH), layer),
                  _layer_spec((1, S5_WIDTH), layer),
                  _const_spec((1, S5_NSTATE // S5_SCAN_SLICES), 2)],
        out_specs=[pl.BlockSpec((nb, lc, S5_WIDTH), lambda g, i: (g, jnp.maximum(i - 1, 0), 0)),
                   pl.BlockSpec((nb, S5_NSTATE), lambda g, i: (g, 0)),
                   pl.BlockSpec((nb, S5_NSTATE), lambda g, i: (g, 0))],
        out_shape=[jax.ShapeDtypeStruct((bsz, t, S5_WIDTH), BF16),
                   jax.ShapeDtypeStruct((bsz, S5_NSTATE), F32),
                   jax.ShapeDtypeStruct((bsz, S5_NSTATE), F32)],
        scratch_shapes=[pltpu.VMEM((PS_COLS // LANES, nb * pitch, LANES), F32)]
                       + [pltpu.VMEM((rows, PS_COLS), F32)] * 3
                       + [pltpu.VMEM((rows, 2 * S5_NSTATE), F32)] * 3
                       + [pltpu.VMEM((S5_WIDTH // LANES, rows, LANES), F32),
                          pltpu.VMEM((nb, S5_NSTATE), F32), pltpu.VMEM((nb, S5_NSTATE), F32)],
        compiler_params=_cparams("parallel", "arbitrary"),
        name="s5",
    )(ps3, ps3, x0r, x0i, lre, lim, wb, wc, d, wglu, bglu, jnp.zeros((1, S5_NSTATE // S5_SCAN_SLICES), jnp.int32))


def _outproj_rows(x, og_ref, om_ref, os_ref, wo_scr):
    acc = _dot(og_ref[...], wo_scr[0:GLA_WIDTH, :])
    acc += _dot(om_ref[...], wo_scr[GLA_WIDTH:GLA_WIDTH + MLA_WIDTH, :])
    acc += _dot(os_ref[...], wo_scr[GLA_WIDTH + MLA_WIDTH:, :])
    return x + acc


def _outproj_body(x_ref, og_ref, om_ref, os_ref, w_ref, g_ref, o_ref, wo_scr, *, final):
    @pl.when(pl.program_id(0) == 0)
    def _():
        wo_scr[...] = w_ref[...].astype(BF16)

    xn = _outproj_rows(x_ref[...], og_ref, om_ref, os_ref, wo_scr)
    if final:
        ms = jnp.mean(xn * xn, axis=-1, keepdims=True)
        xn = xn * lax.rsqrt(ms + EPS) * g_ref[...]
    o_ref[...] = xn


def _out_in_body(x_ref, og_ref, om_ref, os_ref, wo_ref, g_ref, wt_ref, xo_ref, pg_ref, pm_ref, ps_ref,
                 wo_scr, w_scr):
    @pl.when(pl.program_id(0) == 0)
    def _():
        wo_scr[...] = wo_ref[...].astype(BF16)
        w_scr[...] = wt_ref[...].astype(BF16)

    xn = _outproj_rows(x_ref[...], og_ref, om_ref, os_ref, wo_scr)
    xo_ref[...] = xn
    _inproj_rows(xn, g_ref, w_scr, pg_ref, pm_ref, ps_ref)


def _out_in(x2, og, om, os_, w_out, ln, wt, layer, tm):
    n = x2.shape[0]
    row = lambda c: pl.BlockSpec((tm, c), lambda i: (i, 0))
    return pl.pallas_call(
        _out_in_body,
        grid=(n // tm,),
        in_specs=[row(D_MODEL), row(GLA_WIDTH), row(MLA_WIDTH), row(S5_WIDTH),
                  _layer_spec((D_MODEL, D_MODEL), layer, single_buffer=True),
                  _layer_spec((1, D_MODEL), layer + 1),
                  _layer_spec((IN_OFF['end'], D_MODEL), layer + 1, single_buffer=True)],
        out_specs=[row(D_MODEL), row(PG_COLS), row(PM_COLS), row(PS_COLS)],
        out_shape=[jax.ShapeDtypeStruct((n, D_MODEL), F32), jax.ShapeDtypeStruct((n, PG_COLS), BF16),
                   jax.ShapeDtypeStruct((n, PM_COLS), BF16), jax.ShapeDtypeStruct((n, PS_COLS), BF16)],
        scratch_shapes=[pltpu.VMEM((D_MODEL, D_MODEL), BF16), pltpu.VMEM((IN_OFF['end'], D_MODEL), BF16)],
        compiler_params=_cparams("arbitrary"),
        name="outproj_inproj",
    )(x2, og, om, os_, w_out, ln, wt)


def _outproj(x2, og, om, os_, w, gain, layer, tm, final):
    n = x2.shape[0]
    row = lambda c: pl.BlockSpec((tm, c), lambda i: (i, 0))
    return pl.pallas_call(
        functools.partial(_outproj_body, final=final),
        grid=(n // tm,),
        in_specs=[row(D_MODEL), row(GLA_WIDTH), row(MLA_WIDTH), row(S5_WIDTH),
                  _layer_spec((D_MODEL, D_MODEL), layer, single_buffer=True),
                  _const_spec((1, D_MODEL), 1)],
        out_specs=row(D_MODEL),
        out_shape=jax.ShapeDtypeStruct((n, D_MODEL), F32),
        scratch_shapes=[pltpu.VMEM((D_MODEL, D_MODEL), BF16)],
        compiler_params=_cparams("arbitrary"),
        name="outproj_final" if final else "outproj",
    )(x2, og, om, os_, w, gain)


def _prepare_params(ln_gain, w_in, gla_w_gate, gla_b_gate, gla_norm_gain, mla_q_norm_gain, mla_w_uq,
                    mla_kv_norm_gain, mla_w_ukv, s5_lambda_re, s5_lambda_im, s5_b_re, s5_b_im, s5_c_re, s5_c_im,
                    s5_d, s5_log_dt, s5_w_glu, s5_b_glu, w_out):
    depth = w_in.shape[0]
    w_t = jnp.swapaxes(w_in, 1, 2)
    wg = jnp.pad(gla_w_gate, ((0, 0), (0, GLA_QK - GLA_GATE_RANK), (0, 0))).astype(BF16)
    wq = mla_w_uq.reshape(depth, MLA_Q_RANK, MLA_HEADS, MLA_NOPE_DIM + MLA_ROPE_DIM)
    half = MLA_ROPE_DIM // 2
    wq = jnp.concatenate([wq, -wq[..., MLA_NOPE_DIM + half:], wq[..., MLA_NOPE_DIM:MLA_NOPE_DIM + half]], axis=-1)
    wq = jnp.pad(wq, ((0, 0), (0, 256 - MLA_Q_RANK), (0, 0), (0, 0)))
    wq = wq.reshape(depth, 256, MLA_QK_WIDTH).astype(BF16)
    gq = jnp.pad(mla_q_norm_gain, ((0, 0), (0, 256 - MLA_Q_RANK))).reshape(depth, 1, 256)
    wkv = mla_w_ukv.reshape(depth, MLA_KV_RANK, MLA_HEADS, MLA_NOPE_DIM + MLA_V_DIM)
    wk = jnp.pad(wkv[..., :MLA_NOPE_DIM], ((0, 0), (0, 0), (0, 0), (0, MLA_QK_PAD - MLA_NOPE_DIM)))
    wkv_r = jnp.concatenate([wk.reshape(depth, MLA_KV_RANK, MLA_QK_WIDTH),
                             wkv[..., MLA_NOPE_DIM:].reshape(depth, MLA_KV_RANK, MLA_WIDTH)], axis=2).astype(BF16)
    wka = jnp.pad(jnp.transpose(wkv[..., :MLA_NOPE_DIM], (0, 2, 3, 1)),
                  ((0, 0), (0, 0), (0, MLA_QK_PAD - MLA_NOPE_DIM), (0, 0))).astype(BF16)
    wv = jnp.transpose(wkv[..., MLA_NOPE_DIM:], (0, 2, 1, 3)).astype(BF16)
    dt = jnp.exp(s5_log_dt)[:, :, None]
    mag = jnp.exp(s5_lambda_re * dt)
    lbr, lbi = mag * jnp.cos(s5_lambda_im * dt), mag * jnp.sin(s5_lambda_im * dt)
    den = s5_lambda_re * s5_lambda_re + s5_lambda_im * s5_lambda_im
    qr = ((lbr - 1.0) * s5_lambda_re + lbi * s5_lambda_im) / den
    qi = (lbi * s5_lambda_re - (lbr - 1.0) * s5_lambda_im) / den
    b_re_t, b_im_t = jnp.swapaxes(s5_b_re, 2, 3), jnp.swapaxes(s5_b_im, 2, 3)
    bbr = qr[:, :, None, :] * b_re_t - qi[:, :, None, :] * b_im_t
    bbi = qr[:, :, None, :] * b_im_t + qi[:, :, None, :] * b_re_t
    rows = lambda m: m.reshape(depth, S5_WIDTH, S5_STATE)
    wb, wc = _s5_weights(rows(bbr), rows(bbi), rows(s5_c_re), rows(s5_c_im))
    return dict(
        ln=ln_gain.reshape(depth, 1, D_MODEL), w_in=w_t, wg=wg, bg=gla_b_gate.reshape(depth, 1, GLA_QK),
        gla_gain=jnp.tile(gla_norm_gain, (1, GLA_HEADS)).reshape(depth, 1, GLA_WIDTH),
        gq=gq, wq=wq, gkv=mla_kv_norm_gain.reshape(depth, 1, MLA_KV_RANK), wkv=wkv_r, wka=wka, wv=wv,
        lre=lbr.reshape(depth, 1, S5_NSTATE), lim=lbi.reshape(depth, 1, S5_NSTATE),
        wb=wb, wc=wc, d=s5_d.reshape(depth, 1, S5_WIDTH), wglu=s5_w_glu.astype(BF16),
        bglu=s5_b_glu.reshape(depth, 1, S5_WIDTH), w_out=w_out)


def _rope_tables(past, t, reps):
    half = MLA_ROPE_DIM // 2
    inv = ROPE_BASE ** (-np.arange(half, dtype=np.float64) / half)
    ang = (past + np.arange(t, dtype=np.float64))[:, None] * inv[None, :]
    cos, sin = np.cos(ang), np.sin(ang)
    pad = MLA_QK_PAD - MLA_NOPE_DIM - MLA_ROPE_DIM
    cos_t = np.concatenate([np.ones((t, MLA_NOPE_DIM)), cos, cos, np.zeros((t, pad))], axis=1)
    sin_t = np.concatenate([np.zeros((t, MLA_NOPE_DIM)), sin, sin, np.zeros((t, pad))], axis=1)
    return jnp.asarray(np.tile(cos_t, (reps, 1)), F32), jnp.asarray(np.tile(sin_t, (reps, 1)), F32)


def _trunk(x, gla_state, ckv_cache, kpe_cache, s5_re, s5_im, p, final_gain):
    bsz, t, _ = x.shape
    n = bsz * t
    depth = p['w_in'].shape[0]
    past = 0 if ckv_cache is None else ckv_cache.shape[2]
    s_len = past + t
    tl = _tiles(bsz, t, past)
    cos_t, sin_t = _rope_tables(past, t, max(1, tl['prep'] // t))
    kpet_cache = None if kpe_cache is None else jnp.swapaxes(kpe_cache, 2, 3)
    x2 = x.reshape(n, D_MODEL)
    gain_f = final_gain.reshape(1, D_MODEL)
    gla_o, ckv_o, kpe_o, re_o, im_o = [], [], [], [], []
    pg, pm, ps = _inproj(x2, p['ln'], p['w_in'], 0, tl['out_row'])
    for l in range(depth):
        s0 = jnp.zeros((bsz, GLA_HEADS, GLA_DK, GLA_DV), F32) if gla_state is None else gla_state[l]
        o_gla, s_new = _gla(pg.reshape(bsz, t, PG_COLS), p['wg'], p['bg'], p['gla_gain'], l, s0, bsz, t, tl)
        o_gla = o_gla.reshape(n, GLA_WIDTH)
        gla_o.append(s_new)
        prep = functools.partial(_mla_prep, pm, cos_t, sin_t, p['gq'], p['wq'], p['gkv'], p['wkv'], l, bsz, t,
                                 tl['prep'])
        if past == 0:
            q, ckv_new, kpet, k_cat, v_all = prep(with_kv=True)
            kpe_o.append(jnp.swapaxes(kpet, 1, 2))
            o_mla = _attn(q, k_cat, v_all, pm, bsz, t, s_len, past, tl['attn_q'], tl['attn_k'])
        else:
            q, ckv_new, kpe128 = prep(with_kv=False)
            kpe_o.append(kpe128[:, MLA_NOPE_DIM:MLA_NOPE_DIM + MLA_ROPE_DIM].reshape(bsz, t, MLA_ROPE_DIM))
            o_mla = _attn_cached(q, pm, ckv_new, kpe128, ckv_cache, kpet_cache, p['wka'], p['wv'], l, bsz, t,
                                 tl['cached_seqs'])
        ckv_o.append(ckv_new.reshape(bsz, t, MLA_KV_RANK))
        x0r = jnp.zeros((bsz, S5_NSTATE), F32) if s5_re is None else s5_re[l].reshape(bsz, S5_NSTATE)
        x0i = jnp.zeros((bsz, S5_NSTATE), F32) if s5_im is None else s5_im[l].reshape(bsz, S5_NSTATE)
        o_s5, xr, xi = _s5(ps.reshape(bsz, t, PS_COLS), x0r, x0i, p['lre'], p['lim'], p['wb'], p['wc'],
                           p['d'], p['wglu'], p['bglu'], l, bsz, t, tl['s5_rows'])
        re_o.append(xr.reshape(bsz, S5_GROUPS, S5_STATE))
        im_o.append(xi.reshape(bsz, S5_GROUPS, S5_STATE))
        o_s5 = o_s5.reshape(n, S5_WIDTH)
        if l < depth - 1:
            x2, pg, pm, ps = _out_in(x2, o_gla, o_mla, o_s5, p['w_out'], p['ln'], p['w_in'], l, tl['row'])
        else:
            x2 = _outproj(x2, o_gla, o_mla, o_s5, p['w_out'], gain_f, l, tl['out_row'], final=True)
    return (x2.reshape(bsz, t, D_MODEL), jnp.stack(gla_o), jnp.stack(ckv_o), jnp.stack(kpe_o),
            jnp.stack(re_o), jnp.stack(im_o))


def kernel(x_prompt, x_sample, state_gla, cache_mla_ckv, cache_mla_kpe, state_s5_re, state_s5_im, ln_gain, w_in, gla_w_gate, gla_b_gate, gla_norm_gain, mla_q_norm_gain, mla_w_uq, mla_kv_norm_gain, mla_w_ukv, s5_lambda_re, s5_lambda_im, s5_b_re, s5_b_im, s5_c_re, s5_c_im, s5_d, s5_log_dt, s5_w_glu, s5_b_glu, w_out, final_gain):
    p = _prepare_params(ln_gain, w_in, gla_w_gate, gla_b_gate, gla_norm_gain, mla_q_norm_gain, mla_w_uq,
                        mla_kv_norm_gain, mla_w_ukv, s5_lambda_re, s5_lambda_im, s5_b_re, s5_b_im,
                        s5_c_re, s5_c_im, s5_d, s5_log_dt, s5_w_glu, s5_b_glu, w_out)
    y_p, gla_p, ckv_p, kpe_p, re_p, im_p = _trunk(x_prompt, None, None, None, None, None, p, final_gain)
    y_s, gla_s, ckv_s, kpe_s, re_s, im_s = _trunk(x_sample, state_gla, cache_mla_ckv, cache_mla_kpe,
                                                  state_s5_re, state_s5_im, p, final_gain)
    return (y_p, y_s, gla_p, ckv_p, kpe_p, re_p, im_p, gla_s, ckv_s, kpe_s, re_s, im_s)
```

```python
import functools
import math

import numpy as np
import jax
import jax.numpy as jnp
from jax import lax
from jax.experimental import pallas as pl
from jax.experimental.pallas import tpu as pltpu

F32 = jnp.float32
BF16 = jnp.bfloat16

LANES = 128
D_MODEL = 1024
CHUNK = 64
EPS = 1e-6
GLA_HEADS = 4
GLA_DV = 64
GLA_DK = 32
GLA_WIDTH = GLA_HEADS * GLA_DV
GLA_QK = GLA_HEADS * GLA_DK
GLA_GATE_RANK = 16
GLA_GATE_TAU = 16.0
GLA_SUB = 16
GLA_PLAIN_MAX_DECAY = 60.0
MLA_HEADS = 4
MLA_NOPE_DIM = 64
MLA_ROPE_DIM = 32
MLA_V_DIM = 128
MLA_Q_RANK = 192
MLA_KV_RANK = 128
MLA_WIDTH = MLA_HEADS * MLA_V_DIM
MLA_QK_PAD = 128
MLA_QK_WIDTH = MLA_HEADS * MLA_QK_PAD
ROPE_BASE = 10000.0
S5_GROUPS = 16
S5_GROUP_CH = 16
S5_STATE = 64
S5_WIDTH = S5_GROUPS * S5_GROUP_CH
S5_NSTATE = S5_GROUPS * S5_STATE
S5_BATCH_TILE = 8
S5_SCAN_SLICES = 4

PG_COLS = 896
PS_COLS = 512
_IN_SEGS = (('g_q', GLA_QK), ('g_k', GLA_QK), ('g_v', GLA_WIDTH), ('g_lr', GLA_GATE_RANK), ('g_z', GLA_WIDTH),
            ('m_cq', MLA_Q_RANK), ('m_ckv', MLA_KV_RANK), ('m_kr', MLA_ROPE_DIM), ('m_z', MLA_WIDTH),
            ('s_u', S5_WIDTH), ('s_z', S5_WIDTH), ('end', 0))
PG_Q, PG_K, PG_V, PG_Z, PG_LR = slice(0, 128), slice(128, 256), slice(256, 512), slice(512, 768), slice(768, 896)
PG_QKV = slice(PG_Q.start, PG_V.stop)
IN_OFF = dict(zip([n for n, _ in _IN_SEGS], np.cumsum([0] + [w for _, w in _IN_SEGS[:-1]]).tolist()))

VMEM_LIMIT_BYTES = 48 * 1024 * 1024


def _tiles(bsz, t, past):
    n = bsz * t
    s_len = past + t
    return dict(
        row=min(512, n),
        in_row=min(1024, n) if t >= 1024 or t < 512 else 512,
        out_row=min(1024, n),
        cached_seqs=math.gcd(bsz, max(1, 64 // t)),
        gla_rows=min(512, t), gla_chunk=min(CHUNK, t),
        gla_seqs=math.gcd(bsz, max(2, 128 // t)),
        attn_q=min(512, t), attn_k=min(512, s_len),
        s5_rows=min(64, t))


def _cparams(*sem):
    return pltpu.CompilerParams(dimension_semantics=sem, vmem_limit_bytes=VMEM_LIMIT_BYTES)


def _sigmoid(x):
    return 0.5 * (1.0 + jnp.tanh(0.5 * x))


def _dot(a, b):
    return jnp.dot(a, b, preferred_element_type=F32)


def _dot_t(a, b):
    return lax.dot_general(a, b, (((0,), (0,)), ((), ())), preferred_element_type=F32)


def _const_spec(shape, ngrid):
    zeros = (0,) * len(shape)
    return pl.BlockSpec(shape, lambda *_: zeros)


def _layer_spec(shape, layer, single_buffer=False):
    zeros = (0,) * len(shape)
    mode = dict(pipeline_mode=pl.Buffered(1)) if single_buffer else {}
    return pl.BlockSpec((None,) + tuple(shape), lambda *_: (layer,) + zeros, **mode)


def _rope128(x, cos_t, sin_t):
    lane = lax.broadcasted_iota(jnp.int32, x.shape, 1)
    first_half = (lane >= MLA_NOPE_DIM) & (lane < MLA_NOPE_DIM + MLA_ROPE_DIM // 2)
    rot = jnp.where(first_half, -pltpu.roll(x, LANES - MLA_ROPE_DIM // 2, 1), pltpu.roll(x, MLA_ROPE_DIM // 2, 1))
    return x * cos_t + rot * sin_t


def _mla_prep_rows(cq, ckv, kr, mla, with_kv):
    if with_kv:
        cos_ref, sin_ref, gq_ref, wq_ref, gkv_ref, wkv_ref, q_ref, ckv_ref, kpet_ref, k_ref, v_ref = mla
    else:
        cos_ref, sin_ref, gq_ref, wq_ref, gkv_ref, q_ref, ckv_ref, kpe_ref = mla
    cos_t = cos_ref[...]
    sin_t = sin_ref[...]
    ones = jnp.ones((2 * LANES, LANES), BF16)
    ms = _dot((cq * cq).astype(BF16), ones) * (1.0 / MLA_Q_RANK)
    inv = lax.rsqrt(ms + EPS)
    cqn = (cq * jnp.concatenate([inv, inv], axis=1) * gq_ref[...]).astype(BF16)
    qh = _dot(cqn, wq_ref[...])
    scale = (MLA_NOPE_DIM + MLA_ROPE_DIM) ** -0.5 * math.log2(math.e)
    for h in range(MLA_HEADS):
        x = qh[:, h * MLA_QK_PAD:(h + 1) * MLA_QK_PAD]
        roped = x * cos_t + pltpu.roll(x, LANES - MLA_ROPE_DIM, 1) * sin_t
        q_ref[:, h * MLA_QK_PAD:(h + 1) * MLA_QK_PAD] = (roped * scale).astype(BF16)
    ms = _dot((ckv * ckv).astype(BF16), ones[0:LANES]) * (1.0 / MLA_KV_RANK)
    ckv_n = ckv * lax.rsqrt(ms + EPS) * gkv_ref[...]
    ckv_ref[...] = ckv_n
    kpe128 = _rope128(kr, cos_t, sin_t)
    if with_kv:
        kpet_ref[...] = jnp.transpose(kpe128)[MLA_NOPE_DIM:MLA_NOPE_DIM + MLA_ROPE_DIM, :]
        kv = _dot(ckv_n.astype(BF16), wkv_ref[...])
        for h in range(MLA_HEADS):
            sl = slice(h * MLA_QK_PAD, (h + 1) * MLA_QK_PAD)
            k_ref[:, sl] = (kv[:, sl] + kpe128).astype(BF16)
        v_ref[...] = kv[:, MLA_QK_WIDTH:].astype(BF16)
    else:
        kpe_ref[...] = kpe128


def _inproj_rows(x, g_ref, w_scr, og_ref, mz_ref, os_ref, mla, with_kv):
    ms = jnp.mean(x * x, axis=-1, keepdims=True)
    h = (x * lax.rsqrt(ms + EPS) * g_ref[...]).astype(BF16)

    def seg(a, b):
        return lax.dot_general(h, w_scr[a:b, :], (((1,), (1,)), ((), ())), preferred_element_type=F32)

    lane = lax.broadcasted_iota(jnp.int32, (x.shape[0], LANES), 1)
    lane2 = lax.broadcasted_iota(jnp.int32, (x.shape[0], 2 * LANES), 1)
    c = IN_OFF
    og_ref[:, PG_QKV] = seg(c['g_q'], c['g_lr']).astype(BF16)
    og_ref[:, PG_Z] = seg(c['g_z'], c['m_cq']).astype(BF16)
    og_ref[:, PG_LR] = jnp.where(lane < GLA_GATE_RANK, seg(c['g_lr'], c['g_lr'] + LANES), 0.0).astype(BF16)
    mz_ref[...] = seg(c['m_z'], c['s_u']).astype(BF16)
    os_ref[...] = seg(c['s_u'], c['end']).astype(BF16)
    cq = jnp.where(lane2 < MLA_Q_RANK, seg(c['m_cq'], c['m_cq'] + 2 * LANES), 0.0)
    ckv_kr = seg(c['m_ckv'], c['m_ckv'] + 2 * LANES)
    kr = pltpu.roll(ckv_kr[:, LANES:], MLA_NOPE_DIM, 1)
    kr = jnp.where((lane >= MLA_NOPE_DIM) & (lane < MLA_NOPE_DIM + MLA_ROPE_DIM), kr, 0.0)
    _mla_prep_rows(cq, ckv_kr[:, 0:LANES], kr, mla, with_kv)


N_MLA_IN = {True: 6, False: 5}
N_MLA_OUT = {True: 5, False: 3}


def _inproj_body(x_ref, g_ref, wt_ref, *rest, with_kv):
    n_in, n_out = N_MLA_IN[with_kv], N_MLA_OUT[with_kv]
    mla_in, (og_ref, mz_ref, os_ref), mla_out, (w_scr,) = (
        rest[:n_in], rest[n_in:n_in + 3], rest[n_in + 3:n_in + 3 + n_out], rest[n_in + 3 + n_out:])

    @pl.when(pl.program_id(0) == 0)
    def _():
        w_scr[...] = wt_ref[...].astype(BF16)

    _inproj_rows(x_ref[...], g_ref, w_scr, og_ref, mz_ref, os_ref, tuple(mla_in) + tuple(mla_out), with_kv)


def _mla_specs(p, cos_t, sin_t, layer, bsz, t, tm, with_kv):
    n = bsz * t
    ntab = max(1, t // tm)
    assert not with_kv or tm <= t
    row = lambda c: pl.BlockSpec((tm, c), lambda i: (i, 0))
    table = pl.BlockSpec((tm, LANES), lambda i: (i % ntab, 0))
    args = [cos_t, sin_t, p['gq'], p['wq'], p['gkv']]
    in_specs = [table, table, _layer_spec((1, 256), layer), _layer_spec((256, MLA_QK_WIDTH), layer),
                _layer_spec((1, MLA_KV_RANK), layer)]
    out_specs = [row(MLA_QK_WIDTH), row(MLA_KV_RANK)]
    out_shape = [jax.ShapeDtypeStruct((n, MLA_QK_WIDTH), BF16), jax.ShapeDtypeStruct((n, MLA_KV_RANK), F32)]
    if with_kv:
        args.append(p['wkv'])
        in_specs.append(_layer_spec((MLA_KV_RANK, MLA_QK_WIDTH + MLA_WIDTH), layer))
        out_specs += [pl.BlockSpec((None, MLA_ROPE_DIM, tm), lambda i: (i // ntab, 0, i % ntab)),
                      row(MLA_QK_WIDTH), row(MLA_WIDTH)]
        out_shape += [jax.ShapeDtypeStruct((bsz, MLA_ROPE_DIM, t), F32),
                      jax.ShapeDtypeStruct((n, MLA_QK_WIDTH), BF16), jax.ShapeDtypeStruct((n, MLA_WIDTH), BF16)]
    else:
        out_specs.append(row(LANES))
        out_shape.append(jax.ShapeDtypeStruct((n, LANES), F32))
    return args, in_specs, out_specs, out_shape


def _proj_out(tm, n):
    row = lambda c: pl.BlockSpec((tm, c), lambda i: (i, 0))
    return ([row(PG_COLS), row(MLA_WIDTH), row(PS_COLS)],
            [jax.ShapeDtypeStruct((n, PG_COLS), BF16), jax.ShapeDtypeStruct((n, MLA_WIDTH), BF16),
             jax.ShapeDtypeStruct((n, PS_COLS), BF16)])


def _inproj(x2, p, cos_t, sin_t, layer, bsz, t, tm, with_kv):
    n = x2.shape[0]
    m_args, m_in, m_out, m_shape = _mla_specs(p, cos_t, sin_t, layer, bsz, t, tm, with_kv)
    o_specs, o_shape = _proj_out(tm, n)
    return pl.pallas_call(
        functools.partial(_inproj_body, with_kv=with_kv),
        grid=(n // tm,),
        in_specs=[pl.BlockSpec((tm, D_MODEL), lambda i: (i, 0)),
                  _layer_spec((1, D_MODEL), layer),
                  _layer_spec((IN_OFF['end'], D_MODEL), layer, single_buffer=True)] + m_in,
        scratch_shapes=[pltpu.VMEM((IN_OFF['end'], D_MODEL), BF16)],
        out_specs=o_specs + m_out,
        out_shape=o_shape + m_shape,
        compiler_params=_cparams("arbitrary"),
        name="inproj",
    )(x2, p['ln'], p['w_in'], *m_args)


def _gla_body(p_ref, wg_ref, bg_ref, gain_ref, ones_k_ref, ones_v_ref, bd_ref, s0_ref,
              o_ref, sout_ref, q_scr, k_scr, v_scr, b_scr, s_scr, qe_scr, qs_scr, ke_scr, vb_scr, o_scr,
              *, chunk, nch, nseq):
    it = pl.program_id(1)
    sub = min(GLA_SUB, chunk)
    nsub = chunk // sub

    @pl.when(it == 0)
    def _():
        s_scr[...] = jnp.zeros(s_scr.shape, F32)
        for sq in range(nseq):
            for h in range(GLA_HEADS):
                s_scr[sq, h * GLA_DK:(h + 1) * GLA_DK, h * GLA_DV:(h + 1) * GLA_DV] = s0_ref[sq, h]

    def cols(lanes):
        parts = [p_ref[sq, :, lanes] for sq in range(nseq)]
        return parts[0] if nseq == 1 else jnp.concatenate(parts, axis=0)

    q_scr[...] = cols(PG_Q).astype(F32) * (GLA_DK ** -0.5)
    k_scr[...] = cols(PG_K).astype(F32)
    v_scr[...] = cols(PG_V).astype(F32)
    logit = _dot(cols(PG_LR), wg_ref[...]) + bg_ref[...]
    log_a = (jnp.minimum(logit, 0.0) - jnp.log(1.0 + jnp.exp(-jnp.abs(logit)))) * (1.0 / GLA_GATE_TAU)
    row_in_chunk = lax.broadcasted_iota(jnp.int32, log_a.shape, 0) % chunk
    b_all = log_a
    shift = 1
    while shift < chunk:
        b_all = b_all + jnp.where(row_in_chunk >= shift, pltpu.roll(b_all, shift, 0), 0.0)
        shift *= 2
    b_scr[...] = b_all

    row = lax.broadcasted_iota(jnp.int32, (chunk, GLA_QK), 0)
    row_in_sub = lax.broadcasted_iota(jnp.int32, (sub, GLA_QK), 0)

    def decay_columns(b_end):
        col = jnp.transpose(jnp.broadcast_to(jnp.exp(b_end), (GLA_QK, GLA_QK)))
        return jnp.concatenate([col, col], axis=1)

    def load_chunk(sq, c):
        r0 = pl.multiple_of((sq * nch + c) * chunk, chunk)
        return (r0, q_scr[pl.ds(r0, chunk), :], k_scr[pl.ds(r0, chunk), :], v_scr[pl.ds(r0, chunk), :],
                b_scr[pl.ds(r0, chunk), :], b_scr[pl.ds(r0 + chunk - 1, 1), :], s_scr[sq])

    def finish_chunk(sq, r0, o, s_prev, a_state, b_end):
        ms = _dot((o * o).astype(BF16), ones_v_ref[...]) * (1.0 / GLA_DV)
        o_n = o * lax.rsqrt(ms + EPS) * gain_ref[...]
        t0 = r0 - sq * nch * chunk
        z = p_ref[sq, pl.ds(t0, chunk), PG_Z].astype(F32)
        o_ref[sq, pl.ds(t0, chunk), :] = (o_n * (z * _sigmoid(z))).astype(BF16)
        s_scr[sq] = s_prev * decay_columns(b_end) + a_state

    def robust_chunk(c, carry, sq):
        r0, qc, kc, vc, bc, b_end, s_prev = load_chunk(sq, c)
        xs = [qc * jnp.exp(bc)]
        ks = []
        for sj in range(nsub - 1):
            e_j = b_scr[pl.ds(r0 + (sj + 1) * sub - 1, 1), :]
            later = row >= (sj + 1) * sub
            xs.append(jnp.where(later, qc * jnp.exp(jnp.where(later, bc - e_j, 0.0)), 0.0))
            own = (row >= sj * sub) & (row < (sj + 1) * sub)
            ks.append(jnp.where(own, kc * jnp.exp(jnp.where(own, e_j - bc, 0.0)), 0.0))
        ks.append(kc * jnp.exp(b_end - bc))
        k_all = jnp.concatenate(ks, axis=1).astype(BF16)
        a_all = _dot_t(k_all, vc.astype(BF16)) * bd_ref[...]
        w = jnp.concatenate([s_prev, a_all[:(nsub - 1) * GLA_QK]], axis=0).astype(BF16) if nsub > 1 \
            else s_prev.astype(BF16)
        o_off = _dot(jnp.concatenate(xs, axis=1).astype(BF16), w)
        rows = []
        for si in range(nsub):
            q_i = qc[si * sub:(si + 1) * sub]
            b_i = bc[si * sub:(si + 1) * sub]
            es = []
            for j in range(sub):
                r = r0 + si * sub + j
                b_j = b_scr[pl.ds(r, 1), :]
                k_j = k_scr[pl.ds(r, 1), :]
                valid = row_in_sub >= j
                es.append(q_i * k_j * jnp.exp(jnp.where(valid, b_i - b_j, -jnp.inf)))
            e_all = jnp.concatenate(es, axis=0)
            e_hi = e_all.astype(BF16)
            e_lo = (e_all - e_hi.astype(F32)).astype(BF16)
            p_all = _dot(e_hi, ones_k_ref[...]) + _dot(e_lo, ones_k_ref[...])
            acc = o_off[si * sub:(si + 1) * sub]
            for j in range(sub):
                v_j = v_scr[pl.ds(r0 + si * sub + j, 1), :]
                acc = acc + p_all[j * sub:(j + 1) * sub] * v_j
            rows.append(acc)
        o = rows[0] if nsub == 1 else jnp.concatenate(rows, axis=0)
        finish_chunk(sq, r0, o, s_prev, a_all[(nsub - 1) * GLA_QK:], b_end)
        return carry

    lane_head_v = lax.broadcasted_iota(jnp.int32, (chunk, GLA_WIDTH), 1) // GLA_DV
    causal = (lax.broadcasted_iota(jnp.int32, (GLA_HEADS * chunk, chunk), 0) % chunk
              >= lax.broadcasted_iota(jnp.int32, (GLA_HEADS * chunk, chunk), 1))

    def plain_block():
        tt = nseq * nch * chunk
        b_all = b_scr[...]
        q_all = q_scr[...]
        k_all = k_scr[...]
        qe = q_all * jnp.exp(b_all)
        lane_head = lax.broadcasted_iota(jnp.int32, (tt, GLA_QK), 1) // GLA_DK
        qe_scr[...] = qe.astype(BF16)
        for h in range(GLA_HEADS):
            qs_scr[h] = jnp.where(lane_head == h, qe, 0.0).astype(BF16)
        ke_scr[...] = (k_all * jnp.exp(-b_all)).astype(BF16)
        vb_scr[...] = cols(PG_V)
        states = [s_scr[sq] for sq in range(nseq)]
        for c, sq in [(c, sq) for c in range(nch) for sq in range(nseq)]:
            r0 = (sq * nch + c) * chunk
            rs = slice(r0, r0 + chunk)
            s_cur = states[sq]
            qs = jnp.concatenate([qs_scr[h, rs, :] for h in range(GLA_HEADS)], axis=0)
            s = lax.dot_general(qs, ke_scr[rs, :], (((1,), (1,)), ((), ())), preferred_element_type=F32)
            s = jnp.where(causal, s, 0.0).astype(BF16)
            r = _dot(s, vb_scr[rs, :])
            o = _dot(qe_scr[rs, :], s_cur.astype(BF16))
            for h in range(GLA_HEADS):
                o = o + jnp.where(lane_head_v == h, r[h * chunk:(h + 1) * chunk], 0.0)
            o_scr[rs, :] = o
            b_end = b_scr[r0 + chunk - 1:r0 + chunk, :]
            k_end = (k_scr[rs, :] * jnp.exp(b_end - b_scr[rs, :])).astype(BF16)
            a_state = _dot_t(k_end, vb_scr[rs, :]) * bd_ref[0:GLA_QK, :]
            states[sq] = s_cur * decay_columns(b_end) + a_state
        for sq in range(nseq):
            s_scr[sq] = states[sq]
        o = o_scr[...]
        ms = _dot((o * o).astype(BF16), ones_v_ref[...]) * (1.0 / GLA_DV)
        o_n = o * lax.rsqrt(ms + EPS) * gain_ref[...]
        z = cols(PG_Z).astype(F32)
        o_all = (o_n * (z * _sigmoid(z))).astype(BF16)
        for sq in range(nseq):
            o_ref[sq] = o_all[sq * nch * chunk:(sq + 1) * nch * chunk]

    in_range = jnp.max(-b_scr[...]) < GLA_PLAIN_MAX_DECAY

    @pl.when(in_range)
    def _():
        plain_block()

    @pl.when(jnp.logical_not(in_range))
    def _():
        for sq in range(nseq):
            lax.fori_loop(0, nch, functools.partial(robust_chunk, sq=sq), 0)

    @pl.when(it == pl.num_programs(1) - 1)
    def _():
        for sq in range(nseq):
            for h in range(GLA_HEADS):
                sout_ref[sq, h] = s_scr[sq, h * GLA_DK:(h + 1) * GLA_DK, h * GLA_DV:(h + 1) * GLA_DV]


def _gla_consts(chunk):
    nsub = chunk // min(GLA_SUB, chunk)
    hk = np.arange(GLA_QK) // GLA_DK
    hv = np.arange(GLA_WIDTH) // GLA_DV
    same_kv = (hk[:, None] == hv[None, :]).astype(np.float32)
    same_vv = (hv[:, None] == hv[None, :]).astype(np.float32)
    return (jnp.asarray(same_kv, BF16), jnp.asarray(same_vv, BF16),
            jnp.asarray(np.tile(same_kv, (nsub, 1)), F32))


def _gla(pg, wg, bg, gain, layer, s0_bd, bsz, t, tl):
    chunk, tt, nseq = tl['gla_chunk'], tl['gla_rows'], tl['gla_seqs']
    nch = tt // chunk
    nt = t // tt
    rows = nseq * tt
    nsub = chunk // min(GLA_SUB, chunk)
    ones_k, ones_v, bd = _gla_consts(chunk)
    const = lambda shape: _const_spec(shape, 2)
    return pl.pallas_call(
        functools.partial(_gla_body, chunk=chunk, nch=nch, nseq=nseq),
        grid=(bsz // nseq, nt),
        in_specs=[pl.BlockSpec((nseq, tt, PG_COLS), lambda b, i: (b, i, 0)),
                  _layer_spec((GLA_QK, GLA_QK), layer), _layer_spec((1, GLA_QK), layer),
                  _layer_spec((1, GLA_WIDTH), layer),
                  const((GLA_QK, GLA_WIDTH)), const((GLA_WIDTH, GLA_WIDTH)),
                  const((nsub * GLA_QK, GLA_WIDTH)),
                  pl.BlockSpec((nseq, GLA_HEADS, GLA_DK, GLA_DV), lambda b, i: (b, 0, 0, 0))],
        out_specs=[pl.BlockSpec((nseq, tt, GLA_WIDTH), lambda b, i: (b, i, 0)),
                   pl.BlockSpec((nseq, GLA_HEADS, GLA_DK, GLA_DV), lambda b, i: (b, 0, 0, 0))],
        out_shape=[jax.ShapeDtypeStruct((bsz, t, GLA_WIDTH), BF16),
                   jax.ShapeDtypeStruct((bsz, GLA_HEADS, GLA_DK, GLA_DV), F32)],
        scratch_shapes=[pltpu.VMEM((rows, GLA_QK), F32), pltpu.VMEM((rows, GLA_QK), F32),
                        pltpu.VMEM((rows, GLA_WIDTH), F32), pltpu.VMEM((rows, GLA_QK), F32),
                        pltpu.VMEM((nseq, GLA_QK, GLA_WIDTH), F32),
                        pltpu.VMEM((rows, GLA_QK), BF16), pltpu.VMEM((GLA_HEADS, rows, GLA_QK), BF16),
                        pltpu.VMEM((rows, GLA_QK), BF16),
                        pltpu.VMEM((rows, GLA_WIDTH), BF16), pltpu.VMEM((rows, GLA_WIDTH), F32)],
        compiler_params=_cparams("parallel", "arbitrary"),
        name="gla",
    )(pg, wg, bg, gain, ones_k, ones_v, bd, s0_bd)


def _attn_cached_body(q_ref, z_ref, ckv_new_ref, kpe_new_ref, ckv_past_ref, kpet_past_ref, wka_ref, wv_ref,
                      o_ref, kpet_scr, *, past, t, nseq):
    last = (((1,), (1,)), ((), ()))
    hsl = [slice(h * MLA_QK_PAD, (h + 1) * MLA_QK_PAD) for h in range(MLA_HEADS)]
    q_lat_h = [_dot(q_ref[:, hsl[h]], wka_ref[h]).astype(BF16) for h in range(MLA_HEADS)]
    kpet_scr[...] = jnp.zeros(kpet_scr.shape, BF16)
    o_lat = []
    for sq in range(nseq):
        rs = slice(sq * t, (sq + 1) * t)
        q_rows = jnp.concatenate([q_ref[rs, hsl[h]] for h in range(MLA_HEADS)], axis=0)
        q_lat = jnp.concatenate([q_lat_h[h][rs] for h in range(MLA_HEADS)], axis=0)
        c_past = ckv_past_ref[sq].astype(BF16)
        c_new = ckv_new_ref[rs, :].astype(BF16)
        kpet_scr[sq, MLA_NOPE_DIM:MLA_NOPE_DIM + MLA_ROPE_DIM, :] = kpet_past_ref[sq].astype(BF16)
        s_past = (lax.dot_general(q_lat, c_past, last, preferred_element_type=F32)
                  + _dot(q_rows, kpet_scr[sq]))
        s_new = (lax.dot_general(q_lat, c_new, last, preferred_element_type=F32)
                 + lax.dot_general(q_rows, kpe_new_ref[rs, :].astype(BF16), last, preferred_element_type=F32))
        if past // CHUNK != (past + t - 1) // CHUNK:
            q_chunk = (past + lax.broadcasted_iota(jnp.int32, s_past.shape, 0) % t) // CHUNK
            s_past = jnp.where(lax.broadcasted_iota(jnp.int32, s_past.shape, 1) // CHUNK <= q_chunk, s_past, -jnp.inf)
            q_chunk = (past + lax.broadcasted_iota(jnp.int32, s_new.shape, 0) % t) // CHUNK
            s_new = jnp.where((past + lax.broadcasted_iota(jnp.int32, s_new.shape, 1)) // CHUNK <= q_chunk,
                              s_new, -jnp.inf)
        m = jnp.maximum(jnp.max(s_past, axis=-1, keepdims=True), jnp.max(s_new, axis=-1, keepdims=True))
        p_past = jnp.exp2(s_past - m)
        p_new = jnp.exp2(s_new - m)
        l = jnp.sum(p_past, axis=-1, keepdims=True) + jnp.sum(p_new, axis=-1, keepdims=True)
        o_lat.append(((_dot(p_past.astype(BF16), c_past) + _dot(p_new.astype(BF16), c_new)) / l).astype(BF16))
    for h in range(MLA_HEADS):
        vsl = slice(h * MLA_V_DIM, (h + 1) * MLA_V_DIM)
        o_h = jnp.concatenate([o_lat[sq][h * t:(h + 1) * t] for sq in range(nseq)], axis=0)
        z = z_ref[:, vsl].astype(F32)
        o_ref[:, vsl] = (_dot(o_h, wv_ref[h]) * (z * _sigmoid(z))).astype(BF16)


def _attn_cached(q, mz, ckv_new, kpe128, ckv_past, kpet_past, wka, wv, layer, bsz, t, nseq):
    past = ckv_past.shape[2]
    rows = nseq * t
    return pl.pallas_call(
        functools.partial(_attn_cached_body, past=past, t=t, nseq=nseq),
        grid=(bsz // nseq,),
        in_specs=[pl.BlockSpec((rows, MLA_QK_WIDTH), lambda b: (b, 0)),
                  pl.BlockSpec((rows, MLA_WIDTH), lambda b: (b, 0)),
                  pl.BlockSpec((rows, MLA_KV_RANK), lambda b: (b, 0)),
                  pl.BlockSpec((rows, LANES), lambda b: (b, 0)),
                  pl.BlockSpec((None, nseq, past, MLA_KV_RANK), lambda b: (layer, b, 0, 0)),
                  pl.BlockSpec((None, nseq, MLA_ROPE_DIM, past), lambda b: (layer, b, 0, 0)),
                  _layer_spec((MLA_HEADS, MLA_QK_PAD, MLA_KV_RANK), layer),
                  _layer_spec((MLA_HEADS, MLA_KV_RANK, MLA_V_DIM), layer)],
        out_specs=pl.BlockSpec((rows, MLA_WIDTH), lambda b: (b, 0)),
        out_shape=jax.ShapeDtypeStruct((bsz * t, MLA_WIDTH), BF16),
        scratch_shapes=[pltpu.VMEM((nseq, MLA_QK_PAD, past), BF16)],
        compiler_params=_cparams("parallel"),
        name="mla_attn_cached",
    )(q, mz, ckv_new, kpe128, ckv_past, kpet_past, wka, wv)


def _attn_body(q_ref, k_ref, v_ref, z_ref, o_ref, m_scr, acc_scr, *, past, tq, tk, s_len):
    iq = pl.program_id(1)
    q_first = past + iq * tq
    full_keys = jnp.minimum((q_first // CHUNK + 1) * CHUNK, s_len)
    vis_keys = jnp.minimum(((q_first + tq - 1) // CHUNK + 1) * CHUNK, s_len)
    n_full = full_keys // tk
    n_vis = (vis_keys + tk - 1) // tk
    m_scr[...] = jnp.full(m_scr.shape, -jnp.inf, F32)
    acc_scr[...] = jnp.zeros(acc_scr.shape, F32)
    split_diagonal = past % tk == 0 and tq == tk and (tq // 2) % CHUNK == 0

    def block(kb, carry, masked):
        k0 = pl.multiple_of(kb * tk, tk)
        if masked and split_diagonal:
            parts = [(r * (tq // 2), tq // 2, (r + 1) * (tk // 2)) for r in range(2)]
        else:
            parts = [(0, tq, tk)]
        for r0, nr, kext in parts:
            rows = slice(r0, r0 + nr)
            if masked:
                q_pos = (r0 if split_diagonal else q_first + r0) + lax.broadcasted_iota(jnp.int32, (nr, kext), 0)
                k_pos = (0 if split_diagonal else k0) + lax.broadcasted_iota(jnp.int32, (nr, kext), 1)
                visible = k_pos // CHUNK <= q_pos // CHUNK
            ntile, rem = kext // LANES, kext % LANES
            ones_v = jnp.ones((kext, MLA_V_DIM), BF16)
            for h in range(MLA_HEADS):
                sl = slice(h * MLA_QK_PAD, (h + 1) * MLA_QK_PAD)
                vsl = slice(h * MLA_V_DIM, (h + 1) * MLA_V_DIM)
                s = lax.dot_general(q_ref[rows, sl], k_ref[pl.ds(k0, kext), sl], (((1,), (1,)), ((), ())),
                                    preferred_element_type=F32)
                if masked:
                    s = jnp.where(visible, s, -jnp.inf)
                m_prev = m_scr[h, rows]
                m_new = jnp.maximum(m_prev, jnp.max(s, axis=-1, keepdims=True))
                alpha = jnp.exp2(m_prev - m_new)
                ps = [jnp.exp2(s[:, c * LANES:(c + 1) * LANES] - m_new) for c in range(ntile)]
                if rem:
                    ps.append(jnp.exp2(s[:, ntile * LANES:] - m_new[:, :rem]))
                p = jnp.concatenate(ps, axis=1).astype(BF16)
                v_ext = jnp.concatenate([v_ref[pl.ds(k0, kext), vsl], ones_v], axis=1)
                acc_scr[h, rows] = jnp.concatenate([alpha, alpha], axis=1) * acc_scr[h, rows] + _dot(p, v_ext)
                m_scr[h, rows] = m_new
        return carry

    lax.fori_loop(0, n_full, functools.partial(block, masked=False), 0)
    lax.fori_loop(n_full, n_vis, functools.partial(block, masked=True), 0)
    for h in range(MLA_HEADS):
        vsl = slice(h * MLA_V_DIM, (h + 1) * MLA_V_DIM)
        z = z_ref[:, vsl].astype(F32)
        acc = acc_scr[h]
        o_ref[:, vsl] = (acc[:, :MLA_V_DIM] / acc[:, MLA_V_DIM:] * (z * _sigmoid(z))).astype(BF16)


def _attn(q, k, v, mz, bsz, t, s_len, past, tq, tk):
    nq = t // tq
    return pl.pallas_call(
        functools.partial(_attn_body, past=past, tq=tq, tk=tk, s_len=s_len),
        grid=(bsz, nq),
        in_specs=[pl.BlockSpec((tq, MLA_QK_WIDTH), lambda b, iq: (b * nq + iq, 0)),
                  pl.BlockSpec((s_len, MLA_QK_WIDTH), lambda b, iq: (b, 0)),
                  pl.BlockSpec((s_len, MLA_WIDTH), lambda b, iq: (b, 0)),
                  pl.BlockSpec((tq, MLA_WIDTH), lambda b, iq: (b * nq + iq, 0))],
        out_specs=pl.BlockSpec((tq, MLA_WIDTH), lambda b, iq: (b * nq + iq, 0)),
        out_shape=jax.ShapeDtypeStruct((bsz * t, MLA_WIDTH), BF16),
        scratch_shapes=[pltpu.VMEM((MLA_HEADS, tq, LANES), F32), pltpu.VMEM((MLA_HEADS, tq, 2 * MLA_V_DIM), F32)],
        compiler_params=_cparams("parallel", "arbitrary"),
        name="mla_attn",
    )(q, k, v, mz)


def _s5_weights_body(bre_ref, bim_ref, cre_ref, cim_ref, spread_ref, mask_ref, wb_ref, wc_ref):
    def expand(m_ref):
        return _dot(m_ref[...].astype(BF16), spread_ref[...]) * mask_ref[...]

    wb_ref[:, 0:S5_NSTATE] = expand(bre_ref).astype(BF16)
    wb_ref[:, S5_NSTATE:] = expand(bim_ref).astype(BF16)
    wc_ref[0:S5_NSTATE, :] = jnp.transpose(expand(cre_ref)).astype(BF16)
    wc_ref[S5_NSTATE:, :] = jnp.transpose(-expand(cim_ref)).astype(BF16)


def _s5_weights(bbr, bbi, c_re, c_im):
    depth = bbr.shape[0]
    g_row = np.arange(S5_WIDTH) // S5_GROUP_CH
    g_col = np.arange(S5_NSTATE) // S5_STATE
    mask = (g_row[:, None] == g_col[None, :]).astype(np.float32)
    spread = (np.arange(S5_STATE)[:, None] == (np.arange(S5_NSTATE) % S5_STATE)[None, :]).astype(np.float32)
    small = pl.BlockSpec((None, S5_WIDTH, S5_STATE), lambda l: (l, 0, 0))
    return pl.pallas_call(
        _s5_weights_body,
        grid=(depth,),
        in_specs=[small, small, small, small,
                  _const_spec((S5_STATE, S5_NSTATE), 1), _const_spec((S5_WIDTH, S5_NSTATE), 1)],
        out_specs=[pl.BlockSpec((None, S5_WIDTH, 2 * S5_NSTATE), lambda l: (l, 0, 0)),
                   pl.BlockSpec((None, 2 * S5_NSTATE, S5_WIDTH), lambda l: (l, 0, 0))],
        out_shape=[jax.ShapeDtypeStruct((depth, S5_WIDTH, 2 * S5_NSTATE), BF16),
                   jax.ShapeDtypeStruct((depth, 2 * S5_NSTATE, S5_WIDTH), BF16)],
        compiler_params=_cparams("parallel"),
        name="s5_weights",
    )(bbr, bbi, c_re, c_im, jnp.asarray(spread, BF16), jnp.asarray(mask, F32))


def _s5_body(p0_ref, pn_ref, x0r_ref, x0i_ref, lre_ref, lim_ref, wb_ref, wc_ref, d_ref, wglu_ref, bglu_ref, never_ref,
             o_ref, xr_out, xi_out, uz_bt, uz_a, uz_b, uz_c, bu_a, bu_b, bu_c, o_tb, xr_s, xi_s, *, lc, pitch):
    it = pl.program_id(1)
    nb = S5_BATCH_TILE
    ring = ((uz_a, bu_a), (uz_b, bu_b), (uz_c, bu_c))

    def stage_in(blk_ref, uz_tb, bu):
        for b in range(nb):
            for c in range(PS_COLS // LANES):
                uz_bt[c, b * pitch:b * pitch + lc, :] = blk_ref[b, :, c * LANES:(c + 1) * LANES].astype(F32)
        for t in range(lc):
            for c in range(PS_COLS // LANES):
                uz_tb[t * nb:(t + 1) * nb, c * LANES:(c + 1) * LANES] = uz_bt[c, pl.ds(t, nb, stride=pitch), :]
        bu[...] = _dot(uz_tb[:, 0:S5_WIDTH].astype(BF16), wb_ref[...])

    def stage_scan(bu):
        w = S5_NSTATE // S5_SCAN_SLICES
        never = never_ref[...] != 0
        last = None
        for c0 in range(0, S5_NSTATE, w):
            re, im = slice(c0, c0 + w), slice(S5_NSTATE + c0, S5_NSTATE + c0 + w)
            lre = jnp.broadcast_to(lre_ref[:, re], (nb, w))
            lim = jnp.broadcast_to(lim_ref[:, re], (nb, w))
            xr, xi = xr_s[:, re], xi_s[:, re]
            if last is not None:
                xr = jnp.where(never, last, xr)
            for t in range(lc):
                rs = slice(t * nb, (t + 1) * nb)
                xr, xi = lre * xr - lim * xi + bu[rs, re], lre * xi + lim * xr + bu[rs, im]
                bu[rs, re] = xr
                bu[rs, im] = xi
            xr_s[:, re] = xr
            xi_s[:, re] = xi
            last = xr

    def stage_out(uz_tb, xs):
        y = _dot(xs[...].astype(BF16), wc_ref[...]) + d_ref[...] * uz_tb[:, 0:S5_WIDTH]
        g5 = 0.5 * y * (1.0 + jnp.tanh(0.7978845608028654 * (y + 0.044715 * (y * y * y))))
        gate = _sigmoid(_dot(g5.astype(BF16), wglu_ref[...]) + bglu_ref[...])
        z = uz_tb[:, S5_WIDTH:2 * S5_WIDTH]
        o = g5 * gate * (z * _sigmoid(z))
        for c in range(S5_WIDTH // LANES):
            o_tb[c] = o[:, c * LANES:(c + 1) * LANES]
        for b in range(nb):
            for c in range(S5_WIDTH // LANES):
                o_ref[b, :, c * LANES:(c + 1) * LANES] = o_tb[c, pl.ds(b, lc, stride=nb), :].astype(BF16)

    @pl.when(it == 0)
    def _():
        xr_s[...] = x0r_ref[...]
        xi_s[...] = x0i_ref[...]
        stage_in(p0_ref, uz_a, bu_a)
        uz_c[...] = jnp.zeros(uz_c.shape, F32)
        bu_c[...] = jnp.zeros(bu_c.shape, F32)

    for r in range(3):
        @pl.when(it % 3 == r)
        def _(r=r):
            stage_in(pn_ref, *ring[(r + 1) % 3])
            stage_scan(ring[r][1])
            stage_out(*ring[(r + 2) % 3])

    @pl.when(it == pl.num_programs(1) - 2)
    def _():
        xr_out[...] = xr_s[...]
        xi_out[...] = xi_s[...]


def _s5(ps3, x0r, x0i, lre, lim, wb, wc, d, wglu, bglu, layer, bsz, t, lc):
    nb = S5_BATCH_TILE
    nt = t // lc
    pitch = lc + 8
    rows = lc * nb
    return pl.pallas_call(
        functools.partial(_s5_body, lc=lc, pitch=pitch),
        grid=(bsz // nb, nt + 1),
        in_specs=[pl.BlockSpec((nb, lc, PS_COLS), lambda g, i: (g, 0, 0)),
                  pl.BlockSpec((nb, lc, PS_COLS), lambda g, i: (g, jnp.minimum(i + 1, nt - 1), 0)),
                  pl.BlockSpec((nb, S5_NSTATE), lambda g, i: (g, 0)),
                  pl.BlockSpec((nb, S5_NSTATE), lambda g, i: (g, 0)),
                  _layer_spec((1, S5_NSTATE), layer), _layer_spec((1, S5_NSTATE), layer),
                  _layer_spec((S5_WIDTH, 2 * S5_NSTATE), layer), _layer_spec((2 * S5_NSTATE, S5_WIDTH), layer),
                  _layer_spec((1, S5_WIDTH), layer), _layer_spec((S5_WIDTH, S5_WIDTH), layer),
                  _layer_spec((1, S5_WIDTH), layer),
                  _const_spec((1, S5_NSTATE // S5_SCAN_SLICES), 2)],
        out_specs=[pl.BlockSpec((nb, lc, S5_WIDTH), lambda g, i: (g, jnp.maximum(i - 1, 0), 0)),
                   pl.BlockSpec((nb, S5_NSTATE), lambda g, i: (g, 0)),
                   pl.BlockSpec((nb, S5_NSTATE), lambda g, i: (g, 0))],
        out_shape=[jax.ShapeDtypeStruct((bsz, t, S5_WIDTH), BF16),
                   jax.ShapeDtypeStruct((bsz, S5_NSTATE), F32),
                   jax.ShapeDtypeStruct((bsz, S5_NSTATE), F32)],
        scratch_shapes=[pltpu.VMEM((PS_COLS // LANES, nb * pitch, LANES), F32)]
                       + [pltpu.VMEM((rows, PS_COLS), F32)] * 3
                       + [pltpu.VMEM((rows, 2 * S5_NSTATE), F32)] * 3
                       + [pltpu.VMEM((S5_WIDTH // LANES, rows, LANES), F32),
                          pltpu.VMEM((nb, S5_NSTATE), F32), pltpu.VMEM((nb, S5_NSTATE), F32)],
        compiler_params=_cparams("parallel", "arbitrary"),
        name="s5",
    )(ps3, ps3, x0r, x0i, lre, lim, wb, wc, d, wglu, bglu, jnp.zeros((1, S5_NSTATE // S5_SCAN_SLICES), jnp.int32))


def _outproj_rows(x, og_ref, om_ref, os_ref, wo_scr):
    acc = _dot(og_ref[...], wo_scr[0:GLA_WIDTH, :])
    acc += _dot(om_ref[...], wo_scr[GLA_WIDTH:GLA_WIDTH + MLA_WIDTH, :])
    acc += _dot(os_ref[...], wo_scr[GLA_WIDTH + MLA_WIDTH:, :])
    return x + acc


def _outproj_body(x_ref, og_ref, om_ref, os_ref, w_ref, g_ref, o_ref, wo_scr, *, final):
    @pl.when(pl.program_id(0) == 0)
    def _():
        wo_scr[...] = w_ref[...].astype(BF16)

    xn = _outproj_rows(x_ref[...], og_ref, om_ref, os_ref, wo_scr)
    if final:
        ms = jnp.mean(xn * xn, axis=-1, keepdims=True)
        xn = xn * lax.rsqrt(ms + EPS) * g_ref[...]
    o_ref[...] = xn


def _out_in_body(x_ref, og_ref, om_ref, os_ref, wo_ref, g_ref, wt_ref, *rest, with_kv):
    n_in, n_out = N_MLA_IN[with_kv], N_MLA_OUT[with_kv]
    mla_in, (xo_ref, pg_ref, mz_ref, ps_ref), mla_out, (wo_scr, w_scr) = (
        rest[:n_in], rest[n_in:n_in + 4], rest[n_in + 4:n_in + 4 + n_out], rest[n_in + 4 + n_out:])

    @pl.when(pl.program_id(0) == 0)
    def _():
        wo_scr[...] = wo_ref[...].astype(BF16)
        w_scr[...] = wt_ref[...].astype(BF16)

    xn = _outproj_rows(x_ref[...], og_ref, om_ref, os_ref, wo_scr)
    xo_ref[...] = xn
    _inproj_rows(xn, g_ref, w_scr, pg_ref, mz_ref, ps_ref, tuple(mla_in) + tuple(mla_out), with_kv)


def _out_in(x2, og, om, os_, p, cos_t, sin_t, layer, bsz, t, tm, with_kv):
    n = x2.shape[0]
    row = lambda c: pl.BlockSpec((tm, c), lambda i: (i, 0))
    m_args, m_in, m_out, m_shape = _mla_specs(p, cos_t, sin_t, layer + 1, bsz, t, tm, with_kv)
    o_specs, o_shape = _proj_out(tm, n)
    return pl.pallas_call(
        functools.partial(_out_in_body, with_kv=with_kv),
        grid=(n // tm,),
        in_specs=[row(D_MODEL), row(GLA_WIDTH), row(MLA_WIDTH), row(S5_WIDTH),
                  _layer_spec((D_MODEL, D_MODEL), layer, single_buffer=True),
                  _layer_spec((1, D_MODEL), layer + 1),
                  _layer_spec((IN_OFF['end'], D_MODEL), layer + 1, single_buffer=True)] + m_in,
        out_specs=[row(D_MODEL)] + o_specs + m_out,
        out_shape=[jax.ShapeDtypeStruct((n, D_MODEL), F32)] + o_shape + m_shape,
        scratch_shapes=[pltpu.VMEM((D_MODEL, D_MODEL), BF16), pltpu.VMEM((IN_OFF['end'], D_MODEL), BF16)],
        compiler_params=_cparams("arbitrary"),
        name="outproj_inproj",
    )(x2, og, om, os_, p['w_out'], p['ln'], p['w_in'], *m_args)


def _outproj(x2, og, om, os_, w, gain, layer, tm, final):
    n = x2.shape[0]
    row = lambda c: pl.BlockSpec((tm, c), lambda i: (i, 0))
    return pl.pallas_call(
        functools.partial(_outproj_body, final=final),
        grid=(n // tm,),
        in_specs=[row(D_MODEL), row(GLA_WIDTH), row(MLA_WIDTH), row(S5_WIDTH),
                  _layer_spec((D_MODEL, D_MODEL), layer, single_buffer=True),
                  _const_spec((1, D_MODEL), 1)],
        out_specs=row(D_MODEL),
        out_shape=jax.ShapeDtypeStruct((n, D_MODEL), F32),
        scratch_shapes=[pltpu.VMEM((D_MODEL, D_MODEL), BF16)],
        compiler_params=_cparams("arbitrary"),
        name="outproj_final" if final else "outproj",
    )(x2, og, om, os_, w, gain)


def _prepare_params(ln_gain, w_in, gla_w_gate, gla_b_gate, gla_norm_gain, mla_q_norm_gain, mla_w_uq,
                    mla_kv_norm_gain, mla_w_ukv, s5_lambda_re, s5_lambda_im, s5_b_re, s5_b_im, s5_c_re, s5_c_im,
                    s5_d, s5_log_dt, s5_w_glu, s5_b_glu, w_out):
    depth = w_in.shape[0]
    w_t = jnp.swapaxes(w_in, 1, 2)
    wg = jnp.pad(gla_w_gate, ((0, 0), (0, GLA_QK - GLA_GATE_RANK), (0, 0))).astype(BF16)
    wq = mla_w_uq.reshape(depth, MLA_Q_RANK, MLA_HEADS, MLA_NOPE_DIM + MLA_ROPE_DIM)
    half = MLA_ROPE_DIM // 2
    wq = jnp.concatenate([wq, -wq[..., MLA_NOPE_DIM + half:], wq[..., MLA_NOPE_DIM:MLA_NOPE_DIM + half]], axis=-1)
    wq = jnp.pad(wq, ((0, 0), (0, 256 - MLA_Q_RANK), (0, 0), (0, 0)))
    wq = wq.reshape(depth, 256, MLA_QK_WIDTH).astype(BF16)
    gq = jnp.pad(mla_q_norm_gain, ((0, 0), (0, 256 - MLA_Q_RANK))).reshape(depth, 1, 256)
    wkv = mla_w_ukv.reshape(depth, MLA_KV_RANK, MLA_HEADS, MLA_NOPE_DIM + MLA_V_DIM)
    wk = jnp.pad(wkv[..., :MLA_NOPE_DIM], ((0, 0), (0, 0), (0, 0), (0, MLA_QK_PAD - MLA_NOPE_DIM)))
    wkv_r = jnp.concatenate([wk.reshape(depth, MLA_KV_RANK, MLA_QK_WIDTH),
                             wkv[..., MLA_NOPE_DIM:].reshape(depth, MLA_KV_RANK, MLA_WIDTH)], axis=2).astype(BF16)
    wka = jnp.pad(jnp.transpose(wkv[..., :MLA_NOPE_DIM], (0, 2, 3, 1)),
                  ((0, 0), (0, 0), (0, MLA_QK_PAD - MLA_NOPE_DIM), (0, 0))).astype(BF16)
    wv = jnp.transpose(wkv[..., MLA_NOPE_DIM:], (0, 2, 1, 3)).astype(BF16)
    dt = jnp.exp(s5_log_dt)[:, :, None]
    mag = jnp.exp(s5_lambda_re * dt)
    lbr, lbi = mag * jnp.cos(s5_lambda_im * dt), mag * jnp.sin(s5_lambda_im * dt)
    den = s5_lambda_re * s5_lambda_re + s5_lambda_im * s5_lambda_im
    qr = ((lbr - 1.0) * s5_lambda_re + lbi * s5_lambda_im) / den
    qi = (lbi * s5_lambda_re - (lbr - 1.0) * s5_lambda_im) / den
    b_re_t, b_im_t = jnp.swapaxes(s5_b_re, 2, 3), jnp.swapaxes(s5_b_im, 2, 3)
    bbr = qr[:, :, None, :] * b_re_t - qi[:, :, None, :] * b_im_t
    bbi = qr[:, :, None, :] * b_im_t + qi[:, :, None, :] * b_re_t
    rows = lambda m: m.reshape(depth, S5_WIDTH, S5_STATE)
    wb, wc = _s5_weights(rows(bbr), rows(bbi), rows(s5_c_re), rows(s5_c_im))
    return dict(
        ln=ln_gain.reshape(depth, 1, D_MODEL), w_in=w_t, wg=wg, bg=gla_b_gate.reshape(depth, 1, GLA_QK),
        gla_gain=jnp.tile(gla_norm_gain, (1, GLA_HEADS)).reshape(depth, 1, GLA_WIDTH),
        gq=gq, wq=wq, gkv=mla_kv_norm_gain.reshape(depth, 1, MLA_KV_RANK), wkv=wkv_r, wka=wka, wv=wv,
        lre=lbr.reshape(depth, 1, S5_NSTATE), lim=lbi.reshape(depth, 1, S5_NSTATE),
        wb=wb, wc=wc, d=s5_d.reshape(depth, 1, S5_WIDTH), wglu=s5_w_glu.astype(BF16),
        bglu=s5_b_glu.reshape(depth, 1, S5_WIDTH), w_out=w_out)


def _rope_tables(past, t, reps):
    half = MLA_ROPE_DIM // 2
    inv = ROPE_BASE ** (-np.arange(half, dtype=np.float64) / half)
    ang = (past + np.arange(t, dtype=np.float64))[:, None] * inv[None, :]
    cos, sin = np.cos(ang), np.sin(ang)
    pad = MLA_QK_PAD - MLA_NOPE_DIM - MLA_ROPE_DIM
    cos_t = np.concatenate([np.ones((t, MLA_NOPE_DIM)), cos, cos, np.zeros((t, pad))], axis=1)
    sin_t = np.concatenate([np.zeros((t, MLA_NOPE_DIM)), sin, sin, np.zeros((t, pad))], axis=1)
    return jnp.asarray(np.tile(cos_t, (reps, 1)), F32), jnp.asarray(np.tile(sin_t, (reps, 1)), F32)


def _trunk(x, gla_state, ckv_cache, kpe_cache, s5_re, s5_im, p, final_gain):
    bsz, t, _ = x.shape
    n = bsz * t
    depth = p['w_in'].shape[0]
    past = 0 if ckv_cache is None else ckv_cache.shape[2]
    s_len = past + t
    tl = _tiles(bsz, t, past)
    cos_t, sin_t = _rope_tables(past, t, max(1, max(tl['row'], tl['in_row']) // t))
    kpet_cache = None if kpe_cache is None else jnp.swapaxes(kpe_cache, 2, 3)
    x2 = x.reshape(n, D_MODEL)
    gain_f = final_gain.reshape(1, D_MODEL)
    gla_o, ckv_o, kpe_o, re_o, im_o = [], [], [], [], []
    with_kv = past == 0
    proj = _inproj(x2, p, cos_t, sin_t, 0, bsz, t, tl['in_row'], with_kv)
    for l in range(depth):
        pg, mz, ps, q, ckv_new = proj[:5]
        s0 = jnp.zeros((bsz, GLA_HEADS, GLA_DK, GLA_DV), F32) if gla_state is None else gla_state[l]
        o_gla, s_new = _gla(pg.reshape(bsz, t, PG_COLS), p['wg'], p['bg'], p['gla_gain'], l, s0, bsz, t, tl)
        o_gla = o_gla.reshape(n, GLA_WIDTH)
        gla_o.append(s_new)
        if with_kv:
            kpet, k_cat, v_all = proj[5:]
            kpe_o.append(jnp.swapaxes(kpet, 1, 2))
            o_mla = _attn(q, k_cat, v_all, mz, bsz, t, s_len, past, tl['attn_q'], tl['attn_k'])
        else:
            kpe128 = proj[5]
            kpe_o.append(kpe128[:, MLA_NOPE_DIM:MLA_NOPE_DIM + MLA_ROPE_DIM].reshape(bsz, t, MLA_ROPE_DIM))
            o_mla = _attn_cached(q, mz, ckv_new, kpe128, ckv_cache, kpet_cache, p['wka'], p['wv'], l, bsz, t,
                                 tl['cached_seqs'])
        ckv_o.append(ckv_new.reshape(bsz, t, MLA_KV_RANK))
        x0r = jnp.zeros((bsz, S5_NSTATE), F32) if s5_re is None else s5_re[l].reshape(bsz, S5_NSTATE)
        x0i = jnp.zeros((bsz, S5_NSTATE), F32) if s5_im is None else s5_im[l].reshape(bsz, S5_NSTATE)
        o_s5, xr, xi = _s5(ps.reshape(bsz, t, PS_COLS), x0r, x0i, p['lre'], p['lim'], p['wb'], p['wc'],
                           p['d'], p['wglu'], p['bglu'], l, bsz, t, tl['s5_rows'])
        re_o.append(xr.reshape(bsz, S5_GROUPS, S5_STATE))
        im_o.append(xi.reshape(bsz, S5_GROUPS, S5_STATE))
        o_s5 = o_s5.reshape(n, S5_WIDTH)
        if l < depth - 1:
            x2, *proj = _out_in(x2, o_gla, o_mla, o_s5, p, cos_t, sin_t, l, bsz, t, tl['row'], with_kv)
        else:
            x2 = _outproj(x2, o_gla, o_mla, o_s5, p['w_out'], gain_f, l, tl['out_row'], final=True)
    return (x2.reshape(bsz, t, D_MODEL), jnp.stack(gla_o), jnp.stack(ckv_o), jnp.stack(kpe_o),
            jnp.stack(re_o), jnp.stack(im_o))


def kernel(x_prompt, x_sample, state_gla, cache_mla_ckv, cache_mla_kpe, state_s5_re, state_s5_im, ln_gain, w_in, gla_w_gate, gla_b_gate, gla_norm_gain, mla_q_norm_gain, mla_w_uq, mla_kv_norm_gain, mla_w_ukv, s5_lambda_re, s5_lambda_im, s5_b_re, s5_b_im, s5_c_re, s5_c_im, s5_d, s5_log_dt, s5_w_glu, s5_b_glu, w_out, final_gain):
    p = _prepare_params(ln_gain, w_in, gla_w_gate, gla_b_gate, gla_norm_gain, mla_q_norm_gain, mla_w_uq,
                        mla_kv_norm_gain, mla_w_ukv, s5_lambda_re, s5_lambda_im, s5_b_re, s5_b_im,
                        s5_c_re, s5_c_im, s5_d, s5_log_dt, s5_w_glu, s5_b_glu, w_out)
    y_p, gla_p, ckv_p, kpe_p, re_p, im_p = _trunk(x_prompt, None, None, None, None, None, p, final_gain)
    y_s, gla_s, ckv_s, kpe_s, re_s, im_s = _trunk(x_sample, state_gla, cache_mla_ckv, cache_mla_kpe,
                                                  state_s5_re, state_s5_im, p, final_gain)
    return (y_p, y_s, gla_p, ckv_p, kpe_p, re_p, im_p, gla_s, ckv_s, kpe_s, re_s, im_s)
```

```python
import functools
import math

import numpy as np
import jax
import jax.numpy as jnp
from jax import lax
from jax.experimental import pallas as pl
from jax.experimental.pallas import tpu as pltpu

F32 = jnp.float32
BF16 = jnp.bfloat16

LANES = 128
D_MODEL = 1024
CHUNK = 64
EPS = 1e-6
GLA_HEADS = 4
GLA_DV = 64
GLA_DK = 32
GLA_WIDTH = GLA_HEADS * GLA_DV
GLA_QK = GLA_HEADS * GLA_DK
GLA_GATE_RANK = 16
GLA_GATE_TAU = 16.0
GLA_SUB = 16
GLA_PLAIN_MAX_DECAY = 60.0
MLA_HEADS = 4
MLA_NOPE_DIM = 64
MLA_ROPE_DIM = 32
MLA_V_DIM = 128
MLA_Q_RANK = 192
MLA_KV_RANK = 128
MLA_WIDTH = MLA_HEADS * MLA_V_DIM
MLA_QK_PAD = 128
MLA_QK_WIDTH = MLA_HEADS * MLA_QK_PAD
ROPE_BASE = 10000.0
S5_GROUPS = 16
S5_GROUP_CH = 16
S5_STATE = 64
S5_WIDTH = S5_GROUPS * S5_GROUP_CH
S5_NSTATE = S5_GROUPS * S5_STATE
S5_BATCH_TILE = 8
S5_SCAN_SLICES = 4

PG_COLS = 896
PS_COLS = 512
_IN_SEGS = (('g_q', GLA_QK), ('g_k', GLA_QK), ('g_v', GLA_WIDTH), ('g_lr', GLA_GATE_RANK), ('g_z', GLA_WIDTH),
            ('m_cq', MLA_Q_RANK), ('m_ckv', MLA_KV_RANK), ('m_kr', MLA_ROPE_DIM), ('m_z', MLA_WIDTH),
            ('s_u', S5_WIDTH), ('s_z', S5_WIDTH), ('end', 0))
PG_Q, PG_K, PG_V, PG_Z, PG_LR = slice(0, 128), slice(128, 256), slice(256, 512), slice(512, 768), slice(768, 896)
PG_QKV = slice(PG_Q.start, PG_V.stop)
IN_OFF = dict(zip([n for n, _ in _IN_SEGS], np.cumsum([0] + [w for _, w in _IN_SEGS[:-1]]).tolist()))

VMEM_LIMIT_BYTES = 48 * 1024 * 1024


def _tiles(bsz, t, past):
    n = bsz * t
    s_len = past + t
    return dict(
        row=min(512, n),
        in_row=min(1024, n) if t >= 1024 or t < 512 else 512,
        out_row=min(1024, n),
        cached_seqs=math.gcd(bsz, max(1, 64 // t)),
        gla_rows=min(512, t), gla_chunk=min(CHUNK, t),
        gla_seqs=math.gcd(bsz, max(2, 128 // t)),
        attn_q=min(512, t), attn_k=min(512, s_len),
        s5_rows=min(64, t))


def _cparams(*sem):
    return pltpu.CompilerParams(dimension_semantics=sem, vmem_limit_bytes=VMEM_LIMIT_BYTES)


def _sigmoid(x):
    return 0.5 * (1.0 + jnp.tanh(0.5 * x))


def _dot(a, b):
    return jnp.dot(a, b, preferred_element_type=F32)


def _dot_t(a, b):
    return lax.dot_general(a, b, (((0,), (0,)), ((), ())), preferred_element_type=F32)


def _const_spec(shape, ngrid):
    zeros = (0,) * len(shape)
    return pl.BlockSpec(shape, lambda *_: zeros)


def _layer_spec(shape, layer, single_buffer=False):
    zeros = (0,) * len(shape)
    mode = dict(pipeline_mode=pl.Buffered(1)) if single_buffer else {}
    return pl.BlockSpec((None,) + tuple(shape), lambda *_: (layer,) + zeros, **mode)


def _rope128(x, cos_t, sin_t):
    lane = lax.broadcasted_iota(jnp.int32, x.shape, 1)
    first_half = (lane >= MLA_NOPE_DIM) & (lane < MLA_NOPE_DIM + MLA_ROPE_DIM // 2)
    rot = jnp.where(first_half, -pltpu.roll(x, LANES - MLA_ROPE_DIM // 2, 1), pltpu.roll(x, MLA_ROPE_DIM // 2, 1))
    return x * cos_t + rot * sin_t


def _mla_prep_rows(cq, ckv, kr, mla, with_kv):
    if with_kv:
        cos_ref, sin_ref, gq_ref, wq_ref, gkv_ref, wkv_ref, q_ref, ckv_ref, kpet_ref, k_ref, v_ref = mla
    else:
        cos_ref, sin_ref, gq_ref, wq_ref, gkv_ref, q_ref, ckv_ref, kpe_ref = mla
    cos_t = cos_ref[...]
    sin_t = sin_ref[...]
    ones = jnp.ones((2 * LANES, LANES), BF16)
    ms = _dot((cq * cq).astype(BF16), ones) * (1.0 / MLA_Q_RANK)
    inv = lax.rsqrt(ms + EPS)
    cqn = (cq * jnp.concatenate([inv, inv], axis=1) * gq_ref[...]).astype(BF16)
    qh = _dot(cqn, wq_ref[...])
    scale = (MLA_NOPE_DIM + MLA_ROPE_DIM) ** -0.5 * math.log2(math.e)
    for h in range(MLA_HEADS):
        x = qh[:, h * MLA_QK_PAD:(h + 1) * MLA_QK_PAD]
        roped = x * cos_t + pltpu.roll(x, LANES - MLA_ROPE_DIM, 1) * sin_t
        q_ref[:, h * MLA_QK_PAD:(h + 1) * MLA_QK_PAD] = (roped * scale).astype(BF16)
    ms = _dot((ckv * ckv).astype(BF16), ones[0:LANES]) * (1.0 / MLA_KV_RANK)
    ckv_n = ckv * lax.rsqrt(ms + EPS) * gkv_ref[...]
    ckv_ref[...] = ckv_n
    kpe128 = _rope128(kr, cos_t, sin_t)
    if with_kv:
        kpet_ref[...] = jnp.transpose(kpe128)[MLA_NOPE_DIM:MLA_NOPE_DIM + MLA_ROPE_DIM, :]
        kv = _dot(ckv_n.astype(BF16), wkv_ref[...])
        for h in range(MLA_HEADS):
            sl = slice(h * MLA_QK_PAD, (h + 1) * MLA_QK_PAD)
            k_ref[:, sl] = (kv[:, sl] + kpe128).astype(BF16)
        v_ref[...] = kv[:, MLA_QK_WIDTH:].astype(BF16)
    else:
        kpe_ref[...] = kpe128


def _inproj_rows(x, g_ref, w_scr, og_ref, mz_ref, os_ref, mla, with_kv):
    ms = jnp.mean(x * x, axis=-1, keepdims=True)
    h = (x * lax.rsqrt(ms + EPS) * g_ref[...]).astype(BF16)

    def seg(a, b):
        return lax.dot_general(h, w_scr[a:b, :], (((1,), (1,)), ((), ())), preferred_element_type=F32)

    lane = lax.broadcasted_iota(jnp.int32, (x.shape[0], LANES), 1)
    lane2 = lax.broadcasted_iota(jnp.int32, (x.shape[0], 2 * LANES), 1)
    c = IN_OFF
    og_ref[:, PG_QKV] = seg(c['g_q'], c['g_lr']).astype(BF16)
    og_ref[:, PG_Z] = seg(c['g_z'], c['m_cq']).astype(BF16)
    og_ref[:, PG_LR] = jnp.where(lane < GLA_GATE_RANK, seg(c['g_lr'], c['g_lr'] + LANES), 0.0).astype(BF16)
    mz_ref[...] = seg(c['m_z'], c['s_u']).astype(BF16)
    os_ref[...] = seg(c['s_u'], c['end']).astype(BF16)
    cq = jnp.where(lane2 < MLA_Q_RANK, seg(c['m_cq'], c['m_cq'] + 2 * LANES), 0.0)
    ckv_kr = seg(c['m_ckv'], c['m_ckv'] + 2 * LANES)
    kr = pltpu.roll(ckv_kr[:, LANES:], MLA_NOPE_DIM, 1)
    kr = jnp.where((lane >= MLA_NOPE_DIM) & (lane < MLA_NOPE_DIM + MLA_ROPE_DIM), kr, 0.0)
    _mla_prep_rows(cq, ckv_kr[:, 0:LANES], kr, mla, with_kv)


N_MLA_IN = {True: 6, False: 5}
N_MLA_OUT = {True: 5, False: 3}


def _inproj_body(x_ref, g_ref, wt_ref, *rest, with_kv):
    n_in, n_out = N_MLA_IN[with_kv], N_MLA_OUT[with_kv]
    mla_in, (og_ref, mz_ref, os_ref), mla_out, (w_scr,) = (
        rest[:n_in], rest[n_in:n_in + 3], rest[n_in + 3:n_in + 3 + n_out], rest[n_in + 3 + n_out:])

    @pl.when(pl.program_id(0) == 0)
    def _():
        w_scr[...] = wt_ref[...].astype(BF16)

    _inproj_rows(x_ref[...], g_ref, w_scr, og_ref, mz_ref, os_ref, tuple(mla_in) + tuple(mla_out), with_kv)


def _mla_specs(p, cos_t, sin_t, layer, bsz, t, tm, with_kv):
    n = bsz * t
    ntab = max(1, t // tm)
    assert not with_kv or tm <= t
    row = lambda c: pl.BlockSpec((tm, c), lambda i: (i, 0))
    table = pl.BlockSpec((tm, LANES), lambda i: (i % ntab, 0))
    args = [cos_t, sin_t, p['gq'], p['wq'], p['gkv']]
    in_specs = [table, table, _layer_spec((1, 256), layer), _layer_spec((256, MLA_QK_WIDTH), layer),
                _layer_spec((1, MLA_KV_RANK), layer)]
    out_specs = [row(MLA_QK_WIDTH), row(MLA_KV_RANK)]
    out_shape = [jax.ShapeDtypeStruct((n, MLA_QK_WIDTH), BF16), jax.ShapeDtypeStruct((n, MLA_KV_RANK), F32)]
    if with_kv:
        args.append(p['wkv'])
        in_specs.append(_layer_spec((MLA_KV_RANK, MLA_QK_WIDTH + MLA_WIDTH), layer))
        out_specs += [pl.BlockSpec((None, MLA_ROPE_DIM, tm), lambda i: (i // ntab, 0, i % ntab)),
                      row(MLA_QK_WIDTH), row(MLA_WIDTH)]
        out_shape += [jax.ShapeDtypeStruct((bsz, MLA_ROPE_DIM, t), F32),
                      jax.ShapeDtypeStruct((n, MLA_QK_WIDTH), BF16), jax.ShapeDtypeStruct((n, MLA_WIDTH), BF16)]
    else:
        out_specs.append(row(LANES))
        out_shape.append(jax.ShapeDtypeStruct((n, LANES), F32))
    return args, in_specs, out_specs, out_shape


def _proj_out(tm, n):
    row = lambda c: pl.BlockSpec((tm, c), lambda i: (i, 0))
    return ([row(PG_COLS), row(MLA_WIDTH), row(PS_COLS)],
            [jax.ShapeDtypeStruct((n, PG_COLS), BF16), jax.ShapeDtypeStruct((n, MLA_WIDTH), BF16),
             jax.ShapeDtypeStruct((n, PS_COLS), BF16)])


def _inproj(x2, p, cos_t, sin_t, layer, bsz, t, tm, with_kv):
    n = x2.shape[0]
    m_args, m_in, m_out, m_shape = _mla_specs(p, cos_t, sin_t, layer, bsz, t, tm, with_kv)
    o_specs, o_shape = _proj_out(tm, n)
    return pl.pallas_call(
        functools.partial(_inproj_body, with_kv=with_kv),
        grid=(n // tm,),
        in_specs=[pl.BlockSpec((tm, D_MODEL), lambda i: (i, 0)),
                  _layer_spec((1, D_MODEL), layer),
                  _layer_spec((IN_OFF['end'], D_MODEL), layer, single_buffer=True)] + m_in,
        scratch_shapes=[pltpu.VMEM((IN_OFF['end'], D_MODEL), BF16)],
        out_specs=o_specs + m_out,
        out_shape=o_shape + m_shape,
        compiler_params=_cparams("arbitrary"),
        name="inproj",
    )(x2, p['ln'], p['w_in'], *m_args)


def _gla_body(p_ref, wg_ref, bg_ref, gain_ref, ones_k_ref, ones_v_ref, bd_ref, s0_ref,
              o_ref, sout_ref, q_scr, k_scr, v_scr, b_scr, s_scr, qe_scr, qs_scr, ke_scr, vb_scr, o_scr,
              *, chunk, nch, nseq):
    it = pl.program_id(1)
    sub = min(GLA_SUB, chunk)
    nsub = chunk // sub

    @pl.when(it == 0)
    def _():
        s_scr[...] = jnp.zeros(s_scr.shape, F32)
        for sq in range(nseq):
            for h in range(GLA_HEADS):
                s_scr[sq, h * GLA_DK:(h + 1) * GLA_DK, h * GLA_DV:(h + 1) * GLA_DV] = s0_ref[sq, h]

    def cols(lanes):
        parts = [p_ref[sq, :, lanes] for sq in range(nseq)]
        return parts[0] if nseq == 1 else jnp.concatenate(parts, axis=0)

    q_scr[...] = cols(PG_Q).astype(F32) * (GLA_DK ** -0.5)
    k_scr[...] = cols(PG_K).astype(F32)
    v_scr[...] = cols(PG_V).astype(F32)
    logit = _dot(cols(PG_LR), wg_ref[...]) + bg_ref[...]
    log_a = (jnp.minimum(logit, 0.0) - jnp.log(1.0 + jnp.exp(-jnp.abs(logit)))) * (1.0 / GLA_GATE_TAU)
    row_in_chunk = lax.broadcasted_iota(jnp.int32, log_a.shape, 0) % chunk
    b_all = log_a
    shift = 1
    while shift < chunk:
        b_all = b_all + jnp.where(row_in_chunk >= shift, pltpu.roll(b_all, shift, 0), 0.0)
        shift *= 2
    b_scr[...] = b_all

    row = lax.broadcasted_iota(jnp.int32, (chunk, GLA_QK), 0)
    row_in_sub = lax.broadcasted_iota(jnp.int32, (sub, GLA_QK), 0)

    def decay_columns(b_end):
        col = jnp.transpose(jnp.broadcast_to(jnp.exp(b_end), (GLA_QK, GLA_QK)))
        return jnp.concatenate([col, col], axis=1)

    def load_chunk(sq, c):
        r0 = pl.multiple_of((sq * nch + c) * chunk, chunk)
        return (r0, q_scr[pl.ds(r0, chunk), :], k_scr[pl.ds(r0, chunk), :], v_scr[pl.ds(r0, chunk), :],
                b_scr[pl.ds(r0, chunk), :], b_scr[pl.ds(r0 + chunk - 1, 1), :], s_scr[sq])

    def finish_chunk(sq, r0, o, s_prev, a_state, b_end):
        ms = _dot((o * o).astype(BF16), ones_v_ref[...]) * (1.0 / GLA_DV)
        o_n = o * lax.rsqrt(ms + EPS) * gain_ref[...]
        t0 = r0 - sq * nch * chunk
        z = p_ref[sq, pl.ds(t0, chunk), PG_Z].astype(F32)
        o_ref[sq, pl.ds(t0, chunk), :] = (o_n * (z * _sigmoid(z))).astype(BF16)
        s_scr[sq] = s_prev * decay_columns(b_end) + a_state

    def robust_chunk(c, carry, sq):
        r0, qc, kc, vc, bc, b_end, s_prev = load_chunk(sq, c)
        xs = [qc * jnp.exp(bc)]
        ks = []
        for sj in range(nsub - 1):
            e_j = b_scr[pl.ds(r0 + (sj + 1) * sub - 1, 1), :]
            later = row >= (sj + 1) * sub
            xs.append(jnp.where(later, qc * jnp.exp(jnp.where(later, bc - e_j, 0.0)), 0.0))
            own = (row >= sj * sub) & (row < (sj + 1) * sub)
            ks.append(jnp.where(own, kc * jnp.exp(jnp.where(own, e_j - bc, 0.0)), 0.0))
        ks.append(kc * jnp.exp(b_end - bc))
        k_all = jnp.concatenate(ks, axis=1).astype(BF16)
        a_all = _dot_t(k_all, vc.astype(BF16)) * bd_ref[...]
        w = jnp.concatenate([s_prev, a_all[:(nsub - 1) * GLA_QK]], axis=0).astype(BF16) if nsub > 1 \
            else s_prev.astype(BF16)
        o_off = _dot(jnp.concatenate(xs, axis=1).astype(BF16), w)
        rows = []
        for si in range(nsub):
            q_i = qc[si * sub:(si + 1) * sub]
            b_i = bc[si * sub:(si + 1) * sub]
            es = []
            for j in range(sub):
                r = r0 + si * sub + j
                b_j = b_scr[pl.ds(r, 1), :]
                k_j = k_scr[pl.ds(r, 1), :]
                valid = row_in_sub >= j
                es.append(q_i * k_j * jnp.exp(jnp.where(valid, b_i - b_j, -jnp.inf)))
            e_all = jnp.concatenate(es, axis=0)
            e_hi = e_all.astype(BF16)
            e_lo = (e_all - e_hi.astype(F32)).astype(BF16)
            p_all = _dot(e_hi, ones_k_ref[...]) + _dot(e_lo, ones_k_ref[...])
            acc = o_off[si * sub:(si + 1) * sub]
            for j in range(sub):
                v_j = v_scr[pl.ds(r0 + si * sub + j, 1), :]
                acc = acc + p_all[j * sub:(j + 1) * sub] * v_j
            rows.append(acc)
        o = rows[0] if nsub == 1 else jnp.concatenate(rows, axis=0)
        finish_chunk(sq, r0, o, s_prev, a_all[(nsub - 1) * GLA_QK:], b_end)
        return carry

    lane_head_v = lax.broadcasted_iota(jnp.int32, (chunk, GLA_WIDTH), 1) // GLA_DV
    causal = (lax.broadcasted_iota(jnp.int32, (GLA_HEADS * chunk, chunk), 0) % chunk
              >= lax.broadcasted_iota(jnp.int32, (GLA_HEADS * chunk, chunk), 1))

    def plain_block():
        tt = nseq * nch * chunk
        b_all = b_scr[...]
        q_all = q_scr[...]
        k_all = k_scr[...]
        qe = q_all * jnp.exp(b_all)
        lane_head = lax.broadcasted_iota(jnp.int32, (tt, GLA_QK), 1) // GLA_DK
        qe_scr[...] = qe.astype(BF16)
        for h in range(GLA_HEADS):
            qs_scr[h] = jnp.where(lane_head == h, qe, 0.0).astype(BF16)
        ke_scr[...] = (k_all * jnp.exp(-b_all)).astype(BF16)
        vb_scr[...] = cols(PG_V)
        states = [s_scr[sq] for sq in range(nseq)]
        for c, sq in [(c, sq) for c in range(nch) for sq in range(nseq)]:
            r0 = (sq * nch + c) * chunk
            rs = slice(r0, r0 + chunk)
            s_cur = states[sq]
            qs = jnp.concatenate([qs_scr[h, rs, :] for h in range(GLA_HEADS)], axis=0)
            s = lax.dot_general(qs, ke_scr[rs, :], (((1,), (1,)), ((), ())), preferred_element_type=F32)
            s = jnp.where(causal, s, 0.0).astype(BF16)
            r = _dot(s, vb_scr[rs, :])
            o = _dot(qe_scr[rs, :], s_cur.astype(BF16))
            for h in range(GLA_HEADS):
                o = o + jnp.where(lane_head_v == h, r[h * chunk:(h + 1) * chunk], 0.0)
            o_scr[rs, :] = o
            b_end = b_scr[r0 + chunk - 1:r0 + chunk, :]
            k_end = (k_scr[rs, :] * jnp.exp(b_end - b_scr[rs, :])).astype(BF16)
            a_state = _dot_t(k_end, vb_scr[rs, :]) * bd_ref[0:GLA_QK, :]
            states[sq] = s_cur * decay_columns(b_end) + a_state
        for sq in range(nseq):
            s_scr[sq] = states[sq]
        o = o_scr[...]
        ms = _dot((o * o).astype(BF16), ones_v_ref[...]) * (1.0 / GLA_DV)
        o_n = o * lax.rsqrt(ms + EPS) * gain_ref[...]
        z = cols(PG_Z).astype(F32)
        o_all = (o_n * (z * _sigmoid(z))).astype(BF16)
        for sq in range(nseq):
            o_ref[sq] = o_all[sq * nch * chunk:(sq + 1) * nch * chunk]

    in_range = jnp.max(-b_scr[...]) < GLA_PLAIN_MAX_DECAY

    @pl.when(in_range)
    def _():
        plain_block()

    @pl.when(jnp.logical_not(in_range))
    def _():
        for sq in range(nseq):
            lax.fori_loop(0, nch, functools.partial(robust_chunk, sq=sq), 0)

    @pl.when(it == pl.num_programs(1) - 1)
    def _():
        for sq in range(nseq):
            for h in range(GLA_HEADS):
                sout_ref[sq, h] = s_scr[sq, h * GLA_DK:(h + 1) * GLA_DK, h * GLA_DV:(h + 1) * GLA_DV]


def _gla_consts(chunk):
    nsub = chunk // min(GLA_SUB, chunk)
    hk = np.arange(GLA_QK) // GLA_DK
    hv = np.arange(GLA_WIDTH) // GLA_DV
    same_kv = (hk[:, None] == hv[None, :]).astype(np.float32)
    same_vv = (hv[:, None] == hv[None, :]).astype(np.float32)
    return (jnp.asarray(same_kv, BF16), jnp.asarray(same_vv, BF16),
            jnp.asarray(np.tile(same_kv, (nsub, 1)), F32))


def _gla(pg, wg, bg, gain, layer, s0_bd, bsz, t, tl):
    chunk, tt, nseq = tl['gla_chunk'], tl['gla_rows'], tl['gla_seqs']
    nch = tt // chunk
    nt = t // tt
    rows = nseq * tt
    nsub = chunk // min(GLA_SUB, chunk)
    ones_k, ones_v, bd = _gla_consts(chunk)
    const = lambda shape: _const_spec(shape, 2)
    return pl.pallas_call(
        functools.partial(_gla_body, chunk=chunk, nch=nch, nseq=nseq),
        grid=(bsz // nseq, nt),
        in_specs=[pl.BlockSpec((nseq, tt, PG_COLS), lambda b, i: (b, i, 0)),
                  _layer_spec((GLA_QK, GLA_QK), layer), _layer_spec((1, GLA_QK), layer),
                  _layer_spec((1, GLA_WIDTH), layer),
                  const((GLA_QK, GLA_WIDTH)), const((GLA_WIDTH, GLA_WIDTH)),
                  const((nsub * GLA_QK, GLA_WIDTH)),
                  pl.BlockSpec((nseq, GLA_HEADS, GLA_DK, GLA_DV), lambda b, i: (b, 0, 0, 0))],
        out_specs=[pl.BlockSpec((nseq, tt, GLA_WIDTH), lambda b, i: (b, i, 0)),
                   pl.BlockSpec((nseq, GLA_HEADS, GLA_DK, GLA_DV), lambda b, i: (b, 0, 0, 0))],
        out_shape=[jax.ShapeDtypeStruct((bsz, t, GLA_WIDTH), BF16),
                   jax.ShapeDtypeStruct((bsz, GLA_HEADS, GLA_DK, GLA_DV), F32)],
        scratch_shapes=[pltpu.VMEM((rows, GLA_QK), F32), pltpu.VMEM((rows, GLA_QK), F32),
                        pltpu.VMEM((rows, GLA_WIDTH), F32), pltpu.VMEM((rows, GLA_QK), F32),
                        pltpu.VMEM((nseq, GLA_QK, GLA_WIDTH), F32),
                        pltpu.VMEM((rows, GLA_QK), BF16), pltpu.VMEM((GLA_HEADS, rows, GLA_QK), BF16),
                        pltpu.VMEM((rows, GLA_QK), BF16),
                        pltpu.VMEM((rows, GLA_WIDTH), BF16), pltpu.VMEM((rows, GLA_WIDTH), F32)],
        compiler_params=_cparams("parallel", "arbitrary"),
        name="gla",
    )(pg, wg, bg, gain, ones_k, ones_v, bd, s0_bd)


def _attn_cached_body(q_ref, z_ref, ckv_new_ref, kpe_new_ref, ckv_past_ref, kpet_past_ref, wka_ref, wv_ref,
                      o_ref, kpet_scr, *, past, t, nseq):
    last = (((1,), (1,)), ((), ()))
    hsl = [slice(h * MLA_QK_PAD, (h + 1) * MLA_QK_PAD) for h in range(MLA_HEADS)]
    q_lat_h = [_dot(q_ref[:, hsl[h]], wka_ref[h]).astype(BF16) for h in range(MLA_HEADS)]
    kpet_scr[...] = jnp.zeros(kpet_scr.shape, BF16)
    o_lat = []
    for sq in range(nseq):
        rs = slice(sq * t, (sq + 1) * t)
        q_rows = jnp.concatenate([q_ref[rs, hsl[h]] for h in range(MLA_HEADS)], axis=0)
        q_lat = jnp.concatenate([q_lat_h[h][rs] for h in range(MLA_HEADS)], axis=0)
        c_past = ckv_past_ref[sq].astype(BF16)
        c_new = ckv_new_ref[rs, :].astype(BF16)
        kpet_scr[sq, MLA_NOPE_DIM:MLA_NOPE_DIM + MLA_ROPE_DIM, :] = kpet_past_ref[sq].astype(BF16)
        s_past = (lax.dot_general(q_lat, c_past, last, preferred_element_type=F32)
                  + _dot(q_rows, kpet_scr[sq]))
        s_new = (lax.dot_general(q_lat, c_new, last, preferred_element_type=F32)
                 + lax.dot_general(q_rows, kpe_new_ref[rs, :].astype(BF16), last, preferred_element_type=F32))
        if past // CHUNK != (past + t - 1) // CHUNK:
            q_chunk = (past + lax.broadcasted_iota(jnp.int32, s_past.shape, 0) % t) // CHUNK
            s_past = jnp.where(lax.broadcasted_iota(jnp.int32, s_past.shape, 1) // CHUNK <= q_chunk, s_past, -jnp.inf)
            q_chunk = (past + lax.broadcasted_iota(jnp.int32, s_new.shape, 0) % t) // CHUNK
            s_new = jnp.where((past + lax.broadcasted_iota(jnp.int32, s_new.shape, 1)) // CHUNK <= q_chunk,
                              s_new, -jnp.inf)
        m = jnp.maximum(jnp.max(s_past, axis=-1, keepdims=True), jnp.max(s_new, axis=-1, keepdims=True))
        p_past = jnp.exp2(s_past - m)
        p_new = jnp.exp2(s_new - m)
        l = jnp.sum(p_past, axis=-1, keepdims=True) + jnp.sum(p_new, axis=-1, keepdims=True)
        o_lat.append(((_dot(p_past.astype(BF16), c_past) + _dot(p_new.astype(BF16), c_new)) / l).astype(BF16))
    for h in range(MLA_HEADS):
        vsl = slice(h * MLA_V_DIM, (h + 1) * MLA_V_DIM)
        o_h = jnp.concatenate([o_lat[sq][h * t:(h + 1) * t] for sq in range(nseq)], axis=0)
        z = z_ref[:, vsl].astype(F32)
        o_ref[:, vsl] = (_dot(o_h, wv_ref[h]) * (z * _sigmoid(z))).astype(BF16)


def _attn_cached(q, mz, ckv_new, kpe128, ckv_past, kpet_past, wka, wv, layer, bsz, t, nseq):
    past = ckv_past.shape[2]
    rows = nseq * t
    return pl.pallas_call(
        functools.partial(_attn_cached_body, past=past, t=t, nseq=nseq),
        grid=(bsz // nseq,),
        in_specs=[pl.BlockSpec((rows, MLA_QK_WIDTH), lambda b: (b, 0)),
                  pl.BlockSpec((rows, MLA_WIDTH), lambda b: (b, 0)),
                  pl.BlockSpec((rows, MLA_KV_RANK), lambda b: (b, 0)),
                  pl.BlockSpec((rows, LANES), lambda b: (b, 0)),
                  pl.BlockSpec((None, nseq, past, MLA_KV_RANK), lambda b: (layer, b, 0, 0)),
                  pl.BlockSpec((None, nseq, MLA_ROPE_DIM, past), lambda b: (layer, b, 0, 0)),
                  _layer_spec((MLA_HEADS, MLA_QK_PAD, MLA_KV_RANK), layer),
                  _layer_spec((MLA_HEADS, MLA_KV_RANK, MLA_V_DIM), layer)],
        out_specs=pl.BlockSpec((rows, MLA_WIDTH), lambda b: (b, 0)),
        out_shape=jax.ShapeDtypeStruct((bsz * t, MLA_WIDTH), BF16),
        scratch_shapes=[pltpu.VMEM((nseq, MLA_QK_PAD, past), BF16)],
        compiler_params=_cparams("parallel"),
        name="mla_attn_cached",
    )(q, mz, ckv_new, kpe128, ckv_past, kpet_past, wka, wv)


def _attn_body(q_ref, k_ref, v_ref, z_ref, o_ref, m_scr, acc_scr, *, past, tq, tk, s_len):
    iq = pl.program_id(1)
    q_first = past + iq * tq
    full_keys = jnp.minimum((q_first // CHUNK + 1) * CHUNK, s_len)
    vis_keys = jnp.minimum(((q_first + tq - 1) // CHUNK + 1) * CHUNK, s_len)
    n_full = full_keys // tk
    n_vis = (vis_keys + tk - 1) // tk
    m_scr[...] = jnp.full(m_scr.shape, -jnp.inf, F32)
    acc_scr[...] = jnp.zeros(acc_scr.shape, F32)
    split_diagonal = past % tk == 0 and tq == tk and (tq // 2) % CHUNK == 0

    def block(kb, carry, masked):
        k0 = pl.multiple_of(kb * tk, tk)
        if masked and split_diagonal:
            parts = [(r * (tq // 2), tq // 2, (r + 1) * (tk // 2)) for r in range(2)]
        else:
            parts = [(0, tq, tk)]
        for r0, nr, kext in parts:
            rows = slice(r0, r0 + nr)
            if masked:
                q_pos = (r0 if split_diagonal else q_first + r0) + lax.broadcasted_iota(jnp.int32, (nr, kext), 0)
                k_pos = (0 if split_diagonal else k0) + lax.broadcasted_iota(jnp.int32, (nr, kext), 1)
                visible = k_pos // CHUNK <= q_pos // CHUNK
            ntile, rem = kext // LANES, kext % LANES
            ones_v = jnp.ones((kext, MLA_V_DIM), BF16)
            for h in range(MLA_HEADS):
                sl = slice(h * MLA_QK_PAD, (h + 1) * MLA_QK_PAD)
                vsl = slice(h * MLA_V_DIM, (h + 1) * MLA_V_DIM)
                s = lax.dot_general(q_ref[rows, sl], k_ref[pl.ds(k0, kext), sl], (((1,), (1,)), ((), ())),
                                    preferred_element_type=F32)
                if masked:
                    s = jnp.where(visible, s, -jnp.inf)
                m_prev = m_scr[h, rows]
                m_new = jnp.maximum(m_prev, jnp.max(s, axis=-1, keepdims=True))
                alpha = jnp.exp2(m_prev - m_new)
                ps = [jnp.exp2(s[:, c * LANES:(c + 1) * LANES] - m_new) for c in range(ntile)]
                if rem:
                    ps.append(jnp.exp2(s[:, ntile * LANES:] - m_new[:, :rem]))
                p = jnp.concatenate(ps, axis=1).astype(BF16)
                v_ext = jnp.concatenate([v_ref[pl.ds(k0, kext), vsl], ones_v], axis=1)
                acc_scr[h, rows] = jnp.concatenate([alpha, alpha], axis=1) * acc_scr[h, rows] + _dot(p, v_ext)
                m_scr[h, rows] = m_new
        return carry

    lax.fori_loop(0, n_full, functools.partial(block, masked=False), 0)
    lax.fori_loop(n_full, n_vis, functools.partial(block, masked=True), 0)
    for h in range(MLA_HEADS):
        vsl = slice(h * MLA_V_DIM, (h + 1) * MLA_V_DIM)
        z = z_ref[:, vsl].astype(F32)
        acc = acc_scr[h]
        o_ref[:, vsl] = (acc[:, :MLA_V_DIM] / acc[:, MLA_V_DIM:] * (z * _sigmoid(z))).astype(BF16)


def _attn(q, k, v, mz, bsz, t, s_len, past, tq, tk):
    nq = t // tq
    return pl.pallas_call(
        functools.partial(_attn_body, past=past, tq=tq, tk=tk, s_len=s_len),
        grid=(bsz, nq),
        in_specs=[pl.BlockSpec((tq, MLA_QK_WIDTH), lambda b, iq: (b * nq + iq, 0)),
                  pl.BlockSpec((s_len, MLA_QK_WIDTH), lambda b, iq: (b, 0)),
                  pl.BlockSpec((s_len, MLA_WIDTH), lambda b, iq: (b, 0)),
                  pl.BlockSpec((tq, MLA_WIDTH), lambda b, iq: (b * nq + iq, 0))],
        out_specs=pl.BlockSpec((tq, MLA_WIDTH), lambda b, iq: (b * nq + iq, 0)),
        out_shape=jax.ShapeDtypeStruct((bsz * t, MLA_WIDTH), BF16),
        scratch_shapes=[pltpu.VMEM((MLA_HEADS, tq, LANES), F32), pltpu.VMEM((MLA_HEADS, tq, 2 * MLA_V_DIM), F32)],
        compiler_params=_cparams("parallel", "arbitrary"),
        name="mla_attn",
    )(q, k, v, mz)


def _s5_weights_body(bre_ref, bim_ref, cre_ref, cim_ref, spread_ref, mask_ref, wb_ref, wc_ref):
    def expand(m_ref):
        return _dot(m_ref[...].astype(BF16), spread_ref[...]) * mask_ref[...]

    wb_ref[:, 0:S5_NSTATE] = expand(bre_ref).astype(BF16)
    wb_ref[:, S5_NSTATE:] = expand(bim_ref).astype(BF16)
    wc_ref[0:S5_NSTATE, :] = jnp.transpose(expand(cre_ref)).astype(BF16)
    wc_ref[S5_NSTATE:, :] = jnp.transpose(-expand(cim_ref)).astype(BF16)


def _s5_weights(bbr, bbi, c_re, c_im):
    depth = bbr.shape[0]
    g_row = np.arange(S5_WIDTH) // S5_GROUP_CH
    g_col = np.arange(S5_NSTATE) // S5_STATE
    mask = (g_row[:, None] == g_col[None, :]).astype(np.float32)
    spread = (np.arange(S5_STATE)[:, None] == (np.arange(S5_NSTATE) % S5_STATE)[None, :]).astype(np.float32)
    small = pl.BlockSpec((None, S5_WIDTH, S5_STATE), lambda l: (l, 0, 0))
    return pl.pallas_call(
        _s5_weights_body,
        grid=(depth,),
        in_specs=[small, small, small, small,
                  _const_spec((S5_STATE, S5_NSTATE), 1), _const_spec((S5_WIDTH, S5_NSTATE), 1)],
        out_specs=[pl.BlockSpec((None, S5_WIDTH, 2 * S5_NSTATE), lambda l: (l, 0, 0)),
                   pl.BlockSpec((None, 2 * S5_NSTATE, S5_WIDTH), lambda l: (l, 0, 0))],
        out_shape=[jax.ShapeDtypeStruct((depth, S5_WIDTH, 2 * S5_NSTATE), BF16),
                   jax.ShapeDtypeStruct((depth, 2 * S5_NSTATE, S5_WIDTH), BF16)],
        compiler_params=_cparams("parallel"),
        name="s5_weights",
    )(bbr, bbi, c_re, c_im, jnp.asarray(spread, BF16), jnp.asarray(mask, F32))


def _s5_body(p0_ref, pn_ref, x0r_ref, x0i_ref, lre_ref, lim_ref, wb_ref, wc_ref, d_ref, wglu_ref, bglu_ref, never_ref,
             o_ref, xr_out, xi_out, uz_bt, uz_a, uz_b, uz_c, bu_a, bu_b, bu_c, o_tb, xr_s, xi_s, *, lc, pitch):
    it = pl.program_id(1)
    nb = S5_BATCH_TILE
    ring = ((uz_a, bu_a), (uz_b, bu_b), (uz_c, bu_c))

    def stage_in(blk_ref, uz_tb, bu):
        for b in range(nb):
            for c in range(PS_COLS // LANES):
                uz_bt[c, b * pitch:b * pitch + lc, :] = blk_ref[b, :, c * LANES:(c + 1) * LANES].astype(F32)
        for t in range(lc):
            for c in range(PS_COLS // LANES):
                uz_tb[t * nb:(t + 1) * nb, c * LANES:(c + 1) * LANES] = uz_bt[c, pl.ds(t, nb, stride=pitch), :]
        bu[...] = _dot(uz_tb[:, 0:S5_WIDTH].astype(BF16), wb_ref[...])

    def stage_scan(bu):
        w = S5_NSTATE // S5_SCAN_SLICES
        never = never_ref[...] != 0
        last = None
        for c0 in range(0, S5_NSTATE, w):
            re, im = slice(c0, c0 + w), slice(S5_NSTATE + c0, S5_NSTATE + c0 + w)
            lre = jnp.broadcast_to(lre_ref[:, re], (nb, w))
            lim = jnp.broadcast_to(lim_ref[:, re], (nb, w))
            xr, xi = xr_s[:, re], xi_s[:, re]
            if last is not None:
                xr = jnp.where(never, last, xr)
            for t in range(lc):
                rs = slice(t * nb, (t + 1) * nb)
                xr, xi = lre * xr - lim * xi + bu[rs, re], lre * xi + lim * xr + bu[rs, im]
                bu[rs, re] = xr
                bu[rs, im] = xi
            xr_s[:, re] = xr
            xi_s[:, re] = xi
            last = xr

    def stage_out(uz_tb, xs):
        y = _dot(xs[...].astype(BF16), wc_ref[...]) + d_ref[...] * uz_tb[:, 0:S5_WIDTH]
        g5 = 0.5 * y * (1.0 + jnp.tanh(0.7978845608028654 * (y + 0.044715 * (y * y * y))))
        gate = _sigmoid(_dot(g5.astype(BF16), wglu_ref[...]) + bglu_ref[...])
        z = uz_tb[:, S5_WIDTH:2 * S5_WIDTH]
        o = g5 * gate * (z * _sigmoid(z))
        for c in range(S5_WIDTH // LANES):
            o_tb[c] = o[:, c * LANES:(c + 1) * LANES]
        for b in range(nb):
            for c in range(S5_WIDTH // LANES):
                o_ref[b, :, c * LANES:(c + 1) * LANES] = o_tb[c, pl.ds(b, lc, stride=nb), :].astype(BF16)

    @pl.when(it == 0)
    def _():
        xr_s[...] = x0r_ref[...]
        xi_s[...] = x0i_ref[...]
        stage_in(p0_ref, uz_a, bu_a)
        uz_c[...] = jnp.zeros(uz_c.shape, F32)
        bu_c[...] = jnp.zeros(bu_c.shape, F32)

    for r in range(3):
        @pl.when(it % 3 == r)
        def _(r=r):
            stage_in(pn_ref, *ring[(r + 1) % 3])
            stage_scan(ring[r][1])
            stage_out(*ring[(r + 2) % 3])

    @pl.when(it == pl.num_programs(1) - 2)
    def _():
        xr_out[...] = xr_s[...]
        xi_out[...] = xi_s[...]


def _s5(ps3, x0r, x0i, lre, lim, wb, wc, d, wglu, bglu, layer, bsz, t, lc):
    nb = S5_BATCH_TILE
    nt = t // lc
    pitch = lc + 8
    rows = lc * nb
    return pl.pallas_call(
        functools.partial(_s5_body, lc=lc, pitch=pitch),
        grid=(bsz // nb, nt + 1),
        in_specs=[pl.BlockSpec((nb, lc, PS_COLS), lambda g, i: (g, 0, 0)),
                  pl.BlockSpec((nb, lc, PS_COLS), lambda g, i: (g, jnp.minimum(i + 1, nt - 1), 0)),
                  pl.BlockSpec((nb, S5_NSTATE), lambda g, i: (g, 0)),
                  pl.BlockSpec((nb, S5_NSTATE), lambda g, i: (g, 0)),
                  _layer_spec((1, S5_NSTATE), layer), _layer_spec((1, S5_NSTATE), layer),
                  _layer_spec((S5_WIDTH, 2 * S5_NSTATE), layer), _layer_spec((2 * S5_NSTATE, S5_WIDTH), layer),
                  _layer_spec((1, S5_WIDTH), layer), _layer_spec((S5_WIDTH, S5_WIDTH), layer),
                  _layer_spec((1, S5_WIDTH), layer),
                  _const_spec((1, S5_NSTATE // S5_SCAN_SLICES), 2)],
        out_specs=[pl.BlockSpec((nb, lc, S5_WIDTH), lambda g, i: (g, jnp.maximum(i - 1, 0), 0)),
                   pl.BlockSpec((nb, S5_NSTATE), lambda g, i: (g, 0)),
                   pl.BlockSpec((nb, S5_NSTATE), lambda g, i: (g, 0))],
        out_shape=[jax.ShapeDtypeStruct((bsz, t, S5_WIDTH), BF16),
                   jax.ShapeDtypeStruct((bsz, S5_NSTATE), F32),
                   jax.ShapeDtypeStruct((bsz, S5_NSTATE), F32)],
        scratch_shapes=[pltpu.VMEM((PS_COLS // LANES, nb * pitch, LANES), F32)]
                       + [pltpu.VMEM((rows, PS_COLS), F32)] * 3
                       + [pltpu.VMEM((rows, 2 * S5_NSTATE), F32)] * 3
                       + [pltpu.VMEM((S5_WIDTH // LANES, rows, LANES), F32),
                          pltpu.VMEM((nb, S5_NSTATE), F32), pltpu.VMEM((nb, S5_NSTATE), F32)],
        compiler_params=_cparams("parallel", "arbitrary"),
        name="s5",
    )(ps3, ps3, x0r, x0i, lre, lim, wb, wc, d, wglu, bglu, jnp.zeros((1, S5_NSTATE // S5_SCAN_SLICES), jnp.int32))


def _outproj_rows(x, og_ref, om_ref, os_ref, wo_scr):
    acc = _dot(og_ref[...], wo_scr[0:GLA_WIDTH, :])
    acc += _dot(om_ref[...], wo_scr[GLA_WIDTH:GLA_WIDTH + MLA_WIDTH, :])
    acc += _dot(os_ref[...], wo_scr[GLA_WIDTH + MLA_WIDTH:, :])
    return x + acc


def _outproj_body(x_ref, og_ref, om_ref, os_ref, w_ref, g_ref, o_ref, wo_scr, *, final):
    @pl.when(pl.program_id(0) == 0)
    def _():
        wo_scr[...] = w_ref[...].astype(BF16)

    xn = _outproj_rows(x_ref[...], og_ref, om_ref, os_ref, wo_scr)
    if final:
        ms = jnp.mean(xn * xn, axis=-1, keepdims=True)
        xn = xn * lax.rsqrt(ms + EPS) * g_ref[...]
    o_ref[...] = xn


def _out_in_body(x_ref, og_ref, om_ref, os_ref, wo_ref, g_ref, wt_ref, *rest, with_kv, n_prev):
    prev, rest = rest[:n_prev], rest[n_prev:]
    n_in, n_out = N_MLA_IN[with_kv], N_MLA_OUT[with_kv]
    mla_in, (xo_ref, pg_ref, mz_ref, ps_ref), mla_out, (wo_scr, w_scr) = (
        rest[:n_in], rest[n_in:n_in + 4], rest[n_in + 4:n_in + 4 + n_out], rest[n_in + 4 + n_out:])
    if n_prev:
        ckv_all = mla_out[1]
        for j in range(n_prev):
            ckv_all[j] = prev[j][...]
        mla_out = (mla_out[0], ckv_all.at[n_prev]) + tuple(mla_out[2:])

    @pl.when(pl.program_id(0) == 0)
    def _():
        wo_scr[...] = wo_ref[...].astype(BF16)
        w_scr[...] = wt_ref[...].astype(BF16)

    xn = _outproj_rows(x_ref[...], og_ref, om_ref, os_ref, wo_scr)
    xo_ref[...] = xn
    _inproj_rows(xn, g_ref, w_scr, pg_ref, mz_ref, ps_ref, tuple(mla_in) + tuple(mla_out), with_kv)


def _out_in(x2, og, om, os_, p, cos_t, sin_t, layer, bsz, t, tm, with_kv, ckv_prev=()):
    n = x2.shape[0]
    row = lambda c: pl.BlockSpec((tm, c), lambda i: (i, 0))
    m_args, m_in, m_out, m_shape = _mla_specs(p, cos_t, sin_t, layer + 1, bsz, t, tm, with_kv)
    n_prev = len(ckv_prev)
    if n_prev:
        m_out[1] = pl.BlockSpec((n_prev + 1, tm, MLA_KV_RANK), lambda i: (0, i, 0))
        m_shape[1] = jax.ShapeDtypeStruct((n_prev + 1, n, MLA_KV_RANK), F32)
    o_specs, o_shape = _proj_out(tm, n)
    return pl.pallas_call(
        functools.partial(_out_in_body, with_kv=with_kv, n_prev=n_prev),
        grid=(n // tm,),
        in_specs=[row(D_MODEL), row(GLA_WIDTH), row(MLA_WIDTH), row(S5_WIDTH),
                  _layer_spec((D_MODEL, D_MODEL), layer, single_buffer=True),
                  _layer_spec((1, D_MODEL), layer + 1),
                  _layer_spec((IN_OFF['end'], D_MODEL), layer + 1, single_buffer=True)]
                 + [row(MLA_KV_RANK)] * n_prev + m_in,
        out_specs=[row(D_MODEL)] + o_specs + m_out,
        out_shape=[jax.ShapeDtypeStruct((n, D_MODEL), F32)] + o_shape + m_shape,
        scratch_shapes=[pltpu.VMEM((D_MODEL, D_MODEL), BF16), pltpu.VMEM((IN_OFF['end'], D_MODEL), BF16)],
        compiler_params=_cparams("arbitrary"),
        name="outproj_inproj",
    )(x2, og, om, os_, p['w_out'], p['ln'], p['w_in'], *ckv_prev, *m_args)


def _outproj(x2, og, om, os_, w, gain, layer, tm, final):
    n = x2.shape[0]
    row = lambda c: pl.BlockSpec((tm, c), lambda i: (i, 0))
    return pl.pallas_call(
        functools.partial(_outproj_body, final=final),
        grid=(n // tm,),
        in_specs=[row(D_MODEL), row(GLA_WIDTH), row(MLA_WIDTH), row(S5_WIDTH),
                  _layer_spec((D_MODEL, D_MODEL), layer, single_buffer=True),
                  _const_spec((1, D_MODEL), 1)],
        out_specs=row(D_MODEL),
        out_shape=jax.ShapeDtypeStruct((n, D_MODEL), F32),
        scratch_shapes=[pltpu.VMEM((D_MODEL, D_MODEL), BF16)],
        compiler_params=_cparams("arbitrary"),
        name="outproj_final" if final else "outproj",
    )(x2, og, om, os_, w, gain)


def _prepare_params(ln_gain, w_in, gla_w_gate, gla_b_gate, gla_norm_gain, mla_q_norm_gain, mla_w_uq,
                    mla_kv_norm_gain, mla_w_ukv, s5_lambda_re, s5_lambda_im, s5_b_re, s5_b_im, s5_c_re, s5_c_im,
                    s5_d, s5_log_dt, s5_w_glu, s5_b_glu, w_out):
    depth = w_in.shape[0]
    w_t = jnp.swapaxes(w_in, 1, 2)
    wg = jnp.pad(gla_w_gate, ((0, 0), (0, GLA_QK - GLA_GATE_RANK), (0, 0))).astype(BF16)
    wq = mla_w_uq.reshape(depth, MLA_Q_RANK, MLA_HEADS, MLA_NOPE_DIM + MLA_ROPE_DIM)
    half = MLA_ROPE_DIM // 2
    wq = jnp.concatenate([wq, -wq[..., MLA_NOPE_DIM + half:], wq[..., MLA_NOPE_DIM:MLA_NOPE_DIM + half]], axis=-1)
    wq = jnp.pad(wq, ((0, 0), (0, 256 - MLA_Q_RANK), (0, 0), (0, 0)))
    wq = wq.reshape(depth, 256, MLA_QK_WIDTH).astype(BF16)
    gq = jnp.pad(mla_q_norm_gain, ((0, 0), (0, 256 - MLA_Q_RANK))).reshape(depth, 1, 256)
    wkv = mla_w_ukv.reshape(depth, MLA_KV_RANK, MLA_HEADS, MLA_NOPE_DIM + MLA_V_DIM)
    wk = jnp.pad(wkv[..., :MLA_NOPE_DIM], ((0, 0), (0, 0), (0, 0), (0, MLA_QK_PAD - MLA_NOPE_DIM)))
    wkv_r = jnp.concatenate([wk.reshape(depth, MLA_KV_RANK, MLA_QK_WIDTH),
                             wkv[..., MLA_NOPE_DIM:].reshape(depth, MLA_KV_RANK, MLA_WIDTH)], axis=2).astype(BF16)
    wka = jnp.pad(jnp.transpose(wkv[..., :MLA_NOPE_DIM], (0, 2, 3, 1)),
                  ((0, 0), (0, 0), (0, MLA_QK_PAD - MLA_NOPE_DIM), (0, 0))).astype(BF16)
    wv = jnp.transpose(wkv[..., MLA_NOPE_DIM:], (0, 2, 1, 3)).astype(BF16)
    dt = jnp.exp(s5_log_dt)[:, :, None]
    mag = jnp.exp(s5_lambda_re * dt)
    lbr, lbi = mag * jnp.cos(s5_lambda_im * dt), mag * jnp.sin(s5_lambda_im * dt)
    den = s5_lambda_re * s5_lambda_re + s5_lambda_im * s5_lambda_im
    qr = ((lbr - 1.0) * s5_lambda_re + lbi * s5_lambda_im) / den
    qi = (lbi * s5_lambda_re - (lbr - 1.0) * s5_lambda_im) / den
    b_re_t, b_im_t = jnp.swapaxes(s5_b_re, 2, 3), jnp.swapaxes(s5_b_im, 2, 3)
    bbr = qr[:, :, None, :] * b_re_t - qi[:, :, None, :] * b_im_t
    bbi = qr[:, :, None, :] * b_im_t + qi[:, :, None, :] * b_re_t
    rows = lambda m: m.reshape(depth, S5_WIDTH, S5_STATE)
    wb, wc = _s5_weights(rows(bbr), rows(bbi), rows(s5_c_re), rows(s5_c_im))
    return dict(
        ln=ln_gain.reshape(depth, 1, D_MODEL), w_in=w_t, wg=wg, bg=gla_b_gate.reshape(depth, 1, GLA_QK),
        gla_gain=jnp.tile(gla_norm_gain, (1, GLA_HEADS)).reshape(depth, 1, GLA_WIDTH),
        gq=gq, wq=wq, gkv=mla_kv_norm_gain.reshape(depth, 1, MLA_KV_RANK), wkv=wkv_r, wka=wka, wv=wv,
        lre=lbr.reshape(depth, 1, S5_NSTATE), lim=lbi.reshape(depth, 1, S5_NSTATE),
        wb=wb, wc=wc, d=s5_d.reshape(depth, 1, S5_WIDTH), wglu=s5_w_glu.astype(BF16),
        bglu=s5_b_glu.reshape(depth, 1, S5_WIDTH), w_out=w_out)


def _rope_tables(past, t, reps):
    half = MLA_ROPE_DIM // 2
    inv = ROPE_BASE ** (-np.arange(half, dtype=np.float64) / half)
    ang = (past + np.arange(t, dtype=np.float64))[:, None] * inv[None, :]
    cos, sin = np.cos(ang), np.sin(ang)
    pad = MLA_QK_PAD - MLA_NOPE_DIM - MLA_ROPE_DIM
    cos_t = np.concatenate([np.ones((t, MLA_NOPE_DIM)), cos, cos, np.zeros((t, pad))], axis=1)
    sin_t = np.concatenate([np.zeros((t, MLA_NOPE_DIM)), sin, sin, np.zeros((t, pad))], axis=1)
    return jnp.asarray(np.tile(cos_t, (reps, 1)), F32), jnp.asarray(np.tile(sin_t, (reps, 1)), F32)


def _trunk(x, gla_state, ckv_cache, kpe_cache, s5_re, s5_im, p, final_gain):
    bsz, t, _ = x.shape
    n = bsz * t
    depth = p['w_in'].shape[0]
    past = 0 if ckv_cache is None else ckv_cache.shape[2]
    s_len = past + t
    tl = _tiles(bsz, t, past)
    cos_t, sin_t = _rope_tables(past, t, max(1, max(tl['row'], tl['in_row']) // t))
    kpet_cache = None if kpe_cache is None else jnp.swapaxes(kpe_cache, 2, 3)
    x2 = x.reshape(n, D_MODEL)
    gain_f = final_gain.reshape(1, D_MODEL)
    gla_o, ckv_o, kpe_o, re_o, im_o = [], [], [], [], []
    with_kv = past == 0
    proj = _inproj(x2, p, cos_t, sin_t, 0, bsz, t, tl['in_row'], with_kv)
    ckv_stack = None
    for l in range(depth):
        pg, mz, ps, q, ckv_new = proj[:5]
        if ckv_new.ndim == 3:
            ckv_stack, ckv_new = ckv_new, ckv_new[l]
        s0 = jnp.zeros((bsz, GLA_HEADS, GLA_DK, GLA_DV), F32) if gla_state is None else gla_state[l]
        o_gla, s_new = _gla(pg.reshape(bsz, t, PG_COLS), p['wg'], p['bg'], p['gla_gain'], l, s0, bsz, t, tl)
        o_gla = o_gla.reshape(n, GLA_WIDTH)
        gla_o.append(s_new)
        if with_kv:
            kpet, k_cat, v_all = proj[5:]
            kpe_o.append(jnp.swapaxes(kpet, 1, 2))
            o_mla = _attn(q, k_cat, v_all, mz, bsz, t, s_len, past, tl['attn_q'], tl['attn_k'])
        else:
            kpe128 = proj[5]
            kpe_o.append(kpe128[:, MLA_NOPE_DIM:MLA_NOPE_DIM + MLA_ROPE_DIM].reshape(bsz, t, MLA_ROPE_DIM))
            o_mla = _attn_cached(q, mz, ckv_new, kpe128, ckv_cache, kpet_cache, p['wka'], p['wv'], l, bsz, t,
                                 tl['cached_seqs'])
        ckv_o.append(ckv_new)
        x0r = jnp.zeros((bsz, S5_NSTATE), F32) if s5_re is None else s5_re[l].reshape(bsz, S5_NSTATE)
        x0i = jnp.zeros((bsz, S5_NSTATE), F32) if s5_im is None else s5_im[l].reshape(bsz, S5_NSTATE)
        o_s5, xr, xi = _s5(ps.reshape(bsz, t, PS_COLS), x0r, x0i, p['lre'], p['lim'], p['wb'], p['wc'],
                           p['d'], p['wglu'], p['bglu'], l, bsz, t, tl['s5_rows'])
        re_o.append(xr.reshape(bsz, S5_GROUPS, S5_STATE))
        im_o.append(xi.reshape(bsz, S5_GROUPS, S5_STATE))
        o_s5 = o_s5.reshape(n, S5_WIDTH)
        if l < depth - 1:
            stack_here = l + 1 == depth - 1
            x2, *proj = _out_in(x2, o_gla, o_mla, o_s5, p, cos_t, sin_t, l, bsz, t, tl['row'], with_kv,
                                ckv_prev=tuple(ckv_o) if stack_here else ())
        else:
            x2 = _outproj(x2, o_gla, o_mla, o_s5, p['w_out'], gain_f, l, tl['out_row'], final=True)
    ckv_all = jnp.stack(ckv_o) if ckv_stack is None else ckv_stack
    return (x2.reshape(bsz, t, D_MODEL), jnp.stack(gla_o), ckv_all.reshape(depth, bsz, t, MLA_KV_RANK),
            jnp.stack(kpe_o), jnp.stack(re_o), jnp.stack(im_o))


def kernel(x_prompt, x_sample, state_gla, cache_mla_ckv, cache_mla_kpe, state_s5_re, state_s5_im, ln_gain, w_in, gla_w_gate, gla_b_gate, gla_norm_gain, mla_q_norm_gain, mla_w_uq, mla_kv_norm_gain, mla_w_ukv, s5_lambda_re, s5_lambda_im, s5_b_re, s5_b_im, s5_c_re, s5_c_im, s5_d, s5_log_dt, s5_w_glu, s5_b_glu, w_out, final_gain):
    p = _prepare_params(ln_gain, w_in, gla_w_gate, gla_b_gate, gla_norm_gain, mla_q_norm_gain, mla_w_uq,
                        mla_kv_norm_gain, mla_w_ukv, s5_lambda_re, s5_lambda_im, s5_b_re, s5_b_im,
                        s5_c_re, s5_c_im, s5_d, s5_log_dt, s5_w_glu, s5_b_glu, w_out)
    y_p, gla_p, ckv_p, kpe_p, re_p, im_p = _trunk(x_prompt, None, None, None, None, None, p, final_gain)
    y_s, gla_s, ckv_s, kpe_s, re_s, im_s = _trunk(x_sample, state_gla, cache_mla_ckv, cache_mla_kpe,
                                                  state_s5_re, state_s5_im, p, final_gain)
    return (y_p, y_s, gla_p, ckv_p, kpe_p, re_p, im_p, gla_s, ckv_s, kpe_s, re_s, im_s)
```

```python
import functools
import math

import numpy as np
import jax
import jax.numpy as jnp
from jax import lax
from jax.experimental import pallas as pl
from jax.experimental.pallas import tpu as pltpu

F32 = jnp.float32
BF16 = jnp.bfloat16

LANES = 128
D_MODEL = 1024
CHUNK = 64
EPS = 1e-6
GLA_HEADS = 4
GLA_DV = 64
GLA_DK = 32
GLA_WIDTH = GLA_HEADS * GLA_DV
GLA_QK = GLA_HEADS * GLA_DK
GLA_GATE_RANK = 16
GLA_GATE_TAU = 16.0
GLA_SUB = 16
GLA_PLAIN_MAX_DECAY = 60.0
MLA_HEADS = 4
MLA_NOPE_DIM = 64
MLA_ROPE_DIM = 32
MLA_V_DIM = 128
MLA_Q_RANK = 192
MLA_KV_RANK = 128
MLA_WIDTH = MLA_HEADS * MLA_V_DIM
MLA_QK_PAD = 128
MLA_QK_WIDTH = MLA_HEADS * MLA_QK_PAD
ROPE_BASE = 10000.0
S5_GROUPS = 16
S5_GROUP_CH = 16
S5_STATE = 64
S5_WIDTH = S5_GROUPS * S5_GROUP_CH
S5_NSTATE = S5_GROUPS * S5_STATE
S5_BATCH_TILE = 8
S5_SCAN_SLICES = 4
GELU_TANH_SCALE = math.sqrt(2.0 / math.pi)
GELU_TANH_CUBIC = 0.044715

PG_COLS = 896
PS_COLS = 512
_IN_SEGS = (('g_q', GLA_QK), ('g_k', GLA_QK), ('g_v', GLA_WIDTH), ('g_lr', GLA_GATE_RANK), ('g_z', GLA_WIDTH),
            ('m_cq', MLA_Q_RANK), ('m_ckv', MLA_KV_RANK), ('m_kr', MLA_ROPE_DIM), ('m_z', MLA_WIDTH),
            ('s_u', S5_WIDTH), ('s_z', S5_WIDTH), ('end', 0))
PG_Q, PG_K, PG_V, PG_Z, PG_LR = slice(0, 128), slice(128, 256), slice(256, 512), slice(512, 768), slice(768, 896)
PG_QKV = slice(PG_Q.start, PG_V.stop)
IN_OFF = dict(zip([n for n, _ in _IN_SEGS], np.cumsum([0] + [w for _, w in _IN_SEGS[:-1]]).tolist()))

VMEM_LIMIT_BYTES = 48 * 1024 * 1024


def _tiles(bsz, t, past):
    n = bsz * t
    s_len = past + t
    return dict(
        row=min(512, n),
        in_row=min(1024, n) if t >= 1024 or t < 512 else 512,
        out_row=min(1024, n),
        cached_seqs=math.gcd(bsz, max(1, 128 // t)),
        gla_rows=min(512, t), gla_chunk=min(CHUNK, t),
        gla_seqs=math.gcd(bsz, max(2, 128 // t)),
        attn_q=min(512, t), attn_k=min(512, s_len),
        s5_rows=min(64, t))


def _cparams(*sem):
    return pltpu.CompilerParams(dimension_semantics=sem, vmem_limit_bytes=VMEM_LIMIT_BYTES)


def _sigmoid(x):
    return 0.5 * (1.0 + jnp.tanh(0.5 * x))


def _dot(a, b):
    return jnp.dot(a, b, preferred_element_type=F32)


def _dot_t(a, b):
    return lax.dot_general(a, b, (((0,), (0,)), ((), ())), preferred_element_type=F32)


def _const_spec(shape):
    zeros = (0,) * len(shape)
    return pl.BlockSpec(shape, lambda *_: zeros)


def _layer_spec(shape, layer, single_buffer=False):
    zeros = (0,) * len(shape)
    mode = dict(pipeline_mode=pl.Buffered(1)) if single_buffer else {}
    return pl.BlockSpec((None,) + tuple(shape), lambda *_: (layer,) + zeros, **mode)


def _rope128(x, cos_t, sin_t):
    lane = lax.broadcasted_iota(jnp.int32, x.shape, 1)
    first_half = (lane >= MLA_NOPE_DIM) & (lane < MLA_NOPE_DIM + MLA_ROPE_DIM // 2)
    rot = jnp.where(first_half, -pltpu.roll(x, LANES - MLA_ROPE_DIM // 2, 1), pltpu.roll(x, MLA_ROPE_DIM // 2, 1))
    return x * cos_t + rot * sin_t


def _mla_prep_rows(cq, ckv, kr, mla, with_kv):
    if with_kv:
        cos_ref, sin_ref, gq_ref, wq_ref, gkv_ref, wkv_ref, q_ref, ckv_ref, kpet_ref, k_ref, v_ref = mla
    else:
        cos_ref, sin_ref, gq_ref, wq_ref, gkv_ref, q_ref, ckv_ref, kpe_ref = mla
    cos_t = cos_ref[...]
    sin_t = sin_ref[...]
    ones = jnp.ones((2 * LANES, LANES), BF16)
    ms = _dot((cq * cq).astype(BF16), ones) * (1.0 / MLA_Q_RANK)
    inv = lax.rsqrt(ms + EPS)
    cqn = (cq * jnp.concatenate([inv, inv], axis=1) * gq_ref[...]).astype(BF16)
    qh = _dot(cqn, wq_ref[...])
    scale = (MLA_NOPE_DIM + MLA_ROPE_DIM) ** -0.5 * math.log2(math.e)
    for h in range(MLA_HEADS):
        x = qh[:, h * MLA_QK_PAD:(h + 1) * MLA_QK_PAD]
        roped = x * cos_t + pltpu.roll(x, LANES - MLA_ROPE_DIM, 1) * sin_t
        q_ref[:, h * MLA_QK_PAD:(h + 1) * MLA_QK_PAD] = (roped * scale).astype(BF16)
    ms = _dot((ckv * ckv).astype(BF16), ones[0:LANES]) * (1.0 / MLA_KV_RANK)
    ckv_n = ckv * lax.rsqrt(ms + EPS) * gkv_ref[...]
    ckv_ref[...] = ckv_n
    kpe128 = _rope128(kr, cos_t, sin_t)
    if with_kv:
        kpet_ref[...] = jnp.transpose(kpe128)[MLA_NOPE_DIM:MLA_NOPE_DIM + MLA_ROPE_DIM, :]
        kv = _dot(ckv_n.astype(BF16), wkv_ref[...])
        for h in range(MLA_HEADS):
            sl = slice(h * MLA_QK_PAD, (h + 1) * MLA_QK_PAD)
            k_ref[:, sl] = (kv[:, sl] + kpe128).astype(BF16)
        v_ref[...] = kv[:, MLA_QK_WIDTH:].astype(BF16)
    else:
        kpe_ref[...] = kpe128


def _inproj_rows(x, g_ref, w_scr, og_ref, mz_ref, os_ref, mla, with_kv):
    ms = jnp.mean(x * x, axis=-1, keepdims=True)
    h = (x * lax.rsqrt(ms + EPS) * g_ref[...]).astype(BF16)

    def seg(a, b):
        return lax.dot_general(h, w_scr[a:b, :], (((1,), (1,)), ((), ())), preferred_element_type=F32)

    lane = lax.broadcasted_iota(jnp.int32, (x.shape[0], LANES), 1)
    lane2 = lax.broadcasted_iota(jnp.int32, (x.shape[0], 2 * LANES), 1)
    c = IN_OFF
    og_ref[:, PG_QKV] = seg(c['g_q'], c['g_lr']).astype(BF16)
    og_ref[:, PG_Z] = seg(c['g_z'], c['m_cq']).astype(BF16)
    og_ref[:, PG_LR] = jnp.where(lane < GLA_GATE_RANK, seg(c['g_lr'], c['g_lr'] + LANES), 0.0).astype(BF16)
    mz_ref[...] = seg(c['m_z'], c['s_u']).astype(BF16)
    os_ref[...] = seg(c['s_u'], c['end']).astype(BF16)
    cq = jnp.where(lane2 < MLA_Q_RANK, seg(c['m_cq'], c['m_cq'] + 2 * LANES), 0.0)
    ckv_kr = seg(c['m_ckv'], c['m_ckv'] + 2 * LANES)
    kr = pltpu.roll(ckv_kr[:, LANES:], MLA_NOPE_DIM, 1)
    kr = jnp.where((lane >= MLA_NOPE_DIM) & (lane < MLA_NOPE_DIM + MLA_ROPE_DIM), kr, 0.0)
    _mla_prep_rows(cq, ckv_kr[:, 0:LANES], kr, mla, with_kv)


N_MLA_IN = {True: 6, False: 5}
N_MLA_OUT = {True: 5, False: 3}


def _inproj_body(x_ref, g_ref, wt_ref, *rest, with_kv):
    n_in, n_out = N_MLA_IN[with_kv], N_MLA_OUT[with_kv]
    mla_in, (og_ref, mz_ref, os_ref), mla_out, (w_scr,) = (
        rest[:n_in], rest[n_in:n_in + 3], rest[n_in + 3:n_in + 3 + n_out], rest[n_in + 3 + n_out:])

    @pl.when(pl.program_id(0) == 0)
    def _():
        w_scr[...] = wt_ref[...].astype(BF16)

    _inproj_rows(x_ref[...], g_ref, w_scr, og_ref, mz_ref, os_ref, tuple(mla_in) + tuple(mla_out), with_kv)


def _mla_specs(p, cos_t, sin_t, layer, bsz, t, tm, with_kv):
    n = bsz * t
    ntab = max(1, t // tm)
    assert not with_kv or tm <= t
    row = lambda c: pl.BlockSpec((tm, c), lambda i: (i, 0))
    table = pl.BlockSpec((tm, LANES), lambda i: (i % ntab, 0))
    args = [cos_t, sin_t, p['gq'], p['wq'], p['gkv']]
    in_specs = [table, table, _layer_spec((1, 256), layer), _layer_spec((256, MLA_QK_WIDTH), layer),
                _layer_spec((1, MLA_KV_RANK), layer)]
    out_specs = [row(MLA_QK_WIDTH), row(MLA_KV_RANK)]
    out_shape = [jax.ShapeDtypeStruct((n, MLA_QK_WIDTH), BF16), jax.ShapeDtypeStruct((n, MLA_KV_RANK), F32)]
    if with_kv:
        args.append(p['wkv'])
        in_specs.append(_layer_spec((MLA_KV_RANK, MLA_QK_WIDTH + MLA_WIDTH), layer))
        out_specs += [pl.BlockSpec((None, MLA_ROPE_DIM, tm), lambda i: (i // ntab, 0, i % ntab)),
                      row(MLA_QK_WIDTH), row(MLA_WIDTH)]
        out_shape += [jax.ShapeDtypeStruct((bsz, MLA_ROPE_DIM, t), F32),
                      jax.ShapeDtypeStruct((n, MLA_QK_WIDTH), BF16), jax.ShapeDtypeStruct((n, MLA_WIDTH), BF16)]
    else:
        out_specs.append(row(LANES))
        out_shape.append(jax.ShapeDtypeStruct((n, LANES), F32))
    return args, in_specs, out_specs, out_shape


def _proj_out(tm, n):
    row = lambda c: pl.BlockSpec((tm, c), lambda i: (i, 0))
    return ([row(PG_COLS), row(MLA_WIDTH), row(PS_COLS)],
            [jax.ShapeDtypeStruct((n, PG_COLS), BF16), jax.ShapeDtypeStruct((n, MLA_WIDTH), BF16),
             jax.ShapeDtypeStruct((n, PS_COLS), BF16)])


def _inproj(x2, p, cos_t, sin_t, layer, bsz, t, tm, with_kv):
    n = x2.shape[0]
    m_args, m_in, m_out, m_shape = _mla_specs(p, cos_t, sin_t, layer, bsz, t, tm, with_kv)
    o_specs, o_shape = _proj_out(tm, n)
    return pl.pallas_call(
        functools.partial(_inproj_body, with_kv=with_kv),
        grid=(n // tm,),
        in_specs=[pl.BlockSpec((tm, D_MODEL), lambda i: (i, 0)),
                  _layer_spec((1, D_MODEL), layer),
                  _layer_spec((IN_OFF['end'], D_MODEL), layer, single_buffer=True)] + m_in,
        scratch_shapes=[pltpu.VMEM((IN_OFF['end'], D_MODEL), BF16)],
        out_specs=o_specs + m_out,
        out_shape=o_shape + m_shape,
        compiler_params=_cparams("arbitrary"),
        name="inproj",
    )(x2, p['ln'], p['w_in'], *m_args)


def _gla_body(p_ref, wg_ref, bg_ref, gain_ref, ones_k_ref, ones_v_ref, bd_ref, s0_ref,
              o_ref, sout_ref, q_scr, k_scr, v_scr, b_scr, s_scr, qe_scr, qs_scr, ke_scr, vb_scr, o_scr,
              *, chunk, nch, nseq):
    it = pl.program_id(1)
    sub = min(GLA_SUB, chunk)
    nsub = chunk // sub

    @pl.when(it == 0)
    def _():
        s_scr[...] = jnp.zeros(s_scr.shape, F32)
        for sq in range(nseq):
            for h in range(GLA_HEADS):
                s_scr[sq, h * GLA_DK:(h + 1) * GLA_DK, h * GLA_DV:(h + 1) * GLA_DV] = s0_ref[sq, h]

    def cols(lanes):
        parts = [p_ref[sq, :, lanes] for sq in range(nseq)]
        return parts[0] if nseq == 1 else jnp.concatenate(parts, axis=0)

    q_scr[...] = cols(PG_Q).astype(F32) * (GLA_DK ** -0.5)
    k_scr[...] = cols(PG_K).astype(F32)
    v_scr[...] = cols(PG_V).astype(F32)
    logit = _dot(cols(PG_LR), wg_ref[...]) + bg_ref[...]
    log_a = (jnp.minimum(logit, 0.0) - jnp.log(1.0 + jnp.exp(-jnp.abs(logit)))) * (1.0 / GLA_GATE_TAU)
    row_in_chunk = lax.broadcasted_iota(jnp.int32, log_a.shape, 0) % chunk
    b_all = log_a
    shift = 1
    while shift < chunk:
        b_all = b_all + jnp.where(row_in_chunk >= shift, pltpu.roll(b_all, shift, 0), 0.0)
        shift *= 2
    b_scr[...] = b_all

    row = lax.broadcasted_iota(jnp.int32, (chunk, GLA_QK), 0)
    row_in_sub = lax.broadcasted_iota(jnp.int32, (sub, GLA_QK), 0)

    def decay_columns(b_end):
        col = jnp.transpose(jnp.broadcast_to(jnp.exp(b_end), (GLA_QK, GLA_QK)))
        return jnp.concatenate([col, col], axis=1)

    def load_chunk(sq, c):
        r0 = pl.multiple_of((sq * nch + c) * chunk, chunk)
        return (r0, q_scr[pl.ds(r0, chunk), :], k_scr[pl.ds(r0, chunk), :], v_scr[pl.ds(r0, chunk), :],
                b_scr[pl.ds(r0, chunk), :], b_scr[pl.ds(r0 + chunk - 1, 1), :], s_scr[sq])

    def finish_chunk(sq, r0, o, s_prev, a_state, b_end):
        ms = _dot((o * o).astype(BF16), ones_v_ref[...]) * (1.0 / GLA_DV)
        o_n = o * lax.rsqrt(ms + EPS) * gain_ref[...]
        t0 = r0 - sq * nch * chunk
        z = p_ref[sq, pl.ds(t0, chunk), PG_Z].astype(F32)
        o_ref[sq, pl.ds(t0, chunk), :] = (o_n * (z * _sigmoid(z))).astype(BF16)
        s_scr[sq] = s_prev * decay_columns(b_end) + a_state

    def robust_chunk(c, carry, sq):
        r0, qc, kc, vc, bc, b_end, s_prev = load_chunk(sq, c)
        xs = [qc * jnp.exp(bc)]
        ks = []
        for sj in range(nsub - 1):
            e_j = b_scr[pl.ds(r0 + (sj + 1) * sub - 1, 1), :]
            later = row >= (sj + 1) * sub
            xs.append(jnp.where(later, qc * jnp.exp(jnp.where(later, bc - e_j, 0.0)), 0.0))
            own = (row >= sj * sub) & (row < (sj + 1) * sub)
            ks.append(jnp.where(own, kc * jnp.exp(jnp.where(own, e_j - bc, 0.0)), 0.0))
        ks.append(kc * jnp.exp(b_end - bc))
        k_all = jnp.concatenate(ks, axis=1).astype(BF16)
        a_all = _dot_t(k_all, vc.astype(BF16)) * bd_ref[...]
        w = jnp.concatenate([s_prev, a_all[:(nsub - 1) * GLA_QK]], axis=0).astype(BF16) if nsub > 1 \
            else s_prev.astype(BF16)
        o_off = _dot(jnp.concatenate(xs, axis=1).astype(BF16), w)
        rows = []
        for si in range(nsub):
            q_i = qc[si * sub:(si + 1) * sub]
            b_i = bc[si * sub:(si + 1) * sub]
            es = []
            for j in range(sub):
                r = r0 + si * sub + j
                b_j = b_scr[pl.ds(r, 1), :]
                k_j = k_scr[pl.ds(r, 1), :]
                valid = row_in_sub >= j
                es.append(q_i * k_j * jnp.exp(jnp.where(valid, b_i - b_j, -jnp.inf)))
            e_all = jnp.concatenate(es, axis=0)
            e_hi = e_all.astype(BF16)
            e_lo = (e_all - e_hi.astype(F32)).astype(BF16)
            p_all = _dot(e_hi, ones_k_ref[...]) + _dot(e_lo, ones_k_ref[...])
            acc = o_off[si * sub:(si + 1) * sub]
            for j in range(sub):
                v_j = v_scr[pl.ds(r0 + si * sub + j, 1), :]
                acc = acc + p_all[j * sub:(j + 1) * sub] * v_j
            rows.append(acc)
        o = rows[0] if nsub == 1 else jnp.concatenate(rows, axis=0)
        finish_chunk(sq, r0, o, s_prev, a_all[(nsub - 1) * GLA_QK:], b_end)
        return carry

    lane_head_v = lax.broadcasted_iota(jnp.int32, (chunk, GLA_WIDTH), 1) // GLA_DV
    causal = (lax.broadcasted_iota(jnp.int32, (GLA_HEADS * chunk, chunk), 0) % chunk
              >= lax.broadcasted_iota(jnp.int32, (GLA_HEADS * chunk, chunk), 1))

    def plain_block():
        tt = nseq * nch * chunk
        b_all = b_scr[...]
        q_all = q_scr[...]
        k_all = k_scr[...]
        qe = q_all * jnp.exp(b_all)
        lane_head = lax.broadcasted_iota(jnp.int32, (tt, GLA_QK), 1) // GLA_DK
        qe_scr[...] = qe.astype(BF16)
        for h in range(GLA_HEADS):
            qs_scr[h] = jnp.where(lane_head == h, qe, 0.0).astype(BF16)
        ke_scr[...] = (k_all * jnp.exp(-b_all)).astype(BF16)
        vb_scr[...] = cols(PG_V)
        states = [s_scr[sq] for sq in range(nseq)]
        for c, sq in [(c, sq) for c in range(nch) for sq in range(nseq)]:
            r0 = (sq * nch + c) * chunk
            rs = slice(r0, r0 + chunk)
            s_cur = states[sq]
            qs = jnp.concatenate([qs_scr[h, rs, :] for h in range(GLA_HEADS)], axis=0)
            s = lax.dot_general(qs, ke_scr[rs, :], (((1,), (1,)), ((), ())), preferred_element_type=F32)
            s = jnp.where(causal, s, 0.0).astype(BF16)
            r = _dot(s, vb_scr[rs, :])
            o = _dot(qe_scr[rs, :], s_cur.astype(BF16))
            for h in range(GLA_HEADS):
                o = o + jnp.where(lane_head_v == h, r[h * chunk:(h + 1) * chunk], 0.0)
            o_scr[rs, :] = o
            b_end = b_scr[r0 + chunk - 1:r0 + chunk, :]
            k_end = (k_scr[rs, :] * jnp.exp(b_end - b_scr[rs, :])).astype(BF16)
            a_state = _dot_t(k_end, vb_scr[rs, :]) * bd_ref[0:GLA_QK, :]
            states[sq] = s_cur * decay_columns(b_end) + a_state
        for sq in range(nseq):
            s_scr[sq] = states[sq]
        o = o_scr[...]
        ms = _dot((o * o).astype(BF16), ones_v_ref[...]) * (1.0 / GLA_DV)
        o_n = o * lax.rsqrt(ms + EPS) * gain_ref[...]
        z = cols(PG_Z).astype(F32)
        o_all = (o_n * (z * _sigmoid(z))).astype(BF16)
        for sq in range(nseq):
            o_ref[sq] = o_all[sq * nch * chunk:(sq + 1) * nch * chunk]

    in_range = jnp.max(-b_scr[...]) < GLA_PLAIN_MAX_DECAY

    @pl.when(in_range)
    def _():
        plain_block()

    @pl.when(jnp.logical_not(in_range))
    def _():
        for sq in range(nseq):
            lax.fori_loop(0, nch, functools.partial(robust_chunk, sq=sq), 0)

    @pl.when(it == pl.num_programs(1) - 1)
    def _():
        for sq in range(nseq):
            for h in range(GLA_HEADS):
                sout_ref[sq, h] = s_scr[sq, h * GLA_DK:(h + 1) * GLA_DK, h * GLA_DV:(h + 1) * GLA_DV]


def _gla_consts(chunk):
    nsub = chunk // min(GLA_SUB, chunk)
    hk = np.arange(GLA_QK) // GLA_DK
    hv = np.arange(GLA_WIDTH) // GLA_DV
    same_kv = (hk[:, None] == hv[None, :]).astype(np.float32)
    same_vv = (hv[:, None] == hv[None, :]).astype(np.float32)
    return (jnp.asarray(same_kv, BF16), jnp.asarray(same_vv, BF16),
            jnp.asarray(np.tile(same_kv, (nsub, 1)), F32))


def _gla(pg, wg, bg, gain, layer, s0, bsz, t, tl):
    chunk, tt, nseq = tl['gla_chunk'], tl['gla_rows'], tl['gla_seqs']
    nch = tt // chunk
    nt = t // tt
    rows = nseq * tt
    nsub = chunk // min(GLA_SUB, chunk)
    ones_k, ones_v, bd = _gla_consts(chunk)
    const = _const_spec
    return pl.pallas_call(
        functools.partial(_gla_body, chunk=chunk, nch=nch, nseq=nseq),
        grid=(bsz // nseq, nt),
        in_specs=[pl.BlockSpec((nseq, tt, PG_COLS), lambda b, i: (b, i, 0)),
                  _layer_spec((GLA_QK, GLA_QK), layer), _layer_spec((1, GLA_QK), layer),
                  _layer_spec((1, GLA_WIDTH), layer),
                  const((GLA_QK, GLA_WIDTH)), const((GLA_WIDTH, GLA_WIDTH)),
                  const((nsub * GLA_QK, GLA_WIDTH)),
                  pl.BlockSpec((nseq, GLA_HEADS, GLA_DK, GLA_DV), lambda b, i: (b, 0, 0, 0))],
        out_specs=[pl.BlockSpec((nseq, tt, GLA_WIDTH), lambda b, i: (b, i, 0)),
                   pl.BlockSpec((nseq, GLA_HEADS, GLA_DK, GLA_DV), lambda b, i: (b, 0, 0, 0))],
        out_shape=[jax.ShapeDtypeStruct((bsz, t, GLA_WIDTH), BF16),
                   jax.ShapeDtypeStruct((bsz, GLA_HEADS, GLA_DK, GLA_DV), F32)],
        scratch_shapes=[pltpu.VMEM((rows, GLA_QK), F32), pltpu.VMEM((rows, GLA_QK), F32),
                        pltpu.VMEM((rows, GLA_WIDTH), F32), pltpu.VMEM((rows, GLA_QK), F32),
                        pltpu.VMEM((nseq, GLA_QK, GLA_WIDTH), F32),
                        pltpu.VMEM((rows, GLA_QK), BF16), pltpu.VMEM((GLA_HEADS, rows, GLA_QK), BF16),
                        pltpu.VMEM((rows, GLA_QK), BF16),
                        pltpu.VMEM((rows, GLA_WIDTH), BF16), pltpu.VMEM((rows, GLA_WIDTH), F32)],
        compiler_params=_cparams("parallel", "arbitrary"),
        name="gla",
    )(pg, wg, bg, gain, ones_k, ones_v, bd, s0)


def _attn_cached_body(q_ref, z_ref, ckv_new_ref, kpe_new_ref, ckv_past_ref, kpet_past_ref, wka_ref, wv_ref,
                      o_ref, kpet_scr, *, past, t, nseq):
    last = (((1,), (1,)), ((), ()))
    hsl = [slice(h * MLA_QK_PAD, (h + 1) * MLA_QK_PAD) for h in range(MLA_HEADS)]
    q_lat_h = [_dot(q_ref[:, hsl[h]], wka_ref[h]).astype(BF16) for h in range(MLA_HEADS)]
    kpet_scr[...] = jnp.zeros(kpet_scr.shape, BF16)
    o_lat = []
    for sq in range(nseq):
        rs = slice(sq * t, (sq + 1) * t)
        q_rows = jnp.concatenate([q_ref[rs, hsl[h]] for h in range(MLA_HEADS)], axis=0)
        q_lat = jnp.concatenate([q_lat_h[h][rs] for h in range(MLA_HEADS)], axis=0)
        c_past = ckv_past_ref[sq].astype(BF16)
        c_new = ckv_new_ref[rs, :].astype(BF16)
        kpet_scr[sq, MLA_NOPE_DIM:MLA_NOPE_DIM + MLA_ROPE_DIM, :] = kpet_past_ref[sq].astype(BF16)
        s_past = (lax.dot_general(q_lat, c_past, last, preferred_element_type=F32)
                  + _dot(q_rows, kpet_scr[sq]))
        s_new = (lax.dot_general(q_lat, c_new, last, preferred_element_type=F32)
                 + lax.dot_general(q_rows, kpe_new_ref[rs, :].astype(BF16), last, preferred_element_type=F32))
        if past // CHUNK != (past + t - 1) // CHUNK:
            q_chunk = (past + lax.broadcasted_iota(jnp.int32, s_past.shape, 0) % t) // CHUNK
            s_past = jnp.where(lax.broadcasted_iota(jnp.int32, s_past.shape, 1) // CHUNK <= q_chunk, s_past, -jnp.inf)
            q_chunk = (past + lax.broadcasted_iota(jnp.int32, s_new.shape, 0) % t) // CHUNK
            s_new = jnp.where((past + lax.broadcasted_iota(jnp.int32, s_new.shape, 1)) // CHUNK <= q_chunk,
                              s_new, -jnp.inf)
        m = jnp.maximum(jnp.max(s_past, axis=-1, keepdims=True), jnp.max(s_new, axis=-1, keepdims=True))
        p_past = jnp.exp2(s_past - m)
        p_new = jnp.exp2(s_new - m)
        l = jnp.sum(p_past, axis=-1, keepdims=True) + jnp.sum(p_new, axis=-1, keepdims=True)
        o_lat.append(((_dot(p_past.astype(BF16), c_past) + _dot(p_new.astype(BF16), c_new)) / l).astype(BF16))
    for h in range(MLA_HEADS):
        vsl = slice(h * MLA_V_DIM, (h + 1) * MLA_V_DIM)
        o_h = jnp.concatenate([o_lat[sq][h * t:(h + 1) * t] for sq in range(nseq)], axis=0)
        z = z_ref[:, vsl].astype(F32)
        o_ref[:, vsl] = (_dot(o_h, wv_ref[h]) * (z * _sigmoid(z))).astype(BF16)


def _attn_cached(q, mz, ckv_new, kpe128, ckv_past, kpet_past, wka, wv, layer, bsz, t, nseq):
    past = ckv_past.shape[2]
    rows = nseq * t
    return pl.pallas_call(
        functools.partial(_attn_cached_body, past=past, t=t, nseq=nseq),
        grid=(bsz // nseq,),
        in_specs=[pl.BlockSpec((rows, MLA_QK_WIDTH), lambda b: (b, 0)),
                  pl.BlockSpec((rows, MLA_WIDTH), lambda b: (b, 0)),
                  pl.BlockSpec((rows, MLA_KV_RANK), lambda b: (b, 0)),
                  pl.BlockSpec((rows, LANES), lambda b: (b, 0)),
                  pl.BlockSpec((None, nseq, past, MLA_KV_RANK), lambda b: (layer, b, 0, 0)),
                  pl.BlockSpec((None, nseq, MLA_ROPE_DIM, past), lambda b: (layer, b, 0, 0)),
                  _layer_spec((MLA_HEADS, MLA_QK_PAD, MLA_KV_RANK), layer),
                  _layer_spec((MLA_HEADS, MLA_KV_RANK, MLA_V_DIM), layer)],
        out_specs=pl.BlockSpec((rows, MLA_WIDTH), lambda b: (b, 0)),
        out_shape=jax.ShapeDtypeStruct((bsz * t, MLA_WIDTH), BF16),
        scratch_shapes=[pltpu.VMEM((nseq, MLA_QK_PAD, past), BF16)],
        compiler_params=_cparams("parallel"),
        name="mla_attn_cached",
    )(q, mz, ckv_new, kpe128, ckv_past, kpet_past, wka, wv)


def _attn_body(q_ref, k_ref, v_ref, z_ref, o_ref, m_scr, acc_scr, *, past, tq, tk, s_len):
    iq = pl.program_id(1)
    q_first = past + iq * tq
    full_keys = jnp.minimum((q_first // CHUNK + 1) * CHUNK, s_len)
    vis_keys = jnp.minimum(((q_first + tq - 1) // CHUNK + 1) * CHUNK, s_len)
    n_full = full_keys // tk
    n_vis = (vis_keys + tk - 1) // tk
    m_scr[...] = jnp.full(m_scr.shape, -jnp.inf, F32)
    acc_scr[...] = jnp.zeros(acc_scr.shape, F32)
    split_diagonal = past % tk == 0 and tq == tk and (tq // 2) % CHUNK == 0

    def block(kb, carry, masked):
        k0 = pl.multiple_of(kb * tk, tk)
        if masked and split_diagonal:
            parts = [(r * (tq // 2), tq // 2, (r + 1) * (tk // 2)) for r in range(2)]
        else:
            parts = [(0, tq, tk)]
        for r0, nr, kext in parts:
            rows = slice(r0, r0 + nr)
            if masked:
                q_pos = (r0 if split_diagonal else q_first + r0) + lax.broadcasted_iota(jnp.int32, (nr, kext), 0)
                k_pos = (0 if split_diagonal else k0) + lax.broadcasted_iota(jnp.int32, (nr, kext), 1)
                visible = k_pos // CHUNK <= q_pos // CHUNK
            ntile, rem = kext // LANES, kext % LANES
            ones_v = jnp.ones((kext, MLA_V_DIM), BF16)
            for h in range(MLA_HEADS):
                sl = slice(h * MLA_QK_PAD, (h + 1) * MLA_QK_PAD)
                vsl = slice(h * MLA_V_DIM, (h + 1) * MLA_V_DIM)
                s = lax.dot_general(q_ref[rows, sl], k_ref[pl.ds(k0, kext), sl], (((1,), (1,)), ((), ())),
                                    preferred_element_type=F32)
                if masked:
                    s = jnp.where(visible, s, -jnp.inf)
                m_prev = m_scr[h, rows]
                m_new = jnp.maximum(m_prev, jnp.max(s, axis=-1, keepdims=True))
                alpha = jnp.exp2(m_prev - m_new)
                ps = [jnp.exp2(s[:, c * LANES:(c + 1) * LANES] - m_new) for c in range(ntile)]
                if rem:
                    ps.append(jnp.exp2(s[:, ntile * LANES:] - m_new[:, :rem]))
                p = jnp.concatenate(ps, axis=1).astype(BF16)
                v_ext = jnp.concatenate([v_ref[pl.ds(k0, kext), vsl], ones_v], axis=1)
                acc_scr[h, rows] = jnp.concatenate([alpha, alpha], axis=1) * acc_scr[h, rows] + _dot(p, v_ext)
                m_scr[h, rows] = m_new
        return carry

    lax.fori_loop(0, n_full, functools.partial(block, masked=False), 0)
    lax.fori_loop(n_full, n_vis, functools.partial(block, masked=True), 0)
    for h in range(MLA_HEADS):
        vsl = slice(h * MLA_V_DIM, (h + 1) * MLA_V_DIM)
        z = z_ref[:, vsl].astype(F32)
        acc = acc_scr[h]
        o_ref[:, vsl] = (acc[:, :MLA_V_DIM] / acc[:, MLA_V_DIM:] * (z * _sigmoid(z))).astype(BF16)


def _attn(q, k, v, mz, bsz, t, s_len, past, tq, tk):
    nq = t // tq
    return pl.pallas_call(
        functools.partial(_attn_body, past=past, tq=tq, tk=tk, s_len=s_len),
        grid=(bsz, nq),
        in_specs=[pl.BlockSpec((tq, MLA_QK_WIDTH), lambda b, iq: (b * nq + iq, 0)),
                  pl.BlockSpec((s_len, MLA_QK_WIDTH), lambda b, iq: (b, 0)),
                  pl.BlockSpec((s_len, MLA_WIDTH), lambda b, iq: (b, 0)),
                  pl.BlockSpec((tq, MLA_WIDTH), lambda b, iq: (b * nq + iq, 0))],
        out_specs=pl.BlockSpec((tq, MLA_WIDTH), lambda b, iq: (b * nq + iq, 0)),
        out_shape=jax.ShapeDtypeStruct((bsz * t, MLA_WIDTH), BF16),
        scratch_shapes=[pltpu.VMEM((MLA_HEADS, tq, LANES), F32), pltpu.VMEM((MLA_HEADS, tq, 2 * MLA_V_DIM), F32)],
        compiler_params=_cparams("parallel", "arbitrary"),
        name="mla_attn",
    )(q, k, v, mz)


def _s5_weights_body(bre_ref, bim_ref, cre_ref, cim_ref, spread_ref, mask_ref, wb_ref, wc_ref):
    def expand(m_ref):
        return _dot(m_ref[...].astype(BF16), spread_ref[...]) * mask_ref[...]

    wb_ref[:, 0:S5_NSTATE] = expand(bre_ref).astype(BF16)
    wb_ref[:, S5_NSTATE:] = expand(bim_ref).astype(BF16)
    wc_ref[0:S5_NSTATE, :] = jnp.transpose(expand(cre_ref)).astype(BF16)
    wc_ref[S5_NSTATE:, :] = jnp.transpose(-expand(cim_ref)).astype(BF16)


def _s5_weights(bbr, bbi, c_re, c_im):
    depth = bbr.shape[0]
    g_row = np.arange(S5_WIDTH) // S5_GROUP_CH
    g_col = np.arange(S5_NSTATE) // S5_STATE
    mask = (g_row[:, None] == g_col[None, :]).astype(np.float32)
    spread = (np.arange(S5_STATE)[:, None] == (np.arange(S5_NSTATE) % S5_STATE)[None, :]).astype(np.float32)
    small = pl.BlockSpec((None, S5_WIDTH, S5_STATE), lambda l: (l, 0, 0))
    return pl.pallas_call(
        _s5_weights_body,
        grid=(depth,),
        in_specs=[small, small, small, small,
                  _const_spec((S5_STATE, S5_NSTATE)), _const_spec((S5_WIDTH, S5_NSTATE))],
        out_specs=[pl.BlockSpec((None, S5_WIDTH, 2 * S5_NSTATE), lambda l: (l, 0, 0)),
                   pl.BlockSpec((None, 2 * S5_NSTATE, S5_WIDTH), lambda l: (l, 0, 0))],
        out_shape=[jax.ShapeDtypeStruct((depth, S5_WIDTH, 2 * S5_NSTATE), BF16),
                   jax.ShapeDtypeStruct((depth, 2 * S5_NSTATE, S5_WIDTH), BF16)],
        compiler_params=_cparams("parallel"),
        name="s5_weights",
    )(bbr, bbi, c_re, c_im, jnp.asarray(spread, BF16), jnp.asarray(mask, F32))


def _s5_body(p0_ref, pn_ref, x0r_ref, x0i_ref, lre_ref, lim_ref, wb_ref, wc_ref, d_ref, wglu_ref, bglu_ref, never_ref,
             o_ref, xr_out, xi_out, uz_bt, uz_a, uz_b, uz_c, bu_a, bu_b, bu_c, o_tb, xr_s, xi_s, *, lc, pitch):
    it = pl.program_id(1)
    nb = S5_BATCH_TILE
    ring = ((uz_a, bu_a), (uz_b, bu_b), (uz_c, bu_c))

    def stage_in(blk_ref, uz_tb, bu):
        for b in range(nb):
            for c in range(PS_COLS // LANES):
                uz_bt[c, b * pitch:b * pitch + lc, :] = blk_ref[b, :, c * LANES:(c + 1) * LANES].astype(F32)
        for t in range(lc):
            for c in range(PS_COLS // LANES):
                uz_tb[t * nb:(t + 1) * nb, c * LANES:(c + 1) * LANES] = uz_bt[c, pl.ds(t, nb, stride=pitch), :]
        bu[...] = _dot(uz_tb[:, 0:S5_WIDTH].astype(BF16), wb_ref[...])

    def stage_scan(bu):
        w = S5_NSTATE // S5_SCAN_SLICES
        never = never_ref[...] != 0
        last = None
        for c0 in range(0, S5_NSTATE, w):
            re, im = slice(c0, c0 + w), slice(S5_NSTATE + c0, S5_NSTATE + c0 + w)
            lre = jnp.broadcast_to(lre_ref[:, re], (nb, w))
            lim = jnp.broadcast_to(lim_ref[:, re], (nb, w))
            xr, xi = xr_s[:, re], xi_s[:, re]
            if last is not None:
                xr = jnp.where(never, last, xr)
            for t in range(lc):
                rs = slice(t * nb, (t + 1) * nb)
                xr, xi = lre * xr - lim * xi + bu[rs, re], lre * xi + lim * xr + bu[rs, im]
                bu[rs, re] = xr
                bu[rs, im] = xi
            xr_s[:, re] = xr
            xi_s[:, re] = xi
            last = xr

    def stage_out(uz_tb, xs):
        y = _dot(xs[...].astype(BF16), wc_ref[...]) + d_ref[...] * uz_tb[:, 0:S5_WIDTH]
        g5 = 0.5 * y * (1.0 + jnp.tanh(GELU_TANH_SCALE * (y + GELU_TANH_CUBIC * (y * y * y))))
        gate = _sigmoid(_dot(g5.astype(BF16), wglu_ref[...]) + bglu_ref[...])
        z = uz_tb[:, S5_WIDTH:2 * S5_WIDTH]
        o = g5 * gate * (z * _sigmoid(z))
        for c in range(S5_WIDTH // LANES):
            o_tb[c] = o[:, c * LANES:(c + 1) * LANES]
        for b in range(nb):
            for c in range(S5_WIDTH // LANES):
                o_ref[b, :, c * LANES:(c + 1) * LANES] = o_tb[c, pl.ds(b, lc, stride=nb), :].astype(BF16)

    @pl.when(it == 0)
    def _():
        xr_s[...] = x0r_ref[...]
        xi_s[...] = x0i_ref[...]
        stage_in(p0_ref, uz_a, bu_a)
        uz_c[...] = jnp.zeros(uz_c.shape, F32)
        bu_c[...] = jnp.zeros(bu_c.shape, F32)

    for r in range(3):
        @pl.when(it % 3 == r)
        def _(r=r):
            stage_in(pn_ref, *ring[(r + 1) % 3])
            stage_scan(ring[r][1])
            stage_out(*ring[(r + 2) % 3])

    @pl.when(it == pl.num_programs(1) - 2)
    def _():
        xr_out[...] = xr_s[...]
        xi_out[...] = xi_s[...]


def _s5(ps3, x0r, x0i, lre, lim, wb, wc, d, wglu, bglu, layer, bsz, t, lc):
    nb = S5_BATCH_TILE
    nt = t // lc
    pitch = lc + 8
    rows = lc * nb
    return pl.pallas_call(
        functools.partial(_s5_body, lc=lc, pitch=pitch),
        grid=(bsz // nb, nt + 1),
        in_specs=[pl.BlockSpec((nb, lc, PS_COLS), lambda g, i: (g, 0, 0)),
                  pl.BlockSpec((nb, lc, PS_COLS), lambda g, i: (g, jnp.minimum(i + 1, nt - 1), 0)),
                  pl.BlockSpec((nb, S5_NSTATE), lambda g, i: (g, 0)),
                  pl.BlockSpec((nb, S5_NSTATE), lambda g, i: (g, 0)),
                  _layer_spec((1, S5_NSTATE), layer), _layer_spec((1, S5_NSTATE), layer),
                  _layer_spec((S5_WIDTH, 2 * S5_NSTATE), layer), _layer_spec((2 * S5_NSTATE, S5_WIDTH), layer),
                  _layer_spec((1, S5_WIDTH), layer), _layer_spec((S5_WIDTH, S5_WIDTH), layer),
                  _layer_spec((1, S5_WIDTH), layer),
                  _const_spec((1, S5_NSTATE // S5_SCAN_SLICES))],
        out_specs=[pl.BlockSpec((nb, lc, S5_WIDTH), lambda g, i: (g, jnp.maximum(i - 1, 0), 0)),
                   pl.BlockSpec((nb, S5_NSTATE), lambda g, i: (g, 0)),
                   pl.BlockSpec((nb, S5_NSTATE), lambda g, i: (g, 0))],
        out_shape=[jax.ShapeDtypeStruct((bsz, t, S5_WIDTH), BF16),
                   jax.ShapeDtypeStruct((bsz, S5_NSTATE), F32),
                   jax.ShapeDtypeStruct((bsz, S5_NSTATE), F32)],
        scratch_shapes=[pltpu.VMEM((PS_COLS // LANES, nb * pitch, LANES), F32)]
                       + [pltpu.VMEM((rows, PS_COLS), F32)] * 3
                       + [pltpu.VMEM((rows, 2 * S5_NSTATE), F32)] * 3
                       + [pltpu.VMEM((S5_WIDTH // LANES, rows, LANES), F32),
                          pltpu.VMEM((nb, S5_NSTATE), F32), pltpu.VMEM((nb, S5_NSTATE), F32)],
        compiler_params=_cparams("parallel", "arbitrary"),
        name="s5",
    )(ps3, ps3, x0r, x0i, lre, lim, wb, wc, d, wglu, bglu, jnp.zeros((1, S5_NSTATE // S5_SCAN_SLICES), jnp.int32))


def _outproj_rows(x, og_ref, om_ref, os_ref, wo_scr):
    acc = _dot(og_ref[...], wo_scr[0:GLA_WIDTH, :])
    acc += _dot(om_ref[...], wo_scr[GLA_WIDTH:GLA_WIDTH + MLA_WIDTH, :])
    acc += _dot(os_ref[...], wo_scr[GLA_WIDTH + MLA_WIDTH:, :])
    return x + acc


def _outproj_body(x_ref, og_ref, om_ref, os_ref, w_ref, g_ref, o_ref, wo_scr, *, final):
    @pl.when(pl.program_id(0) == 0)
    def _():
        wo_scr[...] = w_ref[...].astype(BF16)

    xn = _outproj_rows(x_ref[...], og_ref, om_ref, os_ref, wo_scr)
    if final:
        ms = jnp.mean(xn * xn, axis=-1, keepdims=True)
        xn = xn * lax.rsqrt(ms + EPS) * g_ref[...]
    o_ref[...] = xn


def _out_in_body(x_ref, og_ref, om_ref, os_ref, wo_ref, g_ref, wt_ref, *rest, with_kv, n_prev):
    prev, rest = rest[:n_prev], rest[n_prev:]
    n_in, n_out = N_MLA_IN[with_kv], N_MLA_OUT[with_kv]
    mla_in, (xo_ref, pg_ref, mz_ref, ps_ref), mla_out, (wo_scr, w_scr) = (
        rest[:n_in], rest[n_in:n_in + 4], rest[n_in + 4:n_in + 4 + n_out], rest[n_in + 4 + n_out:])
    if n_prev:
        ckv_all = mla_out[1]
        for j in range(n_prev):
            ckv_all[j] = prev[j][...]
        mla_out = (mla_out[0], ckv_all.at[n_prev]) + tuple(mla_out[2:])

    @pl.when(pl.program_id(0) == 0)
    def _():
        wo_scr[...] = wo_ref[...].astype(BF16)
        w_scr[...] = wt_ref[...].astype(BF16)

    xn = _outproj_rows(x_ref[...], og_ref, om_ref, os_ref, wo_scr)
    xo_ref[...] = xn
    _inproj_rows(xn, g_ref, w_scr, pg_ref, mz_ref, ps_ref, tuple(mla_in) + tuple(mla_out), with_kv)


def _out_in(x2, og, om, os_, p, cos_t, sin_t, layer, bsz, t, tm, with_kv, ckv_prev=()):
    n = x2.shape[0]
    row = lambda c: pl.BlockSpec((tm, c), lambda i: (i, 0))
    m_args, m_in, m_out, m_shape = _mla_specs(p, cos_t, sin_t, layer + 1, bsz, t, tm, with_kv)
    n_prev = len(ckv_prev)
    if n_prev:
        m_out[1] = pl.BlockSpec((n_prev + 1, tm, MLA_KV_RANK), lambda i: (0, i, 0))
        m_shape[1] = jax.ShapeDtypeStruct((n_prev + 1, n, MLA_KV_RANK), F32)
    o_specs, o_shape = _proj_out(tm, n)
    return pl.pallas_call(
        functools.partial(_out_in_body, with_kv=with_kv, n_prev=n_prev),
        grid=(n // tm,),
        in_specs=[row(D_MODEL), row(GLA_WIDTH), row(MLA_WIDTH), row(S5_WIDTH),
                  _layer_spec((D_MODEL, D_MODEL), layer, single_buffer=True),
                  _layer_spec((1, D_MODEL), layer + 1),
                  _layer_spec((IN_OFF['end'], D_MODEL), layer + 1, single_buffer=True)]
                 + [row(MLA_KV_RANK)] * n_prev + m_in,
        out_specs=[row(D_MODEL)] + o_specs + m_out,
        out_shape=[jax.ShapeDtypeStruct((n, D_MODEL), F32)] + o_shape + m_shape,
        scratch_shapes=[pltpu.VMEM((D_MODEL, D_MODEL), BF16), pltpu.VMEM((IN_OFF['end'], D_MODEL), BF16)],
        compiler_params=_cparams("arbitrary"),
        name="outproj_inproj",
    )(x2, og, om, os_, p['w_out'], p['ln'], p['w_in'], *ckv_prev, *m_args)


def _outproj(x2, og, om, os_, w, gain, layer, tm, final):
    n = x2.shape[0]
    row = lambda c: pl.BlockSpec((tm, c), lambda i: (i, 0))
    return pl.pallas_call(
        functools.partial(_outproj_body, final=final),
        grid=(n // tm,),
        in_specs=[row(D_MODEL), row(GLA_WIDTH), row(MLA_WIDTH), row(S5_WIDTH),
                  _layer_spec((D_MODEL, D_MODEL), layer, single_buffer=True),
                  _const_spec((1, D_MODEL))],
        out_specs=row(D_MODEL),
        out_shape=jax.ShapeDtypeStruct((n, D_MODEL), F32),
        scratch_shapes=[pltpu.VMEM((D_MODEL, D_MODEL), BF16)],
        compiler_params=_cparams("arbitrary"),
        name="outproj_final" if final else "outproj",
    )(x2, og, om, os_, w, gain)


def _prepare_params(ln_gain, w_in, gla_w_gate, gla_b_gate, gla_norm_gain, mla_q_norm_gain, mla_w_uq,
                    mla_kv_norm_gain, mla_w_ukv, s5_lambda_re, s5_lambda_im, s5_b_re, s5_b_im, s5_c_re, s5_c_im,
                    s5_d, s5_log_dt, s5_w_glu, s5_b_glu, w_out):
    depth = w_in.shape[0]
    w_t = jnp.swapaxes(w_in, 1, 2)
    wg = jnp.pad(gla_w_gate, ((0, 0), (0, GLA_QK - GLA_GATE_RANK), (0, 0))).astype(BF16)
    wq = mla_w_uq.reshape(depth, MLA_Q_RANK, MLA_HEADS, MLA_NOPE_DIM + MLA_ROPE_DIM)
    half = MLA_ROPE_DIM // 2
    wq = jnp.concatenate([wq, -wq[..., MLA_NOPE_DIM + half:], wq[..., MLA_NOPE_DIM:MLA_NOPE_DIM + half]], axis=-1)
    wq = jnp.pad(wq, ((0, 0), (0, 256 - MLA_Q_RANK), (0, 0), (0, 0)))
    wq = wq.reshape(depth, 256, MLA_QK_WIDTH).astype(BF16)
    gq = jnp.pad(mla_q_norm_gain, ((0, 0), (0, 256 - MLA_Q_RANK))).reshape(depth, 1, 256)
    wkv = mla_w_ukv.reshape(depth, MLA_KV_RANK, MLA_HEADS, MLA_NOPE_DIM + MLA_V_DIM)
    wk = jnp.pad(wkv[..., :MLA_NOPE_DIM], ((0, 0), (0, 0), (0, 0), (0, MLA_QK_PAD - MLA_NOPE_DIM)))
    wkv_r = jnp.concatenate([wk.reshape(depth, MLA_KV_RANK, MLA_QK_WIDTH),
                             wkv[..., MLA_NOPE_DIM:].reshape(depth, MLA_KV_RANK, MLA_WIDTH)], axis=2).astype(BF16)
    wka = jnp.pad(jnp.transpose(wkv[..., :MLA_NOPE_DIM], (0, 2, 3, 1)),
                  ((0, 0), (0, 0), (0, MLA_QK_PAD - MLA_NOPE_DIM), (0, 0))).astype(BF16)
    wv = jnp.transpose(wkv[..., MLA_NOPE_DIM:], (0, 2, 1, 3)).astype(BF16)
    dt = jnp.exp(s5_log_dt)[:, :, None]
    mag = jnp.exp(s5_lambda_re * dt)
    lbr, lbi = mag * jnp.cos(s5_lambda_im * dt), mag * jnp.sin(s5_lambda_im * dt)
    den = s5_lambda_re * s5_lambda_re + s5_lambda_im * s5_lambda_im
    qr = ((lbr - 1.0) * s5_lambda_re + lbi * s5_lambda_im) / den
    qi = (lbi * s5_lambda_re - (lbr - 1.0) * s5_lambda_im) / den
    b_re_t, b_im_t = jnp.swapaxes(s5_b_re, 2, 3), jnp.swapaxes(s5_b_im, 2, 3)
    bbr = qr[:, :, None, :] * b_re_t - qi[:, :, None, :] * b_im_t
    bbi = qr[:, :, None, :] * b_im_t + qi[:, :, None, :] * b_re_t
    rows = lambda m: m.reshape(depth, S5_WIDTH, S5_STATE)
    wb, wc = _s5_weights(rows(bbr), rows(bbi), rows(s5_c_re), rows(s5_c_im))
    return dict(
        ln=ln_gain.reshape(depth, 1, D_MODEL), w_in=w_t, wg=wg, bg=gla_b_gate.reshape(depth, 1, GLA_QK),
        gla_gain=jnp.tile(gla_norm_gain, (1, GLA_HEADS)).reshape(depth, 1, GLA_WIDTH),
        gq=gq, wq=wq, gkv=mla_kv_norm_gain.reshape(depth, 1, MLA_KV_RANK), wkv=wkv_r, wka=wka, wv=wv,
        lre=lbr.reshape(depth, 1, S5_NSTATE), lim=lbi.reshape(depth, 1, S5_NSTATE),
        wb=wb, wc=wc, d=s5_d.reshape(depth, 1, S5_WIDTH), wglu=s5_w_glu.astype(BF16),
        bglu=s5_b_glu.reshape(depth, 1, S5_WIDTH), w_out=w_out)


def _rope_tables(past, t, reps):
    half = MLA_ROPE_DIM // 2
    inv = ROPE_BASE ** (-np.arange(half, dtype=np.float64) / half)
    ang = (past + np.arange(t, dtype=np.float64))[:, None] * inv[None, :]
    cos, sin = np.cos(ang), np.sin(ang)
    pad = MLA_QK_PAD - MLA_NOPE_DIM - MLA_ROPE_DIM
    cos_t = np.concatenate([np.ones((t, MLA_NOPE_DIM)), cos, cos, np.zeros((t, pad))], axis=1)
    sin_t = np.concatenate([np.zeros((t, MLA_NOPE_DIM)), sin, sin, np.zeros((t, pad))], axis=1)
    return jnp.asarray(np.tile(cos_t, (reps, 1)), F32), jnp.asarray(np.tile(sin_t, (reps, 1)), F32)


def _trunk(x, gla_state, ckv_cache, kpe_cache, s5_re, s5_im, p, final_gain):
    bsz, t, _ = x.shape
    n = bsz * t
    depth = p['w_in'].shape[0]
    past = 0 if ckv_cache is None else ckv_cache.shape[2]
    s_len = past + t
    tl = _tiles(bsz, t, past)
    cos_t, sin_t = _rope_tables(past, t, max(1, max(tl['row'], tl['in_row']) // t))
    kpet_cache = None if kpe_cache is None else jnp.swapaxes(kpe_cache, 2, 3)
    x2 = x.reshape(n, D_MODEL)
    gain_f = final_gain.reshape(1, D_MODEL)
    gla_o, ckv_o, kpe_o, re_o, im_o = [], [], [], [], []
    with_kv = past == 0
    proj = _inproj(x2, p, cos_t, sin_t, 0, bsz, t, tl['in_row'], with_kv)
    ckv_stack = None
    for l in range(depth):
        pg, mz, ps, q, ckv_new = proj[:5]
        if ckv_new.ndim == 3:
            ckv_stack, ckv_new = ckv_new, ckv_new[l]
        s0 = jnp.zeros((bsz, GLA_HEADS, GLA_DK, GLA_DV), F32) if gla_state is None else gla_state[l]
        o_gla, s_new = _gla(pg.reshape(bsz, t, PG_COLS), p['wg'], p['bg'], p['gla_gain'], l, s0, bsz, t, tl)
        o_gla = o_gla.reshape(n, GLA_WIDTH)
        gla_o.append(s_new)
        if with_kv:
            kpet, k_cat, v_all = proj[5:]
            kpe_o.append(jnp.swapaxes(kpet, 1, 2))
            o_mla = _attn(q, k_cat, v_all, mz, bsz, t, s_len, past, tl['attn_q'], tl['attn_k'])
        else:
            kpe128 = proj[5]
            kpe_o.append(kpe128[:, MLA_NOPE_DIM:MLA_NOPE_DIM + MLA_ROPE_DIM].reshape(bsz, t, MLA_ROPE_DIM))
            o_mla = _attn_cached(q, mz, ckv_new, kpe128, ckv_cache, kpet_cache, p['wka'], p['wv'], l, bsz, t,
                                 tl['cached_seqs'])
        ckv_o.append(ckv_new)
        x0r = jnp.zeros((bsz, S5_NSTATE), F32) if s5_re is None else s5_re[l].reshape(bsz, S5_NSTATE)
        x0i = jnp.zeros((bsz, S5_NSTATE), F32) if s5_im is None else s5_im[l].reshape(bsz, S5_NSTATE)
        o_s5, xr, xi = _s5(ps.reshape(bsz, t, PS_COLS), x0r, x0i, p['lre'], p['lim'], p['wb'], p['wc'],
                           p['d'], p['wglu'], p['bglu'], l, bsz, t, tl['s5_rows'])
        re_o.append(xr.reshape(bsz, S5_GROUPS, S5_STATE))
        im_o.append(xi.reshape(bsz, S5_GROUPS, S5_STATE))
        o_s5 = o_s5.reshape(n, S5_WIDTH)
        if l < depth - 1:
            stack_here = l + 1 == depth - 1
            x2, *proj = _out_in(x2, o_gla, o_mla, o_s5, p, cos_t, sin_t, l, bsz, t, tl['row'], with_kv,
                                ckv_prev=tuple(ckv_o) if stack_here else ())
        else:
            x2 = _outproj(x2, o_gla, o_mla, o_s5, p['w_out'], gain_f, l, tl['out_row'], final=True)
    ckv_all = jnp.stack(ckv_o) if ckv_stack is None else ckv_stack
    return (x2.reshape(bsz, t, D_MODEL), jnp.stack(gla_o), ckv_all.reshape(depth, bsz, t, MLA_KV_RANK),
            jnp.stack(kpe_o), jnp.stack(re_o), jnp.stack(im_o))


def kernel(x_prompt, x_sample, state_gla, cache_mla_ckv, cache_mla_kpe, state_s5_re, state_s5_im, ln_gain, w_in, gla_w_gate, gla_b_gate, gla_norm_gain, mla_q_norm_gain, mla_w_uq, mla_kv_norm_gain, mla_w_ukv, s5_lambda_re, s5_lambda_im, s5_b_re, s5_b_im, s5_c_re, s5_c_im, s5_d, s5_log_dt, s5_w_glu, s5_b_glu, w_out, final_gain):
    p = _prepare_params(ln_gain, w_in, gla_w_gate, gla_b_gate, gla_norm_gain, mla_q_norm_gain, mla_w_uq,
                        mla_kv_norm_gain, mla_w_ukv, s5_lambda_re, s5_lambda_im, s5_b_re, s5_b_im,
                        s5_c_re, s5_c_im, s5_d, s5_log_dt, s5_w_glu, s5_b_glu, w_out)
    y_p, gla_p, ckv_p, kpe_p, re_p, im_p = _trunk(x_prompt, None, None, None, None, None, p, final_gain)
    y_s, gla_s, ckv_s, kpe_s, re_s, im_s = _trunk(x_sample, state_gla, cache_mla_ckv, cache_mla_kpe,
                                                  state_s5_re, state_s5_im, p, final_gain)
    return (y_p, y_s, gla_p, ckv_p, kpe_p, re_p, im_p, gla_s, ckv_s, kpe_s, re_s, im_s)
```

```python
import functools
import math

import numpy as np
import jax
import jax.numpy as jnp
from jax import lax
from jax.experimental import pallas as pl
from jax.experimental.pallas import tpu as pltpu

F32 = jnp.float32
BF16 = jnp.bfloat16

LANES = 128
D_MODEL = 1024
CHUNK = 64
EPS = 1e-6
GLA_HEADS = 4
GLA_DV = 64
GLA_DK = 32
GLA_WIDTH = GLA_HEADS * GLA_DV
GLA_QK = GLA_HEADS * GLA_DK
GLA_GATE_RANK = 16
GLA_GATE_TAU = 16.0
GLA_SUB = 16
GLA_PLAIN_MAX_DECAY = 60.0
MLA_HEADS = 4
MLA_NOPE_DIM = 64
MLA_ROPE_DIM = 32
MLA_V_DIM = 128
MLA_Q_RANK = 192
MLA_KV_RANK = 128
MLA_WIDTH = MLA_HEADS * MLA_V_DIM
MLA_QK_PAD = 128
MLA_QK_WIDTH = MLA_HEADS * MLA_QK_PAD
ROPE_BASE = 10000.0
S5_GROUPS = 16
S5_GROUP_CH = 16
S5_STATE = 64
S5_WIDTH = S5_GROUPS * S5_GROUP_CH
S5_NSTATE = S5_GROUPS * S5_STATE
S5_BATCH_TILE = 8
S5_SCAN_SLICES = 4
GELU_TANH_SCALE = math.sqrt(2.0 / math.pi)
GELU_TANH_CUBIC = 0.044715

PG_COLS = 896
PS_COLS = 512
_IN_SEGS = (('g_q', GLA_QK), ('g_k', GLA_QK), ('g_v', GLA_WIDTH), ('g_lr', GLA_GATE_RANK), ('g_z', GLA_WIDTH),
            ('m_cq', MLA_Q_RANK), ('m_ckv', MLA_KV_RANK), ('m_kr', MLA_ROPE_DIM), ('m_z', MLA_WIDTH),
            ('s_u', S5_WIDTH), ('s_z', S5_WIDTH), ('end', 0))
PG_Q, PG_K, PG_V, PG_Z, PG_LR = slice(0, 128), slice(128, 256), slice(256, 512), slice(512, 768), slice(768, 896)
PG_QKV = slice(PG_Q.start, PG_V.stop)
IN_OFF = dict(zip([n for n, _ in _IN_SEGS], np.cumsum([0] + [w for _, w in _IN_SEGS[:-1]]).tolist()))

VMEM_LIMIT_BYTES = 48 * 1024 * 1024


def _tiles(bsz, t, past):
    n = bsz * t
    s_len = past + t
    return dict(
        row=min(512, n),
        in_row=min(1024, n) if t >= 1024 or t < 512 else 512,
        out_row=min(1024, n),
        cached_seqs=math.gcd(bsz, max(1, 128 // t)),
        gla_rows=min(512, t), gla_chunk=min(CHUNK, t),
        gla_seqs=math.gcd(bsz, max(2, 128 // t)),
        attn_q=min(512, t), attn_k=min(512, s_len),
        s5_rows=min(64, t))


def _cparams(*sem):
    return pltpu.CompilerParams(dimension_semantics=sem, vmem_limit_bytes=VMEM_LIMIT_BYTES)


def _sigmoid(x):
    return 0.5 * (1.0 + jnp.tanh(0.5 * x))


def _dot(a, b):
    return jnp.dot(a, b, preferred_element_type=F32)


def _dot_t(a, b):
    return lax.dot_general(a, b, (((0,), (0,)), ((), ())), preferred_element_type=F32)


def _const_spec(shape):
    zeros = (0,) * len(shape)
    return pl.BlockSpec(shape, lambda *_: zeros)


def _layer_spec(shape, layer, single_buffer=False):
    zeros = (0,) * len(shape)
    mode = dict(pipeline_mode=pl.Buffered(1)) if single_buffer else {}
    return pl.BlockSpec((None,) + tuple(shape), lambda *_: (layer,) + zeros, **mode)


def _rope128(x, cos_t, sin_t):
    lane = lax.broadcasted_iota(jnp.int32, x.shape, 1)
    first_half = (lane >= MLA_NOPE_DIM) & (lane < MLA_NOPE_DIM + MLA_ROPE_DIM // 2)
    rot = jnp.where(first_half, -pltpu.roll(x, LANES - MLA_ROPE_DIM // 2, 1), pltpu.roll(x, MLA_ROPE_DIM // 2, 1))
    return x * cos_t + rot * sin_t


def _mla_prep_rows(cq, ckv, kr, mla, with_kv):
    if with_kv:
        cos_ref, sin_ref, gq_ref, wq_ref, gkv_ref, wkv_ref, q_ref, ckv_ref, kpet_ref, k_ref, v_ref = mla
    else:
        cos_ref, sin_ref, gq_ref, wq_ref, gkv_ref, q_ref, ckv_ref, kpe_ref = mla
    cos_t = cos_ref[...]
    sin_t = sin_ref[...]
    ones = jnp.ones((2 * LANES, LANES), BF16)
    ms = _dot((cq * cq).astype(BF16), ones) * (1.0 / MLA_Q_RANK)
    inv = lax.rsqrt(ms + EPS)
    cqn = (cq * jnp.concatenate([inv, inv], axis=1) * gq_ref[...]).astype(BF16)
    qh = _dot(cqn, wq_ref[...])
    scale = (MLA_NOPE_DIM + MLA_ROPE_DIM) ** -0.5 * math.log2(math.e)
    for h in range(MLA_HEADS):
        x = qh[:, h * MLA_QK_PAD:(h + 1) * MLA_QK_PAD]
        roped = x * cos_t + pltpu.roll(x, LANES - MLA_ROPE_DIM, 1) * sin_t
        q_ref[:, h * MLA_QK_PAD:(h + 1) * MLA_QK_PAD] = (roped * scale).astype(BF16)
    ms = _dot((ckv * ckv).astype(BF16), ones[0:LANES]) * (1.0 / MLA_KV_RANK)
    ckv_n = ckv * lax.rsqrt(ms + EPS) * gkv_ref[...]
    ckv_ref[...] = ckv_n
    kpe128 = _rope128(kr, cos_t, sin_t)
    if with_kv:
        kpet_ref[...] = jnp.transpose(kpe128)[MLA_NOPE_DIM:MLA_NOPE_DIM + MLA_ROPE_DIM, :]
        kv = _dot(ckv_n.astype(BF16), wkv_ref[...])
        for h in range(MLA_HEADS):
            sl = slice(h * MLA_QK_PAD, (h + 1) * MLA_QK_PAD)
            k_ref[:, sl] = (kv[:, sl] + kpe128).astype(BF16)
        v_ref[...] = kv[:, MLA_QK_WIDTH:].astype(BF16)
    else:
        kpe_ref[...] = kpe128


def _inproj_rows(x, g_ref, w_scr, og_ref, mz_ref, os_ref, mla, with_kv):
    ms = jnp.mean(x * x, axis=-1, keepdims=True)
    h = (x * lax.rsqrt(ms + EPS) * g_ref[...]).astype(BF16)

    def seg(a, b):
        return lax.dot_general(h, w_scr[a:b, :], (((1,), (1,)), ((), ())), preferred_element_type=F32)

    lane = lax.broadcasted_iota(jnp.int32, (x.shape[0], LANES), 1)
    lane2 = lax.broadcasted_iota(jnp.int32, (x.shape[0], 2 * LANES), 1)
    c = IN_OFF
    og_ref[:, PG_QKV] = seg(c['g_q'], c['g_lr']).astype(BF16)
    og_ref[:, PG_Z] = seg(c['g_z'], c['m_cq']).astype(BF16)
    og_ref[:, PG_LR] = jnp.where(lane < GLA_GATE_RANK, seg(c['g_lr'], c['g_lr'] + LANES), 0.0).astype(BF16)
    mz_ref[...] = seg(c['m_z'], c['s_u']).astype(BF16)
    os_ref[...] = seg(c['s_u'], c['end']).astype(BF16)
    cq = jnp.where(lane2 < MLA_Q_RANK, seg(c['m_cq'], c['m_cq'] + 2 * LANES), 0.0)
    ckv_kr = seg(c['m_ckv'], c['m_ckv'] + 2 * LANES)
    kr = pltpu.roll(ckv_kr[:, LANES:], MLA_NOPE_DIM, 1)
    kr = jnp.where((lane >= MLA_NOPE_DIM) & (lane < MLA_NOPE_DIM + MLA_ROPE_DIM), kr, 0.0)
    _mla_prep_rows(cq, ckv_kr[:, 0:LANES], kr, mla, with_kv)


N_MLA_IN = {True: 6, False: 5}
N_MLA_OUT = {True: 5, False: 3}


def _inproj_body(x_ref, g_ref, wt_ref, *rest, with_kv):
    n_in, n_out = N_MLA_IN[with_kv], N_MLA_OUT[with_kv]
    mla_in, (og_ref, mz_ref, os_ref), mla_out, (w_scr,) = (
        rest[:n_in], rest[n_in:n_in + 3], rest[n_in + 3:n_in + 3 + n_out], rest[n_in + 3 + n_out:])

    @pl.when(pl.program_id(0) == 0)
    def _():
        w_scr[...] = wt_ref[...].astype(BF16)

    _inproj_rows(x_ref[...], g_ref, w_scr, og_ref, mz_ref, os_ref, tuple(mla_in) + tuple(mla_out), with_kv)


def _mla_specs(p, cos_t, sin_t, layer, bsz, t, tm, with_kv):
    n = bsz * t
    ntab = max(1, t // tm)
    assert not with_kv or tm <= t
    row = lambda c: pl.BlockSpec((tm, c), lambda i: (i, 0))
    table = pl.BlockSpec((tm, LANES), lambda i: (i % ntab, 0))
    args = [cos_t, sin_t, p['gq'], p['wq'], p['gkv']]
    in_specs = [table, table, _layer_spec((1, 256), layer), _layer_spec((256, MLA_QK_WIDTH), layer),
                _layer_spec((1, MLA_KV_RANK), layer)]
    out_specs = [row(MLA_QK_WIDTH), row(MLA_KV_RANK)]
    out_shape = [jax.ShapeDtypeStruct((n, MLA_QK_WIDTH), BF16), jax.ShapeDtypeStruct((n, MLA_KV_RANK), F32)]
    if with_kv:
        args.append(p['wkv'])
        in_specs.append(_layer_spec((MLA_KV_RANK, MLA_QK_WIDTH + MLA_WIDTH), layer))
        out_specs += [pl.BlockSpec((None, MLA_ROPE_DIM, tm), lambda i: (i // ntab, 0, i % ntab)),
                      row(MLA_QK_WIDTH), row(MLA_WIDTH)]
        out_shape += [jax.ShapeDtypeStruct((bsz, MLA_ROPE_DIM, t), F32),
                      jax.ShapeDtypeStruct((n, MLA_QK_WIDTH), BF16), jax.ShapeDtypeStruct((n, MLA_WIDTH), BF16)]
    else:
        out_specs.append(row(LANES))
        out_shape.append(jax.ShapeDtypeStruct((n, LANES), F32))
    return args, in_specs, out_specs, out_shape


def _proj_out(tm, n):
    row = lambda c: pl.BlockSpec((tm, c), lambda i: (i, 0))
    return ([row(PG_COLS), row(MLA_WIDTH), row(PS_COLS)],
            [jax.ShapeDtypeStruct((n, PG_COLS), BF16), jax.ShapeDtypeStruct((n, MLA_WIDTH), BF16),
             jax.ShapeDtypeStruct((n, PS_COLS), BF16)])


def _inproj(x2, p, cos_t, sin_t, layer, bsz, t, tm, with_kv):
    n = x2.shape[0]
    m_args, m_in, m_out, m_shape = _mla_specs(p, cos_t, sin_t, layer, bsz, t, tm, with_kv)
    o_specs, o_shape = _proj_out(tm, n)
    return pl.pallas_call(
        functools.partial(_inproj_body, with_kv=with_kv),
        grid=(n // tm,),
        in_specs=[pl.BlockSpec((tm, D_MODEL), lambda i: (i, 0)),
                  _layer_spec((1, D_MODEL), layer),
                  _layer_spec((IN_OFF['end'], D_MODEL), layer, single_buffer=True)] + m_in,
        scratch_shapes=[pltpu.VMEM((IN_OFF['end'], D_MODEL), BF16)],
        out_specs=o_specs + m_out,
        out_shape=o_shape + m_shape,
        compiler_params=_cparams("arbitrary"),
        name="inproj",
    )(x2, p['ln'], p['w_in'], *m_args)


def _gla_body(p_ref, wg_ref, bg_ref, gain_ref, ones_k_ref, ones_v_ref, bd_ref, s0_ref,
              o_ref, sout_ref, q_scr, k_scr, v_scr, b_scr, s_scr, qe_scr, qs_scr, ke_scr, vb_scr, o_scr,
              *, chunk, nch, nseq):
    it = pl.program_id(1)
    sub = min(GLA_SUB, chunk)
    nsub = chunk // sub

    @pl.when(it == 0)
    def _():
        s_scr[...] = jnp.zeros(s_scr.shape, F32)
        for sq in range(nseq):
            for h in range(GLA_HEADS):
                s_scr[sq, h * GLA_DK:(h + 1) * GLA_DK, h * GLA_DV:(h + 1) * GLA_DV] = s0_ref[sq, h]

    def cols(lanes):
        parts = [p_ref[sq, :, lanes] for sq in range(nseq)]
        return parts[0] if nseq == 1 else jnp.concatenate(parts, axis=0)

    q_scr[...] = cols(PG_Q).astype(F32) * (GLA_DK ** -0.5)
    k_scr[...] = cols(PG_K).astype(F32)
    v_scr[...] = cols(PG_V).astype(F32)
    logit = _dot(cols(PG_LR), wg_ref[...]) + bg_ref[...]
    log_a = (jnp.minimum(logit, 0.0) - jnp.log(1.0 + jnp.exp(-jnp.abs(logit)))) * (1.0 / GLA_GATE_TAU)
    row_in_chunk = lax.broadcasted_iota(jnp.int32, log_a.shape, 0) % chunk
    b_all = log_a
    shift = 1
    while shift < chunk:
        b_all = b_all + jnp.where(row_in_chunk >= shift, pltpu.roll(b_all, shift, 0), 0.0)
        shift *= 2
    b_scr[...] = b_all

    row = lax.broadcasted_iota(jnp.int32, (chunk, GLA_QK), 0)
    row_in_sub = lax.broadcasted_iota(jnp.int32, (sub, GLA_QK), 0)

    def decay_columns(b_end):
        col = jnp.transpose(jnp.broadcast_to(jnp.exp(b_end), (GLA_QK, GLA_QK)))
        return jnp.concatenate([col, col], axis=1)

    def load_chunk(sq, c):
        r0 = pl.multiple_of((sq * nch + c) * chunk, chunk)
        return (r0, q_scr[pl.ds(r0, chunk), :], k_scr[pl.ds(r0, chunk), :], v_scr[pl.ds(r0, chunk), :],
                b_scr[pl.ds(r0, chunk), :], b_scr[pl.ds(r0 + chunk - 1, 1), :], s_scr[sq])

    def finish_chunk(sq, r0, o, s_prev, a_state, b_end):
        ms = _dot((o * o).astype(BF16), ones_v_ref[...]) * (1.0 / GLA_DV)
        o_n = o * lax.rsqrt(ms + EPS) * gain_ref[...]
        t0 = r0 - sq * nch * chunk
        z = p_ref[sq, pl.ds(t0, chunk), PG_Z].astype(F32)
        o_ref[sq, pl.ds(t0, chunk), :] = (o_n * (z * _sigmoid(z))).astype(BF16)
        s_scr[sq] = s_prev * decay_columns(b_end) + a_state

    def robust_chunk(c, carry, sq):
        r0, qc, kc, vc, bc, b_end, s_prev = load_chunk(sq, c)
        xs = [qc * jnp.exp(bc)]
        ks = []
        for sj in range(nsub - 1):
            e_j = b_scr[pl.ds(r0 + (sj + 1) * sub - 1, 1), :]
            later = row >= (sj + 1) * sub
            xs.append(jnp.where(later, qc * jnp.exp(jnp.where(later, bc - e_j, 0.0)), 0.0))
            own = (row >= sj * sub) & (row < (sj + 1) * sub)
            ks.append(jnp.where(own, kc * jnp.exp(jnp.where(own, e_j - bc, 0.0)), 0.0))
        ks.append(kc * jnp.exp(b_end - bc))
        k_all = jnp.concatenate(ks, axis=1).astype(BF16)
        a_all = _dot_t(k_all, vc.astype(BF16)) * bd_ref[...]
        w = jnp.concatenate([s_prev, a_all[:(nsub - 1) * GLA_QK]], axis=0).astype(BF16) if nsub > 1 \
            else s_prev.astype(BF16)
        o_off = _dot(jnp.concatenate(xs, axis=1).astype(BF16), w)
        rows = []
        for si in range(nsub):
            q_i = qc[si * sub:(si + 1) * sub]
            b_i = bc[si * sub:(si + 1) * sub]
            es = []
            for j in range(sub):
                r = r0 + si * sub + j
                b_j = b_scr[pl.ds(r, 1), :]
                k_j = k_scr[pl.ds(r, 1), :]
                valid = row_in_sub >= j
                es.append(q_i * k_j * jnp.exp(jnp.where(valid, b_i - b_j, -jnp.inf)))
            e_all = jnp.concatenate(es, axis=0)
            e_hi = e_all.astype(BF16)
            e_lo = (e_all - e_hi.astype(F32)).astype(BF16)
            p_all = _dot(e_hi, ones_k_ref[...]) + _dot(e_lo, ones_k_ref[...])
            acc = o_off[si * sub:(si + 1) * sub]
            for j in range(sub):
                v_j = v_scr[pl.ds(r0 + si * sub + j, 1), :]
                acc = acc + p_all[j * sub:(j + 1) * sub] * v_j
            rows.append(acc)
        o = rows[0] if nsub == 1 else jnp.concatenate(rows, axis=0)
        finish_chunk(sq, r0, o, s_prev, a_all[(nsub - 1) * GLA_QK:], b_end)
        return carry

    lane_head_v = lax.broadcasted_iota(jnp.int32, (chunk, GLA_WIDTH), 1) // GLA_DV
    causal = (lax.broadcasted_iota(jnp.int32, (GLA_HEADS * chunk, chunk), 0) % chunk
              >= lax.broadcasted_iota(jnp.int32, (GLA_HEADS * chunk, chunk), 1))

    def plain_block():
        tt = nseq * nch * chunk
        b_all = b_scr[...]
        q_all = q_scr[...]
        k_all = k_scr[...]
        qe = q_all * jnp.exp(b_all)
        lane_head = lax.broadcasted_iota(jnp.int32, (tt, GLA_QK), 1) // GLA_DK
        qe_scr[...] = qe.astype(BF16)
        for h in range(GLA_HEADS):
            qs_scr[h] = jnp.where(lane_head == h, qe, 0.0).astype(BF16)
        ke_scr[...] = (k_all * jnp.exp(-b_all)).astype(BF16)
        vb_scr[...] = cols(PG_V)
        states = [s_scr[sq] for sq in range(nseq)]
        for c, sq in [(c, sq) for c in range(nch) for sq in range(nseq)]:
            r0 = (sq * nch + c) * chunk
            rs = slice(r0, r0 + chunk)
            s_cur = states[sq]
            qs = jnp.concatenate([qs_scr[h, rs, :] for h in range(GLA_HEADS)], axis=0)
            s = lax.dot_general(qs, ke_scr[rs, :], (((1,), (1,)), ((), ())), preferred_element_type=F32)
            s = jnp.where(causal, s, 0.0).astype(BF16)
            r = _dot(s, vb_scr[rs, :])
            o = _dot(qe_scr[rs, :], s_cur.astype(BF16))
            for h in range(GLA_HEADS):
                o = o + jnp.where(lane_head_v == h, r[h * chunk:(h + 1) * chunk], 0.0)
            o_scr[rs, :] = o
            b_end = b_scr[r0 + chunk - 1:r0 + chunk, :]
            k_end = (k_scr[rs, :] * jnp.exp(b_end - b_scr[rs, :])).astype(BF16)
            a_state = _dot_t(k_end, vb_scr[rs, :]) * bd_ref[0:GLA_QK, :]
            states[sq] = s_cur * decay_columns(b_end) + a_state
        for sq in range(nseq):
            s_scr[sq] = states[sq]
        o = o_scr[...]
        ms = _dot((o * o).astype(BF16), ones_v_ref[...]) * (1.0 / GLA_DV)
        o_n = o * lax.rsqrt(ms + EPS) * gain_ref[...]
        z = cols(PG_Z).astype(F32)
        o_all = (o_n * (z * _sigmoid(z))).astype(BF16)
        for sq in range(nseq):
            o_ref[sq] = o_all[sq * nch * chunk:(sq + 1) * nch * chunk]

    in_range = jnp.max(-b_scr[...]) < GLA_PLAIN_MAX_DECAY

    @pl.when(in_range)
    def _():
        plain_block()

    @pl.when(jnp.logical_not(in_range))
    def _():
        for sq in range(nseq):
            lax.fori_loop(0, nch, functools.partial(robust_chunk, sq=sq), 0)

    @pl.when(it == pl.num_programs(1) - 1)
    def _():
        for sq in range(nseq):
            for h in range(GLA_HEADS):
                sout_ref[sq, h] = s_scr[sq, h * GLA_DK:(h + 1) * GLA_DK, h * GLA_DV:(h + 1) * GLA_DV]


def _gla_consts(chunk):
    nsub = chunk // min(GLA_SUB, chunk)
    hk = np.arange(GLA_QK) // GLA_DK
    hv = np.arange(GLA_WIDTH) // GLA_DV
    same_kv = (hk[:, None] == hv[None, :]).astype(np.float32)
    same_vv = (hv[:, None] == hv[None, :]).astype(np.float32)
    return (jnp.asarray(same_kv, BF16), jnp.asarray(same_vv, BF16),
            jnp.asarray(np.tile(same_kv, (nsub, 1)), F32))


def _gla(pg, wg, bg, gain, layer, s0, bsz, t, tl):
    chunk, tt, nseq = tl['gla_chunk'], tl['gla_rows'], tl['gla_seqs']
    nch = tt // chunk
    nt = t // tt
    rows = nseq * tt
    nsub = chunk // min(GLA_SUB, chunk)
    ones_k, ones_v, bd = _gla_consts(chunk)
    const = _const_spec
    return pl.pallas_call(
        functools.partial(_gla_body, chunk=chunk, nch=nch, nseq=nseq),
        grid=(bsz // nseq, nt),
        in_specs=[pl.BlockSpec((nseq, tt, PG_COLS), lambda b, i: (b, i, 0)),
                  _layer_spec((GLA_QK, GLA_QK), layer), _layer_spec((1, GLA_QK), layer),
                  _layer_spec((1, GLA_WIDTH), layer),
                  const((GLA_QK, GLA_WIDTH)), const((GLA_WIDTH, GLA_WIDTH)),
                  const((nsub * GLA_QK, GLA_WIDTH)),
                  pl.BlockSpec((nseq, GLA_HEADS, GLA_DK, GLA_DV), lambda b, i: (b, 0, 0, 0))],
        out_specs=[pl.BlockSpec((nseq, tt, GLA_WIDTH), lambda b, i: (b, i, 0)),
                   pl.BlockSpec((nseq, GLA_HEADS, GLA_DK, GLA_DV), lambda b, i: (b, 0, 0, 0))],
        out_shape=[jax.ShapeDtypeStruct((bsz, t, GLA_WIDTH), BF16),
                   jax.ShapeDtypeStruct((bsz, GLA_HEADS, GLA_DK, GLA_DV), F32)],
        scratch_shapes=[pltpu.VMEM((rows, GLA_QK), F32), pltpu.VMEM((rows, GLA_QK), F32),
                        pltpu.VMEM((rows, GLA_WIDTH), F32), pltpu.VMEM((rows, GLA_QK), F32),
                        pltpu.VMEM((nseq, GLA_QK, GLA_WIDTH), F32),
                        pltpu.VMEM((rows, GLA_QK), BF16), pltpu.VMEM((GLA_HEADS, rows, GLA_QK), BF16),
                        pltpu.VMEM((rows, GLA_QK), BF16),
                        pltpu.VMEM((rows, GLA_WIDTH), BF16), pltpu.VMEM((rows, GLA_WIDTH), F32)],
        compiler_params=_cparams("parallel", "arbitrary"),
        name="gla",
    )(pg, wg, bg, gain, ones_k, ones_v, bd, s0)


def _attn_cached_body(q_ref, z_ref, ckv_new_ref, kpe_new_ref, ckv_past_ref, kpet_past_ref, wka_ref, wv_ref,
                      o_ref, kpet_scr, *, past, t, nseq):
    last = (((1,), (1,)), ((), ()))
    hsl = [slice(h * MLA_QK_PAD, (h + 1) * MLA_QK_PAD) for h in range(MLA_HEADS)]
    q_lat_h = [_dot(q_ref[:, hsl[h]], wka_ref[h]).astype(BF16) for h in range(MLA_HEADS)]
    kpet_scr[...] = jnp.zeros(kpet_scr.shape, BF16)
    o_lat = []
    for sq in range(nseq):
        rs = slice(sq * t, (sq + 1) * t)
        q_rows = jnp.concatenate([q_ref[rs, hsl[h]] for h in range(MLA_HEADS)], axis=0)
        q_lat = jnp.concatenate([q_lat_h[h][rs] for h in range(MLA_HEADS)], axis=0)
        c_past = ckv_past_ref[sq].astype(BF16)
        c_new = ckv_new_ref[rs, :].astype(BF16)
        kpet_scr[sq, MLA_NOPE_DIM:MLA_NOPE_DIM + MLA_ROPE_DIM, :] = kpet_past_ref[sq].astype(BF16)
        s_past = (lax.dot_general(q_lat, c_past, last, preferred_element_type=F32)
                  + _dot(q_rows, kpet_scr[sq]))
        s_new = (lax.dot_general(q_lat, c_new, last, preferred_element_type=F32)
                 + lax.dot_general(q_rows, kpe_new_ref[rs, :].astype(BF16), last, preferred_element_type=F32))
        if past // CHUNK != (past + t - 1) // CHUNK:
            q_chunk = (past + lax.broadcasted_iota(jnp.int32, s_past.shape, 0) % t) // CHUNK
            s_past = jnp.where(lax.broadcasted_iota(jnp.int32, s_past.shape, 1) // CHUNK <= q_chunk, s_past, -jnp.inf)
            q_chunk = (past + lax.broadcasted_iota(jnp.int32, s_new.shape, 0) % t) // CHUNK
            s_new = jnp.where((past + lax.broadcasted_iota(jnp.int32, s_new.shape, 1)) // CHUNK <= q_chunk,
                              s_new, -jnp.inf)
        m = jnp.maximum(jnp.max(s_past, axis=-1, keepdims=True), jnp.max(s_new, axis=-1, keepdims=True))
        p_past = jnp.exp2(s_past - m)
        p_new = jnp.exp2(s_new - m)
        l = jnp.sum(p_past, axis=-1, keepdims=True) + jnp.sum(p_new, axis=-1, keepdims=True)
        o_lat.append(((_dot(p_past.astype(BF16), c_past) + _dot(p_new.astype(BF16), c_new)) / l).astype(BF16))
    for h in range(MLA_HEADS):
        vsl = slice(h * MLA_V_DIM, (h + 1) * MLA_V_DIM)
        o_h = jnp.concatenate([o_lat[sq][h * t:(h + 1) * t] for sq in range(nseq)], axis=0)
        z = z_ref[:, vsl].astype(F32)
        o_ref[:, vsl] = (_dot(o_h, wv_ref[h]) * (z * _sigmoid(z))).astype(BF16)


def _attn_cached(q, mz, ckv_new, kpe128, ckv_past, kpet_past, wka, wv, layer, bsz, t, nseq):
    past = ckv_past.shape[2]
    rows = nseq * t
    return pl.pallas_call(
        functools.partial(_attn_cached_body, past=past, t=t, nseq=nseq),
        grid=(bsz // nseq,),
        in_specs=[pl.BlockSpec((rows, MLA_QK_WIDTH), lambda b: (b, 0)),
                  pl.BlockSpec((rows, MLA_WIDTH), lambda b: (b, 0)),
                  pl.BlockSpec((rows, MLA_KV_RANK), lambda b: (b, 0)),
                  pl.BlockSpec((rows, LANES), lambda b: (b, 0)),
                  pl.BlockSpec((None, nseq, past, MLA_KV_RANK), lambda b: (layer, b, 0, 0)),
                  pl.BlockSpec((None, nseq, MLA_ROPE_DIM, past), lambda b: (layer, b, 0, 0)),
                  _layer_spec((MLA_HEADS, MLA_QK_PAD, MLA_KV_RANK), layer),
                  _layer_spec((MLA_HEADS, MLA_KV_RANK, MLA_V_DIM), layer)],
        out_specs=pl.BlockSpec((rows, MLA_WIDTH), lambda b: (b, 0)),
        out_shape=jax.ShapeDtypeStruct((bsz * t, MLA_WIDTH), BF16),
        scratch_shapes=[pltpu.VMEM((nseq, MLA_QK_PAD, past), BF16)],
        compiler_params=_cparams("parallel"),
        name="mla_attn_cached",
    )(q, mz, ckv_new, kpe128, ckv_past, kpet_past, wka, wv)


def _attn_body(q_ref, k_ref, v_ref, z_ref, o_ref, m_scr, acc_scr, *, past, tq, tk, s_len):
    iq = pl.program_id(1)
    q_first = past + iq * tq
    full_keys = jnp.minimum((q_first // CHUNK + 1) * CHUNK, s_len)
    vis_keys = jnp.minimum(((q_first + tq - 1) // CHUNK + 1) * CHUNK, s_len)
    n_full = full_keys // tk
    n_vis = (vis_keys + tk - 1) // tk
    split_diagonal = past % tk == 0 and tq == tk and (tq // 2) % CHUNK == 0

    def block(kb, carry, masked, first=False):
        k0 = pl.multiple_of(kb * tk, tk)
        if masked and split_diagonal:
            parts = [(r * (tq // 2), tq // 2, (r + 1) * (tk // 2)) for r in range(2)]
        else:
            parts = [(0, tq, tk)]
        for r0, nr, kext in parts:
            rows = slice(r0, r0 + nr)
            if masked:
                q_pos = (r0 if split_diagonal else q_first + r0) + lax.broadcasted_iota(jnp.int32, (nr, kext), 0)
                k_pos = (0 if split_diagonal else k0) + lax.broadcasted_iota(jnp.int32, (nr, kext), 1)
                visible = k_pos // CHUNK <= q_pos // CHUNK
            ntile, rem = kext // LANES, kext % LANES
            ones_v = jnp.ones((kext, MLA_V_DIM), BF16)
            for h in range(MLA_HEADS):
                sl = slice(h * MLA_QK_PAD, (h + 1) * MLA_QK_PAD)
                vsl = slice(h * MLA_V_DIM, (h + 1) * MLA_V_DIM)
                s = lax.dot_general(q_ref[rows, sl], k_ref[pl.ds(k0, kext), sl], (((1,), (1,)), ((), ())),
                                    preferred_element_type=F32)
                if masked:
                    s = jnp.where(visible, s, -jnp.inf)
                if first:
                    m_new = jnp.broadcast_to(jnp.max(s, axis=-1, keepdims=True), (nr, LANES))
                else:
                    m_prev = m_scr[h, rows]
                    m_new = jnp.maximum(m_prev, jnp.max(s, axis=-1, keepdims=True))
                    alpha = jnp.exp2(m_prev - m_new)
                ps = [jnp.exp2(s[:, c * LANES:(c + 1) * LANES] - m_new) for c in range(ntile)]
                if rem:
                    ps.append(jnp.exp2(s[:, ntile * LANES:] - m_new[:, :rem]))
                p = jnp.concatenate(ps, axis=1).astype(BF16)
                v_ext = jnp.concatenate([v_ref[pl.ds(k0, kext), vsl], ones_v], axis=1)
                pv = _dot(p, v_ext)
                acc_scr[h, rows] = pv if first else jnp.concatenate([alpha, alpha], axis=1) * acc_scr[h, rows] + pv
                m_scr[h, rows] = m_new
        return carry

    @pl.when(n_full > 0)
    def _():
        block(0, 0, masked=False, first=True)
        lax.fori_loop(1, n_full, functools.partial(block, masked=False), 0)
        lax.fori_loop(n_full, n_vis, functools.partial(block, masked=True), 0)

    @pl.when(n_full == 0)
    def _():
        block(0, 0, masked=True, first=True)
        lax.fori_loop(1, n_vis, functools.partial(block, masked=True), 0)

    for h in range(MLA_HEADS):
        vsl = slice(h * MLA_V_DIM, (h + 1) * MLA_V_DIM)
        z = z_ref[:, vsl].astype(F32)
        acc = acc_scr[h]
        o_ref[:, vsl] = (acc[:, :MLA_V_DIM] / acc[:, MLA_V_DIM:] * (z * _sigmoid(z))).astype(BF16)


def _attn(q, k, v, mz, bsz, t, s_len, past, tq, tk):
    nq = t // tq
    return pl.pallas_call(
        functools.partial(_attn_body, past=past, tq=tq, tk=tk, s_len=s_len),
        grid=(bsz, nq),
        in_specs=[pl.BlockSpec((tq, MLA_QK_WIDTH), lambda b, iq: (b * nq + iq, 0)),
                  pl.BlockSpec((s_len, MLA_QK_WIDTH), lambda b, iq: (b, 0)),
                  pl.BlockSpec((s_len, MLA_WIDTH), lambda b, iq: (b, 0)),
                  pl.BlockSpec((tq, MLA_WIDTH), lambda b, iq: (b * nq + iq, 0))],
        out_specs=pl.BlockSpec((tq, MLA_WIDTH), lambda b, iq: (b * nq + iq, 0)),
        out_shape=jax.ShapeDtypeStruct((bsz * t, MLA_WIDTH), BF16),
        scratch_shapes=[pltpu.VMEM((MLA_HEADS, tq, LANES), F32), pltpu.VMEM((MLA_HEADS, tq, 2 * MLA_V_DIM), F32)],
        compiler_params=_cparams("parallel", "arbitrary"),
        name="mla_attn",
    )(q, k, v, mz)


def _s5_weights_body(bre_ref, bim_ref, cre_ref, cim_ref, spread_ref, mask_ref, wb_ref, wc_ref):
    def expand(m_ref):
        return _dot(m_ref[...].astype(BF16), spread_ref[...]) * mask_ref[...]

    wb_ref[:, 0:S5_NSTATE] = expand(bre_ref).astype(BF16)
    wb_ref[:, S5_NSTATE:] = expand(bim_ref).astype(BF16)
    wc_ref[0:S5_NSTATE, :] = jnp.transpose(expand(cre_ref)).astype(BF16)
    wc_ref[S5_NSTATE:, :] = jnp.transpose(-expand(cim_ref)).astype(BF16)


def _s5_weights(bbr, bbi, c_re, c_im):
    depth = bbr.shape[0]
    g_row = np.arange(S5_WIDTH) // S5_GROUP_CH
    g_col = np.arange(S5_NSTATE) // S5_STATE
    mask = (g_row[:, None] == g_col[None, :]).astype(np.float32)
    spread = (np.arange(S5_STATE)[:, None] == (np.arange(S5_NSTATE) % S5_STATE)[None, :]).astype(np.float32)
    small = pl.BlockSpec((None, S5_WIDTH, S5_STATE), lambda l: (l, 0, 0))
    return pl.pallas_call(
        _s5_weights_body,
        grid=(depth,),
        in_specs=[small, small, small, small,
                  _const_spec((S5_STATE, S5_NSTATE)), _const_spec((S5_WIDTH, S5_NSTATE))],
        out_specs=[pl.BlockSpec((None, S5_WIDTH, 2 * S5_NSTATE), lambda l: (l, 0, 0)),
                   pl.BlockSpec((None, 2 * S5_NSTATE, S5_WIDTH), lambda l: (l, 0, 0))],
        out_shape=[jax.ShapeDtypeStruct((depth, S5_WIDTH, 2 * S5_NSTATE), BF16),
                   jax.ShapeDtypeStruct((depth, 2 * S5_NSTATE, S5_WIDTH), BF16)],
        compiler_params=_cparams("parallel"),
        name="s5_weights",
    )(bbr, bbi, c_re, c_im, jnp.asarray(spread, BF16), jnp.asarray(mask, F32))


def _s5_body(p0_ref, pn_ref, x0r_ref, x0i_ref, lre_ref, lim_ref, wb_ref, wc_ref, d_ref, wglu_ref, bglu_ref, never_ref,
             o_ref, xr_out, xi_out, uz_bt, uz_a, uz_b, uz_c, bu_a, bu_b, bu_c, o_tb, xr_s, xi_s, *, lc, pitch):
    it = pl.program_id(1)
    nb = S5_BATCH_TILE
    ring = ((uz_a, bu_a), (uz_b, bu_b), (uz_c, bu_c))

    def stage_in(blk_ref, uz_tb, bu):
        for b in range(nb):
            for c in range(PS_COLS // LANES):
                uz_bt[c, b * pitch:b * pitch + lc, :] = blk_ref[b, :, c * LANES:(c + 1) * LANES].astype(F32)
        for t in range(lc):
            for c in range(PS_COLS // LANES):
                uz_tb[t * nb:(t + 1) * nb, c * LANES:(c + 1) * LANES] = uz_bt[c, pl.ds(t, nb, stride=pitch), :]
        bu[...] = _dot(uz_tb[:, 0:S5_WIDTH].astype(BF16), wb_ref[...])

    def stage_scan(bu):
        w = S5_NSTATE // S5_SCAN_SLICES
        never = never_ref[...] != 0
        last = None
        for c0 in range(0, S5_NSTATE, w):
            re, im = slice(c0, c0 + w), slice(S5_NSTATE + c0, S5_NSTATE + c0 + w)
            lre = jnp.broadcast_to(lre_ref[:, re], (nb, w))
            lim = jnp.broadcast_to(lim_ref[:, re], (nb, w))
            xr, xi = xr_s[:, re], xi_s[:, re]
            if last is not None:
                xr = jnp.where(never, last, xr)
            for t in range(lc):
                rs = slice(t * nb, (t + 1) * nb)
                xr, xi = lre * xr - lim * xi + bu[rs, re], lre * xi + lim * xr + bu[rs, im]
                bu[rs, re] = xr
                bu[rs, im] = xi
            xr_s[:, re] = xr
            xi_s[:, re] = xi
            last = xr

    def stage_out(uz_tb, xs):
        y = _dot(xs[...].astype(BF16), wc_ref[...]) + d_ref[...] * uz_tb[:, 0:S5_WIDTH]
        g5 = 0.5 * y * (1.0 + jnp.tanh(GELU_TANH_SCALE * (y + GELU_TANH_CUBIC * (y * y * y))))
        gate = _sigmoid(_dot(g5.astype(BF16), wglu_ref[...]) + bglu_ref[...])
        z = uz_tb[:, S5_WIDTH:2 * S5_WIDTH]
        o = g5 * gate * (z * _sigmoid(z))
        for c in range(S5_WIDTH // LANES):
            o_tb[c] = o[:, c * LANES:(c + 1) * LANES]
        for b in range(nb):
            for c in range(S5_WIDTH // LANES):
                o_ref[b, :, c * LANES:(c + 1) * LANES] = o_tb[c, pl.ds(b, lc, stride=nb), :].astype(BF16)

    @pl.when(it == 0)
    def _():
        xr_s[...] = x0r_ref[...]
        xi_s[...] = x0i_ref[...]
        stage_in(p0_ref, uz_a, bu_a)
        uz_c[...] = jnp.zeros(uz_c.shape, F32)
        bu_c[...] = jnp.zeros(bu_c.shape, F32)

    for r in range(3):
        @pl.when(it % 3 == r)
        def _(r=r):
            stage_in(pn_ref, *ring[(r + 1) % 3])
            stage_scan(ring[r][1])
            stage_out(*ring[(r + 2) % 3])

    @pl.when(it == pl.num_programs(1) - 2)
    def _():
        xr_out[...] = xr_s[...]
        xi_out[...] = xi_s[...]


def _s5(ps3, x0r, x0i, lre, lim, wb, wc, d, wglu, bglu, layer, bsz, t, lc):
    nb = S5_BATCH_TILE
    nt = t // lc
    pitch = lc + 8
    rows = lc * nb
    return pl.pallas_call(
        functools.partial(_s5_body, lc=lc, pitch=pitch),
        grid=(bsz // nb, nt + 1),
        in_specs=[pl.BlockSpec((nb, lc, PS_COLS), lambda g, i: (g, 0, 0)),
                  pl.BlockSpec((nb, lc, PS_COLS), lambda g, i: (g, jnp.minimum(i + 1, nt - 1), 0)),
                  pl.BlockSpec((nb, S5_NSTATE), lambda g, i: (g, 0)),
                  pl.BlockSpec((nb, S5_NSTATE), lambda g, i: (g, 0)),
                  _layer_spec((1, S5_NSTATE), layer), _layer_spec((1, S5_NSTATE), layer),
                  _layer_spec((S5_WIDTH, 2 * S5_NSTATE), layer), _layer_spec((2 * S5_NSTATE, S5_WIDTH), layer),
                  _layer_spec((1, S5_WIDTH), layer), _layer_spec((S5_WIDTH, S5_WIDTH), layer),
                  _layer_spec((1, S5_WIDTH), layer),
                  _const_spec((1, S5_NSTATE // S5_SCAN_SLICES))],
        out_specs=[pl.BlockSpec((nb, lc, S5_WIDTH), lambda g, i: (g, jnp.maximum(i - 1, 0), 0)),
                   pl.BlockSpec((nb, S5_NSTATE), lambda g, i: (g, 0)),
                   pl.BlockSpec((nb, S5_NSTATE), lambda g, i: (g, 0))],
        out_shape=[jax.ShapeDtypeStruct((bsz, t, S5_WIDTH), BF16),
                   jax.ShapeDtypeStruct((bsz, S5_NSTATE), F32),
                   jax.ShapeDtypeStruct((bsz, S5_NSTATE), F32)],
        scratch_shapes=[pltpu.VMEM((PS_COLS // LANES, nb * pitch, LANES), F32)]
                       + [pltpu.VMEM((rows, PS_COLS), F32)] * 3
                       + [pltpu.VMEM((rows, 2 * S5_NSTATE), F32)] * 3
                       + [pltpu.VMEM((S5_WIDTH // LANES, rows, LANES), F32),
                          pltpu.VMEM((nb, S5_NSTATE), F32), pltpu.VMEM((nb, S5_NSTATE), F32)],
        compiler_params=_cparams("parallel", "arbitrary"),
        name="s5",
    )(ps3, ps3, x0r, x0i, lre, lim, wb, wc, d, wglu, bglu, jnp.zeros((1, S5_NSTATE // S5_SCAN_SLICES), jnp.int32))


def _outproj_rows(x, og_ref, om_ref, os_ref, wo_scr):
    acc = _dot(og_ref[...], wo_scr[0:GLA_WIDTH, :])
    acc += _dot(om_ref[...], wo_scr[GLA_WIDTH:GLA_WIDTH + MLA_WIDTH, :])
    acc += _dot(os_ref[...], wo_scr[GLA_WIDTH + MLA_WIDTH:, :])
    return x + acc


def _outproj_body(x_ref, og_ref, om_ref, os_ref, w_ref, g_ref, o_ref, wo_scr, *, final):
    @pl.when(pl.program_id(0) == 0)
    def _():
        wo_scr[...] = w_ref[...].astype(BF16)

    xn = _outproj_rows(x_ref[...], og_ref, om_ref, os_ref, wo_scr)
    if final:
        ms = jnp.mean(xn * xn, axis=-1, keepdims=True)
        xn = xn * lax.rsqrt(ms + EPS) * g_ref[...]
    o_ref[...] = xn


def _out_in_body(x_ref, og_ref, om_ref, os_ref, wo_ref, g_ref, wt_ref, *rest, with_kv, n_prev):
    prev, rest = rest[:n_prev], rest[n_prev:]
    n_in, n_out = N_MLA_IN[with_kv], N_MLA_OUT[with_kv]
    mla_in, (xo_ref, pg_ref, mz_ref, ps_ref), mla_out, (wo_scr, w_scr) = (
        rest[:n_in], rest[n_in:n_in + 4], rest[n_in + 4:n_in + 4 + n_out], rest[n_in + 4 + n_out:])
    if n_prev:
        ckv_all = mla_out[1]
        for j in range(n_prev):
            ckv_all[j] = prev[j][...]
        mla_out = (mla_out[0], ckv_all.at[n_prev]) + tuple(mla_out[2:])

    @pl.when(pl.program_id(0) == 0)
    def _():
        wo_scr[...] = wo_ref[...].astype(BF16)
        w_scr[...] = wt_ref[...].astype(BF16)

    xn = _outproj_rows(x_ref[...], og_ref, om_ref, os_ref, wo_scr)
    xo_ref[...] = xn
    _inproj_rows(xn, g_ref, w_scr, pg_ref, mz_ref, ps_ref, tuple(mla_in) + tuple(mla_out), with_kv)


def _out_in(x2, og, om, os_, p, cos_t, sin_t, layer, bsz, t, tm, with_kv, ckv_prev=()):
    n = x2.shape[0]
    row = lambda c: pl.BlockSpec((tm, c), lambda i: (i, 0))
    m_args, m_in, m_out, m_shape = _mla_specs(p, cos_t, sin_t, layer + 1, bsz, t, tm, with_kv)
    n_prev = len(ckv_prev)
    if n_prev:
        m_out[1] = pl.BlockSpec((n_prev + 1, tm, MLA_KV_RANK), lambda i: (0, i, 0))
        m_shape[1] = jax.ShapeDtypeStruct((n_prev + 1, n, MLA_KV_RANK), F32)
    o_specs, o_shape = _proj_out(tm, n)
    return pl.pallas_call(
        functools.partial(_out_in_body, with_kv=with_kv, n_prev=n_prev),
        grid=(n // tm,),
        in_specs=[row(D_MODEL), row(GLA_WIDTH), row(MLA_WIDTH), row(S5_WIDTH),
                  _layer_spec((D_MODEL, D_MODEL), layer, single_buffer=True),
                  _layer_spec((1, D_MODEL), layer + 1),
                  _layer_spec((IN_OFF['end'], D_MODEL), layer + 1, single_buffer=True)]
                 + [row(MLA_KV_RANK)] * n_prev + m_in,
        out_specs=[row(D_MODEL)] + o_specs + m_out,
        out_shape=[jax.ShapeDtypeStruct((n, D_MODEL), F32)] + o_shape + m_shape,
        scratch_shapes=[pltpu.VMEM((D_MODEL, D_MODEL), BF16), pltpu.VMEM((IN_OFF['end'], D_MODEL), BF16)],
        compiler_params=_cparams("arbitrary"),
        name="outproj_inproj",
    )(x2, og, om, os_, p['w_out'], p['ln'], p['w_in'], *ckv_prev, *m_args)


def _outproj(x2, og, om, os_, w, gain, layer, tm, final):
    n = x2.shape[0]
    row = lambda c: pl.BlockSpec((tm, c), lambda i: (i, 0))
    return pl.pallas_call(
        functools.partial(_outproj_body, final=final),
        grid=(n // tm,),
        in_specs=[row(D_MODEL), row(GLA_WIDTH), row(MLA_WIDTH), row(S5_WIDTH),
                  _layer_spec((D_MODEL, D_MODEL), layer, single_buffer=True),
                  _const_spec((1, D_MODEL))],
        out_specs=row(D_MODEL),
        out_shape=jax.ShapeDtypeStruct((n, D_MODEL), F32),
        scratch_shapes=[pltpu.VMEM((D_MODEL, D_MODEL), BF16)],
        compiler_params=_cparams("arbitrary"),
        name="outproj_final" if final else "outproj",
    )(x2, og, om, os_, w, gain)


def _prepare_params(ln_gain, w_in, gla_w_gate, gla_b_gate, gla_norm_gain, mla_q_norm_gain, mla_w_uq,
                    mla_kv_norm_gain, mla_w_ukv, s5_lambda_re, s5_lambda_im, s5_b_re, s5_b_im, s5_c_re, s5_c_im,
                    s5_d, s5_log_dt, s5_w_glu, s5_b_glu, w_out):
    depth = w_in.shape[0]
    w_t = jnp.swapaxes(w_in, 1, 2)
    wg = jnp.pad(gla_w_gate, ((0, 0), (0, GLA_QK - GLA_GATE_RANK), (0, 0))).astype(BF16)
    wq = mla_w_uq.reshape(depth, MLA_Q_RANK, MLA_HEADS, MLA_NOPE_DIM + MLA_ROPE_DIM)
    half = MLA_ROPE_DIM // 2
    wq = jnp.concatenate([wq, -wq[..., MLA_NOPE_DIM + half:], wq[..., MLA_NOPE_DIM:MLA_NOPE_DIM + half]], axis=-1)
    wq = jnp.pad(wq, ((0, 0), (0, 256 - MLA_Q_RANK), (0, 0), (0, 0)))
    wq = wq.reshape(depth, 256, MLA_QK_WIDTH).astype(BF16)
    gq = jnp.pad(mla_q_norm_gain, ((0, 0), (0, 256 - MLA_Q_RANK))).reshape(depth, 1, 256)
    wkv = mla_w_ukv.reshape(depth, MLA_KV_RANK, MLA_HEADS, MLA_NOPE_DIM + MLA_V_DIM)
    wk = jnp.pad(wkv[..., :MLA_NOPE_DIM], ((0, 0), (0, 0), (0, 0), (0, MLA_QK_PAD - MLA_NOPE_DIM)))
    wkv_r = jnp.concatenate([wk.reshape(depth, MLA_KV_RANK, MLA_QK_WIDTH),
                             wkv[..., MLA_NOPE_DIM:].reshape(depth, MLA_KV_RANK, MLA_WIDTH)], axis=2).astype(BF16)
    wka = jnp.pad(jnp.transpose(wkv[..., :MLA_NOPE_DIM], (0, 2, 3, 1)),
                  ((0, 0), (0, 0), (0, MLA_QK_PAD - MLA_NOPE_DIM), (0, 0))).astype(BF16)
    wv = jnp.transpose(wkv[..., MLA_NOPE_DIM:], (0, 2, 1, 3)).astype(BF16)
    dt = jnp.exp(s5_log_dt)[:, :, None]
    mag = jnp.exp(s5_lambda_re * dt)
    lbr, lbi = mag * jnp.cos(s5_lambda_im * dt), mag * jnp.sin(s5_lambda_im * dt)
    den = s5_lambda_re * s5_lambda_re + s5_lambda_im * s5_lambda_im
    qr = ((lbr - 1.0) * s5_lambda_re + lbi * s5_lambda_im) / den
    qi = (lbi * s5_lambda_re - (lbr - 1.0) * s5_lambda_im) / den
    b_re_t, b_im_t = jnp.swapaxes(s5_b_re, 2, 3), jnp.swapaxes(s5_b_im, 2, 3)
    bbr = qr[:, :, None, :] * b_re_t - qi[:, :, None, :] * b_im_t
    bbi = qr[:, :, None, :] * b_im_t + qi[:, :, None, :] * b_re_t
    rows = lambda m: m.reshape(depth, S5_WIDTH, S5_STATE)
    wb, wc = _s5_weights(rows(bbr), rows(bbi), rows(s5_c_re), rows(s5_c_im))
    return dict(
        ln=ln_gain.reshape(depth, 1, D_MODEL), w_in=w_t, wg=wg, bg=gla_b_gate.reshape(depth, 1, GLA_QK),
        gla_gain=jnp.tile(gla_norm_gain, (1, GLA_HEADS)).reshape(depth, 1, GLA_WIDTH),
        gq=gq, wq=wq, gkv=mla_kv_norm_gain.reshape(depth, 1, MLA_KV_RANK), wkv=wkv_r, wka=wka, wv=wv,
        lre=lbr.reshape(depth, 1, S5_NSTATE), lim=lbi.reshape(depth, 1, S5_NSTATE),
        wb=wb, wc=wc, d=s5_d.reshape(depth, 1, S5_WIDTH), wglu=s5_w_glu.astype(BF16),
        bglu=s5_b_glu.reshape(depth, 1, S5_WIDTH), w_out=w_out)


def _rope_tables(past, t, reps):
    half = MLA_ROPE_DIM // 2
    inv = ROPE_BASE ** (-np.arange(half, dtype=np.float64) / half)
    ang = (past + np.arange(t, dtype=np.float64))[:, None] * inv[None, :]
    cos, sin = np.cos(ang), np.sin(ang)
    pad = MLA_QK_PAD - MLA_NOPE_DIM - MLA_ROPE_DIM
    cos_t = np.concatenate([np.ones((t, MLA_NOPE_DIM)), cos, cos, np.zeros((t, pad))], axis=1)
    sin_t = np.concatenate([np.zeros((t, MLA_NOPE_DIM)), sin, sin, np.zeros((t, pad))], axis=1)
    return jnp.asarray(np.tile(cos_t, (reps, 1)), F32), jnp.asarray(np.tile(sin_t, (reps, 1)), F32)


def _trunk(x, gla_state, ckv_cache, kpe_cache, s5_re, s5_im, p, final_gain):
    bsz, t, _ = x.shape
    n = bsz * t
    depth = p['w_in'].shape[0]
    past = 0 if ckv_cache is None else ckv_cache.shape[2]
    s_len = past + t
    tl = _tiles(bsz, t, past)
    cos_t, sin_t = _rope_tables(past, t, max(1, max(tl['row'], tl['in_row']) // t))
    kpet_cache = None if kpe_cache is None else jnp.swapaxes(kpe_cache, 2, 3)
    x2 = x.reshape(n, D_MODEL)
    gain_f = final_gain.reshape(1, D_MODEL)
    gla_o, ckv_o, kpe_o, re_o, im_o = [], [], [], [], []
    with_kv = past == 0
    proj = _inproj(x2, p, cos_t, sin_t, 0, bsz, t, tl['in_row'], with_kv)
    ckv_stack = None
    for l in range(depth):
        pg, mz, ps, q, ckv_new = proj[:5]
        if ckv_new.ndim == 3:
            ckv_stack, ckv_new = ckv_new, ckv_new[l]
        s0 = jnp.zeros((bsz, GLA_HEADS, GLA_DK, GLA_DV), F32) if gla_state is None else gla_state[l]
        o_gla, s_new = _gla(pg.reshape(bsz, t, PG_COLS), p['wg'], p['bg'], p['gla_gain'], l, s0, bsz, t, tl)
        o_gla = o_gla.reshape(n, GLA_WIDTH)
        gla_o.append(s_new)
        if with_kv:
            kpet, k_cat, v_all = proj[5:]
            kpe_o.append(jnp.swapaxes(kpet, 1, 2))
            o_mla = _attn(q, k_cat, v_all, mz, bsz, t, s_len, past, tl['attn_q'], tl['attn_k'])
        else:
            kpe128 = proj[5]
            kpe_o.append(kpe128[:, MLA_NOPE_DIM:MLA_NOPE_DIM + MLA_ROPE_DIM].reshape(bsz, t, MLA_ROPE_DIM))
            o_mla = _attn_cached(q, mz, ckv_new, kpe128, ckv_cache, kpet_cache, p['wka'], p['wv'], l, bsz, t,
                                 tl['cached_seqs'])
        ckv_o.append(ckv_new)
        x0r = jnp.zeros((bsz, S5_NSTATE), F32) if s5_re is None else s5_re[l].reshape(bsz, S5_NSTATE)
        x0i = jnp.zeros((bsz, S5_NSTATE), F32) if s5_im is None else s5_im[l].reshape(bsz, S5_NSTATE)
        o_s5, xr, xi = _s5(ps.reshape(bsz, t, PS_COLS), x0r, x0i, p['lre'], p['lim'], p['wb'], p['wc'],
                           p['d'], p['wglu'], p['bglu'], l, bsz, t, tl['s5_rows'])
        re_o.append(xr.reshape(bsz, S5_GROUPS, S5_STATE))
        im_o.append(xi.reshape(bsz, S5_GROUPS, S5_STATE))
        o_s5 = o_s5.reshape(n, S5_WIDTH)
        if l < depth - 1:
            stack_here = l + 1 == depth - 1
            x2, *proj = _out_in(x2, o_gla, o_mla, o_s5, p, cos_t, sin_t, l, bsz, t, tl['row'], with_kv,
                                ckv_prev=tuple(ckv_o) if stack_here else ())
        else:
            x2 = _outproj(x2, o_gla, o_mla, o_s5, p['w_out'], gain_f, l, tl['out_row'], final=True)
    ckv_all = jnp.stack(ckv_o) if ckv_stack is None else ckv_stack
    return (x2.reshape(bsz, t, D_MODEL), jnp.stack(gla_o), ckv_all.reshape(depth, bsz, t, MLA_KV_RANK),
            jnp.stack(kpe_o), jnp.stack(re_o), jnp.stack(im_o))


def kernel(x_prompt, x_sample, state_gla, cache_mla_ckv, cache_mla_kpe, state_s5_re, state_s5_im, ln_gain, w_in, gla_w_gate, gla_b_gate, gla_norm_gain, mla_q_norm_gain, mla_w_uq, mla_kv_norm_gain, mla_w_ukv, s5_lambda_re, s5_lambda_im, s5_b_re, s5_b_im, s5_c_re, s5_c_im, s5_d, s5_log_dt, s5_w_glu, s5_b_glu, w_out, final_gain):
    p = _prepare_params(ln_gain, w_in, gla_w_gate, gla_b_gate, gla_norm_gain, mla_q_norm_gain, mla_w_uq,
                        mla_kv_norm_gain, mla_w_ukv, s5_lambda_re, s5_lambda_im, s5_b_re, s5_b_im,
                        s5_c_re, s5_c_im, s5_d, s5_log_dt, s5_w_glu, s5_b_glu, w_out)
    y_p, gla_p, ckv_p, kpe_p, re_p, im_p = _trunk(x_prompt, None, None, None, None, None, p, final_gain)
    y_s, gla_s, ckv_s, kpe_s, re_s, im_s = _trunk(x_sample, state_gla, cache_mla_ckv, cache_mla_kpe,
                                                  state_s5_re, state_s5_im, p, final_gain)
    return (y_p, y_s, gla_p, ckv_p, kpe_p, re_p, im_p, gla_s, ckv_s, kpe_s, re_s, im_s)
```

```python
import functools
import math

import numpy as np
import jax
import jax.numpy as jnp
from jax import lax
from jax.experimental import pallas as pl
from jax.experimental.pallas import tpu as pltpu

F32 = jnp.float32
BF16 = jnp.bfloat16

LANES = 128
D_MODEL = 1024
CHUNK = 64
EPS = 1e-6
GLA_HEADS = 4
GLA_DV = 64
GLA_DK = 32
GLA_WIDTH = GLA_HEADS * GLA_DV
GLA_QK = GLA_HEADS * GLA_DK
GLA_GATE_RANK = 16
GLA_GATE_TAU = 16.0
GLA_SUB = 16
GLA_PLAIN_MAX_DECAY = 60.0
MLA_HEADS = 4
MLA_NOPE_DIM = 64
MLA_ROPE_DIM = 32
MLA_V_DIM = 128
MLA_Q_RANK = 192
MLA_KV_RANK = 128
MLA_WIDTH = MLA_HEADS * MLA_V_DIM
MLA_QK_PAD = 128
MLA_QK_WIDTH = MLA_HEADS * MLA_QK_PAD
ROPE_BASE = 10000.0
S5_GROUPS = 16
S5_GROUP_CH = 16
S5_STATE = 64
S5_WIDTH = S5_GROUPS * S5_GROUP_CH
S5_NSTATE = S5_GROUPS * S5_STATE
S5_BATCH_TILE = 8
S5_SCAN_SLICES = 4
GELU_TANH_SCALE = math.sqrt(2.0 / math.pi)
GELU_TANH_CUBIC = 0.044715

PG_COLS = 896
PS_COLS = 512
_IN_SEGS = (('g_q', GLA_QK), ('g_k', GLA_QK), ('g_v', GLA_WIDTH), ('g_lr', GLA_GATE_RANK), ('g_z', GLA_WIDTH),
            ('m_cq', MLA_Q_RANK), ('m_ckv', MLA_KV_RANK), ('m_kr', MLA_ROPE_DIM), ('m_z', MLA_WIDTH),
            ('s_u', S5_WIDTH), ('s_z', S5_WIDTH), ('end', 0))
PG_Q, PG_K, PG_V, PG_Z, PG_LR = slice(0, 128), slice(128, 256), slice(256, 512), slice(512, 768), slice(768, 896)
PG_QKV = slice(PG_Q.start, PG_V.stop)
IN_OFF = dict(zip([n for n, _ in _IN_SEGS], np.cumsum([0] + [w for _, w in _IN_SEGS[:-1]]).tolist()))

VMEM_LIMIT_BYTES = 48 * 1024 * 1024


def _tiles(bsz, t, past):
    n = bsz * t
    s_len = past + t
    return dict(
        row=min(512, n),
        in_row=min(1024, n) if t >= 1024 or t < 512 else 512,
        out_row=min(1024, n),
        cached_seqs=math.gcd(bsz, max(1, 128 // t)),
        gla_rows=min(512, t), gla_chunk=min(CHUNK, t),
        gla_seqs=math.gcd(bsz, max(2, 128 // t)),
        attn_q=min(512, t), attn_k=min(512, s_len),
        s5_rows=min(64, t))


def _cparams(*sem):
    return pltpu.CompilerParams(dimension_semantics=sem, vmem_limit_bytes=VMEM_LIMIT_BYTES)


def _sigmoid(x):
    return 0.5 * (1.0 + jnp.tanh(0.5 * x))


def _dot(a, b):
    return jnp.dot(a, b, preferred_element_type=F32)


def _dot_t(a, b):
    return lax.dot_general(a, b, (((0,), (0,)), ((), ())), preferred_element_type=F32)


def _const_spec(shape):
    zeros = (0,) * len(shape)
    return pl.BlockSpec(shape, lambda *_: zeros)


def _layer_spec(shape, layer, single_buffer=False):
    zeros = (0,) * len(shape)
    mode = dict(pipeline_mode=pl.Buffered(1)) if single_buffer else {}
    return pl.BlockSpec((None,) + tuple(shape), lambda *_: (layer,) + zeros, **mode)


def _rope128(x, cos_t, sin_t):
    lane = lax.broadcasted_iota(jnp.int32, x.shape, 1)
    first_half = (lane >= MLA_NOPE_DIM) & (lane < MLA_NOPE_DIM + MLA_ROPE_DIM // 2)
    rot = jnp.where(first_half, -pltpu.roll(x, LANES - MLA_ROPE_DIM // 2, 1), pltpu.roll(x, MLA_ROPE_DIM // 2, 1))
    return x * cos_t + rot * sin_t


def _mla_prep_rows(cq, ckv, kr, mla, with_kv):
    if with_kv:
        cos_ref, sin_ref, gq_ref, wq_ref, gkv_ref, wkv_ref, q_ref, ckv_ref, kpet_ref, k_ref, v_ref = mla
    else:
        cos_ref, sin_ref, gq_ref, wq_ref, gkv_ref, q_ref, ckv_ref, kpe_ref = mla
    cos_t = cos_ref[...]
    sin_t = sin_ref[...]
    ones = jnp.ones((2 * LANES, LANES), BF16)
    ms = _dot((cq * cq).astype(BF16), ones) * (1.0 / MLA_Q_RANK)
    inv = lax.rsqrt(ms + EPS)
    cqn = (cq * jnp.concatenate([inv, inv], axis=1) * gq_ref[...]).astype(BF16)
    qh = _dot(cqn, wq_ref[...])
    scale = (MLA_NOPE_DIM + MLA_ROPE_DIM) ** -0.5 * math.log2(math.e)
    for h in range(MLA_HEADS):
        x = qh[:, h * MLA_QK_PAD:(h + 1) * MLA_QK_PAD]
        roped = x * cos_t + pltpu.roll(x, LANES - MLA_ROPE_DIM, 1) * sin_t
        q_ref[:, h * MLA_QK_PAD:(h + 1) * MLA_QK_PAD] = (roped * scale).astype(BF16)
    ms = _dot((ckv * ckv).astype(BF16), ones[0:LANES]) * (1.0 / MLA_KV_RANK)
    ckv_n = ckv * lax.rsqrt(ms + EPS) * gkv_ref[...]
    ckv_ref[...] = ckv_n
    kpe128 = _rope128(kr, cos_t, sin_t)
    if with_kv:
        kpet_ref[...] = jnp.transpose(kpe128)[MLA_NOPE_DIM:MLA_NOPE_DIM + MLA_ROPE_DIM, :]
        kv = _dot(ckv_n.astype(BF16), wkv_ref[...])
        for h in range(MLA_HEADS):
            sl = slice(h * MLA_QK_PAD, (h + 1) * MLA_QK_PAD)
            k_ref[:, sl] = (kv[:, sl] + kpe128).astype(BF16)
        v_ref[...] = kv[:, MLA_QK_WIDTH:].astype(BF16)
    else:
        kpe_ref[...] = kpe128


def _inproj_rows(x, g_ref, w_scr, og_ref, mz_ref, os_ref, mla, with_kv):
    ms = jnp.mean(x * x, axis=-1, keepdims=True)
    h = (x * lax.rsqrt(ms + EPS) * g_ref[...]).astype(BF16)

    def seg(a, b):
        return lax.dot_general(h, w_scr[a:b, :], (((1,), (1,)), ((), ())), preferred_element_type=F32)

    lane = lax.broadcasted_iota(jnp.int32, (x.shape[0], LANES), 1)
    lane2 = lax.broadcasted_iota(jnp.int32, (x.shape[0], 2 * LANES), 1)
    c = IN_OFF
    og_ref[:, PG_QKV] = seg(c['g_q'], c['g_lr']).astype(BF16)
    og_ref[:, PG_Z] = seg(c['g_z'], c['m_cq']).astype(BF16)
    og_ref[:, PG_LR] = jnp.where(lane < GLA_GATE_RANK, seg(c['g_lr'], c['g_lr'] + LANES), 0.0).astype(BF16)
    mz_ref[...] = seg(c['m_z'], c['s_u']).astype(BF16)
    os_ref[...] = seg(c['s_u'], c['end']).astype(BF16)
    cq = jnp.where(lane2 < MLA_Q_RANK, seg(c['m_cq'], c['m_cq'] + 2 * LANES), 0.0)
    ckv_kr = seg(c['m_ckv'], c['m_ckv'] + 2 * LANES)
    kr = pltpu.roll(ckv_kr[:, LANES:], MLA_NOPE_DIM, 1)
    kr = jnp.where((lane >= MLA_NOPE_DIM) & (lane < MLA_NOPE_DIM + MLA_ROPE_DIM), kr, 0.0)
    _mla_prep_rows(cq, ckv_kr[:, 0:LANES], kr, mla, with_kv)


N_MLA_IN = {True: 6, False: 5}
N_MLA_OUT = {True: 5, False: 3}


def _inproj_body(x_ref, g_ref, wt_ref, *rest, with_kv):
    n_in, n_out = N_MLA_IN[with_kv], N_MLA_OUT[with_kv]
    mla_in, (og_ref, mz_ref, os_ref), mla_out, (w_scr,) = (
        rest[:n_in], rest[n_in:n_in + 3], rest[n_in + 3:n_in + 3 + n_out], rest[n_in + 3 + n_out:])

    @pl.when(pl.program_id(0) == 0)
    def _():
        w_scr[...] = wt_ref[...].astype(BF16)

    _inproj_rows(x_ref[...], g_ref, w_scr, og_ref, mz_ref, os_ref, tuple(mla_in) + tuple(mla_out), with_kv)


def _mla_specs(p, cos_t, sin_t, layer, bsz, t, tm, with_kv):
    n = bsz * t
    ntab = max(1, t // tm)
    assert not with_kv or tm <= t
    row = lambda c: pl.BlockSpec((tm, c), lambda i: (i, 0))
    table = pl.BlockSpec((tm, LANES), lambda i: (i % ntab, 0))
    args = [cos_t, sin_t, p['gq'], p['wq'], p['gkv']]
    in_specs = [table, table, _layer_spec((1, 256), layer), _layer_spec((256, MLA_QK_WIDTH), layer),
                _layer_spec((1, MLA_KV_RANK), layer)]
    out_specs = [row(MLA_QK_WIDTH), row(MLA_KV_RANK)]
    out_shape = [jax.ShapeDtypeStruct((n, MLA_QK_WIDTH), BF16), jax.ShapeDtypeStruct((n, MLA_KV_RANK), F32)]
    if with_kv:
        args.append(p['wkv'])
        in_specs.append(_layer_spec((MLA_KV_RANK, MLA_QK_WIDTH + MLA_WIDTH), layer))
        out_specs += [pl.BlockSpec((None, MLA_ROPE_DIM, tm), lambda i: (i // ntab, 0, i % ntab)),
                      row(MLA_QK_WIDTH), row(MLA_WIDTH)]
        out_shape += [jax.ShapeDtypeStruct((bsz, MLA_ROPE_DIM, t), F32),
                      jax.ShapeDtypeStruct((n, MLA_QK_WIDTH), BF16), jax.ShapeDtypeStruct((n, MLA_WIDTH), BF16)]
    else:
        out_specs.append(row(LANES))
        out_shape.append(jax.ShapeDtypeStruct((n, LANES), F32))
    return args, in_specs, out_specs, out_shape


def _proj_out(tm, n):
    row = lambda c: pl.BlockSpec((tm, c), lambda i: (i, 0))
    return ([row(PG_COLS), row(MLA_WIDTH), row(PS_COLS)],
            [jax.ShapeDtypeStruct((n, PG_COLS), BF16), jax.ShapeDtypeStruct((n, MLA_WIDTH), BF16),
             jax.ShapeDtypeStruct((n, PS_COLS), BF16)])


def _inproj(x2, p, cos_t, sin_t, layer, bsz, t, tm, with_kv):
    n = x2.shape[0]
    m_args, m_in, m_out, m_shape = _mla_specs(p, cos_t, sin_t, layer, bsz, t, tm, with_kv)
    o_specs, o_shape = _proj_out(tm, n)
    return pl.pallas_call(
        functools.partial(_inproj_body, with_kv=with_kv),
        grid=(n // tm,),
        in_specs=[pl.BlockSpec((tm, D_MODEL), lambda i: (i, 0)),
                  _layer_spec((1, D_MODEL), layer),
                  _layer_spec((IN_OFF['end'], D_MODEL), layer, single_buffer=True)] + m_in,
        scratch_shapes=[pltpu.VMEM((IN_OFF['end'], D_MODEL), BF16)],
        out_specs=o_specs + m_out,
        out_shape=o_shape + m_shape,
        compiler_params=_cparams("arbitrary"),
        name="inproj",
    )(x2, p['ln'], p['w_in'], *m_args)


def _gla_body(p_ref, wg_ref, bg_ref, gain_ref, ones_k_ref, ones_v_ref, bd_ref, s0_ref,
              o_ref, sout_ref, q_scr, k_scr, v_scr, b_scr, s_scr, qe_scr, qs_scr, ke_scr, vb_scr, o_scr,
              *, chunk, nch, nseq):
    it = pl.program_id(1)
    sub = min(GLA_SUB, chunk)
    nsub = chunk // sub

    @pl.when(it == 0)
    def _():
        s_scr[...] = jnp.zeros(s_scr.shape, F32)
        for sq in range(nseq):
            for h in range(GLA_HEADS):
                s_scr[sq, h * GLA_DK:(h + 1) * GLA_DK, h * GLA_DV:(h + 1) * GLA_DV] = s0_ref[sq, h]

    def cols(lanes):
        parts = [p_ref[sq, :, lanes] for sq in range(nseq)]
        return parts[0] if nseq == 1 else jnp.concatenate(parts, axis=0)

    q_scr[...] = cols(PG_Q).astype(F32) * (GLA_DK ** -0.5)
    k_scr[...] = cols(PG_K).astype(F32)
    v_scr[...] = cols(PG_V).astype(F32)
    logit = _dot(cols(PG_LR), wg_ref[...]) + bg_ref[...]
    log_a = (jnp.minimum(logit, 0.0) - jnp.log(1.0 + jnp.exp(-jnp.abs(logit)))) * (1.0 / GLA_GATE_TAU)
    row_in_chunk = lax.broadcasted_iota(jnp.int32, log_a.shape, 0) % chunk
    b_all = log_a
    shift = 1
    while shift < chunk:
        b_all = b_all + jnp.where(row_in_chunk >= shift, pltpu.roll(b_all, shift, 0), 0.0)
        shift *= 2
    b_scr[...] = b_all

    row = lax.broadcasted_iota(jnp.int32, (chunk, GLA_QK), 0)
    row_in_sub = lax.broadcasted_iota(jnp.int32, (sub, GLA_QK), 0)

    def decay_columns(b_end):
        col = jnp.transpose(jnp.broadcast_to(jnp.exp(b_end), (GLA_QK, GLA_QK)))
        return jnp.concatenate([col, col], axis=1)

    def load_chunk(sq, c):
        r0 = pl.multiple_of((sq * nch + c) * chunk, chunk)
        return (r0, q_scr[pl.ds(r0, chunk), :], k_scr[pl.ds(r0, chunk), :], v_scr[pl.ds(r0, chunk), :],
                b_scr[pl.ds(r0, chunk), :], b_scr[pl.ds(r0 + chunk - 1, 1), :], s_scr[sq])

    def finish_chunk(sq, r0, o, s_prev, a_state, b_end):
        ms = _dot((o * o).astype(BF16), ones_v_ref[...]) * (1.0 / GLA_DV)
        o_n = o * lax.rsqrt(ms + EPS) * gain_ref[...]
        t0 = r0 - sq * nch * chunk
        z = p_ref[sq, pl.ds(t0, chunk), PG_Z].astype(F32)
        o_ref[sq, pl.ds(t0, chunk), :] = (o_n * (z * _sigmoid(z))).astype(BF16)
        s_scr[sq] = s_prev * decay_columns(b_end) + a_state

    def robust_chunk(c, carry, sq):
        r0, qc, kc, vc, bc, b_end, s_prev = load_chunk(sq, c)
        xs = [qc * jnp.exp(bc)]
        ks = []
        for sj in range(nsub - 1):
            e_j = b_scr[pl.ds(r0 + (sj + 1) * sub - 1, 1), :]
            later = row >= (sj + 1) * sub
            xs.append(jnp.where(later, qc * jnp.exp(jnp.where(later, bc - e_j, 0.0)), 0.0))
            own = (row >= sj * sub) & (row < (sj + 1) * sub)
            ks.append(jnp.where(own, kc * jnp.exp(jnp.where(own, e_j - bc, 0.0)), 0.0))
        ks.append(kc * jnp.exp(b_end - bc))
        k_all = jnp.concatenate(ks, axis=1).astype(BF16)
        a_all = _dot_t(k_all, vc.astype(BF16)) * bd_ref[...]
        w = jnp.concatenate([s_prev, a_all[:(nsub - 1) * GLA_QK]], axis=0).astype(BF16) if nsub > 1 \
            else s_prev.astype(BF16)
        o_off = _dot(jnp.concatenate(xs, axis=1).astype(BF16), w)
        rows = []
        for si in range(nsub):
            q_i = qc[si * sub:(si + 1) * sub]
            b_i = bc[si * sub:(si + 1) * sub]
            es = []
            for j in range(sub):
                r = r0 + si * sub + j
                b_j = b_scr[pl.ds(r, 1), :]
                k_j = k_scr[pl.ds(r, 1), :]
                valid = row_in_sub >= j
                es.append(q_i * k_j * jnp.exp(jnp.where(valid, b_i - b_j, -jnp.inf)))
            e_all = jnp.concatenate(es, axis=0)
            e_hi = e_all.astype(BF16)
            e_lo = (e_all - e_hi.astype(F32)).astype(BF16)
            p_all = _dot(e_hi, ones_k_ref[...]) + _dot(e_lo, ones_k_ref[...])
            acc = o_off[si * sub:(si + 1) * sub]
            for j in range(sub):
                v_j = v_scr[pl.ds(r0 + si * sub + j, 1), :]
                acc = acc + p_all[j * sub:(j + 1) * sub] * v_j
            rows.append(acc)
        o = rows[0] if nsub == 1 else jnp.concatenate(rows, axis=0)
        finish_chunk(sq, r0, o, s_prev, a_all[(nsub - 1) * GLA_QK:], b_end)
        return carry

    lane_head_v = lax.broadcasted_iota(jnp.int32, (chunk, GLA_WIDTH), 1) // GLA_DV
    causal = (lax.broadcasted_iota(jnp.int32, (GLA_HEADS * chunk, chunk), 0) % chunk
              >= lax.broadcasted_iota(jnp.int32, (GLA_HEADS * chunk, chunk), 1))

    def plain_block():
        tt = nseq * nch * chunk
        b_all = b_scr[...]
        q_all = q_scr[...]
        k_all = k_scr[...]
        qe = q_all * jnp.exp(b_all)
        lane_head = lax.broadcasted_iota(jnp.int32, (tt, GLA_QK), 1) // GLA_DK
        qe_scr[...] = qe.astype(BF16)
        for h in range(GLA_HEADS):
            qs_scr[h] = jnp.where(lane_head == h, qe, 0.0).astype(BF16)
        ke_scr[...] = (k_all * jnp.exp(-b_all)).astype(BF16)
        vb_scr[...] = cols(PG_V)
        states = [s_scr[sq] for sq in range(nseq)]
        for c, sq in [(c, sq) for c in range(nch) for sq in range(nseq)]:
            r0 = (sq * nch + c) * chunk
            rs = slice(r0, r0 + chunk)
            s_cur = states[sq]
            qs = jnp.concatenate([qs_scr[h, rs, :] for h in range(GLA_HEADS)], axis=0)
            s = lax.dot_general(qs, ke_scr[rs, :], (((1,), (1,)), ((), ())), preferred_element_type=F32)
            s = jnp.where(causal, s, 0.0).astype(BF16)
            r = _dot(s, vb_scr[rs, :])
            o = _dot(qe_scr[rs, :], s_cur.astype(BF16))
            for h in range(GLA_HEADS):
                o = o + jnp.where(lane_head_v == h, r[h * chunk:(h + 1) * chunk], 0.0)
            o_scr[rs, :] = o
            b_end = b_scr[r0 + chunk - 1:r0 + chunk, :]
            k_end = (k_scr[rs, :] * jnp.exp(b_end - b_scr[rs, :])).astype(BF16)
            a_state = _dot_t(k_end, vb_scr[rs, :]) * bd_ref[0:GLA_QK, :]
            states[sq] = s_cur * decay_columns(b_end) + a_state
        for sq in range(nseq):
            s_scr[sq] = states[sq]
        o = o_scr[...]
        ms = _dot((o * o).astype(BF16), ones_v_ref[...]) * (1.0 / GLA_DV)
        o_n = o * lax.rsqrt(ms + EPS) * gain_ref[...]
        z = cols(PG_Z).astype(F32)
        o_all = (o_n * (z * _sigmoid(z))).astype(BF16)
        for sq in range(nseq):
            o_ref[sq] = o_all[sq * nch * chunk:(sq + 1) * nch * chunk]

    in_range = jnp.max(-b_scr[...]) < GLA_PLAIN_MAX_DECAY

    @pl.when(in_range)
    def _():
        plain_block()

    @pl.when(jnp.logical_not(in_range))
    def _():
        for sq in range(nseq):
            lax.fori_loop(0, nch, functools.partial(robust_chunk, sq=sq), 0)

    @pl.when(it == pl.num_programs(1) - 1)
    def _():
        for sq in range(nseq):
            for h in range(GLA_HEADS):
                sout_ref[sq, h] = s_scr[sq, h * GLA_DK:(h + 1) * GLA_DK, h * GLA_DV:(h + 1) * GLA_DV]


def _gla_consts(chunk):
    nsub = chunk // min(GLA_SUB, chunk)
    hk = np.arange(GLA_QK) // GLA_DK
    hv = np.arange(GLA_WIDTH) // GLA_DV
    same_kv = (hk[:, None] == hv[None, :]).astype(np.float32)
    same_vv = (hv[:, None] == hv[None, :]).astype(np.float32)
    return (jnp.asarray(same_kv, BF16), jnp.asarray(same_vv, BF16),
            jnp.asarray(np.tile(same_kv, (nsub, 1)), F32))


def _gla(pg, wg, bg, gain, layer, s0, bsz, t, tl):
    chunk, tt, nseq = tl['gla_chunk'], tl['gla_rows'], tl['gla_seqs']
    nch = tt // chunk
    nt = t // tt
    rows = nseq * tt
    nsub = chunk // min(GLA_SUB, chunk)
    ones_k, ones_v, bd = _gla_consts(chunk)
    const = _const_spec
    return pl.pallas_call(
        functools.partial(_gla_body, chunk=chunk, nch=nch, nseq=nseq),
        grid=(bsz // nseq, nt),
        in_specs=[pl.BlockSpec((nseq, tt, PG_COLS), lambda b, i: (b, i, 0)),
                  _layer_spec((GLA_QK, GLA_QK), layer), _layer_spec((1, GLA_QK), layer),
                  _layer_spec((1, GLA_WIDTH), layer),
                  const((GLA_QK, GLA_WIDTH)), const((GLA_WIDTH, GLA_WIDTH)),
                  const((nsub * GLA_QK, GLA_WIDTH)),
                  pl.BlockSpec((nseq, GLA_HEADS, GLA_DK, GLA_DV), lambda b, i: (b, 0, 0, 0))],
        out_specs=[pl.BlockSpec((nseq, tt, GLA_WIDTH), lambda b, i: (b, i, 0)),
                   pl.BlockSpec((nseq, GLA_HEADS, GLA_DK, GLA_DV), lambda b, i: (b, 0, 0, 0))],
        out_shape=[jax.ShapeDtypeStruct((bsz, t, GLA_WIDTH), BF16),
                   jax.ShapeDtypeStruct((bsz, GLA_HEADS, GLA_DK, GLA_DV), F32)],
        scratch_shapes=[pltpu.VMEM((rows, GLA_QK), F32), pltpu.VMEM((rows, GLA_QK), F32),
                        pltpu.VMEM((rows, GLA_WIDTH), F32), pltpu.VMEM((rows, GLA_QK), F32),
                        pltpu.VMEM((nseq, GLA_QK, GLA_WIDTH), F32),
                        pltpu.VMEM((rows, GLA_QK), BF16), pltpu.VMEM((GLA_HEADS, rows, GLA_QK), BF16),
                        pltpu.VMEM((rows, GLA_QK), BF16),
                        pltpu.VMEM((rows, GLA_WIDTH), BF16), pltpu.VMEM((rows, GLA_WIDTH), F32)],
        compiler_params=_cparams("parallel", "arbitrary"),
        name="gla",
    )(pg, wg, bg, gain, ones_k, ones_v, bd, s0)


def _attn_cached_body(q_ref, z_ref, ckv_new_ref, kpe_new_ref, ckv_past_ref, kpet_past_ref, wka_ref, wv_ref,
                      o_ref, kpet_scr, *, past, t, nseq):
    last = (((1,), (1,)), ((), ()))
    hsl = [slice(h * MLA_QK_PAD, (h + 1) * MLA_QK_PAD) for h in range(MLA_HEADS)]
    q_lat_h = [_dot(q_ref[:, hsl[h]], wka_ref[h]).astype(BF16) for h in range(MLA_HEADS)]
    kpet_scr[...] = jnp.zeros(kpet_scr.shape, BF16)
    o_lat = []
    for sq in range(nseq):
        rs = slice(sq * t, (sq + 1) * t)
        q_rows = jnp.concatenate([q_ref[rs, hsl[h]] for h in range(MLA_HEADS)], axis=0)
        q_lat = jnp.concatenate([q_lat_h[h][rs] for h in range(MLA_HEADS)], axis=0)
        c_past = ckv_past_ref[sq].astype(BF16)
        c_new = ckv_new_ref[rs, :].astype(BF16)
        kpet_scr[sq, MLA_NOPE_DIM:MLA_NOPE_DIM + MLA_ROPE_DIM, :] = kpet_past_ref[sq].astype(BF16)
        s_past = (lax.dot_general(q_lat, c_past, last, preferred_element_type=F32)
                  + _dot(q_rows, kpet_scr[sq]))
        s_new = (lax.dot_general(q_lat, c_new, last, preferred_element_type=F32)
                 + lax.dot_general(q_rows, kpe_new_ref[rs, :].astype(BF16), last, preferred_element_type=F32))
        if past // CHUNK != (past + t - 1) // CHUNK:
            q_chunk = (past + lax.broadcasted_iota(jnp.int32, s_past.shape, 0) % t) // CHUNK
            s_past = jnp.where(lax.broadcasted_iota(jnp.int32, s_past.shape, 1) // CHUNK <= q_chunk, s_past, -jnp.inf)
            q_chunk = (past + lax.broadcasted_iota(jnp.int32, s_new.shape, 0) % t) // CHUNK
            s_new = jnp.where((past + lax.broadcasted_iota(jnp.int32, s_new.shape, 1)) // CHUNK <= q_chunk,
                              s_new, -jnp.inf)
        m = jnp.maximum(jnp.max(s_past, axis=-1, keepdims=True), jnp.max(s_new, axis=-1, keepdims=True))
        p_past = jnp.exp2(s_past - m)
        p_new = jnp.exp2(s_new - m)
        l = jnp.sum(p_past, axis=-1, keepdims=True) + jnp.sum(p_new, axis=-1, keepdims=True)
        o_lat.append(((_dot(p_past.astype(BF16), c_past) + _dot(p_new.astype(BF16), c_new)) / l).astype(BF16))
    for h in range(MLA_HEADS):
        vsl = slice(h * MLA_V_DIM, (h + 1) * MLA_V_DIM)
        o_h = jnp.concatenate([o_lat[sq][h * t:(h + 1) * t] for sq in range(nseq)], axis=0)
        z = z_ref[:, vsl].astype(F32)
        o_ref[:, vsl] = (_dot(o_h, wv_ref[h]) * (z * _sigmoid(z))).astype(BF16)


def _attn_cached(q, mz, ckv_new, kpe128, ckv_past, kpet_past, wka, wv, layer, bsz, t, nseq):
    past = ckv_past.shape[2]
    rows = nseq * t
    return pl.pallas_call(
        functools.partial(_attn_cached_body, past=past, t=t, nseq=nseq),
        grid=(bsz // nseq,),
        in_specs=[pl.BlockSpec((rows, MLA_QK_WIDTH), lambda b: (b, 0)),
                  pl.BlockSpec((rows, MLA_WIDTH), lambda b: (b, 0)),
                  pl.BlockSpec((rows, MLA_KV_RANK), lambda b: (b, 0)),
                  pl.BlockSpec((rows, LANES), lambda b: (b, 0)),
                  pl.BlockSpec((None, nseq, past, MLA_KV_RANK), lambda b: (layer, b, 0, 0)),
                  pl.BlockSpec((None, nseq, MLA_ROPE_DIM, past), lambda b: (layer, b, 0, 0)),
                  _layer_spec((MLA_HEADS, MLA_QK_PAD, MLA_KV_RANK), layer),
                  _layer_spec((MLA_HEADS, MLA_KV_RANK, MLA_V_DIM), layer)],
        out_specs=pl.BlockSpec((rows, MLA_WIDTH), lambda b: (b, 0)),
        out_shape=jax.ShapeDtypeStruct((bsz * t, MLA_WIDTH), BF16),
        scratch_shapes=[pltpu.VMEM((nseq, MLA_QK_PAD, past), BF16)],
        compiler_params=_cparams("parallel"),
        name="mla_attn_cached",
    )(q, mz, ckv_new, kpe128, ckv_past, kpet_past, wka, wv)


def _attn_body(q_ref, k_ref, v_ref, z_ref, o_ref, m_scr, acc_scr, *, past, tq, tk, s_len):
    iq = pl.program_id(1)
    q_first = past + iq * tq
    full_keys = jnp.minimum((q_first // CHUNK + 1) * CHUNK, s_len)
    vis_keys = jnp.minimum(((q_first + tq - 1) // CHUNK + 1) * CHUNK, s_len)
    n_full = full_keys // tk
    n_vis = (vis_keys + tk - 1) // tk
    split_diagonal = past % tk == 0 and tq == tk and (tq // 2) % CHUNK == 0

    def block(kb, carry, masked, first=False):
        k0 = pl.multiple_of(kb * tk, tk)
        if masked and split_diagonal:
            parts = [(r * (tq // 2), tq // 2, (r + 1) * (tk // 2)) for r in range(2)]
        else:
            parts = [(0, tq, tk)]
        for r0, nr, kext in parts:
            rows = slice(r0, r0 + nr)
            if masked:
                q_pos = (r0 if split_diagonal else q_first + r0) + lax.broadcasted_iota(jnp.int32, (nr, kext), 0)
                k_pos = (0 if split_diagonal else k0) + lax.broadcasted_iota(jnp.int32, (nr, kext), 1)
                visible = k_pos // CHUNK <= q_pos // CHUNK
            ntile, rem = kext // LANES, kext % LANES
            ones_v = jnp.ones((kext, MLA_V_DIM), BF16)
            for h in range(MLA_HEADS):
                sl = slice(h * MLA_QK_PAD, (h + 1) * MLA_QK_PAD)
                vsl = slice(h * MLA_V_DIM, (h + 1) * MLA_V_DIM)
                s = lax.dot_general(q_ref[rows, sl], k_ref[pl.ds(k0, kext), sl], (((1,), (1,)), ((), ())),
                                    preferred_element_type=F32)
                if masked:
                    s = jnp.where(visible, s, -jnp.inf)
                if first:
                    m_new = jnp.broadcast_to(jnp.max(s, axis=-1, keepdims=True), (nr, LANES))
                else:
                    m_prev = m_scr[h, rows]
                    m_new = jnp.maximum(m_prev, jnp.max(s, axis=-1, keepdims=True))
                    alpha = jnp.exp2(m_prev - m_new)
                ps = [jnp.exp2(s[:, c * LANES:(c + 1) * LANES] - m_new) for c in range(ntile)]
                if rem:
                    ps.append(jnp.exp2(s[:, ntile * LANES:] - m_new[:, :rem]))
                p = jnp.concatenate(ps, axis=1).astype(BF16)
                v_ext = jnp.concatenate([v_ref[pl.ds(k0, kext), vsl], ones_v], axis=1)
                pv = _dot(p, v_ext)
                acc_scr[h, rows] = pv if first else jnp.concatenate([alpha, alpha], axis=1) * acc_scr[h, rows] + pv
                m_scr[h, rows] = m_new
        return carry

    @pl.when(n_full > 0)
    def _():
        block(0, 0, masked=False, first=True)
        lax.fori_loop(1, n_full, functools.partial(block, masked=False), 0)
        lax.fori_loop(n_full, n_vis, functools.partial(block, masked=True), 0)

    @pl.when(n_full == 0)
    def _():
        block(0, 0, masked=True, first=True)
        lax.fori_loop(1, n_vis, functools.partial(block, masked=True), 0)

    for h in range(MLA_HEADS):
        vsl = slice(h * MLA_V_DIM, (h + 1) * MLA_V_DIM)
        z = z_ref[:, vsl].astype(F32)
        acc = acc_scr[h]
        o_ref[:, vsl] = (acc[:, :MLA_V_DIM] / acc[:, MLA_V_DIM:] * (z * _sigmoid(z))).astype(BF16)


def _attn(q, k, v, mz, bsz, t, s_len, past, tq, tk):
    nq = t // tq
    return pl.pallas_call(
        functools.partial(_attn_body, past=past, tq=tq, tk=tk, s_len=s_len),
        grid=(bsz, nq),
        in_specs=[pl.BlockSpec((tq, MLA_QK_WIDTH), lambda b, iq: (b * nq + iq, 0)),
                  pl.BlockSpec((s_len, MLA_QK_WIDTH), lambda b, iq: (b, 0)),
                  pl.BlockSpec((s_len, MLA_WIDTH), lambda b, iq: (b, 0)),
                  pl.BlockSpec((tq, MLA_WIDTH), lambda b, iq: (b * nq + iq, 0))],
        out_specs=pl.BlockSpec((tq, MLA_WIDTH), lambda b, iq: (b * nq + iq, 0)),
        out_shape=jax.ShapeDtypeStruct((bsz * t, MLA_WIDTH), BF16),
        scratch_shapes=[pltpu.VMEM((MLA_HEADS, tq, LANES), F32), pltpu.VMEM((MLA_HEADS, tq, 2 * MLA_V_DIM), F32)],
        compiler_params=_cparams("parallel", "arbitrary"),
        name="mla_attn",
    )(q, k, v, mz)


def _s5_weights_body(bre_ref, bim_ref, cre_ref, cim_ref, spread_ref, mask_ref, wb_ref, wc_ref):
    def expand(m_ref):
        return _dot(m_ref[...].astype(BF16), spread_ref[...]) * mask_ref[...]

    wb_ref[:, 0:S5_NSTATE] = expand(bre_ref).astype(BF16)
    wb_ref[:, S5_NSTATE:] = expand(bim_ref).astype(BF16)
    wc_ref[0:S5_NSTATE, :] = jnp.transpose(expand(cre_ref)).astype(BF16)
    wc_ref[S5_NSTATE:, :] = jnp.transpose(-expand(cim_ref)).astype(BF16)


def _s5_weights(bbr, bbi, c_re, c_im):
    depth = bbr.shape[0]
    g_row = np.arange(S5_WIDTH) // S5_GROUP_CH
    g_col = np.arange(S5_NSTATE) // S5_STATE
    mask = (g_row[:, None] == g_col[None, :]).astype(np.float32)
    spread = (np.arange(S5_STATE)[:, None] == (np.arange(S5_NSTATE) % S5_STATE)[None, :]).astype(np.float32)
    small = pl.BlockSpec((None, S5_WIDTH, S5_STATE), lambda l: (l, 0, 0))
    return pl.pallas_call(
        _s5_weights_body,
        grid=(depth,),
        in_specs=[small, small, small, small,
                  _const_spec((S5_STATE, S5_NSTATE)), _const_spec((S5_WIDTH, S5_NSTATE))],
        out_specs=[pl.BlockSpec((None, S5_WIDTH, 2 * S5_NSTATE), lambda l: (l, 0, 0)),
                   pl.BlockSpec((None, 2 * S5_NSTATE, S5_WIDTH), lambda l: (l, 0, 0))],
        out_shape=[jax.ShapeDtypeStruct((depth, S5_WIDTH, 2 * S5_NSTATE), BF16),
                   jax.ShapeDtypeStruct((depth, 2 * S5_NSTATE, S5_WIDTH), BF16)],
        compiler_params=_cparams("parallel"),
        name="s5_weights",
    )(bbr, bbi, c_re, c_im, jnp.asarray(spread, BF16), jnp.asarray(mask, F32))


def _s5_body(p0_ref, pn_ref, x0r_ref, x0i_ref, lre_ref, lim_ref, wb_ref, wc_ref, d_ref, wglu_ref, bglu_ref, never_ref,
             o_ref, xr_out, xi_out, uz_bt, uz_a, uz_b, uz_c, bu_a, bu_b, bu_c, o_tb, xr_s, xi_s, *, lc, pitch, nt):
    it = pl.program_id(1)
    nb = S5_BATCH_TILE
    ring = ((uz_a, bu_a), (uz_b, bu_b), (uz_c, bu_c))

    def stage_in(blk_ref, uz_tb, bu):
        for b in range(nb):
            for c in range(PS_COLS // LANES):
                uz_bt[c, b * pitch:b * pitch + lc, :] = blk_ref[b, :, c * LANES:(c + 1) * LANES].astype(F32)
        for t in range(lc):
            for c in range(PS_COLS // LANES):
                uz_tb[t * nb:(t + 1) * nb, c * LANES:(c + 1) * LANES] = uz_bt[c, pl.ds(t, nb, stride=pitch), :]
        bu[...] = _dot(uz_tb[:, 0:S5_WIDTH].astype(BF16), wb_ref[...])

    def stage_scan(bu):
        w = S5_NSTATE // S5_SCAN_SLICES
        never = never_ref[...] != 0
        last = None
        for c0 in range(0, S5_NSTATE, w):
            re, im = slice(c0, c0 + w), slice(S5_NSTATE + c0, S5_NSTATE + c0 + w)
            lre = jnp.broadcast_to(lre_ref[:, re], (nb, w))
            lim = jnp.broadcast_to(lim_ref[:, re], (nb, w))
            xr, xi = xr_s[:, re], xi_s[:, re]
            if last is not None:
                xr = jnp.where(never, last, xr)
            for t in range(lc):
                rs = slice(t * nb, (t + 1) * nb)
                xr, xi = lre * xr - lim * xi + bu[rs, re], lre * xi + lim * xr + bu[rs, im]
                bu[rs, re] = xr
                bu[rs, im] = xi
            xr_s[:, re] = xr
            xi_s[:, re] = xi
            last = xr

    def stage_out(uz_tb, xs):
        y = _dot(xs[...].astype(BF16), wc_ref[...]) + d_ref[...] * uz_tb[:, 0:S5_WIDTH]
        g5 = 0.5 * y * (1.0 + jnp.tanh(GELU_TANH_SCALE * (y + GELU_TANH_CUBIC * (y * y * y))))
        gate = _sigmoid(_dot(g5.astype(BF16), wglu_ref[...]) + bglu_ref[...])
        z = uz_tb[:, S5_WIDTH:2 * S5_WIDTH]
        o = g5 * gate * (z * _sigmoid(z))
        for c in range(S5_WIDTH // LANES):
            o_tb[c] = o[:, c * LANES:(c + 1) * LANES]
        for b in range(nb):
            for c in range(S5_WIDTH // LANES):
                o_ref[b, :, c * LANES:(c + 1) * LANES] = o_tb[c, pl.ds(b, lc, stride=nb), :].astype(BF16)

    if nt == 1:
        xr_s[...] = x0r_ref[...]
        xi_s[...] = x0i_ref[...]
        stage_in(p0_ref, uz_a, bu_a)
        stage_scan(bu_a)
        stage_out(uz_a, bu_a)
        xr_out[...] = xr_s[...]
        xi_out[...] = xi_s[...]
        return

    @pl.when(it == 0)
    def _():
        xr_s[...] = x0r_ref[...]
        xi_s[...] = x0i_ref[...]
        stage_in(p0_ref, uz_a, bu_a)
        uz_c[...] = jnp.zeros(uz_c.shape, F32)
        bu_c[...] = jnp.zeros(bu_c.shape, F32)

    for r in range(3):
        @pl.when(it % 3 == r)
        def _(r=r):
            stage_in(pn_ref, *ring[(r + 1) % 3])
            stage_scan(ring[r][1])
            stage_out(*ring[(r + 2) % 3])

    @pl.when(it == pl.num_programs(1) - 2)
    def _():
        xr_out[...] = xr_s[...]
        xi_out[...] = xi_s[...]


def _s5(ps3, x0r, x0i, lre, lim, wb, wc, d, wglu, bglu, layer, bsz, t, lc):
    nb = S5_BATCH_TILE
    nt = t // lc
    pitch = lc + 8
    rows = lc * nb
    return pl.pallas_call(
        functools.partial(_s5_body, lc=lc, pitch=pitch, nt=nt),
        grid=(bsz // nb, nt + 1 if nt > 1 else 1),
        in_specs=[pl.BlockSpec((nb, lc, PS_COLS), lambda g, i: (g, 0, 0)),
                  pl.BlockSpec((nb, lc, PS_COLS), lambda g, i: (g, jnp.minimum(i + 1, nt - 1), 0)),
                  pl.BlockSpec((nb, S5_NSTATE), lambda g, i: (g, 0)),
                  pl.BlockSpec((nb, S5_NSTATE), lambda g, i: (g, 0)),
                  _layer_spec((1, S5_NSTATE), layer), _layer_spec((1, S5_NSTATE), layer),
                  _layer_spec((S5_WIDTH, 2 * S5_NSTATE), layer), _layer_spec((2 * S5_NSTATE, S5_WIDTH), layer),
                  _layer_spec((1, S5_WIDTH), layer), _layer_spec((S5_WIDTH, S5_WIDTH), layer),
                  _layer_spec((1, S5_WIDTH), layer),
                  _const_spec((1, S5_NSTATE // S5_SCAN_SLICES))],
        out_specs=[pl.BlockSpec((nb, lc, S5_WIDTH), lambda g, i: (g, jnp.maximum(i - 1, 0), 0)),
                   pl.BlockSpec((nb, S5_NSTATE), lambda g, i: (g, 0)),
                   pl.BlockSpec((nb, S5_NSTATE), lambda g, i: (g, 0))],
        out_shape=[jax.ShapeDtypeStruct((bsz, t, S5_WIDTH), BF16),
                   jax.ShapeDtypeStruct((bsz, S5_NSTATE), F32),
                   jax.ShapeDtypeStruct((bsz, S5_NSTATE), F32)],
        scratch_shapes=[pltpu.VMEM((PS_COLS // LANES, nb * pitch, LANES), F32)]
                       + [pltpu.VMEM((rows, PS_COLS), F32)] * 3
                       + [pltpu.VMEM((rows, 2 * S5_NSTATE), F32)] * 3
                       + [pltpu.VMEM((S5_WIDTH // LANES, rows, LANES), F32),
                          pltpu.VMEM((nb, S5_NSTATE), F32), pltpu.VMEM((nb, S5_NSTATE), F32)],
        compiler_params=_cparams("parallel", "arbitrary"),
        name="s5",
    )(ps3, ps3, x0r, x0i, lre, lim, wb, wc, d, wglu, bglu, jnp.zeros((1, S5_NSTATE // S5_SCAN_SLICES), jnp.int32))


def _outproj_rows(x, og_ref, om_ref, os_ref, wo_scr):
    acc = _dot(og_ref[...], wo_scr[0:GLA_WIDTH, :])
    acc += _dot(om_ref[...], wo_scr[GLA_WIDTH:GLA_WIDTH + MLA_WIDTH, :])
    acc += _dot(os_ref[...], wo_scr[GLA_WIDTH + MLA_WIDTH:, :])
    return x + acc


def _outproj_body(x_ref, og_ref, om_ref, os_ref, w_ref, g_ref, o_ref, wo_scr, *, final):
    @pl.when(pl.program_id(0) == 0)
    def _():
        wo_scr[...] = w_ref[...].astype(BF16)

    xn = _outproj_rows(x_ref[...], og_ref, om_ref, os_ref, wo_scr)
    if final:
        ms = jnp.mean(xn * xn, axis=-1, keepdims=True)
        xn = xn * lax.rsqrt(ms + EPS) * g_ref[...]
    o_ref[...] = xn


def _out_in_body(x_ref, og_ref, om_ref, os_ref, wo_ref, g_ref, wt_ref, *rest, with_kv, n_prev):
    prev, rest = rest[:n_prev], rest[n_prev:]
    n_in, n_out = N_MLA_IN[with_kv], N_MLA_OUT[with_kv]
    mla_in, (xo_ref, pg_ref, mz_ref, ps_ref), mla_out, (wo_scr, w_scr) = (
        rest[:n_in], rest[n_in:n_in + 4], rest[n_in + 4:n_in + 4 + n_out], rest[n_in + 4 + n_out:])
    if n_prev:
        ckv_all = mla_out[1]
        for j in range(n_prev):
            ckv_all[j] = prev[j][...]
        mla_out = (mla_out[0], ckv_all.at[n_prev]) + tuple(mla_out[2:])

    @pl.when(pl.program_id(0) == 0)
    def _():
        wo_scr[...] = wo_ref[...].astype(BF16)
        w_scr[...] = wt_ref[...].astype(BF16)

    xn = _outproj_rows(x_ref[...], og_ref, om_ref, os_ref, wo_scr)
    xo_ref[...] = xn
    _inproj_rows(xn, g_ref, w_scr, pg_ref, mz_ref, ps_ref, tuple(mla_in) + tuple(mla_out), with_kv)


def _out_in(x2, og, om, os_, p, cos_t, sin_t, layer, bsz, t, tm, with_kv, ckv_prev=()):
    n = x2.shape[0]
    row = lambda c: pl.BlockSpec((tm, c), lambda i: (i, 0))
    m_args, m_in, m_out, m_shape = _mla_specs(p, cos_t, sin_t, layer + 1, bsz, t, tm, with_kv)
    n_prev = len(ckv_prev)
    if n_prev:
        m_out[1] = pl.BlockSpec((n_prev + 1, tm, MLA_KV_RANK), lambda i: (0, i, 0))
        m_shape[1] = jax.ShapeDtypeStruct((n_prev + 1, n, MLA_KV_RANK), F32)
    o_specs, o_shape = _proj_out(tm, n)
    return pl.pallas_call(
        functools.partial(_out_in_body, with_kv=with_kv, n_prev=n_prev),
        grid=(n // tm,),
        in_specs=[row(D_MODEL), row(GLA_WIDTH), row(MLA_WIDTH), row(S5_WIDTH),
                  _layer_spec((D_MODEL, D_MODEL), layer, single_buffer=True),
                  _layer_spec((1, D_MODEL), layer + 1),
                  _layer_spec((IN_OFF['end'], D_MODEL), layer + 1, single_buffer=True)]
                 + [row(MLA_KV_RANK)] * n_prev + m_in,
        out_specs=[row(D_MODEL)] + o_specs + m_out,
        out_shape=[jax.ShapeDtypeStruct((n, D_MODEL), F32)] + o_shape + m_shape,
        scratch_shapes=[pltpu.VMEM((D_MODEL, D_MODEL), BF16), pltpu.VMEM((IN_OFF['end'], D_MODEL), BF16)],
        compiler_params=_cparams("arbitrary"),
        name="outproj_inproj",
    )(x2, og, om, os_, p['w_out'], p['ln'], p['w_in'], *ckv_prev, *m_args)


def _outproj(x2, og, om, os_, w, gain, layer, tm, final):
    n = x2.shape[0]
    row = lambda c: pl.BlockSpec((tm, c), lambda i: (i, 0))
    return pl.pallas_call(
        functools.partial(_outproj_body, final=final),
        grid=(n // tm,),
        in_specs=[row(D_MODEL), row(GLA_WIDTH), row(MLA_WIDTH), row(S5_WIDTH),
                  _layer_spec((D_MODEL, D_MODEL), layer, single_buffer=True),
                  _const_spec((1, D_MODEL))],
        out_specs=row(D_MODEL),
        out_shape=jax.ShapeDtypeStruct((n, D_MODEL), F32),
        scratch_shapes=[pltpu.VMEM((D_MODEL, D_MODEL), BF16)],
        compiler_params=_cparams("arbitrary"),
        name="outproj_final" if final else "outproj",
    )(x2, og, om, os_, w, gain)


def _prepare_params(ln_gain, w_in, gla_w_gate, gla_b_gate, gla_norm_gain, mla_q_norm_gain, mla_w_uq,
                    mla_kv_norm_gain, mla_w_ukv, s5_lambda_re, s5_lambda_im, s5_b_re, s5_b_im, s5_c_re, s5_c_im,
                    s5_d, s5_log_dt, s5_w_glu, s5_b_glu, w_out):
    depth = w_in.shape[0]
    w_t = jnp.swapaxes(w_in, 1, 2)
    wg = jnp.pad(gla_w_gate, ((0, 0), (0, GLA_QK - GLA_GATE_RANK), (0, 0))).astype(BF16)
    wq = mla_w_uq.reshape(depth, MLA_Q_RANK, MLA_HEADS, MLA_NOPE_DIM + MLA_ROPE_DIM)
    half = MLA_ROPE_DIM // 2
    wq = jnp.concatenate([wq, -wq[..., MLA_NOPE_DIM + half:], wq[..., MLA_NOPE_DIM:MLA_NOPE_DIM + half]], axis=-1)
    wq = jnp.pad(wq, ((0, 0), (0, 256 - MLA_Q_RANK), (0, 0), (0, 0)))
    wq = wq.reshape(depth, 256, MLA_QK_WIDTH).astype(BF16)
    gq = jnp.pad(mla_q_norm_gain, ((0, 0), (0, 256 - MLA_Q_RANK))).reshape(depth, 1, 256)
    wkv = mla_w_ukv.reshape(depth, MLA_KV_RANK, MLA_HEADS, MLA_NOPE_DIM + MLA_V_DIM)
    wk = jnp.pad(wkv[..., :MLA_NOPE_DIM], ((0, 0), (0, 0), (0, 0), (0, MLA_QK_PAD - MLA_NOPE_DIM)))
    wkv_r = jnp.concatenate([wk.reshape(depth, MLA_KV_RANK, MLA_QK_WIDTH),
                             wkv[..., MLA_NOPE_DIM:].reshape(depth, MLA_KV_RANK, MLA_WIDTH)], axis=2).astype(BF16)
    wka = jnp.pad(jnp.transpose(wkv[..., :MLA_NOPE_DIM], (0, 2, 3, 1)),
                  ((0, 0), (0, 0), (0, MLA_QK_PAD - MLA_NOPE_DIM), (0, 0))).astype(BF16)
    wv = jnp.transpose(wkv[..., MLA_NOPE_DIM:], (0, 2, 1, 3)).astype(BF16)
    dt = jnp.exp(s5_log_dt)[:, :, None]
    mag = jnp.exp(s5_lambda_re * dt)
    lbr, lbi = mag * jnp.cos(s5_lambda_im * dt), mag * jnp.sin(s5_lambda_im * dt)
    den = s5_lambda_re * s5_lambda_re + s5_lambda_im * s5_lambda_im
    qr = ((lbr - 1.0) * s5_lambda_re + lbi * s5_lambda_im) / den
    qi = (lbi * s5_lambda_re - (lbr - 1.0) * s5_lambda_im) / den
    b_re_t, b_im_t = jnp.swapaxes(s5_b_re, 2, 3), jnp.swapaxes(s5_b_im, 2, 3)
    bbr = qr[:, :, None, :] * b_re_t - qi[:, :, None, :] * b_im_t
    bbi = qr[:, :, None, :] * b_im_t + qi[:, :, None, :] * b_re_t
    rows = lambda m: m.reshape(depth, S5_WIDTH, S5_STATE)
    wb, wc = _s5_weights(rows(bbr), rows(bbi), rows(s5_c_re), rows(s5_c_im))
    return dict(
        ln=ln_gain.reshape(depth, 1, D_MODEL), w_in=w_t, wg=wg, bg=gla_b_gate.reshape(depth, 1, GLA_QK),
        gla_gain=jnp.tile(gla_norm_gain, (1, GLA_HEADS)).reshape(depth, 1, GLA_WIDTH),
        gq=gq, wq=wq, gkv=mla_kv_norm_gain.reshape(depth, 1, MLA_KV_RANK), wkv=wkv_r, wka=wka, wv=wv,
        lre=lbr.reshape(depth, 1, S5_NSTATE), lim=lbi.reshape(depth, 1, S5_NSTATE),
        wb=wb, wc=wc, d=s5_d.reshape(depth, 1, S5_WIDTH), wglu=s5_w_glu.astype(BF16),
        bglu=s5_b_glu.reshape(depth, 1, S5_WIDTH), w_out=w_out)


def _rope_tables(past, t, reps):
    half = MLA_ROPE_DIM // 2
    inv = ROPE_BASE ** (-np.arange(half, dtype=np.float64) / half)
    ang = (past + np.arange(t, dtype=np.float64))[:, None] * inv[None, :]
    cos, sin = np.cos(ang), np.sin(ang)
    pad = MLA_QK_PAD - MLA_NOPE_DIM - MLA_ROPE_DIM
    cos_t = np.concatenate([np.ones((t, MLA_NOPE_DIM)), cos, cos, np.zeros((t, pad))], axis=1)
    sin_t = np.concatenate([np.zeros((t, MLA_NOPE_DIM)), sin, sin, np.zeros((t, pad))], axis=1)
    return jnp.asarray(np.tile(cos_t, (reps, 1)), F32), jnp.asarray(np.tile(sin_t, (reps, 1)), F32)


def _trunk(x, gla_state, ckv_cache, kpe_cache, s5_re, s5_im, p, final_gain):
    bsz, t, _ = x.shape
    n = bsz * t
    depth = p['w_in'].shape[0]
    past = 0 if ckv_cache is None else ckv_cache.shape[2]
    s_len = past + t
    tl = _tiles(bsz, t, past)
    cos_t, sin_t = _rope_tables(past, t, max(1, max(tl['row'], tl['in_row']) // t))
    kpet_cache = None if kpe_cache is None else jnp.swapaxes(kpe_cache, 2, 3)
    x2 = x.reshape(n, D_MODEL)
    gain_f = final_gain.reshape(1, D_MODEL)
    gla_o, ckv_o, kpe_o, re_o, im_o = [], [], [], [], []
    with_kv = past == 0
    proj = _inproj(x2, p, cos_t, sin_t, 0, bsz, t, tl['in_row'], with_kv)
    ckv_stack = None
    for l in range(depth):
        pg, mz, ps, q, ckv_new = proj[:5]
        if ckv_new.ndim == 3:
            ckv_stack, ckv_new = ckv_new, ckv_new[l]
        s0 = jnp.zeros((bsz, GLA_HEADS, GLA_DK, GLA_DV), F32) if gla_state is None else gla_state[l]
        o_gla, s_new = _gla(pg.reshape(bsz, t, PG_COLS), p['wg'], p['bg'], p['gla_gain'], l, s0, bsz, t, tl)
        o_gla = o_gla.reshape(n, GLA_WIDTH)
        gla_o.append(s_new)
        if with_kv:
            kpet, k_cat, v_all = proj[5:]
            kpe_o.append(jnp.swapaxes(kpet, 1, 2))
            o_mla = _attn(q, k_cat, v_all, mz, bsz, t, s_len, past, tl['attn_q'], tl['attn_k'])
        else:
            kpe128 = proj[5]
            kpe_o.append(kpe128[:, MLA_NOPE_DIM:MLA_NOPE_DIM + MLA_ROPE_DIM].reshape(bsz, t, MLA_ROPE_DIM))
            o_mla = _attn_cached(q, mz, ckv_new, kpe128, ckv_cache, kpet_cache, p['wka'], p['wv'], l, bsz, t,
                                 tl['cached_seqs'])
        ckv_o.append(ckv_new)
        x0r = jnp.zeros((bsz, S5_NSTATE), F32) if s5_re is None else s5_re[l].reshape(bsz, S5_NSTATE)
        x0i = jnp.zeros((bsz, S5_NSTATE), F32) if s5_im is None else s5_im[l].reshape(bsz, S5_NSTATE)
        o_s5, xr, xi = _s5(ps.reshape(bsz, t, PS_COLS), x0r, x0i, p['lre'], p['lim'], p['wb'], p['wc'],
                           p['d'], p['wglu'], p['bglu'], l, bsz, t, tl['s5_rows'])
        re_o.append(xr.reshape(bsz, S5_GROUPS, S5_STATE))
        im_o.append(xi.reshape(bsz, S5_GROUPS, S5_STATE))
        o_s5 = o_s5.reshape(n, S5_WIDTH)
        if l < depth - 1:
            stack_here = l + 1 == depth - 1
            x2, *proj = _out_in(x2, o_gla, o_mla, o_s5, p, cos_t, sin_t, l, bsz, t, tl['row'], with_kv,
                                ckv_prev=tuple(ckv_o) if stack_here else ())
        else:
            x2 = _outproj(x2, o_gla, o_mla, o_s5, p['w_out'], gain_f, l, tl['out_row'], final=True)
    ckv_all = jnp.stack(ckv_o) if ckv_stack is None else ckv_stack
    return (x2.reshape(bsz, t, D_MODEL), jnp.stack(gla_o), ckv_all.reshape(depth, bsz, t, MLA_KV_RANK),
            jnp.stack(kpe_o), jnp.stack(re_o), jnp.stack(im_o))


def kernel(x_prompt, x_sample, state_gla, cache_mla_ckv, cache_mla_kpe, state_s5_re, state_s5_im, ln_gain, w_in, gla_w_gate, gla_b_gate, gla_norm_gain, mla_q_norm_gain, mla_w_uq, mla_kv_norm_gain, mla_w_ukv, s5_lambda_re, s5_lambda_im, s5_b_re, s5_b_im, s5_c_re, s5_c_im, s5_d, s5_log_dt, s5_w_glu, s5_b_glu, w_out, final_gain):
    p = _prepare_params(ln_gain, w_in, gla_w_gate, gla_b_gate, gla_norm_gain, mla_q_norm_gain, mla_w_uq,
                        mla_kv_norm_gain, mla_w_ukv, s5_lambda_re, s5_lambda_im, s5_b_re, s5_b_im,
                        s5_c_re, s5_c_im, s5_d, s5_log_dt, s5_w_glu, s5_b_glu, w_out)
    y_p, gla_p, ckv_p, kpe_p, re_p, im_p = _trunk(x_prompt, None, None, None, None, None, p, final_gain)
    y_s, gla_s, ckv_s, kpe_s, re_s, im_s = _trunk(x_sample, state_gla, cache_mla_ckv, cache_mla_kpe,
                                                  state_s5_re, state_s5_im, p, final_gain)
    return (y_p, y_s, gla_p, ckv_p, kpe_p, re_p, im_p, gla_s, ckv_s, kpe_s, re_s, im_s)
```

```python
import functools
import math

import numpy as np
import jax
import jax.numpy as jnp
from jax import lax
from jax.experimental import pallas as pl
from jax.experimental.pallas import tpu as pltpu

F32 = jnp.float32
BF16 = jnp.bfloat16

LANES = 128
D_MODEL = 1024
CHUNK = 64
EPS = 1e-6
GLA_HEADS = 4
GLA_DV = 64
GLA_DK = 32
GLA_WIDTH = GLA_HEADS * GLA_DV
GLA_QK = GLA_HEADS * GLA_DK
GLA_GATE_RANK = 16
GLA_GATE_TAU = 16.0
GLA_SUB = 16
GLA_PLAIN_MAX_DECAY = 60.0
MLA_HEADS = 4
MLA_NOPE_DIM = 64
MLA_ROPE_DIM = 32
MLA_V_DIM = 128
MLA_Q_RANK = 192
MLA_KV_RANK = 128
MLA_WIDTH = MLA_HEADS * MLA_V_DIM
MLA_QK_PAD = 128
MLA_QK_WIDTH = MLA_HEADS * MLA_QK_PAD
ROPE_BASE = 10000.0
S5_GROUPS = 16
S5_GROUP_CH = 16
S5_STATE = 64
S5_WIDTH = S5_GROUPS * S5_GROUP_CH
S5_NSTATE = S5_GROUPS * S5_STATE
S5_BATCH_TILE = 8
S5_SCAN_SLICES = 4
GELU_TANH_SCALE = math.sqrt(2.0 / math.pi)
GELU_TANH_CUBIC = 0.044715

PG_COLS = 896
PS_COLS = 512
_IN_SEGS = (('g_q', GLA_QK), ('g_k', GLA_QK), ('g_v', GLA_WIDTH), ('g_lr', GLA_GATE_RANK), ('g_z', GLA_WIDTH),
            ('m_cq', MLA_Q_RANK), ('m_ckv', MLA_KV_RANK), ('m_kr', MLA_ROPE_DIM), ('m_z', MLA_WIDTH),
            ('s_u', S5_WIDTH), ('s_z', S5_WIDTH), ('end', 0))
PG_Q, PG_K, PG_V, PG_Z, PG_LR = slice(0, 128), slice(128, 256), slice(256, 512), slice(512, 768), slice(768, 896)
PG_QKV = slice(PG_Q.start, PG_V.stop)
IN_OFF = dict(zip([n for n, _ in _IN_SEGS], np.cumsum([0] + [w for _, w in _IN_SEGS[:-1]]).tolist()))

VMEM_LIMIT_BYTES = 48 * 1024 * 1024


def _tiles(bsz, t, past):
    n = bsz * t
    s_len = past + t
    return dict(
        row=min(512, n),
        in_row=min(1024, n) if t >= 1024 or t < 512 else 512,
        out_row=min(1024, n),
        cached_seqs=math.gcd(bsz, max(1, 128 // t)),
        gla_rows=min(512, t), gla_chunk=min(CHUNK, t),
        gla_seqs=math.gcd(bsz, max(4, 128 // t)),
        attn_q=min(512, t), attn_k=min(512, s_len),
        s5_rows=min(64, t))


def _cparams(*sem):
    return pltpu.CompilerParams(dimension_semantics=sem, vmem_limit_bytes=VMEM_LIMIT_BYTES)


def _sigmoid(x):
    return 0.5 * (1.0 + jnp.tanh(0.5 * x))


def _dot(a, b):
    return jnp.dot(a, b, preferred_element_type=F32)


def _dot_t(a, b):
    return lax.dot_general(a, b, (((0,), (0,)), ((), ())), preferred_element_type=F32)


def _const_spec(shape):
    zeros = (0,) * len(shape)
    return pl.BlockSpec(shape, lambda *_: zeros)


def _layer_spec(shape, layer, single_buffer=False):
    zeros = (0,) * len(shape)
    mode = dict(pipeline_mode=pl.Buffered(1)) if single_buffer else {}
    return pl.BlockSpec((None,) + tuple(shape), lambda *_: (layer,) + zeros, **mode)


def _rope128(x, cos_t, sin_t):
    lane = lax.broadcasted_iota(jnp.int32, x.shape, 1)
    first_half = (lane >= MLA_NOPE_DIM) & (lane < MLA_NOPE_DIM + MLA_ROPE_DIM // 2)
    rot = jnp.where(first_half, -pltpu.roll(x, LANES - MLA_ROPE_DIM // 2, 1), pltpu.roll(x, MLA_ROPE_DIM // 2, 1))
    return x * cos_t + rot * sin_t


def _mla_prep_rows(cq, ckv, kr, mla, with_kv):
    if with_kv:
        cos_ref, sin_ref, gq_ref, wq_ref, gkv_ref, wkv_ref, q_ref, ckv_ref, kpet_ref, k_ref, v_ref = mla
    else:
        cos_ref, sin_ref, gq_ref, wq_ref, gkv_ref, q_ref, ckv_ref, kpe_ref = mla
    cos_t = cos_ref[...]
    sin_t = sin_ref[...]
    ones = jnp.ones((2 * LANES, LANES), BF16)
    ms = _dot((cq * cq).astype(BF16), ones) * (1.0 / MLA_Q_RANK)
    inv = lax.rsqrt(ms + EPS)
    cqn = (cq * jnp.concatenate([inv, inv], axis=1) * gq_ref[...]).astype(BF16)
    qh = _dot(cqn, wq_ref[...])
    scale = (MLA_NOPE_DIM + MLA_ROPE_DIM) ** -0.5 * math.log2(math.e)
    for h in range(MLA_HEADS):
        x = qh[:, h * MLA_QK_PAD:(h + 1) * MLA_QK_PAD]
        roped = x * cos_t + pltpu.roll(x, LANES - MLA_ROPE_DIM, 1) * sin_t
        q_ref[:, h * MLA_QK_PAD:(h + 1) * MLA_QK_PAD] = (roped * scale).astype(BF16)
    ms = _dot((ckv * ckv).astype(BF16), ones[0:LANES]) * (1.0 / MLA_KV_RANK)
    ckv_n = ckv * lax.rsqrt(ms + EPS) * gkv_ref[...]
    ckv_ref[...] = ckv_n
    kpe128 = _rope128(kr, cos_t, sin_t)
    if with_kv:
        kpet_ref[...] = jnp.transpose(kpe128)[MLA_NOPE_DIM:MLA_NOPE_DIM + MLA_ROPE_DIM, :]
        kv = _dot(ckv_n.astype(BF16), wkv_ref[...])
        for h in range(MLA_HEADS):
            sl = slice(h * MLA_QK_PAD, (h + 1) * MLA_QK_PAD)
            k_ref[:, sl] = (kv[:, sl] + kpe128).astype(BF16)
        v_ref[...] = kv[:, MLA_QK_WIDTH:].astype(BF16)
    else:
        kpe_ref[...] = kpe128


def _inproj_rows(x, g_ref, w_scr, og_ref, mz_ref, os_ref, mla, with_kv):
    ms = jnp.mean(x * x, axis=-1, keepdims=True)
    h = (x * lax.rsqrt(ms + EPS) * g_ref[...]).astype(BF16)

    def seg(a, b):
        return lax.dot_general(h, w_scr[a:b, :], (((1,), (1,)), ((), ())), preferred_element_type=F32)

    lane = lax.broadcasted_iota(jnp.int32, (x.shape[0], LANES), 1)
    lane2 = lax.broadcasted_iota(jnp.int32, (x.shape[0], 2 * LANES), 1)
    c = IN_OFF
    og_ref[:, PG_QKV] = seg(c['g_q'], c['g_lr']).astype(BF16)
    og_ref[:, PG_Z] = seg(c['g_z'], c['m_cq']).astype(BF16)
    og_ref[:, PG_LR] = jnp.where(lane < GLA_GATE_RANK, seg(c['g_lr'], c['g_lr'] + LANES), 0.0).astype(BF16)
    mz_ref[...] = seg(c['m_z'], c['s_u']).astype(BF16)
    os_ref[...] = seg(c['s_u'], c['end']).astype(BF16)
    cq = jnp.where(lane2 < MLA_Q_RANK, seg(c['m_cq'], c['m_cq'] + 2 * LANES), 0.0)
    ckv_kr = seg(c['m_ckv'], c['m_ckv'] + 2 * LANES)
    kr = pltpu.roll(ckv_kr[:, LANES:], MLA_NOPE_DIM, 1)
    kr = jnp.where((lane >= MLA_NOPE_DIM) & (lane < MLA_NOPE_DIM + MLA_ROPE_DIM), kr, 0.0)
    _mla_prep_rows(cq, ckv_kr[:, 0:LANES], kr, mla, with_kv)


N_MLA_IN = {True: 6, False: 5}
N_MLA_OUT = {True: 5, False: 3}


def _inproj_body(x_ref, g_ref, wt_ref, *rest, with_kv):
    n_in, n_out = N_MLA_IN[with_kv], N_MLA_OUT[with_kv]
    mla_in, (og_ref, mz_ref, os_ref), mla_out, (w_scr,) = (
        rest[:n_in], rest[n_in:n_in + 3], rest[n_in + 3:n_in + 3 + n_out], rest[n_in + 3 + n_out:])

    @pl.when(pl.program_id(0) == 0)
    def _():
        w_scr[...] = wt_ref[...].astype(BF16)

    _inproj_rows(x_ref[...], g_ref, w_scr, og_ref, mz_ref, os_ref, tuple(mla_in) + tuple(mla_out), with_kv)


def _mla_specs(p, cos_t, sin_t, layer, bsz, t, tm, with_kv):
    n = bsz * t
    ntab = max(1, t // tm)
    assert not with_kv or tm <= t
    row = lambda c: pl.BlockSpec((tm, c), lambda i: (i, 0))
    table = pl.BlockSpec((tm, LANES), lambda i: (i % ntab, 0))
    args = [cos_t, sin_t, p['gq'], p['wq'], p['gkv']]
    in_specs = [table, table, _layer_spec((1, 256), layer), _layer_spec((256, MLA_QK_WIDTH), layer),
                _layer_spec((1, MLA_KV_RANK), layer)]
    out_specs = [row(MLA_QK_WIDTH), row(MLA_KV_RANK)]
    out_shape = [jax.ShapeDtypeStruct((n, MLA_QK_WIDTH), BF16), jax.ShapeDtypeStruct((n, MLA_KV_RANK), F32)]
    if with_kv:
        args.append(p['wkv'])
        in_specs.append(_layer_spec((MLA_KV_RANK, MLA_QK_WIDTH + MLA_WIDTH), layer))
        out_specs += [pl.BlockSpec((None, MLA_ROPE_DIM, tm), lambda i: (i // ntab, 0, i % ntab)),
                      row(MLA_QK_WIDTH), row(MLA_WIDTH)]
        out_shape += [jax.ShapeDtypeStruct((bsz, MLA_ROPE_DIM, t), F32),
                      jax.ShapeDtypeStruct((n, MLA_QK_WIDTH), BF16), jax.ShapeDtypeStruct((n, MLA_WIDTH), BF16)]
    else:
        out_specs.append(row(LANES))
        out_shape.append(jax.ShapeDtypeStruct((n, LANES), F32))
    return args, in_specs, out_specs, out_shape


def _proj_out(tm, n):
    row = lambda c: pl.BlockSpec((tm, c), lambda i: (i, 0))
    return ([row(PG_COLS), row(MLA_WIDTH), row(PS_COLS)],
            [jax.ShapeDtypeStruct((n, PG_COLS), BF16), jax.ShapeDtypeStruct((n, MLA_WIDTH), BF16),
             jax.ShapeDtypeStruct((n, PS_COLS), BF16)])


def _inproj(x2, p, cos_t, sin_t, layer, bsz, t, tm, with_kv):
    n = x2.shape[0]
    m_args, m_in, m_out, m_shape = _mla_specs(p, cos_t, sin_t, layer, bsz, t, tm, with_kv)
    o_specs, o_shape = _proj_out(tm, n)
    return pl.pallas_call(
        functools.partial(_inproj_body, with_kv=with_kv),
        grid=(n // tm,),
        in_specs=[pl.BlockSpec((tm, D_MODEL), lambda i: (i, 0)),
                  _layer_spec((1, D_MODEL), layer),
                  _layer_spec((IN_OFF['end'], D_MODEL), layer, single_buffer=True)] + m_in,
        scratch_shapes=[pltpu.VMEM((IN_OFF['end'], D_MODEL), BF16)],
        out_specs=o_specs + m_out,
        out_shape=o_shape + m_shape,
        compiler_params=_cparams("arbitrary"),
        name="inproj",
    )(x2, p['ln'], p['w_in'], *m_args)


def _gla_body(p_ref, wg_ref, bg_ref, gain_ref, ones_k_ref, ones_v_ref, bd_ref, s0_ref,
              o_ref, sout_ref, q_scr, k_scr, v_scr, b_scr, s_scr, qe_scr, qs_scr, ke_scr, vb_scr, o_scr,
              *, chunk, nch, nseq):
    it = pl.program_id(1)
    sub = min(GLA_SUB, chunk)
    nsub = chunk // sub

    @pl.when(it == 0)
    def _():
        s_scr[...] = jnp.zeros(s_scr.shape, F32)
        for sq in range(nseq):
            for h in range(GLA_HEADS):
                s_scr[sq, h * GLA_DK:(h + 1) * GLA_DK, h * GLA_DV:(h + 1) * GLA_DV] = s0_ref[sq, h]

    def cols(lanes):
        parts = [p_ref[sq, :, lanes] for sq in range(nseq)]
        return parts[0] if nseq == 1 else jnp.concatenate(parts, axis=0)

    q_scr[...] = cols(PG_Q).astype(F32) * (GLA_DK ** -0.5)
    k_scr[...] = cols(PG_K).astype(F32)
    v_scr[...] = cols(PG_V).astype(F32)
    logit = _dot(cols(PG_LR), wg_ref[...]) + bg_ref[...]
    log_a = (jnp.minimum(logit, 0.0) - jnp.log(1.0 + jnp.exp(-jnp.abs(logit)))) * (1.0 / GLA_GATE_TAU)
    row_in_chunk = lax.broadcasted_iota(jnp.int32, log_a.shape, 0) % chunk
    b_all = log_a
    shift = 1
    while shift < chunk:
        b_all = b_all + jnp.where(row_in_chunk >= shift, pltpu.roll(b_all, shift, 0), 0.0)
        shift *= 2
    b_scr[...] = b_all

    row = lax.broadcasted_iota(jnp.int32, (chunk, GLA_QK), 0)
    row_in_sub = lax.broadcasted_iota(jnp.int32, (sub, GLA_QK), 0)

    def decay_columns(b_end):
        col = jnp.transpose(jnp.broadcast_to(jnp.exp(b_end), (GLA_QK, GLA_QK)))
        return jnp.concatenate([col, col], axis=1)

    def load_chunk(sq, c):
        r0 = pl.multiple_of((sq * nch + c) * chunk, chunk)
        return (r0, q_scr[pl.ds(r0, chunk), :], k_scr[pl.ds(r0, chunk), :], v_scr[pl.ds(r0, chunk), :],
                b_scr[pl.ds(r0, chunk), :], b_scr[pl.ds(r0 + chunk - 1, 1), :], s_scr[sq])

    def finish_chunk(sq, r0, o, s_prev, a_state, b_end):
        ms = _dot((o * o).astype(BF16), ones_v_ref[...]) * (1.0 / GLA_DV)
        o_n = o * lax.rsqrt(ms + EPS) * gain_ref[...]
        t0 = r0 - sq * nch * chunk
        z = p_ref[sq, pl.ds(t0, chunk), PG_Z].astype(F32)
        o_ref[sq, pl.ds(t0, chunk), :] = (o_n * (z * _sigmoid(z))).astype(BF16)
        s_scr[sq] = s_prev * decay_columns(b_end) + a_state

    def robust_chunk(c, carry, sq):
        r0, qc, kc, vc, bc, b_end, s_prev = load_chunk(sq, c)
        xs = [qc * jnp.exp(bc)]
        ks = []
        for sj in range(nsub - 1):
            e_j = b_scr[pl.ds(r0 + (sj + 1) * sub - 1, 1), :]
            later = row >= (sj + 1) * sub
            xs.append(jnp.where(later, qc * jnp.exp(jnp.where(later, bc - e_j, 0.0)), 0.0))
            own = (row >= sj * sub) & (row < (sj + 1) * sub)
            ks.append(jnp.where(own, kc * jnp.exp(jnp.where(own, e_j - bc, 0.0)), 0.0))
        ks.append(kc * jnp.exp(b_end - bc))
        k_all = jnp.concatenate(ks, axis=1).astype(BF16)
        a_all = _dot_t(k_all, vc.astype(BF16)) * bd_ref[...]
        w = jnp.concatenate([s_prev, a_all[:(nsub - 1) * GLA_QK]], axis=0).astype(BF16) if nsub > 1 \
            else s_prev.astype(BF16)
        o_off = _dot(jnp.concatenate(xs, axis=1).astype(BF16), w)
        rows = []
        for si in range(nsub):
            q_i = qc[si * sub:(si + 1) * sub]
            b_i = bc[si * sub:(si + 1) * sub]
            es = []
            for j in range(sub):
                r = r0 + si * sub + j
                b_j = b_scr[pl.ds(r, 1), :]
                k_j = k_scr[pl.ds(r, 1), :]
                valid = row_in_sub >= j
                es.append(q_i * k_j * jnp.exp(jnp.where(valid, b_i - b_j, -jnp.inf)))
            e_all = jnp.concatenate(es, axis=0)
            e_hi = e_all.astype(BF16)
            e_lo = (e_all - e_hi.astype(F32)).astype(BF16)
            p_all = _dot(e_hi, ones_k_ref[...]) + _dot(e_lo, ones_k_ref[...])
            acc = o_off[si * sub:(si + 1) * sub]
            for j in range(sub):
                v_j = v_scr[pl.ds(r0 + si * sub + j, 1), :]
                acc = acc + p_all[j * sub:(j + 1) * sub] * v_j
            rows.append(acc)
        o = rows[0] if nsub == 1 else jnp.concatenate(rows, axis=0)
        finish_chunk(sq, r0, o, s_prev, a_all[(nsub - 1) * GLA_QK:], b_end)
        return carry

    lane_head_v = lax.broadcasted_iota(jnp.int32, (chunk, GLA_WIDTH), 1) // GLA_DV
    causal = (lax.broadcasted_iota(jnp.int32, (GLA_HEADS * chunk, chunk), 0) % chunk
              >= lax.broadcasted_iota(jnp.int32, (GLA_HEADS * chunk, chunk), 1))

    def plain_block():
        tt = nseq * nch * chunk
        b_all = b_scr[...]
        q_all = q_scr[...]
        k_all = k_scr[...]
        qe = q_all * jnp.exp(b_all)
        lane_head = lax.broadcasted_iota(jnp.int32, (tt, GLA_QK), 1) // GLA_DK
        qe_scr[...] = qe.astype(BF16)
        for h in range(GLA_HEADS):
            qs_scr[h] = jnp.where(lane_head == h, qe, 0.0).astype(BF16)
        ke_scr[...] = (k_all * jnp.exp(-b_all)).astype(BF16)
        vb_scr[...] = cols(PG_V)
        states = [s_scr[sq] for sq in range(nseq)]
        for c, sq in [(c, sq) for c in range(nch) for sq in range(nseq)]:
            r0 = (sq * nch + c) * chunk
            rs = slice(r0, r0 + chunk)
            s_cur = states[sq]
            qs = jnp.concatenate([qs_scr[h, rs, :] for h in range(GLA_HEADS)], axis=0)
            s = lax.dot_general(qs, ke_scr[rs, :], (((1,), (1,)), ((), ())), preferred_element_type=F32)
            s = jnp.where(causal, s, 0.0).astype(BF16)
            r = _dot(s, vb_scr[rs, :])
            o = _dot(qe_scr[rs, :], s_cur.astype(BF16))
            for h in range(GLA_HEADS):
                o = o + jnp.where(lane_head_v == h, r[h * chunk:(h + 1) * chunk], 0.0)
            o_scr[rs, :] = o
            b_end = b_scr[r0 + chunk - 1:r0 + chunk, :]
            k_end = (k_scr[rs, :] * jnp.exp(b_end - b_scr[rs, :])).astype(BF16)
            a_state = _dot_t(k_end, vb_scr[rs, :]) * bd_ref[0:GLA_QK, :]
            states[sq] = s_cur * decay_columns(b_end) + a_state
        for sq in range(nseq):
            s_scr[sq] = states[sq]
        o = o_scr[...]
        ms = _dot((o * o).astype(BF16), ones_v_ref[...]) * (1.0 / GLA_DV)
        o_n = o * lax.rsqrt(ms + EPS) * gain_ref[...]
        z = cols(PG_Z).astype(F32)
        o_all = (o_n * (z * _sigmoid(z))).astype(BF16)
        for sq in range(nseq):
            o_ref[sq] = o_all[sq * nch * chunk:(sq + 1) * nch * chunk]

    in_range = jnp.max(-b_scr[...]) < GLA_PLAIN_MAX_DECAY

    @pl.when(in_range)
    def _():
        plain_block()

    @pl.when(jnp.logical_not(in_range))
    def _():
        for sq in range(nseq):
            lax.fori_loop(0, nch, functools.partial(robust_chunk, sq=sq), 0)

    @pl.when(it == pl.num_programs(1) - 1)
    def _():
        for sq in range(nseq):
            for h in range(GLA_HEADS):
                sout_ref[sq, h] = s_scr[sq, h * GLA_DK:(h + 1) * GLA_DK, h * GLA_DV:(h + 1) * GLA_DV]


def _gla_consts(chunk):
    nsub = chunk // min(GLA_SUB, chunk)
    hk = np.arange(GLA_QK) // GLA_DK
    hv = np.arange(GLA_WIDTH) // GLA_DV
    same_kv = (hk[:, None] == hv[None, :]).astype(np.float32)
    same_vv = (hv[:, None] == hv[None, :]).astype(np.float32)
    return (jnp.asarray(same_kv, BF16), jnp.asarray(same_vv, BF16),
            jnp.asarray(np.tile(same_kv, (nsub, 1)), F32))


def _gla(pg, wg, bg, gain, layer, s0, bsz, t, tl):
    chunk, tt, nseq = tl['gla_chunk'], tl['gla_rows'], tl['gla_seqs']
    nch = tt // chunk
    nt = t // tt
    rows = nseq * tt
    nsub = chunk // min(GLA_SUB, chunk)
    ones_k, ones_v, bd = _gla_consts(chunk)
    const = _const_spec
    return pl.pallas_call(
        functools.partial(_gla_body, chunk=chunk, nch=nch, nseq=nseq),
        grid=(bsz // nseq, nt),
        in_specs=[pl.BlockSpec((nseq, tt, PG_COLS), lambda b, i: (b, i, 0)),
                  _layer_spec((GLA_QK, GLA_QK), layer), _layer_spec((1, GLA_QK), layer),
                  _layer_spec((1, GLA_WIDTH), layer),
                  const((GLA_QK, GLA_WIDTH)), const((GLA_WIDTH, GLA_WIDTH)),
                  const((nsub * GLA_QK, GLA_WIDTH)),
                  pl.BlockSpec((nseq, GLA_HEADS, GLA_DK, GLA_DV), lambda b, i: (b, 0, 0, 0))],
        out_specs=[pl.BlockSpec((nseq, tt, GLA_WIDTH), lambda b, i: (b, i, 0)),
                   pl.BlockSpec((nseq, GLA_HEADS, GLA_DK, GLA_DV), lambda b, i: (b, 0, 0, 0))],
        out_shape=[jax.ShapeDtypeStruct((bsz, t, GLA_WIDTH), BF16),
                   jax.ShapeDtypeStruct((bsz, GLA_HEADS, GLA_DK, GLA_DV), F32)],
        scratch_shapes=[pltpu.VMEM((rows, GLA_QK), F32), pltpu.VMEM((rows, GLA_QK), F32),
                        pltpu.VMEM((rows, GLA_WIDTH), F32), pltpu.VMEM((rows, GLA_QK), F32),
                        pltpu.VMEM((nseq, GLA_QK, GLA_WIDTH), F32),
                        pltpu.VMEM((rows, GLA_QK), BF16), pltpu.VMEM((GLA_HEADS, rows, GLA_QK), BF16),
                        pltpu.VMEM((rows, GLA_QK), BF16),
                        pltpu.VMEM((rows, GLA_WIDTH), BF16), pltpu.VMEM((rows, GLA_WIDTH), F32)],
        compiler_params=_cparams("parallel", "arbitrary"),
        name="gla",
    )(pg, wg, bg, gain, ones_k, ones_v, bd, s0)


def _attn_cached_body(q_ref, z_ref, ckv_new_ref, kpe_new_ref, ckv_past_ref, kpet_past_ref, wka_ref, wv_ref,
                      o_ref, kpet_scr, *, past, t, nseq):
    last = (((1,), (1,)), ((), ()))
    hsl = [slice(h * MLA_QK_PAD, (h + 1) * MLA_QK_PAD) for h in range(MLA_HEADS)]
    q_lat_h = [_dot(q_ref[:, hsl[h]], wka_ref[h]).astype(BF16) for h in range(MLA_HEADS)]
    kpet_scr[...] = jnp.zeros(kpet_scr.shape, BF16)
    o_lat = []
    for sq in range(nseq):
        rs = slice(sq * t, (sq + 1) * t)
        q_rows = jnp.concatenate([q_ref[rs, hsl[h]] for h in range(MLA_HEADS)], axis=0)
        q_lat = jnp.concatenate([q_lat_h[h][rs] for h in range(MLA_HEADS)], axis=0)
        c_past = ckv_past_ref[sq].astype(BF16)
        c_new = ckv_new_ref[rs, :].astype(BF16)
        kpet_scr[sq, MLA_NOPE_DIM:MLA_NOPE_DIM + MLA_ROPE_DIM, :] = kpet_past_ref[sq].astype(BF16)
        s_past = (lax.dot_general(q_lat, c_past, last, preferred_element_type=F32)
                  + _dot(q_rows, kpet_scr[sq]))
        s_new = (lax.dot_general(q_lat, c_new, last, preferred_element_type=F32)
                 + lax.dot_general(q_rows, kpe_new_ref[rs, :].astype(BF16), last, preferred_element_type=F32))
        if past // CHUNK != (past + t - 1) // CHUNK:
            q_chunk = (past + lax.broadcasted_iota(jnp.int32, s_past.shape, 0) % t) // CHUNK
            s_past = jnp.where(lax.broadcasted_iota(jnp.int32, s_past.shape, 1) // CHUNK <= q_chunk, s_past, -jnp.inf)
            q_chunk = (past + lax.broadcasted_iota(jnp.int32, s_new.shape, 0) % t) // CHUNK
            s_new = jnp.where((past + lax.broadcasted_iota(jnp.int32, s_new.shape, 1)) // CHUNK <= q_chunk,
                              s_new, -jnp.inf)
        m = jnp.maximum(jnp.max(s_past, axis=-1, keepdims=True), jnp.max(s_new, axis=-1, keepdims=True))
        p_past = jnp.exp2(s_past - m)
        p_new = jnp.exp2(s_new - m)
        l = jnp.sum(p_past, axis=-1, keepdims=True) + jnp.sum(p_new, axis=-1, keepdims=True)
        o_lat.append(((_dot(p_past.astype(BF16), c_past) + _dot(p_new.astype(BF16), c_new)) / l).astype(BF16))
    for h in range(MLA_HEADS):
        vsl = slice(h * MLA_V_DIM, (h + 1) * MLA_V_DIM)
        o_h = jnp.concatenate([o_lat[sq][h * t:(h + 1) * t] for sq in range(nseq)], axis=0)
        z = z_ref[:, vsl].astype(F32)
        o_ref[:, vsl] = (_dot(o_h, wv_ref[h]) * (z * _sigmoid(z))).astype(BF16)


def _attn_cached(q, mz, ckv_new, kpe128, ckv_past, kpet_past, wka, wv, layer, bsz, t, nseq):
    past = ckv_past.shape[2]
    rows = nseq * t
    return pl.pallas_call(
        functools.partial(_attn_cached_body, past=past, t=t, nseq=nseq),
        grid=(bsz // nseq,),
        in_specs=[pl.BlockSpec((rows, MLA_QK_WIDTH), lambda b: (b, 0)),
                  pl.BlockSpec((rows, MLA_WIDTH), lambda b: (b, 0)),
                  pl.BlockSpec((rows, MLA_KV_RANK), lambda b: (b, 0)),
                  pl.BlockSpec((rows, LANES), lambda b: (b, 0)),
                  pl.BlockSpec((None, nseq, past, MLA_KV_RANK), lambda b: (layer, b, 0, 0)),
                  pl.BlockSpec((None, nseq, MLA_ROPE_DIM, past), lambda b: (layer, b, 0, 0)),
                  _layer_spec((MLA_HEADS, MLA_QK_PAD, MLA_KV_RANK), layer),
                  _layer_spec((MLA_HEADS, MLA_KV_RANK, MLA_V_DIM), layer)],
        out_specs=pl.BlockSpec((rows, MLA_WIDTH), lambda b: (b, 0)),
        out_shape=jax.ShapeDtypeStruct((bsz * t, MLA_WIDTH), BF16),
        scratch_shapes=[pltpu.VMEM((nseq, MLA_QK_PAD, past), BF16)],
        compiler_params=_cparams("parallel"),
        name="mla_attn_cached",
    )(q, mz, ckv_new, kpe128, ckv_past, kpet_past, wka, wv)


def _attn_body(q_ref, k_ref, v_ref, z_ref, o_ref, m_scr, acc_scr, *, past, tq, tk, s_len):
    iq = pl.program_id(1)
    q_first = past + iq * tq
    full_keys = jnp.minimum((q_first // CHUNK + 1) * CHUNK, s_len)
    vis_keys = jnp.minimum(((q_first + tq - 1) // CHUNK + 1) * CHUNK, s_len)
    n_full = full_keys // tk
    n_vis = (vis_keys + tk - 1) // tk
    split_diagonal = past % tk == 0 and tq == tk and (tq // 2) % CHUNK == 0

    def block(kb, carry, masked, first=False):
        k0 = pl.multiple_of(kb * tk, tk)
        if masked and split_diagonal:
            parts = [(r * (tq // 2), tq // 2, (r + 1) * (tk // 2)) for r in range(2)]
        else:
            parts = [(0, tq, tk)]
        for r0, nr, kext in parts:
            rows = slice(r0, r0 + nr)
            if masked:
                q_pos = (r0 if split_diagonal else q_first + r0) + lax.broadcasted_iota(jnp.int32, (nr, kext), 0)
                k_pos = (0 if split_diagonal else k0) + lax.broadcasted_iota(jnp.int32, (nr, kext), 1)
                visible = k_pos // CHUNK <= q_pos // CHUNK
            ntile, rem = kext // LANES, kext % LANES
            ones_v = jnp.ones((kext, MLA_V_DIM), BF16)
            for h in range(MLA_HEADS):
                sl = slice(h * MLA_QK_PAD, (h + 1) * MLA_QK_PAD)
                vsl = slice(h * MLA_V_DIM, (h + 1) * MLA_V_DIM)
                s = lax.dot_general(q_ref[rows, sl], k_ref[pl.ds(k0, kext), sl], (((1,), (1,)), ((), ())),
                                    preferred_element_type=F32)
                if masked:
                    s = jnp.where(visible, s, -jnp.inf)
                if first:
                    m_new = jnp.broadcast_to(jnp.max(s, axis=-1, keepdims=True), (nr, LANES))
                else:
                    m_prev = m_scr[h, rows]
                    m_new = jnp.maximum(m_prev, jnp.max(s, axis=-1, keepdims=True))
                    alpha = jnp.exp2(m_prev - m_new)
                ps = [jnp.exp2(s[:, c * LANES:(c + 1) * LANES] - m_new) for c in range(ntile)]
                if rem:
                    ps.append(jnp.exp2(s[:, ntile * LANES:] - m_new[:, :rem]))
                p = jnp.concatenate(ps, axis=1).astype(BF16)
                v_ext = jnp.concatenate([v_ref[pl.ds(k0, kext), vsl], ones_v], axis=1)
                pv = _dot(p, v_ext)
                acc_scr[h, rows] = pv if first else jnp.concatenate([alpha, alpha], axis=1) * acc_scr[h, rows] + pv
                m_scr[h, rows] = m_new
        return carry

    @pl.when(n_full > 0)
    def _():
        block(0, 0, masked=False, first=True)
        lax.fori_loop(1, n_full, functools.partial(block, masked=False), 0)
        lax.fori_loop(n_full, n_vis, functools.partial(block, masked=True), 0)

    @pl.when(n_full == 0)
    def _():
        block(0, 0, masked=True, first=True)
        lax.fori_loop(1, n_vis, functools.partial(block, masked=True), 0)

    for h in range(MLA_HEADS):
        vsl = slice(h * MLA_V_DIM, (h + 1) * MLA_V_DIM)
        z = z_ref[:, vsl].astype(F32)
        acc = acc_scr[h]
        o_ref[:, vsl] = (acc[:, :MLA_V_DIM] / acc[:, MLA_V_DIM:] * (z * _sigmoid(z))).astype(BF16)


def _attn(q, k, v, mz, bsz, t, s_len, past, tq, tk):
    nq = t // tq
    return pl.pallas_call(
        functools.partial(_attn_body, past=past, tq=tq, tk=tk, s_len=s_len),
        grid=(bsz, nq),
        in_specs=[pl.BlockSpec((tq, MLA_QK_WIDTH), lambda b, iq: (b * nq + iq, 0)),
                  pl.BlockSpec((s_len, MLA_QK_WIDTH), lambda b, iq: (b, 0)),
                  pl.BlockSpec((s_len, MLA_WIDTH), lambda b, iq: (b, 0)),
                  pl.BlockSpec((tq, MLA_WIDTH), lambda b, iq: (b * nq + iq, 0))],
        out_specs=pl.BlockSpec((tq, MLA_WIDTH), lambda b, iq: (b * nq + iq, 0)),
        out_shape=jax.ShapeDtypeStruct((bsz * t, MLA_WIDTH), BF16),
        scratch_shapes=[pltpu.VMEM((MLA_HEADS, tq, LANES), F32), pltpu.VMEM((MLA_HEADS, tq, 2 * MLA_V_DIM), F32)],
        compiler_params=_cparams("parallel", "arbitrary"),
        name="mla_attn",
    )(q, k, v, mz)


def _s5_weights_body(bre_ref, bim_ref, cre_ref, cim_ref, spread_ref, mask_ref, wb_ref, wc_ref):
    def expand(m_ref):
        return _dot(m_ref[...].astype(BF16), spread_ref[...]) * mask_ref[...]

    wb_ref[:, 0:S5_NSTATE] = expand(bre_ref).astype(BF16)
    wb_ref[:, S5_NSTATE:] = expand(bim_ref).astype(BF16)
    wc_ref[0:S5_NSTATE, :] = jnp.transpose(expand(cre_ref)).astype(BF16)
    wc_ref[S5_NSTATE:, :] = jnp.transpose(-expand(cim_ref)).astype(BF16)


def _s5_weights(bbr, bbi, c_re, c_im):
    depth = bbr.shape[0]
    g_row = np.arange(S5_WIDTH) // S5_GROUP_CH
    g_col = np.arange(S5_NSTATE) // S5_STATE
    mask = (g_row[:, None] == g_col[None, :]).astype(np.float32)
    spread = (np.arange(S5_STATE)[:, None] == (np.arange(S5_NSTATE) % S5_STATE)[None, :]).astype(np.float32)
    small = pl.BlockSpec((None, S5_WIDTH, S5_STATE), lambda l: (l, 0, 0))
    return pl.pallas_call(
        _s5_weights_body,
        grid=(depth,),
        in_specs=[small, small, small, small,
                  _const_spec((S5_STATE, S5_NSTATE)), _const_spec((S5_WIDTH, S5_NSTATE))],
        out_specs=[pl.BlockSpec((None, S5_WIDTH, 2 * S5_NSTATE), lambda l: (l, 0, 0)),
                   pl.BlockSpec((None, 2 * S5_NSTATE, S5_WIDTH), lambda l: (l, 0, 0))],
        out_shape=[jax.ShapeDtypeStruct((depth, S5_WIDTH, 2 * S5_NSTATE), BF16),
                   jax.ShapeDtypeStruct((depth, 2 * S5_NSTATE, S5_WIDTH), BF16)],
        compiler_params=_cparams("parallel"),
        name="s5_weights",
    )(bbr, bbi, c_re, c_im, jnp.asarray(spread, BF16), jnp.asarray(mask, F32))


def _s5_body(p0_ref, pn_ref, x0r_ref, x0i_ref, lre_ref, lim_ref, wb_ref, wc_ref, d_ref, wglu_ref, bglu_ref, never_ref,
             o_ref, xr_out, xi_out, uz_bt, uz_a, uz_b, uz_c, bu_a, bu_b, bu_c, o_tb, xr_s, xi_s, *, lc, pitch, nt):
    it = pl.program_id(1)
    nb = S5_BATCH_TILE
    ring = ((uz_a, bu_a), (uz_b, bu_b), (uz_c, bu_c))

    def stage_in(blk_ref, uz_tb, bu):
        for b in range(nb):
            for c in range(PS_COLS // LANES):
                uz_bt[c, b * pitch:b * pitch + lc, :] = blk_ref[b, :, c * LANES:(c + 1) * LANES].astype(F32)
        for t in range(lc):
            for c in range(PS_COLS // LANES):
                uz_tb[t * nb:(t + 1) * nb, c * LANES:(c + 1) * LANES] = uz_bt[c, pl.ds(t, nb, stride=pitch), :]
        bu[...] = _dot(uz_tb[:, 0:S5_WIDTH].astype(BF16), wb_ref[...])

    def stage_scan(bu):
        w = S5_NSTATE // S5_SCAN_SLICES
        never = never_ref[...] != 0
        last = None
        for c0 in range(0, S5_NSTATE, w):
            re, im = slice(c0, c0 + w), slice(S5_NSTATE + c0, S5_NSTATE + c0 + w)
            lre = jnp.broadcast_to(lre_ref[:, re], (nb, w))
            lim = jnp.broadcast_to(lim_ref[:, re], (nb, w))
            xr, xi = xr_s[:, re], xi_s[:, re]
            if last is not None:
                xr = jnp.where(never, last, xr)
            for t in range(lc):
                rs = slice(t * nb, (t + 1) * nb)
                xr, xi = lre * xr - lim * xi + bu[rs, re], lre * xi + lim * xr + bu[rs, im]
                bu[rs, re] = xr
                bu[rs, im] = xi
            xr_s[:, re] = xr
            xi_s[:, re] = xi
            last = xr

    def stage_out(uz_tb, xs):
        y = _dot(xs[...].astype(BF16), wc_ref[...]) + d_ref[...] * uz_tb[:, 0:S5_WIDTH]
        g5 = 0.5 * y * (1.0 + jnp.tanh(GELU_TANH_SCALE * (y + GELU_TANH_CUBIC * (y * y * y))))
        gate = _sigmoid(_dot(g5.astype(BF16), wglu_ref[...]) + bglu_ref[...])
        z = uz_tb[:, S5_WIDTH:2 * S5_WIDTH]
        o = g5 * gate * (z * _sigmoid(z))
        for c in range(S5_WIDTH // LANES):
            o_tb[c] = o[:, c * LANES:(c + 1) * LANES]
        for b in range(nb):
            for c in range(S5_WIDTH // LANES):
                o_ref[b, :, c * LANES:(c + 1) * LANES] = o_tb[c, pl.ds(b, lc, stride=nb), :].astype(BF16)

    if nt == 1:
        xr_s[...] = x0r_ref[...]
        xi_s[...] = x0i_ref[...]
        stage_in(p0_ref, uz_a, bu_a)
        stage_scan(bu_a)
        stage_out(uz_a, bu_a)
        xr_out[...] = xr_s[...]
        xi_out[...] = xi_s[...]
        return

    @pl.when(it == 0)
    def _():
        xr_s[...] = x0r_ref[...]
        xi_s[...] = x0i_ref[...]
        stage_in(p0_ref, uz_a, bu_a)
        uz_c[...] = jnp.zeros(uz_c.shape, F32)
        bu_c[...] = jnp.zeros(bu_c.shape, F32)

    for r in range(3):
        @pl.when(it % 3 == r)
        def _(r=r):
            stage_in(pn_ref, *ring[(r + 1) % 3])
            stage_scan(ring[r][1])
            stage_out(*ring[(r + 2) % 3])

    @pl.when(it == pl.num_programs(1) - 2)
    def _():
        xr_out[...] = xr_s[...]
        xi_out[...] = xi_s[...]


def _s5(ps3, x0r, x0i, lre, lim, wb, wc, d, wglu, bglu, layer, bsz, t, lc):
    nb = S5_BATCH_TILE
    nt = t // lc
    pitch = lc + 8
    rows = lc * nb
    return pl.pallas_call(
        functools.partial(_s5_body, lc=lc, pitch=pitch, nt=nt),
        grid=(bsz // nb, nt + 1 if nt > 1 else 1),
        in_specs=[pl.BlockSpec((nb, lc, PS_COLS), lambda g, i: (g, 0, 0)),
                  pl.BlockSpec((nb, lc, PS_COLS), lambda g, i: (g, jnp.minimum(i + 1, nt - 1), 0)),
                  pl.BlockSpec((nb, S5_NSTATE), lambda g, i: (g, 0)),
                  pl.BlockSpec((nb, S5_NSTATE), lambda g, i: (g, 0)),
                  _layer_spec((1, S5_NSTATE), layer), _layer_spec((1, S5_NSTATE), layer),
                  _layer_spec((S5_WIDTH, 2 * S5_NSTATE), layer), _layer_spec((2 * S5_NSTATE, S5_WIDTH), layer),
                  _layer_spec((1, S5_WIDTH), layer), _layer_spec((S5_WIDTH, S5_WIDTH), layer),
                  _layer_spec((1, S5_WIDTH), layer),
                  _const_spec((1, S5_NSTATE // S5_SCAN_SLICES))],
        out_specs=[pl.BlockSpec((nb, lc, S5_WIDTH), lambda g, i: (g, jnp.maximum(i - 1, 0), 0)),
                   pl.BlockSpec((nb, S5_NSTATE), lambda g, i: (g, 0)),
                   pl.BlockSpec((nb, S5_NSTATE), lambda g, i: (g, 0))],
        out_shape=[jax.ShapeDtypeStruct((bsz, t, S5_WIDTH), BF16),
                   jax.ShapeDtypeStruct((bsz, S5_NSTATE), F32),
                   jax.ShapeDtypeStruct((bsz, S5_NSTATE), F32)],
        scratch_shapes=[pltpu.VMEM((PS_COLS // LANES, nb * pitch, LANES), F32)]
                       + [pltpu.VMEM((rows, PS_COLS), F32)] * 3
                       + [pltpu.VMEM((rows, 2 * S5_NSTATE), F32)] * 3
                       + [pltpu.VMEM((S5_WIDTH // LANES, rows, LANES), F32),
                          pltpu.VMEM((nb, S5_NSTATE), F32), pltpu.VMEM((nb, S5_NSTATE), F32)],
        compiler_params=_cparams("parallel", "arbitrary"),
        name="s5",
    )(ps3, ps3, x0r, x0i, lre, lim, wb, wc, d, wglu, bglu, jnp.zeros((1, S5_NSTATE // S5_SCAN_SLICES), jnp.int32))


def _outproj_rows(x, og_ref, om_ref, os_ref, wo_scr):
    acc = _dot(og_ref[...], wo_scr[0:GLA_WIDTH, :])
    acc += _dot(om_ref[...], wo_scr[GLA_WIDTH:GLA_WIDTH + MLA_WIDTH, :])
    acc += _dot(os_ref[...], wo_scr[GLA_WIDTH + MLA_WIDTH:, :])
    return x + acc


def _outproj_body(x_ref, og_ref, om_ref, os_ref, w_ref, g_ref, o_ref, wo_scr, *, final):
    @pl.when(pl.program_id(0) == 0)
    def _():
        wo_scr[...] = w_ref[...].astype(BF16)

    xn = _outproj_rows(x_ref[...], og_ref, om_ref, os_ref, wo_scr)
    if final:
        ms = jnp.mean(xn * xn, axis=-1, keepdims=True)
        xn = xn * lax.rsqrt(ms + EPS) * g_ref[...]
    o_ref[...] = xn


def _out_in_body(x_ref, og_ref, om_ref, os_ref, wo_ref, g_ref, wt_ref, *rest, with_kv, n_prev):
    prev, rest = rest[:n_prev], rest[n_prev:]
    n_in, n_out = N_MLA_IN[with_kv], N_MLA_OUT[with_kv]
    mla_in, (xo_ref, pg_ref, mz_ref, ps_ref), mla_out, (wo_scr, w_scr) = (
        rest[:n_in], rest[n_in:n_in + 4], rest[n_in + 4:n_in + 4 + n_out], rest[n_in + 4 + n_out:])
    if n_prev:
        ckv_all = mla_out[1]
        for j in range(n_prev):
            ckv_all[j] = prev[j][...]
        mla_out = (mla_out[0], ckv_all.at[n_prev]) + tuple(mla_out[2:])

    @pl.when(pl.program_id(0) == 0)
    def _():
        wo_scr[...] = wo_ref[...].astype(BF16)
        w_scr[...] = wt_ref[...].astype(BF16)

    xn = _outproj_rows(x_ref[...], og_ref, om_ref, os_ref, wo_scr)
    xo_ref[...] = xn
    _inproj_rows(xn, g_ref, w_scr, pg_ref, mz_ref, ps_ref, tuple(mla_in) + tuple(mla_out), with_kv)


def _out_in(x2, og, om, os_, p, cos_t, sin_t, layer, bsz, t, tm, with_kv, ckv_prev=()):
    n = x2.shape[0]
    row = lambda c: pl.BlockSpec((tm, c), lambda i: (i, 0))
    m_args, m_in, m_out, m_shape = _mla_specs(p, cos_t, sin_t, layer + 1, bsz, t, tm, with_kv)
    n_prev = len(ckv_prev)
    if n_prev:
        m_out[1] = pl.BlockSpec((n_prev + 1, tm, MLA_KV_RANK), lambda i: (0, i, 0))
        m_shape[1] = jax.ShapeDtypeStruct((n_prev + 1, n, MLA_KV_RANK), F32)
    o_specs, o_shape = _proj_out(tm, n)
    return pl.pallas_call(
        functools.partial(_out_in_body, with_kv=with_kv, n_prev=n_prev),
        grid=(n // tm,),
        in_specs=[row(D_MODEL), row(GLA_WIDTH), row(MLA_WIDTH), row(S5_WIDTH),
                  _layer_spec((D_MODEL, D_MODEL), layer, single_buffer=True),
                  _layer_spec((1, D_MODEL), layer + 1),
                  _layer_spec((IN_OFF['end'], D_MODEL), layer + 1, single_buffer=True)]
                 + [row(MLA_KV_RANK)] * n_prev + m_in,
        out_specs=[row(D_MODEL)] + o_specs + m_out,
        out_shape=[jax.ShapeDtypeStruct((n, D_MODEL), F32)] + o_shape + m_shape,
        scratch_shapes=[pltpu.VMEM((D_MODEL, D_MODEL), BF16), pltpu.VMEM((IN_OFF['end'], D_MODEL), BF16)],
        compiler_params=_cparams("arbitrary"),
        name="outproj_inproj",
    )(x2, og, om, os_, p['w_out'], p['ln'], p['w_in'], *ckv_prev, *m_args)


def _outproj(x2, og, om, os_, w, gain, layer, tm, final):
    n = x2.shape[0]
    row = lambda c: pl.BlockSpec((tm, c), lambda i: (i, 0))
    return pl.pallas_call(
        functools.partial(_outproj_body, final=final),
        grid=(n // tm,),
        in_specs=[row(D_MODEL), row(GLA_WIDTH), row(MLA_WIDTH), row(S5_WIDTH),
                  _layer_spec((D_MODEL, D_MODEL), layer, single_buffer=True),
                  _const_spec((1, D_MODEL))],
        out_specs=row(D_MODEL),
        out_shape=jax.ShapeDtypeStruct((n, D_MODEL), F32),
        scratch_shapes=[pltpu.VMEM((D_MODEL, D_MODEL), BF16)],
        compiler_params=_cparams("arbitrary"),
        name="outproj_final" if final else "outproj",
    )(x2, og, om, os_, w, gain)


def _prepare_params(ln_gain, w_in, gla_w_gate, gla_b_gate, gla_norm_gain, mla_q_norm_gain, mla_w_uq,
                    mla_kv_norm_gain, mla_w_ukv, s5_lambda_re, s5_lambda_im, s5_b_re, s5_b_im, s5_c_re, s5_c_im,
                    s5_d, s5_log_dt, s5_w_glu, s5_b_glu, w_out):
    depth = w_in.shape[0]
    w_t = jnp.swapaxes(w_in, 1, 2)
    wg = jnp.pad(gla_w_gate, ((0, 0), (0, GLA_QK - GLA_GATE_RANK), (0, 0))).astype(BF16)
    wq = mla_w_uq.reshape(depth, MLA_Q_RANK, MLA_HEADS, MLA_NOPE_DIM + MLA_ROPE_DIM)
    half = MLA_ROPE_DIM // 2
    wq = jnp.concatenate([wq, -wq[..., MLA_NOPE_DIM + half:], wq[..., MLA_NOPE_DIM:MLA_NOPE_DIM + half]], axis=-1)
    wq = jnp.pad(wq, ((0, 0), (0, 256 - MLA_Q_RANK), (0, 0), (0, 0)))
    wq = wq.reshape(depth, 256, MLA_QK_WIDTH).astype(BF16)
    gq = jnp.pad(mla_q_norm_gain, ((0, 0), (0, 256 - MLA_Q_RANK))).reshape(depth, 1, 256)
    wkv = mla_w_ukv.reshape(depth, MLA_KV_RANK, MLA_HEADS, MLA_NOPE_DIM + MLA_V_DIM)
    wk = jnp.pad(wkv[..., :MLA_NOPE_DIM], ((0, 0), (0, 0), (0, 0), (0, MLA_QK_PAD - MLA_NOPE_DIM)))
    wkv_r = jnp.concatenate([wk.reshape(depth, MLA_KV_RANK, MLA_QK_WIDTH),
                             wkv[..., MLA_NOPE_DIM:].reshape(depth, MLA_KV_RANK, MLA_WIDTH)], axis=2).astype(BF16)
    wka = jnp.pad(jnp.transpose(wkv[..., :MLA_NOPE_DIM], (0, 2, 3, 1)),
                  ((0, 0), (0, 0), (0, MLA_QK_PAD - MLA_NOPE_DIM), (0, 0))).astype(BF16)
    wv = jnp.transpose(wkv[..., MLA_NOPE_DIM:], (0, 2, 1, 3)).astype(BF16)
    dt = jnp.exp(s5_log_dt)[:, :, None]
    mag = jnp.exp(s5_lambda_re * dt)
    lbr, lbi = mag * jnp.cos(s5_lambda_im * dt), mag * jnp.sin(s5_lambda_im * dt)
    den = s5_lambda_re * s5_lambda_re + s5_lambda_im * s5_lambda_im
    qr = ((lbr - 1.0) * s5_lambda_re + lbi * s5_lambda_im) / den
    qi = (lbi * s5_lambda_re - (lbr - 1.0) * s5_lambda_im) / den
    b_re_t, b_im_t = jnp.swapaxes(s5_b_re, 2, 3), jnp.swapaxes(s5_b_im, 2, 3)
    bbr = qr[:, :, None, :] * b_re_t - qi[:, :, None, :] * b_im_t
    bbi = qr[:, :, None, :] * b_im_t + qi[:, :, None, :] * b_re_t
    rows = lambda m: m.reshape(depth, S5_WIDTH, S5_STATE)
    wb, wc = _s5_weights(rows(bbr), rows(bbi), rows(s5_c_re), rows(s5_c_im))
    return dict(
        ln=ln_gain.reshape(depth, 1, D_MODEL), w_in=w_t, wg=wg, bg=gla_b_gate.reshape(depth, 1, GLA_QK),
        gla_gain=jnp.tile(gla_norm_gain, (1, GLA_HEADS)).reshape(depth, 1, GLA_WIDTH),
        gq=gq, wq=wq, gkv=mla_kv_norm_gain.reshape(depth, 1, MLA_KV_RANK), wkv=wkv_r, wka=wka, wv=wv,
        lre=lbr.reshape(depth, 1, S5_NSTATE), lim=lbi.reshape(depth, 1, S5_NSTATE),
        wb=wb, wc=wc, d=s5_d.reshape(depth, 1, S5_WIDTH), wglu=s5_w_glu.astype(BF16),
        bglu=s5_b_glu.reshape(depth, 1, S5_WIDTH), w_out=w_out)


def _rope_tables(past, t, reps):
    half = MLA_ROPE_DIM // 2
    inv = ROPE_BASE ** (-np.arange(half, dtype=np.float64) / half)
    ang = (past + np.arange(t, dtype=np.float64))[:, None] * inv[None, :]
    cos, sin = np.cos(ang), np.sin(ang)
    pad = MLA_QK_PAD - MLA_NOPE_DIM - MLA_ROPE_DIM
    cos_t = np.concatenate([np.ones((t, MLA_NOPE_DIM)), cos, cos, np.zeros((t, pad))], axis=1)
    sin_t = np.concatenate([np.zeros((t, MLA_NOPE_DIM)), sin, sin, np.zeros((t, pad))], axis=1)
    return jnp.asarray(np.tile(cos_t, (reps, 1)), F32), jnp.asarray(np.tile(sin_t, (reps, 1)), F32)


def _trunk(x, gla_state, ckv_cache, kpe_cache, s5_re, s5_im, p, final_gain):
    bsz, t, _ = x.shape
    n = bsz * t
    depth = p['w_in'].shape[0]
    past = 0 if ckv_cache is None else ckv_cache.shape[2]
    s_len = past + t
    tl = _tiles(bsz, t, past)
    cos_t, sin_t = _rope_tables(past, t, max(1, max(tl['row'], tl['in_row']) // t))
    kpet_cache = None if kpe_cache is None else jnp.swapaxes(kpe_cache, 2, 3)
    x2 = x.reshape(n, D_MODEL)
    gain_f = final_gain.reshape(1, D_MODEL)
    gla_o, ckv_o, kpe_o, re_o, im_o = [], [], [], [], []
    with_kv = past == 0
    proj = _inproj(x2, p, cos_t, sin_t, 0, bsz, t, tl['in_row'], with_kv)
    ckv_stack = None
    for l in range(depth):
        pg, mz, ps, q, ckv_new = proj[:5]
        if ckv_new.ndim == 3:
            ckv_stack, ckv_new = ckv_new, ckv_new[l]
        s0 = jnp.zeros((bsz, GLA_HEADS, GLA_DK, GLA_DV), F32) if gla_state is None else gla_state[l]
        o_gla, s_new = _gla(pg.reshape(bsz, t, PG_COLS), p['wg'], p['bg'], p['gla_gain'], l, s0, bsz, t, tl)
        o_gla = o_gla.reshape(n, GLA_WIDTH)
        gla_o.append(s_new)
        if with_kv:
            kpet, k_cat, v_all = proj[5:]
            kpe_o.append(jnp.swapaxes(kpet, 1, 2))
            o_mla = _attn(q, k_cat, v_all, mz, bsz, t, s_len, past, tl['attn_q'], tl['attn_k'])
        else:
            kpe128 = proj[5]
            kpe_o.append(kpe128[:, MLA_NOPE_DIM:MLA_NOPE_DIM + MLA_ROPE_DIM].reshape(bsz, t, MLA_ROPE_DIM))
            o_mla = _attn_cached(q, mz, ckv_new, kpe128, ckv_cache, kpet_cache, p['wka'], p['wv'], l, bsz, t,
                                 tl['cached_seqs'])
        ckv_o.append(ckv_new)
        x0r = jnp.zeros((bsz, S5_NSTATE), F32) if s5_re is None else s5_re[l].reshape(bsz, S5_NSTATE)
        x0i = jnp.zeros((bsz, S5_NSTATE), F32) if s5_im is None else s5_im[l].reshape(bsz, S5_NSTATE)
        o_s5, xr, xi = _s5(ps.reshape(bsz, t, PS_COLS), x0r, x0i, p['lre'], p['lim'], p['wb'], p['wc'],
                           p['d'], p['wglu'], p['bglu'], l, bsz, t, tl['s5_rows'])
        re_o.append(xr.reshape(bsz, S5_GROUPS, S5_STATE))
        im_o.append(xi.reshape(bsz, S5_GROUPS, S5_STATE))
        o_s5 = o_s5.reshape(n, S5_WIDTH)
        if l < depth - 1:
            stack_here = l + 1 == depth - 1
            x2, *proj = _out_in(x2, o_gla, o_mla, o_s5, p, cos_t, sin_t, l, bsz, t, tl['row'], with_kv,
                                ckv_prev=tuple(ckv_o) if stack_here else ())
        else:
            x2 = _outproj(x2, o_gla, o_mla, o_s5, p['w_out'], gain_f, l, tl['out_row'], final=True)
    ckv_all = jnp.stack(ckv_o) if ckv_stack is None else ckv_stack
    return (x2.reshape(bsz, t, D_MODEL), jnp.stack(gla_o), ckv_all.reshape(depth, bsz, t, MLA_KV_RANK),
            jnp.stack(kpe_o), jnp.stack(re_o), jnp.stack(im_o))


def kernel(x_prompt, x_sample, state_gla, cache_mla_ckv, cache_mla_kpe, state_s5_re, state_s5_im, ln_gain, w_in, gla_w_gate, gla_b_gate, gla_norm_gain, mla_q_norm_gain, mla_w_uq, mla_kv_norm_gain, mla_w_ukv, s5_lambda_re, s5_lambda_im, s5_b_re, s5_b_im, s5_c_re, s5_c_im, s5_d, s5_log_dt, s5_w_glu, s5_b_glu, w_out, final_gain):
    p = _prepare_params(ln_gain, w_in, gla_w_gate, gla_b_gate, gla_norm_gain, mla_q_norm_gain, mla_w_uq,
                        mla_kv_norm_gain, mla_w_ukv, s5_lambda_re, s5_lambda_im, s5_b_re, s5_b_im,
                        s5_c_re, s5_c_im, s5_d, s5_log_dt, s5_w_glu, s5_b_glu, w_out)
    y_p, gla_p, ckv_p, kpe_p, re_p, im_p = _trunk(x_prompt, None, None, None, None, None, p, final_gain)
    y_s, gla_s, ckv_s, kpe_s, re_s, im_s = _trunk(x_sample, state_gla, cache_mla_ckv, cache_mla_kpe,
                                                  state_s5_re, state_s5_im, p, final_gain)
    return (y_p, y_s, gla_p, ckv_p, kpe_p, re_p, im_p, gla_s, ckv_s, kpe_s, re_s, im_s)
```

```python
import functools
import math

import numpy as np
import jax
import jax.numpy as jnp
from jax import lax
from jax.experimental import pallas as pl
from jax.experimental.pallas import tpu as pltpu

F32 = jnp.float32
BF16 = jnp.bfloat16

LANES = 128
D_MODEL = 1024
CHUNK = 64
EPS = 1e-6
GLA_HEADS = 4
GLA_DV = 64
GLA_DK = 32
GLA_WIDTH = GLA_HEADS * GLA_DV
GLA_QK = GLA_HEADS * GLA_DK
GLA_GATE_RANK = 16
GLA_GATE_TAU = 16.0
GLA_SUB = 16
GLA_PLAIN_MAX_DECAY = 60.0
MLA_HEADS = 4
MLA_NOPE_DIM = 64
MLA_ROPE_DIM = 32
MLA_V_DIM = 128
MLA_Q_RANK = 192
MLA_KV_RANK = 128
MLA_WIDTH = MLA_HEADS * MLA_V_DIM
MLA_QK_PAD = 128
MLA_QK_WIDTH = MLA_HEADS * MLA_QK_PAD
ROPE_BASE = 10000.0
S5_GROUPS = 16
S5_GROUP_CH = 16
S5_STATE = 64
S5_WIDTH = S5_GROUPS * S5_GROUP_CH
S5_NSTATE = S5_GROUPS * S5_STATE
S5_BATCH_TILE = 8
S5_SCAN_SLICES = 4
GELU_TANH_SCALE = math.sqrt(2.0 / math.pi)
GELU_TANH_CUBIC = 0.044715

PG_COLS = 896
PS_COLS = 512
_IN_SEGS = (('g_q', GLA_QK), ('g_k', GLA_QK), ('g_v', GLA_WIDTH), ('g_lr', GLA_GATE_RANK), ('g_z', GLA_WIDTH),
            ('m_cq', MLA_Q_RANK), ('m_ckv', MLA_KV_RANK), ('m_kr', MLA_ROPE_DIM), ('m_z', MLA_WIDTH),
            ('s_u', S5_WIDTH), ('s_z', S5_WIDTH), ('end', 0))
PG_Q, PG_K, PG_V, PG_Z, PG_LR = slice(0, 128), slice(128, 256), slice(256, 512), slice(512, 768), slice(768, 896)
PG_QKV = slice(PG_Q.start, PG_V.stop)
IN_OFF = dict(zip([n for n, _ in _IN_SEGS], np.cumsum([0] + [w for _, w in _IN_SEGS[:-1]]).tolist()))

VMEM_LIMIT_BYTES = 48 * 1024 * 1024


def _tiles(bsz, t, past):
    n = bsz * t
    s_len = past + t
    return dict(
        row=min(512, n),
        in_row=min(1024, n) if t >= 1024 or t < 512 else 512,
        out_row=min(1024, n),
        cached_seqs=math.gcd(bsz, max(1, 128 // t)),
        gla_rows=min(512, t), gla_chunk=min(CHUNK, t),
        gla_seqs=math.gcd(bsz, max(8, 128 // t)),
        attn_q=min(512, t), attn_k=min(512, s_len),
        s5_rows=min(64, t))


def _cparams(*sem):
    return pltpu.CompilerParams(dimension_semantics=sem, vmem_limit_bytes=VMEM_LIMIT_BYTES)


def _sigmoid(x):
    return 0.5 * (1.0 + jnp.tanh(0.5 * x))


def _dot(a, b):
    return jnp.dot(a, b, preferred_element_type=F32)


def _dot_t(a, b):
    return lax.dot_general(a, b, (((0,), (0,)), ((), ())), preferred_element_type=F32)


def _const_spec(shape):
    zeros = (0,) * len(shape)
    return pl.BlockSpec(shape, lambda *_: zeros)


def _layer_spec(shape, layer, single_buffer=False):
    zeros = (0,) * len(shape)
    mode = dict(pipeline_mode=pl.Buffered(1)) if single_buffer else {}
    return pl.BlockSpec((None,) + tuple(shape), lambda *_: (layer,) + zeros, **mode)


def _rope128(x, cos_t, sin_t):
    lane = lax.broadcasted_iota(jnp.int32, x.shape, 1)
    first_half = (lane >= MLA_NOPE_DIM) & (lane < MLA_NOPE_DIM + MLA_ROPE_DIM // 2)
    rot = jnp.where(first_half, -pltpu.roll(x, LANES - MLA_ROPE_DIM // 2, 1), pltpu.roll(x, MLA_ROPE_DIM // 2, 1))
    return x * cos_t + rot * sin_t


def _mla_prep_rows(cq, ckv, kr, mla, with_kv):
    if with_kv:
        cos_ref, sin_ref, gq_ref, wq_ref, gkv_ref, wkv_ref, q_ref, ckv_ref, kpet_ref, k_ref, v_ref = mla
    else:
        cos_ref, sin_ref, gq_ref, wq_ref, gkv_ref, q_ref, ckv_ref, kpe_ref = mla
    cos_t = cos_ref[...]
    sin_t = sin_ref[...]
    ones = jnp.ones((2 * LANES, LANES), BF16)
    ms = _dot((cq * cq).astype(BF16), ones) * (1.0 / MLA_Q_RANK)
    inv = lax.rsqrt(ms + EPS)
    cqn = (cq * jnp.concatenate([inv, inv], axis=1) * gq_ref[...]).astype(BF16)
    qh = _dot(cqn, wq_ref[...])
    scale = (MLA_NOPE_DIM + MLA_ROPE_DIM) ** -0.5 * math.log2(math.e)
    for h in range(MLA_HEADS):
        x = qh[:, h * MLA_QK_PAD:(h + 1) * MLA_QK_PAD]
        roped = x * cos_t + pltpu.roll(x, LANES - MLA_ROPE_DIM, 1) * sin_t
        q_ref[:, h * MLA_QK_PAD:(h + 1) * MLA_QK_PAD] = (roped * scale).astype(BF16)
    ms = _dot((ckv * ckv).astype(BF16), ones[0:LANES]) * (1.0 / MLA_KV_RANK)
    ckv_n = ckv * lax.rsqrt(ms + EPS) * gkv_ref[...]
    ckv_ref[...] = ckv_n
    kpe128 = _rope128(kr, cos_t, sin_t)
    if with_kv:
        kpet_ref[...] = jnp.transpose(kpe128)[MLA_NOPE_DIM:MLA_NOPE_DIM + MLA_ROPE_DIM, :]
        kv = _dot(ckv_n.astype(BF16), wkv_ref[...])
        for h in range(MLA_HEADS):
            sl = slice(h * MLA_QK_PAD, (h + 1) * MLA_QK_PAD)
            k_ref[:, sl] = (kv[:, sl] + kpe128).astype(BF16)
        v_ref[...] = kv[:, MLA_QK_WIDTH:].astype(BF16)
    else:
        kpe_ref[...] = kpe128


def _inproj_rows(x, g_ref, w_scr, og_ref, mz_ref, os_ref, mla, with_kv):
    ms = jnp.mean(x * x, axis=-1, keepdims=True)
    h = (x * lax.rsqrt(ms + EPS) * g_ref[...]).astype(BF16)

    def seg(a, b):
        return lax.dot_general(h, w_scr[a:b, :], (((1,), (1,)), ((), ())), preferred_element_type=F32)

    lane = lax.broadcasted_iota(jnp.int32, (x.shape[0], LANES), 1)
    lane2 = lax.broadcasted_iota(jnp.int32, (x.shape[0], 2 * LANES), 1)
    c = IN_OFF
    og_ref[:, PG_QKV] = seg(c['g_q'], c['g_lr']).astype(BF16)
    og_ref[:, PG_Z] = seg(c['g_z'], c['m_cq']).astype(BF16)
    og_ref[:, PG_LR] = jnp.where(lane < GLA_GATE_RANK, seg(c['g_lr'], c['g_lr'] + LANES), 0.0).astype(BF16)
    mz_ref[...] = seg(c['m_z'], c['s_u']).astype(BF16)
    os_ref[...] = seg(c['s_u'], c['end']).astype(BF16)
    cq = jnp.where(lane2 < MLA_Q_RANK, seg(c['m_cq'], c['m_cq'] + 2 * LANES), 0.0)
    ckv_kr = seg(c['m_ckv'], c['m_ckv'] + 2 * LANES)
    kr = pltpu.roll(ckv_kr[:, LANES:], MLA_NOPE_DIM, 1)
    kr = jnp.where((lane >= MLA_NOPE_DIM) & (lane < MLA_NOPE_DIM + MLA_ROPE_DIM), kr, 0.0)
    _mla_prep_rows(cq, ckv_kr[:, 0:LANES], kr, mla, with_kv)


N_MLA_IN = {True: 6, False: 5}
N_MLA_OUT = {True: 5, False: 3}


def _inproj_body(x_ref, g_ref, wt_ref, *rest, with_kv):
    n_in, n_out = N_MLA_IN[with_kv], N_MLA_OUT[with_kv]
    mla_in, (og_ref, mz_ref, os_ref), mla_out, (w_scr,) = (
        rest[:n_in], rest[n_in:n_in + 3], rest[n_in + 3:n_in + 3 + n_out], rest[n_in + 3 + n_out:])

    @pl.when(pl.program_id(0) == 0)
    def _():
        w_scr[...] = wt_ref[...].astype(BF16)

    _inproj_rows(x_ref[...], g_ref, w_scr, og_ref, mz_ref, os_ref, tuple(mla_in) + tuple(mla_out), with_kv)


def _mla_specs(p, cos_t, sin_t, layer, bsz, t, tm, with_kv):
    n = bsz * t
    ntab = max(1, t // tm)
    assert not with_kv or tm <= t
    row = lambda c: pl.BlockSpec((tm, c), lambda i: (i, 0))
    table = pl.BlockSpec((tm, LANES), lambda i: (i % ntab, 0))
    args = [cos_t, sin_t, p['gq'], p['wq'], p['gkv']]
    in_specs = [table, table, _layer_spec((1, 256), layer), _layer_spec((256, MLA_QK_WIDTH), layer),
                _layer_spec((1, MLA_KV_RANK), layer)]
    out_specs = [row(MLA_QK_WIDTH), row(MLA_KV_RANK)]
    out_shape = [jax.ShapeDtypeStruct((n, MLA_QK_WIDTH), BF16), jax.ShapeDtypeStruct((n, MLA_KV_RANK), F32)]
    if with_kv:
        args.append(p['wkv'])
        in_specs.append(_layer_spec((MLA_KV_RANK, MLA_QK_WIDTH + MLA_WIDTH), layer))
        out_specs += [pl.BlockSpec((None, MLA_ROPE_DIM, tm), lambda i: (i // ntab, 0, i % ntab)),
                      row(MLA_QK_WIDTH), row(MLA_WIDTH)]
        out_shape += [jax.ShapeDtypeStruct((bsz, MLA_ROPE_DIM, t), F32),
                      jax.ShapeDtypeStruct((n, MLA_QK_WIDTH), BF16), jax.ShapeDtypeStruct((n, MLA_WIDTH), BF16)]
    else:
        out_specs.append(row(LANES))
        out_shape.append(jax.ShapeDtypeStruct((n, LANES), F32))
    return args, in_specs, out_specs, out_shape


def _proj_out(tm, n):
    row = lambda c: pl.BlockSpec((tm, c), lambda i: (i, 0))
    return ([row(PG_COLS), row(MLA_WIDTH), row(PS_COLS)],
            [jax.ShapeDtypeStruct((n, PG_COLS), BF16), jax.ShapeDtypeStruct((n, MLA_WIDTH), BF16),
             jax.ShapeDtypeStruct((n, PS_COLS), BF16)])


def _inproj(x2, p, cos_t, sin_t, layer, bsz, t, tm, with_kv):
    n = x2.shape[0]
    m_args, m_in, m_out, m_shape = _mla_specs(p, cos_t, sin_t, layer, bsz, t, tm, with_kv)
    o_specs, o_shape = _proj_out(tm, n)
    return pl.pallas_call(
        functools.partial(_inproj_body, with_kv=with_kv),
        grid=(n // tm,),
        in_specs=[pl.BlockSpec((tm, D_MODEL), lambda i: (i, 0)),
                  _layer_spec((1, D_MODEL), layer),
                  _layer_spec((IN_OFF['end'], D_MODEL), layer, single_buffer=True)] + m_in,
        scratch_shapes=[pltpu.VMEM((IN_OFF['end'], D_MODEL), BF16)],
        out_specs=o_specs + m_out,
        out_shape=o_shape + m_shape,
        compiler_params=_cparams("arbitrary"),
        name="inproj",
    )(x2, p['ln'], p['w_in'], *m_args)


def _gla_body(p_ref, wg_ref, bg_ref, gain_ref, ones_k_ref, ones_v_ref, bd_ref, s0_ref,
              o_ref, sout_ref, q_scr, k_scr, v_scr, b_scr, s_scr, qe_scr, qs_scr, ke_scr, vb_scr, o_scr,
              *, chunk, nch, nseq):
    it = pl.program_id(1)
    sub = min(GLA_SUB, chunk)
    nsub = chunk // sub

    @pl.when(it == 0)
    def _():
        s_scr[...] = jnp.zeros(s_scr.shape, F32)
        for sq in range(nseq):
            for h in range(GLA_HEADS):
                s_scr[sq, h * GLA_DK:(h + 1) * GLA_DK, h * GLA_DV:(h + 1) * GLA_DV] = s0_ref[sq, h]

    def cols(lanes):
        parts = [p_ref[sq, :, lanes] for sq in range(nseq)]
        return parts[0] if nseq == 1 else jnp.concatenate(parts, axis=0)

    q_scr[...] = cols(PG_Q).astype(F32) * (GLA_DK ** -0.5)
    k_scr[...] = cols(PG_K).astype(F32)
    v_scr[...] = cols(PG_V).astype(F32)
    logit = _dot(cols(PG_LR), wg_ref[...]) + bg_ref[...]
    log_a = (jnp.minimum(logit, 0.0) - jnp.log(1.0 + jnp.exp(-jnp.abs(logit)))) * (1.0 / GLA_GATE_TAU)
    row_in_chunk = lax.broadcasted_iota(jnp.int32, log_a.shape, 0) % chunk
    b_all = log_a
    shift = 1
    while shift < chunk:
        b_all = b_all + jnp.where(row_in_chunk >= shift, pltpu.roll(b_all, shift, 0), 0.0)
        shift *= 2
    b_scr[...] = b_all

    row = lax.broadcasted_iota(jnp.int32, (chunk, GLA_QK), 0)
    row_in_sub = lax.broadcasted_iota(jnp.int32, (sub, GLA_QK), 0)

    def decay_columns(b_end):
        col = jnp.transpose(jnp.broadcast_to(jnp.exp(b_end), (GLA_QK, GLA_QK)))
        return jnp.concatenate([col, col], axis=1)

    def load_chunk(sq, c):
        r0 = pl.multiple_of((sq * nch + c) * chunk, chunk)
        return (r0, q_scr[pl.ds(r0, chunk), :], k_scr[pl.ds(r0, chunk), :], v_scr[pl.ds(r0, chunk), :],
                b_scr[pl.ds(r0, chunk), :], b_scr[pl.ds(r0 + chunk - 1, 1), :], s_scr[sq])

    def finish_chunk(sq, r0, o, s_prev, a_state, b_end):
        ms = _dot((o * o).astype(BF16), ones_v_ref[...]) * (1.0 / GLA_DV)
        o_n = o * lax.rsqrt(ms + EPS) * gain_ref[...]
        t0 = r0 - sq * nch * chunk
        z = p_ref[sq, pl.ds(t0, chunk), PG_Z].astype(F32)
        o_ref[sq, pl.ds(t0, chunk), :] = (o_n * (z * _sigmoid(z))).astype(BF16)
        s_scr[sq] = s_prev * decay_columns(b_end) + a_state

    def robust_chunk(c, carry, sq):
        r0, qc, kc, vc, bc, b_end, s_prev = load_chunk(sq, c)
        xs = [qc * jnp.exp(bc)]
        ks = []
        for sj in range(nsub - 1):
            e_j = b_scr[pl.ds(r0 + (sj + 1) * sub - 1, 1), :]
            later = row >= (sj + 1) * sub
            xs.append(jnp.where(later, qc * jnp.exp(jnp.where(later, bc - e_j, 0.0)), 0.0))
            own = (row >= sj * sub) & (row < (sj + 1) * sub)
            ks.append(jnp.where(own, kc * jnp.exp(jnp.where(own, e_j - bc, 0.0)), 0.0))
        ks.append(kc * jnp.exp(b_end - bc))
        k_all = jnp.concatenate(ks, axis=1).astype(BF16)
        a_all = _dot_t(k_all, vc.astype(BF16)) * bd_ref[...]
        w = jnp.concatenate([s_prev, a_all[:(nsub - 1) * GLA_QK]], axis=0).astype(BF16) if nsub > 1 \
            else s_prev.astype(BF16)
        o_off = _dot(jnp.concatenate(xs, axis=1).astype(BF16), w)
        rows = []
        for si in range(nsub):
            q_i = qc[si * sub:(si + 1) * sub]
            b_i = bc[si * sub:(si + 1) * sub]
            es = []
            for j in range(sub):
                r = r0 + si * sub + j
                b_j = b_scr[pl.ds(r, 1), :]
                k_j = k_scr[pl.ds(r, 1), :]
                valid = row_in_sub >= j
                es.append(q_i * k_j * jnp.exp(jnp.where(valid, b_i - b_j, -jnp.inf)))
            e_all = jnp.concatenate(es, axis=0)
            e_hi = e_all.astype(BF16)
            e_lo = (e_all - e_hi.astype(F32)).astype(BF16)
            p_all = _dot(e_hi, ones_k_ref[...]) + _dot(e_lo, ones_k_ref[...])
            acc = o_off[si * sub:(si + 1) * sub]
            for j in range(sub):
                v_j = v_scr[pl.ds(r0 + si * sub + j, 1), :]
                acc = acc + p_all[j * sub:(j + 1) * sub] * v_j
            rows.append(acc)
        o = rows[0] if nsub == 1 else jnp.concatenate(rows, axis=0)
        finish_chunk(sq, r0, o, s_prev, a_all[(nsub - 1) * GLA_QK:], b_end)
        return carry

    lane_head_v = lax.broadcasted_iota(jnp.int32, (chunk, GLA_WIDTH), 1) // GLA_DV
    causal = (lax.broadcasted_iota(jnp.int32, (GLA_HEADS * chunk, chunk), 0) % chunk
              >= lax.broadcasted_iota(jnp.int32, (GLA_HEADS * chunk, chunk), 1))

    def plain_block():
        tt = nseq * nch * chunk
        b_all = b_scr[...]
        q_all = q_scr[...]
        k_all = k_scr[...]
        qe = q_all * jnp.exp(b_all)
        lane_head = lax.broadcasted_iota(jnp.int32, (tt, GLA_QK), 1) // GLA_DK
        qe_scr[...] = qe.astype(BF16)
        for h in range(GLA_HEADS):
            qs_scr[h] = jnp.where(lane_head == h, qe, 0.0).astype(BF16)
        ke_scr[...] = (k_all * jnp.exp(-b_all)).astype(BF16)
        vb_scr[...] = cols(PG_V)
        states = [s_scr[sq] for sq in range(nseq)]
        for c, sq in [(c, sq) for c in range(nch) for sq in range(nseq)]:
            r0 = (sq * nch + c) * chunk
            rs = slice(r0, r0 + chunk)
            s_cur = states[sq]
            qs = jnp.concatenate([qs_scr[h, rs, :] for h in range(GLA_HEADS)], axis=0)
            s = lax.dot_general(qs, ke_scr[rs, :], (((1,), (1,)), ((), ())), preferred_element_type=F32)
            s = jnp.where(causal, s, 0.0).astype(BF16)
            r = _dot(s, vb_scr[rs, :])
            o = _dot(qe_scr[rs, :], s_cur.astype(BF16))
            for h in range(GLA_HEADS):
                o = o + jnp.where(lane_head_v == h, r[h * chunk:(h + 1) * chunk], 0.0)
            o_scr[rs, :] = o
            b_end = b_scr[r0 + chunk - 1:r0 + chunk, :]
            k_end = (k_scr[rs, :] * jnp.exp(b_end - b_scr[rs, :])).astype(BF16)
            a_state = _dot_t(k_end, vb_scr[rs, :]) * bd_ref[0:GLA_QK, :]
            states[sq] = s_cur * decay_columns(b_end) + a_state
        for sq in range(nseq):
            s_scr[sq] = states[sq]
        o = o_scr[...]
        ms = _dot((o * o).astype(BF16), ones_v_ref[...]) * (1.0 / GLA_DV)
        o_n = o * lax.rsqrt(ms + EPS) * gain_ref[...]
        z = cols(PG_Z).astype(F32)
        o_all = (o_n * (z * _sigmoid(z))).astype(BF16)
        for sq in range(nseq):
            o_ref[sq] = o_all[sq * nch * chunk:(sq + 1) * nch * chunk]

    in_range = jnp.max(-b_scr[...]) < GLA_PLAIN_MAX_DECAY

    @pl.when(in_range)
    def _():
        plain_block()

    @pl.when(jnp.logical_not(in_range))
    def _():
        for sq in range(nseq):
            lax.fori_loop(0, nch, functools.partial(robust_chunk, sq=sq), 0)

    @pl.when(it == pl.num_programs(1) - 1)
    def _():
        for sq in range(nseq):
            for h in range(GLA_HEADS):
                sout_ref[sq, h] = s_scr[sq, h * GLA_DK:(h + 1) * GLA_DK, h * GLA_DV:(h + 1) * GLA_DV]


def _gla_consts(chunk):
    nsub = chunk // min(GLA_SUB, chunk)
    hk = np.arange(GLA_QK) // GLA_DK
    hv = np.arange(GLA_WIDTH) // GLA_DV
    same_kv = (hk[:, None] == hv[None, :]).astype(np.float32)
    same_vv = (hv[:, None] == hv[None, :]).astype(np.float32)
    return (jnp.asarray(same_kv, BF16), jnp.asarray(same_vv, BF16),
            jnp.asarray(np.tile(same_kv, (nsub, 1)), F32))


def _gla(pg, wg, bg, gain, layer, s0, bsz, t, tl):
    chunk, tt, nseq = tl['gla_chunk'], tl['gla_rows'], tl['gla_seqs']
    nch = tt // chunk
    nt = t // tt
    rows = nseq * tt
    nsub = chunk // min(GLA_SUB, chunk)
    ones_k, ones_v, bd = _gla_consts(chunk)
    const = _const_spec
    return pl.pallas_call(
        functools.partial(_gla_body, chunk=chunk, nch=nch, nseq=nseq),
        grid=(bsz // nseq, nt),
        in_specs=[pl.BlockSpec((nseq, tt, PG_COLS), lambda b, i: (b, i, 0)),
                  _layer_spec((GLA_QK, GLA_QK), layer), _layer_spec((1, GLA_QK), layer),
                  _layer_spec((1, GLA_WIDTH), layer),
                  const((GLA_QK, GLA_WIDTH)), const((GLA_WIDTH, GLA_WIDTH)),
                  const((nsub * GLA_QK, GLA_WIDTH)),
                  pl.BlockSpec((nseq, GLA_HEADS, GLA_DK, GLA_DV), lambda b, i: (b, 0, 0, 0))],
        out_specs=[pl.BlockSpec((nseq, tt, GLA_WIDTH), lambda b, i: (b, i, 0)),
                   pl.BlockSpec((nseq, GLA_HEADS, GLA_DK, GLA_DV), lambda b, i: (b, 0, 0, 0))],
        out_shape=[jax.ShapeDtypeStruct((bsz, t, GLA_WIDTH), BF16),
                   jax.ShapeDtypeStruct((bsz, GLA_HEADS, GLA_DK, GLA_DV), F32)],
        scratch_shapes=[pltpu.VMEM((rows, GLA_QK), F32), pltpu.VMEM((rows, GLA_QK), F32),
                        pltpu.VMEM((rows, GLA_WIDTH), F32), pltpu.VMEM((rows, GLA_QK), F32),
                        pltpu.VMEM((nseq, GLA_QK, GLA_WIDTH), F32),
                        pltpu.VMEM((rows, GLA_QK), BF16), pltpu.VMEM((GLA_HEADS, rows, GLA_QK), BF16),
                        pltpu.VMEM((rows, GLA_QK), BF16),
                        pltpu.VMEM((rows, GLA_WIDTH), BF16), pltpu.VMEM((rows, GLA_WIDTH), F32)],
        compiler_params=_cparams("parallel", "arbitrary"),
        name="gla",
    )(pg, wg, bg, gain, ones_k, ones_v, bd, s0)


def _attn_cached_body(q_ref, z_ref, ckv_new_ref, kpe_new_ref, ckv_past_ref, kpet_past_ref, wka_ref, wv_ref,
                      o_ref, kpet_scr, *, past, t, nseq):
    last = (((1,), (1,)), ((), ()))
    hsl = [slice(h * MLA_QK_PAD, (h + 1) * MLA_QK_PAD) for h in range(MLA_HEADS)]
    q_lat_h = [_dot(q_ref[:, hsl[h]], wka_ref[h]).astype(BF16) for h in range(MLA_HEADS)]
    kpet_scr[...] = jnp.zeros(kpet_scr.shape, BF16)
    o_lat = []
    for sq in range(nseq):
        rs = slice(sq * t, (sq + 1) * t)
        q_rows = jnp.concatenate([q_ref[rs, hsl[h]] for h in range(MLA_HEADS)], axis=0)
        q_lat = jnp.concatenate([q_lat_h[h][rs] for h in range(MLA_HEADS)], axis=0)
        c_past = ckv_past_ref[sq].astype(BF16)
        c_new = ckv_new_ref[rs, :].astype(BF16)
        kpet_scr[sq, MLA_NOPE_DIM:MLA_NOPE_DIM + MLA_ROPE_DIM, :] = kpet_past_ref[sq].astype(BF16)
        s_past = (lax.dot_general(q_lat, c_past, last, preferred_element_type=F32)
                  + _dot(q_rows, kpet_scr[sq]))
        s_new = (lax.dot_general(q_lat, c_new, last, preferred_element_type=F32)
                 + lax.dot_general(q_rows, kpe_new_ref[rs, :].astype(BF16), last, preferred_element_type=F32))
        if past // CHUNK != (past + t - 1) // CHUNK:
            q_chunk = (past + lax.broadcasted_iota(jnp.int32, s_past.shape, 0) % t) // CHUNK
            s_past = jnp.where(lax.broadcasted_iota(jnp.int32, s_past.shape, 1) // CHUNK <= q_chunk, s_past, -jnp.inf)
            q_chunk = (past + lax.broadcasted_iota(jnp.int32, s_new.shape, 0) % t) // CHUNK
            s_new = jnp.where((past + lax.broadcasted_iota(jnp.int32, s_new.shape, 1)) // CHUNK <= q_chunk,
                              s_new, -jnp.inf)
        m = jnp.maximum(jnp.max(s_past, axis=-1, keepdims=True), jnp.max(s_new, axis=-1, keepdims=True))
        p_past = jnp.exp2(s_past - m)
        p_new = jnp.exp2(s_new - m)
        l = jnp.sum(p_past, axis=-1, keepdims=True) + jnp.sum(p_new, axis=-1, keepdims=True)
        o_lat.append(((_dot(p_past.astype(BF16), c_past) + _dot(p_new.astype(BF16), c_new)) / l).astype(BF16))
    for h in range(MLA_HEADS):
        vsl = slice(h * MLA_V_DIM, (h + 1) * MLA_V_DIM)
        o_h = jnp.concatenate([o_lat[sq][h * t:(h + 1) * t] for sq in range(nseq)], axis=0)
        z = z_ref[:, vsl].astype(F32)
        o_ref[:, vsl] = (_dot(o_h, wv_ref[h]) * (z * _sigmoid(z))).astype(BF16)


def _attn_cached(q, mz, ckv_new, kpe128, ckv_past, kpet_past, wka, wv, layer, bsz, t, nseq):
    past = ckv_past.shape[2]
    rows = nseq * t
    return pl.pallas_call(
        functools.partial(_attn_cached_body, past=past, t=t, nseq=nseq),
        grid=(bsz // nseq,),
        in_specs=[pl.BlockSpec((rows, MLA_QK_WIDTH), lambda b: (b, 0)),
                  pl.BlockSpec((rows, MLA_WIDTH), lambda b: (b, 0)),
                  pl.BlockSpec((rows, MLA_KV_RANK), lambda b: (b, 0)),
                  pl.BlockSpec((rows, LANES), lambda b: (b, 0)),
                  pl.BlockSpec((None, nseq, past, MLA_KV_RANK), lambda b: (layer, b, 0, 0)),
                  pl.BlockSpec((None, nseq, MLA_ROPE_DIM, past), lambda b: (layer, b, 0, 0)),
                  _layer_spec((MLA_HEADS, MLA_QK_PAD, MLA_KV_RANK), layer),
                  _layer_spec((MLA_HEADS, MLA_KV_RANK, MLA_V_DIM), layer)],
        out_specs=pl.BlockSpec((rows, MLA_WIDTH), lambda b: (b, 0)),
        out_shape=jax.ShapeDtypeStruct((bsz * t, MLA_WIDTH), BF16),
        scratch_shapes=[pltpu.VMEM((nseq, MLA_QK_PAD, past), BF16)],
        compiler_params=_cparams("parallel"),
        name="mla_attn_cached",
    )(q, mz, ckv_new, kpe128, ckv_past, kpet_past, wka, wv)


def _attn_body(q_ref, k_ref, v_ref, z_ref, o_ref, m_scr, acc_scr, *, past, tq, tk, s_len):
    iq = pl.program_id(1)
    q_first = past + iq * tq
    full_keys = jnp.minimum((q_first // CHUNK + 1) * CHUNK, s_len)
    vis_keys = jnp.minimum(((q_first + tq - 1) // CHUNK + 1) * CHUNK, s_len)
    n_full = full_keys // tk
    n_vis = (vis_keys + tk - 1) // tk
    split_diagonal = past % tk == 0 and tq == tk and (tq // 2) % CHUNK == 0

    def block(kb, carry, masked, first=False):
        k0 = pl.multiple_of(kb * tk, tk)
        if masked and split_diagonal:
            parts = [(r * (tq // 2), tq // 2, (r + 1) * (tk // 2)) for r in range(2)]
        else:
            parts = [(0, tq, tk)]
        for r0, nr, kext in parts:
            rows = slice(r0, r0 + nr)
            if masked:
                q_pos = (r0 if split_diagonal else q_first + r0) + lax.broadcasted_iota(jnp.int32, (nr, kext), 0)
                k_pos = (0 if split_diagonal else k0) + lax.broadcasted_iota(jnp.int32, (nr, kext), 1)
                visible = k_pos // CHUNK <= q_pos // CHUNK
            ntile, rem = kext // LANES, kext % LANES
            ones_v = jnp.ones((kext, MLA_V_DIM), BF16)
            for h in range(MLA_HEADS):
                sl = slice(h * MLA_QK_PAD, (h + 1) * MLA_QK_PAD)
                vsl = slice(h * MLA_V_DIM, (h + 1) * MLA_V_DIM)
                s = lax.dot_general(q_ref[rows, sl], k_ref[pl.ds(k0, kext), sl], (((1,), (1,)), ((), ())),
                                    preferred_element_type=F32)
                if masked:
                    s = jnp.where(visible, s, -jnp.inf)
                if first:
                    m_new = jnp.broadcast_to(jnp.max(s, axis=-1, keepdims=True), (nr, LANES))
                else:
                    m_prev = m_scr[h, rows]
                    m_new = jnp.maximum(m_prev, jnp.max(s, axis=-1, keepdims=True))
                    alpha = jnp.exp2(m_prev - m_new)
                ps = [jnp.exp2(s[:, c * LANES:(c + 1) * LANES] - m_new) for c in range(ntile)]
                if rem:
                    ps.append(jnp.exp2(s[:, ntile * LANES:] - m_new[:, :rem]))
                p = jnp.concatenate(ps, axis=1).astype(BF16)
                v_ext = jnp.concatenate([v_ref[pl.ds(k0, kext), vsl], ones_v], axis=1)
                pv = _dot(p, v_ext)
                acc_scr[h, rows] = pv if first else jnp.concatenate([alpha, alpha], axis=1) * acc_scr[h, rows] + pv
                m_scr[h, rows] = m_new
        return carry

    @pl.when(n_full > 0)
    def _():
        block(0, 0, masked=False, first=True)
        lax.fori_loop(1, n_full, functools.partial(block, masked=False), 0)
        lax.fori_loop(n_full, n_vis, functools.partial(block, masked=True), 0)

    @pl.when(n_full == 0)
    def _():
        block(0, 0, masked=True, first=True)
        lax.fori_loop(1, n_vis, functools.partial(block, masked=True), 0)

    for h in range(MLA_HEADS):
        vsl = slice(h * MLA_V_DIM, (h + 1) * MLA_V_DIM)
        z = z_ref[:, vsl].astype(F32)
        acc = acc_scr[h]
        o_ref[:, vsl] = (acc[:, :MLA_V_DIM] / acc[:, MLA_V_DIM:] * (z * _sigmoid(z))).astype(BF16)


def _attn(q, k, v, mz, bsz, t, s_len, past, tq, tk):
    nq = t // tq
    return pl.pallas_call(
        functools.partial(_attn_body, past=past, tq=tq, tk=tk, s_len=s_len),
        grid=(bsz, nq),
        in_specs=[pl.BlockSpec((tq, MLA_QK_WIDTH), lambda b, iq: (b * nq + iq, 0)),
                  pl.BlockSpec((s_len, MLA_QK_WIDTH), lambda b, iq: (b, 0)),
                  pl.BlockSpec((s_len, MLA_WIDTH), lambda b, iq: (b, 0)),
                  pl.BlockSpec((tq, MLA_WIDTH), lambda b, iq: (b * nq + iq, 0))],
        out_specs=pl.BlockSpec((tq, MLA_WIDTH), lambda b, iq: (b * nq + iq, 0)),
        out_shape=jax.ShapeDtypeStruct((bsz * t, MLA_WIDTH), BF16),
        scratch_shapes=[pltpu.VMEM((MLA_HEADS, tq, LANES), F32), pltpu.VMEM((MLA_HEADS, tq, 2 * MLA_V_DIM), F32)],
        compiler_params=_cparams("parallel", "arbitrary"),
        name="mla_attn",
    )(q, k, v, mz)


def _s5_weights_body(bre_ref, bim_ref, cre_ref, cim_ref, spread_ref, mask_ref, wb_ref, wc_ref):
    def expand(m_ref):
        return _dot(m_ref[...].astype(BF16), spread_ref[...]) * mask_ref[...]

    wb_ref[:, 0:S5_NSTATE] = expand(bre_ref).astype(BF16)
    wb_ref[:, S5_NSTATE:] = expand(bim_ref).astype(BF16)
    wc_ref[0:S5_NSTATE, :] = jnp.transpose(expand(cre_ref)).astype(BF16)
    wc_ref[S5_NSTATE:, :] = jnp.transpose(-expand(cim_ref)).astype(BF16)


def _s5_weights(bbr, bbi, c_re, c_im):
    depth = bbr.shape[0]
    g_row = np.arange(S5_WIDTH) // S5_GROUP_CH
    g_col = np.arange(S5_NSTATE) // S5_STATE
    mask = (g_row[:, None] == g_col[None, :]).astype(np.float32)
    spread = (np.arange(S5_STATE)[:, None] == (np.arange(S5_NSTATE) % S5_STATE)[None, :]).astype(np.float32)
    small = pl.BlockSpec((None, S5_WIDTH, S5_STATE), lambda l: (l, 0, 0))
    return pl.pallas_call(
        _s5_weights_body,
        grid=(depth,),
        in_specs=[small, small, small, small,
                  _const_spec((S5_STATE, S5_NSTATE)), _const_spec((S5_WIDTH, S5_NSTATE))],
        out_specs=[pl.BlockSpec((None, S5_WIDTH, 2 * S5_NSTATE), lambda l: (l, 0, 0)),
                   pl.BlockSpec((None, 2 * S5_NSTATE, S5_WIDTH), lambda l: (l, 0, 0))],
        out_shape=[jax.ShapeDtypeStruct((depth, S5_WIDTH, 2 * S5_NSTATE), BF16),
                   jax.ShapeDtypeStruct((depth, 2 * S5_NSTATE, S5_WIDTH), BF16)],
        compiler_params=_cparams("parallel"),
        name="s5_weights",
    )(bbr, bbi, c_re, c_im, jnp.asarray(spread, BF16), jnp.asarray(mask, F32))


def _s5_body(p0_ref, pn_ref, x0r_ref, x0i_ref, lre_ref, lim_ref, wb_ref, wc_ref, d_ref, wglu_ref, bglu_ref, never_ref,
             o_ref, xr_out, xi_out, uz_bt, uz_a, uz_b, uz_c, bu_a, bu_b, bu_c, o_tb, xr_s, xi_s, *, lc, pitch, nt):
    it = pl.program_id(1)
    nb = S5_BATCH_TILE
    ring = ((uz_a, bu_a), (uz_b, bu_b), (uz_c, bu_c))

    def stage_in(blk_ref, uz_tb, bu):
        for b in range(nb):
            for c in range(PS_COLS // LANES):
                uz_bt[c, b * pitch:b * pitch + lc, :] = blk_ref[b, :, c * LANES:(c + 1) * LANES].astype(F32)
        for t in range(lc):
            for c in range(PS_COLS // LANES):
                uz_tb[t * nb:(t + 1) * nb, c * LANES:(c + 1) * LANES] = uz_bt[c, pl.ds(t, nb, stride=pitch), :]
        bu[...] = _dot(uz_tb[:, 0:S5_WIDTH].astype(BF16), wb_ref[...])

    def stage_scan(bu):
        w = S5_NSTATE // S5_SCAN_SLICES
        never = never_ref[...] != 0
        last = None
        for c0 in range(0, S5_NSTATE, w):
            re, im = slice(c0, c0 + w), slice(S5_NSTATE + c0, S5_NSTATE + c0 + w)
            lre = jnp.broadcast_to(lre_ref[:, re], (nb, w))
            lim = jnp.broadcast_to(lim_ref[:, re], (nb, w))
            xr, xi = xr_s[:, re], xi_s[:, re]
            if last is not None:
                xr = jnp.where(never, last, xr)
            for t in range(lc):
                rs = slice(t * nb, (t + 1) * nb)
                xr, xi = lre * xr - lim * xi + bu[rs, re], lre * xi + lim * xr + bu[rs, im]
                bu[rs, re] = xr
                bu[rs, im] = xi
            xr_s[:, re] = xr
            xi_s[:, re] = xi
            last = xr

    def stage_out(uz_tb, xs):
        y = _dot(xs[...].astype(BF16), wc_ref[...]) + d_ref[...] * uz_tb[:, 0:S5_WIDTH]
        g5 = 0.5 * y * (1.0 + jnp.tanh(GELU_TANH_SCALE * (y + GELU_TANH_CUBIC * (y * y * y))))
        gate = _sigmoid(_dot(g5.astype(BF16), wglu_ref[...]) + bglu_ref[...])
        z = uz_tb[:, S5_WIDTH:2 * S5_WIDTH]
        o = g5 * gate * (z * _sigmoid(z))
        for c in range(S5_WIDTH // LANES):
            o_tb[c] = o[:, c * LANES:(c + 1) * LANES]
        for b in range(nb):
            for c in range(S5_WIDTH // LANES):
                o_ref[b, :, c * LANES:(c + 1) * LANES] = o_tb[c, pl.ds(b, lc, stride=nb), :].astype(BF16)

    if nt == 1:
        xr_s[...] = x0r_ref[...]
        xi_s[...] = x0i_ref[...]
        stage_in(p0_ref, uz_a, bu_a)
        stage_scan(bu_a)
        stage_out(uz_a, bu_a)
        xr_out[...] = xr_s[...]
        xi_out[...] = xi_s[...]
        return

    @pl.when(it == 0)
    def _():
        xr_s[...] = x0r_ref[...]
        xi_s[...] = x0i_ref[...]
        stage_in(p0_ref, uz_a, bu_a)
        uz_c[...] = jnp.zeros(uz_c.shape, F32)
        bu_c[...] = jnp.zeros(bu_c.shape, F32)

    for r in range(3):
        @pl.when(it % 3 == r)
        def _(r=r):
            stage_in(pn_ref, *ring[(r + 1) % 3])
            stage_scan(ring[r][1])
            stage_out(*ring[(r + 2) % 3])

    @pl.when(it == pl.num_programs(1) - 2)
    def _():
        xr_out[...] = xr_s[...]
        xi_out[...] = xi_s[...]


def _s5(ps3, x0r, x0i, lre, lim, wb, wc, d, wglu, bglu, layer, bsz, t, lc):
    nb = S5_BATCH_TILE
    nt = t // lc
    pitch = lc + 8
    rows = lc * nb
    return pl.pallas_call(
        functools.partial(_s5_body, lc=lc, pitch=pitch, nt=nt),
        grid=(bsz // nb, nt + 1 if nt > 1 else 1),
        in_specs=[pl.BlockSpec((nb, lc, PS_COLS), lambda g, i: (g, 0, 0)),
                  pl.BlockSpec((nb, lc, PS_COLS), lambda g, i: (g, jnp.minimum(i + 1, nt - 1), 0)),
                  pl.BlockSpec((nb, S5_NSTATE), lambda g, i: (g, 0)),
                  pl.BlockSpec((nb, S5_NSTATE), lambda g, i: (g, 0)),
                  _layer_spec((1, S5_NSTATE), layer), _layer_spec((1, S5_NSTATE), layer),
                  _layer_spec((S5_WIDTH, 2 * S5_NSTATE), layer), _layer_spec((2 * S5_NSTATE, S5_WIDTH), layer),
                  _layer_spec((1, S5_WIDTH), layer), _layer_spec((S5_WIDTH, S5_WIDTH), layer),
                  _layer_spec((1, S5_WIDTH), layer),
                  _const_spec((1, S5_NSTATE // S5_SCAN_SLICES))],
        out_specs=[pl.BlockSpec((nb, lc, S5_WIDTH), lambda g, i: (g, jnp.maximum(i - 1, 0), 0)),
                   pl.BlockSpec((nb, S5_NSTATE), lambda g, i: (g, 0)),
                   pl.BlockSpec((nb, S5_NSTATE), lambda g, i: (g, 0))],
        out_shape=[jax.ShapeDtypeStruct((bsz, t, S5_WIDTH), BF16),
                   jax.ShapeDtypeStruct((bsz, S5_NSTATE), F32),
                   jax.ShapeDtypeStruct((bsz, S5_NSTATE), F32)],
        scratch_shapes=[pltpu.VMEM((PS_COLS // LANES, nb * pitch, LANES), F32)]
                       + [pltpu.VMEM((rows, PS_COLS), F32)] * 3
                       + [pltpu.VMEM((rows, 2 * S5_NSTATE), F32)] * 3
                       + [pltpu.VMEM((S5_WIDTH // LANES, rows, LANES), F32),
                          pltpu.VMEM((nb, S5_NSTATE), F32), pltpu.VMEM((nb, S5_NSTATE), F32)],
        compiler_params=_cparams("parallel", "arbitrary"),
        name="s5",
    )(ps3, ps3, x0r, x0i, lre, lim, wb, wc, d, wglu, bglu, jnp.zeros((1, S5_NSTATE // S5_SCAN_SLICES), jnp.int32))


def _outproj_rows(x, og_ref, om_ref, os_ref, wo_scr):
    acc = _dot(og_ref[...], wo_scr[0:GLA_WIDTH, :])
    acc += _dot(om_ref[...], wo_scr[GLA_WIDTH:GLA_WIDTH + MLA_WIDTH, :])
    acc += _dot(os_ref[...], wo_scr[GLA_WIDTH + MLA_WIDTH:, :])
    return x + acc


def _outproj_body(x_ref, og_ref, om_ref, os_ref, w_ref, g_ref, o_ref, wo_scr, *, final):
    @pl.when(pl.program_id(0) == 0)
    def _():
        wo_scr[...] = w_ref[...].astype(BF16)

    xn = _outproj_rows(x_ref[...], og_ref, om_ref, os_ref, wo_scr)
    if final:
        ms = jnp.mean(xn * xn, axis=-1, keepdims=True)
        xn = xn * lax.rsqrt(ms + EPS) * g_ref[...]
    o_ref[...] = xn


def _out_in_body(x_ref, og_ref, om_ref, os_ref, wo_ref, g_ref, wt_ref, *rest, with_kv, n_prev):
    prev, rest = rest[:n_prev], rest[n_prev:]
    n_in, n_out = N_MLA_IN[with_kv], N_MLA_OUT[with_kv]
    mla_in, (xo_ref, pg_ref, mz_ref, ps_ref), mla_out, (wo_scr, w_scr) = (
        rest[:n_in], rest[n_in:n_in + 4], rest[n_in + 4:n_in + 4 + n_out], rest[n_in + 4 + n_out:])
    if n_prev:
        ckv_all = mla_out[1]
        for j in range(n_prev):
            ckv_all[j] = prev[j][...]
        mla_out = (mla_out[0], ckv_all.at[n_prev]) + tuple(mla_out[2:])

    @pl.when(pl.program_id(0) == 0)
    def _():
        wo_scr[...] = wo_ref[...].astype(BF16)
        w_scr[...] = wt_ref[...].astype(BF16)

    xn = _outproj_rows(x_ref[...], og_ref, om_ref, os_ref, wo_scr)
    xo_ref[...] = xn
    _inproj_rows(xn, g_ref, w_scr, pg_ref, mz_ref, ps_ref, tuple(mla_in) + tuple(mla_out), with_kv)


def _out_in(x2, og, om, os_, p, cos_t, sin_t, layer, bsz, t, tm, with_kv, ckv_prev=()):
    n = x2.shape[0]
    row = lambda c: pl.BlockSpec((tm, c), lambda i: (i, 0))
    m_args, m_in, m_out, m_shape = _mla_specs(p, cos_t, sin_t, layer + 1, bsz, t, tm, with_kv)
    n_prev = len(ckv_prev)
    if n_prev:
        m_out[1] = pl.BlockSpec((n_prev + 1, tm, MLA_KV_RANK), lambda i: (0, i, 0))
        m_shape[1] = jax.ShapeDtypeStruct((n_prev + 1, n, MLA_KV_RANK), F32)
    o_specs, o_shape = _proj_out(tm, n)
    return pl.pallas_call(
        functools.partial(_out_in_body, with_kv=with_kv, n_prev=n_prev),
        grid=(n // tm,),
        in_specs=[row(D_MODEL), row(GLA_WIDTH), row(MLA_WIDTH), row(S5_WIDTH),
                  _layer_spec((D_MODEL, D_MODEL), layer, single_buffer=True),
                  _layer_spec((1, D_MODEL), layer + 1),
                  _layer_spec((IN_OFF['end'], D_MODEL), layer + 1, single_buffer=True)]
                 + [row(MLA_KV_RANK)] * n_prev + m_in,
        out_specs=[row(D_MODEL)] + o_specs + m_out,
        out_shape=[jax.ShapeDtypeStruct((n, D_MODEL), F32)] + o_shape + m_shape,
        scratch_shapes=[pltpu.VMEM((D_MODEL, D_MODEL), BF16), pltpu.VMEM((IN_OFF['end'], D_MODEL), BF16)],
        compiler_params=_cparams("arbitrary"),
        name="outproj_inproj",
    )(x2, og, om, os_, p['w_out'], p['ln'], p['w_in'], *ckv_prev, *m_args)


def _outproj(x2, og, om, os_, w, gain, layer, tm, final):
    n = x2.shape[0]
    row = lambda c: pl.BlockSpec((tm, c), lambda i: (i, 0))
    return pl.pallas_call(
        functools.partial(_outproj_body, final=final),
        grid=(n // tm,),
        in_specs=[row(D_MODEL), row(GLA_WIDTH), row(MLA_WIDTH), row(S5_WIDTH),
                  _layer_spec((D_MODEL, D_MODEL), layer, single_buffer=True),
                  _const_spec((1, D_MODEL))],
        out_specs=row(D_MODEL),
        out_shape=jax.ShapeDtypeStruct((n, D_MODEL), F32),
        scratch_shapes=[pltpu.VMEM((D_MODEL, D_MODEL), BF16)],
        compiler_params=_cparams("arbitrary"),
        name="outproj_final" if final else "outproj",
    )(x2, og, om, os_, w, gain)


def _prepare_params(ln_gain, w_in, gla_w_gate, gla_b_gate, gla_norm_gain, mla_q_norm_gain, mla_w_uq,
                    mla_kv_norm_gain, mla_w_ukv, s5_lambda_re, s5_lambda_im, s5_b_re, s5_b_im, s5_c_re, s5_c_im,
                    s5_d, s5_log_dt, s5_w_glu, s5_b_glu, w_out):
    depth = w_in.shape[0]
    w_t = jnp.swapaxes(w_in, 1, 2)
    wg = jnp.pad(gla_w_gate, ((0, 0), (0, GLA_QK - GLA_GATE_RANK), (0, 0))).astype(BF16)
    wq = mla_w_uq.reshape(depth, MLA_Q_RANK, MLA_HEADS, MLA_NOPE_DIM + MLA_ROPE_DIM)
    half = MLA_ROPE_DIM // 2
    wq = jnp.concatenate([wq, -wq[..., MLA_NOPE_DIM + half:], wq[..., MLA_NOPE_DIM:MLA_NOPE_DIM + half]], axis=-1)
    wq = jnp.pad(wq, ((0, 0), (0, 256 - MLA_Q_RANK), (0, 0), (0, 0)))
    wq = wq.reshape(depth, 256, MLA_QK_WIDTH).astype(BF16)
    gq = jnp.pad(mla_q_norm_gain, ((0, 0), (0, 256 - MLA_Q_RANK))).reshape(depth, 1, 256)
    wkv = mla_w_ukv.reshape(depth, MLA_KV_RANK, MLA_HEADS, MLA_NOPE_DIM + MLA_V_DIM)
    wk = jnp.pad(wkv[..., :MLA_NOPE_DIM], ((0, 0), (0, 0), (0, 0), (0, MLA_QK_PAD - MLA_NOPE_DIM)))
    wkv_r = jnp.concatenate([wk.reshape(depth, MLA_KV_RANK, MLA_QK_WIDTH),
                             wkv[..., MLA_NOPE_DIM:].reshape(depth, MLA_KV_RANK, MLA_WIDTH)], axis=2).astype(BF16)
    wka = jnp.pad(jnp.transpose(wkv[..., :MLA_NOPE_DIM], (0, 2, 3, 1)),
                  ((0, 0), (0, 0), (0, MLA_QK_PAD - MLA_NOPE_DIM), (0, 0))).astype(BF16)
    wv = jnp.transpose(wkv[..., MLA_NOPE_DIM:], (0, 2, 1, 3)).astype(BF16)
    dt = jnp.exp(s5_log_dt)[:, :, None]
    mag = jnp.exp(s5_lambda_re * dt)
    lbr, lbi = mag * jnp.cos(s5_lambda_im * dt), mag * jnp.sin(s5_lambda_im * dt)
    den = s5_lambda_re * s5_lambda_re + s5_lambda_im * s5_lambda_im
    qr = ((lbr - 1.0) * s5_lambda_re + lbi * s5_lambda_im) / den
    qi = (lbi * s5_lambda_re - (lbr - 1.0) * s5_lambda_im) / den
    b_re_t, b_im_t = jnp.swapaxes(s5_b_re, 2, 3), jnp.swapaxes(s5_b_im, 2, 3)
    bbr = qr[:, :, None, :] * b_re_t - qi[:, :, None, :] * b_im_t
    bbi = qr[:, :, None, :] * b_im_t + qi[:, :, None, :] * b_re_t
    rows = lambda m: m.reshape(depth, S5_WIDTH, S5_STATE)
    wb, wc = _s5_weights(rows(bbr), rows(bbi), rows(s5_c_re), rows(s5_c_im))
    return dict(
        ln=ln_gain.reshape(depth, 1, D_MODEL), w_in=w_t, wg=wg, bg=gla_b_gate.reshape(depth, 1, GLA_QK),
        gla_gain=jnp.tile(gla_norm_gain, (1, GLA_HEADS)).reshape(depth, 1, GLA_WIDTH),
        gq=gq, wq=wq, gkv=mla_kv_norm_gain.reshape(depth, 1, MLA_KV_RANK), wkv=wkv_r, wka=wka, wv=wv,
        lre=lbr.reshape(depth, 1, S5_NSTATE), lim=lbi.reshape(depth, 1, S5_NSTATE),
        wb=wb, wc=wc, d=s5_d.reshape(depth, 1, S5_WIDTH), wglu=s5_w_glu.astype(BF16),
        bglu=s5_b_glu.reshape(depth, 1, S5_WIDTH), w_out=w_out)


def _rope_tables(past, t, reps):
    half = MLA_ROPE_DIM // 2
    inv = ROPE_BASE ** (-np.arange(half, dtype=np.float64) / half)
    ang = (past + np.arange(t, dtype=np.float64))[:, None] * inv[None, :]
    cos, sin = np.cos(ang), np.sin(ang)
    pad = MLA_QK_PAD - MLA_NOPE_DIM - MLA_ROPE_DIM
    cos_t = np.concatenate([np.ones((t, MLA_NOPE_DIM)), cos, cos, np.zeros((t, pad))], axis=1)
    sin_t = np.concatenate([np.zeros((t, MLA_NOPE_DIM)), sin, sin, np.zeros((t, pad))], axis=1)
    return jnp.asarray(np.tile(cos_t, (reps, 1)), F32), jnp.asarray(np.tile(sin_t, (reps, 1)), F32)


def _trunk(x, gla_state, ckv_cache, kpe_cache, s5_re, s5_im, p, final_gain):
    bsz, t, _ = x.shape
    n = bsz * t
    depth = p['w_in'].shape[0]
    past = 0 if ckv_cache is None else ckv_cache.shape[2]
    s_len = past + t
    tl = _tiles(bsz, t, past)
    cos_t, sin_t = _rope_tables(past, t, max(1, max(tl['row'], tl['in_row']) // t))
    kpet_cache = None if kpe_cache is None else jnp.swapaxes(kpe_cache, 2, 3)
    x2 = x.reshape(n, D_MODEL)
    gain_f = final_gain.reshape(1, D_MODEL)
    gla_o, ckv_o, kpe_o, re_o, im_o = [], [], [], [], []
    with_kv = past == 0
    proj = _inproj(x2, p, cos_t, sin_t, 0, bsz, t, tl['in_row'], with_kv)
    ckv_stack = None
    for l in range(depth):
        pg, mz, ps, q, ckv_new = proj[:5]
        if ckv_new.ndim == 3:
            ckv_stack, ckv_new = ckv_new, ckv_new[l]
        s0 = jnp.zeros((bsz, GLA_HEADS, GLA_DK, GLA_DV), F32) if gla_state is None else gla_state[l]
        o_gla, s_new = _gla(pg.reshape(bsz, t, PG_COLS), p['wg'], p['bg'], p['gla_gain'], l, s0, bsz, t, tl)
        o_gla = o_gla.reshape(n, GLA_WIDTH)
        gla_o.append(s_new)
        if with_kv:
            kpet, k_cat, v_all = proj[5:]
            kpe_o.append(jnp.swapaxes(kpet, 1, 2))
            o_mla = _attn(q, k_cat, v_all, mz, bsz, t, s_len, past, tl['attn_q'], tl['attn_k'])
        else:
            kpe128 = proj[5]
            kpe_o.append(kpe128[:, MLA_NOPE_DIM:MLA_NOPE_DIM + MLA_ROPE_DIM].reshape(bsz, t, MLA_ROPE_DIM))
            o_mla = _attn_cached(q, mz, ckv_new, kpe128, ckv_cache, kpet_cache, p['wka'], p['wv'], l, bsz, t,
                                 tl['cached_seqs'])
        ckv_o.append(ckv_new)
        x0r = jnp.zeros((bsz, S5_NSTATE), F32) if s5_re is None else s5_re[l].reshape(bsz, S5_NSTATE)
        x0i = jnp.zeros((bsz, S5_NSTATE), F32) if s5_im is None else s5_im[l].reshape(bsz, S5_NSTATE)
        o_s5, xr, xi = _s5(ps.reshape(bsz, t, PS_COLS), x0r, x0i, p['lre'], p['lim'], p['wb'], p['wc'],
                           p['d'], p['wglu'], p['bglu'], l, bsz, t, tl['s5_rows'])
        re_o.append(xr.reshape(bsz, S5_GROUPS, S5_STATE))
        im_o.append(xi.reshape(bsz, S5_GROUPS, S5_STATE))
        o_s5 = o_s5.reshape(n, S5_WIDTH)
        if l < depth - 1:
            stack_here = l + 1 == depth - 1
            x2, *proj = _out_in(x2, o_gla, o_mla, o_s5, p, cos_t, sin_t, l, bsz, t, tl['row'], with_kv,
                                ckv_prev=tuple(ckv_o) if stack_here else ())
        else:
            x2 = _outproj(x2, o_gla, o_mla, o_s5, p['w_out'], gain_f, l, tl['out_row'], final=True)
    ckv_all = jnp.stack(ckv_o) if ckv_stack is None else ckv_stack
    return (x2.reshape(bsz, t, D_MODEL), jnp.stack(gla_o), ckv_all.reshape(depth, bsz, t, MLA_KV_RANK),
            jnp.stack(kpe_o), jnp.stack(re_o), jnp.stack(im_o))


def kernel(x_prompt, x_sample, state_gla, cache_mla_ckv, cache_mla_kpe, state_s5_re, state_s5_im, ln_gain, w_in, gla_w_gate, gla_b_gate, gla_norm_gain, mla_q_norm_gain, mla_w_uq, mla_kv_norm_gain, mla_w_ukv, s5_lambda_re, s5_lambda_im, s5_b_re, s5_b_im, s5_c_re, s5_c_im, s5_d, s5_log_dt, s5_w_glu, s5_b_glu, w_out, final_gain):
    p = _prepare_params(ln_gain, w_in, gla_w_gate, gla_b_gate, gla_norm_gain, mla_q_norm_gain, mla_w_uq,
                        mla_kv_norm_gain, mla_w_ukv, s5_lambda_re, s5_lambda_im, s5_b_re, s5_b_im,
                        s5_c_re, s5_c_im, s5_d, s5_log_dt, s5_w_glu, s5_b_glu, w_out)
    y_p, gla_p, ckv_p, kpe_p, re_p, im_p = _trunk(x_prompt, None, None, None, None, None, p, final_gain)
    y_s, gla_s, ckv_s, kpe_s, re_s, im_s = _trunk(x_sample, state_gla, cache_mla_ckv, cache_mla_kpe,
                                                  state_s5_re, state_s5_im, p, final_gain)
    return (y_p, y_s, gla_p, ckv_p, kpe_p, re_p, im_p, gla_s, ckv_s, kpe_s, re_s, im_s)
```

```python
import functools
import math

import numpy as np
import jax
import jax.numpy as jnp
from jax import lax
from jax.experimental import pallas as pl
from jax.experimental.pallas import tpu as pltpu

F32 = jnp.float32
BF16 = jnp.bfloat16

LANES = 128
D_MODEL = 1024
CHUNK = 64
EPS = 1e-6
GLA_HEADS = 4
GLA_DV = 64
GLA_DK = 32
GLA_WIDTH = GLA_HEADS * GLA_DV
GLA_QK = GLA_HEADS * GLA_DK
GLA_GATE_RANK = 16
GLA_GATE_TAU = 16.0
GLA_SUB = 16
GLA_PLAIN_MAX_DECAY = 60.0
MLA_HEADS = 4
MLA_NOPE_DIM = 64
MLA_ROPE_DIM = 32
MLA_V_DIM = 128
MLA_Q_RANK = 192
MLA_KV_RANK = 128
MLA_WIDTH = MLA_HEADS * MLA_V_DIM
MLA_QK_PAD = 128
MLA_QK_WIDTH = MLA_HEADS * MLA_QK_PAD
ROPE_BASE = 10000.0
S5_GROUPS = 16
S5_GROUP_CH = 16
S5_STATE = 64
S5_WIDTH = S5_GROUPS * S5_GROUP_CH
S5_NSTATE = S5_GROUPS * S5_STATE
S5_BATCH_TILE = 8
S5_SCAN_SLICES = 4
GELU_TANH_SCALE = math.sqrt(2.0 / math.pi)
GELU_TANH_CUBIC = 0.044715

PG_COLS = 896
PS_COLS = 512
_IN_SEGS = (('g_q', GLA_QK), ('g_k', GLA_QK), ('g_v', GLA_WIDTH), ('g_lr', GLA_GATE_RANK), ('g_z', GLA_WIDTH),
            ('m_cq', MLA_Q_RANK), ('m_ckv', MLA_KV_RANK), ('m_kr', MLA_ROPE_DIM), ('m_z', MLA_WIDTH),
            ('s_u', S5_WIDTH), ('s_z', S5_WIDTH), ('end', 0))
PG_Q, PG_K, PG_V, PG_Z, PG_LR = slice(0, 128), slice(128, 256), slice(256, 512), slice(512, 768), slice(768, 896)
PG_QKV = slice(PG_Q.start, PG_V.stop)
IN_OFF = dict(zip([n for n, _ in _IN_SEGS], np.cumsum([0] + [w for _, w in _IN_SEGS[:-1]]).tolist()))

VMEM_LIMIT_BYTES = 48 * 1024 * 1024
X_RING = 3


def _tiles(bsz, t, past):
    n = bsz * t
    s_len = past + t
    return dict(
        row=min(512, n),
        in_row=min(1024, n) if t >= 1024 or t < 512 else 512,
        out_row=min(1024, n),
        cached_seqs=math.gcd(bsz, max(1, 128 // t)),
        gla_rows=min(512, t), gla_chunk=min(CHUNK, t),
        gla_seqs=math.gcd(bsz, max(8, 128 // t)),
        attn_q=min(512, t), attn_k=min(512, s_len),
        s5_rows=min(64, t))


def _cparams(*sem):
    return pltpu.CompilerParams(dimension_semantics=sem, vmem_limit_bytes=VMEM_LIMIT_BYTES)


def _sigmoid(x):
    return 0.5 * (1.0 + jnp.tanh(0.5 * x))


def _dot(a, b):
    return jnp.dot(a, b, preferred_element_type=F32)


def _dot_t(a, b):
    return lax.dot_general(a, b, (((0,), (0,)), ((), ())), preferred_element_type=F32)


def _const_spec(shape):
    zeros = (0,) * len(shape)
    return pl.BlockSpec(shape, lambda *_: zeros)


def _layer_spec(shape, layer, single_buffer=False):
    zeros = (0,) * len(shape)
    mode = dict(pipeline_mode=pl.Buffered(1)) if single_buffer else {}
    return pl.BlockSpec((None,) + tuple(shape), lambda *_: (layer,) + zeros, **mode)


def _rope128(x, cos_t, sin_t):
    lane = lax.broadcasted_iota(jnp.int32, x.shape, 1)
    first_half = (lane >= MLA_NOPE_DIM) & (lane < MLA_NOPE_DIM + MLA_ROPE_DIM // 2)
    rot = jnp.where(first_half, -pltpu.roll(x, LANES - MLA_ROPE_DIM // 2, 1), pltpu.roll(x, MLA_ROPE_DIM // 2, 1))
    return x * cos_t + rot * sin_t


def _mla_prep_rows(cq, ckv, kr, mla, with_kv):
    if with_kv:
        cos_ref, sin_ref, gq_ref, wq_ref, gkv_ref, wkv_ref, q_ref, ckv_ref, kpet_ref, k_ref, v_ref = mla
    else:
        cos_ref, sin_ref, gq_ref, wq_ref, gkv_ref, q_ref, ckv_ref, kpe_ref = mla
    cos_t = cos_ref[...]
    sin_t = sin_ref[...]
    ones = jnp.ones((2 * LANES, LANES), BF16)
    ms = _dot((cq * cq).astype(BF16), ones) * (1.0 / MLA_Q_RANK)
    inv = lax.rsqrt(ms + EPS)
    cqn = (cq * jnp.concatenate([inv, inv], axis=1) * gq_ref[...]).astype(BF16)
    qh = _dot(cqn, wq_ref[...])
    scale = (MLA_NOPE_DIM + MLA_ROPE_DIM) ** -0.5 * math.log2(math.e)
    for h in range(MLA_HEADS):
        x = qh[:, h * MLA_QK_PAD:(h + 1) * MLA_QK_PAD]
        roped = x * cos_t + pltpu.roll(x, LANES - MLA_ROPE_DIM, 1) * sin_t
        q_ref[:, h * MLA_QK_PAD:(h + 1) * MLA_QK_PAD] = (roped * scale).astype(BF16)
    ms = _dot((ckv * ckv).astype(BF16), ones[0:LANES]) * (1.0 / MLA_KV_RANK)
    ckv_n = ckv * lax.rsqrt(ms + EPS) * gkv_ref[...]
    ckv_ref[...] = ckv_n
    kpe128 = _rope128(kr, cos_t, sin_t)
    if with_kv:
        kpet_ref[...] = jnp.transpose(kpe128)[MLA_NOPE_DIM:MLA_NOPE_DIM + MLA_ROPE_DIM, :]
        kv = _dot(ckv_n.astype(BF16), wkv_ref[...])
        for h in range(MLA_HEADS):
            sl = slice(h * MLA_QK_PAD, (h + 1) * MLA_QK_PAD)
            k_ref[:, sl] = (kv[:, sl] + kpe128).astype(BF16)
        v_ref[...] = kv[:, MLA_QK_WIDTH:].astype(BF16)
    else:
        kpe_ref[...] = kpe128


def _inproj_rows(x, g_ref, w_scr, og_ref, mz_ref, os_ref, mla, with_kv):
    ms = jnp.mean(x * x, axis=-1, keepdims=True)
    h = (x * lax.rsqrt(ms + EPS) * g_ref[...]).astype(BF16)

    def seg(a, b):
        return lax.dot_general(h, w_scr[a:b, :], (((1,), (1,)), ((), ())), preferred_element_type=F32)

    lane = lax.broadcasted_iota(jnp.int32, (x.shape[0], LANES), 1)
    lane2 = lax.broadcasted_iota(jnp.int32, (x.shape[0], 2 * LANES), 1)
    c = IN_OFF
    og_ref[:, PG_QKV] = seg(c['g_q'], c['g_lr']).astype(BF16)
    og_ref[:, PG_Z] = seg(c['g_z'], c['m_cq']).astype(BF16)
    og_ref[:, PG_LR] = jnp.where(lane < GLA_GATE_RANK, seg(c['g_lr'], c['g_lr'] + LANES), 0.0).astype(BF16)
    mz_ref[...] = seg(c['m_z'], c['s_u']).astype(BF16)
    os_ref[...] = seg(c['s_u'], c['end']).astype(BF16)
    cq = jnp.where(lane2 < MLA_Q_RANK, seg(c['m_cq'], c['m_cq'] + 2 * LANES), 0.0)
    ckv_kr = seg(c['m_ckv'], c['m_ckv'] + 2 * LANES)
    kr = pltpu.roll(ckv_kr[:, LANES:], MLA_NOPE_DIM, 1)
    kr = jnp.where((lane >= MLA_NOPE_DIM) & (lane < MLA_NOPE_DIM + MLA_ROPE_DIM), kr, 0.0)
    _mla_prep_rows(cq, ckv_kr[:, 0:LANES], kr, mla, with_kv)


N_MLA_IN = {True: 6, False: 5}
N_MLA_OUT = {True: 5, False: 3}


def _inproj_body(x_ref, g_ref, wt_ref, *rest, with_kv):
    n_in, n_out = N_MLA_IN[with_kv], N_MLA_OUT[with_kv]
    mla_in, (og_ref, mz_ref, os_ref), mla_out, (w_scr,) = (
        rest[:n_in], rest[n_in:n_in + 3], rest[n_in + 3:n_in + 3 + n_out], rest[n_in + 3 + n_out:])

    @pl.when(pl.program_id(0) == 0)
    def _():
        w_scr[...] = wt_ref[...].astype(BF16)

    _inproj_rows(x_ref[...], g_ref, w_scr, og_ref, mz_ref, os_ref, tuple(mla_in) + tuple(mla_out), with_kv)


def _mla_specs(p, cos_t, sin_t, layer, bsz, t, tm, with_kv):
    n = bsz * t
    ntab = max(1, t // tm)
    assert not with_kv or tm <= t
    row = lambda c: pl.BlockSpec((tm, c), lambda i: (i, 0))
    table = pl.BlockSpec((tm, LANES), lambda i: (i % ntab, 0))
    args = [cos_t, sin_t, p['gq'], p['wq'], p['gkv']]
    in_specs = [table, table, _layer_spec((1, 256), layer), _layer_spec((256, MLA_QK_WIDTH), layer),
                _layer_spec((1, MLA_KV_RANK), layer)]
    out_specs = [row(MLA_QK_WIDTH), row(MLA_KV_RANK)]
    out_shape = [jax.ShapeDtypeStruct((n, MLA_QK_WIDTH), BF16), jax.ShapeDtypeStruct((n, MLA_KV_RANK), F32)]
    if with_kv:
        args.append(p['wkv'])
        in_specs.append(_layer_spec((MLA_KV_RANK, MLA_QK_WIDTH + MLA_WIDTH), layer))
        out_specs += [pl.BlockSpec((None, MLA_ROPE_DIM, tm), lambda i: (i // ntab, 0, i % ntab)),
                      row(MLA_QK_WIDTH), row(MLA_WIDTH)]
        out_shape += [jax.ShapeDtypeStruct((bsz, MLA_ROPE_DIM, t), F32),
                      jax.ShapeDtypeStruct((n, MLA_QK_WIDTH), BF16), jax.ShapeDtypeStruct((n, MLA_WIDTH), BF16)]
    else:
        out_specs.append(row(LANES))
        out_shape.append(jax.ShapeDtypeStruct((n, LANES), F32))
    return args, in_specs, out_specs, out_shape


def _proj_out(tm, n):
    row = lambda c: pl.BlockSpec((tm, c), lambda i: (i, 0))
    return ([row(PG_COLS), row(MLA_WIDTH), row(PS_COLS)],
            [jax.ShapeDtypeStruct((n, PG_COLS), BF16), jax.ShapeDtypeStruct((n, MLA_WIDTH), BF16),
             jax.ShapeDtypeStruct((n, PS_COLS), BF16)])


def _inproj(x2, p, cos_t, sin_t, layer, bsz, t, tm, with_kv):
    n = x2.shape[0]
    m_args, m_in, m_out, m_shape = _mla_specs(p, cos_t, sin_t, layer, bsz, t, tm, with_kv)
    o_specs, o_shape = _proj_out(tm, n)
    return pl.pallas_call(
        functools.partial(_inproj_body, with_kv=with_kv),
        grid=(n // tm,),
        in_specs=[pl.BlockSpec((tm, D_MODEL), lambda i: (i, 0)),
                  _layer_spec((1, D_MODEL), layer),
                  _layer_spec((IN_OFF['end'], D_MODEL), layer, single_buffer=True)] + m_in,
        scratch_shapes=[pltpu.VMEM((IN_OFF['end'], D_MODEL), BF16)],
        out_specs=o_specs + m_out,
        out_shape=o_shape + m_shape,
        compiler_params=_cparams("arbitrary"),
        name="inproj",
    )(x2, p['ln'], p['w_in'], *m_args)


def _gla_body(p_ref, wg_ref, bg_ref, gain_ref, ones_k_ref, ones_v_ref, bd_ref, s0_ref,
              o_ref, sout_ref, q_scr, k_scr, v_scr, b_scr, s_scr, qe_scr, qs_scr, ke_scr, vb_scr, o_scr,
              *, chunk, nch, nseq):
    it = pl.program_id(1)
    sub = min(GLA_SUB, chunk)
    nsub = chunk // sub

    @pl.when(it == 0)
    def _():
        s_scr[...] = jnp.zeros(s_scr.shape, F32)
        for sq in range(nseq):
            for h in range(GLA_HEADS):
                s_scr[sq, h * GLA_DK:(h + 1) * GLA_DK, h * GLA_DV:(h + 1) * GLA_DV] = s0_ref[sq, h]

    def cols(lanes):
        parts = [p_ref[sq, :, lanes] for sq in range(nseq)]
        return parts[0] if nseq == 1 else jnp.concatenate(parts, axis=0)

    q_scr[...] = cols(PG_Q).astype(F32) * (GLA_DK ** -0.5)
    k_scr[...] = cols(PG_K).astype(F32)
    v_scr[...] = cols(PG_V).astype(F32)
    logit = _dot(cols(PG_LR), wg_ref[...]) + bg_ref[...]
    log_a = (jnp.minimum(logit, 0.0) - jnp.log(1.0 + jnp.exp(-jnp.abs(logit)))) * (1.0 / GLA_GATE_TAU)
    row_in_chunk = lax.broadcasted_iota(jnp.int32, log_a.shape, 0) % chunk
    b_all = log_a
    shift = 1
    while shift < chunk:
        b_all = b_all + jnp.where(row_in_chunk >= shift, pltpu.roll(b_all, shift, 0), 0.0)
        shift *= 2
    b_scr[...] = b_all

    row = lax.broadcasted_iota(jnp.int32, (chunk, GLA_QK), 0)
    row_in_sub = lax.broadcasted_iota(jnp.int32, (sub, GLA_QK), 0)

    def decay_columns(b_end):
        col = jnp.transpose(jnp.broadcast_to(jnp.exp(b_end), (GLA_QK, GLA_QK)))
        return jnp.concatenate([col, col], axis=1)

    def load_chunk(sq, c):
        r0 = pl.multiple_of((sq * nch + c) * chunk, chunk)
        return (r0, q_scr[pl.ds(r0, chunk), :], k_scr[pl.ds(r0, chunk), :], v_scr[pl.ds(r0, chunk), :],
                b_scr[pl.ds(r0, chunk), :], b_scr[pl.ds(r0 + chunk - 1, 1), :], s_scr[sq])

    def finish_chunk(sq, r0, o, s_prev, a_state, b_end):
        ms = _dot((o * o).astype(BF16), ones_v_ref[...]) * (1.0 / GLA_DV)
        o_n = o * lax.rsqrt(ms + EPS) * gain_ref[...]
        t0 = r0 - sq * nch * chunk
        z = p_ref[sq, pl.ds(t0, chunk), PG_Z].astype(F32)
        o_ref[sq, pl.ds(t0, chunk), :] = (o_n * (z * _sigmoid(z))).astype(BF16)
        s_scr[sq] = s_prev * decay_columns(b_end) + a_state

    def robust_chunk(c, carry, sq):
        r0, qc, kc, vc, bc, b_end, s_prev = load_chunk(sq, c)
        xs = [qc * jnp.exp(bc)]
        ks = []
        for sj in range(nsub - 1):
            e_j = b_scr[pl.ds(r0 + (sj + 1) * sub - 1, 1), :]
            later = row >= (sj + 1) * sub
            xs.append(jnp.where(later, qc * jnp.exp(jnp.where(later, bc - e_j, 0.0)), 0.0))
            own = (row >= sj * sub) & (row < (sj + 1) * sub)
            ks.append(jnp.where(own, kc * jnp.exp(jnp.where(own, e_j - bc, 0.0)), 0.0))
        ks.append(kc * jnp.exp(b_end - bc))
        k_all = jnp.concatenate(ks, axis=1).astype(BF16)
        a_all = _dot_t(k_all, vc.astype(BF16)) * bd_ref[...]
        w = jnp.concatenate([s_prev, a_all[:(nsub - 1) * GLA_QK]], axis=0).astype(BF16) if nsub > 1 \
            else s_prev.astype(BF16)
        o_off = _dot(jnp.concatenate(xs, axis=1).astype(BF16), w)
        rows = []
        for si in range(nsub):
            q_i = qc[si * sub:(si + 1) * sub]
            b_i = bc[si * sub:(si + 1) * sub]
            es = []
            for j in range(sub):
                r = r0 + si * sub + j
                b_j = b_scr[pl.ds(r, 1), :]
                k_j = k_scr[pl.ds(r, 1), :]
                valid = row_in_sub >= j
                es.append(q_i * k_j * jnp.exp(jnp.where(valid, b_i - b_j, -jnp.inf)))
            e_all = jnp.concatenate(es, axis=0)
            e_hi = e_all.astype(BF16)
            e_lo = (e_all - e_hi.astype(F32)).astype(BF16)
            p_all = _dot(e_hi, ones_k_ref[...]) + _dot(e_lo, ones_k_ref[...])
            acc = o_off[si * sub:(si + 1) * sub]
            for j in range(sub):
                v_j = v_scr[pl.ds(r0 + si * sub + j, 1), :]
                acc = acc + p_all[j * sub:(j + 1) * sub] * v_j
            rows.append(acc)
        o = rows[0] if nsub == 1 else jnp.concatenate(rows, axis=0)
        finish_chunk(sq, r0, o, s_prev, a_all[(nsub - 1) * GLA_QK:], b_end)
        return carry

    lane_head_v = lax.broadcasted_iota(jnp.int32, (chunk, GLA_WIDTH), 1) // GLA_DV
    causal = (lax.broadcasted_iota(jnp.int32, (GLA_HEADS * chunk, chunk), 0) % chunk
              >= lax.broadcasted_iota(jnp.int32, (GLA_HEADS * chunk, chunk), 1))

    def plain_block():
        tt = nseq * nch * chunk
        b_all = b_scr[...]
        q_all = q_scr[...]
        k_all = k_scr[...]
        qe = q_all * jnp.exp(b_all)
        lane_head = lax.broadcasted_iota(jnp.int32, (tt, GLA_QK), 1) // GLA_DK
        qe_scr[...] = qe.astype(BF16)
        for h in range(GLA_HEADS):
            qs_scr[h] = jnp.where(lane_head == h, qe, 0.0).astype(BF16)
        ke_scr[...] = (k_all * jnp.exp(-b_all)).astype(BF16)
        vb_scr[...] = cols(PG_V)
        states = [s_scr[sq] for sq in range(nseq)]
        for c, sq in [(c, sq) for c in range(nch) for sq in range(nseq)]:
            r0 = (sq * nch + c) * chunk
            rs = slice(r0, r0 + chunk)
            s_cur = states[sq]
            qs = jnp.concatenate([qs_scr[h, rs, :] for h in range(GLA_HEADS)], axis=0)
            s = lax.dot_general(qs, ke_scr[rs, :], (((1,), (1,)), ((), ())), preferred_element_type=F32)
            s = jnp.where(causal, s, 0.0).astype(BF16)
            r = _dot(s, vb_scr[rs, :])
            o = _dot(qe_scr[rs, :], s_cur.astype(BF16))
            for h in range(GLA_HEADS):
                o = o + jnp.where(lane_head_v == h, r[h * chunk:(h + 1) * chunk], 0.0)
            o_scr[rs, :] = o
            b_end = b_scr[r0 + chunk - 1:r0 + chunk, :]
            k_end = (k_scr[rs, :] * jnp.exp(b_end - b_scr[rs, :])).astype(BF16)
            a_state = _dot_t(k_end, vb_scr[rs, :]) * bd_ref[0:GLA_QK, :]
            states[sq] = s_cur * decay_columns(b_end) + a_state
        for sq in range(nseq):
            s_scr[sq] = states[sq]
        o = o_scr[...]
        ms = _dot((o * o).astype(BF16), ones_v_ref[...]) * (1.0 / GLA_DV)
        o_n = o * lax.rsqrt(ms + EPS) * gain_ref[...]
        z = cols(PG_Z).astype(F32)
        o_all = (o_n * (z * _sigmoid(z))).astype(BF16)
        for sq in range(nseq):
            o_ref[sq] = o_all[sq * nch * chunk:(sq + 1) * nch * chunk]

    in_range = jnp.max(-b_scr[...]) < GLA_PLAIN_MAX_DECAY

    @pl.when(in_range)
    def _():
        plain_block()

    @pl.when(jnp.logical_not(in_range))
    def _():
        for sq in range(nseq):
            lax.fori_loop(0, nch, functools.partial(robust_chunk, sq=sq), 0)

    @pl.when(it == pl.num_programs(1) - 1)
    def _():
        for sq in range(nseq):
            for h in range(GLA_HEADS):
                sout_ref[sq, h] = s_scr[sq, h * GLA_DK:(h + 1) * GLA_DK, h * GLA_DV:(h + 1) * GLA_DV]


def _gla_consts(chunk):
    nsub = chunk // min(GLA_SUB, chunk)
    hk = np.arange(GLA_QK) // GLA_DK
    hv = np.arange(GLA_WIDTH) // GLA_DV
    same_kv = (hk[:, None] == hv[None, :]).astype(np.float32)
    same_vv = (hv[:, None] == hv[None, :]).astype(np.float32)
    return (jnp.asarray(same_kv, BF16), jnp.asarray(same_vv, BF16),
            jnp.asarray(np.tile(same_kv, (nsub, 1)), F32))


def _gla(pg, wg, bg, gain, layer, s0, bsz, t, tl):
    chunk, tt, nseq = tl['gla_chunk'], tl['gla_rows'], tl['gla_seqs']
    nch = tt // chunk
    nt = t // tt
    rows = nseq * tt
    nsub = chunk // min(GLA_SUB, chunk)
    ones_k, ones_v, bd = _gla_consts(chunk)
    const = _const_spec
    return pl.pallas_call(
        functools.partial(_gla_body, chunk=chunk, nch=nch, nseq=nseq),
        grid=(bsz // nseq, nt),
        in_specs=[pl.BlockSpec((nseq, tt, PG_COLS), lambda b, i: (b, i, 0)),
                  _layer_spec((GLA_QK, GLA_QK), layer), _layer_spec((1, GLA_QK), layer),
                  _layer_spec((1, GLA_WIDTH), layer),
                  const((GLA_QK, GLA_WIDTH)), const((GLA_WIDTH, GLA_WIDTH)),
                  const((nsub * GLA_QK, GLA_WIDTH)),
                  pl.BlockSpec((nseq, GLA_HEADS, GLA_DK, GLA_DV), lambda b, i: (b, 0, 0, 0))],
        out_specs=[pl.BlockSpec((nseq, tt, GLA_WIDTH), lambda b, i: (b, i, 0)),
                   pl.BlockSpec((nseq, GLA_HEADS, GLA_DK, GLA_DV), lambda b, i: (b, 0, 0, 0))],
        out_shape=[jax.ShapeDtypeStruct((bsz, t, GLA_WIDTH), BF16),
                   jax.ShapeDtypeStruct((bsz, GLA_HEADS, GLA_DK, GLA_DV), F32)],
        scratch_shapes=[pltpu.VMEM((rows, GLA_QK), F32), pltpu.VMEM((rows, GLA_QK), F32),
                        pltpu.VMEM((rows, GLA_WIDTH), F32), pltpu.VMEM((rows, GLA_QK), F32),
                        pltpu.VMEM((nseq, GLA_QK, GLA_WIDTH), F32),
                        pltpu.VMEM((rows, GLA_QK), BF16), pltpu.VMEM((GLA_HEADS, rows, GLA_QK), BF16),
                        pltpu.VMEM((rows, GLA_QK), BF16),
                        pltpu.VMEM((rows, GLA_WIDTH), BF16), pltpu.VMEM((rows, GLA_WIDTH), F32)],
        compiler_params=_cparams("parallel", "arbitrary"),
        name="gla",
    )(pg, wg, bg, gain, ones_k, ones_v, bd, s0)


def _attn_cached_body(q_ref, z_ref, ckv_new_ref, kpe_new_ref, ckv_past_ref, kpet_past_ref, wka_ref, wv_ref,
                      o_ref, kpet_scr, *, past, t, nseq):
    last = (((1,), (1,)), ((), ()))
    hsl = [slice(h * MLA_QK_PAD, (h + 1) * MLA_QK_PAD) for h in range(MLA_HEADS)]
    q_lat_h = [_dot(q_ref[:, hsl[h]], wka_ref[h]).astype(BF16) for h in range(MLA_HEADS)]
    kpet_scr[...] = jnp.zeros(kpet_scr.shape, BF16)
    o_lat = []
    for sq in range(nseq):
        rs = slice(sq * t, (sq + 1) * t)
        q_rows = jnp.concatenate([q_ref[rs, hsl[h]] for h in range(MLA_HEADS)], axis=0)
        q_lat = jnp.concatenate([q_lat_h[h][rs] for h in range(MLA_HEADS)], axis=0)
        c_past = ckv_past_ref[sq].astype(BF16)
        c_new = ckv_new_ref[rs, :].astype(BF16)
        kpet_scr[sq, MLA_NOPE_DIM:MLA_NOPE_DIM + MLA_ROPE_DIM, :] = kpet_past_ref[sq].astype(BF16)
        s_past = (lax.dot_general(q_lat, c_past, last, preferred_element_type=F32)
                  + _dot(q_rows, kpet_scr[sq]))
        s_new = (lax.dot_general(q_lat, c_new, last, preferred_element_type=F32)
                 + lax.dot_general(q_rows, kpe_new_ref[rs, :].astype(BF16), last, preferred_element_type=F32))
        if past // CHUNK != (past + t - 1) // CHUNK:
            q_chunk = (past + lax.broadcasted_iota(jnp.int32, s_past.shape, 0) % t) // CHUNK
            s_past = jnp.where(lax.broadcasted_iota(jnp.int32, s_past.shape, 1) // CHUNK <= q_chunk, s_past, -jnp.inf)
            q_chunk = (past + lax.broadcasted_iota(jnp.int32, s_new.shape, 0) % t) // CHUNK
            s_new = jnp.where((past + lax.broadcasted_iota(jnp.int32, s_new.shape, 1)) // CHUNK <= q_chunk,
                              s_new, -jnp.inf)
        m = jnp.maximum(jnp.max(s_past, axis=-1, keepdims=True), jnp.max(s_new, axis=-1, keepdims=True))
        p_past = jnp.exp2(s_past - m)
        p_new = jnp.exp2(s_new - m)
        l = jnp.sum(p_past, axis=-1, keepdims=True) + jnp.sum(p_new, axis=-1, keepdims=True)
        o_lat.append(((_dot(p_past.astype(BF16), c_past) + _dot(p_new.astype(BF16), c_new)) / l).astype(BF16))
    for h in range(MLA_HEADS):
        vsl = slice(h * MLA_V_DIM, (h + 1) * MLA_V_DIM)
        o_h = jnp.concatenate([o_lat[sq][h * t:(h + 1) * t] for sq in range(nseq)], axis=0)
        z = z_ref[:, vsl].astype(F32)
        o_ref[:, vsl] = (_dot(o_h, wv_ref[h]) * (z * _sigmoid(z))).astype(BF16)


def _attn_cached(q, mz, ckv_new, kpe128, ckv_past, kpet_past, wka, wv, layer, bsz, t, nseq):
    past = ckv_past.shape[2]
    rows = nseq * t
    return pl.pallas_call(
        functools.partial(_attn_cached_body, past=past, t=t, nseq=nseq),
        grid=(bsz // nseq,),
        in_specs=[pl.BlockSpec((rows, MLA_QK_WIDTH), lambda b: (b, 0)),
                  pl.BlockSpec((rows, MLA_WIDTH), lambda b: (b, 0)),
                  pl.BlockSpec((rows, MLA_KV_RANK), lambda b: (b, 0)),
                  pl.BlockSpec((rows, LANES), lambda b: (b, 0)),
                  pl.BlockSpec((None, nseq, past, MLA_KV_RANK), lambda b: (layer, b, 0, 0)),
                  pl.BlockSpec((None, nseq, MLA_ROPE_DIM, past), lambda b: (layer, b, 0, 0)),
                  _layer_spec((MLA_HEADS, MLA_QK_PAD, MLA_KV_RANK), layer),
                  _layer_spec((MLA_HEADS, MLA_KV_RANK, MLA_V_DIM), layer)],
        out_specs=pl.BlockSpec((rows, MLA_WIDTH), lambda b: (b, 0)),
        out_shape=jax.ShapeDtypeStruct((bsz * t, MLA_WIDTH), BF16),
        scratch_shapes=[pltpu.VMEM((nseq, MLA_QK_PAD, past), BF16)],
        compiler_params=_cparams("parallel"),
        name="mla_attn_cached",
    )(q, mz, ckv_new, kpe128, ckv_past, kpet_past, wka, wv)


def _attn_body(q_ref, k_ref, v_ref, z_ref, o_ref, m_scr, acc_scr, *, past, tq, tk, s_len):
    iq = pl.program_id(1)
    q_first = past + iq * tq
    full_keys = jnp.minimum((q_first // CHUNK + 1) * CHUNK, s_len)
    vis_keys = jnp.minimum(((q_first + tq - 1) // CHUNK + 1) * CHUNK, s_len)
    n_full = full_keys // tk
    n_vis = (vis_keys + tk - 1) // tk
    split_diagonal = past % tk == 0 and tq == tk and (tq // 2) % CHUNK == 0

    def block(kb, carry, masked, first=False):
        k0 = pl.multiple_of(kb * tk, tk)
        if masked and split_diagonal:
            parts = [(r * (tq // 2), tq // 2, (r + 1) * (tk // 2)) for r in range(2)]
        else:
            parts = [(0, tq, tk)]
        for r0, nr, kext in parts:
            rows = slice(r0, r0 + nr)
            if masked:
                q_pos = (r0 if split_diagonal else q_first + r0) + lax.broadcasted_iota(jnp.int32, (nr, kext), 0)
                k_pos = (0 if split_diagonal else k0) + lax.broadcasted_iota(jnp.int32, (nr, kext), 1)
                visible = k_pos // CHUNK <= q_pos // CHUNK
            ntile, rem = kext // LANES, kext % LANES
            ones_v = jnp.ones((kext, MLA_V_DIM), BF16)
            for h in range(MLA_HEADS):
                sl = slice(h * MLA_QK_PAD, (h + 1) * MLA_QK_PAD)
                vsl = slice(h * MLA_V_DIM, (h + 1) * MLA_V_DIM)
                s = lax.dot_general(q_ref[rows, sl], k_ref[pl.ds(k0, kext), sl], (((1,), (1,)), ((), ())),
                                    preferred_element_type=F32)
                if masked:
                    s = jnp.where(visible, s, -jnp.inf)
                if first:
                    m_new = jnp.broadcast_to(jnp.max(s, axis=-1, keepdims=True), (nr, LANES))
                else:
                    m_prev = m_scr[h, rows]
                    m_new = jnp.maximum(m_prev, jnp.max(s, axis=-1, keepdims=True))
                    alpha = jnp.exp2(m_prev - m_new)
                ps = [jnp.exp2(s[:, c * LANES:(c + 1) * LANES] - m_new) for c in range(ntile)]
                if rem:
                    ps.append(jnp.exp2(s[:, ntile * LANES:] - m_new[:, :rem]))
                p = jnp.concatenate(ps, axis=1).astype(BF16)
                v_ext = jnp.concatenate([v_ref[pl.ds(k0, kext), vsl], ones_v], axis=1)
                pv = _dot(p, v_ext)
                acc_scr[h, rows] = pv if first else jnp.concatenate([alpha, alpha], axis=1) * acc_scr[h, rows] + pv
                m_scr[h, rows] = m_new
        return carry

    @pl.when(n_full > 0)
    def _():
        block(0, 0, masked=False, first=True)
        lax.fori_loop(1, n_full, functools.partial(block, masked=False), 0)
        lax.fori_loop(n_full, n_vis, functools.partial(block, masked=True), 0)

    @pl.when(n_full == 0)
    def _():
        block(0, 0, masked=True, first=True)
        lax.fori_loop(1, n_vis, functools.partial(block, masked=True), 0)

    for h in range(MLA_HEADS):
        vsl = slice(h * MLA_V_DIM, (h + 1) * MLA_V_DIM)
        z = z_ref[:, vsl].astype(F32)
        acc = acc_scr[h]
        o_ref[:, vsl] = (acc[:, :MLA_V_DIM] / acc[:, MLA_V_DIM:] * (z * _sigmoid(z))).astype(BF16)


def _attn(q, k, v, mz, bsz, t, s_len, past, tq, tk):
    nq = t // tq
    return pl.pallas_call(
        functools.partial(_attn_body, past=past, tq=tq, tk=tk, s_len=s_len),
        grid=(bsz, nq),
        in_specs=[pl.BlockSpec((tq, MLA_QK_WIDTH), lambda b, iq: (b * nq + iq, 0)),
                  pl.BlockSpec((s_len, MLA_QK_WIDTH), lambda b, iq: (b, 0)),
                  pl.BlockSpec((s_len, MLA_WIDTH), lambda b, iq: (b, 0)),
                  pl.BlockSpec((tq, MLA_WIDTH), lambda b, iq: (b * nq + iq, 0))],
        out_specs=pl.BlockSpec((tq, MLA_WIDTH), lambda b, iq: (b * nq + iq, 0)),
        out_shape=jax.ShapeDtypeStruct((bsz * t, MLA_WIDTH), BF16),
        scratch_shapes=[pltpu.VMEM((MLA_HEADS, tq, LANES), F32), pltpu.VMEM((MLA_HEADS, tq, 2 * MLA_V_DIM), F32)],
        compiler_params=_cparams("parallel", "arbitrary"),
        name="mla_attn",
    )(q, k, v, mz)


def _s5_weights_body(bre_ref, bim_ref, cre_ref, cim_ref, spread_ref, mask_ref, wb_ref, wc_ref):
    def expand(m_ref):
        return _dot(m_ref[...].astype(BF16), spread_ref[...]) * mask_ref[...]

    wb_ref[:, 0:S5_NSTATE] = expand(bre_ref).astype(BF16)
    wb_ref[:, S5_NSTATE:] = expand(bim_ref).astype(BF16)
    wc_ref[0:S5_NSTATE, :] = jnp.transpose(expand(cre_ref)).astype(BF16)
    wc_ref[S5_NSTATE:, :] = jnp.transpose(-expand(cim_ref)).astype(BF16)


def _s5_weights(bbr, bbi, c_re, c_im):
    depth = bbr.shape[0]
    g_row = np.arange(S5_WIDTH) // S5_GROUP_CH
    g_col = np.arange(S5_NSTATE) // S5_STATE
    mask = (g_row[:, None] == g_col[None, :]).astype(np.float32)
    spread = (np.arange(S5_STATE)[:, None] == (np.arange(S5_NSTATE) % S5_STATE)[None, :]).astype(np.float32)
    small = pl.BlockSpec((None, S5_WIDTH, S5_STATE), lambda l: (l, 0, 0))
    return pl.pallas_call(
        _s5_weights_body,
        grid=(depth,),
        in_specs=[small, small, small, small,
                  _const_spec((S5_STATE, S5_NSTATE)), _const_spec((S5_WIDTH, S5_NSTATE))],
        out_specs=[pl.BlockSpec((None, S5_WIDTH, 2 * S5_NSTATE), lambda l: (l, 0, 0)),
                   pl.BlockSpec((None, 2 * S5_NSTATE, S5_WIDTH), lambda l: (l, 0, 0))],
        out_shape=[jax.ShapeDtypeStruct((depth, S5_WIDTH, 2 * S5_NSTATE), BF16),
                   jax.ShapeDtypeStruct((depth, 2 * S5_NSTATE, S5_WIDTH), BF16)],
        compiler_params=_cparams("parallel"),
        name="s5_weights",
    )(bbr, bbi, c_re, c_im, jnp.asarray(spread, BF16), jnp.asarray(mask, F32))


def _s5_body(p0_ref, pn_ref, x0r_ref, x0i_ref, lre_ref, lim_ref, wb_ref, wc_ref, d_ref, wglu_ref, bglu_ref, never_ref,
             o_ref, xr_out, xi_out, uz_bt, uz_a, uz_b, uz_c, bu_a, bu_b, bu_c, o_tb, xr_s, xi_s, *, lc, pitch, nt):
    it = pl.program_id(1)
    nb = S5_BATCH_TILE
    ring = ((uz_a, bu_a), (uz_b, bu_b), (uz_c, bu_c))

    def stage_in(blk_ref, uz_tb, bu):
        for b in range(nb):
            for c in range(PS_COLS // LANES):
                uz_bt[c, b * pitch:b * pitch + lc, :] = blk_ref[b, :, c * LANES:(c + 1) * LANES].astype(F32)
        for t in range(lc):
            for c in range(PS_COLS // LANES):
                uz_tb[t * nb:(t + 1) * nb, c * LANES:(c + 1) * LANES] = uz_bt[c, pl.ds(t, nb, stride=pitch), :]
        bu[...] = _dot(uz_tb[:, 0:S5_WIDTH].astype(BF16), wb_ref[...])

    def stage_scan(bu):
        w = S5_NSTATE // S5_SCAN_SLICES
        never = never_ref[...] != 0
        last = None
        for c0 in range(0, S5_NSTATE, w):
            re, im = slice(c0, c0 + w), slice(S5_NSTATE + c0, S5_NSTATE + c0 + w)
            lre = jnp.broadcast_to(lre_ref[:, re], (nb, w))
            lim = jnp.broadcast_to(lim_ref[:, re], (nb, w))
            xr, xi = xr_s[:, re], xi_s[:, re]
            if last is not None:
                xr = jnp.where(never, last, xr)
            for t in range(lc):
                rs = slice(t * nb, (t + 1) * nb)
                xr, xi = lre * xr - lim * xi + bu[rs, re], lre * xi + lim * xr + bu[rs, im]
                bu[rs, re] = xr
                bu[rs, im] = xi
            xr_s[:, re] = xr
            xi_s[:, re] = xi
            last = xr

    def stage_out(uz_tb, xs):
        y = _dot(xs[...].astype(BF16), wc_ref[...]) + d_ref[...] * uz_tb[:, 0:S5_WIDTH]
        g5 = 0.5 * y * (1.0 + jnp.tanh(GELU_TANH_SCALE * (y + GELU_TANH_CUBIC * (y * y * y))))
        gate = _sigmoid(_dot(g5.astype(BF16), wglu_ref[...]) + bglu_ref[...])
        z = uz_tb[:, S5_WIDTH:2 * S5_WIDTH]
        o = g5 * gate * (z * _sigmoid(z))
        for c in range(S5_WIDTH // LANES):
            o_tb[c] = o[:, c * LANES:(c + 1) * LANES]
        for b in range(nb):
            for c in range(S5_WIDTH // LANES):
                o_ref[b, :, c * LANES:(c + 1) * LANES] = o_tb[c, pl.ds(b, lc, stride=nb), :].astype(BF16)

    if nt == 1:
        xr_s[...] = x0r_ref[...]
        xi_s[...] = x0i_ref[...]
        stage_in(p0_ref, uz_a, bu_a)
        stage_scan(bu_a)
        stage_out(uz_a, bu_a)
        xr_out[...] = xr_s[...]
        xi_out[...] = xi_s[...]
        return

    @pl.when(it == 0)
    def _():
        xr_s[...] = x0r_ref[...]
        xi_s[...] = x0i_ref[...]
        stage_in(p0_ref, uz_a, bu_a)
        uz_c[...] = jnp.zeros(uz_c.shape, F32)
        bu_c[...] = jnp.zeros(bu_c.shape, F32)

    for r in range(3):
        @pl.when(it % 3 == r)
        def _(r=r):
            stage_in(pn_ref, *ring[(r + 1) % 3])
            stage_scan(ring[r][1])
            stage_out(*ring[(r + 2) % 3])

    @pl.when(it == pl.num_programs(1) - 2)
    def _():
        xr_out[...] = xr_s[...]
        xi_out[...] = xi_s[...]


def _s5(ps3, x0r, x0i, lre, lim, wb, wc, d, wglu, bglu, layer, bsz, t, lc):
    nb = S5_BATCH_TILE
    nt = t // lc
    pitch = lc + 8
    rows = lc * nb
    return pl.pallas_call(
        functools.partial(_s5_body, lc=lc, pitch=pitch, nt=nt),
        grid=(bsz // nb, nt + 1 if nt > 1 else 1),
        in_specs=[pl.BlockSpec((nb, lc, PS_COLS), lambda g, i: (g, 0, 0)),
                  pl.BlockSpec((nb, lc, PS_COLS), lambda g, i: (g, jnp.minimum(i + 1, nt - 1), 0)),
                  pl.BlockSpec((nb, S5_NSTATE), lambda g, i: (g, 0)),
                  pl.BlockSpec((nb, S5_NSTATE), lambda g, i: (g, 0)),
                  _layer_spec((1, S5_NSTATE), layer), _layer_spec((1, S5_NSTATE), layer),
                  _layer_spec((S5_WIDTH, 2 * S5_NSTATE), layer), _layer_spec((2 * S5_NSTATE, S5_WIDTH), layer),
                  _layer_spec((1, S5_WIDTH), layer), _layer_spec((S5_WIDTH, S5_WIDTH), layer),
                  _layer_spec((1, S5_WIDTH), layer),
                  _const_spec((1, S5_NSTATE // S5_SCAN_SLICES))],
        out_specs=[pl.BlockSpec((nb, lc, S5_WIDTH), lambda g, i: (g, jnp.maximum(i - 1, 0), 0)),
                   pl.BlockSpec((nb, S5_NSTATE), lambda g, i: (g, 0)),
                   pl.BlockSpec((nb, S5_NSTATE), lambda g, i: (g, 0))],
        out_shape=[jax.ShapeDtypeStruct((bsz, t, S5_WIDTH), BF16),
                   jax.ShapeDtypeStruct((bsz, S5_NSTATE), F32),
                   jax.ShapeDtypeStruct((bsz, S5_NSTATE), F32)],
        scratch_shapes=[pltpu.VMEM((PS_COLS // LANES, nb * pitch, LANES), F32)]
                       + [pltpu.VMEM((rows, PS_COLS), F32)] * 3
                       + [pltpu.VMEM((rows, 2 * S5_NSTATE), F32)] * 3
                       + [pltpu.VMEM((S5_WIDTH // LANES, rows, LANES), F32),
                          pltpu.VMEM((nb, S5_NSTATE), F32), pltpu.VMEM((nb, S5_NSTATE), F32)],
        compiler_params=_cparams("parallel", "arbitrary"),
        name="s5",
    )(ps3, ps3, x0r, x0i, lre, lim, wb, wc, d, wglu, bglu, jnp.zeros((1, S5_NSTATE // S5_SCAN_SLICES), jnp.int32))


def _outproj_rows(x, og_ref, om_ref, os_ref, wo_scr):
    acc = _dot(og_ref[...], wo_scr[0:GLA_WIDTH, :])
    acc += _dot(om_ref[...], wo_scr[GLA_WIDTH:GLA_WIDTH + MLA_WIDTH, :])
    acc += _dot(os_ref[...], wo_scr[GLA_WIDTH + MLA_WIDTH:, :])
    return x + acc


def _outproj_body(x_hbm, og_ref, om_ref, os_ref, w_ref, g_ref, o_ref, wo_scr, x_buf, x_sem, *, final, tm):
    i, nsteps = pl.program_id(0), pl.num_programs(0)

    def fetch(step):
        slot = step % X_RING
        return pltpu.make_async_copy(x_hbm.at[pl.ds(step * tm, tm), :], x_buf.at[slot], x_sem.at[slot])

    @pl.when(i == 0)
    def _():
        for k in range(X_RING - 1):
            @pl.when(k < nsteps)
            def _():
                fetch(k).start()
        wo_scr[...] = w_ref[...].astype(BF16)

    @pl.when(i + (X_RING - 1) < nsteps)
    def _():
        fetch(i + (X_RING - 1)).start()

    fetch(i).wait()
    xn = _outproj_rows(x_buf[i % X_RING], og_ref, om_ref, os_ref, wo_scr)
    if final:
        ms = jnp.mean(xn * xn, axis=-1, keepdims=True)
        xn = xn * lax.rsqrt(ms + EPS) * g_ref[...]
    o_ref[...] = xn


def _out_in_body(x_ref, og_ref, om_ref, os_ref, wo_ref, g_ref, wt_ref, *rest, with_kv, n_prev):
    prev, rest = rest[:n_prev], rest[n_prev:]
    n_in, n_out = N_MLA_IN[with_kv], N_MLA_OUT[with_kv]
    mla_in, (xo_ref, pg_ref, mz_ref, ps_ref), mla_out, (wo_scr, w_scr) = (
        rest[:n_in], rest[n_in:n_in + 4], rest[n_in + 4:n_in + 4 + n_out], rest[n_in + 4 + n_out:])
    if n_prev:
        ckv_all = mla_out[1]
        for j in range(n_prev):
            ckv_all[j] = prev[j][...]
        mla_out = (mla_out[0], ckv_all.at[n_prev]) + tuple(mla_out[2:])

    @pl.when(pl.program_id(0) == 0)
    def _():
        wo_scr[...] = wo_ref[...].astype(BF16)
        w_scr[...] = wt_ref[...].astype(BF16)

    xn = _outproj_rows(x_ref[...], og_ref, om_ref, os_ref, wo_scr)
    xo_ref[...] = xn
    _inproj_rows(xn, g_ref, w_scr, pg_ref, mz_ref, ps_ref, tuple(mla_in) + tuple(mla_out), with_kv)


def _out_in(x2, og, om, os_, p, cos_t, sin_t, layer, bsz, t, tm, with_kv, ckv_prev=()):
    n = x2.shape[0]
    row = lambda c: pl.BlockSpec((tm, c), lambda i: (i, 0))
    m_args, m_in, m_out, m_shape = _mla_specs(p, cos_t, sin_t, layer + 1, bsz, t, tm, with_kv)
    n_prev = len(ckv_prev)
    if n_prev:
        m_out[1] = pl.BlockSpec((n_prev + 1, tm, MLA_KV_RANK), lambda i: (0, i, 0))
        m_shape[1] = jax.ShapeDtypeStruct((n_prev + 1, n, MLA_KV_RANK), F32)
    o_specs, o_shape = _proj_out(tm, n)
    return pl.pallas_call(
        functools.partial(_out_in_body, with_kv=with_kv, n_prev=n_prev),
        grid=(n // tm,),
        in_specs=[row(D_MODEL), row(GLA_WIDTH), row(MLA_WIDTH), row(S5_WIDTH),
                  _layer_spec((D_MODEL, D_MODEL), layer, single_buffer=True),
                  _layer_spec((1, D_MODEL), layer + 1),
                  _layer_spec((IN_OFF['end'], D_MODEL), layer + 1, single_buffer=True)]
                 + [row(MLA_KV_RANK)] * n_prev + m_in,
        out_specs=[row(D_MODEL)] + o_specs + m_out,
        out_shape=[jax.ShapeDtypeStruct((n, D_MODEL), F32)] + o_shape + m_shape,
        scratch_shapes=[pltpu.VMEM((D_MODEL, D_MODEL), BF16), pltpu.VMEM((IN_OFF['end'], D_MODEL), BF16)],
        compiler_params=_cparams("arbitrary"),
        name="outproj_inproj",
    )(x2, og, om, os_, p['w_out'], p['ln'], p['w_in'], *ckv_prev, *m_args)


def _outproj(x2, og, om, os_, w, gain, layer, tm, final):
    n = x2.shape[0]
    row = lambda c: pl.BlockSpec((tm, c), lambda i: (i, 0))
    return pl.pallas_call(
        functools.partial(_outproj_body, final=final, tm=tm),
        grid=(n // tm,),
        in_specs=[pl.BlockSpec(memory_space=pl.ANY), row(GLA_WIDTH), row(MLA_WIDTH), row(S5_WIDTH),
                  _layer_spec((D_MODEL, D_MODEL), layer, single_buffer=True),
                  _const_spec((1, D_MODEL))],
        out_specs=row(D_MODEL),
        out_shape=jax.ShapeDtypeStruct((n, D_MODEL), F32),
        scratch_shapes=[pltpu.VMEM((D_MODEL, D_MODEL), BF16), pltpu.VMEM((X_RING, tm, D_MODEL), F32),
                        pltpu.SemaphoreType.DMA((X_RING,))],
        compiler_params=_cparams("arbitrary"),
        name="outproj_final" if final else "outproj",
    )(x2, og, om, os_, w, gain)


def _prepare_params(ln_gain, w_in, gla_w_gate, gla_b_gate, gla_norm_gain, mla_q_norm_gain, mla_w_uq,
                    mla_kv_norm_gain, mla_w_ukv, s5_lambda_re, s5_lambda_im, s5_b_re, s5_b_im, s5_c_re, s5_c_im,
                    s5_d, s5_log_dt, s5_w_glu, s5_b_glu, w_out):
    depth = w_in.shape[0]
    w_t = jnp.swapaxes(w_in, 1, 2)
    wg = jnp.pad(gla_w_gate, ((0, 0), (0, GLA_QK - GLA_GATE_RANK), (0, 0))).astype(BF16)
    wq = mla_w_uq.reshape(depth, MLA_Q_RANK, MLA_HEADS, MLA_NOPE_DIM + MLA_ROPE_DIM)
    half = MLA_ROPE_DIM // 2
    wq = jnp.concatenate([wq, -wq[..., MLA_NOPE_DIM + half:], wq[..., MLA_NOPE_DIM:MLA_NOPE_DIM + half]], axis=-1)
    wq = jnp.pad(wq, ((0, 0), (0, 256 - MLA_Q_RANK), (0, 0), (0, 0)))
    wq = wq.reshape(depth, 256, MLA_QK_WIDTH).astype(BF16)
    gq = jnp.pad(mla_q_norm_gain, ((0, 0), (0, 256 - MLA_Q_RANK))).reshape(depth, 1, 256)
    wkv = mla_w_ukv.reshape(depth, MLA_KV_RANK, MLA_HEADS, MLA_NOPE_DIM + MLA_V_DIM)
    wk = jnp.pad(wkv[..., :MLA_NOPE_DIM], ((0, 0), (0, 0), (0, 0), (0, MLA_QK_PAD - MLA_NOPE_DIM)))
    wkv_r = jnp.concatenate([wk.reshape(depth, MLA_KV_RANK, MLA_QK_WIDTH),
                             wkv[..., MLA_NOPE_DIM:].reshape(depth, MLA_KV_RANK, MLA_WIDTH)], axis=2).astype(BF16)
    wka = jnp.pad(jnp.transpose(wkv[..., :MLA_NOPE_DIM], (0, 2, 3, 1)),
                  ((0, 0), (0, 0), (0, MLA_QK_PAD - MLA_NOPE_DIM), (0, 0))).astype(BF16)
    wv = jnp.transpose(wkv[..., MLA_NOPE_DIM:], (0, 2, 1, 3)).astype(BF16)
    dt = jnp.exp(s5_log_dt)[:, :, None]
    mag = jnp.exp(s5_lambda_re * dt)
    lbr, lbi = mag * jnp.cos(s5_lambda_im * dt), mag * jnp.sin(s5_lambda_im * dt)
    den = s5_lambda_re * s5_lambda_re + s5_lambda_im * s5_lambda_im
    qr = ((lbr - 1.0) * s5_lambda_re + lbi * s5_lambda_im) / den
    qi = (lbi * s5_lambda_re - (lbr - 1.0) * s5_lambda_im) / den
    b_re_t, b_im_t = jnp.swapaxes(s5_b_re, 2, 3), jnp.swapaxes(s5_b_im, 2, 3)
    bbr = qr[:, :, None, :] * b_re_t - qi[:, :, None, :] * b_im_t
    bbi = qr[:, :, None, :] * b_im_t + qi[:, :, None, :] * b_re_t
    rows = lambda m: m.reshape(depth, S5_WIDTH, S5_STATE)
    wb, wc = _s5_weights(rows(bbr), rows(bbi), rows(s5_c_re), rows(s5_c_im))
    return dict(
        ln=ln_gain.reshape(depth, 1, D_MODEL), w_in=w_t, wg=wg, bg=gla_b_gate.reshape(depth, 1, GLA_QK),
        gla_gain=jnp.tile(gla_norm_gain, (1, GLA_HEADS)).reshape(depth, 1, GLA_WIDTH),
        gq=gq, wq=wq, gkv=mla_kv_norm_gain.reshape(depth, 1, MLA_KV_RANK), wkv=wkv_r, wka=wka, wv=wv,
        lre=lbr.reshape(depth, 1, S5_NSTATE), lim=lbi.reshape(depth, 1, S5_NSTATE),
        wb=wb, wc=wc, d=s5_d.reshape(depth, 1, S5_WIDTH), wglu=s5_w_glu.astype(BF16),
        bglu=s5_b_glu.reshape(depth, 1, S5_WIDTH), w_out=w_out)


def _rope_tables(past, t, reps):
    half = MLA_ROPE_DIM // 2
    inv = ROPE_BASE ** (-np.arange(half, dtype=np.float64) / half)
    ang = (past + np.arange(t, dtype=np.float64))[:, None] * inv[None, :]
    cos, sin = np.cos(ang), np.sin(ang)
    pad = MLA_QK_PAD - MLA_NOPE_DIM - MLA_ROPE_DIM
    cos_t = np.concatenate([np.ones((t, MLA_NOPE_DIM)), cos, cos, np.zeros((t, pad))], axis=1)
    sin_t = np.concatenate([np.zeros((t, MLA_NOPE_DIM)), sin, sin, np.zeros((t, pad))], axis=1)
    return jnp.asarray(np.tile(cos_t, (reps, 1)), F32), jnp.asarray(np.tile(sin_t, (reps, 1)), F32)


def _trunk(x, gla_state, ckv_cache, kpe_cache, s5_re, s5_im, p, final_gain):
    bsz, t, _ = x.shape
    n = bsz * t
    depth = p['w_in'].shape[0]
    past = 0 if ckv_cache is None else ckv_cache.shape[2]
    s_len = past + t
    tl = _tiles(bsz, t, past)
    cos_t, sin_t = _rope_tables(past, t, max(1, max(tl['row'], tl['in_row']) // t))
    kpet_cache = None if kpe_cache is None else jnp.swapaxes(kpe_cache, 2, 3)
    x2 = x.reshape(n, D_MODEL)
    gain_f = final_gain.reshape(1, D_MODEL)
    gla_o, ckv_o, kpe_o, re_o, im_o = [], [], [], [], []
    with_kv = past == 0
    proj = _inproj(x2, p, cos_t, sin_t, 0, bsz, t, tl['in_row'], with_kv)
    ckv_stack = None
    for l in range(depth):
        pg, mz, ps, q, ckv_new = proj[:5]
        if ckv_new.ndim == 3:
            ckv_stack, ckv_new = ckv_new, ckv_new[l]
        s0 = jnp.zeros((bsz, GLA_HEADS, GLA_DK, GLA_DV), F32) if gla_state is None else gla_state[l]
        o_gla, s_new = _gla(pg.reshape(bsz, t, PG_COLS), p['wg'], p['bg'], p['gla_gain'], l, s0, bsz, t, tl)
        o_gla = o_gla.reshape(n, GLA_WIDTH)
        gla_o.append(s_new)
        if with_kv:
            kpet, k_cat, v_all = proj[5:]
            kpe_o.append(jnp.swapaxes(kpet, 1, 2))
            o_mla = _attn(q, k_cat, v_all, mz, bsz, t, s_len, past, tl['attn_q'], tl['attn_k'])
        else:
            kpe128 = proj[5]
            kpe_o.append(kpe128[:, MLA_NOPE_DIM:MLA_NOPE_DIM + MLA_ROPE_DIM].reshape(bsz, t, MLA_ROPE_DIM))
            o_mla = _attn_cached(q, mz, ckv_new, kpe128, ckv_cache, kpet_cache, p['wka'], p['wv'], l, bsz, t,
                                 tl['cached_seqs'])
        ckv_o.append(ckv_new)
        x0r = jnp.zeros((bsz, S5_NSTATE), F32) if s5_re is None else s5_re[l].reshape(bsz, S5_NSTATE)
        x0i = jnp.zeros((bsz, S5_NSTATE), F32) if s5_im is None else s5_im[l].reshape(bsz, S5_NSTATE)
        o_s5, xr, xi = _s5(ps.reshape(bsz, t, PS_COLS), x0r, x0i, p['lre'], p['lim'], p['wb'], p['wc'],
                           p['d'], p['wglu'], p['bglu'], l, bsz, t, tl['s5_rows'])
        re_o.append(xr.reshape(bsz, S5_GROUPS, S5_STATE))
        im_o.append(xi.reshape(bsz, S5_GROUPS, S5_STATE))
        o_s5 = o_s5.reshape(n, S5_WIDTH)
        if l < depth - 1:
            stack_here = l + 1 == depth - 1
            x2, *proj = _out_in(x2, o_gla, o_mla, o_s5, p, cos_t, sin_t, l, bsz, t, tl['row'], with_kv,
                                ckv_prev=tuple(ckv_o) if stack_here else ())
        else:
            x2 = _outproj(x2, o_gla, o_mla, o_s5, p['w_out'], gain_f, l, tl['out_row'], final=True)
    ckv_all = jnp.stack(ckv_o) if ckv_stack is None else ckv_stack
    return (x2.reshape(bsz, t, D_MODEL), jnp.stack(gla_o), ckv_all.reshape(depth, bsz, t, MLA_KV_RANK),
            jnp.stack(kpe_o), jnp.stack(re_o), jnp.stack(im_o))


def kernel(x_prompt, x_sample, state_gla, cache_mla_ckv, cache_mla_kpe, state_s5_re, state_s5_im, ln_gain, w_in, gla_w_gate, gla_b_gate, gla_norm_gain, mla_q_norm_gain, mla_w_uq, mla_kv_norm_gain, mla_w_ukv, s5_lambda_re, s5_lambda_im, s5_b_re, s5_b_im, s5_c_re, s5_c_im, s5_d, s5_log_dt, s5_w_glu, s5_b_glu, w_out, final_gain):
    p = _prepare_params(ln_gain, w_in, gla_w_gate, gla_b_gate, gla_norm_gain, mla_q_norm_gain, mla_w_uq,
                        mla_kv_norm_gain, mla_w_ukv, s5_lambda_re, s5_lambda_im, s5_b_re, s5_b_im,
                        s5_c_re, s5_c_im, s5_d, s5_log_dt, s5_w_glu, s5_b_glu, w_out)
    y_p, gla_p, ckv_p, kpe_p, re_p, im_p = _trunk(x_prompt, None, None, None, None, None, p, final_gain)
    y_s, gla_s, ckv_s, kpe_s, re_s, im_s = _trunk(x_sample, state_gla, cache_mla_ckv, cache_mla_kpe,
                                                  state_s5_re, state_s5_im, p, final_gain)
    return (y_p, y_s, gla_p, ckv_p, kpe_p, re_p, im_p, gla_s, ckv_s, kpe_s, re_s, im_s)
```

```python
import functools
import math

import numpy as np
import jax
import jax.numpy as jnp
from jax import lax
from jax.experimental import pallas as pl
from jax.experimental.pallas import tpu as pltpu

F32 = jnp.float32
BF16 = jnp.bfloat16

LANES = 128
D_MODEL = 1024
CHUNK = 64
EPS = 1e-6
GLA_HEADS = 4
GLA_DV = 64
GLA_DK = 32
GLA_WIDTH = GLA_HEADS * GLA_DV
GLA_QK = GLA_HEADS * GLA_DK
GLA_GATE_RANK = 16
GLA_GATE_TAU = 16.0
GLA_SUB = 16
GLA_PLAIN_MAX_DECAY = 60.0
MLA_HEADS = 4
MLA_NOPE_DIM = 64
MLA_ROPE_DIM = 32
MLA_V_DIM = 128
MLA_Q_RANK = 192
MLA_KV_RANK = 128
MLA_WIDTH = MLA_HEADS * MLA_V_DIM
MLA_QK_PAD = 128
MLA_QK_WIDTH = MLA_HEADS * MLA_QK_PAD
ROPE_BASE = 10000.0
S5_GROUPS = 16
S5_GROUP_CH = 16
S5_STATE = 64
S5_WIDTH = S5_GROUPS * S5_GROUP_CH
S5_NSTATE = S5_GROUPS * S5_STATE
S5_BATCH_TILE = 8
S5_SCAN_SLICES = 4
GELU_TANH_SCALE = math.sqrt(2.0 / math.pi)
GELU_TANH_CUBIC = 0.044715

PG_COLS = 896
PS_COLS = 512
_IN_SEGS = (('g_q', GLA_QK), ('g_k', GLA_QK), ('g_v', GLA_WIDTH), ('g_lr', GLA_GATE_RANK), ('g_z', GLA_WIDTH),
            ('m_cq', MLA_Q_RANK), ('m_ckv', MLA_KV_RANK), ('m_kr', MLA_ROPE_DIM), ('m_z', MLA_WIDTH),
            ('s_u', S5_WIDTH), ('s_z', S5_WIDTH), ('end', 0))
PG_Q, PG_K, PG_V, PG_Z, PG_LR = slice(0, 128), slice(128, 256), slice(256, 512), slice(512, 768), slice(768, 896)
PG_QKV = slice(PG_Q.start, PG_V.stop)
IN_OFF = dict(zip([n for n, _ in _IN_SEGS], np.cumsum([0] + [w for _, w in _IN_SEGS[:-1]]).tolist()))

VMEM_LIMIT_BYTES = 48 * 1024 * 1024
X_RING = 3


def _tiles(bsz, t, past):
    n = bsz * t
    s_len = past + t
    return dict(
        row=min(512, n),
        in_row=min(1024, n) if t >= 1024 or t < 512 else 512,
        out_row=min(1024, n),
        cached_seqs=math.gcd(bsz, max(1, 128 // t)),
        gla_rows=min(512, t), gla_chunk=min(CHUNK, t),
        gla_seqs=math.gcd(bsz, max(8, 128 // t)),
        attn_q=min(512, t), attn_k=min(512, s_len),
        s5_rows=min(64, t))


def _cparams(*sem):
    return pltpu.CompilerParams(dimension_semantics=sem, vmem_limit_bytes=VMEM_LIMIT_BYTES)


def _sigmoid(x):
    return 0.5 * (1.0 + jnp.tanh(0.5 * x))


def _dot(a, b):
    return jnp.dot(a, b, preferred_element_type=F32)


def _dot_t(a, b):
    return lax.dot_general(a, b, (((0,), (0,)), ((), ())), preferred_element_type=F32)


def _const_spec(shape):
    zeros = (0,) * len(shape)
    return pl.BlockSpec(shape, lambda *_: zeros)


def _layer_spec(shape, layer, single_buffer=False):
    zeros = (0,) * len(shape)
    mode = dict(pipeline_mode=pl.Buffered(1)) if single_buffer else {}
    return pl.BlockSpec((None,) + tuple(shape), lambda *_: (layer,) + zeros, **mode)


def _rope128(x, cos_t, sin_t):
    lane = lax.broadcasted_iota(jnp.int32, x.shape, 1)
    first_half = (lane >= MLA_NOPE_DIM) & (lane < MLA_NOPE_DIM + MLA_ROPE_DIM // 2)
    rot = jnp.where(first_half, -pltpu.roll(x, LANES - MLA_ROPE_DIM // 2, 1), pltpu.roll(x, MLA_ROPE_DIM // 2, 1))
    return x * cos_t + rot * sin_t


def _mla_prep_rows(cq, ckv, kr, mla, with_kv):
    if with_kv:
        cos_ref, sin_ref, gq_ref, wq_ref, gkv_ref, wkv_ref, q_ref, ckv_ref, kpet_ref, k_ref, v_ref = mla
    else:
        cos_ref, sin_ref, gq_ref, wq_ref, gkv_ref, q_ref, ckv_ref, kpe_ref = mla
    cos_t = cos_ref[...]
    sin_t = sin_ref[...]
    ones = jnp.ones((2 * LANES, LANES), BF16)
    ms = _dot((cq * cq).astype(BF16), ones) * (1.0 / MLA_Q_RANK)
    inv = lax.rsqrt(ms + EPS)
    cqn = (cq * jnp.concatenate([inv, inv], axis=1) * gq_ref[...]).astype(BF16)
    qh = _dot(cqn, wq_ref[...])
    scale = (MLA_NOPE_DIM + MLA_ROPE_DIM) ** -0.5 * math.log2(math.e)
    for h in range(MLA_HEADS):
        x = qh[:, h * MLA_QK_PAD:(h + 1) * MLA_QK_PAD]
        roped = x * cos_t + pltpu.roll(x, LANES - MLA_ROPE_DIM, 1) * sin_t
        q_ref[:, h * MLA_QK_PAD:(h + 1) * MLA_QK_PAD] = (roped * scale).astype(BF16)
    ms = _dot((ckv * ckv).astype(BF16), ones[0:LANES]) * (1.0 / MLA_KV_RANK)
    ckv_n = ckv * lax.rsqrt(ms + EPS) * gkv_ref[...]
    ckv_ref[...] = ckv_n
    kpe128 = _rope128(kr, cos_t, sin_t)
    if with_kv:
        kpet_ref[...] = jnp.transpose(kpe128)[MLA_NOPE_DIM:MLA_NOPE_DIM + MLA_ROPE_DIM, :]
        kv = _dot(ckv_n.astype(BF16), wkv_ref[...])
        for h in range(MLA_HEADS):
            sl = slice(h * MLA_QK_PAD, (h + 1) * MLA_QK_PAD)
            k_ref[:, sl] = (kv[:, sl] + kpe128).astype(BF16)
        v_ref[...] = kv[:, MLA_QK_WIDTH:].astype(BF16)
    else:
        kpe_ref[...] = kpe128


def _inproj_rows(x, g_ref, w_scr, og_ref, mz_ref, os_ref, mla, with_kv):
    ms = jnp.mean(x * x, axis=-1, keepdims=True)
    h = (x * lax.rsqrt(ms + EPS) * g_ref[...]).astype(BF16)

    def seg(a, b):
        return lax.dot_general(h, w_scr[a:b, :], (((1,), (1,)), ((), ())), preferred_element_type=F32)

    lane = lax.broadcasted_iota(jnp.int32, (x.shape[0], LANES), 1)
    lane2 = lax.broadcasted_iota(jnp.int32, (x.shape[0], 2 * LANES), 1)
    c = IN_OFF
    og_ref[:, PG_QKV] = seg(c['g_q'], c['g_lr']).astype(BF16)
    og_ref[:, PG_Z] = seg(c['g_z'], c['m_cq']).astype(BF16)
    og_ref[:, PG_LR] = jnp.where(lane < GLA_GATE_RANK, seg(c['g_lr'], c['g_lr'] + LANES), 0.0).astype(BF16)
    mz_ref[...] = seg(c['m_z'], c['s_u']).astype(BF16)
    os_ref[...] = seg(c['s_u'], c['end']).astype(BF16)
    cq = jnp.where(lane2 < MLA_Q_RANK, seg(c['m_cq'], c['m_cq'] + 2 * LANES), 0.0)
    ckv_kr = seg(c['m_ckv'], c['m_ckv'] + 2 * LANES)
    kr = pltpu.roll(ckv_kr[:, LANES:], MLA_NOPE_DIM, 1)
    kr = jnp.where((lane >= MLA_NOPE_DIM) & (lane < MLA_NOPE_DIM + MLA_ROPE_DIM), kr, 0.0)
    _mla_prep_rows(cq, ckv_kr[:, 0:LANES], kr, mla, with_kv)


N_MLA_IN = {True: 6, False: 5}
N_MLA_OUT = {True: 5, False: 3}


def _inproj_body(x_ref, g_ref, wt_ref, *rest, with_kv):
    n_in, n_out = N_MLA_IN[with_kv], N_MLA_OUT[with_kv]
    mla_in, (og_ref, mz_ref, os_ref), mla_out, (w_scr,) = (
        rest[:n_in], rest[n_in:n_in + 3], rest[n_in + 3:n_in + 3 + n_out], rest[n_in + 3 + n_out:])

    @pl.when(pl.program_id(0) == 0)
    def _():
        w_scr[...] = wt_ref[...].astype(BF16)

    _inproj_rows(x_ref[...], g_ref, w_scr, og_ref, mz_ref, os_ref, tuple(mla_in) + tuple(mla_out), with_kv)


def _mla_specs(p, cos_t, sin_t, layer, bsz, t, tm, with_kv):
    n = bsz * t
    ntab = max(1, t // tm)
    assert not with_kv or tm <= t
    row = lambda c: pl.BlockSpec((tm, c), lambda i: (i, 0))
    table = pl.BlockSpec((tm, LANES), lambda i: (i % ntab, 0))
    args = [cos_t, sin_t, p['gq'], p['wq'], p['gkv']]
    in_specs = [table, table, _layer_spec((1, 256), layer), _layer_spec((256, MLA_QK_WIDTH), layer),
                _layer_spec((1, MLA_KV_RANK), layer)]
    out_specs = [row(MLA_QK_WIDTH), row(MLA_KV_RANK)]
    out_shape = [jax.ShapeDtypeStruct((n, MLA_QK_WIDTH), BF16), jax.ShapeDtypeStruct((n, MLA_KV_RANK), F32)]
    if with_kv:
        args.append(p['wkv'])
        in_specs.append(_layer_spec((MLA_KV_RANK, MLA_QK_WIDTH + MLA_WIDTH), layer))
        out_specs += [pl.BlockSpec((None, MLA_ROPE_DIM, tm), lambda i: (i // ntab, 0, i % ntab)),
                      row(MLA_QK_WIDTH), row(MLA_WIDTH)]
        out_shape += [jax.ShapeDtypeStruct((bsz, MLA_ROPE_DIM, t), F32),
                      jax.ShapeDtypeStruct((n, MLA_QK_WIDTH), BF16), jax.ShapeDtypeStruct((n, MLA_WIDTH), BF16)]
    else:
        out_specs.append(row(LANES))
        out_shape.append(jax.ShapeDtypeStruct((n, LANES), F32))
    return args, in_specs, out_specs, out_shape


def _proj_out(tm, n):
    row = lambda c: pl.BlockSpec((tm, c), lambda i: (i, 0))
    return ([row(PG_COLS), row(MLA_WIDTH), row(PS_COLS)],
            [jax.ShapeDtypeStruct((n, PG_COLS), BF16), jax.ShapeDtypeStruct((n, MLA_WIDTH), BF16),
             jax.ShapeDtypeStruct((n, PS_COLS), BF16)])


def _inproj(x2, p, cos_t, sin_t, layer, bsz, t, tm, with_kv):
    n = x2.shape[0]
    m_args, m_in, m_out, m_shape = _mla_specs(p, cos_t, sin_t, layer, bsz, t, tm, with_kv)
    o_specs, o_shape = _proj_out(tm, n)
    return pl.pallas_call(
        functools.partial(_inproj_body, with_kv=with_kv),
        grid=(n // tm,),
        in_specs=[pl.BlockSpec((tm, D_MODEL), lambda i: (i, 0)),
                  _layer_spec((1, D_MODEL), layer),
                  _layer_spec((IN_OFF['end'], D_MODEL), layer, single_buffer=True)] + m_in,
        scratch_shapes=[pltpu.VMEM((IN_OFF['end'], D_MODEL), BF16)],
        out_specs=o_specs + m_out,
        out_shape=o_shape + m_shape,
        compiler_params=_cparams("arbitrary"),
        name="inproj",
    )(x2, p['ln'], p['w_in'], *m_args)


def _gla_body(p_ref, wg_ref, bg_ref, gain_ref, ones_k_ref, ones_v_ref, bd_ref, s0_ref,
              o_ref, sout_ref, q_scr, k_scr, v_scr, b_scr, s_scr, qe_scr, qs_scr, ke_scr, vb_scr, o_scr,
              *, chunk, nch, nseq):
    it = pl.program_id(1)
    sub = min(GLA_SUB, chunk)
    nsub = chunk // sub

    @pl.when(it == 0)
    def _():
        s_scr[...] = jnp.zeros(s_scr.shape, F32)
        for sq in range(nseq):
            for h in range(GLA_HEADS):
                s_scr[sq, h * GLA_DK:(h + 1) * GLA_DK, h * GLA_DV:(h + 1) * GLA_DV] = s0_ref[sq, h]

    def cols(lanes):
        parts = [p_ref[sq, :, lanes] for sq in range(nseq)]
        return parts[0] if nseq == 1 else jnp.concatenate(parts, axis=0)

    q_scr[...] = cols(PG_Q).astype(F32) * (GLA_DK ** -0.5)
    k_scr[...] = cols(PG_K).astype(F32)
    v_scr[...] = cols(PG_V).astype(F32)
    logit = _dot(cols(PG_LR), wg_ref[...]) + bg_ref[...]
    log_a = (jnp.minimum(logit, 0.0) - jnp.log(1.0 + jnp.exp(-jnp.abs(logit)))) * (1.0 / GLA_GATE_TAU)
    row_in_chunk = lax.broadcasted_iota(jnp.int32, log_a.shape, 0) % chunk
    b_all = log_a
    shift = 1
    while shift < chunk:
        b_all = b_all + jnp.where(row_in_chunk >= shift, pltpu.roll(b_all, shift, 0), 0.0)
        shift *= 2
    b_scr[...] = b_all

    row = lax.broadcasted_iota(jnp.int32, (chunk, GLA_QK), 0)
    row_in_sub = lax.broadcasted_iota(jnp.int32, (sub, GLA_QK), 0)

    def decay_columns(b_end):
        col = jnp.transpose(jnp.broadcast_to(jnp.exp(b_end), (GLA_QK, GLA_QK)))
        return jnp.concatenate([col, col], axis=1)

    def load_chunk(sq, c):
        r0 = pl.multiple_of((sq * nch + c) * chunk, chunk)
        return (r0, q_scr[pl.ds(r0, chunk), :], k_scr[pl.ds(r0, chunk), :], v_scr[pl.ds(r0, chunk), :],
                b_scr[pl.ds(r0, chunk), :], b_scr[pl.ds(r0 + chunk - 1, 1), :], s_scr[sq])

    def finish_chunk(sq, r0, o, s_prev, a_state, b_end):
        ms = _dot((o * o).astype(BF16), ones_v_ref[...]) * (1.0 / GLA_DV)
        o_n = o * lax.rsqrt(ms + EPS) * gain_ref[...]
        t0 = r0 - sq * nch * chunk
        z = p_ref[sq, pl.ds(t0, chunk), PG_Z].astype(F32)
        o_ref[sq, pl.ds(t0, chunk), :] = (o_n * (z * _sigmoid(z))).astype(BF16)
        s_scr[sq] = s_prev * decay_columns(b_end) + a_state

    def robust_chunk(c, carry, sq):
        r0, qc, kc, vc, bc, b_end, s_prev = load_chunk(sq, c)
        xs = [qc * jnp.exp(bc)]
        ks = []
        for sj in range(nsub - 1):
            e_j = b_scr[pl.ds(r0 + (sj + 1) * sub - 1, 1), :]
            later = row >= (sj + 1) * sub
            xs.append(jnp.where(later, qc * jnp.exp(jnp.where(later, bc - e_j, 0.0)), 0.0))
            own = (row >= sj * sub) & (row < (sj + 1) * sub)
            ks.append(jnp.where(own, kc * jnp.exp(jnp.where(own, e_j - bc, 0.0)), 0.0))
        ks.append(kc * jnp.exp(b_end - bc))
        k_all = jnp.concatenate(ks, axis=1).astype(BF16)
        a_all = _dot_t(k_all, vc.astype(BF16)) * bd_ref[...]
        w = jnp.concatenate([s_prev, a_all[:(nsub - 1) * GLA_QK]], axis=0).astype(BF16) if nsub > 1 \
            else s_prev.astype(BF16)
        o_off = _dot(jnp.concatenate(xs, axis=1).astype(BF16), w)
        rows = []
        for si in range(nsub):
            q_i = qc[si * sub:(si + 1) * sub]
            b_i = bc[si * sub:(si + 1) * sub]
            es = []
            for j in range(sub):
                r = r0 + si * sub + j
                b_j = b_scr[pl.ds(r, 1), :]
                k_j = k_scr[pl.ds(r, 1), :]
                valid = row_in_sub >= j
                es.append(q_i * k_j * jnp.exp(jnp.where(valid, b_i - b_j, -jnp.inf)))
            e_all = jnp.concatenate(es, axis=0)
            e_hi = e_all.astype(BF16)
            e_lo = (e_all - e_hi.astype(F32)).astype(BF16)
            p_all = _dot(e_hi, ones_k_ref[...]) + _dot(e_lo, ones_k_ref[...])
            acc = o_off[si * sub:(si + 1) * sub]
            for j in range(sub):
                v_j = v_scr[pl.ds(r0 + si * sub + j, 1), :]
                acc = acc + p_all[j * sub:(j + 1) * sub] * v_j
            rows.append(acc)
        o = rows[0] if nsub == 1 else jnp.concatenate(rows, axis=0)
        finish_chunk(sq, r0, o, s_prev, a_all[(nsub - 1) * GLA_QK:], b_end)
        return carry

    lane_head_v = lax.broadcasted_iota(jnp.int32, (chunk, GLA_WIDTH), 1) // GLA_DV
    causal = (lax.broadcasted_iota(jnp.int32, (GLA_HEADS * chunk, chunk), 0) % chunk
              >= lax.broadcasted_iota(jnp.int32, (GLA_HEADS * chunk, chunk), 1))

    def plain_block():
        tt = nseq * nch * chunk
        b_all = b_scr[...]
        q_all = q_scr[...]
        k_all = k_scr[...]
        qe = q_all * jnp.exp(b_all)
        lane_head = lax.broadcasted_iota(jnp.int32, (tt, GLA_QK), 1) // GLA_DK
        qe_scr[...] = qe.astype(BF16)
        for h in range(GLA_HEADS):
            qs_scr[h] = jnp.where(lane_head == h, qe, 0.0).astype(BF16)
        ke_scr[...] = (k_all * jnp.exp(-b_all)).astype(BF16)
        vb_scr[...] = cols(PG_V)
        states = [s_scr[sq] for sq in range(nseq)]
        for c, sq in [(c, sq) for c in range(nch) for sq in range(nseq)]:
            r0 = (sq * nch + c) * chunk
            rs = slice(r0, r0 + chunk)
            s_cur = states[sq]
            qs = jnp.concatenate([qs_scr[h, rs, :] for h in range(GLA_HEADS)], axis=0)
            s = lax.dot_general(qs, ke_scr[rs, :], (((1,), (1,)), ((), ())), preferred_element_type=F32)
            s = jnp.where(causal, s, 0.0).astype(BF16)
            r = _dot(s, vb_scr[rs, :])
            o = _dot(qe_scr[rs, :], s_cur.astype(BF16))
            for h in range(GLA_HEADS):
                o = o + jnp.where(lane_head_v == h, r[h * chunk:(h + 1) * chunk], 0.0)
            o_scr[rs, :] = o
            b_end = b_scr[r0 + chunk - 1:r0 + chunk, :]
            k_end = (k_scr[rs, :] * jnp.exp(b_end - b_scr[rs, :])).astype(BF16)
            a_state = _dot_t(k_end, vb_scr[rs, :]) * bd_ref[0:GLA_QK, :]
            states[sq] = s_cur * decay_columns(b_end) + a_state
        for sq in range(nseq):
            s_scr[sq] = states[sq]
        o = o_scr[...]
        ms = _dot((o * o).astype(BF16), ones_v_ref[...]) * (1.0 / GLA_DV)
        o_n = o * lax.rsqrt(ms + EPS) * gain_ref[...]
        z = cols(PG_Z).astype(F32)
        o_all = (o_n * (z * _sigmoid(z))).astype(BF16)
        for sq in range(nseq):
            o_ref[sq] = o_all[sq * nch * chunk:(sq + 1) * nch * chunk]

    in_range = jnp.max(-b_scr[...]) < GLA_PLAIN_MAX_DECAY

    @pl.when(in_range)
    def _():
        plain_block()

    @pl.when(jnp.logical_not(in_range))
    def _():
        for sq in range(nseq):
            lax.fori_loop(0, nch, functools.partial(robust_chunk, sq=sq), 0)

    @pl.when(it == pl.num_programs(1) - 1)
    def _():
        for sq in range(nseq):
            for h in range(GLA_HEADS):
                sout_ref[sq, h] = s_scr[sq, h * GLA_DK:(h + 1) * GLA_DK, h * GLA_DV:(h + 1) * GLA_DV]


def _gla_consts(chunk):
    nsub = chunk // min(GLA_SUB, chunk)
    hk = np.arange(GLA_QK) // GLA_DK
    hv = np.arange(GLA_WIDTH) // GLA_DV
    same_kv = (hk[:, None] == hv[None, :]).astype(np.float32)
    same_vv = (hv[:, None] == hv[None, :]).astype(np.float32)
    return (jnp.asarray(same_kv, BF16), jnp.asarray(same_vv, BF16),
            jnp.asarray(np.tile(same_kv, (nsub, 1)), F32))


def _gla(pg, wg, bg, gain, layer, s0, bsz, t, tl):
    chunk, tt, nseq = tl['gla_chunk'], tl['gla_rows'], tl['gla_seqs']
    nch = tt // chunk
    nt = t // tt
    rows = nseq * tt
    nsub = chunk // min(GLA_SUB, chunk)
    ones_k, ones_v, bd = _gla_consts(chunk)
    const = _const_spec
    return pl.pallas_call(
        functools.partial(_gla_body, chunk=chunk, nch=nch, nseq=nseq),
        grid=(bsz // nseq, nt),
        in_specs=[pl.BlockSpec((nseq, tt, PG_COLS), lambda b, i: (b, i, 0)),
                  _layer_spec((GLA_QK, GLA_QK), layer), _layer_spec((1, GLA_QK), layer),
                  _layer_spec((1, GLA_WIDTH), layer),
                  const((GLA_QK, GLA_WIDTH)), const((GLA_WIDTH, GLA_WIDTH)),
                  const((nsub * GLA_QK, GLA_WIDTH)),
                  pl.BlockSpec((nseq, GLA_HEADS, GLA_DK, GLA_DV), lambda b, i: (b, 0, 0, 0))],
        out_specs=[pl.BlockSpec((nseq, tt, GLA_WIDTH), lambda b, i: (b, i, 0)),
                   pl.BlockSpec((nseq, GLA_HEADS, GLA_DK, GLA_DV), lambda b, i: (b, 0, 0, 0))],
        out_shape=[jax.ShapeDtypeStruct((bsz, t, GLA_WIDTH), BF16),
                   jax.ShapeDtypeStruct((bsz, GLA_HEADS, GLA_DK, GLA_DV), F32)],
        scratch_shapes=[pltpu.VMEM((rows, GLA_QK), F32), pltpu.VMEM((rows, GLA_QK), F32),
                        pltpu.VMEM((rows, GLA_WIDTH), F32), pltpu.VMEM((rows, GLA_QK), F32),
                        pltpu.VMEM((nseq, GLA_QK, GLA_WIDTH), F32),
                        pltpu.VMEM((rows, GLA_QK), BF16), pltpu.VMEM((GLA_HEADS, rows, GLA_QK), BF16),
                        pltpu.VMEM((rows, GLA_QK), BF16),
                        pltpu.VMEM((rows, GLA_WIDTH), BF16), pltpu.VMEM((rows, GLA_WIDTH), F32)],
        compiler_params=_cparams("parallel", "arbitrary"),
        name="gla",
    )(pg, wg, bg, gain, ones_k, ones_v, bd, s0)


def _attn_cached_body(q_ref, z_ref, ckv_new_ref, kpe_new_ref, ckv_past_ref, kpet_past_ref, wka_ref, wv_ref,
                      o_ref, kpet_scr, *, past, t, nseq):
    last = (((1,), (1,)), ((), ()))
    hsl = [slice(h * MLA_QK_PAD, (h + 1) * MLA_QK_PAD) for h in range(MLA_HEADS)]
    q_lat_h = [_dot(q_ref[:, hsl[h]], wka_ref[h]).astype(BF16) for h in range(MLA_HEADS)]
    kpet_scr[...] = jnp.zeros(kpet_scr.shape, BF16)
    o_lat = []
    for sq in range(nseq):
        rs = slice(sq * t, (sq + 1) * t)
        q_rows = jnp.concatenate([q_ref[rs, hsl[h]] for h in range(MLA_HEADS)], axis=0)
        q_lat = jnp.concatenate([q_lat_h[h][rs] for h in range(MLA_HEADS)], axis=0)
        c_past = ckv_past_ref[sq].astype(BF16)
        c_new = ckv_new_ref[rs, :].astype(BF16)
        kpet_scr[sq, MLA_NOPE_DIM:MLA_NOPE_DIM + MLA_ROPE_DIM, :] = kpet_past_ref[sq].astype(BF16)
        s_past = (lax.dot_general(q_lat, c_past, last, preferred_element_type=F32)
                  + _dot(q_rows, kpet_scr[sq]))
        s_new = (lax.dot_general(q_lat, c_new, last, preferred_element_type=F32)
                 + lax.dot_general(q_rows, kpe_new_ref[rs, :].astype(BF16), last, preferred_element_type=F32))
        if past // CHUNK != (past + t - 1) // CHUNK:
            q_chunk = (past + lax.broadcasted_iota(jnp.int32, s_past.shape, 0) % t) // CHUNK
            s_past = jnp.where(lax.broadcasted_iota(jnp.int32, s_past.shape, 1) // CHUNK <= q_chunk, s_past, -jnp.inf)
            q_chunk = (past + lax.broadcasted_iota(jnp.int32, s_new.shape, 0) % t) // CHUNK
            s_new = jnp.where((past + lax.broadcasted_iota(jnp.int32, s_new.shape, 1)) // CHUNK <= q_chunk,
                              s_new, -jnp.inf)
        m = jnp.maximum(jnp.max(s_past, axis=-1, keepdims=True), jnp.max(s_new, axis=-1, keepdims=True))
        p_past = jnp.exp2(s_past - m)
        p_new = jnp.exp2(s_new - m)
        l = jnp.sum(p_past, axis=-1, keepdims=True) + jnp.sum(p_new, axis=-1, keepdims=True)
        o_lat.append(((_dot(p_past.astype(BF16), c_past) + _dot(p_new.astype(BF16), c_new)) / l).astype(BF16))
    for h in range(MLA_HEADS):
        vsl = slice(h * MLA_V_DIM, (h + 1) * MLA_V_DIM)
        o_h = jnp.concatenate([o_lat[sq][h * t:(h + 1) * t] for sq in range(nseq)], axis=0)
        z = z_ref[:, vsl].astype(F32)
        o_ref[:, vsl] = (_dot(o_h, wv_ref[h]) * (z * _sigmoid(z))).astype(BF16)


def _attn_cached(q, mz, ckv_new, kpe128, ckv_past, kpet_past, wka, wv, layer, bsz, t, nseq):
    past = ckv_past.shape[2]
    rows = nseq * t
    return pl.pallas_call(
        functools.partial(_attn_cached_body, past=past, t=t, nseq=nseq),
        grid=(bsz // nseq,),
        in_specs=[pl.BlockSpec((rows, MLA_QK_WIDTH), lambda b: (b, 0)),
                  pl.BlockSpec((rows, MLA_WIDTH), lambda b: (b, 0)),
                  pl.BlockSpec((rows, MLA_KV_RANK), lambda b: (b, 0)),
                  pl.BlockSpec((rows, LANES), lambda b: (b, 0)),
                  pl.BlockSpec((None, nseq, past, MLA_KV_RANK), lambda b: (layer, b, 0, 0)),
                  pl.BlockSpec((None, nseq, MLA_ROPE_DIM, past), lambda b: (layer, b, 0, 0)),
                  _layer_spec((MLA_HEADS, MLA_QK_PAD, MLA_KV_RANK), layer),
                  _layer_spec((MLA_HEADS, MLA_KV_RANK, MLA_V_DIM), layer)],
        out_specs=pl.BlockSpec((rows, MLA_WIDTH), lambda b: (b, 0)),
        out_shape=jax.ShapeDtypeStruct((bsz * t, MLA_WIDTH), BF16),
        scratch_shapes=[pltpu.VMEM((nseq, MLA_QK_PAD, past), BF16)],
        compiler_params=_cparams("parallel"),
        name="mla_attn_cached",
    )(q, mz, ckv_new, kpe128, ckv_past, kpet_past, wka, wv)


def _attn_body(q_ref, k_ref, v_ref, z_ref, o_ref, m_scr, acc_scr, *, past, tq, tk, s_len):
    iq = pl.program_id(1)
    q_first = past + iq * tq
    full_keys = jnp.minimum((q_first // CHUNK + 1) * CHUNK, s_len)
    vis_keys = jnp.minimum(((q_first + tq - 1) // CHUNK + 1) * CHUNK, s_len)
    n_full = full_keys // tk
    n_vis = (vis_keys + tk - 1) // tk
    split_diagonal = past % tk == 0 and tq == tk and (tq // 2) % CHUNK == 0

    def block(kb, carry, masked, first=False):
        k0 = pl.multiple_of(kb * tk, tk)
        if masked and split_diagonal:
            parts = [(r * (tq // 2), tq // 2, (r + 1) * (tk // 2)) for r in range(2)]
        else:
            parts = [(0, tq, tk)]
        for r0, nr, kext in parts:
            rows = slice(r0, r0 + nr)
            if masked:
                q_pos = (r0 if split_diagonal else q_first + r0) + lax.broadcasted_iota(jnp.int32, (nr, kext), 0)
                k_pos = (0 if split_diagonal else k0) + lax.broadcasted_iota(jnp.int32, (nr, kext), 1)
                visible = k_pos // CHUNK <= q_pos // CHUNK
            ntile, rem = kext // LANES, kext % LANES
            ones_v = jnp.ones((kext, MLA_V_DIM), BF16)
            for h in range(MLA_HEADS):
                sl = slice(h * MLA_QK_PAD, (h + 1) * MLA_QK_PAD)
                vsl = slice(h * MLA_V_DIM, (h + 1) * MLA_V_DIM)
                s = lax.dot_general(q_ref[rows, sl], k_ref[pl.ds(k0, kext), sl], (((1,), (1,)), ((), ())),
                                    preferred_element_type=F32)
                if masked:
                    s = jnp.where(visible, s, -jnp.inf)
                if first:
                    m_new = jnp.broadcast_to(jnp.max(s, axis=-1, keepdims=True), (nr, LANES))
                else:
                    m_prev = m_scr[h, rows]
                    m_new = jnp.maximum(m_prev, jnp.max(s, axis=-1, keepdims=True))
                    alpha = jnp.exp2(m_prev - m_new)
                ps = [jnp.exp2(s[:, c * LANES:(c + 1) * LANES] - m_new) for c in range(ntile)]
                if rem:
                    ps.append(jnp.exp2(s[:, ntile * LANES:] - m_new[:, :rem]))
                p = jnp.concatenate(ps, axis=1).astype(BF16)
                v_ext = jnp.concatenate([v_ref[pl.ds(k0, kext), vsl], ones_v], axis=1)
                pv = _dot(p, v_ext)
                acc_scr[h, rows] = pv if first else jnp.concatenate([alpha, alpha], axis=1) * acc_scr[h, rows] + pv
                m_scr[h, rows] = m_new
        return carry

    @pl.when(n_full > 0)
    def _():
        block(0, 0, masked=False, first=True)
        lax.fori_loop(1, n_full, functools.partial(block, masked=False), 0)
        lax.fori_loop(n_full, n_vis, functools.partial(block, masked=True), 0)

    @pl.when(n_full == 0)
    def _():
        block(0, 0, masked=True, first=True)
        lax.fori_loop(1, n_vis, functools.partial(block, masked=True), 0)

    for h in range(MLA_HEADS):
        vsl = slice(h * MLA_V_DIM, (h + 1) * MLA_V_DIM)
        z = z_ref[:, vsl].astype(F32)
        acc = acc_scr[h]
        o_ref[:, vsl] = (acc[:, :MLA_V_DIM] / acc[:, MLA_V_DIM:] * (z * _sigmoid(z))).astype(BF16)


def _attn(q, k, v, mz, bsz, t, s_len, past, tq, tk):
    nq = t // tq
    return pl.pallas_call(
        functools.partial(_attn_body, past=past, tq=tq, tk=tk, s_len=s_len),
        grid=(bsz, nq),
        in_specs=[pl.BlockSpec((tq, MLA_QK_WIDTH), lambda b, iq: (b * nq + iq, 0)),
                  pl.BlockSpec((s_len, MLA_QK_WIDTH), lambda b, iq: (b, 0)),
                  pl.BlockSpec((s_len, MLA_WIDTH), lambda b, iq: (b, 0)),
                  pl.BlockSpec((tq, MLA_WIDTH), lambda b, iq: (b * nq + iq, 0))],
        out_specs=pl.BlockSpec((tq, MLA_WIDTH), lambda b, iq: (b * nq + iq, 0)),
        out_shape=jax.ShapeDtypeStruct((bsz * t, MLA_WIDTH), BF16),
        scratch_shapes=[pltpu.VMEM((MLA_HEADS, tq, LANES), F32), pltpu.VMEM((MLA_HEADS, tq, 2 * MLA_V_DIM), F32)],
        compiler_params=_cparams("parallel", "arbitrary"),
        name="mla_attn",
    )(q, k, v, mz)


def _s5_weights_body(bre_ref, bim_ref, cre_ref, cim_ref, spread_ref, mask_ref, wb_ref, wc_ref):
    def expand(m_ref):
        return _dot(m_ref[...].astype(BF16), spread_ref[...]) * mask_ref[...]

    wb_ref[:, 0:S5_NSTATE] = expand(bre_ref).astype(BF16)
    wb_ref[:, S5_NSTATE:] = expand(bim_ref).astype(BF16)
    wc_ref[0:S5_NSTATE, :] = jnp.transpose(expand(cre_ref)).astype(BF16)
    wc_ref[S5_NSTATE:, :] = jnp.transpose(-expand(cim_ref)).astype(BF16)


def _s5_weights(bbr, bbi, c_re, c_im):
    depth = bbr.shape[0]
    g_row = np.arange(S5_WIDTH) // S5_GROUP_CH
    g_col = np.arange(S5_NSTATE) // S5_STATE
    mask = (g_row[:, None] == g_col[None, :]).astype(np.float32)
    spread = (np.arange(S5_STATE)[:, None] == (np.arange(S5_NSTATE) % S5_STATE)[None, :]).astype(np.float32)
    small = pl.BlockSpec((None, S5_WIDTH, S5_STATE), lambda l: (l, 0, 0))
    return pl.pallas_call(
        _s5_weights_body,
        grid=(depth,),
        in_specs=[small, small, small, small,
                  _const_spec((S5_STATE, S5_NSTATE)), _const_spec((S5_WIDTH, S5_NSTATE))],
        out_specs=[pl.BlockSpec((None, S5_WIDTH, 2 * S5_NSTATE), lambda l: (l, 0, 0)),
                   pl.BlockSpec((None, 2 * S5_NSTATE, S5_WIDTH), lambda l: (l, 0, 0))],
        out_shape=[jax.ShapeDtypeStruct((depth, S5_WIDTH, 2 * S5_NSTATE), BF16),
                   jax.ShapeDtypeStruct((depth, 2 * S5_NSTATE, S5_WIDTH), BF16)],
        compiler_params=_cparams("parallel"),
        name="s5_weights",
    )(bbr, bbi, c_re, c_im, jnp.asarray(spread, BF16), jnp.asarray(mask, F32))


def _s5_body(p0_ref, pn_ref, x0r_ref, x0i_ref, lre_ref, lim_ref, wb_ref, wc_ref, d_ref, wglu_ref, bglu_ref, never_ref,
             o_ref, xr_out, xi_out, uz_bt, uz_a, uz_b, uz_c, bu_a, bu_b, bu_c, o_tb, xr_s, xi_s, *, lc, pitch, nt):
    it = pl.program_id(1)
    nb = S5_BATCH_TILE
    ring = ((uz_a, bu_a), (uz_b, bu_b), (uz_c, bu_c))

    def stage_in(blk_ref, uz_tb, bu):
        for b in range(nb):
            for c in range(PS_COLS // LANES):
                uz_bt[c, b * pitch:b * pitch + lc, :] = blk_ref[b, :, c * LANES:(c + 1) * LANES].astype(F32)
        for t in range(lc):
            for c in range(PS_COLS // LANES):
                uz_tb[t * nb:(t + 1) * nb, c * LANES:(c + 1) * LANES] = uz_bt[c, pl.ds(t, nb, stride=pitch), :]
        bu[...] = _dot(uz_tb[:, 0:S5_WIDTH].astype(BF16), wb_ref[...])

    def stage_scan(bu):
        w = S5_NSTATE // S5_SCAN_SLICES
        never = never_ref[...] != 0
        last = None
        for c0 in range(0, S5_NSTATE, w):
            re, im = slice(c0, c0 + w), slice(S5_NSTATE + c0, S5_NSTATE + c0 + w)
            lre = jnp.broadcast_to(lre_ref[:, re], (nb, w))
            lim = jnp.broadcast_to(lim_ref[:, re], (nb, w))
            xr, xi = xr_s[:, re], xi_s[:, re]
            if last is not None:
                xr = jnp.where(never, last, xr)
            for t in range(lc):
                rs = slice(t * nb, (t + 1) * nb)
                xr, xi = lre * xr - lim * xi + bu[rs, re], lre * xi + lim * xr + bu[rs, im]
                bu[rs, re] = xr
                bu[rs, im] = xi
            xr_s[:, re] = xr
            xi_s[:, re] = xi
            last = xr

    def stage_out(uz_tb, xs):
        y = _dot(xs[...].astype(BF16), wc_ref[...]) + d_ref[...] * uz_tb[:, 0:S5_WIDTH]
        g5 = 0.5 * y * (1.0 + jnp.tanh(GELU_TANH_SCALE * (y + GELU_TANH_CUBIC * (y * y * y))))
        gate = _sigmoid(_dot(g5.astype(BF16), wglu_ref[...]) + bglu_ref[...])
        z = uz_tb[:, S5_WIDTH:2 * S5_WIDTH]
        o = g5 * gate * (z * _sigmoid(z))
        for c in range(S5_WIDTH // LANES):
            o_tb[c] = o[:, c * LANES:(c + 1) * LANES]
        for b in range(nb):
            for c in range(S5_WIDTH // LANES):
                o_ref[b, :, c * LANES:(c + 1) * LANES] = o_tb[c, pl.ds(b, lc, stride=nb), :].astype(BF16)

    if nt == 1:
        xr_s[...] = x0r_ref[...]
        xi_s[...] = x0i_ref[...]
        stage_in(p0_ref, uz_a, bu_a)
        stage_scan(bu_a)
        stage_out(uz_a, bu_a)
        xr_out[...] = xr_s[...]
        xi_out[...] = xi_s[...]
        return

    @pl.when(it == 0)
    def _():
        xr_s[...] = x0r_ref[...]
        xi_s[...] = x0i_ref[...]
        stage_in(p0_ref, uz_a, bu_a)
        uz_c[...] = jnp.zeros(uz_c.shape, F32)
        bu_c[...] = jnp.zeros(bu_c.shape, F32)

    for r in range(3):
        @pl.when(it % 3 == r)
        def _(r=r):
            stage_in(pn_ref, *ring[(r + 1) % 3])
            stage_scan(ring[r][1])
            stage_out(*ring[(r + 2) % 3])

    @pl.when(it == pl.num_programs(1) - 2)
    def _():
        xr_out[...] = xr_s[...]
        xi_out[...] = xi_s[...]


def _s5(ps3, x0r, x0i, lre, lim, wb, wc, d, wglu, bglu, layer, bsz, t, lc):
    nb = S5_BATCH_TILE
    nt = t // lc
    pitch = lc + 8
    rows = lc * nb
    return pl.pallas_call(
        functools.partial(_s5_body, lc=lc, pitch=pitch, nt=nt),
        grid=(bsz // nb, nt + 1 if nt > 1 else 1),
        in_specs=[pl.BlockSpec((nb, lc, PS_COLS), lambda g, i: (g, 0, 0)),
                  pl.BlockSpec((nb, lc, PS_COLS), lambda g, i: (g, jnp.minimum(i + 1, nt - 1), 0)),
                  pl.BlockSpec((nb, S5_NSTATE), lambda g, i: (g, 0)),
                  pl.BlockSpec((nb, S5_NSTATE), lambda g, i: (g, 0)),
                  _layer_spec((1, S5_NSTATE), layer), _layer_spec((1, S5_NSTATE), layer),
                  _layer_spec((S5_WIDTH, 2 * S5_NSTATE), layer), _layer_spec((2 * S5_NSTATE, S5_WIDTH), layer),
                  _layer_spec((1, S5_WIDTH), layer), _layer_spec((S5_WIDTH, S5_WIDTH), layer),
                  _layer_spec((1, S5_WIDTH), layer),
                  _const_spec((1, S5_NSTATE // S5_SCAN_SLICES))],
        out_specs=[pl.BlockSpec((nb, lc, S5_WIDTH), lambda g, i: (g, jnp.maximum(i - 1, 0), 0)),
                   pl.BlockSpec((nb, S5_NSTATE), lambda g, i: (g, 0)),
                   pl.BlockSpec((nb, S5_NSTATE), lambda g, i: (g, 0))],
        out_shape=[jax.ShapeDtypeStruct((bsz, t, S5_WIDTH), BF16),
                   jax.ShapeDtypeStruct((bsz, S5_NSTATE), F32),
                   jax.ShapeDtypeStruct((bsz, S5_NSTATE), F32)],
        scratch_shapes=[pltpu.VMEM((PS_COLS // LANES, nb * pitch, LANES), F32)]
                       + [pltpu.VMEM((rows, PS_COLS), F32)] * 3
                       + [pltpu.VMEM((rows, 2 * S5_NSTATE), F32)] * 3
                       + [pltpu.VMEM((S5_WIDTH // LANES, rows, LANES), F32),
                          pltpu.VMEM((nb, S5_NSTATE), F32), pltpu.VMEM((nb, S5_NSTATE), F32)],
        compiler_params=_cparams("parallel", "arbitrary"),
        name="s5",
    )(ps3, ps3, x0r, x0i, lre, lim, wb, wc, d, wglu, bglu, jnp.zeros((1, S5_NSTATE // S5_SCAN_SLICES), jnp.int32))


def _outproj_rows(x, og_ref, om_ref, os_ref, wo_scr):
    acc = _dot(og_ref[...], wo_scr[0:GLA_WIDTH, :])
    acc += _dot(om_ref[...], wo_scr[GLA_WIDTH:GLA_WIDTH + MLA_WIDTH, :])
    acc += _dot(os_ref[...], wo_scr[GLA_WIDTH + MLA_WIDTH:, :])
    return x + acc


def _outproj_body(x_hbm, og_ref, om_ref, os_ref, w_ref, g_ref, o_ref, wo_scr, x_buf, x_sem, *, final, tm):
    i, nsteps = pl.program_id(0), pl.num_programs(0)

    def fetch(step):
        slot = step % X_RING
        return pltpu.make_async_copy(x_hbm.at[pl.ds(step * tm, tm), :], x_buf.at[slot], x_sem.at[slot])

    @pl.when(i == 0)
    def _():
        for k in range(X_RING - 1):
            @pl.when(k < nsteps)
            def _():
                fetch(k).start()
        wo_scr[...] = w_ref[...].astype(BF16)

    @pl.when(i + (X_RING - 1) < nsteps)
    def _():
        fetch(i + (X_RING - 1)).start()

    fetch(i).wait()
    xn = _outproj_rows(x_buf[i % X_RING], og_ref, om_ref, os_ref, wo_scr)
    if final:
        ms = jnp.mean(xn * xn, axis=-1, keepdims=True)
        xn = xn * lax.rsqrt(ms + EPS) * g_ref[...]
    o_ref[...] = xn


def _out_in_body(x_hbm, og_ref, om_ref, os_ref, wo_ref, g_ref, wt_ref, *rest, with_kv, n_prev, tm):
    prev, rest = rest[:n_prev], rest[n_prev:]
    n_in, n_out = N_MLA_IN[with_kv], N_MLA_OUT[with_kv]
    mla_in, (xo_ref, pg_ref, mz_ref, ps_ref), mla_out, (wo_scr, w_scr, x_buf, x_sem) = (
        rest[:n_in], rest[n_in:n_in + 4], rest[n_in + 4:n_in + 4 + n_out], rest[n_in + 4 + n_out:])
    if n_prev:
        ckv_all = mla_out[1]
        for j in range(n_prev):
            ckv_all[j] = prev[j][...]
        mla_out = (mla_out[0], ckv_all.at[n_prev]) + tuple(mla_out[2:])

    i, nsteps = pl.program_id(0), pl.num_programs(0)

    def fetch(step):
        slot = step % X_RING
        return pltpu.make_async_copy(x_hbm.at[pl.ds(step * tm, tm), :], x_buf.at[slot], x_sem.at[slot])

    @pl.when(i == 0)
    def _():
        for k in range(X_RING - 1):
            @pl.when(k < nsteps)
            def _():
                fetch(k).start()
        wo_scr[...] = wo_ref[...].astype(BF16)
        w_scr[...] = wt_ref[...].astype(BF16)

    @pl.when(i + (X_RING - 1) < nsteps)
    def _():
        fetch(i + (X_RING - 1)).start()

    fetch(i).wait()
    xn = _outproj_rows(x_buf[i % X_RING], og_ref, om_ref, os_ref, wo_scr)
    xo_ref[...] = xn
    _inproj_rows(xn, g_ref, w_scr, pg_ref, mz_ref, ps_ref, tuple(mla_in) + tuple(mla_out), with_kv)


def _out_in(x2, og, om, os_, p, cos_t, sin_t, layer, bsz, t, tm, with_kv, ckv_prev=()):
    n = x2.shape[0]
    row = lambda c: pl.BlockSpec((tm, c), lambda i: (i, 0))
    m_args, m_in, m_out, m_shape = _mla_specs(p, cos_t, sin_t, layer + 1, bsz, t, tm, with_kv)
    n_prev = len(ckv_prev)
    if n_prev:
        m_out[1] = pl.BlockSpec((n_prev + 1, tm, MLA_KV_RANK), lambda i: (0, i, 0))
        m_shape[1] = jax.ShapeDtypeStruct((n_prev + 1, n, MLA_KV_RANK), F32)
    o_specs, o_shape = _proj_out(tm, n)
    return pl.pallas_call(
        functools.partial(_out_in_body, with_kv=with_kv, n_prev=n_prev, tm=tm),
        grid=(n // tm,),
        in_specs=[pl.BlockSpec(memory_space=pl.ANY), row(GLA_WIDTH), row(MLA_WIDTH), row(S5_WIDTH),
                  _layer_spec((D_MODEL, D_MODEL), layer, single_buffer=True),
                  _layer_spec((1, D_MODEL), layer + 1),
                  _layer_spec((IN_OFF['end'], D_MODEL), layer + 1, single_buffer=True)]
                 + [row(MLA_KV_RANK)] * n_prev + m_in,
        out_specs=[row(D_MODEL)] + o_specs + m_out,
        out_shape=[jax.ShapeDtypeStruct((n, D_MODEL), F32)] + o_shape + m_shape,
        scratch_shapes=[pltpu.VMEM((D_MODEL, D_MODEL), BF16), pltpu.VMEM((IN_OFF['end'], D_MODEL), BF16),
                        pltpu.VMEM((X_RING, tm, D_MODEL), F32), pltpu.SemaphoreType.DMA((X_RING,))],
        compiler_params=_cparams("arbitrary"),
        name="outproj_inproj",
    )(x2, og, om, os_, p['w_out'], p['ln'], p['w_in'], *ckv_prev, *m_args)


def _outproj(x2, og, om, os_, w, gain, layer, tm, final):
    n = x2.shape[0]
    row = lambda c: pl.BlockSpec((tm, c), lambda i: (i, 0))
    return pl.pallas_call(
        functools.partial(_outproj_body, final=final, tm=tm),
        grid=(n // tm,),
        in_specs=[pl.BlockSpec(memory_space=pl.ANY), row(GLA_WIDTH), row(MLA_WIDTH), row(S5_WIDTH),
                  _layer_spec((D_MODEL, D_MODEL), layer, single_buffer=True),
                  _const_spec((1, D_MODEL))],
        out_specs=row(D_MODEL),
        out_shape=jax.ShapeDtypeStruct((n, D_MODEL), F32),
        scratch_shapes=[pltpu.VMEM((D_MODEL, D_MODEL), BF16), pltpu.VMEM((X_RING, tm, D_MODEL), F32),
                        pltpu.SemaphoreType.DMA((X_RING,))],
        compiler_params=_cparams("arbitrary"),
        name="outproj_final" if final else "outproj",
    )(x2, og, om, os_, w, gain)


def _prepare_params(ln_gain, w_in, gla_w_gate, gla_b_gate, gla_norm_gain, mla_q_norm_gain, mla_w_uq,
                    mla_kv_norm_gain, mla_w_ukv, s5_lambda_re, s5_lambda_im, s5_b_re, s5_b_im, s5_c_re, s5_c_im,
                    s5_d, s5_log_dt, s5_w_glu, s5_b_glu, w_out):
    depth = w_in.shape[0]
    w_t = jnp.swapaxes(w_in, 1, 2)
    wg = jnp.pad(gla_w_gate, ((0, 0), (0, GLA_QK - GLA_GATE_RANK), (0, 0))).astype(BF16)
    wq = mla_w_uq.reshape(depth, MLA_Q_RANK, MLA_HEADS, MLA_NOPE_DIM + MLA_ROPE_DIM)
    half = MLA_ROPE_DIM // 2
    wq = jnp.concatenate([wq, -wq[..., MLA_NOPE_DIM + half:], wq[..., MLA_NOPE_DIM:MLA_NOPE_DIM + half]], axis=-1)
    wq = jnp.pad(wq, ((0, 0), (0, 256 - MLA_Q_RANK), (0, 0), (0, 0)))
    wq = wq.reshape(depth, 256, MLA_QK_WIDTH).astype(BF16)
    gq = jnp.pad(mla_q_norm_gain, ((0, 0), (0, 256 - MLA_Q_RANK))).reshape(depth, 1, 256)
    wkv = mla_w_ukv.reshape(depth, MLA_KV_RANK, MLA_HEADS, MLA_NOPE_DIM + MLA_V_DIM)
    wk = jnp.pad(wkv[..., :MLA_NOPE_DIM], ((0, 0), (0, 0), (0, 0), (0, MLA_QK_PAD - MLA_NOPE_DIM)))
    wkv_r = jnp.concatenate([wk.reshape(depth, MLA_KV_RANK, MLA_QK_WIDTH),
                             wkv[..., MLA_NOPE_DIM:].reshape(depth, MLA_KV_RANK, MLA_WIDTH)], axis=2).astype(BF16)
    wka = jnp.pad(jnp.transpose(wkv[..., :MLA_NOPE_DIM], (0, 2, 3, 1)),
                  ((0, 0), (0, 0), (0, MLA_QK_PAD - MLA_NOPE_DIM), (0, 0))).astype(BF16)
    wv = jnp.transpose(wkv[..., MLA_NOPE_DIM:], (0, 2, 1, 3)).astype(BF16)
    dt = jnp.exp(s5_log_dt)[:, :, None]
    mag = jnp.exp(s5_lambda_re * dt)
    lbr, lbi = mag * jnp.cos(s5_lambda_im * dt), mag * jnp.sin(s5_lambda_im * dt)
    den = s5_lambda_re * s5_lambda_re + s5_lambda_im * s5_lambda_im
    qr = ((lbr - 1.0) * s5_lambda_re + lbi * s5_lambda_im) / den
    qi = (lbi * s5_lambda_re - (lbr - 1.0) * s5_lambda_im) / den
    b_re_t, b_im_t = jnp.swapaxes(s5_b_re, 2, 3), jnp.swapaxes(s5_b_im, 2, 3)
    bbr = qr[:, :, None, :] * b_re_t - qi[:, :, None, :] * b_im_t
    bbi = qr[:, :, None, :] * b_im_t + qi[:, :, None, :] * b_re_t
    rows = lambda m: m.reshape(depth, S5_WIDTH, S5_STATE)
    wb, wc = _s5_weights(rows(bbr), rows(bbi), rows(s5_c_re), rows(s5_c_im))
    return dict(
        ln=ln_gain.reshape(depth, 1, D_MODEL), w_in=w_t, wg=wg, bg=gla_b_gate.reshape(depth, 1, GLA_QK),
        gla_gain=jnp.tile(gla_norm_gain, (1, GLA_HEADS)).reshape(depth, 1, GLA_WIDTH),
        gq=gq, wq=wq, gkv=mla_kv_norm_gain.reshape(depth, 1, MLA_KV_RANK), wkv=wkv_r, wka=wka, wv=wv,
        lre=lbr.reshape(depth, 1, S5_NSTATE), lim=lbi.reshape(depth, 1, S5_NSTATE),
        wb=wb, wc=wc, d=s5_d.reshape(depth, 1, S5_WIDTH), wglu=s5_w_glu.astype(BF16),
        bglu=s5_b_glu.reshape(depth, 1, S5_WIDTH), w_out=w_out)


def _rope_tables(past, t, reps):
    half = MLA_ROPE_DIM // 2
    inv = ROPE_BASE ** (-np.arange(half, dtype=np.float64) / half)
    ang = (past + np.arange(t, dtype=np.float64))[:, None] * inv[None, :]
    cos, sin = np.cos(ang), np.sin(ang)
    pad = MLA_QK_PAD - MLA_NOPE_DIM - MLA_ROPE_DIM
    cos_t = np.concatenate([np.ones((t, MLA_NOPE_DIM)), cos, cos, np.zeros((t, pad))], axis=1)
    sin_t = np.concatenate([np.zeros((t, MLA_NOPE_DIM)), sin, sin, np.zeros((t, pad))], axis=1)
    return jnp.asarray(np.tile(cos_t, (reps, 1)), F32), jnp.asarray(np.tile(sin_t, (reps, 1)), F32)


def _trunk(x, gla_state, ckv_cache, kpe_cache, s5_re, s5_im, p, final_gain):
    bsz, t, _ = x.shape
    n = bsz * t
    depth = p['w_in'].shape[0]
    past = 0 if ckv_cache is None else ckv_cache.shape[2]
    s_len = past + t
    tl = _tiles(bsz, t, past)
    cos_t, sin_t = _rope_tables(past, t, max(1, max(tl['row'], tl['in_row']) // t))
    kpet_cache = None if kpe_cache is None else jnp.swapaxes(kpe_cache, 2, 3)
    x2 = x.reshape(n, D_MODEL)
    gain_f = final_gain.reshape(1, D_MODEL)
    gla_o, ckv_o, kpe_o, re_o, im_o = [], [], [], [], []
    with_kv = past == 0
    proj = _inproj(x2, p, cos_t, sin_t, 0, bsz, t, tl['in_row'], with_kv)
    ckv_stack = None
    for l in range(depth):
        pg, mz, ps, q, ckv_new = proj[:5]
        if ckv_new.ndim == 3:
            ckv_stack, ckv_new = ckv_new, ckv_new[l]
        s0 = jnp.zeros((bsz, GLA_HEADS, GLA_DK, GLA_DV), F32) if gla_state is None else gla_state[l]
        o_gla, s_new = _gla(pg.reshape(bsz, t, PG_COLS), p['wg'], p['bg'], p['gla_gain'], l, s0, bsz, t, tl)
        o_gla = o_gla.reshape(n, GLA_WIDTH)
        gla_o.append(s_new)
        if with_kv:
            kpet, k_cat, v_all = proj[5:]
            kpe_o.append(jnp.swapaxes(kpet, 1, 2))
            o_mla = _attn(q, k_cat, v_all, mz, bsz, t, s_len, past, tl['attn_q'], tl['attn_k'])
        else:
            kpe128 = proj[5]
            kpe_o.append(kpe128[:, MLA_NOPE_DIM:MLA_NOPE_DIM + MLA_ROPE_DIM].reshape(bsz, t, MLA_ROPE_DIM))
            o_mla = _attn_cached(q, mz, ckv_new, kpe128, ckv_cache, kpet_cache, p['wka'], p['wv'], l, bsz, t,
                                 tl['cached_seqs'])
        ckv_o.append(ckv_new)
        x0r = jnp.zeros((bsz, S5_NSTATE), F32) if s5_re is None else s5_re[l].reshape(bsz, S5_NSTATE)
        x0i = jnp.zeros((bsz, S5_NSTATE), F32) if s5_im is None else s5_im[l].reshape(bsz, S5_NSTATE)
        o_s5, xr, xi = _s5(ps.reshape(bsz, t, PS_COLS), x0r, x0i, p['lre'], p['lim'], p['wb'], p['wc'],
                           p['d'], p['wglu'], p['bglu'], l, bsz, t, tl['s5_rows'])
        re_o.append(xr.reshape(bsz, S5_GROUPS, S5_STATE))
        im_o.append(xi.reshape(bsz, S5_GROUPS, S5_STATE))
        o_s5 = o_s5.reshape(n, S5_WIDTH)
        if l < depth - 1:
            stack_here = l + 1 == depth - 1
            x2, *proj = _out_in(x2, o_gla, o_mla, o_s5, p, cos_t, sin_t, l, bsz, t, tl['row'], with_kv,
                                ckv_prev=tuple(ckv_o) if stack_here else ())
        else:
            x2 = _outproj(x2, o_gla, o_mla, o_s5, p['w_out'], gain_f, l, tl['out_row'], final=True)
    ckv_all = jnp.stack(ckv_o) if ckv_stack is None else ckv_stack
    return (x2.reshape(bsz, t, D_MODEL), jnp.stack(gla_o), ckv_all.reshape(depth, bsz, t, MLA_KV_RANK),
            jnp.stack(kpe_o), jnp.stack(re_o), jnp.stack(im_o))


def kernel(x_prompt, x_sample, state_gla, cache_mla_ckv, cache_mla_kpe, state_s5_re, state_s5_im, ln_gain, w_in, gla_w_gate, gla_b_gate, gla_norm_gain, mla_q_norm_gain, mla_w_uq, mla_kv_norm_gain, mla_w_ukv, s5_lambda_re, s5_lambda_im, s5_b_re, s5_b_im, s5_c_re, s5_c_im, s5_d, s5_log_dt, s5_w_glu, s5_b_glu, w_out, final_gain):
    p = _prepare_params(ln_gain, w_in, gla_w_gate, gla_b_gate, gla_norm_gain, mla_q_norm_gain, mla_w_uq,
                        mla_kv_norm_gain, mla_w_ukv, s5_lambda_re, s5_lambda_im, s5_b_re, s5_b_im,
                        s5_c_re, s5_c_im, s5_d, s5_log_dt, s5_w_glu, s5_b_glu, w_out)
    y_p, gla_p, ckv_p, kpe_p, re_p, im_p = _trunk(x_prompt, None, None, None, None, None, p, final_gain)
    y_s, gla_s, ckv_s, kpe_s, re_s, im_s = _trunk(x_sample, state_gla, cache_mla_ckv, cache_mla_kpe,
                                                  state_s5_re, state_s5_im, p, final_gain)
    return (y_p, y_s, gla_p, ckv_p, kpe_p, re_p, im_p, gla_s, ckv_s, kpe_s, re_s, im_s)
```
